```python
import jax, jax.numpy as jnp
from jax import lax
import numpy as np

D_MODEL = 1024
BATCH = 16
SEQ = 2048
DEPTH = 2

CHUNK = 64
Q_BLOCK = 128
N_BRANCH = 3
ATTN_HEADS = 8
ATTN_HEAD_DIM = 64
ATTN_WIDTH = ATTN_HEADS * ATTN_HEAD_DIM
POOL_WINDOWS = (2, 4, 8, 16)
POOL_GROUPS = 4
POOL_WIDTH = 512
POOL_GROUP_DIM = POOL_WIDTH // POOL_GROUPS
CONV_WIDTH = 512
CONV_K = 3
FFN_HIDDEN = -(-(8 * D_MODEL) // (3 * 256)) * 256
RMS_EPS = 1e-6
NEG_INF = -1e30

IN_SPLITS = (ATTN_WIDTH, ATTN_WIDTH, ATTN_WIDTH, ATTN_HEADS, POOL_WIDTH,
             CONV_WIDTH, CONV_WIDTH, CONV_WIDTH, N_BRANCH * D_MODEL)
IN_COLS = sum(IN_SPLITS)
IN_SPLIT_POINTS = tuple(int(p) for p in np.cumsum(IN_SPLITS)[:-1])

kernel_name = "hybrid_fox_pool_shortconv_gated_block"


def rms_norm(x, g):
    xf = x.astype(jnp.float32)
    y = xf * lax.rsqrt(jnp.mean(xf * xf, axis=-1, keepdims=True) + RMS_EPS)
    return (y * g.astype(jnp.float32)).astype(x.dtype)


def forgetting_attention(q, k, v, f_logit, b_f):
    B, S, H, Dh = q.shape
    log_f = jax.nn.log_sigmoid((f_logit + b_f).astype(jnp.float32))
    F = jnp.cumsum(log_f, axis=1)
    F_k = F.transpose(0, 2, 1)
    nblk = S // Q_BLOCK
    q_blocks = q.reshape(B, nblk, Q_BLOCK, H, Dh).swapaxes(0, 1)
    F_q_blocks = F.reshape(B, nblk, Q_BLOCK, H).swapaxes(0, 1)
    q_pos_blocks = jnp.arange(S, dtype=jnp.int32).reshape(nblk, Q_BLOCK)
    k_pos = jnp.arange(S, dtype=jnp.int32)
    scale = Dh ** -0.5

    def one_block(args):
        q_i, F_q_i, pos_i = args
        logits = jnp.einsum('bqhd,bkhd->bhqk', q_i, k).astype(jnp.float32) * scale
        decay = F_q_i.transpose(0, 2, 1)[..., :, None] - F_k[..., None, :]
        mask = k_pos[None, :] <= pos_i[:, None]
        logits = jnp.where(mask, logits + decay, NEG_INF)
        p = jax.nn.softmax(logits, axis=-1)
        return jnp.einsum('bhqk,bkhd->bqhd', p.astype(v.dtype), v)

    out = lax.map(one_block, (q_blocks, F_q_blocks, q_pos_blocks))
    return out.swapaxes(0, 1).reshape(B, S, H * Dh)


def multiscale_pool(u, pool_w, pool_scale):
    B, S, C = u.shape
    uf = u.astype(jnp.float32)
    c0 = jnp.pad(jnp.cumsum(uf, axis=1), ((0, 0), (1, 0), (0, 0)))
    n_avail = jnp.arange(1, S + 1, dtype=jnp.float32)[:, None]
    outs = []
    for g, w in enumerate(POOL_WINDOWS):
        sl = slice(g * POOL_GROUP_DIM, (g + 1) * POOL_GROUP_DIM)
        cg = c0[..., sl]
        lag = jnp.pad(cg, ((0, 0), (w, 0), (0, 0)))[:, :S + 1]
        mean = (cg - lag)[:, 1:] / jnp.minimum(n_avail, float(w))
        outs.append(mean - uf[..., sl])
    d = jnp.stack(outs, axis=2).astype(u.dtype)
    y = jnp.einsum('bsgc,gcd->bsgd', d, pool_w).reshape(B, S, C)
    return y * pool_scale


def short_gated_conv(v, gate_b, gate_c, conv_w):
    C = v.shape[-1]
    z = gate_c * v
    y = lax.conv_general_dilated(z, conv_w[:, None, :], window_strides=(1,),
                                 padding=[(CONV_K - 1, 0)],
                                 dimension_numbers=('NWC', 'WIO', 'NWC'),
                                 feature_group_count=C)
    return gate_b * y


def _fwd_setup_inputs(seed: int = 0) -> dict:
    key = jax.random.key(seed)
    ks = jax.random.split(key, 18)
    f32 = jnp.float32

    def nrm(k, shape, fan_in):
        return jax.random.normal(k, shape, f32) * (fan_in ** -0.5)

    return {
        "x": jax.random.normal(ks[0], (BATCH, SEQ, D_MODEL), f32),
        "attn_norm": 1.0 + 0.05 * jax.random.normal(ks[1], (DEPTH, D_MODEL), f32),
        "w_in": nrm(ks[2], (DEPTH, D_MODEL, IN_COLS), D_MODEL),
        "b_forget": jax.random.uniform(ks[3], (DEPTH, ATTN_HEADS), f32, 1.0, 4.0),
        "b_gate": 0.1 * jax.random.normal(ks[4], (DEPTH, N_BRANCH * D_MODEL), f32),
        "w_proj_attn": nrm(ks[5], (DEPTH, ATTN_WIDTH, D_MODEL), ATTN_WIDTH),
        "pool_w": nrm(ks[6], (DEPTH, POOL_GROUPS, POOL_GROUP_DIM, POOL_GROUP_DIM), POOL_GROUP_DIM),
        "pool_scale": 1.0 + 0.1 * jax.random.normal(ks[7], (DEPTH, POOL_WIDTH), f32),
        "w_proj_pool": nrm(ks[8], (DEPTH, POOL_WIDTH, D_MODEL), POOL_WIDTH),
        "conv_w": nrm(ks[9], (DEPTH, CONV_K, CONV_WIDTH), CONV_K),
        "w_proj_conv": nrm(ks[10], (DEPTH, CONV_WIDTH, D_MODEL), CONV_WIDTH),
        "w_out": nrm(ks[11], (DEPTH, D_MODEL, D_MODEL), D_MODEL),
        "ffn_norm": 1.0 + 0.05 * jax.random.normal(ks[12], (DEPTH, D_MODEL), f32),
        "w_gate_up": nrm(ks[13], (DEPTH, D_MODEL, 2 * FFN_HIDDEN), D_MODEL),
        "w_down": nrm(ks[14], (DEPTH, FFN_HIDDEN, D_MODEL), FFN_HIDDEN),
        "final_norm": 1.0 + 0.05 * jax.random.normal(ks[15], (D_MODEL,), f32),
    }


def _fwd_reference(x, attn_norm, w_in, b_forget, b_gate, w_proj_attn, pool_w, pool_scale,
              w_proj_pool, conv_w, w_proj_conv, w_out, ffn_norm, w_gate_up, w_down,
              final_norm):
    B, S, D = x.shape
    for l in range(DEPTH):
        h = rms_norm(x, attn_norm[l])
        proj = h @ w_in[l]
        q, k, v, f_logit, u, cv, cb, cc, g = jnp.split(proj, IN_SPLIT_POINTS, axis=-1)
        heads = (B, S, ATTN_HEADS, ATTN_HEAD_DIM)
        y_a = forgetting_attention(q.reshape(heads), k.reshape(heads), v.reshape(heads),
                                   f_logit, b_forget[l]) @ w_proj_attn[l]
        y_b = multiscale_pool(u, pool_w[l], pool_scale[l]) @ w_proj_pool[l]
        y_c = short_gated_conv(cv, cb, cc, conv_w[l]) @ w_proj_conv[l]
        gates = jax.nn.sigmoid(g + b_gate[l]).reshape(B, S, N_BRANCH, D)
        mixed = gates[..., 0, :] * y_a + gates[..., 1, :] * y_b + gates[..., 2, :] * y_c
        x = x + mixed @ w_out[l]
        h = rms_norm(x, ffn_norm[l])
        a, b = jnp.split(h @ w_gate_up[l], 2, axis=-1)
        x = x + (jax.nn.silu(a) * b) @ w_down[l]
    return rms_norm(x, final_norm)


import jax as _jax
import jax.numpy as _jnp

TWIN_FORMAT = 'train_step'
FWD_PARAMS = ['x', 'attn_norm', 'w_in', 'b_forget', 'b_gate', 'w_proj_attn', 'pool_w', 'pool_scale', 'w_proj_pool', 'conv_w', 'w_proj_conv', 'w_out', 'ffn_norm', 'w_gate_up', 'w_down', 'final_norm']
TWIN_WEIGHTS = ['attn_norm', 'w_in', 'b_forget', 'b_gate', 'w_proj_attn', 'pool_w', 'pool_scale', 'w_proj_pool', 'conv_w', 'w_proj_conv', 'w_out', 'ffn_norm', 'w_gate_up', 'w_down', 'final_norm']
TWIN_DIFF_INPUT = 'x'
TWIN_INPUTS = ['x', 'attn_norm', 'w_in', 'b_forget', 'b_gate', 'w_proj_attn', 'pool_w', 'pool_scale', 'w_proj_pool', 'conv_w', 'w_proj_conv', 'w_out', 'ffn_norm', 'w_gate_up', 'w_down', 'final_norm', 'loss_target', 'm_attn_norm', 'm_w_in', 'm_b_forget', 'm_b_gate', 'm_w_proj_attn', 'm_pool_w', 'm_pool_scale', 'm_w_proj_pool', 'm_conv_w', 'm_w_proj_conv', 'm_w_out', 'm_ffn_norm', 'm_w_gate_up', 'm_w_down', 'm_final_norm', 'v_attn_norm', 'v_w_in', 'v_b_forget', 'v_b_gate', 'v_w_proj_attn', 'v_pool_w', 'v_pool_scale', 'v_w_proj_pool', 'v_conv_w', 'v_w_proj_conv', 'v_w_out', 'v_ffn_norm', 'v_w_gate_up', 'v_w_down', 'v_final_norm']
TWIN_OUTPUTS = ['loss', 'grad_x', 'grad_attn_norm', 'grad_w_in', 'grad_b_forget', 'grad_b_gate', 'grad_w_proj_attn', 'grad_pool_w', 'grad_pool_scale', 'grad_w_proj_pool', 'grad_conv_w', 'grad_w_proj_conv', 'grad_w_out', 'grad_ffn_norm', 'grad_w_gate_up', 'grad_w_down', 'grad_final_norm', 'delta_attn_norm', 'delta_w_in', 'delta_b_forget', 'delta_b_gate', 'delta_w_proj_attn', 'delta_pool_w', 'delta_pool_scale', 'delta_w_proj_pool', 'delta_conv_w', 'delta_w_proj_conv', 'delta_w_out', 'delta_ffn_norm', 'delta_w_gate_up', 'delta_w_down', 'delta_final_norm', 'new_m_attn_norm', 'new_m_w_in', 'new_m_b_forget', 'new_m_b_gate', 'new_m_w_proj_attn', 'new_m_pool_w', 'new_m_pool_scale', 'new_m_w_proj_pool', 'new_m_conv_w', 'new_m_w_proj_conv', 'new_m_w_out', 'new_m_ffn_norm', 'new_m_w_gate_up', 'new_m_w_down', 'new_m_final_norm', 'new_v_attn_norm', 'new_v_w_in', 'new_v_b_forget', 'new_v_b_gate', 'new_v_w_proj_attn', 'new_v_pool_w', 'new_v_pool_scale', 'new_v_w_proj_pool', 'new_v_conv_w', 'new_v_w_proj_conv', 'new_v_w_out', 'new_v_ffn_norm', 'new_v_w_gate_up', 'new_v_w_down', 'new_v_final_norm']
TWIN_LEAF_KINDS = {'loss': 'loss', 'grad_x': 'grad_x', 'grad_attn_norm': 'grad_w', 'grad_w_in': 'grad_w', 'grad_b_forget': 'grad_w', 'grad_b_gate': 'grad_w', 'grad_w_proj_attn': 'grad_w', 'grad_pool_w': 'grad_w', 'grad_pool_scale': 'grad_w', 'grad_w_proj_pool': 'grad_w', 'grad_conv_w': 'grad_w', 'grad_w_proj_conv': 'grad_w', 'grad_w_out': 'grad_w', 'grad_ffn_norm': 'grad_w', 'grad_w_gate_up': 'grad_w', 'grad_w_down': 'grad_w', 'grad_final_norm': 'grad_w', 'delta_attn_norm': 'delta_w', 'delta_w_in': 'delta_w', 'delta_b_forget': 'delta_w', 'delta_b_gate': 'delta_w', 'delta_w_proj_attn': 'delta_w', 'delta_pool_w': 'delta_w', 'delta_pool_scale': 'delta_w', 'delta_w_proj_pool': 'delta_w', 'delta_conv_w': 'delta_w', 'delta_w_proj_conv': 'delta_w', 'delta_w_out': 'delta_w', 'delta_ffn_norm': 'delta_w', 'delta_w_gate_up': 'delta_w', 'delta_w_down': 'delta_w', 'delta_final_norm': 'delta_w', 'new_m_attn_norm': 'new_m', 'new_m_w_in': 'new_m', 'new_m_b_forget': 'new_m', 'new_m_b_gate': 'new_m', 'new_m_w_proj_attn': 'new_m', 'new_m_pool_w': 'new_m', 'new_m_pool_scale': 'new_m', 'new_m_w_proj_pool': 'new_m', 'new_m_conv_w': 'new_m', 'new_m_w_proj_conv': 'new_m', 'new_m_w_out': 'new_m', 'new_m_ffn_norm': 'new_m', 'new_m_w_gate_up': 'new_m', 'new_m_w_down': 'new_m', 'new_m_final_norm': 'new_m', 'new_v_attn_norm': 'new_v', 'new_v_w_in': 'new_v', 'new_v_b_forget': 'new_v', 'new_v_b_gate': 'new_v', 'new_v_w_proj_attn': 'new_v', 'new_v_pool_w': 'new_v', 'new_v_pool_scale': 'new_v', 'new_v_w_proj_pool': 'new_v', 'new_v_conv_w': 'new_v', 'new_v_w_proj_conv': 'new_v', 'new_v_w_out': 'new_v', 'new_v_ffn_norm': 'new_v', 'new_v_w_gate_up': 'new_v', 'new_v_w_down': 'new_v', 'new_v_final_norm': 'new_v'}


def _forward(args):
    return _fwd_reference(*[args[k] for k in FWD_PARAMS])


def _output_shape():
    out = _jax.eval_shape(lambda: _forward(_fwd_setup_inputs(0)))
    return out.shape, out.dtype

N_MICROBATCH = 1
ADAM_LR = 0.001
ADAM_B1 = 0.9
ADAM_B2 = 0.999
ADAM_EPS = 1e-08
ADAM_WD = 0.01
ADAM_STEP = 10
PER_EXAMPLE_BATCH_AXIS = {'x': 0, 'loss_target': 0}
SHARED_INPUTS = []
_WEIGHT_DTYPES = {'attn_norm': _jnp.float32, 'w_in': _jnp.float32, 'b_forget': _jnp.float32, 'b_gate': _jnp.float32, 'w_proj_attn': _jnp.float32, 'pool_w': _jnp.float32, 'pool_scale': _jnp.float32, 'w_proj_pool': _jnp.float32, 'conv_w': _jnp.float32, 'w_proj_conv': _jnp.float32, 'w_out': _jnp.float32, 'ffn_norm': _jnp.float32, 'w_gate_up': _jnp.float32, 'w_down': _jnp.float32, 'final_norm': _jnp.float32}
MOMENT_SCALE = {'attn_norm': 1.814104e-01, 'w_in': 7.086911e-02, 'b_forget': 1.686086e-01, 'b_gate': 2.666217e-02, 'w_proj_attn': 3.331999e-02, 'pool_w': 1.072783e-01, 'pool_scale': 1.120231e-01, 'w_proj_pool': 7.645414e-02, 'conv_w': 1.229584e-01, 'w_proj_conv': 8.652017e-02, 'w_out': 1.194430e-01, 'ffn_norm': 1.168939e-01, 'w_gate_up': 4.876311e-02, 'w_down': 7.980238e-02, 'final_norm': 3.196489e+01}


def _to_microbatches(a, axis):
    t = _jnp.moveaxis(a, axis, 0)
    t = t.reshape((N_MICROBATCH, t.shape[0] // N_MICROBATCH) + t.shape[1:])
    return _jnp.moveaxis(t, 1, axis + 1)


def setup_inputs(seed: int = 0) -> dict:
    inp = _fwd_setup_inputs(seed)
    key = _jax.random.fold_in(_jax.random.key(seed), 7919)
    shape, _ = _output_shape()
    out = dict(inp)
    out["loss_target"] = _jax.random.normal(_jax.random.fold_in(key, 0), shape, _jnp.float32)
    for i, name in enumerate(TWIN_WEIGHTS):
        w = inp[name].astype(_jnp.float32)
        if MOMENT_SCALE is None:
            s = _jnp.sqrt(_jnp.mean(_jnp.square(w)) + 1e-30)
        else:
            s = MOMENT_SCALE[name]
        km, kv = _jax.random.split(_jax.random.fold_in(key, i + 1))
        out[name] = w
        out["m_" + name] = s * _jax.random.normal(km, w.shape, _jnp.float32)
        out["v_" + name] = (s * s) * _jax.random.uniform(kv, w.shape, _jnp.float32, 0.5, 1.5)
    if N_MICROBATCH > 1:
        for name, axis in PER_EXAMPLE_BATCH_AXIS.items():
            out[name] = _to_microbatches(out[name], axis)
    return {'x': out['x'], 'attn_norm': out['attn_norm'], 'w_in': out['w_in'], 'b_forget': out['b_forget'], 'b_gate': out['b_gate'], 'w_proj_attn': out['w_proj_attn'], 'pool_w': out['pool_w'], 'pool_scale': out['pool_scale'], 'w_proj_pool': out['w_proj_pool'], 'conv_w': out['conv_w'], 'w_proj_conv': out['w_proj_conv'], 'w_out': out['w_out'], 'ffn_norm': out['ffn_norm'], 'w_gate_up': out['w_gate_up'], 'w_down': out['w_down'], 'final_norm': out['final_norm'], 'loss_target': out['loss_target'], 'm_attn_norm': out['m_attn_norm'], 'm_w_in': out['m_w_in'], 'm_b_forget': out['m_b_forget'], 'm_b_gate': out['m_b_gate'], 'm_w_proj_attn': out['m_w_proj_attn'], 'm_pool_w': out['m_pool_w'], 'm_pool_scale': out['m_pool_scale'], 'm_w_proj_pool': out['m_w_proj_pool'], 'm_conv_w': out['m_conv_w'], 'm_w_proj_conv': out['m_w_proj_conv'], 'm_w_out': out['m_w_out'], 'm_ffn_norm': out['m_ffn_norm'], 'm_w_gate_up': out['m_w_gate_up'], 'm_w_down': out['m_w_down'], 'm_final_norm': out['m_final_norm'], 'v_attn_norm': out['v_attn_norm'], 'v_w_in': out['v_w_in'], 'v_b_forget': out['v_b_forget'], 'v_b_gate': out['v_b_gate'], 'v_w_proj_attn': out['v_w_proj_attn'], 'v_pool_w': out['v_pool_w'], 'v_pool_scale': out['v_pool_scale'], 'v_w_proj_pool': out['v_w_proj_pool'], 'v_conv_w': out['v_conv_w'], 'v_w_proj_conv': out['v_w_proj_conv'], 'v_w_out': out['v_w_out'], 'v_ffn_norm': out['v_ffn_norm'], 'v_w_gate_up': out['v_w_gate_up'], 'v_w_down': out['v_w_down'], 'v_final_norm': out['v_final_norm']}


def _loss(weights, diff, rest, loss_target):
    with _jax.named_scope("forward"):
        args = {**rest, TWIN_DIFF_INPUT: diff, **{k: w.astype(_WEIGHT_DTYPES[k]) for k, w in weights.items()}}
        y = _forward(args)
    with _jax.named_scope("loss_head"):
        err = _jnp.square(y.astype(_jnp.float32) - loss_target)
        return 0.5 * _jnp.sum(_jnp.mean(err, axis=-1)) if err.ndim else 0.5 * err


def _adamw(w, g, m, v):
    m = ADAM_B1 * m + (1.0 - ADAM_B1) * g
    v = ADAM_B2 * v + (1.0 - ADAM_B2) * _jnp.square(g)
    m_hat = m / (1.0 - ADAM_B1 ** ADAM_STEP)
    v_hat = v / (1.0 - ADAM_B2 ** ADAM_STEP)
    delta = -ADAM_LR * (m_hat / (_jnp.sqrt(v_hat) + ADAM_EPS) + ADAM_WD * w)
    return delta, m, v


def reference(x, attn_norm, w_in, b_forget, b_gate, w_proj_attn, pool_w, pool_scale, w_proj_pool, conv_w, w_proj_conv, w_out, ffn_norm, w_gate_up, w_down, final_norm, loss_target, m_attn_norm, m_w_in, m_b_forget, m_b_gate, m_w_proj_attn, m_pool_w, m_pool_scale, m_w_proj_pool, m_conv_w, m_w_proj_conv, m_w_out, m_ffn_norm, m_w_gate_up, m_w_down, m_final_norm, v_attn_norm, v_w_in, v_b_forget, v_b_gate, v_w_proj_attn, v_pool_w, v_pool_scale, v_w_proj_pool, v_conv_w, v_w_proj_conv, v_w_out, v_ffn_norm, v_w_gate_up, v_w_down, v_final_norm):
    given = dict(x=x, attn_norm=attn_norm, w_in=w_in, b_forget=b_forget, b_gate=b_gate, w_proj_attn=w_proj_attn, pool_w=pool_w, pool_scale=pool_scale, w_proj_pool=w_proj_pool, conv_w=conv_w, w_proj_conv=w_proj_conv, w_out=w_out, ffn_norm=ffn_norm, w_gate_up=w_gate_up, w_down=w_down, final_norm=final_norm, loss_target=loss_target, m_attn_norm=m_attn_norm, m_w_in=m_w_in, m_b_forget=m_b_forget, m_b_gate=m_b_gate, m_w_proj_attn=m_w_proj_attn, m_pool_w=m_pool_w, m_pool_scale=m_pool_scale, m_w_proj_pool=m_w_proj_pool, m_conv_w=m_conv_w, m_w_proj_conv=m_w_proj_conv, m_w_out=m_w_out, m_ffn_norm=m_ffn_norm, m_w_gate_up=m_w_gate_up, m_w_down=m_w_down, m_final_norm=m_final_norm, v_attn_norm=v_attn_norm, v_w_in=v_w_in, v_b_forget=v_b_forget, v_b_gate=v_b_gate, v_w_proj_attn=v_w_proj_attn, v_pool_w=v_pool_w, v_pool_scale=v_pool_scale, v_w_proj_pool=v_w_proj_pool, v_conv_w=v_conv_w, v_w_proj_conv=v_w_proj_conv, v_w_out=v_w_out, v_ffn_norm=v_ffn_norm, v_w_gate_up=v_w_gate_up, v_w_down=v_w_down, v_final_norm=v_final_norm)
    weights = {n: given[n] for n in TWIN_WEIGHTS}
    shared = {n: given[n] for n in SHARED_INPUTS}
    per_example = {n: given[n] for n in ['x']}
    grad_fn = _jax.value_and_grad(_loss, argnums=(0, 1))

    def one_microbatch(ex, loss_target):
        ex = dict(ex)
        diff = ex.pop(TWIN_DIFF_INPUT)
        return grad_fn(weights, diff, {**shared, **ex}, loss_target)

    if N_MICROBATCH == 1:
        loss, (grad_w, grad_x) = one_microbatch(per_example, given["loss_target"])
    else:
        def body(carry, xs):
            loss_sum, grad_sum = carry
            l_k, (gw_k, gx_k) = one_microbatch(xs[0], xs[1])
            with _jax.named_scope("update"):
                return (loss_sum + l_k, _jax.tree.map(_jnp.add, grad_sum, gw_k)), gx_k

        init = (_jnp.zeros((), _jnp.float32), _jax.tree.map(_jnp.zeros_like, weights))
        (loss, grad_w), grad_x = _jax.lax.scan(body, init, (per_example, given["loss_target"]))
    with _jax.named_scope("update"):
        delta_w, new_m, new_v = {}, {}, {}
        for n in TWIN_WEIGHTS:
            delta_w[n], new_m[n], new_v[n] = _adamw(weights[n], grad_w[n], given["m_" + n], given["v_" + n])
    return (loss, grad_x, *[grad_w[n] for n in TWIN_WEIGHTS], *[delta_w[n] for n in TWIN_WEIGHTS],
            *[new_m[n] for n in TWIN_WEIGHTS], *[new_v[n] for n in TWIN_WEIGHTS])
```

```python
import numpy as np
import jax
import jax.numpy as jnp
from jax import lax
from jax.experimental import pallas as pl
from jax.experimental.pallas import tpu as pltpu

F32, BF16 = jnp.float32, jnp.bfloat16
SDS = jax.ShapeDtypeStruct
MESH = pl.DeviceIdType.MESH
AXES = ("x", "y", "c")
N_CHIPS = 4
LANES = 128
VMEM_LIMIT = 48 * 1024 * 1024

HEADS, HEAD_DIM = 8, 64
HEAD_PAD = 128
BRANCH_W = 512
GROUP_W = 128
POOL_WINDOWS = (2, 4, 8, 16)
F_PAD = 512
FFN_TILE = 256
ATTN_BLOCK = 256
RMS_EPS = 1e-6
NEG_INF = -1e30
ADAM_LR, ADAM_B1, ADAM_B2, ADAM_EPS, ADAM_WD, ADAM_STEP = 0.001, 0.9, 0.999, 1e-08, 0.01, 10

NT = (((1,), (1,)), ((), ()))
TN = (((0,), (0,)), ((), ()))


def _tile(n, prefs):
    for p in prefs:
        if n % p == 0:
            return p
    raise ValueError(f"no tile of {prefs} divides {n}")


def _params(*sem):
    return pltpu.CompilerParams(dimension_semantics=sem, vmem_limit_bytes=VMEM_LIMIT)


def _sigmoid(z):
    return 1.0 / (1.0 + jnp.exp(-z))


def _split3(x):
    h1 = x.astype(BF16)
    r1 = x - h1.astype(F32)
    h2 = r1.astype(BF16)
    h3 = (r1 - h2.astype(F32)).astype(BF16)
    return h1, h2, h3


def _dot(a, b):
    return jnp.dot(a, b, preferred_element_type=F32)


def norm_matmul(x, gain, w, name):
    T, D = x.shape
    N = w.shape[1]
    tm = _tile(T, (1024, 512, 256, 128))
    tn = _tile(N, (512, 256, 128))

    def body(x_ref, g_ref, w_ref, y_ref, h_ref):
        @pl.when(pl.program_id(1) == 0)
        def _():
            xf = x_ref[...]
            r = lax.rsqrt(jnp.mean(xf * xf, axis=-1, keepdims=True) + RMS_EPS)
            h_ref[...] = ((xf * r) * g_ref[...]).astype(BF16)

        y_ref[...] = _dot(h_ref[...], w_ref[...]).astype(BF16)

    return pl.pallas_call(
        body, name=name, grid=(T // tm, N // tn),
        in_specs=[pl.BlockSpec((tm, D), lambda i, j: (i, 0)),
                  pl.BlockSpec((1, D), lambda i, j: (0, 0)),
                  pl.BlockSpec((D, tn), lambda i, j: (0, j))],
        out_specs=[pl.BlockSpec((tm, tn), lambda i, j: (i, j)),
                   pl.BlockSpec((tm, D), lambda i, j: (i, 0))],
        out_shape=[SDS((T, N), BF16), SDS((T, D), BF16)],
        compiler_params=_params("arbitrary", "arbitrary"),
    )(x, gain, w)


def matmul_nt_normbwd(dy, w, x, gain, dres, name):
    T, N = dy.shape
    D = w.shape[0]
    tm = _tile(T, (512, 256, 128))
    tk = _tile(N, (512, 256, 128))
    nk = N // tk

    def body(dy_ref, w_ref, x_ref, g_ref, dres_ref, dx_ref, dxb_ref, dg_ref, acc_ref):
        i, k = pl.program_id(0), pl.program_id(1)

        @pl.when(k == 0)
        def _():
            acc_ref[...] = jnp.zeros_like(acc_ref)

        acc_ref[...] += lax.dot_general(dy_ref[...], w_ref[...], NT, preferred_element_type=F32)

        @pl.when(k == nk - 1)
        def _():
            xf = x_ref[...]
            r = lax.rsqrt(jnp.mean(xf * xf, axis=-1, keepdims=True) + RMS_EPS)
            xhat = xf * r
            dh = acc_ref[...]
            dhg = dh * g_ref[...]
            dx = dres_ref[...] + r * (dhg - xhat * jnp.mean(dhg * xhat, axis=-1, keepdims=True))
            dx_ref[...] = dx
            dxb_ref[...] = dx.astype(BF16)
            part = jnp.sum(dh * xhat, axis=0, keepdims=True)

            @pl.when(i == 0)
            def _():
                dg_ref[...] = part

            @pl.when(i > 0)
            def _():
                dg_ref[...] += part

    return pl.pallas_call(
        body, name=name, grid=(T // tm, nk),
        in_specs=[pl.BlockSpec((tm, tk), lambda i, k: (i, k)),
                  pl.BlockSpec((D, tk), lambda i, k: (0, k)),
                  pl.BlockSpec((tm, D), lambda i, k: (i, 0)),
                  pl.BlockSpec((1, D), lambda i, k: (0, 0)),
                  pl.BlockSpec((tm, D), lambda i, k: (i, 0))],
        out_specs=[pl.BlockSpec((tm, D), lambda i, k: (i, 0)),
                   pl.BlockSpec((tm, D), lambda i, k: (i, 0)),
                   pl.BlockSpec((1, D), lambda i, k: (0, 0))],
        out_shape=[SDS((T, D), F32), SDS((T, D), BF16), SDS((1, D), F32)],
        scratch_shapes=[pltpu.VMEM((tm, D), F32)],
        compiler_params=_params("arbitrary", "arbitrary"),
    )(dy, w, x, gain, dres)


def matmul_tn(a, b, name, a_cols=None, b_cols=None):
    T = a.shape[0]
    a0, M = a_cols if a_cols else (0, a.shape[1])
    b0, N = b_cols if b_cols else (0, b.shape[1])
    tm = _tile(M, (1024, 512, 256, 128))
    tn = _tile(N, (512, 256, 128))
    tk = _tile(T, (4096, 2048, 1024, 512, 256))
    assert a0 % tm == 0 and b0 % tn == 0
    ai, bj, nk = a0 // tm, b0 // tn, T // tk

    def body(a_ref, b_ref, o_ref, acc_ref):
        k = pl.program_id(2)

        @pl.when(k == 0)
        def _():
            acc_ref[...] = jnp.zeros_like(acc_ref)

        acc_ref[...] += lax.dot_general(a_ref[...], b_ref[...], TN, preferred_element_type=F32)

        @pl.when(k == nk - 1)
        def _():
            o_ref[...] = acc_ref[...]

    return pl.pallas_call(
        body, name=name, grid=(M // tm, N // tn, nk),
        in_specs=[pl.BlockSpec((tk, tm), lambda i, j, k: (k, ai + i)),
                  pl.BlockSpec((tk, tn), lambda i, j, k: (k, bj + j))],
        out_specs=pl.BlockSpec((tm, tn), lambda i, j, k: (i, j)),
        out_shape=SDS((M, N), F32),
        scratch_shapes=[pltpu.VMEM((tm, tn), F32)],
        compiler_params=_params("arbitrary", "arbitrary", "arbitrary"),
    )(a, b)


def ffn_down_fwd(ab, w_down, x1):
    T, D = x1.shape
    F = w_down.shape[0]
    tm = _tile(T, (512, 256, 128))
    tk = FFN_TILE
    nk = F // tk

    def body(ab_ref, w_ref, x_ref, x2_ref, s_ref, acc_ref):
        k = pl.program_id(1)

        @pl.when(k == 0)
        def _():
            acc_ref[...] = x_ref[...]

        a = ab_ref[:, :tk].astype(F32)
        b = ab_ref[:, tk:].astype(F32)
        s = (a * _sigmoid(a) * b).astype(BF16)
        s_ref[...] = s
        acc_ref[...] += _dot(s, w_ref[...])

        @pl.when(k == nk - 1)
        def _():
            x2_ref[...] = acc_ref[...]

    return pl.pallas_call(
        body, name="ffn_down_fwd", grid=(T // tm, nk),
        in_specs=[pl.BlockSpec((tm, 2 * tk), lambda i, k: (i, k)),
                  pl.BlockSpec((tk, D), lambda i, k: (k, 0)),
                  pl.BlockSpec((tm, D), lambda i, k: (i, 0))],
        out_specs=[pl.BlockSpec((tm, D), lambda i, k: (i, 0)),
                   pl.BlockSpec((tm, tk), lambda i, k: (i, k))],
        out_shape=[SDS((T, D), F32), SDS((T, F), BF16)],
        scratch_shapes=[pltpu.VMEM((tm, D), F32)],
        compiler_params=_params("arbitrary", "arbitrary"),
    )(ab, w_down, x1)


def ffn_down_bwd(dx2b, w_down, ab):
    T, D = dx2b.shape
    F = w_down.shape[0]
    tm = _tile(T, (512, 256, 128))
    tn = FFN_TILE

    def body(dx_ref, w_ref, ab_ref, dab_ref):
        ds = lax.dot_general(dx_ref[...], w_ref[...], NT, preferred_element_type=F32)
        a = ab_ref[:, :tn].astype(F32)
        b = ab_ref[:, tn:].astype(F32)
        sg = _sigmoid(a)
        silu = a * sg
        dab_ref[:, :tn] = (ds * b * (sg * (1.0 + a * (1.0 - sg)))).astype(BF16)
        dab_ref[:, tn:] = (ds * silu).astype(BF16)

    return pl.pallas_call(
        body, name="ffn_down_bwd", grid=(T // tm, F // tn),
        in_specs=[pl.BlockSpec((tm, D), lambda i, j: (i, 0)),
                  pl.BlockSpec((tn, D), lambda i, j: (j, 0)),
                  pl.BlockSpec((tm, 2 * tn), lambda i, j: (i, j))],
        out_specs=pl.BlockSpec((tm, 2 * tn), lambda i, j: (i, j)),
        out_shape=SDS((T, 2 * F), BF16),
        compiler_params=_params("arbitrary", "arbitrary"),
    )(dx2b, w_down, ab)


def mix_fwd(ao, po, co, proj, b_gate, wpa, wpp, wpc, w_out, x):
    T, D = x.shape
    tm = _tile(T, (256, 128))

    def body(ao_ref, po_ref, co_ref, g_ref, bg_ref, wpa_ref, wpp_ref, wpc_ref, wo_ref, x_ref,
             x1_ref, ys_ref, mixed_ref):
        mixed = jnp.zeros((tm, D), F32)
        for n, (br, wp) in enumerate(((ao_ref, wpa_ref), (po_ref, wpp_ref), (co_ref, wpc_ref))):
            y = _dot(br[...], wp[...])
            cols = slice(n * D, (n + 1) * D)
            gate = _sigmoid(g_ref[:, cols].astype(F32) + bg_ref[:, cols])
            ys_ref[:, cols] = y.astype(BF16)
            mixed = mixed + gate * y
        mb = mixed.astype(BF16)
        mixed_ref[...] = mb
        x1_ref[...] = x_ref[...] + _dot(mb, wo_ref[...])

    row = lambda w: pl.BlockSpec((tm, w), lambda i: (i, 0))
    full = lambda r, c: pl.BlockSpec((r, c), lambda i: (0, 0))
    return pl.pallas_call(
        body, name="mix_fwd", grid=(T // tm,),
        in_specs=[row(BRANCH_W), row(BRANCH_W), row(BRANCH_W), row(3 * D), full(1, 3 * D),
                  full(BRANCH_W, D), full(BRANCH_W, D), full(BRANCH_W, D), full(D, D), row(D)],
        out_specs=[row(D), row(3 * D), row(D)],
        out_shape=[SDS((T, D), F32), SDS((T, 3 * D), BF16), SDS((T, D), BF16)],
        compiler_params=_params("arbitrary"),
    )(ao, po, co, proj, b_gate, wpa, wpp, wpc, w_out, x)


def mix_bwd(dx1b, w_out, proj, b_gate, ys, wpa, wpp, wpc):
    T, D = dx1b.shape
    tm = _tile(T, (256, 128))

    def body(dx_ref, wo_ref, g_ref, bg_ref, ys_ref, wpa_ref, wpp_ref, wpc_ref,
             dys_ref, dg_ref, dao_ref, dpo_ref, dco_ref, dbg_ref):
        i = pl.program_id(0)
        dmixed = lax.dot_general(dx_ref[...], wo_ref[...], NT, preferred_element_type=F32)
        for n, (wp, dbr) in enumerate(((wpa_ref, dao_ref), (wpp_ref, dpo_ref), (wpc_ref, dco_ref))):
            cols = slice(n * D, (n + 1) * D)
            gate = _sigmoid(g_ref[:, cols].astype(F32) + bg_ref[:, cols])
            dy = (dmixed * gate).astype(BF16)
            dys_ref[:, cols] = dy
            dgp = dmixed * ys_ref[:, cols].astype(F32) * gate * (1.0 - gate)
            dg_ref[:, cols] = dgp.astype(BF16)
            part = jnp.sum(dgp, axis=0, keepdims=True)

            @pl.when(i == 0)
            def _():
                dbg_ref[:, cols] = part

            @pl.when(i > 0)
            def _():
                dbg_ref[:, cols] += part

            dbr[...] = lax.dot_general(dy, wp[...], NT, preferred_element_type=F32).astype(BF16)

    row = lambda w: pl.BlockSpec((tm, w), lambda i: (i, 0))
    full = lambda r, c: pl.BlockSpec((r, c), lambda i: (0, 0))
    return pl.pallas_call(
        body, name="mix_bwd", grid=(T // tm,),
        in_specs=[row(D), full(D, D), row(3 * D), full(1, 3 * D), row(3 * D),
                  full(BRANCH_W, D), full(BRANCH_W, D), full(BRANCH_W, D)],
        out_specs=[row(3 * D), row(3 * D), row(BRANCH_W), row(BRANCH_W), row(BRANCH_W), full(1, 3 * D)],
        out_shape=[SDS((T, 3 * D), BF16), SDS((T, 3 * D), BF16), SDS((T, BRANCH_W), BF16),
                   SDS((T, BRANCH_W), BF16), SDS((T, BRANCH_W), BF16), SDS((1, 3 * D), F32)],
        compiler_params=_params("arbitrary"),
    )(dx1b, w_out, proj, b_gate, ys, wpa, wpp, wpc)


def loss_head(x2, gain, target):
    T, D = x2.shape
    tm = _tile(T, (512, 256, 128))

    def body(x_ref, g_ref, t_ref, loss_ref, dx_ref, dxb_ref, dg_ref):
        i = pl.program_id(0)
        xf = x_ref[...]
        g = g_ref[...]
        r = lax.rsqrt(jnp.mean(xf * xf, axis=-1, keepdims=True) + RMS_EPS)
        xhat = xf * r
        diff = xhat * g - t_ref[...]
        part_loss = 0.5 * jnp.sum(jnp.mean(diff * diff, axis=-1, keepdims=True), axis=0, keepdims=True)
        dy = diff * (1.0 / D)
        dhg = dy * g
        dx = r * (dhg - xhat * jnp.mean(dhg * xhat, axis=-1, keepdims=True))
        dx_ref[...] = dx
        dxb_ref[...] = dx.astype(BF16)
        part_g = jnp.sum(dy * xhat, axis=0, keepdims=True)
        part_l = jnp.broadcast_to(part_loss, (1, LANES))

        @pl.when(i == 0)
        def _():
            dg_ref[...] = part_g
            loss_ref[...] = part_l

        @pl.when(i > 0)
        def _():
            dg_ref[...] += part_g
            loss_ref[...] += part_l

    row = pl.BlockSpec((tm, D), lambda i: (i, 0))
    return pl.pallas_call(
        body, name="loss_head", grid=(T // tm,),
        in_specs=[row, pl.BlockSpec((1, D), lambda i: (0, 0)), row],
        out_specs=[pl.BlockSpec((1, LANES), lambda i: (0, 0)), row, row, pl.BlockSpec((1, D), lambda i: (0, 0))],
        out_shape=[SDS((1, LANES), F32), SDS((T, D), F32), SDS((T, D), BF16), SDS((1, D), F32)],
        compiler_params=_params("arbitrary"),
    )(x2, gain, target)


def _placement_constants():
    w = HEADS * HEAD_PAD
    pq = np.zeros((BRANCH_W, w), np.float32)
    pk = np.zeros((BRANCH_W, w), np.float32)
    pfq = np.zeros((3, LANES, w), np.float32)
    pfk = np.zeros((3, LANES, w), np.float32)
    cq = np.zeros((1, w), np.float32)
    ck = np.zeros((1, w), np.float32)
    eq = np.zeros((w, LANES), np.float32)
    ek = np.zeros((w, LANES), np.float32)
    for h in range(HEADS):
        for d in range(HEAD_DIM):
            pq[h * HEAD_DIM + d, h * HEAD_PAD + d] = HEAD_DIM ** -0.5
            pk[h * HEAD_DIM + d, h * HEAD_PAD + d] = 1.0
        for i in range(3):
            pfq[i, h, h * HEAD_PAD + HEAD_DIM + i] = 1.0
            pfk[i, h, h * HEAD_PAD + HEAD_DIM + 3 + i] = -1.0
            cq[0, h * HEAD_PAD + HEAD_DIM + 3 + i] = 1.0
            ck[0, h * HEAD_PAD + HEAD_DIM + i] = 1.0
        eq[h * HEAD_PAD + HEAD_DIM, h] = 1.0
        ek[h * HEAD_PAD + HEAD_DIM + 3, h] = -1.0
    bf = lambda a: jnp.asarray(a, BF16)
    return dict(pq=bf(pq), pk=bf(pk), pfq=bf(pfq), pfk=bf(pfk), cq=jnp.asarray(cq), ck=jnp.asarray(ck),
                pqt=bf(pq.T.copy()), pkt=bf(pk.T.copy()), eq=bf(eq), ek=bf(ek))


def attn_prep(proj3, bf_row, cst, lay):
    Bl, S, _ = proj3.shape
    ts = ATTN_BLOCK
    w = HEADS * HEAD_PAD

    def body(q_ref, k_ref, f_ref, bf_ref, pq_ref, pk_ref, pfq_ref, pfk_ref, cq_ref, ck_ref,
             qa_ref, ka_ref, carry_ref):
        @pl.when(pl.program_id(1) == 0)
        def _():
            carry_ref[...] = jnp.zeros_like(carry_ref)

        z = f_ref[...].astype(F32) + bf_ref[...]
        logf = jnp.minimum(z, 0.0) - jnp.log(1.0 + jnp.exp(-jnp.abs(z)))
        r = lax.broadcasted_iota(jnp.int32, (ts, ts), 0)
        c = lax.broadcasted_iota(jnp.int32, (ts, ts), 1)
        tri = jnp.where(r >= c, 1.0, 0.0).astype(BF16)
        fcum = carry_ref[...]
        for part in _split3(logf):
            fcum = fcum + _dot(tri, part)
        carry_ref[...] = fcum[ts - 1:ts, :]
        qa = _dot(q_ref[...], pq_ref[...]) + cq_ref[...]
        ka = _dot(k_ref[...], pk_ref[...]) + ck_ref[...]
        for i, part in enumerate(_split3(fcum)):
            qa = qa + _dot(part, pfq_ref[i])
            ka = ka + _dot(part, pfk_ref[i])
        qa_ref[...] = qa.astype(BF16)
        ka_ref[...] = ka.astype(BF16)

    cfull = lambda shape: pl.BlockSpec(shape, lambda b, s: (0,) * len(shape))
    return pl.pallas_call(
        body, name="attn_prep", grid=(Bl, S // ts),
        in_specs=[pl.BlockSpec((None, ts, BRANCH_W), lambda b, s: (b, s, lay["q"] // BRANCH_W)),
                  pl.BlockSpec((None, ts, BRANCH_W), lambda b, s: (b, s, lay["k"] // BRANCH_W)),
                  pl.BlockSpec((None, ts, LANES), lambda b, s: (b, s, lay["f"] // LANES)),
                  cfull((1, LANES)), cfull((BRANCH_W, w)), cfull((BRANCH_W, w)),
                  cfull((3, LANES, w)), cfull((3, LANES, w)), cfull((1, w)), cfull((1, w))],
        out_specs=[pl.BlockSpec((None, ts, w), lambda b, s: (b, s, 0)),
                   pl.BlockSpec((None, ts, w), lambda b, s: (b, s, 0))],
        out_shape=[SDS((Bl, S, w), BF16), SDS((Bl, S, w), BF16)],
        scratch_shapes=[pltpu.VMEM((1, LANES), F32)],
        compiler_params=_params("arbitrary", "arbitrary"),
    )(proj3, proj3, proj3, bf_row, cst["pq"], cst["pk"], cst["pfq"], cst["pfk"], cst["cq"], cst["ck"])


def attn_fwd(qa, ka, proj3, lay):
    Bl, S, _ = qa.shape
    tq = tk = ATTN_BLOCK
    nq = S // tq
    pairs = HEADS // 2
    pw = 2 * HEAD_PAD

    def body(qa_ref, ka_ref, v_ref, o_ref, lse_ref):
        i = pl.program_id(2)
        row = lax.broadcasted_iota(jnp.int32, (tk, tq), 0)
        col = lax.broadcasted_iota(jnp.int32, (tk, tq), 1)
        o_t = []
        for h in range(2):
            hs = slice(h * HEAD_PAD, (h + 1) * HEAD_PAD)
            q = qa_ref[:, hs]

            def step(j, carry, masked, hs=hs, q=q):
                m, l, acc = carry
                off = pl.multiple_of(j * tk, tk)
                k = ka_ref[pl.ds(off, tk), hs]
                v = v_ref[pl.ds(off, tk), :]
                st = lax.dot_general(k, q, NT, preferred_element_type=F32)
                if masked:
                    st = jnp.where(row <= col, st, NEG_INF)
                m_new = jnp.maximum(m, jnp.max(st, axis=0, keepdims=True))
                alpha = jnp.exp(m - m_new)
                p = jnp.exp(st - m_new)
                l = alpha * l + jnp.sum(p, axis=0, keepdims=True)
                acc = alpha * acc + lax.dot_general(v, p.astype(BF16), TN, preferred_element_type=F32)
                return m_new, l, acc

            init = (jnp.full((1, tq), NEG_INF, F32), jnp.zeros((1, tq), F32), jnp.zeros((2 * HEAD_DIM, tq), F32))
            carry = lax.fori_loop(0, i, lambda j, c, step=step: step(j, c, False), init)
            m, l, acc = step(i, carry, True)
            o_t.append(acc[h * HEAD_DIM:(h + 1) * HEAD_DIM, :] / l)
            lse_ref[h:h + 1, :] = m + jnp.log(l)
        o_ref[...] = jnp.concatenate(o_t, axis=0).T.astype(BF16)

    return pl.pallas_call(
        body, name="attn_fwd", grid=(Bl, pairs, nq),
        in_specs=[pl.BlockSpec((None, tq, pw), lambda b, p, i: (b, i, p)),
                  pl.BlockSpec((None, S, pw), lambda b, p, i: (b, 0, p)),
                  pl.BlockSpec((None, S, 2 * HEAD_DIM), lambda b, p, i: (b, 0, lay["v"] // (2 * HEAD_DIM) + p))],
        out_specs=[pl.BlockSpec((None, tq, 2 * HEAD_DIM), lambda b, p, i: (b, i, p)),
                   pl.BlockSpec((None, None, None, 2, tq), lambda b, p, i: (b, p, i, 0, 0))],
        out_shape=[SDS((Bl, S, BRANCH_W), BF16), SDS((Bl, pairs, nq, 2, tq), F32)],
        compiler_params=_params("arbitrary", "arbitrary", "arbitrary"),
    )(qa, ka, proj3)


def attn_bwd(qa, ka, proj3, dao, ao, lse, lay):
    Bl, S, _ = qa.shape
    tq = tk = ATTN_BLOCK
    nq = S // tq
    pairs = HEADS // 2
    pw = 2 * HEAD_PAD
    vw = 2 * HEAD_DIM

    def body(qa_ref, ka_ref, v_ref, do_ref, o_ref, lse_ref, dqa_ref, dka_ref, dv_ref, delta_ref):
        j = pl.program_id(2)
        row = lax.broadcasted_iota(jnp.int32, (tk, tq), 0)
        col = lax.broadcasted_iota(jnp.int32, (tk, tq), 1)
        lane8 = lax.broadcasted_iota(jnp.int32, (8, vw), 1)
        lane_k = lax.broadcasted_iota(jnp.int32, (tk, vw), 1)

        @pl.when(j == 0)
        def _():
            dqa_ref[...] = jnp.zeros_like(dqa_ref)
            for i in range(nq):
                rows = slice(i * tq, (i + 1) * tq)
                doo = do_ref[rows, :].astype(F32) * o_ref[rows, :].astype(F32)
                hi = doo.astype(BF16)
                lo = (doo - hi.astype(F32)).astype(BF16)
                for h in range(2):
                    sel = jnp.where((lane8 >= h * HEAD_DIM) & (lane8 < (h + 1) * HEAD_DIM), 1.0, 0.0).astype(BF16)
                    delta_ref[i, h] = (lax.dot_general(sel, hi, NT, preferred_element_type=F32)
                                       + lax.dot_general(sel, lo, NT, preferred_element_type=F32))

        dvs = []
        for h in range(2):
            hs = slice(h * HEAD_PAD, (h + 1) * HEAD_PAD)
            k = ka_ref[:, hs]
            in_head = (lane_k >= h * HEAD_DIM) & (lane_k < (h + 1) * HEAD_DIM)
            v = jnp.where(in_head, v_ref[...], jnp.zeros_like(v_ref[...]))

            def step(i, carry, masked, hs=hs, k=k, v=v, h=h):
                dk, dv = carry
                off = pl.multiple_of(i * tq, tq)
                q = qa_ref[pl.ds(off, tq), hs]
                do = do_ref[pl.ds(off, tq), :]
                lse_row = lse_ref[i, h:h + 1, :]
                delta_row = delta_ref[i, h, 0:1, :]
                st = lax.dot_general(k, q, NT, preferred_element_type=F32)
                p = jnp.exp(st - lse_row)
                if masked:
                    p = jnp.where(row <= col, p, 0.0)
                pb = p.astype(BF16)
                dv = dv + _dot(pb, do)
                dpt = lax.dot_general(v, do, NT, preferred_element_type=F32)
                ds = (p * (dpt - delta_row)).astype(BF16)
                dk = dk + _dot(ds, q)
                dqa_ref[pl.ds(off, tq), hs] += lax.dot_general(ds, k, TN, preferred_element_type=F32)
                return dk, dv

            carry = step(j, (jnp.zeros((tk, HEAD_PAD), F32), jnp.zeros((tk, vw), F32)), True)
            dk, dv = lax.fori_loop(j + 1, nq, lambda i, c, step=step: step(i, c, False), carry)
            dka_ref[:, hs] = dk
            dvs.append(dv)
        dv_ref[...] = jnp.where(lane_k < HEAD_DIM, dvs[0], dvs[1]).astype(BF16)

    return pl.pallas_call(
        body, name="attn_bwd", grid=(Bl, pairs, nq),
        in_specs=[pl.BlockSpec((None, S, pw), lambda b, p, j: (b, 0, p)),
                  pl.BlockSpec((None, tk, pw), lambda b, p, j: (b, j, p)),
                  pl.BlockSpec((None, tk, vw), lambda b, p, j: (b, j, lay["v"] // vw + p)),
                  pl.BlockSpec((None, S, vw), lambda b, p, j: (b, 0, p)),
                  pl.BlockSpec((None, S, vw), lambda b, p, j: (b, 0, p)),
                  pl.BlockSpec((None, None, nq, 2, tq), lambda b, p, j: (b, p, 0, 0, 0))],
        out_specs=[pl.BlockSpec((None, S, pw), lambda b, p, j: (b, 0, p)),
                   pl.BlockSpec((None, tk, pw), lambda b, p, j: (b, j, p)),
                   pl.BlockSpec((None, tk, vw), lambda b, p, j: (b, j, p))],
        out_shape=[SDS((Bl, S, HEADS * HEAD_PAD), F32), SDS((Bl, S, HEADS * HEAD_PAD), F32),
                   SDS((Bl, S, BRANCH_W), BF16)],
        scratch_shapes=[pltpu.VMEM((nq, 2, 8, tq), F32)],
        compiler_params=_params("arbitrary", "arbitrary", "arbitrary"),
    )(qa, ka, proj3, dao, ao, lse)


def attn_post(dqa, dka, proj3, bf_row, cst, lay):
    Bl, S, w = dqa.shape
    ts = ATTN_BLOCK
    ns = S // ts

    def body(dqa_ref, dka_ref, f_ref, bf_ref, pqt_ref, pkt_ref, eq_ref, ek_ref,
             dq_ref, dk_ref, df_ref, dbf_ref, carry_ref):
        b, s = pl.program_id(0), pl.program_id(1)

        @pl.when(s == 0)
        def _():
            carry_ref[...] = jnp.zeros_like(carry_ref)

        dqa_v, dka_v = dqa_ref[...], dka_ref[...]
        qh = dqa_v.astype(BF16)
        kh = dka_v.astype(BF16)
        dq_ref[...] = _dot(qh, pqt_ref[...]).astype(BF16)
        dk_ref[...] = _dot(kh, pkt_ref[...]).astype(BF16)
        ql = (dqa_v - qh.astype(F32)).astype(BF16)
        kl = (dka_v - kh.astype(F32)).astype(BF16)
        d_f = (_dot(qh, eq_ref[...]) + _dot(ql, eq_ref[...])) + (_dot(kh, ek_ref[...]) + _dot(kl, ek_ref[...]))
        r = lax.broadcasted_iota(jnp.int32, (ts, ts), 0)
        c = lax.broadcasted_iota(jnp.int32, (ts, ts), 1)
        triu = jnp.where(c >= r, 1.0, 0.0).astype(BF16)
        rev = carry_ref[...]
        for part in _split3(d_f):
            rev = rev + _dot(triu, part)
        carry_ref[...] = rev[0:1, :]
        z = f_ref[...].astype(F32) + bf_ref[...]
        lane = lax.broadcasted_iota(jnp.int32, (ts, LANES), 1)
        dfl = jnp.where(lane < HEADS, rev / (1.0 + jnp.exp(z)), 0.0)
        df_ref[...] = jnp.concatenate([dfl.astype(BF16), jnp.zeros((ts, F_PAD - LANES), BF16)], axis=1)
        part = jnp.sum(dfl, axis=0, keepdims=True)

        @pl.when((b == 0) & (s == 0))
        def _():
            dbf_ref[...] = part

        @pl.when((b > 0) | (s > 0))
        def _():
            dbf_ref[...] += part

    cfull = lambda shape: pl.BlockSpec(shape, lambda b, s: (0,) * len(shape))
    rev_blk = lambda wd, c0=0: pl.BlockSpec((None, ts, wd), lambda b, s: (b, ns - 1 - s, c0))
    return pl.pallas_call(
        body, name="attn_post", grid=(Bl, ns),
        in_specs=[rev_blk(w), rev_blk(w), rev_blk(LANES, lay["f"] // LANES), cfull((1, LANES)),
                  cfull((w, BRANCH_W)), cfull((w, BRANCH_W)), cfull((w, LANES)), cfull((w, LANES))],
        out_specs=[rev_blk(BRANCH_W), rev_blk(BRANCH_W), rev_blk(F_PAD), cfull((1, LANES))],
        out_shape=[SDS((Bl, S, BRANCH_W), BF16), SDS((Bl, S, BRANCH_W), BF16), SDS((Bl, S, F_PAD), BF16),
                   SDS((1, LANES), F32)],
        scratch_shapes=[pltpu.VMEM((1, LANES), F32)],
        compiler_params=_params("arbitrary", "arbitrary"),
    )(dqa, dka, proj3, bf_row, cst["pqt"], cst["pkt"], cst["eq"], cst["ek"])


def _shift_down(x, k, row):
    return jnp.where(row >= k, pltpu.roll(x, k, axis=0), 0.0)


def _shift_up(x, k, row):
    n = x.shape[0]
    return jnp.where(row < n - k, pltpu.roll(x, n - k, axis=0), 0.0)


def _window_sum(x, g, row, shift):
    s2 = x + shift(x, 1, row)
    s4 = s2 + shift(s2, 2, row)
    s8 = s4 + shift(s4, 4, row)
    s16 = s8 + shift(s8, 8, row)
    return jnp.where(g == 0, s2, jnp.where(g == 1, s4, jnp.where(g == 2, s8, s16)))


def _window_count(g, row):
    wnd = jnp.where(g == 0, 2, jnp.where(g == 1, 4, jnp.where(g == 2, 8, 16)))
    return jnp.minimum(row + 1, wnd).astype(F32)


def _branch_specs(lay, S, order):
    def spec(name):
        c0 = lay[name] // GROUP_W
        if order == "bg":
            return pl.BlockSpec((None, S, GROUP_W), lambda b, g: (b, 0, c0 + g))
        return pl.BlockSpec((None, S, GROUP_W), lambda g, b: (b, 0, c0 + g))
    return [spec(n) for n in ("u", "cv", "cb", "cc")]


def poolconv_fwd(proj3, pool_w, pool_scale, conv_w, lay):
    Bl, S, _ = proj3.shape

    def body(u_ref, cv_ref, cb_ref, cc_ref, pw_ref, ps_ref, cw_ref, po_ref, co_ref):
        g = pl.program_id(1)
        row = lax.broadcasted_iota(jnp.int32, (S, GROUP_W), 0)
        u = u_ref[...].astype(F32)
        d = _window_sum(u, g, row, _shift_down) / _window_count(g, row) - u
        po_ref[...] = (_dot(d.astype(BF16), pw_ref[...]) * ps_ref[...]).astype(BF16)
        z = cc_ref[...].astype(F32) * cv_ref[...].astype(F32)
        y = cw_ref[0:1, :] * _shift_down(z, 2, row) + cw_ref[1:2, :] * _shift_down(z, 1, row) + cw_ref[2:3, :] * z
        co_ref[...] = (cb_ref[...].astype(F32) * y).astype(BF16)

    out = pl.BlockSpec((None, S, GROUP_W), lambda b, g: (b, 0, g))
    return pl.pallas_call(
        body, name="poolconv_fwd", grid=(Bl, BRANCH_W // GROUP_W),
        in_specs=_branch_specs(lay, S, "bg") + [
            pl.BlockSpec((None, GROUP_W, GROUP_W), lambda b, g: (g, 0, 0)),
            pl.BlockSpec((1, GROUP_W), lambda b, g: (0, g)),
            pl.BlockSpec((3, GROUP_W), lambda b, g: (0, g))],
        out_specs=[out, out],
        out_shape=[SDS((Bl, S, BRANCH_W), BF16), SDS((Bl, S, BRANCH_W), BF16)],
        compiler_params=_params("arbitrary", "arbitrary"),
    )(proj3, proj3, proj3, proj3, pool_w, pool_scale, conv_w)


def poolconv_bwd(proj3, dpo, dco, pool_w, pool_scale, conv_w, lay):
    Bl, S, _ = proj3.shape

    def body(u_ref, cv_ref, cb_ref, cc_ref, dpo_ref, dco_ref, pw_ref, ps_ref, cw_ref,
             du_ref, dcv_ref, dcb_ref, dcc_ref, dpw_ref, dps_ref, dcw_ref):
        g, b = pl.program_id(0), pl.program_id(1)
        row = lax.broadcasted_iota(jnp.int32, (S, GROUP_W), 0)
        cnt = _window_count(g, row)
        u = u_ref[...].astype(F32)
        d = (_window_sum(u, g, row, _shift_down) / cnt - u).astype(BF16)
        pw = pw_ref[...]
        ypre = _dot(d, pw)
        dpo_v = dpo_ref[...].astype(F32)
        dps = jnp.sum(dpo_v * ypre, axis=0, keepdims=True)
        dyp = (dpo_v * ps_ref[...]).astype(BF16)
        dpw = lax.dot_general(d, dyp, TN, preferred_element_type=F32)
        dd = lax.dot_general(dyp, pw, NT, preferred_element_type=F32)
        du_ref[...] = (_window_sum(dd / cnt, g, row, _shift_up) - dd).astype(BF16)

        cv, cb, cc = cv_ref[...].astype(F32), cb_ref[...].astype(F32), cc_ref[...].astype(F32)
        z = cc * cv
        z1, z2 = _shift_down(z, 1, row), _shift_down(z, 2, row)
        w0, w1, w2 = cw_ref[0:1, :], cw_ref[1:2, :], cw_ref[2:3, :]
        y = w0 * z2 + w1 * z1 + w2 * z
        dco_v = dco_ref[...].astype(F32)
        dcb_ref[...] = (dco_v * y).astype(BF16)
        dy = dco_v * cb
        dz = w0 * _shift_up(dy, 2, row) + w1 * _shift_up(dy, 1, row) + w2 * dy
        dcc_ref[...] = (dz * cv).astype(BF16)
        dcv_ref[...] = (dz * cc).astype(BF16)
        dcw = jnp.concatenate([jnp.sum(dy * z2, axis=0, keepdims=True),
                               jnp.sum(dy * z1, axis=0, keepdims=True),
                               jnp.sum(dy * z, axis=0, keepdims=True)], axis=0)

        @pl.when(b == 0)
        def _():
            dpw_ref[...] = dpw
            dps_ref[...] = dps
            dcw_ref[...] = dcw

        @pl.when(b > 0)
        def _():
            dpw_ref[...] += dpw
            dps_ref[...] += dps
            dcw_ref[...] += dcw

    blk = pl.BlockSpec((None, S, GROUP_W), lambda g, b: (b, 0, g))
    wspecs = [pl.BlockSpec((None, GROUP_W, GROUP_W), lambda g, b: (g, 0, 0)),
              pl.BlockSpec((1, GROUP_W), lambda g, b: (0, g)),
              pl.BlockSpec((3, GROUP_W), lambda g, b: (0, g))]
    act = SDS((Bl, S, BRANCH_W), BF16)
    return pl.pallas_call(
        body, name="poolconv_bwd", grid=(BRANCH_W // GROUP_W, Bl),
        in_specs=_branch_specs(lay, S, "gb") + [blk, blk] + wspecs,
        out_specs=[blk, blk, blk, blk] + wspecs,
        out_shape=[act, act, act, act, SDS(pool_w.shape, F32), SDS(pool_scale.shape, F32), SDS(conv_w.shape, F32)],
        compiler_params=_params("arbitrary", "arbitrary"),
    )(proj3, proj3, proj3, proj3, dpo, dco, pool_w, pool_scale, conv_w)


def _row_tile(rows, cols, n_arrays):
    if rows % 8:
        return rows
    lanes = -(-cols // LANES) * LANES
    for t in (2048, 1024, 512, 256, 128, 64, 32, 16, 8):
        if rows % t == 0 and 2 * n_arrays * t * lanes * 4 <= VMEM_LIMIT // 2:
            return t
    return rows


def add_pair(a, b):
    R = a.shape[0]
    tr = _row_tile(R, LANES, 3)

    def body(a_ref, b_ref, o_ref):
        o_ref[...] = (a_ref[...].astype(F32) + b_ref[...].astype(F32)).astype(BF16)

    blk = pl.BlockSpec((tr, LANES), lambda i: (i, 0))
    return pl.pallas_call(body, name="add_pair", grid=(R // tr,), in_specs=[blk, blk], out_specs=blk,
                          out_shape=SDS((R, LANES), BF16), compiler_params=_params("arbitrary"))(a, b)


def add_chips(parts):
    R = parts.shape[1]
    tr = _row_tile(R, LANES, 5)

    def body(p_ref, o_ref):
        acc = p_ref[0].astype(F32)
        for j in range(1, N_CHIPS):
            acc = acc + p_ref[j].astype(F32)
        o_ref[...] = acc

    return pl.pallas_call(body, name="add_chips", grid=(R // tr,),
                          in_specs=[pl.BlockSpec((N_CHIPS, tr, LANES), lambda i: (0, i, 0))],
                          out_specs=pl.BlockSpec((tr, LANES), lambda i: (i, 0)),
                          out_shape=SDS((R, LANES), F32), compiler_params=_params("arbitrary"))(parts)


def adamw(w, g, m, v, name):
    R, C = w.shape
    tr = _row_tile(R, C, 7)

    def body(w_ref, g_ref, m_ref, v_ref, d_ref, nm_ref, nv_ref):
        gv = g_ref[...]
        m_new = ADAM_B1 * m_ref[...] + (1.0 - ADAM_B1) * gv
        v_new = ADAM_B2 * v_ref[...] + (1.0 - ADAM_B2) * (gv * gv)
        m_hat = m_new / (1.0 - ADAM_B1 ** ADAM_STEP)
        v_hat = v_new / (1.0 - ADAM_B2 ** ADAM_STEP)
        d_ref[...] = -ADAM_LR * (m_hat / (jnp.sqrt(v_hat) + ADAM_EPS) + ADAM_WD * w_ref[...])
        nm_ref[...] = m_new
        nv_ref[...] = v_new

    blk = pl.BlockSpec((tr, C), lambda i: (i, 0))
    out = SDS((R, C), F32)
    return pl.pallas_call(body, name=name, grid=(R // tr,), in_specs=[blk] * 4, out_specs=[blk] * 3,
                          out_shape=[out, out, out], compiler_params=_params("arbitrary"))(w, g, m, v)


def _position():
    return lax.axis_index("x"), lax.axis_index("y"), lax.axis_index("c")


def _other_chips(x, y):
    return [(1 - x, y), (x, 1 - y), (1 - x, 1 - y)]


_ANY = pl.BlockSpec(memory_space=pl.ANY)


def allgather_chips(buf, name):
    def body(src_ref, out_ref, send_sems, recv_sems, local_sem):
        x, y, c = _position()
        me = 2 * x + y
        mine = pltpu.make_async_copy(src_ref, out_ref.at[me], local_sem)
        mine.start()
        sends = []
        for k, (px, py) in enumerate(_other_chips(x, y)):
            cp = pltpu.make_async_remote_copy(src_ref=src_ref, dst_ref=out_ref.at[me], send_sem=send_sems.at[k],
                                              recv_sem=recv_sems.at[k], device_id=(px, py, c), device_id_type=MESH)
            cp.start()
            sends.append(cp)
        for k, (px, py) in enumerate(_other_chips(x, y)):
            pltpu.make_async_remote_copy(src_ref=src_ref, dst_ref=out_ref.at[2 * px + py], send_sem=send_sems.at[k],
                                         recv_sem=recv_sems.at[k], device_id=(px, py, c),
                                         device_id_type=MESH).wait_recv()
        for cp in sends:
            cp.wait_send()
        mine.wait()

    return pl.pallas_call(
        body, name=name, in_specs=[_ANY], out_specs=_ANY,
        out_shape=SDS((N_CHIPS,) + buf.shape, buf.dtype),
        scratch_shapes=[pltpu.SemaphoreType.DMA((3,)), pltpu.SemaphoreType.DMA((3,)), pltpu.SemaphoreType.DMA],
        compiler_params=pltpu.CompilerParams(has_side_effects=True),
    )(buf)


def exchange_chips(parts, name):
    def body(src_ref, out_ref, send_sems, recv_sems, local_sem):
        x, y, c = _position()
        me = 2 * x + y
        mine = pltpu.make_async_copy(src_ref.at[me], out_ref.at[me], local_sem)
        mine.start()
        sends = []
        for k, (px, py) in enumerate(_other_chips(x, y)):
            cp = pltpu.make_async_remote_copy(src_ref=src_ref.at[2 * px + py], dst_ref=out_ref.at[me],
                                              send_sem=send_sems.at[k], recv_sem=recv_sems.at[k],
                                              device_id=(px, py, c), device_id_type=MESH)
            cp.start()
            sends.append(cp)
        for k, (px, py) in enumerate(_other_chips(x, y)):
            pltpu.make_async_remote_copy(src_ref=src_ref.at[me], dst_ref=out_ref.at[2 * px + py],
                                         send_sem=send_sems.at[k], recv_sem=recv_sems.at[k],
                                         device_id=(px, py, c), device_id_type=MESH).wait_recv()
        for cp in sends:
            cp.wait_send()
        mine.wait()

    return pl.pallas_call(
        body, name=name, in_specs=[_ANY], out_specs=_ANY, out_shape=SDS(parts.shape, parts.dtype),
        scratch_shapes=[pltpu.SemaphoreType.DMA((3,)), pltpu.SemaphoreType.DMA((3,)), pltpu.SemaphoreType.DMA],
        compiler_params=pltpu.CompilerParams(has_side_effects=True),
    )(parts)


def swap_sibling(buf, name):
    def body(src_ref, out_ref, send_sem, recv_sem):
        x, y, c = _position()
        cp = pltpu.make_async_remote_copy(src_ref=src_ref, dst_ref=out_ref, send_sem=send_sem, recv_sem=recv_sem,
                                          device_id=(x, y, 1 - c), device_id_type=MESH)
        cp.start()
        cp.wait()

    return pl.pallas_call(
        body, name=name, in_specs=[_ANY], out_specs=_ANY, out_shape=SDS(buf.shape, buf.dtype),
        scratch_shapes=[pltpu.SemaphoreType.DMA, pltpu.SemaphoreType.DMA],
        compiler_params=pltpu.CompilerParams(has_side_effects=True),
    )(buf)


def join_halves(half, name):
    def body(src_ref, out_ref, send_sem, recv_sem, local_sem):
        x, y, c = _position()
        mine = pltpu.make_async_copy(src_ref, out_ref.at[c], local_sem)
        mine.start()
        cp = pltpu.make_async_remote_copy(src_ref=src_ref, dst_ref=out_ref.at[c], send_sem=send_sem,
                                          recv_sem=recv_sem, device_id=(x, y, 1 - c), device_id_type=MESH)
        cp.start()
        pltpu.make_async_remote_copy(src_ref=src_ref, dst_ref=out_ref.at[1 - c], send_sem=send_sem,
                                     recv_sem=recv_sem, device_id=(x, y, 1 - c), device_id_type=MESH).wait_recv()
        cp.wait_send()
        mine.wait()

    return pl.pallas_call(
        body, name=name, in_specs=[_ANY], out_specs=_ANY, out_shape=SDS((2,) + half.shape, half.dtype),
        scratch_shapes=[pltpu.SemaphoreType.DMA, pltpu.SemaphoreType.DMA, pltpu.SemaphoreType.DMA],
        compiler_params=pltpu.CompilerParams(has_side_effects=True),
    )(half)


SHARDED = ("w_in", "w_proj_attn", "w_proj_pool", "conv_w", "w_proj_conv", "w_out", "w_gate_up", "w_down")
ROW_SHARDED = ("w_out", "w_down")
REPLICATED = ("attn_norm", "b_forget", "b_gate", "pool_w", "pool_scale", "ffn_norm", "final_norm")


def _proj_layout(D):
    names = ("q", "k", "v", "u", "cv", "cb", "cc", "f")
    lay = {"g": 0}
    for n, name in enumerate(names):
        lay[name] = 3 * D + n * BRANCH_W
    lay["width"] = lay["f"] + F_PAD
    return lay


def _pack_w_in(w):
    qkv, f, rest, g = w[..., :1536], w[..., 1536:1544], w[..., 1544:3592], w[..., 3592:]
    pad = jnp.zeros(w.shape[:-1] + (F_PAD - HEADS,), w.dtype)
    return jnp.concatenate([g, qkv, rest, f, pad], axis=-1)


def _unpack_w_in(p, D):
    lay = _proj_layout(D)
    return jnp.concatenate([p[..., lay["q"]:lay["u"]], p[..., lay["f"]:lay["f"] + HEADS],
                            p[..., lay["u"]:lay["f"]], p[..., :3 * D]], axis=-1)


def _interleave_gu(w):
    lead, n = w.shape[:-1], w.shape[-1]
    f = n // 2
    return w.reshape(lead + (2, f // FFN_TILE, FFN_TILE)).swapaxes(-3, -2).reshape(lead + (n,))


def _deinterleave_gu(w):
    lead, n = w.shape[:-1], w.shape[-1]
    f = n // 2
    return w.reshape(lead + (f // FFN_TILE, 2, FFN_TILE)).swapaxes(-3, -2).reshape(lead + (n,))


def _to_rows(vec, multiple):
    n = vec.shape[0]
    padded = -(-n // multiple) * multiple
    return jnp.pad(vec, (0, padded - n)).reshape(padded // LANES, LANES)


def _split_flat(vec, shapes):
    out, at = [], 0
    for shp in shapes:
        n = int(np.prod(shp))
        out.append(vec[at:at + n].reshape(shp))
        at += n
    return out


def _shards_of(full, name):
    L, R, C = full.shape
    if name in ROW_SHARDED:
        return full.reshape(L, N_CHIPS, R // N_CHIPS, C).transpose(1, 0, 2, 3)
    return full.reshape(L, R, N_CHIPS, C // N_CHIPS).transpose(2, 0, 1, 3)


def _full_of(shards, name):
    _, L, r, c = shards.shape
    if name in ROW_SHARDED:
        return shards.transpose(1, 0, 2, 3).reshape(L, N_CHIPS * r, c)
    return shards.transpose(1, 2, 0, 3).reshape(L, r, N_CHIPS * c)


def kernel(x, attn_norm, w_in, b_forget, b_gate, w_proj_attn, pool_w, pool_scale, w_proj_pool, conv_w, w_proj_conv, w_out, ffn_norm, w_gate_up, w_down, final_norm, loss_target, m_attn_norm, m_w_in, m_b_forget, m_b_gate, m_w_proj_attn, m_pool_w, m_pool_scale, m_w_proj_pool, m_conv_w, m_w_proj_conv, m_w_out, m_ffn_norm, m_w_gate_up, m_w_down, m_final_norm, v_attn_norm, v_w_in, v_b_forget, v_b_gate, v_w_proj_attn, v_pool_w, v_pool_scale, v_w_proj_pool, v_conv_w, v_w_proj_conv, v_w_out, v_ffn_norm, v_w_gate_up, v_w_down, v_final_norm):
    weights = dict(attn_norm=attn_norm, w_in=w_in, b_forget=b_forget, b_gate=b_gate, w_proj_attn=w_proj_attn,
                   pool_w=pool_w, pool_scale=pool_scale, w_proj_pool=w_proj_pool, conv_w=conv_w,
                   w_proj_conv=w_proj_conv, w_out=w_out, ffn_norm=ffn_norm, w_gate_up=w_gate_up, w_down=w_down,
                   final_norm=final_norm)
    mom_m = dict(attn_norm=m_attn_norm, w_in=m_w_in, b_forget=m_b_forget, b_gate=m_b_gate, w_proj_attn=m_w_proj_attn,
                 pool_w=m_pool_w, pool_scale=m_pool_scale, w_proj_pool=m_w_proj_pool, conv_w=m_conv_w,
                 w_proj_conv=m_w_proj_conv, w_out=m_w_out, ffn_norm=m_ffn_norm, w_gate_up=m_w_gate_up,
                 w_down=m_w_down, final_norm=m_final_norm)
    mom_v = dict(attn_norm=v_attn_norm, w_in=v_w_in, b_forget=v_b_forget, b_gate=v_b_gate, w_proj_attn=v_w_proj_attn,
                 pool_w=v_pool_w, pool_scale=v_pool_scale, w_proj_pool=v_w_proj_pool, conv_w=v_conv_w,
                 w_proj_conv=v_w_proj_conv, w_out=v_w_out, ffn_norm=v_ffn_norm, w_gate_up=v_w_gate_up,
                 w_down=v_w_down, final_norm=v_final_norm)
    order = ("attn_norm", "w_in", "b_forget", "b_gate", "w_proj_attn", "pool_w", "pool_scale", "w_proj_pool",
             "conv_w", "w_proj_conv", "w_out", "ffn_norm", "w_gate_up", "w_down", "final_norm")

    Bl, S, D = x.shape
    T = Bl * S
    L = w_in.shape[0]
    F = w_down.shape[1] * N_CHIPS
    lay = _proj_layout(D)
    cst = _placement_constants()
    assert S % ATTN_BLOCK == 0 and F % FFN_TILE == 0 and D % BRANCH_W == 0
    assert w_in.shape[2] * N_CHIPS == 3592 + 3 * D

    shard_shapes = [weights[n].shape for n in SHARDED]
    flat = jnp.concatenate([weights[n].astype(BF16).reshape(-1) for n in SHARDED])
    gathered = allgather_chips(_to_rows(flat, 16 * LANES), "allgather_weights").reshape(N_CHIPS, -1)
    parts = jax.vmap(lambda vec: tuple(_split_flat(vec, shard_shapes)))(gathered)
    full = {n: _full_of(p, n) for n, p in zip(SHARDED, parts)}
    w_in_p = _pack_w_in(full["w_in"])
    w_gu_p = _interleave_gu(full["w_gate_up"])
    pool_w_b = pool_w.astype(BF16)
    conv_w_full = full["conv_w"].astype(F32)
    bf_rows = jnp.pad(b_forget, ((0, 0), (0, LANES - HEADS)))

    xs = x.reshape(T, D)
    saved = []
    for l in range(L):
        proj, h = norm_matmul(xs, attn_norm[l:l + 1], w_in_p[l], "in_proj")
        proj3 = proj.reshape(Bl, S, lay["width"])
        qa, ka = attn_prep(proj3, bf_rows[l:l + 1], cst, lay)
        ao, lse = attn_fwd(qa, ka, proj3, lay)
        po, co = poolconv_fwd(proj3, pool_w_b[l], pool_scale[l:l + 1], conv_w_full[l], lay)
        ao2, po2, co2 = (a.reshape(T, BRANCH_W) for a in (ao, po, co))
        x1, ys, mixed = mix_fwd(ao2, po2, co2, proj, b_gate[l:l + 1], full["w_proj_attn"][l],
                                full["w_proj_pool"][l], full["w_proj_conv"][l], full["w_out"][l], xs)
        ab, h2 = norm_matmul(x1, ffn_norm[l:l + 1], w_gu_p[l], "gate_up_proj")
        x2, s_act = ffn_down_fwd(ab, full["w_down"][l], x1)
        saved.append(dict(x=xs, proj=proj, proj3=proj3, h=h, qa=qa, ka=ka, ao=ao, lse=lse, ao2=ao2, po2=po2,
                          co2=co2, ys=ys, mixed=mixed, x1=x1, ab=ab, h2=h2, s=s_act))
        xs = x2

    loss_row, dx, dxb, g_final = loss_head(xs, final_norm.reshape(1, D), loss_target.reshape(T, D))
    loss = lax.psum(loss_row[0, 0], AXES)

    grads = {n: [None] * L for n in order if n != "final_norm"}
    for l in reversed(range(L)):
        sv = saved[l]
        dab = ffn_down_bwd(dxb, full["w_down"][l], sv["ab"])
        grads["w_down"][l] = matmul_tn(sv["s"], dxb, "grad_w_down")
        grads["w_gate_up"][l] = _deinterleave_gu(matmul_tn(sv["h2"], dab, "grad_w_gate_up"))
        dx1, dx1b, g_fn = matmul_nt_normbwd(dab, w_gu_p[l], sv["x1"], ffn_norm[l:l + 1], dx, "gate_up_bwd")
        grads["ffn_norm"][l] = g_fn[0]
        dys, dg, dao, dpo, dco, g_bg = mix_bwd(dx1b, full["w_out"][l], sv["proj"], b_gate[l:l + 1], sv["ys"],
                                               full["w_proj_attn"][l], full["w_proj_pool"][l],
                                               full["w_proj_conv"][l])
        grads["b_gate"][l] = g_bg[0]
        grads["w_out"][l] = matmul_tn(sv["mixed"], dx1b, "grad_w_out")
        for n, (name, br) in enumerate((("w_proj_attn", sv["ao2"]), ("w_proj_pool", sv["po2"]),
                                        ("w_proj_conv", sv["co2"]))):
            grads[name][l] = matmul_tn(br, dys, "grad_" + name, b_cols=(n * D, D))
        to3 = lambda a: a.reshape(Bl, S, BRANCH_W)
        dqa, dka, dv = attn_bwd(sv["qa"], sv["ka"], sv["proj3"], to3(dao), sv["ao"], sv["lse"], lay)
        dq, dk, df, g_bf = attn_post(dqa, dka, sv["proj3"], bf_rows[l:l + 1], cst, lay)
        grads["b_forget"][l] = g_bf[0, :HEADS]
        du, dcv, dcb, dcc, g_pw, g_ps, g_cw = poolconv_bwd(sv["proj3"], to3(dpo), to3(dco), pool_w_b[l],
                                                            pool_scale[l:l + 1], conv_w_full[l], lay)
        grads["pool_w"][l], grads["pool_scale"][l], grads["conv_w"][l] = g_pw, g_ps[0], g_cw
        dproj = jnp.concatenate([dg] + [a.reshape(T, -1) for a in (dq, dk, dv, du, dcv, dcb, dcc, df)], axis=1)
        grads["w_in"][l] = _unpack_w_in(matmul_tn(sv["h"], dproj, "grad_w_in"), D)
        dx, dxb, g_an = matmul_nt_normbwd(dproj, w_in_p[l], sv["x"], attn_norm[l:l + 1], dx1, "in_proj_bwd")
        grads["attn_norm"][l] = g_an[0]
    grad_x = dx.reshape(Bl, S, D)
    local = {n: jnp.stack(g) for n, g in grads.items()}
    local["final_norm"] = g_final[0]

    small_shapes = [weights[n].shape for n in REPLICATED]
    small = jnp.concatenate([local[n].reshape(-1) for n in REPLICATED])
    small = jnp.pad(small, (0, -small.shape[0] % (N_CHIPS * LANES))).reshape(N_CHIPS, -1)
    per_chip = jnp.concatenate([_shards_of(local[n], n).reshape(N_CHIPS, -1) for n in SHARDED] + [small], axis=1)
    n_chip = per_chip.shape[1]
    per_chip = jnp.pad(per_chip, ((0, 0), (0, -n_chip % (2 * 1024 * LANES)))).astype(BF16)
    rows_half = per_chip.shape[1] // (2 * LANES)
    halves = per_chip.reshape(N_CHIPS, 2, rows_half, LANES).transpose(1, 0, 2, 3)
    c = lax.axis_index("c")
    keep = lax.dynamic_index_in_dim(halves, c, 0, keepdims=False).reshape(N_CHIPS * rows_half, LANES)
    give = lax.dynamic_index_in_dim(halves, 1 - c, 0, keepdims=False).reshape(N_CHIPS * rows_half, LANES)
    chip_sum = add_pair(keep, swap_sibling(give, "swap_grad_halves"))
    arrived = exchange_chips(chip_sum.reshape(N_CHIPS, rows_half, LANES), "exchange_grad_chips")
    reduced = join_halves(add_chips(arrived), "join_grad_halves").reshape(-1)[:n_chip]
    n_small = small.shape[1]
    shard_grads = dict(zip(SHARDED, _split_flat(reduced[:n_chip - n_small], shard_shapes)))
    small_all = allgather_chips(_to_rows(reduced[n_chip - n_small:], 8 * LANES), "allgather_small_grads")
    small_vec = small_all.reshape(N_CHIPS, -1)[:, :n_small].reshape(-1)
    rep_grads = dict(zip(REPLICATED, _split_flat(small_vec, small_shapes)))

    delta, new_m, new_v = {}, {}, {}
    for n in SHARDED:
        shp = weights[n].shape
        two_d = lambda a: a.reshape(-1, shp[-1])
        d, nm, nv = adamw(two_d(weights[n]), two_d(shard_grads[n]), two_d(mom_m[n]), two_d(mom_v[n]), "adamw_" + n)
        delta[n], new_m[n], new_v[n] = d.reshape(shp), nm.reshape(shp), nv.reshape(shp)
    rows = lambda d: _to_rows(jnp.concatenate([d[n].reshape(-1) for n in REPLICATED]), 8 * LANES)
    outs = adamw(rows(weights), rows(rep_grads), rows(mom_m), rows(mom_v), "adamw_replicated")
    for res, o in zip((delta, new_m, new_v), outs):
        res.update(zip(REPLICATED, _split_flat(o.reshape(-1), small_shapes)))
    all_grads = {**shard_grads, **rep_grads}

    return (loss, grad_x, *[all_grads[n] for n in order], *[delta[n] for n in order],
            *[new_m[n] for n in order], *[new_v[n] for n in order])
```

```python
import numpy as np
import jax
import jax.numpy as jnp
from jax import lax
from jax.experimental import pallas as pl
from jax.experimental.pallas import tpu as pltpu

F32, BF16 = jnp.float32, jnp.bfloat16
SDS = jax.ShapeDtypeStruct
MESH = pl.DeviceIdType.MESH
AXES = ("x", "y", "c")
N_CHIPS = 4
N_LAYERS = 2
LANES = 128
VMEM_LIMIT = 48 * 1024 * 1024

HEADS, HEAD_DIM = 8, 64
HEAD_PAD = 128
BRANCH_W = 512
GROUP_W = 128
N_GROUPS = BRANCH_W // GROUP_W
POOL_WINDOWS = (2, 4, 8, 16)
F_PAD = 512
FFN_TILE = 256
ATTN_BLOCK = 256
RMS_EPS = 1e-6
NEG_INF = -1e30
ADAM_LR, ADAM_B1, ADAM_B2, ADAM_EPS, ADAM_WD, ADAM_STEP = 0.001, 0.9, 0.999, 1e-08, 0.01, 10

NT = (((1,), (1,)), ((), ()))
TN = (((0,), (0,)), ((), ()))
_ANY = pl.BlockSpec(memory_space=pl.ANY)


def _tile(n, prefs):
    for p in prefs:
        if n % p == 0:
            return p
    raise ValueError(f"no tile of {prefs} divides {n}")


def _params(*sem):
    return pltpu.CompilerParams(dimension_semantics=sem, vmem_limit_bytes=VMEM_LIMIT)


def _sigmoid(z):
    return 1.0 / (1.0 + jnp.exp(-z))


def _split3(x):
    h1 = x.astype(BF16)
    r1 = x - h1.astype(F32)
    h2 = r1.astype(BF16)
    h3 = (r1 - h2.astype(F32)).astype(BF16)
    return h1, h2, h3


def _dot(a, b):
    return jnp.dot(a, b, preferred_element_type=F32)


def _dot_nt(a, b):
    return lax.dot_general(a, b, NT, preferred_element_type=F32)


def _dot_tn(a, b):
    return lax.dot_general(a, b, TN, preferred_element_type=F32)


def norm_matmul(x, gain, w, layer, by_shard, name):
    T, D = x.shape
    if by_shard:
        tn = w.shape[3]
        N = N_CHIPS * tn
        w_spec = pl.BlockSpec((None, None, D, tn), lambda i, j: (j, layer, 0, 0))
    else:
        N = w.shape[2]
        tn = _tile(N, (512, 256, 128))
        w_spec = pl.BlockSpec((None, D, tn), lambda i, j: (layer, 0, j))
    tm = _tile(T, (1024, 512, 256, 128))

    def body(x_ref, g_ref, w_ref, y_ref, h_ref):
        @pl.when(pl.program_id(1) == 0)
        def _():
            xf = x_ref[...]
            r = lax.rsqrt(jnp.mean(xf * xf, axis=-1, keepdims=True) + RMS_EPS)
            h_ref[...] = ((xf * r) * g_ref[...]).astype(BF16)

        y_ref[...] = _dot(h_ref[...], w_ref[...]).astype(BF16)

    return pl.pallas_call(
        body, name=name, grid=(T // tm, N // tn),
        in_specs=[pl.BlockSpec((tm, D), lambda i, j: (i, 0)),
                  pl.BlockSpec((None, 1, D), lambda i, j: (layer, 0, 0)),
                  w_spec],
        out_specs=[pl.BlockSpec((tm, tn), lambda i, j: (i, j)),
                   pl.BlockSpec((tm, D), lambda i, j: (i, 0))],
        out_shape=[SDS((T, N), BF16), SDS((T, D), BF16)],
        compiler_params=_params("arbitrary", "arbitrary"),
    )(x, gain, w)


def matmul_nt_normbwd(dys, w, layer, by_shard, x, gain, dres, name):
    T, D = x.shape
    width = dys[0].shape[1]
    if by_shard:
        tk = w.shape[3]
        w_spec = pl.BlockSpec((None, None, D, tk), lambda i, k: (k, layer, 0, 0))
    else:
        tk = _tile(width, (512, 256, 128))
        w_spec = pl.BlockSpec((None, D, tk), lambda i, k: (layer, 0, k))
    per = width // tk
    nk = per * len(dys)
    tm = _tile(T, (512, 256, 128))
    n_dy = len(dys)

    def dy_spec(p):
        return pl.BlockSpec((tm, tk), lambda i, k: (i, jnp.clip(k - p * per, 0, per - 1)))

    def body(*refs):
        dy_refs = refs[:n_dy]
        w_ref, x_ref, g_ref, dres_ref, dx_ref, dxb_ref, dg_ref, acc_ref = refs[n_dy:]
        i, k = pl.program_id(0), pl.program_id(1)

        @pl.when(k == 0)
        def _():
            acc_ref[...] = jnp.zeros_like(acc_ref)

        for p in range(n_dy):
            @pl.when((k >= p * per) & (k < (p + 1) * per))
            def _(p=p):
                acc_ref[...] += _dot_nt(dy_refs[p][...], w_ref[...])

        @pl.when(k == nk - 1)
        def _():
            xf = x_ref[...]
            r = lax.rsqrt(jnp.mean(xf * xf, axis=-1, keepdims=True) + RMS_EPS)
            xhat = xf * r
            dh = acc_ref[...]
            dhg = dh * g_ref[...]
            dx = dres_ref[...] + r * (dhg - xhat * jnp.mean(dhg * xhat, axis=-1, keepdims=True))
            dx_ref[...] = dx
            dxb_ref[...] = dx.astype(BF16)
            part = jnp.sum(dh * xhat, axis=0, keepdims=True)

            @pl.when(i == 0)
            def _():
                dg_ref[...] = part

            @pl.when(i > 0)
            def _():
                dg_ref[...] += part

    row = pl.BlockSpec((tm, D), lambda i, k: (i, 0))
    return pl.pallas_call(
        body, name=name, grid=(T // tm, nk),
        in_specs=[dy_spec(p) for p in range(n_dy)] + [
            w_spec, row, pl.BlockSpec((None, 1, D), lambda i, k: (layer, 0, 0)), row],
        out_specs=[row, row, pl.BlockSpec((1, D), lambda i, k: (0, 0))],
        out_shape=[SDS((T, D), F32), SDS((T, D), BF16), SDS((1, D), F32)],
        scratch_shapes=[pltpu.VMEM((tm, D), F32)],
        compiler_params=_params("arbitrary", "arbitrary"),
    )(*dys, w, x, gain, dres)


def matmul_tn(a, bs, name, layer, stacked, b_col0=0, n_cols=None, by_dest=False, tn=None, tk=None):
    T, M = a.shape
    width = bs[0].shape[1]
    N = n_cols if n_cols else width * len(bs)
    tm = _tile(M, (1024, 512, 256, 128))
    tn = tn or _tile(N, (512, 256, 128))
    tk = tk or _tile(T, (4096, 2048, 1024, 512, 256))
    assert b_col0 % tn == 0 and width % tn == 0
    j0, per, nk, n_b = b_col0 // tn, width // tn, T // tk, len(bs)

    def b_spec(p):
        return pl.BlockSpec((tk, tn), lambda i, j, k: (k, jnp.clip(j0 + j - p * per, 0, per - 1)))

    def body(*refs):
        a_ref, b_refs = refs[0], refs[1:1 + n_b]
        o_ref, acc_ref = refs[-2], refs[-1]
        j, k = pl.program_id(1), pl.program_id(2)

        @pl.when(k == 0)
        def _():
            acc_ref[...] = jnp.zeros_like(acc_ref)

        for p in range(n_b):
            @pl.when((j0 + j >= p * per) & (j0 + j < (p + 1) * per))
            def _(p=p):
                acc_ref[...] += _dot_tn(a_ref[...], b_refs[p][...])

        @pl.when(k == nk - 1)
        def _():
            o_ref[...] = acc_ref[...].astype(BF16)

    if by_dest:
        cs = N // N_CHIPS
        npd = cs // tn
        out_shape = SDS((N_LAYERS, N_CHIPS, M, cs), BF16)
        out_spec = pl.BlockSpec((None, None, tm, tn), lambda i, j, k: (layer, j // npd, i, j % npd))
    else:
        out_shape = SDS((N_LAYERS, M, N), BF16)
        out_spec = pl.BlockSpec((None, tm, tn), lambda i, j, k: (layer, i, j))
    ins = [a] + list(bs)
    in_specs = [pl.BlockSpec((tk, tm), lambda i, j, k: (k, i))] + [b_spec(p) for p in range(n_b)]
    aliases = {}
    if stacked is not None:
        ins.append(stacked)
        in_specs.append(_ANY)
        aliases = {len(ins) - 1: 0}

    def body_wrap(*refs):
        if stacked is not None:
            refs = refs[:1 + n_b] + refs[2 + n_b:]
        body(*refs)

    return pl.pallas_call(
        body_wrap, name=name, grid=(M // tm, N // tn, nk),
        in_specs=in_specs, out_specs=out_spec, out_shape=out_shape,
        scratch_shapes=[pltpu.VMEM((tm, tn), F32)], input_output_aliases=aliases,
        compiler_params=_params("arbitrary", "arbitrary", "arbitrary"),
    )(*ins)


def ffn_down_fwd(ab, w_down, layer, x1):
    T, D = x1.shape
    F = w_down.shape[1]
    tm = _tile(T, (512, 256, 128))
    tk = FFN_TILE
    nk = F // tk

    def body(a_ref, b_ref, w_ref, x_ref, x2_ref, s_ref, acc_ref):
        k = pl.program_id(1)

        @pl.when(k == 0)
        def _():
            acc_ref[...] = x_ref[...]

        a = a_ref[...].astype(F32)
        s = (a * _sigmoid(a) * b_ref[...].astype(F32)).astype(BF16)
        s_ref[...] = s
        acc_ref[...] += _dot(s, w_ref[...])

        @pl.when(k == nk - 1)
        def _():
            x2_ref[...] = acc_ref[...]

    return pl.pallas_call(
        body, name="ffn_down_fwd", grid=(T // tm, nk),
        in_specs=[pl.BlockSpec((tm, tk), lambda i, k: (i, k)),
                  pl.BlockSpec((tm, tk), lambda i, k: (i, nk + k)),
                  pl.BlockSpec((None, tk, D), lambda i, k: (layer, k, 0)),
                  pl.BlockSpec((tm, D), lambda i, k: (i, 0))],
        out_specs=[pl.BlockSpec((tm, D), lambda i, k: (i, 0)),
                   pl.BlockSpec((tm, tk), lambda i, k: (i, k))],
        out_shape=[SDS((T, D), F32), SDS((T, F), BF16)],
        scratch_shapes=[pltpu.VMEM((tm, D), F32)],
        compiler_params=_params("arbitrary", "arbitrary"),
    )(ab, ab, w_down, x1)


def ffn_down_bwd(dx2b, w_down, layer, ab):
    T, D = dx2b.shape
    F = w_down.shape[1]
    tm = _tile(T, (512, 256, 128))
    tn = FFN_TILE
    nj = F // tn

    def body(dx_ref, w_ref, a_ref, b_ref, da_ref, db_ref):
        ds = _dot_nt(dx_ref[...], w_ref[...])
        a = a_ref[...].astype(F32)
        sg = _sigmoid(a)
        da_ref[...] = (ds * b_ref[...].astype(F32) * (sg * (1.0 + a * (1.0 - sg)))).astype(BF16)
        db_ref[...] = (ds * (a * sg)).astype(BF16)

    blk = pl.BlockSpec((tm, tn), lambda i, j: (i, j))
    return pl.pallas_call(
        body, name="ffn_down_bwd", grid=(T // tm, nj),
        in_specs=[pl.BlockSpec((tm, D), lambda i, j: (i, 0)),
                  pl.BlockSpec((None, tn, D), lambda i, j: (layer, j, 0)),
                  blk, pl.BlockSpec((tm, tn), lambda i, j: (i, nj + j))],
        out_specs=[blk, blk],
        out_shape=[SDS((T, F), BF16), SDS((T, F), BF16)],
        compiler_params=_params("arbitrary", "arbitrary"),
    )(dx2b, w_down, ab, ab)


def _mix_specs(tm, D, layer):
    cs = D // N_CHIPS
    row = lambda w: pl.BlockSpec((tm, w), lambda i: (i, 0))
    wp = pl.BlockSpec((N_CHIPS, None, BRANCH_W, cs), lambda i: (0, layer, 0, 0))
    wo = pl.BlockSpec((None, N_CHIPS, cs, D), lambda i: (layer, 0, 0, 0))
    bg = pl.BlockSpec((None, 1, 3 * D), lambda i: (layer, 0, 0))
    return row, wp, wo, bg


def mix_fwd(ao, po, co, proj, b_gate, wpa, wpp, wpc, w_out, layer, x):
    T, D = x.shape
    cs = D // N_CHIPS
    tm = _tile(T, (256, 128))
    row, wp, wo, bg = _mix_specs(tm, D, layer)

    def body(ao_ref, po_ref, co_ref, g_ref, bg_ref, wpa_ref, wpp_ref, wpc_ref, wo_ref, x_ref,
             x1_ref, ys_ref, mixed_ref):
        mixed = jnp.zeros((tm, D), F32)
        for n, (br, wp_ref) in enumerate(((ao_ref, wpa_ref), (po_ref, wpp_ref), (co_ref, wpc_ref))):
            y = jnp.concatenate([_dot(br[...], wp_ref[j]) for j in range(N_CHIPS)], axis=1)
            cols = slice(n * D, (n + 1) * D)
            gate = _sigmoid(g_ref[:, cols].astype(F32) + bg_ref[:, cols])
            ys_ref[:, cols] = y.astype(BF16)
            mixed = mixed + gate * y
        mb = mixed.astype(BF16)
        mixed_ref[...] = mb
        acc = x_ref[...]
        for j in range(N_CHIPS):
            acc = acc + _dot(mb[:, j * cs:(j + 1) * cs], wo_ref[j])
        x1_ref[...] = acc

    return pl.pallas_call(
        body, name="mix_fwd", grid=(T // tm,),
        in_specs=[row(BRANCH_W), row(BRANCH_W), row(BRANCH_W), row(3 * D), bg, wp, wp, wp, wo, row(D)],
        out_specs=[row(D), row(3 * D), row(D)],
        out_shape=[SDS((T, D), F32), SDS((T, 3 * D), BF16), SDS((T, D), BF16)],
        compiler_params=_params("arbitrary"),
    )(ao, po, co, proj, b_gate, wpa, wpp, wpc, w_out, x)


def mix_bwd(dx1b, w_out, proj, b_gate, ys, wpa, wpp, wpc, layer, width):
    T, D = dx1b.shape
    cs = D // N_CHIPS
    tm = _tile(T, (256, 128))
    row, wp, wo, bg = _mix_specs(tm, D, layer)

    def body(dx_ref, wo_ref, g_ref, bg_ref, ys_ref, wpa_ref, wpp_ref, wpc_ref,
             dys_ref, dg_ref, dao_ref, dpo_ref, dco_ref, dbg_ref):
        i = pl.program_id(0)
        dx = dx_ref[...]
        dmixed = jnp.concatenate([_dot_nt(dx, wo_ref[j]) for j in range(N_CHIPS)], axis=1)
        for n, (wp_ref, dbr) in enumerate(((wpa_ref, dao_ref), (wpp_ref, dpo_ref), (wpc_ref, dco_ref))):
            cols = slice(n * D, (n + 1) * D)
            gate = _sigmoid(g_ref[:, cols].astype(F32) + bg_ref[:, cols])
            dy = (dmixed * gate).astype(BF16)
            dys_ref[:, cols] = dy
            dgp = dmixed * ys_ref[:, cols].astype(F32) * gate * (1.0 - gate)
            dg_ref[:, cols] = dgp.astype(BF16)
            part = jnp.sum(dgp, axis=0, keepdims=True)

            @pl.when(i == 0)
            def _():
                dbg_ref[:, cols] = part

            @pl.when(i > 0)
            def _():
                dbg_ref[:, cols] += part

            acc = jnp.zeros((tm, BRANCH_W), F32)
            for j in range(N_CHIPS):
                acc = acc + _dot_nt(dy[:, j * cs:(j + 1) * cs], wp_ref[j])
            dbr[...] = acc.astype(BF16)

    return pl.pallas_call(
        body, name="mix_bwd", grid=(T // tm,),
        in_specs=[row(D), wo, row(3 * D), bg, row(3 * D), wp, wp, wp],
        out_specs=[row(3 * D), row(3 * D), row(BRANCH_W), row(BRANCH_W), row(BRANCH_W),
                   pl.BlockSpec((1, 3 * D), lambda i: (0, 0))],
        out_shape=[SDS((T, 3 * D), BF16), SDS((T, width), BF16), SDS((T, BRANCH_W), BF16),
                   SDS((T, BRANCH_W), BF16), SDS((T, BRANCH_W), BF16), SDS((1, 3 * D), F32)],
        compiler_params=_params("arbitrary"),
    )(dx1b, w_out, proj, b_gate, ys, wpa, wpp, wpc)


def loss_head(x2, gain, target):
    T, D = x2.shape
    tm = _tile(T, (512, 256, 128))

    def body(x_ref, g_ref, t_ref, loss_ref, dx_ref, dxb_ref, dg_ref):
        i = pl.program_id(0)
        xf = x_ref[...]
        g = g_ref[...]
        r = lax.rsqrt(jnp.mean(xf * xf, axis=-1, keepdims=True) + RMS_EPS)
        xhat = xf * r
        diff = xhat * g - t_ref[...]
        part_loss = 0.5 * jnp.sum(jnp.mean(diff * diff, axis=-1, keepdims=True), axis=0, keepdims=True)
        dy = diff * (1.0 / D)
        dhg = dy * g
        dx = r * (dhg - xhat * jnp.mean(dhg * xhat, axis=-1, keepdims=True))
        dx_ref[...] = dx
        dxb_ref[...] = dx.astype(BF16)
        part_g = jnp.sum(dy * xhat, axis=0, keepdims=True)
        part_l = jnp.broadcast_to(part_loss, (1, LANES))

        @pl.when(i == 0)
        def _():
            dg_ref[...] = part_g
            loss_ref[...] = part_l

        @pl.when(i > 0)
        def _():
            dg_ref[...] += part_g
            loss_ref[...] += part_l

    row = pl.BlockSpec((tm, D), lambda i: (i, 0))
    return pl.pallas_call(
        body, name="loss_head", grid=(T // tm,),
        in_specs=[row, pl.BlockSpec((1, D), lambda i: (0, 0)), row],
        out_specs=[pl.BlockSpec((1, LANES), lambda i: (0, 0)), row, row, pl.BlockSpec((1, D), lambda i: (0, 0))],
        out_shape=[SDS((1, LANES), F32), SDS((T, D), F32), SDS((T, D), BF16), SDS((1, D), F32)],
        compiler_params=_params("arbitrary"),
    )(x2, gain, target)


def _placement_constants():
    w = HEADS * HEAD_PAD
    pq = np.zeros((BRANCH_W, w), np.float32)
    pk = np.zeros((BRANCH_W, w), np.float32)
    pfq = np.zeros((3, LANES, w), np.float32)
    pfk = np.zeros((3, LANES, w), np.float32)
    cq = np.zeros((1, w), np.float32)
    ck = np.zeros((1, w), np.float32)
    eq = np.zeros((w, LANES), np.float32)
    ek = np.zeros((w, LANES), np.float32)
    for h in range(HEADS):
        for d in range(HEAD_DIM):
            pq[h * HEAD_DIM + d, h * HEAD_PAD + d] = HEAD_DIM ** -0.5
            pk[h * HEAD_DIM + d, h * HEAD_PAD + d] = 1.0
        for i in range(3):
            pfq[i, h, h * HEAD_PAD + HEAD_DIM + i] = 1.0
            pfk[i, h, h * HEAD_PAD + HEAD_DIM + 3 + i] = -1.0
            cq[0, h * HEAD_PAD + HEAD_DIM + 3 + i] = 1.0
            ck[0, h * HEAD_PAD + HEAD_DIM + i] = 1.0
        eq[h * HEAD_PAD + HEAD_DIM, h] = 1.0
        ek[h * HEAD_PAD + HEAD_DIM + 3, h] = -1.0
    bf = lambda a: jnp.asarray(a, BF16)
    return dict(pq=bf(pq), pk=bf(pk), pfq=bf(pfq), pfk=bf(pfk), cq=jnp.asarray(cq), ck=jnp.asarray(ck),
                pqkt=bf(np.concatenate([pq.T, pk.T], axis=0)), eq=bf(eq), ek=bf(ek))


def attn_prep(proj3, bf_rows, layer, cst, lay):
    Bl, S, _ = proj3.shape
    ts = ATTN_BLOCK
    w = HEADS * HEAD_PAD

    def body(q_ref, k_ref, f_ref, bf_ref, pq_ref, pk_ref, pfq_ref, pfk_ref, cq_ref, ck_ref,
             qa_ref, ka_ref, carry_ref):
        @pl.when(pl.program_id(1) == 0)
        def _():
            carry_ref[...] = jnp.zeros_like(carry_ref)

        z = f_ref[...].astype(F32) + bf_ref[...]
        logf = jnp.minimum(z, 0.0) - jnp.log(1.0 + jnp.exp(-jnp.abs(z)))
        r = lax.broadcasted_iota(jnp.int32, (ts, ts), 0)
        c = lax.broadcasted_iota(jnp.int32, (ts, ts), 1)
        tri = jnp.where(r >= c, 1.0, 0.0).astype(BF16)
        fcum = carry_ref[...]
        for part in _split3(logf):
            fcum = fcum + _dot(tri, part)
        carry_ref[...] = fcum[ts - 1:ts, :]
        qa = _dot(q_ref[...], pq_ref[...]) + cq_ref[...]
        ka = _dot(k_ref[...], pk_ref[...]) + ck_ref[...]
        for i, part in enumerate(_split3(fcum)):
            qa = qa + _dot(part, pfq_ref[i])
            ka = ka + _dot(part, pfk_ref[i])
        qa_ref[...] = qa.astype(BF16)
        ka_ref[...] = ka.astype(BF16)

    cfull = lambda shape: pl.BlockSpec(shape, lambda b, s: (0,) * len(shape))
    return pl.pallas_call(
        body, name="attn_prep", grid=(Bl, S // ts),
        in_specs=[pl.BlockSpec((None, ts, BRANCH_W), lambda b, s: (b, s, lay["q"] // BRANCH_W)),
                  pl.BlockSpec((None, ts, BRANCH_W), lambda b, s: (b, s, lay["k"] // BRANCH_W)),
                  pl.BlockSpec((None, ts, LANES), lambda b, s: (b, s, lay["f"] // LANES)),
                  pl.BlockSpec((None, 1, LANES), lambda b, s: (layer, 0, 0)),
                  cfull((BRANCH_W, w)), cfull((BRANCH_W, w)),
                  cfull((3, LANES, w)), cfull((3, LANES, w)), cfull((1, w)), cfull((1, w))],
        out_specs=[pl.BlockSpec((None, ts, w), lambda b, s: (b, s, 0)),
                   pl.BlockSpec((None, ts, w), lambda b, s: (b, s, 0))],
        out_shape=[SDS((Bl, S, w), BF16), SDS((Bl, S, w), BF16)],
        scratch_shapes=[pltpu.VMEM((1, LANES), F32)],
        compiler_params=_params("arbitrary", "arbitrary"),
    )(proj3, proj3, proj3, bf_rows, cst["pq"], cst["pk"], cst["pfq"], cst["pfk"], cst["cq"], cst["ck"])


def attn_fwd(qa, ka, proj3, lay):
    Bl, S, _ = qa.shape
    tq = ATTN_BLOCK
    nq = S // tq
    pairs = HEADS // 2
    pw = 2 * HEAD_PAD
    vw = 2 * HEAD_DIM

    def body(qa_ref, ka_ref, v_ref, o_ref, lse_ref):
        row = lax.broadcasted_iota(jnp.int32, (tq, tq), 0)
        col = lax.broadcasted_iota(jnp.int32, (tq, tq), 1)
        causal = row <= col
        for i in range(nq):
            nk = (i + 1) * tq
            rows = slice(i * tq, nk)
            o_t = []
            for h in range(2):
                hs = slice(h * HEAD_PAD, (h + 1) * HEAD_PAD)
                st = _dot_nt(ka_ref[0:nk, hs], qa_ref[rows, hs])
                diag = jnp.where(causal, st[nk - tq:], NEG_INF)
                m = jnp.max(diag, axis=0, keepdims=True)
                if i:
                    m = jnp.maximum(m, jnp.max(st[:nk - tq], axis=0, keepdims=True))
                p_diag = jnp.exp(diag - m)
                l = jnp.sum(p_diag, axis=0, keepdims=True)
                if i:
                    p_top = jnp.exp(st[:nk - tq] - m)
                    l = l + jnp.sum(p_top, axis=0, keepdims=True)
                    p = jnp.concatenate([p_top.astype(BF16), p_diag.astype(BF16)], axis=0)
                else:
                    p = p_diag.astype(BF16)
                acc = _dot_tn(v_ref[0:nk, :], p)
                o_t.append(acc[h * HEAD_DIM:(h + 1) * HEAD_DIM, :] / l)
                lse_ref[h:h + 1, rows] = m + jnp.log(l)
            o_ref[rows, :] = jnp.concatenate(o_t, axis=0).T.astype(BF16)

    return pl.pallas_call(
        body, name="attn_fwd", grid=(Bl, pairs),
        in_specs=[pl.BlockSpec((None, S, pw), lambda b, p: (b, 0, p)),
                  pl.BlockSpec((None, S, pw), lambda b, p: (b, 0, p)),
                  pl.BlockSpec((None, S, vw), lambda b, p: (b, 0, lay["v"] // vw + p))],
        out_specs=[pl.BlockSpec((None, S, vw), lambda b, p: (b, 0, p)),
                   pl.BlockSpec((None, None, 2, S), lambda b, p: (b, p, 0, 0))],
        out_shape=[SDS((Bl, S, BRANCH_W), BF16), SDS((Bl, pairs, 2, S), F32)],
        compiler_params=_params("arbitrary", "arbitrary"),
    )(qa, ka, proj3)


def attn_bwd(qa, ka, proj3, dao, ao, lse, dproj3, lay):
    Bl, S, _ = qa.shape
    tk = ATTN_BLOCK
    nq = S // tk
    pairs = HEADS // 2
    pw = 2 * HEAD_PAD
    vw = 2 * HEAD_DIM

    def body(qa_ref, ka_ref, v_ref, do_ref, o_ref, lse_ref, _, dqa_ref, dka_ref, dv_ref):
        row = lax.broadcasted_iota(jnp.int32, (tk, tk), 0)
        col = lax.broadcasted_iota(jnp.int32, (tk, tk), 1)
        causal = row <= col
        lane8 = lax.broadcasted_iota(jnp.int32, (8, vw), 1)
        lane_s = lax.broadcasted_iota(jnp.int32, (S, vw), 1)
        lane_k = lax.broadcasted_iota(jnp.int32, (tk, vw), 1)
        doo = do_ref[...].astype(F32) * o_ref[...].astype(F32)
        hi = doo.astype(BF16)
        lo = (doo - hi.astype(F32)).astype(BF16)
        delta, v_head = [], []
        for h in range(2):
            sel = jnp.where((lane8 >= h * HEAD_DIM) & (lane8 < (h + 1) * HEAD_DIM), 1.0, 0.0).astype(BF16)
            delta.append((_dot_nt(sel, hi) + _dot_nt(sel, lo))[0:1, :])
            in_head = (lane_s >= h * HEAD_DIM) & (lane_s < (h + 1) * HEAD_DIM)
            v_head.append(jnp.where(in_head, v_ref[...], jnp.zeros_like(v_ref[...])))
        dqa_ref[...] = jnp.zeros_like(dqa_ref)
        for j in range(nq):
            q0 = j * tk
            krows = slice(q0, q0 + tk)
            do = do_ref[q0:, :]
            dvs = []
            for h in range(2):
                hs = slice(h * HEAD_PAD, (h + 1) * HEAD_PAD)
                k = ka_ref[krows, hs]
                q = qa_ref[q0:, hs]
                st = _dot_nt(k, q)
                p = jnp.exp(st - lse_ref[h:h + 1, q0:])
                p_diag = jnp.where(causal, p[:, :tk], 0.0)
                p = jnp.concatenate([p_diag, p[:, tk:]], axis=1) if j < nq - 1 else p_diag
                dvs.append(_dot(p.astype(BF16), do))
                dpt = _dot_nt(v_head[h][krows, :], do)
                ds = (p * (dpt - delta[h][:, q0:])).astype(BF16)
                dka_ref[krows, hs] = _dot(ds, q)
                dqa_ref[q0:, hs] += _dot_tn(ds, k)
            dv_ref[krows, :] = jnp.where(lane_k < HEAD_DIM, dvs[0], dvs[1]).astype(BF16)

    seq = lambda w, c0=0: pl.BlockSpec((None, S, w), lambda b, p: (b, 0, c0 + p))
    return pl.pallas_call(
        body, name="attn_bwd", grid=(Bl, pairs),
        in_specs=[seq(pw), seq(pw), seq(vw, lay["v"] // vw), seq(vw), seq(vw),
                  pl.BlockSpec((None, None, 2, S), lambda b, p: (b, p, 0, 0)), _ANY],
        out_specs=[seq(pw), seq(pw), seq(vw, lay["v"] // vw)],
        out_shape=[SDS((Bl, S, HEADS * HEAD_PAD), F32), SDS((Bl, S, HEADS * HEAD_PAD), F32),
                   SDS(dproj3.shape, BF16)],
        input_output_aliases={6: 2},
        compiler_params=_params("arbitrary", "arbitrary"),
    )(qa, ka, proj3, dao, ao, lse, dproj3)


def attn_post(dqa, dka, proj3, bf_rows, layer, dproj3, cst, lay):
    Bl, S, w = dqa.shape
    ts = ATTN_BLOCK
    ns = S // ts
    qkf = 2 * BRANCH_W + F_PAD

    def body(dqa_ref, dka_ref, f_ref, bf_ref, pqkt_ref, eq_ref, ek_ref, _, dqkf_ref, dbf_ref, carry_ref):
        b, s = pl.program_id(0), pl.program_id(1)

        @pl.when(s == 0)
        def _():
            carry_ref[...] = jnp.zeros_like(carry_ref)

        dqa_v, dka_v = dqa_ref[...], dka_ref[...]
        qh = dqa_v.astype(BF16)
        kh = dka_v.astype(BF16)
        dqkf_ref[:, :BRANCH_W] = _dot(qh, pqkt_ref[:w, :]).astype(BF16)
        dqkf_ref[:, BRANCH_W:2 * BRANCH_W] = _dot(kh, pqkt_ref[w:, :]).astype(BF16)
        ql = (dqa_v - qh.astype(F32)).astype(BF16)
        kl = (dka_v - kh.astype(F32)).astype(BF16)
        d_f = (_dot(qh, eq_ref[...]) + _dot(ql, eq_ref[...])) + (_dot(kh, ek_ref[...]) + _dot(kl, ek_ref[...]))
        r = lax.broadcasted_iota(jnp.int32, (ts, ts), 0)
        c = lax.broadcasted_iota(jnp.int32, (ts, ts), 1)
        triu = jnp.where(c >= r, 1.0, 0.0).astype(BF16)
        rev = carry_ref[...]
        for part in _split3(d_f):
            rev = rev + _dot(triu, part)
        carry_ref[...] = rev[0:1, :]
        z = f_ref[...].astype(F32) + bf_ref[...]
        lane = lax.broadcasted_iota(jnp.int32, (ts, LANES), 1)
        dfl = jnp.where(lane < HEADS, rev / (1.0 + jnp.exp(z)), 0.0)
        dqkf_ref[:, 2 * BRANCH_W:] = jnp.concatenate(
            [dfl.astype(BF16), jnp.zeros((ts, F_PAD - LANES), BF16)], axis=1)
        part = jnp.sum(dfl, axis=0, keepdims=True)

        @pl.when((b == 0) & (s == 0))
        def _():
            dbf_ref[...] = part

        @pl.when((b > 0) | (s > 0))
        def _():
            dbf_ref[...] += part

    assert lay["q"] % qkf == 0
    cfull = lambda shape: pl.BlockSpec(shape, lambda b, s: (0,) * len(shape))
    rev_blk = lambda wd, c0=0: pl.BlockSpec((None, ts, wd), lambda b, s: (b, ns - 1 - s, c0))
    return pl.pallas_call(
        body, name="attn_post", grid=(Bl, ns),
        in_specs=[rev_blk(w), rev_blk(w), rev_blk(LANES, lay["f"] // LANES),
                  pl.BlockSpec((None, 1, LANES), lambda b, s: (layer, 0, 0)),
                  cfull((2 * w, BRANCH_W)), cfull((w, LANES)), cfull((w, LANES)), _ANY],
        out_specs=[rev_blk(qkf, lay["q"] // qkf), cfull((1, LANES))],
        out_shape=[SDS(dproj3.shape, BF16), SDS((1, LANES), F32)],
        scratch_shapes=[pltpu.VMEM((1, LANES), F32)],
        input_output_aliases={7: 0},
        compiler_params=_params("arbitrary", "arbitrary"),
    )(dqa, dka, proj3, bf_rows, cst["pqkt"], cst["eq"], cst["ek"], dproj3)


def _shift_down(x, k, row):
    return jnp.where(row >= k, pltpu.roll(x, k, axis=0), 0.0)


def _shift_up(x, k, row):
    n = x.shape[0]
    return jnp.where(row < n - k, pltpu.roll(x, n - k, axis=0), 0.0)


def _window_sum(x, g, row, shift):
    s2 = x + shift(x, 1, row)
    s4 = s2 + shift(s2, 2, row)
    s8 = s4 + shift(s4, 4, row)
    s16 = s8 + shift(s8, 8, row)
    return jnp.where(g == 0, s2, jnp.where(g == 1, s4, jnp.where(g == 2, s8, s16)))


def _window_count(g, row):
    wnd = jnp.where(g == 0, 2, jnp.where(g == 1, 4, jnp.where(g == 2, 8, 16)))
    return jnp.minimum(row + 1, wnd).astype(F32)


def _group_columns(ref):
    return [ref[:, n * GROUP_W:(n + 1) * GROUP_W].astype(F32) for n in range(4)]


def poolconv_fwd(proj3, pool_w, pool_scale, conv_w, layer, lay):
    Bl, S, _ = proj3.shape

    def body(x_ref, pw_ref, ps_ref, cw_ref, po_ref, co_ref):
        g = pl.program_id(1)
        row = lax.broadcasted_iota(jnp.int32, (S, GROUP_W), 0)
        u, cv, cb, cc = _group_columns(x_ref)
        d = _window_sum(u, g, row, _shift_down) / _window_count(g, row) - u
        po_ref[...] = (_dot(d.astype(BF16), pw_ref[...]) * ps_ref[...]).astype(BF16)
        z = cc * cv
        y = cw_ref[0:1, :] * _shift_down(z, 2, row) + cw_ref[1:2, :] * _shift_down(z, 1, row) + cw_ref[2:3, :] * z
        co_ref[...] = (cb * y).astype(BF16)

    out = pl.BlockSpec((None, S, GROUP_W), lambda b, g: (b, 0, g))
    return pl.pallas_call(
        body, name="poolconv_fwd", grid=(Bl, N_GROUPS),
        in_specs=[pl.BlockSpec((None, S, BRANCH_W), lambda b, g: (b, 0, lay["pc"] // BRANCH_W + g)),
                  pl.BlockSpec((None, None, GROUP_W, GROUP_W), lambda b, g: (layer, g, 0, 0)),
                  pl.BlockSpec((None, 1, GROUP_W), lambda b, g: (layer, 0, g)),
                  pl.BlockSpec((None, None, 3, GROUP_W), lambda b, g: (g, layer, 0, 0))],
        out_specs=[out, out],
        out_shape=[SDS((Bl, S, BRANCH_W), BF16), SDS((Bl, S, BRANCH_W), BF16)],
        compiler_params=_params("arbitrary", "arbitrary"),
    )(proj3, pool_w, pool_scale, conv_w)


def poolconv_bwd(proj3, dpo, dco, pool_w, pool_scale, conv_w, layer, dproj3, dcw_stacked, lay):
    Bl, S, _ = proj3.shape

    def body(x_ref, dpo_ref, dco_ref, pw_ref, ps_ref, cw_ref, *rest):
        dx_ref, dpw_ref, dps_ref, dcw_ref, dcw_acc = rest[-5:]
        g, b = pl.program_id(0), pl.program_id(1)
        row = lax.broadcasted_iota(jnp.int32, (S, GROUP_W), 0)
        cnt = _window_count(g, row)
        u, cv, cb, cc = _group_columns(x_ref)
        d = (_window_sum(u, g, row, _shift_down) / cnt - u).astype(BF16)
        pw = pw_ref[...]
        ypre = _dot(d, pw)
        dpo_v = dpo_ref[...].astype(F32)
        dps = jnp.sum(dpo_v * ypre, axis=0, keepdims=True)
        dyp = (dpo_v * ps_ref[...]).astype(BF16)
        dpw = _dot_tn(d, dyp)
        dd = _dot_nt(dyp, pw)
        dx_ref[:, 0:GROUP_W] = (_window_sum(dd / cnt, g, row, _shift_up) - dd).astype(BF16)

        z = cc * cv
        z1, z2 = _shift_down(z, 1, row), _shift_down(z, 2, row)
        w0, w1, w2 = cw_ref[0:1, :], cw_ref[1:2, :], cw_ref[2:3, :]
        y = w0 * z2 + w1 * z1 + w2 * z
        dco_v = dco_ref[...].astype(F32)
        dy = dco_v * cb
        dz = w0 * _shift_up(dy, 2, row) + w1 * _shift_up(dy, 1, row) + w2 * dy
        dx_ref[:, GROUP_W:2 * GROUP_W] = (dz * cc).astype(BF16)
        dx_ref[:, 2 * GROUP_W:3 * GROUP_W] = (dco_v * y).astype(BF16)
        dx_ref[:, 3 * GROUP_W:] = (dz * cv).astype(BF16)
        dcw = jnp.concatenate([jnp.sum(dy * z2, axis=0, keepdims=True),
                               jnp.sum(dy * z1, axis=0, keepdims=True),
                               jnp.sum(dy * z, axis=0, keepdims=True)], axis=0)

        @pl.when(b == 0)
        def _():
            dpw_ref[...] = dpw
            dps_ref[...] = dps
            dcw_acc[...] = dcw

        @pl.when(b > 0)
        def _():
            dpw_ref[...] += dpw
            dps_ref[...] += dps
            dcw_acc[...] += dcw

        @pl.when(b == Bl - 1)
        def _():
            dcw_ref[...] = dcw_acc[...].astype(BF16)

    blk = pl.BlockSpec((None, S, GROUP_W), lambda g, b: (b, 0, g))
    pc = pl.BlockSpec((None, S, BRANCH_W), lambda g, b: (b, 0, lay["pc"] // BRANCH_W + g))
    ins = [proj3, dpo, dco, pool_w, pool_scale, conv_w, dproj3]
    in_specs = [pc, blk, blk,
                pl.BlockSpec((None, None, GROUP_W, GROUP_W), lambda g, b: (layer, g, 0, 0)),
                pl.BlockSpec((None, 1, GROUP_W), lambda g, b: (layer, 0, g)),
                pl.BlockSpec((None, None, 3, GROUP_W), lambda g, b: (g, layer, 0, 0)), _ANY]
    aliases = {6: 0}
    if dcw_stacked is not None:
        ins.append(dcw_stacked)
        in_specs.append(_ANY)
        aliases[7] = 3
    return pl.pallas_call(
        body, name="poolconv_bwd", grid=(N_GROUPS, Bl),
        in_specs=in_specs,
        out_specs=[pc, pl.BlockSpec((None, GROUP_W, GROUP_W), lambda g, b: (g, 0, 0)),
                   pl.BlockSpec((1, GROUP_W), lambda g, b: (0, g)),
                   pl.BlockSpec((None, None, 3, GROUP_W), lambda g, b: (layer, g, 0, 0))],
        out_shape=[SDS(dproj3.shape, BF16), SDS((N_GROUPS, GROUP_W, GROUP_W), F32), SDS((1, BRANCH_W), F32),
                   SDS((N_LAYERS, N_CHIPS, 3, GROUP_W), BF16)],
        scratch_shapes=[pltpu.VMEM((3, GROUP_W), F32)],
        input_output_aliases=aliases,
        compiler_params=_params("arbitrary", "arbitrary"),
    )(*ins)


def _row_tile(rows, cols, n_arrays):
    if rows % 8:
        return rows
    lanes = -(-cols // LANES) * LANES
    for t in (2048, 1024, 512, 256, 128, 64, 32, 16, 8):
        if rows % t == 0 and 2 * n_arrays * t * lanes * 4 <= VMEM_LIMIT // 2:
            return t
    return rows


def add_pair(kept, core, received, name):
    _, n, R, C = kept.shape
    tr = _row_tile(R, C, 3)

    def body(core_ref, a_ref, b_ref, o_ref):
        o_ref[...] = (a_ref[...].astype(F32) + b_ref[...].astype(F32)).astype(BF16)

    blk = pl.BlockSpec((None, tr, C), lambda d, i, core_ref: (d, i, 0))
    grid_spec = pltpu.PrefetchScalarGridSpec(
        num_scalar_prefetch=1, grid=(n, R // tr),
        in_specs=[pl.BlockSpec((None, None, tr, C), lambda d, i, core_ref: (core_ref[0], d, i, 0)), blk],
        out_specs=blk)
    return pl.pallas_call(body, name=name, grid_spec=grid_spec, out_shape=SDS((n, R, C), BF16),
                          compiler_params=_params("arbitrary", "arbitrary"))(core, kept, received)


def add_chips(parts, name):
    _, R, C = parts.shape
    tr = _row_tile(R, C, 4)

    def body(p_ref, o_ref):
        acc = p_ref[0].astype(F32)
        for j in range(1, N_CHIPS):
            acc = acc + p_ref[j].astype(F32)
        o_ref[...] = acc

    return pl.pallas_call(body, name=name, grid=(R // tr,),
                          in_specs=[pl.BlockSpec((N_CHIPS, tr, C), lambda i: (0, i, 0))],
                          out_specs=pl.BlockSpec((tr, C), lambda i: (i, 0)),
                          out_shape=SDS((R, C), F32), compiler_params=_params("arbitrary"))(parts)


def adamw(w, g, m, v, name):
    R, C = w.shape
    tr = _row_tile(R, C, 7)

    def body(w_ref, g_ref, m_ref, v_ref, d_ref, nm_ref, nv_ref):
        gv = g_ref[...]
        m_new = ADAM_B1 * m_ref[...] + (1.0 - ADAM_B1) * gv
        v_new = ADAM_B2 * v_ref[...] + (1.0 - ADAM_B2) * (gv * gv)
        m_hat = m_new / (1.0 - ADAM_B1 ** ADAM_STEP)
        v_hat = v_new / (1.0 - ADAM_B2 ** ADAM_STEP)
        d_ref[...] = -ADAM_LR * (m_hat / (jnp.sqrt(v_hat) + ADAM_EPS) + ADAM_WD * w_ref[...])
        nm_ref[...] = m_new
        nv_ref[...] = v_new

    blk = pl.BlockSpec((tr, C), lambda i: (i, 0))
    out = SDS((R, C), F32)
    return pl.pallas_call(body, name=name, grid=(R // tr,), in_specs=[blk] * 4, out_specs=[blk] * 3,
                          out_shape=[out, out, out], compiler_params=_params("arbitrary"))(w, g, m, v)


def _position():
    return lax.axis_index("x"), lax.axis_index("y"), lax.axis_index("c")


def _other_chips(x, y):
    return [(1 - x, y), (x, 1 - y), (1 - x, 1 - y)]


def _remote(src, dst, send_sem, recv_sem, device):
    return pltpu.make_async_remote_copy(src_ref=src, dst_ref=dst, send_sem=send_sem, recv_sem=recv_sem,
                                        device_id=device, device_id_type=MESH)


_COMM = pltpu.CompilerParams(has_side_effects=True)


def allgather_weights(shards, row_sharded):
    n = len(shards)

    def window(out_ref, t, chip, layer):
        return out_ref.at[layer, chip] if row_sharded[t] else out_ref.at[chip, layer]

    def body(*refs):
        srcs, outs = refs[:n], refs[n:2 * n]
        send_sems, recv_sems, fwd_send, fwd_recv, local_sems = refs[2 * n:]
        x, y, c = _position()
        me = 2 * x + y
        others = _other_chips(x, y)
        local = []
        for t in range(n):
            for layer in range(N_LAYERS):
                cp = pltpu.make_async_copy(srcs[t].at[layer], window(outs[t], t, me, layer), local_sems.at[t, layer])
                cp.start()
                local.append(cp)
        sends = []
        for t in range(n):
            for k, (px, py) in enumerate(others):
                cp = _remote(srcs[t].at[c], window(outs[t], t, me, c), send_sems.at[t, k], recv_sems.at[t, k],
                             (px, py, c))
                cp.start()
                sends.append(cp)
        for t in range(n):
            for k, (px, py) in enumerate(others):
                win = window(outs[t], t, 2 * px + py, c)
                _remote(srcs[t].at[c], win, send_sems.at[t, k], recv_sems.at[t, k], (px, py, c)).wait_recv()
                cp = _remote(win, win, fwd_send.at[t, k], fwd_recv.at[t, k], (x, y, 1 - c))
                cp.start()
                sends.append(cp)
        for t in range(n):
            for k, (px, py) in enumerate(others):
                win = window(outs[t], t, 2 * px + py, 1 - c)
                _remote(win, win, fwd_send.at[t, k], fwd_recv.at[t, k], (x, y, 1 - c)).wait_recv()
        for cp in sends:
            cp.wait_send()
        for cp in local:
            cp.wait()

    def out_shape(t):
        L, r, c = shards[t].shape
        return SDS((L, N_CHIPS, r, c) if row_sharded[t] else (N_CHIPS, L, r, c), shards[t].dtype)

    sem = pltpu.SemaphoreType.DMA
    return pl.pallas_call(
        body, name="allgather_weights", in_specs=[_ANY] * n, out_specs=[_ANY] * n,
        out_shape=[out_shape(t) for t in range(n)],
        scratch_shapes=[sem((n, 3)), sem((n, 3)), sem((n, 3)), sem((n, 3)), sem((n, N_LAYERS))],
        compiler_params=_COMM,
    )(*shards)


def allgather_chips(buf, name):
    def body(src_ref, out_ref, send_sems, recv_sems, local_sem):
        x, y, c = _position()
        me = 2 * x + y
        mine = pltpu.make_async_copy(src_ref, out_ref.at[me], local_sem)
        mine.start()
        sends = []
        for k, (px, py) in enumerate(_other_chips(x, y)):
            cp = _remote(src_ref, out_ref.at[me], send_sems.at[k], recv_sems.at[k], (px, py, c))
            cp.start()
            sends.append(cp)
        for k, (px, py) in enumerate(_other_chips(x, y)):
            _remote(src_ref, out_ref.at[2 * px + py], send_sems.at[k], recv_sems.at[k], (px, py, c)).wait_recv()
        for cp in sends:
            cp.wait_send()
        mine.wait()

    sem = pltpu.SemaphoreType.DMA
    return pl.pallas_call(
        body, name=name, in_specs=[_ANY], out_specs=_ANY, out_shape=SDS((N_CHIPS,) + buf.shape, buf.dtype),
        scratch_shapes=[sem((3,)), sem((3,)), sem], compiler_params=_COMM,
    )(buf)


def swap_sibling(tensors, name):
    n = len(tensors)

    def body(*refs):
        srcs, outs, send_sems, recv_sems = refs[:n], refs[n:2 * n], refs[2 * n], refs[2 * n + 1]
        x, y, c = _position()
        cps = [_remote(srcs[t].at[1 - c], outs[t], send_sems.at[t], recv_sems.at[t], (x, y, 1 - c))
               for t in range(n)]
        for cp in cps:
            cp.start()
        for cp in cps:
            cp.wait()

    sem = pltpu.SemaphoreType.DMA
    return pl.pallas_call(
        body, name=name, in_specs=[_ANY] * n, out_specs=[_ANY] * n,
        out_shape=[SDS(t.shape[1:], t.dtype) for t in tensors],
        scratch_shapes=[sem((n,)), sem((n,))], compiler_params=_COMM,
    )(*tensors)


def exchange_chips(tensors, name):
    n = len(tensors)

    def body(*refs):
        srcs, outs = refs[:n], refs[n:2 * n]
        send_sems, recv_sems, local_sems = refs[2 * n:]
        x, y, c = _position()
        me = 2 * x + y
        others = _other_chips(x, y)
        cps = []
        for t in range(n):
            cp = pltpu.make_async_copy(srcs[t].at[me], outs[t].at[me], local_sems.at[t])
            cp.start()
            cps.append(cp)
        sends = []
        for t in range(n):
            for k, (px, py) in enumerate(others):
                cp = _remote(srcs[t].at[2 * px + py], outs[t].at[me], send_sems.at[t, k], recv_sems.at[t, k],
                             (px, py, c))
                cp.start()
                sends.append(cp)
        for t in range(n):
            for k, (px, py) in enumerate(others):
                _remote(srcs[t].at[me], outs[t].at[2 * px + py], send_sems.at[t, k], recv_sems.at[t, k],
                        (px, py, c)).wait_recv()
        for cp in sends:
            cp.wait_send()
        for cp in cps:
            cp.wait()

    sem = pltpu.SemaphoreType.DMA
    return pl.pallas_call(
        body, name=name, in_specs=[_ANY] * n, out_specs=[_ANY] * n,
        out_shape=[SDS(t.shape, t.dtype) for t in tensors],
        scratch_shapes=[sem((n, 3)), sem((n, 3)), sem((n,))], compiler_params=_COMM,
    )(*tensors)


def join_halves(tensors, name):
    n = len(tensors)

    def body(*refs):
        srcs, outs = refs[:n], refs[n:2 * n]
        send_sems, recv_sems, local_sems = refs[2 * n:]
        x, y, c = _position()
        sib = (x, y, 1 - c)
        cps, sends = [], []
        for t in range(n):
            cp = pltpu.make_async_copy(srcs[t], outs[t].at[c], local_sems.at[t])
            cp.start()
            cps.append(cp)
            cp = _remote(srcs[t], outs[t].at[c], send_sems.at[t], recv_sems.at[t], sib)
            cp.start()
            sends.append(cp)
        for t in range(n):
            _remote(srcs[t], outs[t].at[1 - c], send_sems.at[t], recv_sems.at[t], sib).wait_recv()
        for cp in sends:
            cp.wait_send()
        for cp in cps:
            cp.wait()

    sem = pltpu.SemaphoreType.DMA
    return pl.pallas_call(
        body, name=name, in_specs=[_ANY] * n, out_specs=[_ANY] * n,
        out_shape=[SDS((2,) + t.shape, t.dtype) for t in tensors],
        scratch_shapes=[sem((n,)), sem((n,)), sem((n,))], compiler_params=_COMM,
    )(*tensors)


BIG = ("w_in", "w_proj_attn", "w_proj_pool", "w_proj_conv", "conv_w", "w_out", "w_gate_up", "w_down")
ROW_SHARDED = ("w_out", "w_down")
REPLICATED = ("attn_norm", "b_forget", "b_gate", "pool_w", "pool_scale", "ffn_norm", "final_norm")
ORDER = ("attn_norm", "w_in", "b_forget", "b_gate", "w_proj_attn", "pool_w", "pool_scale", "w_proj_pool",
         "conv_w", "w_proj_conv", "w_out", "ffn_norm", "w_gate_up", "w_down", "final_norm")


def _proj_layout(D):
    lay = {"g": 0, "q": 3 * D}
    lay["k"] = lay["q"] + BRANCH_W
    lay["f"] = lay["k"] + BRANCH_W
    lay["v"] = lay["f"] + F_PAD
    lay["pc"] = lay["v"] + BRANCH_W
    lay["width"] = lay["pc"] + 4 * BRANCH_W
    return lay


_REF = dict(q=0, k=512, v=1024, f=1536, u=1544, cv=2056, cb=2568, cc=3080, g=3592)


def _pack_w_in(w):
    cols = lambda a, n: w[..., a:a + n]
    groups = []
    for gi in range(N_GROUPS):
        groups += [cols(_REF[name] + gi * GROUP_W, GROUP_W) for name in ("u", "cv", "cb", "cc")]
    pad = jnp.zeros(w.shape[:-1] + (F_PAD - HEADS,), w.dtype)
    return jnp.concatenate([w[..., _REF["g"]:], cols(_REF["q"], 512), cols(_REF["k"], 512),
                            cols(_REF["f"], HEADS), pad, cols(_REF["v"], 512)] + groups, axis=-1)


def _unpack_w_in(p, D):
    lay = _proj_layout(D)
    cols = lambda a, n: p[..., a:a + n]
    kinds = []
    for kind in range(4):
        kinds += [cols(lay["pc"] + gi * BRANCH_W + kind * GROUP_W, GROUP_W) for gi in range(N_GROUPS)]
    return jnp.concatenate([cols(lay["q"], 512), cols(lay["k"], 512), cols(lay["v"], 512),
                            cols(lay["f"], HEADS)] + kinds + [cols(0, 3 * D)], axis=-1)


def _split_flat(vec, shapes):
    out, at = [], 0
    for shp in shapes:
        n = int(np.prod(shp))
        out.append(vec[at:at + n].reshape(shp))
        at += n
    return out


def kernel(x, attn_norm, w_in, b_forget, b_gate, w_proj_attn, pool_w, pool_scale, w_proj_pool, conv_w, w_proj_conv, w_out, ffn_norm, w_gate_up, w_down, final_norm, loss_target, m_attn_norm, m_w_in, m_b_forget, m_b_gate, m_w_proj_attn, m_pool_w, m_pool_scale, m_w_proj_pool, m_conv_w, m_w_proj_conv, m_w_out, m_ffn_norm, m_w_gate_up, m_w_down, m_final_norm, v_attn_norm, v_w_in, v_b_forget, v_b_gate, v_w_proj_attn, v_pool_w, v_pool_scale, v_w_proj_pool, v_conv_w, v_w_proj_conv, v_w_out, v_ffn_norm, v_w_gate_up, v_w_down, v_final_norm):
    weights = dict(attn_norm=attn_norm, w_in=w_in, b_forget=b_forget, b_gate=b_gate, w_proj_attn=w_proj_attn,
                   pool_w=pool_w, pool_scale=pool_scale, w_proj_pool=w_proj_pool, conv_w=conv_w,
                   w_proj_conv=w_proj_conv, w_out=w_out, ffn_norm=ffn_norm, w_gate_up=w_gate_up, w_down=w_down,
                   final_norm=final_norm)
    mom_m = dict(attn_norm=m_attn_norm, w_in=m_w_in, b_forget=m_b_forget, b_gate=m_b_gate, w_proj_attn=m_w_proj_attn,
                 pool_w=m_pool_w, pool_scale=m_pool_scale, w_proj_pool=m_w_proj_pool, conv_w=m_conv_w,
                 w_proj_conv=m_w_proj_conv, w_out=m_w_out, ffn_norm=m_ffn_norm, w_gate_up=m_w_gate_up,
                 w_down=m_w_down, final_norm=m_final_norm)
    mom_v = dict(attn_norm=v_attn_norm, w_in=v_w_in, b_forget=v_b_forget, b_gate=v_b_gate, w_proj_attn=v_w_proj_attn,
                 pool_w=v_pool_w, pool_scale=v_pool_scale, w_proj_pool=v_w_proj_pool, conv_w=v_conv_w,
                 w_proj_conv=v_w_proj_conv, w_out=v_w_out, ffn_norm=v_ffn_norm, w_gate_up=v_w_gate_up,
                 w_down=v_w_down, final_norm=v_final_norm)

    Bl, S, D = x.shape
    T = Bl * S
    L = w_in.shape[0]
    F = w_down.shape[1] * N_CHIPS
    lay = _proj_layout(D)
    cst = _placement_constants()
    assert L == N_LAYERS and S % ATTN_BLOCK == 0 and F % FFN_TILE == 0 and D % BRANCH_W == 0
    assert w_in.shape[2] * N_CHIPS == _REF["g"] + 3 * D and conv_w.shape[2] == GROUP_W

    cast = lambda n: weights[n] if n == "conv_w" else weights[n].astype(BF16)
    gathered = dict(zip(BIG, allgather_weights([cast(n) for n in BIG], [n in ROW_SHARDED for n in BIG])))
    w_in_cols = gathered["w_in"].transpose(1, 2, 0, 3).reshape(L, D, -1)
    w_in_p = _pack_w_in(w_in_cols)
    w_down_f = gathered["w_down"].reshape(L, F, D)
    w_gu, w_o, conv_w_g = gathered["w_gate_up"], gathered["w_out"], gathered["conv_w"]
    wpa, wpp, wpc = gathered["w_proj_attn"], gathered["w_proj_pool"], gathered["w_proj_conv"]
    pool_w_b = pool_w.astype(BF16)
    an3, fn3 = attn_norm.reshape(L, 1, D), ffn_norm.reshape(L, 1, D)
    bg3, ps3 = b_gate.reshape(L, 1, 3 * D), pool_scale.reshape(L, 1, BRANCH_W)
    bf3 = jnp.pad(b_forget, ((0, 0), (0, LANES - HEADS))).reshape(L, 1, LANES)

    xs = x.reshape(T, D)
    saved = []
    for l in range(L):
        proj, h = norm_matmul(xs, an3, w_in_p, l, False, "in_proj")
        proj3 = proj.reshape(Bl, S, lay["width"])
        qa, ka = attn_prep(proj3, bf3, l, cst, lay)
        ao, lse = attn_fwd(qa, ka, proj3, lay)
        po, co = poolconv_fwd(proj3, pool_w_b, ps3, conv_w_g, l, lay)
        ao2, po2, co2 = (a.reshape(T, BRANCH_W) for a in (ao, po, co))
        x1, ys, mixed = mix_fwd(ao2, po2, co2, proj, bg3, wpa, wpp, wpc, w_o, l, xs)
        ab, h2 = norm_matmul(x1, fn3, w_gu, l, True, "gate_up_proj")
        x2, s_act = ffn_down_fwd(ab, w_down_f, l, x1)
        saved.append(dict(x=xs, proj=proj, proj3=proj3, h=h, qa=qa, ka=ka, ao=ao, lse=lse, ao2=ao2, po2=po2,
                          co2=co2, ys=ys, mixed=mixed, x1=x1, ab=ab, h2=h2, s=s_act))
        xs = x2

    loss_row, dx, dxb, g_final = loss_head(xs, final_norm.reshape(1, D), loss_target.reshape(T, D))
    loss = lax.psum(loss_row[0, 0], AXES)

    stacked = {n: None for n in BIG}
    small = {n: [None] * L for n in REPLICATED if n != "final_norm"}
    to3 = lambda a: a.reshape(Bl, S, -1)
    for l in reversed(range(L)):
        sv = saved[l]
        da, db = ffn_down_bwd(dxb, w_down_f, l, sv["ab"])
        stacked["w_down"] = matmul_tn(sv["s"], [dxb], "grad_w_down", l, stacked["w_down"])
        stacked["w_gate_up"] = matmul_tn(sv["h2"], [da, db], "grad_w_gate_up", l, stacked["w_gate_up"],
                                         by_dest=True, tn=2 * F // N_CHIPS, tk=_tile(T, (1024, 512, 256)))
        dx1, dx1b, g_fn = matmul_nt_normbwd([da, db], w_gu, l, True, sv["x1"], fn3, dx, "gate_up_bwd")
        small["ffn_norm"][l] = g_fn[0]
        dys, dproj, dao, dpo, dco, g_bg = mix_bwd(dx1b, w_o, sv["proj"], bg3, sv["ys"], wpa, wpp, wpc, l,
                                                  lay["width"])
        small["b_gate"][l] = g_bg[0]
        stacked["w_out"] = matmul_tn(sv["mixed"], [dx1b], "grad_w_out", l, stacked["w_out"])
        for n, (name, br) in enumerate((("w_proj_attn", sv["ao2"]), ("w_proj_pool", sv["po2"]),
                                        ("w_proj_conv", sv["co2"]))):
            stacked[name] = matmul_tn(br, [dys], "grad_" + name, l, stacked[name], b_col0=n * D, n_cols=D,
                                      by_dest=True, tn=D // N_CHIPS)
        dqa, dka, dproj3 = attn_bwd(sv["qa"], sv["ka"], sv["proj3"], to3(dao), sv["ao"], sv["lse"], to3(dproj), lay)
        dproj3, g_bf = attn_post(dqa, dka, sv["proj3"], bf3, l, dproj3, cst, lay)
        small["b_forget"][l] = g_bf[0, :HEADS]
        dproj3, g_pw, g_ps, stacked["conv_w"] = poolconv_bwd(sv["proj3"], to3(dpo), to3(dco), pool_w_b, ps3,
                                                             conv_w_g, l, dproj3, stacked["conv_w"], lay)
        small["pool_w"][l], small["pool_scale"][l] = g_pw, g_ps[0]
        dproj = dproj3.reshape(T, lay["width"])
        stacked["w_in"] = matmul_tn(sv["h"], [dproj], "grad_w_in", l, stacked["w_in"])
        dx, dxb, g_an = matmul_nt_normbwd([dproj], w_in_p, l, False, sv["x"], an3, dx1, "in_proj_bwd")
        small["attn_norm"][l] = g_an[0]
    grad_x = dx.reshape(Bl, S, D)

    g_w_in = _unpack_w_in(stacked["w_in"], D)
    cs_in = g_w_in.shape[2] // N_CHIPS
    by_dest = dict(stacked)
    by_dest["w_in"] = g_w_in.reshape(L, D, N_CHIPS, cs_in).transpose(0, 2, 1, 3)
    by_dest["w_out"] = stacked["w_out"].reshape(L, N_CHIPS, D // N_CHIPS, D)
    by_dest["w_down"] = stacked["w_down"].reshape(L, N_CHIPS, F // N_CHIPS, D)
    small_shapes = [weights[n].shape for n in REPLICATED]
    small_vec = jnp.concatenate([jnp.stack(small[n]).reshape(-1) for n in REPLICATED[:-1]] + [g_final[0]])
    n_small = small_vec.shape[0]
    small_vec = jnp.pad(small_vec, (0, -n_small % (2 * N_CHIPS * 16 * LANES))).astype(BF16)
    names = BIG + ("small",)
    by_dest["small"] = small_vec.reshape(N_CHIPS, 2, -1, LANES).transpose(1, 0, 2, 3)
    core = lax.axis_index("c").astype(jnp.int32).reshape(1)
    received = swap_sibling([by_dest[n] for n in names], "swap_grad_layers")
    chip_sum = [add_pair(by_dest[n], core, r, "add_pair_" + n) for n, r in zip(names, received)]
    arrived = exchange_chips(chip_sum, "exchange_grad_chips")
    reduced = join_halves([add_chips(a, "add_chips_" + n) for n, a in zip(names, arrived)], "join_grad_layers")
    shard_grads = dict(zip(names, reduced))
    small_all = allgather_chips(shard_grads.pop("small"), "allgather_small_grads").reshape(-1)[:n_small]
    rep_grads = dict(zip(REPLICATED, _split_flat(small_all, small_shapes)))

    delta, new_m, new_v = {}, {}, {}
    for n in BIG:
        shp = weights[n].shape
        two_d = lambda a: a.reshape(-1, shp[-1])
        d, nm, nv = adamw(two_d(weights[n]), two_d(shard_grads[n]), two_d(mom_m[n]), two_d(mom_v[n]), "adamw_" + n)
        delta[n], new_m[n], new_v[n] = d.reshape(shp), nm.reshape(shp), nv.reshape(shp)
        shard_grads[n] = shard_grads[n].reshape(shp)

    def rows(d):
        vec = jnp.concatenate([d[n].reshape(-1) for n in REPLICATED])
        return jnp.pad(vec, (0, -n_small % (8 * LANES))).reshape(-1, LANES)

    outs = adamw(rows(weights), rows(rep_grads), rows(mom_m), rows(mom_v), "adamw_replicated")
    for res, o in zip((delta, new_m, new_v), outs):
        res.update(zip(REPLICATED, _split_flat(o.reshape(-1), small_shapes)))
    all_grads = {**shard_grads, **rep_grads}

    return (loss, grad_x, *[all_grads[n] for n in ORDER], *[delta[n] for n in ORDER],
            *[new_m[n] for n in ORDER], *[new_v[n] for n in ORDER])
```

```python
import numpy as np
import jax
import jax.numpy as jnp
from jax import lax
from jax.experimental import pallas as pl
from jax.experimental.pallas import tpu as pltpu

F32, BF16 = jnp.float32, jnp.bfloat16
SDS = jax.ShapeDtypeStruct
MESH = pl.DeviceIdType.MESH
AXES = ("x", "y", "c")
N_CHIPS = 4
N_LAYERS = 2
LANES = 128
VMEM_LIMIT = 48 * 1024 * 1024

HEADS, HEAD_DIM = 8, 64
HEAD_PAD = 128
BRANCH_W = 512
GROUP_W = 128
N_GROUPS = BRANCH_W // GROUP_W
POOL_WINDOWS = (2, 4, 8, 16)
F_PAD = 512
ATTN_BLOCK = 256
RMS_EPS = 1e-6
NEG_INF = -1e30
ADAM_LR, ADAM_B1, ADAM_B2, ADAM_EPS, ADAM_WD, ADAM_STEP = 0.001, 0.9, 0.999, 1e-08, 0.01, 10

NT = (((1,), (1,)), ((), ()))
TN = (((0,), (0,)), ((), ()))
_ANY = pl.BlockSpec(memory_space=pl.ANY)


def _tile(n, prefs):
    for p in prefs:
        if n % p == 0:
            return p
    raise ValueError(f"no tile of {prefs} divides {n}")


def _params(*sem):
    return pltpu.CompilerParams(dimension_semantics=sem, vmem_limit_bytes=VMEM_LIMIT)


def _sigmoid(z):
    return 1.0 / (1.0 + jnp.exp(-z))


def _split3(x):
    h1 = x.astype(BF16)
    r1 = x - h1.astype(F32)
    h2 = r1.astype(BF16)
    h3 = (r1 - h2.astype(F32)).astype(BF16)
    return h1, h2, h3


def _dot(a, b):
    return jnp.dot(a, b, preferred_element_type=F32)


def _dot_nt(a, b):
    return lax.dot_general(a, b, NT, preferred_element_type=F32)


def _dot_tn(a, b):
    return lax.dot_general(a, b, TN, preferred_element_type=F32)


def norm_matmul(x, gain, w, layer, by_shard, name):
    T, D = x.shape
    if by_shard:
        tn = w.shape[3]
        N = N_CHIPS * tn
        w_spec = pl.BlockSpec((None, None, D, tn), lambda i, j: (j, layer, 0, 0))
    else:
        N = w.shape[2]
        tn = _tile(N, (512, 256, 128))
        w_spec = pl.BlockSpec((None, D, tn), lambda i, j: (layer, 0, j))
    tm = _tile(T, (1024, 512, 256, 128))

    def body(x_ref, g_ref, w_ref, y_ref, h_ref):
        @pl.when(pl.program_id(1) == 0)
        def _():
            xf = x_ref[...]
            r = lax.rsqrt(jnp.mean(xf * xf, axis=-1, keepdims=True) + RMS_EPS)
            h_ref[...] = ((xf * r) * g_ref[...]).astype(BF16)

        y_ref[...] = _dot(h_ref[...], w_ref[...]).astype(BF16)

    return pl.pallas_call(
        body, name=name, grid=(T // tm, N // tn),
        in_specs=[pl.BlockSpec((tm, D), lambda i, j: (i, 0)),
                  pl.BlockSpec((None, 1, D), lambda i, j: (layer, 0, 0)),
                  w_spec],
        out_specs=[pl.BlockSpec((tm, tn), lambda i, j: (i, j)),
                   pl.BlockSpec((tm, D), lambda i, j: (i, 0))],
        out_shape=[SDS((T, N), BF16), SDS((T, D), BF16)],
        compiler_params=_params("arbitrary", "arbitrary"),
    )(x, gain, w)


def matmul_nt_normbwd(dys, w, layer, by_shard, x, gain, dres, name):
    T, D = x.shape
    width = dys[0].shape[1]
    if by_shard:
        tk = w.shape[3]
        w_spec = pl.BlockSpec((None, None, D, tk), lambda i, k: (k, layer, 0, 0))
    else:
        tk = _tile(width, (1024, 512, 256, 128))
        w_spec = pl.BlockSpec((None, D, tk), lambda i, k: (layer, 0, k))
    per = width // tk
    nk = per * len(dys)
    tm = _tile(T, (512, 256, 128))
    n_dy = len(dys)

    def dy_spec(p):
        return pl.BlockSpec((tm, tk), lambda i, k: (i, jnp.clip(k - p * per, 0, per - 1)))

    def body(*refs):
        dy_refs = refs[:n_dy]
        w_ref, x_ref, g_ref, dres_ref, dx_ref, dxb_ref, dg_ref, acc_ref = refs[n_dy:]
        i, k = pl.program_id(0), pl.program_id(1)

        @pl.when(k == 0)
        def _():
            acc_ref[...] = jnp.zeros_like(acc_ref)

        for p in range(n_dy):
            @pl.when((k >= p * per) & (k < (p + 1) * per))
            def _(p=p):
                acc_ref[...] += _dot_nt(dy_refs[p][...], w_ref[...])

        @pl.when(k == nk - 1)
        def _():
            xf = x_ref[...]
            r = lax.rsqrt(jnp.mean(xf * xf, axis=-1, keepdims=True) + RMS_EPS)
            xhat = xf * r
            dh = acc_ref[...]
            dhg = dh * g_ref[...]
            dx = dres_ref[...] + r * (dhg - xhat * jnp.mean(dhg * xhat, axis=-1, keepdims=True))
            dx_ref[...] = dx
            dxb_ref[...] = dx.astype(BF16)
            part = jnp.sum(dh * xhat, axis=0, keepdims=True)

            @pl.when(i == 0)
            def _():
                dg_ref[...] = part

            @pl.when(i > 0)
            def _():
                dg_ref[...] += part

    row = pl.BlockSpec((tm, D), lambda i, k: (i, 0))
    return pl.pallas_call(
        body, name=name, grid=(T // tm, nk),
        in_specs=[dy_spec(p) for p in range(n_dy)] + [
            w_spec, row, pl.BlockSpec((None, 1, D), lambda i, k: (layer, 0, 0)), row],
        out_specs=[row, row, pl.BlockSpec((1, D), lambda i, k: (0, 0))],
        out_shape=[SDS((T, D), F32), SDS((T, D), BF16), SDS((1, D), F32)],
        scratch_shapes=[pltpu.VMEM((tm, D), F32)],
        compiler_params=_params("arbitrary", "arbitrary"),
    )(*dys, w, x, gain, dres)


def matmul_tn(a, bs, name, layer, stacked, b_col0=0, n_cols=None, by_dest=False, tn=None, tk=None):
    T, M = a.shape
    width = bs[0].shape[1]
    N = n_cols if n_cols else width * len(bs)
    tm = _tile(M, (1024, 512, 256, 128))
    tn = tn or _tile(N, (512, 256, 128))
    tk = tk or _tile(T, (4096, 2048, 1024, 512, 256))
    assert b_col0 % tn == 0 and width % tn == 0
    j0, per, nk, n_b = b_col0 // tn, width // tn, T // tk, len(bs)

    def b_spec(p):
        return pl.BlockSpec((tk, tn), lambda i, j, k: (k, jnp.clip(j0 + j - p * per, 0, per - 1)))

    def body(*refs):
        a_ref, b_refs = refs[0], refs[1:1 + n_b]
        o_ref, acc_ref = refs[-2], refs[-1]
        j, k = pl.program_id(1), pl.program_id(2)

        @pl.when(k == 0)
        def _():
            acc_ref[...] = jnp.zeros_like(acc_ref)

        for p in range(n_b):
            @pl.when((j0 + j >= p * per) & (j0 + j < (p + 1) * per))
            def _(p=p):
                acc_ref[...] += _dot_tn(a_ref[...], b_refs[p][...])

        @pl.when(k == nk - 1)
        def _():
            o_ref[...] = acc_ref[...].astype(BF16)

    if by_dest:
        cs = N // N_CHIPS
        npd = cs // tn
        out_shape = SDS((N_LAYERS, N_CHIPS, M, cs), BF16)
        out_spec = pl.BlockSpec((None, None, tm, tn), lambda i, j, k: (layer, j // npd, i, j % npd))
    else:
        out_shape = SDS((N_LAYERS, M, N), BF16)
        out_spec = pl.BlockSpec((None, tm, tn), lambda i, j, k: (layer, i, j))
    ins = [a] + list(bs)
    in_specs = [pl.BlockSpec((tk, tm), lambda i, j, k: (k, i))] + [b_spec(p) for p in range(n_b)]
    aliases = {}
    if stacked is not None:
        ins.append(stacked)
        in_specs.append(_ANY)
        aliases = {len(ins) - 1: 0}

    def body_wrap(*refs):
        if stacked is not None:
            refs = refs[:1 + n_b] + refs[2 + n_b:]
        body(*refs)

    return pl.pallas_call(
        body_wrap, name=name, grid=(M // tm, N // tn, nk),
        in_specs=in_specs, out_specs=out_spec, out_shape=out_shape,
        scratch_shapes=[pltpu.VMEM((tm, tn), F32)], input_output_aliases=aliases,
        compiler_params=_params("arbitrary", "arbitrary", "arbitrary"),
    )(*ins)


def ffn_down_fwd(ab, w_down, layer, x1):
    T, D = x1.shape
    F = w_down.shape[1]
    tm = _tile(T, (512, 256, 128))
    tk = F // 2
    nk = F // tk

    def body(a_ref, b_ref, w_ref, x_ref, x2_ref, s_ref, acc_ref):
        k = pl.program_id(1)

        @pl.when(k == 0)
        def _():
            acc_ref[...] = x_ref[...]

        a = a_ref[...].astype(F32)
        s = (a * _sigmoid(a) * b_ref[...].astype(F32)).astype(BF16)
        s_ref[...] = s
        acc_ref[...] += _dot(s, w_ref[...])

        @pl.when(k == nk - 1)
        def _():
            x2_ref[...] = acc_ref[...]

    return pl.pallas_call(
        body, name="ffn_down_fwd", grid=(T // tm, nk),
        in_specs=[pl.BlockSpec((tm, tk), lambda i, k: (i, k)),
                  pl.BlockSpec((tm, tk), lambda i, k: (i, nk + k)),
                  pl.BlockSpec((None, tk, D), lambda i, k: (layer, k, 0)),
                  pl.BlockSpec((tm, D), lambda i, k: (i, 0))],
        out_specs=[pl.BlockSpec((tm, D), lambda i, k: (i, 0)),
                   pl.BlockSpec((tm, tk), lambda i, k: (i, k))],
        out_shape=[SDS((T, D), F32), SDS((T, F), BF16)],
        scratch_shapes=[pltpu.VMEM((tm, D), F32)],
        compiler_params=_params("arbitrary", "arbitrary"),
    )(ab, ab, w_down, x1)


def ffn_down_bwd(dx2b, w_down, layer, ab):
    T, D = dx2b.shape
    F = w_down.shape[1]
    tm = _tile(T, (512, 256, 128))
    tn = F // 2
    nj = F // tn

    def body(dx_ref, w_ref, a_ref, b_ref, da_ref, db_ref):
        ds = _dot_nt(dx_ref[...], w_ref[...])
        a = a_ref[...].astype(F32)
        sg = _sigmoid(a)
        da_ref[...] = (ds * b_ref[...].astype(F32) * (sg * (1.0 + a * (1.0 - sg)))).astype(BF16)
        db_ref[...] = (ds * (a * sg)).astype(BF16)

    blk = pl.BlockSpec((tm, tn), lambda i, j: (i, j))
    return pl.pallas_call(
        body, name="ffn_down_bwd", grid=(T // tm, nj),
        in_specs=[pl.BlockSpec((tm, D), lambda i, j: (i, 0)),
                  pl.BlockSpec((None, tn, D), lambda i, j: (layer, j, 0)),
                  blk, pl.BlockSpec((tm, tn), lambda i, j: (i, nj + j))],
        out_specs=[blk, blk],
        out_shape=[SDS((T, F), BF16), SDS((T, F), BF16)],
        compiler_params=_params("arbitrary", "arbitrary"),
    )(dx2b, w_down, ab, ab)


def _mix_specs(tm, D, layer):
    cs = D // N_CHIPS
    row = lambda w: pl.BlockSpec((tm, w), lambda i: (i, 0))
    wp = pl.BlockSpec((N_CHIPS, None, BRANCH_W, cs), lambda i: (0, layer, 0, 0))
    wo = pl.BlockSpec((None, N_CHIPS, cs, D), lambda i: (layer, 0, 0, 0))
    bg = pl.BlockSpec((None, 1, 3 * D), lambda i: (layer, 0, 0))
    return row, wp, wo, bg


def mix_fwd(ao, po, co, proj, b_gate, wpa, wpp, wpc, w_out, layer, x):
    T, D = x.shape
    cs = D // N_CHIPS
    tm = _tile(T, (256, 128))
    row, wp, wo, bg = _mix_specs(tm, D, layer)

    def body(ao_ref, po_ref, co_ref, g_ref, bg_ref, wpa_ref, wpp_ref, wpc_ref, wo_ref, x_ref,
             x1_ref, ys_ref, mixed_ref):
        mixed = jnp.zeros((tm, D), F32)
        for n, (br, wp_ref) in enumerate(((ao_ref, wpa_ref), (po_ref, wpp_ref), (co_ref, wpc_ref))):
            y = jnp.concatenate([_dot(br[...], wp_ref[j]) for j in range(N_CHIPS)], axis=1)
            cols = slice(n * D, (n + 1) * D)
            gate = _sigmoid(g_ref[:, cols].astype(F32) + bg_ref[:, cols])
            ys_ref[:, cols] = y.astype(BF16)
            mixed = mixed + gate * y
        mb = mixed.astype(BF16)
        mixed_ref[...] = mb
        acc = x_ref[...]
        for j in range(N_CHIPS):
            acc = acc + _dot(mb[:, j * cs:(j + 1) * cs], wo_ref[j])
        x1_ref[...] = acc

    return pl.pallas_call(
        body, name="mix_fwd", grid=(T // tm,),
        in_specs=[row(BRANCH_W), row(BRANCH_W), row(BRANCH_W), row(3 * D), bg, wp, wp, wp, wo, row(D)],
        out_specs=[row(D), row(3 * D), row(D)],
        out_shape=[SDS((T, D), F32), SDS((T, 3 * D), BF16), SDS((T, D), BF16)],
        compiler_params=_params("arbitrary"),
    )(ao, po, co, proj, b_gate, wpa, wpp, wpc, w_out, x)


def mix_bwd(dx1b, w_out, proj, b_gate, ys, wpa, wpp, wpc, layer, width):
    T, D = dx1b.shape
    cs = D // N_CHIPS
    tm = _tile(T, (256, 128))
    row, wp, wo, bg = _mix_specs(tm, D, layer)

    def body(dx_ref, wo_ref, g_ref, bg_ref, ys_ref, wpa_ref, wpp_ref, wpc_ref,
             dys_ref, dg_ref, dao_ref, dpo_ref, dco_ref, dbg_ref):
        i = pl.program_id(0)
        dx = dx_ref[...]
        dmixed = jnp.concatenate([_dot_nt(dx, wo_ref[j]) for j in range(N_CHIPS)], axis=1)
        for n, (wp_ref, dbr) in enumerate(((wpa_ref, dao_ref), (wpp_ref, dpo_ref), (wpc_ref, dco_ref))):
            cols = slice(n * D, (n + 1) * D)
            gate = _sigmoid(g_ref[:, cols].astype(F32) + bg_ref[:, cols])
            dy = (dmixed * gate).astype(BF16)
            dys_ref[:, cols] = dy
            dgp = dmixed * ys_ref[:, cols].astype(F32) * gate * (1.0 - gate)
            dg_ref[:, cols] = dgp.astype(BF16)
            part = jnp.sum(dgp, axis=0, keepdims=True)

            @pl.when(i == 0)
            def _():
                dbg_ref[:, cols] = part

            @pl.when(i > 0)
            def _():
                dbg_ref[:, cols] += part

            acc = jnp.zeros((tm, BRANCH_W), F32)
            for j in range(N_CHIPS):
                acc = acc + _dot_nt(dy[:, j * cs:(j + 1) * cs], wp_ref[j])
            dbr[...] = acc.astype(BF16)

    return pl.pallas_call(
        body, name="mix_bwd", grid=(T // tm,),
        in_specs=[row(D), wo, row(3 * D), bg, row(3 * D), wp, wp, wp],
        out_specs=[row(3 * D), row(3 * D), row(BRANCH_W), row(BRANCH_W), row(BRANCH_W),
                   pl.BlockSpec((1, 3 * D), lambda i: (0, 0))],
        out_shape=[SDS((T, 3 * D), BF16), SDS((T, width), BF16), SDS((T, BRANCH_W), BF16),
                   SDS((T, BRANCH_W), BF16), SDS((T, BRANCH_W), BF16), SDS((1, 3 * D), F32)],
        compiler_params=_params("arbitrary"),
    )(dx1b, w_out, proj, b_gate, ys, wpa, wpp, wpc)


def loss_head(x2, gain, target):
    T, D = x2.shape
    tm = _tile(T, (512, 256, 128))

    def body(x_ref, g_ref, t_ref, loss_ref, dx_ref, dxb_ref, dg_ref):
        i = pl.program_id(0)
        xf = x_ref[...]
        g = g_ref[...]
        r = lax.rsqrt(jnp.mean(xf * xf, axis=-1, keepdims=True) + RMS_EPS)
        xhat = xf * r
        diff = xhat * g - t_ref[...]
        part_loss = 0.5 * jnp.sum(jnp.mean(diff * diff, axis=-1, keepdims=True), axis=0, keepdims=True)
        dy = diff * (1.0 / D)
        dhg = dy * g
        dx = r * (dhg - xhat * jnp.mean(dhg * xhat, axis=-1, keepdims=True))
        dx_ref[...] = dx
        dxb_ref[...] = dx.astype(BF16)
        part_g = jnp.sum(dy * xhat, axis=0, keepdims=True)
        part_l = jnp.broadcast_to(part_loss, (1, LANES))

        @pl.when(i == 0)
        def _():
            dg_ref[...] = part_g
            loss_ref[...] = part_l

        @pl.when(i > 0)
        def _():
            dg_ref[...] += part_g
            loss_ref[...] += part_l

    row = pl.BlockSpec((tm, D), lambda i: (i, 0))
    return pl.pallas_call(
        body, name="loss_head", grid=(T // tm,),
        in_specs=[row, pl.BlockSpec((1, D), lambda i: (0, 0)), row],
        out_specs=[pl.BlockSpec((1, LANES), lambda i: (0, 0)), row, row, pl.BlockSpec((1, D), lambda i: (0, 0))],
        out_shape=[SDS((1, LANES), F32), SDS((T, D), F32), SDS((T, D), BF16), SDS((1, D), F32)],
        compiler_params=_params("arbitrary"),
    )(x2, gain, target)


def _placement_constants():
    w = HEADS * HEAD_PAD
    pq = np.zeros((BRANCH_W, w), np.float32)
    pk = np.zeros((BRANCH_W, w), np.float32)
    pfq = np.zeros((3, LANES, w), np.float32)
    pfk = np.zeros((3, LANES, w), np.float32)
    cq = np.zeros((1, w), np.float32)
    ck = np.zeros((1, w), np.float32)
    eq = np.zeros((w, LANES), np.float32)
    ek = np.zeros((w, LANES), np.float32)
    for h in range(HEADS):
        for d in range(HEAD_DIM):
            pq[h * HEAD_DIM + d, h * HEAD_PAD + d] = HEAD_DIM ** -0.5
            pk[h * HEAD_DIM + d, h * HEAD_PAD + d] = 1.0
        for i in range(3):
            pfq[i, h, h * HEAD_PAD + HEAD_DIM + i] = 1.0
            pfk[i, h, h * HEAD_PAD + HEAD_DIM + 3 + i] = -1.0
            cq[0, h * HEAD_PAD + HEAD_DIM + 3 + i] = 1.0
            ck[0, h * HEAD_PAD + HEAD_DIM + i] = 1.0
        eq[h * HEAD_PAD + HEAD_DIM, h] = 1.0
        ek[h * HEAD_PAD + HEAD_DIM + 3, h] = -1.0
    bf = lambda a: jnp.asarray(a, BF16)
    return dict(pq=bf(pq), pk=bf(pk), pfq=bf(pfq), pfk=bf(pfk), cq=jnp.asarray(cq), ck=jnp.asarray(ck),
                pqkt=bf(np.concatenate([pq.T, pk.T], axis=0)), eq=bf(eq), ek=bf(ek))


def attn_prep(proj3, bf_rows, layer, cst, lay):
    Bl, S, _ = proj3.shape
    ts = ATTN_BLOCK
    w = HEADS * HEAD_PAD

    def body(q_ref, k_ref, f_ref, bf_ref, pq_ref, pk_ref, pfq_ref, pfk_ref, cq_ref, ck_ref,
             qa_ref, ka_ref, carry_ref):
        @pl.when(pl.program_id(1) == 0)
        def _():
            carry_ref[...] = jnp.zeros_like(carry_ref)

        z = f_ref[...].astype(F32) + bf_ref[...]
        logf = jnp.minimum(z, 0.0) - jnp.log(1.0 + jnp.exp(-jnp.abs(z)))
        r = lax.broadcasted_iota(jnp.int32, (ts, ts), 0)
        c = lax.broadcasted_iota(jnp.int32, (ts, ts), 1)
        tri = jnp.where(r >= c, 1.0, 0.0).astype(BF16)
        fcum = carry_ref[...]
        for part in _split3(logf):
            fcum = fcum + _dot(tri, part)
        carry_ref[...] = fcum[ts - 1:ts, :]
        qa = _dot(q_ref[...], pq_ref[...]) + cq_ref[...]
        ka = _dot(k_ref[...], pk_ref[...]) + ck_ref[...]
        for i, part in enumerate(_split3(fcum)):
            qa = qa + _dot(part, pfq_ref[i])
            ka = ka + _dot(part, pfk_ref[i])
        qa_ref[...] = qa.astype(BF16)
        ka_ref[...] = ka.astype(BF16)

    cfull = lambda shape: pl.BlockSpec(shape, lambda b, s: (0,) * len(shape))
    return pl.pallas_call(
        body, name="attn_prep", grid=(Bl, S // ts),
        in_specs=[pl.BlockSpec((None, ts, BRANCH_W), lambda b, s: (b, s, lay["q"] // BRANCH_W)),
                  pl.BlockSpec((None, ts, BRANCH_W), lambda b, s: (b, s, lay["k"] // BRANCH_W)),
                  pl.BlockSpec((None, ts, LANES), lambda b, s: (b, s, lay["f"] // LANES)),
                  pl.BlockSpec((None, 1, LANES), lambda b, s: (layer, 0, 0)),
                  cfull((BRANCH_W, w)), cfull((BRANCH_W, w)),
                  cfull((3, LANES, w)), cfull((3, LANES, w)), cfull((1, w)), cfull((1, w))],
        out_specs=[pl.BlockSpec((None, ts, w), lambda b, s: (b, s, 0)),
                   pl.BlockSpec((None, ts, w), lambda b, s: (b, s, 0))],
        out_shape=[SDS((Bl, S, w), BF16), SDS((Bl, S, w), BF16)],
        scratch_shapes=[pltpu.VMEM((1, LANES), F32)],
        compiler_params=_params("arbitrary", "arbitrary"),
    )(proj3, proj3, proj3, bf_rows, cst["pq"], cst["pk"], cst["pfq"], cst["pfk"], cst["cq"], cst["ck"])


def attn_fwd(qa, ka, proj3, lay):
    Bl, S, _ = qa.shape
    tq = ATTN_BLOCK
    nq = S // tq
    pairs = HEADS // 2
    pw = 2 * HEAD_PAD
    vw = 2 * HEAD_DIM

    def body(qa_ref, ka_ref, v_ref, o_ref, lse_ref):
        row = lax.broadcasted_iota(jnp.int32, (tq, tq), 0)
        col = lax.broadcasted_iota(jnp.int32, (tq, tq), 1)
        causal = row <= col
        for i in range(nq):
            nk = (i + 1) * tq
            rows = slice(i * tq, nk)
            o_t = []
            for h in range(2):
                hs = slice(h * HEAD_PAD, (h + 1) * HEAD_PAD)
                st = _dot_nt(ka_ref[0:nk, hs], qa_ref[rows, hs])
                diag = jnp.where(causal, st[nk - tq:], NEG_INF)
                m = jnp.max(diag, axis=0, keepdims=True)
                if i:
                    m = jnp.maximum(m, jnp.max(st[:nk - tq], axis=0, keepdims=True))
                p_diag = jnp.exp(diag - m)
                l = jnp.sum(p_diag, axis=0, keepdims=True)
                if i:
                    p_top = jnp.exp(st[:nk - tq] - m)
                    l = l + jnp.sum(p_top, axis=0, keepdims=True)
                    p = jnp.concatenate([p_top.astype(BF16), p_diag.astype(BF16)], axis=0)
                else:
                    p = p_diag.astype(BF16)
                acc = _dot_tn(v_ref[0:nk, :], p)
                o_t.append(acc[h * HEAD_DIM:(h + 1) * HEAD_DIM, :] / l)
                lse_ref[h:h + 1, rows] = m + jnp.log(l)
            o_ref[rows, :] = jnp.concatenate(o_t, axis=0).T.astype(BF16)

    return pl.pallas_call(
        body, name="attn_fwd", grid=(Bl, pairs),
        in_specs=[pl.BlockSpec((None, S, pw), lambda b, p: (b, 0, p)),
                  pl.BlockSpec((None, S, pw), lambda b, p: (b, 0, p)),
                  pl.BlockSpec((None, S, vw), lambda b, p: (b, 0, lay["v"] // vw + p))],
        out_specs=[pl.BlockSpec((None, S, vw), lambda b, p: (b, 0, p)),
                   pl.BlockSpec((None, None, 2, S), lambda b, p: (b, p, 0, 0))],
        out_shape=[SDS((Bl, S, BRANCH_W), BF16), SDS((Bl, pairs, 2, S), F32)],
        compiler_params=_params("arbitrary", "arbitrary"),
    )(qa, ka, proj3)


def attn_bwd(qa, ka, proj3, dao, ao, lse, dproj3, lay):
    Bl, S, _ = qa.shape
    tk = ATTN_BLOCK
    nq = S // tk
    pairs = HEADS // 2
    pw = 2 * HEAD_PAD
    vw = 2 * HEAD_DIM

    def body(qa_ref, ka_ref, v_ref, do_ref, o_ref, lse_ref, _, dqa_ref, dka_ref, dv_ref):
        row = lax.broadcasted_iota(jnp.int32, (tk, tk), 0)
        col = lax.broadcasted_iota(jnp.int32, (tk, tk), 1)
        causal = row <= col
        lane8 = lax.broadcasted_iota(jnp.int32, (8, vw), 1)
        lane_s = lax.broadcasted_iota(jnp.int32, (S, vw), 1)
        lane_k = lax.broadcasted_iota(jnp.int32, (tk, vw), 1)
        doo = do_ref[...].astype(F32) * o_ref[...].astype(F32)
        hi = doo.astype(BF16)
        lo = (doo - hi.astype(F32)).astype(BF16)
        delta, v_head = [], []
        for h in range(2):
            sel = jnp.where((lane8 >= h * HEAD_DIM) & (lane8 < (h + 1) * HEAD_DIM), 1.0, 0.0).astype(BF16)
            delta.append((_dot_nt(sel, hi) + _dot_nt(sel, lo))[0:1, :])
            in_head = (lane_s >= h * HEAD_DIM) & (lane_s < (h + 1) * HEAD_DIM)
            v_head.append(jnp.where(in_head, v_ref[...], jnp.zeros_like(v_ref[...])))
        dqa_ref[...] = jnp.zeros_like(dqa_ref)
        for j in range(nq):
            q0 = j * tk
            krows = slice(q0, q0 + tk)
            do = do_ref[q0:, :]
            dvs = []
            for h in range(2):
                hs = slice(h * HEAD_PAD, (h + 1) * HEAD_PAD)
                k = ka_ref[krows, hs]
                q = qa_ref[q0:, hs]
                st = _dot_nt(k, q)
                p = jnp.exp(st - lse_ref[h:h + 1, q0:])
                p_diag = jnp.where(causal, p[:, :tk], 0.0)
                p = jnp.concatenate([p_diag, p[:, tk:]], axis=1) if j < nq - 1 else p_diag
                dvs.append(_dot(p.astype(BF16), do))
                dpt = _dot_nt(v_head[h][krows, :], do)
                ds = (p * (dpt - delta[h][:, q0:])).astype(BF16)
                dka_ref[krows, hs] = _dot(ds, q)
                dqa_ref[q0:, hs] += _dot_tn(ds, k)
            dv_ref[krows, :] = jnp.where(lane_k < HEAD_DIM, dvs[0], dvs[1]).astype(BF16)

    seq = lambda w, c0=0: pl.BlockSpec((None, S, w), lambda b, p: (b, 0, c0 + p))
    return pl.pallas_call(
        body, name="attn_bwd", grid=(Bl, pairs),
        in_specs=[seq(pw), seq(pw), seq(vw, lay["v"] // vw), seq(vw), seq(vw),
                  pl.BlockSpec((None, None, 2, S), lambda b, p: (b, p, 0, 0)), _ANY],
        out_specs=[seq(pw), seq(pw), seq(vw, lay["v"] // vw)],
        out_shape=[SDS((Bl, S, HEADS * HEAD_PAD), F32), SDS((Bl, S, HEADS * HEAD_PAD), F32),
                   SDS(dproj3.shape, BF16)],
        input_output_aliases={6: 2},
        compiler_params=_params("arbitrary", "arbitrary"),
    )(qa, ka, proj3, dao, ao, lse, dproj3)


def attn_post(dqa, dka, proj3, bf_rows, layer, dproj3, cst, lay):
    Bl, S, w = dqa.shape
    ts = ATTN_BLOCK
    ns = S // ts
    qkf = 2 * BRANCH_W + F_PAD

    def body(dqa_ref, dka_ref, f_ref, bf_ref, pqkt_ref, eq_ref, ek_ref, _, dqkf_ref, dbf_ref, carry_ref):
        b, s = pl.program_id(0), pl.program_id(1)

        @pl.when(s == 0)
        def _():
            carry_ref[...] = jnp.zeros_like(carry_ref)

        dqa_v, dka_v = dqa_ref[...], dka_ref[...]
        qh = dqa_v.astype(BF16)
        kh = dka_v.astype(BF16)
        dqkf_ref[:, :BRANCH_W] = _dot(qh, pqkt_ref[:w, :]).astype(BF16)
        dqkf_ref[:, BRANCH_W:2 * BRANCH_W] = _dot(kh, pqkt_ref[w:, :]).astype(BF16)
        ql = (dqa_v - qh.astype(F32)).astype(BF16)
        kl = (dka_v - kh.astype(F32)).astype(BF16)
        d_f = (_dot(qh, eq_ref[...]) + _dot(ql, eq_ref[...])) + (_dot(kh, ek_ref[...]) + _dot(kl, ek_ref[...]))
        r = lax.broadcasted_iota(jnp.int32, (ts, ts), 0)
        c = lax.broadcasted_iota(jnp.int32, (ts, ts), 1)
        triu = jnp.where(c >= r, 1.0, 0.0).astype(BF16)
        rev = carry_ref[...]
        for part in _split3(d_f):
            rev = rev + _dot(triu, part)
        carry_ref[...] = rev[0:1, :]
        z = f_ref[...].astype(F32) + bf_ref[...]
        lane = lax.broadcasted_iota(jnp.int32, (ts, LANES), 1)
        dfl = jnp.where(lane < HEADS, rev / (1.0 + jnp.exp(z)), 0.0)
        dqkf_ref[:, 2 * BRANCH_W:] = jnp.concatenate(
            [dfl.astype(BF16), jnp.zeros((ts, F_PAD - LANES), BF16)], axis=1)
        part = jnp.sum(dfl, axis=0, keepdims=True)

        @pl.when((b == 0) & (s == 0))
        def _():
            dbf_ref[...] = part

        @pl.when((b > 0) | (s > 0))
        def _():
            dbf_ref[...] += part

    assert lay["q"] % qkf == 0
    cfull = lambda shape: pl.BlockSpec(shape, lambda b, s: (0,) * len(shape))
    rev_blk = lambda wd, c0=0: pl.BlockSpec((None, ts, wd), lambda b, s: (b, ns - 1 - s, c0))
    return pl.pallas_call(
        body, name="attn_post", grid=(Bl, ns),
        in_specs=[rev_blk(w), rev_blk(w), rev_blk(LANES, lay["f"] // LANES),
                  pl.BlockSpec((None, 1, LANES), lambda b, s: (layer, 0, 0)),
                  cfull((2 * w, BRANCH_W)), cfull((w, LANES)), cfull((w, LANES)), _ANY],
        out_specs=[rev_blk(qkf, lay["q"] // qkf), cfull((1, LANES))],
        out_shape=[SDS(dproj3.shape, BF16), SDS((1, LANES), F32)],
        scratch_shapes=[pltpu.VMEM((1, LANES), F32)],
        input_output_aliases={7: 0},
        compiler_params=_params("arbitrary", "arbitrary"),
    )(dqa, dka, proj3, bf_rows, cst["pqkt"], cst["eq"], cst["ek"], dproj3)


def _shift_down(x, k, row):
    return jnp.where(row >= k, pltpu.roll(x, k, axis=0), 0.0)


def _shift_up(x, k, row):
    n = x.shape[0]
    return jnp.where(row < n - k, pltpu.roll(x, n - k, axis=0), 0.0)


def _window_sum(x, g, row, shift):
    s2 = x + shift(x, 1, row)
    s4 = s2 + shift(s2, 2, row)
    s8 = s4 + shift(s4, 4, row)
    s16 = s8 + shift(s8, 8, row)
    return jnp.where(g == 0, s2, jnp.where(g == 1, s4, jnp.where(g == 2, s8, s16)))


def _window_count(g, row):
    wnd = jnp.where(g == 0, 2, jnp.where(g == 1, 4, jnp.where(g == 2, 8, 16)))
    return jnp.minimum(row + 1, wnd).astype(F32)


def _group_columns(ref):
    return [ref[:, n * GROUP_W:(n + 1) * GROUP_W].astype(F32) for n in range(4)]


def poolconv_fwd(proj3, pool_w, pool_scale, conv_w, layer, lay):
    Bl, S, _ = proj3.shape

    def body(x_ref, pw_ref, ps_ref, cw_ref, po_ref, co_ref):
        g = pl.program_id(1)
        row = lax.broadcasted_iota(jnp.int32, (S, GROUP_W), 0)
        u, cv, cb, cc = _group_columns(x_ref)
        d = _window_sum(u, g, row, _shift_down) / _window_count(g, row) - u
        po_ref[...] = (_dot(d.astype(BF16), pw_ref[...]) * ps_ref[...]).astype(BF16)
        z = cc * cv
        y = cw_ref[0:1, :] * _shift_down(z, 2, row) + cw_ref[1:2, :] * _shift_down(z, 1, row) + cw_ref[2:3, :] * z
        co_ref[...] = (cb * y).astype(BF16)

    out = pl.BlockSpec((None, S, GROUP_W), lambda b, g: (b, 0, g))
    return pl.pallas_call(
        body, name="poolconv_fwd", grid=(Bl, N_GROUPS),
        in_specs=[pl.BlockSpec((None, S, BRANCH_W), lambda b, g: (b, 0, lay["pc"] // BRANCH_W + g)),
                  pl.BlockSpec((None, None, GROUP_W, GROUP_W), lambda b, g: (layer, g, 0, 0)),
                  pl.BlockSpec((None, 1, GROUP_W), lambda b, g: (layer, 0, g)),
                  pl.BlockSpec((None, None, 3, GROUP_W), lambda b, g: (g, layer, 0, 0))],
        out_specs=[out, out],
        out_shape=[SDS((Bl, S, BRANCH_W), BF16), SDS((Bl, S, BRANCH_W), BF16)],
        compiler_params=_params("arbitrary", "arbitrary"),
    )(proj3, pool_w, pool_scale, conv_w)


def poolconv_bwd(proj3, dpo, dco, pool_w, pool_scale, conv_w, layer, dproj3, dcw_stacked, lay):
    Bl, S, _ = proj3.shape

    def body(x_ref, dpo_ref, dco_ref, pw_ref, ps_ref, cw_ref, *rest):
        dx_ref, dpw_ref, dps_ref, dcw_ref, dcw_acc = rest[-5:]
        g, b = pl.program_id(0), pl.program_id(1)
        row = lax.broadcasted_iota(jnp.int32, (S, GROUP_W), 0)
        cnt = _window_count(g, row)
        u, cv, cb, cc = _group_columns(x_ref)
        d = (_window_sum(u, g, row, _shift_down) / cnt - u).astype(BF16)
        pw = pw_ref[...]
        ypre = _dot(d, pw)
        dpo_v = dpo_ref[...].astype(F32)
        dps = jnp.sum(dpo_v * ypre, axis=0, keepdims=True)
        dyp = (dpo_v * ps_ref[...]).astype(BF16)
        dpw = _dot_tn(d, dyp)
        dd = _dot_nt(dyp, pw)
        dx_ref[:, 0:GROUP_W] = (_window_sum(dd / cnt, g, row, _shift_up) - dd).astype(BF16)

        z = cc * cv
        z1, z2 = _shift_down(z, 1, row), _shift_down(z, 2, row)
        w0, w1, w2 = cw_ref[0:1, :], cw_ref[1:2, :], cw_ref[2:3, :]
        y = w0 * z2 + w1 * z1 + w2 * z
        dco_v = dco_ref[...].astype(F32)
        dy = dco_v * cb
        dz = w0 * _shift_up(dy, 2, row) + w1 * _shift_up(dy, 1, row) + w2 * dy
        dx_ref[:, GROUP_W:2 * GROUP_W] = (dz * cc).astype(BF16)
        dx_ref[:, 2 * GROUP_W:3 * GROUP_W] = (dco_v * y).astype(BF16)
        dx_ref[:, 3 * GROUP_W:] = (dz * cv).astype(BF16)
        dcw = jnp.concatenate([jnp.sum(dy * z2, axis=0, keepdims=True),
                               jnp.sum(dy * z1, axis=0, keepdims=True),
                               jnp.sum(dy * z, axis=0, keepdims=True)], axis=0)

        @pl.when(b == 0)
        def _():
            dpw_ref[...] = dpw
            dps_ref[...] = dps
            dcw_acc[...] = dcw

        @pl.when(b > 0)
        def _():
            dpw_ref[...] += dpw
            dps_ref[...] += dps
            dcw_acc[...] += dcw

        @pl.when(b == Bl - 1)
        def _():
            dcw_ref[...] = dcw_acc[...].astype(BF16)

    blk = pl.BlockSpec((None, S, GROUP_W), lambda g, b: (b, 0, g))
    pc = pl.BlockSpec((None, S, BRANCH_W), lambda g, b: (b, 0, lay["pc"] // BRANCH_W + g))
    ins = [proj3, dpo, dco, pool_w, pool_scale, conv_w, dproj3]
    in_specs = [pc, blk, blk,
                pl.BlockSpec((None, None, GROUP_W, GROUP_W), lambda g, b: (layer, g, 0, 0)),
                pl.BlockSpec((None, 1, GROUP_W), lambda g, b: (layer, 0, g)),
                pl.BlockSpec((None, None, 3, GROUP_W), lambda g, b: (g, layer, 0, 0)), _ANY]
    aliases = {6: 0}
    if dcw_stacked is not None:
        ins.append(dcw_stacked)
        in_specs.append(_ANY)
        aliases[7] = 3
    return pl.pallas_call(
        body, name="poolconv_bwd", grid=(N_GROUPS, Bl),
        in_specs=in_specs,
        out_specs=[pc, pl.BlockSpec((None, GROUP_W, GROUP_W), lambda g, b: (g, 0, 0)),
                   pl.BlockSpec((1, GROUP_W), lambda g, b: (0, g)),
                   pl.BlockSpec((None, None, 3, GROUP_W), lambda g, b: (layer, g, 0, 0))],
        out_shape=[SDS(dproj3.shape, BF16), SDS((N_GROUPS, GROUP_W, GROUP_W), F32), SDS((1, BRANCH_W), F32),
                   SDS((N_LAYERS, N_CHIPS, 3, GROUP_W), BF16)],
        scratch_shapes=[pltpu.VMEM((3, GROUP_W), F32)],
        input_output_aliases=aliases,
        compiler_params=_params("arbitrary", "arbitrary"),
    )(*ins)


def _row_tile(rows, cols, n_arrays):
    if rows % 8:
        return rows
    lanes = -(-cols // LANES) * LANES
    for t in (2048, 1024, 512, 256, 128, 64, 32, 16, 8):
        if rows % t == 0 and 2 * n_arrays * t * lanes * 4 <= VMEM_LIMIT // 2:
            return t
    return rows


def add_pair(kept, core, received, name):
    _, n, R, C = kept.shape
    tr = _row_tile(R, C, 3)

    def body(core_ref, a_ref, b_ref, o_ref):
        o_ref[...] = (a_ref[...].astype(F32) + b_ref[...].astype(F32)).astype(BF16)

    blk = pl.BlockSpec((None, tr, C), lambda d, i, core_ref: (d, i, 0))
    grid_spec = pltpu.PrefetchScalarGridSpec(
        num_scalar_prefetch=1, grid=(n, R // tr),
        in_specs=[pl.BlockSpec((None, None, tr, C), lambda d, i, core_ref: (core_ref[0], d, i, 0)), blk],
        out_specs=blk)
    return pl.pallas_call(body, name=name, grid_spec=grid_spec, out_shape=SDS((n, R, C), BF16),
                          compiler_params=_params("arbitrary", "arbitrary"))(core, kept, received)


def add_chips(parts, core, name):
    _, R, C = parts.shape
    tr = _row_tile(R, C, 4)

    def body(core_ref, p_ref, o_ref):
        acc = p_ref[0].astype(F32)
        for j in range(1, N_CHIPS):
            acc = acc + p_ref[j].astype(F32)
        o_ref[...] = acc

    grid_spec = pltpu.PrefetchScalarGridSpec(
        num_scalar_prefetch=1, grid=(R // tr,),
        in_specs=[pl.BlockSpec((N_CHIPS, tr, C), lambda i, core_ref: (0, i, 0))],
        out_specs=pl.BlockSpec((None, tr, C), lambda i, core_ref: (core_ref[0], i, 0)))
    return pl.pallas_call(body, name=name, grid_spec=grid_spec, out_shape=SDS((2, R, C), F32),
                          compiler_params=_params("arbitrary"))(core, parts)


def adamw(w, g, m, v, name):
    R, C = w.shape
    tr = _row_tile(R, C, 7)

    def body(w_ref, g_ref, m_ref, v_ref, d_ref, nm_ref, nv_ref):
        gv = g_ref[...]
        m_new = ADAM_B1 * m_ref[...] + (1.0 - ADAM_B1) * gv
        v_new = ADAM_B2 * v_ref[...] + (1.0 - ADAM_B2) * (gv * gv)
        m_hat = m_new / (1.0 - ADAM_B1 ** ADAM_STEP)
        v_hat = v_new / (1.0 - ADAM_B2 ** ADAM_STEP)
        d_ref[...] = -ADAM_LR * (m_hat / (jnp.sqrt(v_hat) + ADAM_EPS) + ADAM_WD * w_ref[...])
        nm_ref[...] = m_new
        nv_ref[...] = v_new

    blk = pl.BlockSpec((tr, C), lambda i: (i, 0))
    out = SDS((R, C), F32)
    return pl.pallas_call(body, name=name, grid=(R // tr,), in_specs=[blk] * 4, out_specs=[blk] * 3,
                          out_shape=[out, out, out], compiler_params=_params("arbitrary"))(w, g, m, v)


def _position():
    return lax.axis_index("x"), lax.axis_index("y"), lax.axis_index("c")


def _other_chips(x, y):
    return [(1 - x, y), (x, 1 - y), (1 - x, 1 - y)]


def _remote(src, dst, send_sem, recv_sem, device):
    return pltpu.make_async_remote_copy(src_ref=src, dst_ref=dst, send_sem=send_sem, recv_sem=recv_sem,
                                        device_id=device, device_id_type=MESH)


_COMM = pltpu.CompilerParams(has_side_effects=True)


def allgather_weights(shards, row_sharded):
    n = len(shards)
    me_chip = 2 * lax.axis_index("x") + lax.axis_index("y")
    bufs = []
    for t, sh in enumerate(shards):
        L, r, c = sh.shape
        if row_sharded[t]:
            bufs.append(lax.dynamic_update_slice(lax.empty((L, N_CHIPS, r, c), sh.dtype), sh[:, None],
                                                 (0, me_chip, 0, 0)))
        else:
            bufs.append(lax.dynamic_update_slice(lax.empty((N_CHIPS, L, r, c), sh.dtype), sh[None],
                                                 (me_chip, 0, 0, 0)))

    def window(ref, t, chip, layer):
        return ref.at[layer, chip] if row_sharded[t] else ref.at[chip, layer]

    def body(*refs):
        outs = refs[n:2 * n]
        send_sems, recv_sems, fwd_send, fwd_recv = refs[2 * n:]
        x, y, c = _position()
        me = 2 * x + y
        others = _other_chips(x, y)
        sends = []
        for t in range(n):
            for k, (px, py) in enumerate(others):
                win = window(outs[t], t, me, c)
                cp = _remote(win, win, send_sems.at[t, k], recv_sems.at[t, k], (px, py, c))
                cp.start()
                sends.append(cp)
        for t in range(n):
            for k, (px, py) in enumerate(others):
                win = window(outs[t], t, 2 * px + py, c)
                _remote(win, win, send_sems.at[t, k], recv_sems.at[t, k], (px, py, c)).wait_recv()
                cp = _remote(win, win, fwd_send.at[t, k], fwd_recv.at[t, k], (x, y, 1 - c))
                cp.start()
                sends.append(cp)
        for t in range(n):
            for k, (px, py) in enumerate(others):
                win = window(outs[t], t, 2 * px + py, 1 - c)
                _remote(win, win, fwd_send.at[t, k], fwd_recv.at[t, k], (x, y, 1 - c)).wait_recv()
        for cp in sends:
            cp.wait_send()

    sem = pltpu.SemaphoreType.DMA
    return pl.pallas_call(
        body, name="allgather_weights", in_specs=[_ANY] * n, out_specs=[_ANY] * n,
        out_shape=[SDS(b.shape, b.dtype) for b in bufs],
        scratch_shapes=[sem((n, 3)), sem((n, 3)), sem((n, 3)), sem((n, 3))],
        input_output_aliases={t: t for t in range(n)},
        compiler_params=_COMM,
    )(*bufs)


def allgather_chips(buf, name):
    def body(src_ref, out_ref, send_sems, recv_sems, local_sem):
        x, y, c = _position()
        me = 2 * x + y
        mine = pltpu.make_async_copy(src_ref, out_ref.at[me], local_sem)
        mine.start()
        sends = []
        for k, (px, py) in enumerate(_other_chips(x, y)):
            cp = _remote(src_ref, out_ref.at[me], send_sems.at[k], recv_sems.at[k], (px, py, c))
            cp.start()
            sends.append(cp)
        for k, (px, py) in enumerate(_other_chips(x, y)):
            _remote(src_ref, out_ref.at[2 * px + py], send_sems.at[k], recv_sems.at[k], (px, py, c)).wait_recv()
        for cp in sends:
            cp.wait_send()
        mine.wait()

    sem = pltpu.SemaphoreType.DMA
    return pl.pallas_call(
        body, name=name, in_specs=[_ANY], out_specs=_ANY, out_shape=SDS((N_CHIPS,) + buf.shape, buf.dtype),
        scratch_shapes=[sem((3,)), sem((3,)), sem], compiler_params=_COMM,
    )(buf)


def swap_sibling(tensors, name):
    n = len(tensors)

    def body(*refs):
        srcs, outs, send_sems, recv_sems = refs[:n], refs[n:2 * n], refs[2 * n], refs[2 * n + 1]
        x, y, c = _position()
        cps = [_remote(srcs[t].at[1 - c], outs[t], send_sems.at[t], recv_sems.at[t], (x, y, 1 - c))
               for t in range(n)]
        for cp in cps:
            cp.start()
        for cp in cps:
            cp.wait()

    sem = pltpu.SemaphoreType.DMA
    return pl.pallas_call(
        body, name=name, in_specs=[_ANY] * n, out_specs=[_ANY] * n,
        out_shape=[SDS(t.shape[1:], t.dtype) for t in tensors],
        scratch_shapes=[sem((n,)), sem((n,))], compiler_params=_COMM,
    )(*tensors)


def exchange_chips(tensors, name):
    n = len(tensors)

    def body(*refs):
        srcs, outs = refs[:n], refs[n:2 * n]
        send_sems, recv_sems, local_sems = refs[2 * n:]
        x, y, c = _position()
        me = 2 * x + y
        others = _other_chips(x, y)
        cps = []
        for t in range(n):
            cp = pltpu.make_async_copy(srcs[t].at[me], outs[t].at[me], local_sems.at[t])
            cp.start()
            cps.append(cp)
        sends = []
        for t in range(n):
            for k, (px, py) in enumerate(others):
                cp = _remote(srcs[t].at[2 * px + py], outs[t].at[me], send_sems.at[t, k], recv_sems.at[t, k],
                             (px, py, c))
                cp.start()
                sends.append(cp)
        for t in range(n):
            for k, (px, py) in enumerate(others):
                _remote(srcs[t].at[me], outs[t].at[2 * px + py], send_sems.at[t, k], recv_sems.at[t, k],
                        (px, py, c)).wait_recv()
        for cp in sends:
            cp.wait_send()
        for cp in cps:
            cp.wait()

    sem = pltpu.SemaphoreType.DMA
    return pl.pallas_call(
        body, name=name, in_specs=[_ANY] * n, out_specs=[_ANY] * n,
        out_shape=[SDS(t.shape, t.dtype) for t in tensors],
        scratch_shapes=[sem((n, 3)), sem((n, 3)), sem((n,))], compiler_params=_COMM,
    )(*tensors)


def join_halves(tensors, name):
    n = len(tensors)

    def body(*refs):
        outs, send_sems, recv_sems = refs[n:2 * n], refs[2 * n], refs[2 * n + 1]
        x, y, c = _position()
        sib = (x, y, 1 - c)
        sends = []
        for t in range(n):
            cp = _remote(outs[t].at[c], outs[t].at[c], send_sems.at[t], recv_sems.at[t], sib)
            cp.start()
            sends.append(cp)
        for t in range(n):
            _remote(outs[t].at[c], outs[t].at[1 - c], send_sems.at[t], recv_sems.at[t], sib).wait_recv()
        for cp in sends:
            cp.wait_send()

    sem = pltpu.SemaphoreType.DMA
    return pl.pallas_call(
        body, name=name, in_specs=[_ANY] * n, out_specs=[_ANY] * n,
        out_shape=[SDS(t.shape, t.dtype) for t in tensors],
        scratch_shapes=[sem((n,)), sem((n,))], input_output_aliases={t: t for t in range(n)},
        compiler_params=_COMM,
    )(*tensors)


BIG = ("w_in", "w_proj_attn", "w_proj_pool", "w_proj_conv", "conv_w", "w_out", "w_gate_up", "w_down")
ROW_SHARDED = ("w_out", "w_down")
REPLICATED = ("attn_norm", "b_forget", "b_gate", "pool_w", "pool_scale", "ffn_norm", "final_norm")
ORDER = ("attn_norm", "w_in", "b_forget", "b_gate", "w_proj_attn", "pool_w", "pool_scale", "w_proj_pool",
         "conv_w", "w_proj_conv", "w_out", "ffn_norm", "w_gate_up", "w_down", "final_norm")


def _proj_layout(D):
    lay = {"g": 0, "q": 3 * D}
    lay["k"] = lay["q"] + BRANCH_W
    lay["f"] = lay["k"] + BRANCH_W
    lay["v"] = lay["f"] + F_PAD
    lay["pc"] = lay["v"] + BRANCH_W
    lay["width"] = lay["pc"] + 4 * BRANCH_W
    return lay


_REF = dict(q=0, k=512, v=1024, f=1536, u=1544, cv=2056, cb=2568, cc=3080, g=3592)


def _pack_w_in(w):
    cols = lambda a, n: w[..., a:a + n]
    groups = []
    for gi in range(N_GROUPS):
        groups += [cols(_REF[name] + gi * GROUP_W, GROUP_W) for name in ("u", "cv", "cb", "cc")]
    pad = jnp.zeros(w.shape[:-1] + (F_PAD - HEADS,), w.dtype)
    return jnp.concatenate([w[..., _REF["g"]:], cols(_REF["q"], 512), cols(_REF["k"], 512),
                            cols(_REF["f"], HEADS), pad, cols(_REF["v"], 512)] + groups, axis=-1)


def _unpack_w_in(p, D):
    lay = _proj_layout(D)
    cols = lambda a, n: p[..., a:a + n]
    kinds = []
    for kind in range(4):
        kinds += [cols(lay["pc"] + gi * BRANCH_W + kind * GROUP_W, GROUP_W) for gi in range(N_GROUPS)]
    return jnp.concatenate([cols(lay["q"], 512), cols(lay["k"], 512), cols(lay["v"], 512),
                            cols(lay["f"], HEADS)] + kinds + [cols(0, 3 * D)], axis=-1)


def _split_flat(vec, shapes):
    out, at = [], 0
    for shp in shapes:
        n = int(np.prod(shp))
        out.append(vec[at:at + n].reshape(shp))
        at += n
    return out


def kernel(x, attn_norm, w_in, b_forget, b_gate, w_proj_attn, pool_w, pool_scale, w_proj_pool, conv_w, w_proj_conv, w_out, ffn_norm, w_gate_up, w_down, final_norm, loss_target, m_attn_norm, m_w_in, m_b_forget, m_b_gate, m_w_proj_attn, m_pool_w, m_pool_scale, m_w_proj_pool, m_conv_w, m_w_proj_conv, m_w_out, m_ffn_norm, m_w_gate_up, m_w_down, m_final_norm, v_attn_norm, v_w_in, v_b_forget, v_b_gate, v_w_proj_attn, v_pool_w, v_pool_scale, v_w_proj_pool, v_conv_w, v_w_proj_conv, v_w_out, v_ffn_norm, v_w_gate_up, v_w_down, v_final_norm):
    weights = dict(attn_norm=attn_norm, w_in=w_in, b_forget=b_forget, b_gate=b_gate, w_proj_attn=w_proj_attn,
                   pool_w=pool_w, pool_scale=pool_scale, w_proj_pool=w_proj_pool, conv_w=conv_w,
                   w_proj_conv=w_proj_conv, w_out=w_out, ffn_norm=ffn_norm, w_gate_up=w_gate_up, w_down=w_down,
                   final_norm=final_norm)
    mom_m = dict(attn_norm=m_attn_norm, w_in=m_w_in, b_forget=m_b_forget, b_gate=m_b_gate, w_proj_attn=m_w_proj_attn,
                 pool_w=m_pool_w, pool_scale=m_pool_scale, w_proj_pool=m_w_proj_pool, conv_w=m_conv_w,
                 w_proj_conv=m_w_proj_conv, w_out=m_w_out, ffn_norm=m_ffn_norm, w_gate_up=m_w_gate_up,
                 w_down=m_w_down, final_norm=m_final_norm)
    mom_v = dict(attn_norm=v_attn_norm, w_in=v_w_in, b_forget=v_b_forget, b_gate=v_b_gate, w_proj_attn=v_w_proj_attn,
                 pool_w=v_pool_w, pool_scale=v_pool_scale, w_proj_pool=v_w_proj_pool, conv_w=v_conv_w,
                 w_proj_conv=v_w_proj_conv, w_out=v_w_out, ffn_norm=v_ffn_norm, w_gate_up=v_w_gate_up,
                 w_down=v_w_down, final_norm=v_final_norm)

    Bl, S, D = x.shape
    T = Bl * S
    L = w_in.shape[0]
    F = w_down.shape[1] * N_CHIPS
    lay = _proj_layout(D)
    cst = _placement_constants()
    assert L == N_LAYERS and S % ATTN_BLOCK == 0 and F % (2 * LANES) == 0 and D % BRANCH_W == 0
    assert w_in.shape[2] * N_CHIPS == _REF["g"] + 3 * D and conv_w.shape[2] == GROUP_W

    cast = lambda n: weights[n] if n == "conv_w" else weights[n].astype(BF16)
    gathered = dict(zip(BIG, allgather_weights([cast(n) for n in BIG], [n in ROW_SHARDED for n in BIG])))
    w_in_cols = gathered["w_in"].transpose(1, 2, 0, 3).reshape(L, D, -1)
    w_in_p = _pack_w_in(w_in_cols)
    w_down_f = gathered["w_down"].reshape(L, F, D)
    w_gu, w_o, conv_w_g = gathered["w_gate_up"], gathered["w_out"], gathered["conv_w"]
    wpa, wpp, wpc = gathered["w_proj_attn"], gathered["w_proj_pool"], gathered["w_proj_conv"]
    pool_w_b = pool_w.astype(BF16)
    an3, fn3 = attn_norm.reshape(L, 1, D), ffn_norm.reshape(L, 1, D)
    bg3, ps3 = b_gate.reshape(L, 1, 3 * D), pool_scale.reshape(L, 1, BRANCH_W)
    bf3 = jnp.pad(b_forget, ((0, 0), (0, LANES - HEADS))).reshape(L, 1, LANES)

    xs = x.reshape(T, D)
    saved = []
    for l in range(L):
        proj, h = norm_matmul(xs, an3, w_in_p, l, False, "in_proj")
        proj3 = proj.reshape(Bl, S, lay["width"])
        qa, ka = attn_prep(proj3, bf3, l, cst, lay)
        ao, lse = attn_fwd(qa, ka, proj3, lay)
        po, co = poolconv_fwd(proj3, pool_w_b, ps3, conv_w_g, l, lay)
        ao2, po2, co2 = (a.reshape(T, BRANCH_W) for a in (ao, po, co))
        x1, ys, mixed = mix_fwd(ao2, po2, co2, proj, bg3, wpa, wpp, wpc, w_o, l, xs)
        ab, h2 = norm_matmul(x1, fn3, w_gu, l, True, "gate_up_proj")
        x2, s_act = ffn_down_fwd(ab, w_down_f, l, x1)
        saved.append(dict(x=xs, proj=proj, proj3=proj3, h=h, qa=qa, ka=ka, ao=ao, lse=lse, ao2=ao2, po2=po2,
                          co2=co2, ys=ys, mixed=mixed, x1=x1, ab=ab, h2=h2, s=s_act))
        xs = x2

    loss_row, dx, dxb, g_final = loss_head(xs, final_norm.reshape(1, D), loss_target.reshape(T, D))
    loss = lax.psum(loss_row[0, 0], AXES)

    stacked = {n: None for n in BIG}
    small = {n: [None] * L for n in REPLICATED if n != "final_norm"}
    to3 = lambda a: a.reshape(Bl, S, -1)
    for l in reversed(range(L)):
        sv = saved[l]
        da, db = ffn_down_bwd(dxb, w_down_f, l, sv["ab"])
        stacked["w_down"] = matmul_tn(sv["s"], [dxb], "grad_w_down", l, stacked["w_down"])
        stacked["w_gate_up"] = matmul_tn(sv["h2"], [da, db], "grad_w_gate_up", l, stacked["w_gate_up"],
                                         by_dest=True, tn=2 * F // N_CHIPS, tk=_tile(T, (1024, 512, 256)))
        dx1, dx1b, g_fn = matmul_nt_normbwd([da, db], w_gu, l, True, sv["x1"], fn3, dx, "gate_up_bwd")
        small["ffn_norm"][l] = g_fn[0]
        dys, dproj, dao, dpo, dco, g_bg = mix_bwd(dx1b, w_o, sv["proj"], bg3, sv["ys"], wpa, wpp, wpc, l,
                                                  lay["width"])
        small["b_gate"][l] = g_bg[0]
        stacked["w_out"] = matmul_tn(sv["mixed"], [dx1b], "grad_w_out", l, stacked["w_out"])
        for n, (name, br) in enumerate((("w_proj_attn", sv["ao2"]), ("w_proj_pool", sv["po2"]),
                                        ("w_proj_conv", sv["co2"]))):
            stacked[name] = matmul_tn(br, [dys], "grad_" + name, l, stacked[name], b_col0=n * D, n_cols=D,
                                      by_dest=True, tn=D // N_CHIPS)
        dqa, dka, dproj3 = attn_bwd(sv["qa"], sv["ka"], sv["proj3"], to3(dao), sv["ao"], sv["lse"], to3(dproj), lay)
        dproj3, g_bf = attn_post(dqa, dka, sv["proj3"], bf3, l, dproj3, cst, lay)
        small["b_forget"][l] = g_bf[0, :HEADS]
        dproj3, g_pw, g_ps, stacked["conv_w"] = poolconv_bwd(sv["proj3"], to3(dpo), to3(dco), pool_w_b, ps3,
                                                             conv_w_g, l, dproj3, stacked["conv_w"], lay)
        small["pool_w"][l], small["pool_scale"][l] = g_pw, g_ps[0]
        dproj = dproj3.reshape(T, lay["width"])
        stacked["w_in"] = matmul_tn(sv["h"], [dproj], "grad_w_in", l, stacked["w_in"])
        dx, dxb, g_an = matmul_nt_normbwd([dproj], w_in_p, l, False, sv["x"], an3, dx1, "in_proj_bwd")
        small["attn_norm"][l] = g_an[0]
    grad_x = dx.reshape(Bl, S, D)

    g_w_in = _unpack_w_in(stacked["w_in"], D)
    cs_in = g_w_in.shape[2] // N_CHIPS
    by_dest = dict(stacked)
    by_dest["w_in"] = g_w_in.reshape(L, D, N_CHIPS, cs_in).transpose(0, 2, 1, 3)
    by_dest["w_out"] = stacked["w_out"].reshape(L, N_CHIPS, D // N_CHIPS, D)
    by_dest["w_down"] = stacked["w_down"].reshape(L, N_CHIPS, F // N_CHIPS, D)
    small_shapes = [weights[n].shape for n in REPLICATED]
    small_vec = jnp.concatenate([jnp.stack(small[n]).reshape(-1) for n in REPLICATED[:-1]] + [g_final[0]])
    n_small = small_vec.shape[0]
    small_vec = jnp.pad(small_vec, (0, -n_small % (2 * N_CHIPS * 16 * LANES))).astype(BF16)
    names = BIG + ("small",)
    by_dest["small"] = small_vec.reshape(N_CHIPS, 2, -1, LANES).transpose(1, 0, 2, 3)
    core = lax.axis_index("c").astype(jnp.int32).reshape(1)
    received = swap_sibling([by_dest[n] for n in names], "swap_grad_layers")
    chip_sum = [add_pair(by_dest[n], core, r, "add_pair_" + n) for n, r in zip(names, received)]
    arrived = exchange_chips(chip_sum, "exchange_grad_chips")
    reduced = join_halves([add_chips(a, core, "add_chips_" + n) for n, a in zip(names, arrived)],
                          "join_grad_layers")
    shard_grads = dict(zip(names, reduced))
    small_all = allgather_chips(shard_grads.pop("small"), "allgather_small_grads").reshape(-1)[:n_small]
    rep_grads = dict(zip(REPLICATED, _split_flat(small_all, small_shapes)))

    delta, new_m, new_v = {}, {}, {}
    for n in BIG:
        shp = weights[n].shape
        two_d = lambda a: a.reshape(-1, shp[-1])
        d, nm, nv = adamw(two_d(weights[n]), two_d(shard_grads[n]), two_d(mom_m[n]), two_d(mom_v[n]), "adamw_" + n)
        delta[n], new_m[n], new_v[n] = d.reshape(shp), nm.reshape(shp), nv.reshape(shp)
        shard_grads[n] = shard_grads[n].reshape(shp)

    def rows(d):
        vec = jnp.concatenate([d[n].reshape(-1) for n in REPLICATED])
        return jnp.pad(vec, (0, -n_small % (8 * LANES))).reshape(-1, LANES)

    outs = adamw(rows(weights), rows(rep_grads), rows(mom_m), rows(mom_v), "adamw_replicated")
    for res, o in zip((delta, new_m, new_v), outs):
        res.update(zip(REPLICATED, _split_flat(o.reshape(-1), small_shapes)))
    all_grads = {**shard_grads, **rep_grads}

    return (loss, grad_x, *[all_grads[n] for n in ORDER], *[delta[n] for n in ORDER],
            *[new_m[n] for n in ORDER], *[new_v[n] for n in ORDER])
```

```python
import numpy as np
import jax
import jax.numpy as jnp
from jax import lax
from jax.experimental import pallas as pl
from jax.experimental.pallas import tpu as pltpu

F32, BF16 = jnp.float32, jnp.bfloat16
SDS = jax.ShapeDtypeStruct
MESH = pl.DeviceIdType.MESH
AXES = ("x", "y", "c")
N_CHIPS = 4
N_LAYERS = 2
LANES = 128
VMEM_LIMIT = 48 * 1024 * 1024

HEADS, HEAD_DIM = 8, 64
HEAD_PAD = 128
BRANCH_W = 512
GROUP_W = 128
N_GROUPS = BRANCH_W // GROUP_W
POOL_WINDOWS = (2, 4, 8, 16)
F_PAD = 512
ATTN_BLOCK = 256
RMS_EPS = 1e-6
NEG_INF = -1e30
ADAM_LR, ADAM_B1, ADAM_B2, ADAM_EPS, ADAM_WD, ADAM_STEP = 0.001, 0.9, 0.999, 1e-08, 0.01, 10

NT = (((1,), (1,)), ((), ()))
TN = (((0,), (0,)), ((), ()))
_ANY = pl.BlockSpec(memory_space=pl.ANY)


def _tile(n, prefs):
    for p in prefs:
        if n % p == 0:
            return p
    raise ValueError(f"no tile of {prefs} divides {n}")


def _params(*sem):
    return pltpu.CompilerParams(dimension_semantics=sem, vmem_limit_bytes=VMEM_LIMIT)


def _sigmoid(z):
    return 0.5 * jnp.tanh(0.5 * z) + 0.5


def _split3(x):
    h1 = x.astype(BF16)
    r1 = x - h1.astype(F32)
    h2 = r1.astype(BF16)
    h3 = (r1 - h2.astype(F32)).astype(BF16)
    return h1, h2, h3


def _dot(a, b):
    return jnp.dot(a, b, preferred_element_type=F32)


def _dot_nt(a, b):
    return lax.dot_general(a, b, NT, preferred_element_type=F32)


def _dot_tn(a, b):
    return lax.dot_general(a, b, TN, preferred_element_type=F32)


def norm_matmul(x, gain, w, layer, kind, name):
    T, D = x.shape
    if kind == "by_shard":
        tn = w.shape[3]
        N = N_CHIPS * tn
        w_spec = pl.BlockSpec((None, None, D, tn), lambda i, j: (j, layer, 0, 0))
        mm = _dot
    else:
        N = w.shape[1]
        tn = _tile(N, (1024, 512, 256, 128))
        w_spec = pl.BlockSpec((None, tn, D), lambda i, j: (layer, j, 0))
        mm = _dot_nt
    tm = _tile(T, (1024, 512, 256, 128))

    def body(x_ref, g_ref, w_ref, y_ref, h_ref):
        @pl.when(pl.program_id(1) == 0)
        def _():
            xf = x_ref[...]
            r = lax.rsqrt(jnp.mean(xf * xf, axis=-1, keepdims=True) + RMS_EPS)
            h_ref[...] = ((xf * r) * g_ref[...]).astype(BF16)

        y_ref[...] = mm(h_ref[...], w_ref[...]).astype(BF16)

    return pl.pallas_call(
        body, name=name, grid=(T // tm, N // tn),
        in_specs=[pl.BlockSpec((tm, D), lambda i, j: (i, 0)),
                  pl.BlockSpec((None, 1, D), lambda i, j: (layer, 0, 0)),
                  w_spec],
        out_specs=[pl.BlockSpec((tm, tn), lambda i, j: (i, j)),
                   pl.BlockSpec((tm, D), lambda i, j: (i, 0))],
        out_shape=[SDS((T, N), BF16), SDS((T, D), BF16)],
        compiler_params=_params("arbitrary", "arbitrary"),
    )(x, gain, w)


def matmul_nt_normbwd(dys, w, layer, kind, x, gain, dres, name):
    T, D = x.shape
    width = dys[0].shape[1]
    if kind == "by_shard":
        tk = w.shape[3]
        w_spec = pl.BlockSpec((None, None, D, tk), lambda i, k: (k, layer, 0, 0))
        mm = _dot_nt
    else:
        tk = _tile(width, (1024, 512, 256, 128))
        w_spec = pl.BlockSpec((None, tk, D), lambda i, k: (layer, k, 0))
        mm = _dot
    per = width // tk
    nk = per * len(dys)
    tm = _tile(T, (512, 256, 128))
    n_dy = len(dys)

    def dy_spec(p):
        return pl.BlockSpec((tm, tk), lambda i, k: (i, jnp.clip(k - p * per, 0, per - 1)))

    def body(*refs):
        dy_refs = refs[:n_dy]
        w_ref, x_ref, g_ref, dres_ref, dx_ref, dxb_ref, dg_ref, acc_ref = refs[n_dy:]
        i, k = pl.program_id(0), pl.program_id(1)

        @pl.when(k == 0)
        def _():
            acc_ref[...] = jnp.zeros_like(acc_ref)

        for p in range(n_dy):
            @pl.when((k >= p * per) & (k < (p + 1) * per))
            def _(p=p):
                acc_ref[...] += mm(dy_refs[p][...], w_ref[...])

        @pl.when(k == nk - 1)
        def _():
            xf = x_ref[...]
            r = lax.rsqrt(jnp.mean(xf * xf, axis=-1, keepdims=True) + RMS_EPS)
            xhat = xf * r
            dh = acc_ref[...]
            dhg = dh * g_ref[...]
            dx = dres_ref[...] + r * (dhg - xhat * jnp.mean(dhg * xhat, axis=-1, keepdims=True))
            dx_ref[...] = dx
            dxb_ref[...] = dx.astype(BF16)
            part = jnp.sum(dh * xhat, axis=0, keepdims=True)

            @pl.when(i == 0)
            def _():
                dg_ref[...] = part

            @pl.when(i > 0)
            def _():
                dg_ref[...] += part

    row = pl.BlockSpec((tm, D), lambda i, k: (i, 0))
    return pl.pallas_call(
        body, name=name, grid=(T // tm, nk),
        in_specs=[dy_spec(p) for p in range(n_dy)] + [
            w_spec, row, pl.BlockSpec((None, 1, D), lambda i, k: (layer, 0, 0)), row],
        out_specs=[row, row, pl.BlockSpec((1, D), lambda i, k: (0, 0))],
        out_shape=[SDS((T, D), F32), SDS((T, D), BF16), SDS((1, D), F32)],
        scratch_shapes=[pltpu.VMEM((tm, D), F32)],
        compiler_params=_params("arbitrary", "arbitrary"),
    )(*dys, w, x, gain, dres)


def matmul_tn(a, bs, name, layer, stacked, b_col0=0, n_cols=None, by_dest=False, tn=None, tk=None):
    T, M = a.shape
    width = bs[0].shape[1]
    N = n_cols if n_cols else width * len(bs)
    tm = _tile(M, (1024, 512, 256, 128))
    tn = tn or _tile(N, (512, 256, 128))
    tk = tk or _tile(T, (4096, 2048, 1024, 512, 256))
    assert b_col0 % tn == 0 and width % tn == 0
    j0, per, nk, n_b = b_col0 // tn, width // tn, T // tk, len(bs)

    def b_spec(p):
        return pl.BlockSpec((tk, tn), lambda i, j, k: (k, jnp.clip(j0 + j - p * per, 0, per - 1)))

    def body(*refs):
        a_ref, b_refs = refs[0], refs[1:1 + n_b]
        o_ref, acc_ref = refs[-2], refs[-1]
        j, k = pl.program_id(1), pl.program_id(2)

        @pl.when(k == 0)
        def _():
            acc_ref[...] = jnp.zeros_like(acc_ref)

        for p in range(n_b):
            @pl.when((j0 + j >= p * per) & (j0 + j < (p + 1) * per))
            def _(p=p):
                acc_ref[...] += _dot_tn(a_ref[...], b_refs[p][...])

        @pl.when(k == nk - 1)
        def _():
            o_ref[...] = acc_ref[...].astype(BF16)

    if by_dest:
        cs = N // N_CHIPS
        npd = cs // tn
        out_shape = SDS((N_LAYERS, N_CHIPS, M, cs), BF16)
        out_spec = pl.BlockSpec((None, None, tm, tn), lambda i, j, k: (layer, j // npd, i, j % npd))
    else:
        out_shape = SDS((N_LAYERS, M, N), BF16)
        out_spec = pl.BlockSpec((None, tm, tn), lambda i, j, k: (layer, i, j))
    ins = [a] + list(bs)
    in_specs = [pl.BlockSpec((tk, tm), lambda i, j, k: (k, i))] + [b_spec(p) for p in range(n_b)]
    aliases = {}
    if stacked is not None:
        ins.append(stacked)
        in_specs.append(_ANY)
        aliases = {len(ins) - 1: 0}

    def body_wrap(*refs):
        if stacked is not None:
            refs = refs[:1 + n_b] + refs[2 + n_b:]
        body(*refs)

    return pl.pallas_call(
        body_wrap, name=name, grid=(M // tm, N // tn, nk),
        in_specs=in_specs, out_specs=out_spec, out_shape=out_shape,
        scratch_shapes=[pltpu.VMEM((tm, tn), F32)], input_output_aliases=aliases,
        compiler_params=_params("arbitrary", "arbitrary", "arbitrary"),
    )(*ins)


def ffn_down_fwd(ab, w_down, layer, x1):
    T, D = x1.shape
    F = w_down.shape[1]
    tm = _tile(T, (512, 256, 128))
    tk = F // 2
    nk = F // tk

    def body(a_ref, b_ref, w_ref, x_ref, x2_ref, s_ref, acc_ref):
        k = pl.program_id(1)

        @pl.when(k == 0)
        def _():
            acc_ref[...] = x_ref[...]

        a = a_ref[...].astype(F32)
        s = (a * _sigmoid(a) * b_ref[...].astype(F32)).astype(BF16)
        s_ref[...] = s
        acc_ref[...] += _dot(s, w_ref[...])

        @pl.when(k == nk - 1)
        def _():
            x2_ref[...] = acc_ref[...]

    return pl.pallas_call(
        body, name="ffn_down_fwd", grid=(T // tm, nk),
        in_specs=[pl.BlockSpec((tm, tk), lambda i, k: (i, k)),
                  pl.BlockSpec((tm, tk), lambda i, k: (i, nk + k)),
                  pl.BlockSpec((None, tk, D), lambda i, k: (layer, k, 0)),
                  pl.BlockSpec((tm, D), lambda i, k: (i, 0))],
        out_specs=[pl.BlockSpec((tm, D), lambda i, k: (i, 0)),
                   pl.BlockSpec((tm, tk), lambda i, k: (i, k))],
        out_shape=[SDS((T, D), F32), SDS((T, F), BF16)],
        scratch_shapes=[pltpu.VMEM((tm, D), F32)],
        compiler_params=_params("arbitrary", "arbitrary"),
    )(ab, ab, w_down, x1)


def ffn_down_bwd(dx2b, w_down, layer, ab):
    T, D = dx2b.shape
    F = w_down.shape[1]
    tm = _tile(T, (512, 256, 128))
    tn = F // 2
    nj = F // tn

    def body(dx_ref, w_ref, a_ref, b_ref, da_ref, db_ref):
        ds = _dot_nt(dx_ref[...], w_ref[...])
        a = a_ref[...].astype(F32)
        sg = _sigmoid(a)
        da_ref[...] = (ds * b_ref[...].astype(F32) * (sg * (1.0 + a * (1.0 - sg)))).astype(BF16)
        db_ref[...] = (ds * (a * sg)).astype(BF16)

    blk = pl.BlockSpec((tm, tn), lambda i, j: (i, j))
    return pl.pallas_call(
        body, name="ffn_down_bwd", grid=(T // tm, nj),
        in_specs=[pl.BlockSpec((tm, D), lambda i, j: (i, 0)),
                  pl.BlockSpec((None, tn, D), lambda i, j: (layer, j, 0)),
                  blk, pl.BlockSpec((tm, tn), lambda i, j: (i, nj + j))],
        out_specs=[blk, blk],
        out_shape=[SDS((T, F), BF16), SDS((T, F), BF16)],
        compiler_params=_params("arbitrary", "arbitrary"),
    )(dx2b, w_down, ab, ab)


def _mix_specs(tm, D, layer):
    cs = D // N_CHIPS
    row = lambda w: pl.BlockSpec((tm, w), lambda i: (i, 0))
    wp = pl.BlockSpec((N_CHIPS, None, BRANCH_W, cs), lambda i: (0, layer, 0, 0))
    wo = pl.BlockSpec((None, N_CHIPS, cs, D), lambda i: (layer, 0, 0, 0))
    bg = pl.BlockSpec((None, 1, 3 * D), lambda i: (layer, 0, 0))
    return row, wp, wo, bg


def mix_fwd(ao, po, co, proj, b_gate, wpa, wpp, wpc, w_out, layer, x):
    T, D = x.shape
    cs = D // N_CHIPS
    tm = _tile(T, (256, 128))
    row, wp, wo, bg = _mix_specs(tm, D, layer)

    def body(ao_ref, po_ref, co_ref, g_ref, bg_ref, wpa_ref, wpp_ref, wpc_ref, wo_ref, x_ref,
             x1_ref, ys_ref, mixed_ref):
        mixed = jnp.zeros((tm, D), F32)
        for n, (br, wp_ref) in enumerate(((ao_ref, wpa_ref), (po_ref, wpp_ref), (co_ref, wpc_ref))):
            y = jnp.concatenate([_dot(br[...], wp_ref[j]) for j in range(N_CHIPS)], axis=1)
            cols = slice(n * D, (n + 1) * D)
            gate = _sigmoid(g_ref[:, cols].astype(F32) + bg_ref[:, cols])
            ys_ref[:, cols] = y.astype(BF16)
            mixed = mixed + gate * y
        mb = mixed.astype(BF16)
        mixed_ref[...] = mb
        acc = x_ref[...]
        for j in range(N_CHIPS):
            acc = acc + _dot(mb[:, j * cs:(j + 1) * cs], wo_ref[j])
        x1_ref[...] = acc

    return pl.pallas_call(
        body, name="mix_fwd", grid=(T // tm,),
        in_specs=[row(BRANCH_W), row(BRANCH_W), row(BRANCH_W), row(3 * D), bg, wp, wp, wp, wo, row(D)],
        out_specs=[row(D), row(3 * D), row(D)],
        out_shape=[SDS((T, D), F32), SDS((T, 3 * D), BF16), SDS((T, D), BF16)],
        compiler_params=_params("arbitrary"),
    )(ao, po, co, proj, b_gate, wpa, wpp, wpc, w_out, x)


def mix_bwd(dx1b, w_out, proj, b_gate, ys, wpa, wpp, wpc, layer, width):
    T, D = dx1b.shape
    cs = D // N_CHIPS
    tm = _tile(T, (256, 128))
    row, wp, wo, bg = _mix_specs(tm, D, layer)

    def body(dx_ref, wo_ref, g_ref, bg_ref, ys_ref, wpa_ref, wpp_ref, wpc_ref,
             dys_ref, dg_ref, dao_ref, dpo_ref, dco_ref, dbg_ref):
        i = pl.program_id(0)
        dx = dx_ref[...]
        dmixed = jnp.concatenate([_dot_nt(dx, wo_ref[j]) for j in range(N_CHIPS)], axis=1)
        for n, (wp_ref, dbr) in enumerate(((wpa_ref, dao_ref), (wpp_ref, dpo_ref), (wpc_ref, dco_ref))):
            cols = slice(n * D, (n + 1) * D)
            gate = _sigmoid(g_ref[:, cols].astype(F32) + bg_ref[:, cols])
            dy = (dmixed * gate).astype(BF16)
            dys_ref[:, cols] = dy
            dgp = dmixed * ys_ref[:, cols].astype(F32) * gate * (1.0 - gate)
            dg_ref[:, cols] = dgp.astype(BF16)
            part = jnp.sum(dgp, axis=0, keepdims=True)

            @pl.when(i == 0)
            def _():
                dbg_ref[:, cols] = part

            @pl.when(i > 0)
            def _():
                dbg_ref[:, cols] += part

            acc = jnp.zeros((tm, BRANCH_W), F32)
            for j in range(N_CHIPS):
                acc = acc + _dot_nt(dy[:, j * cs:(j + 1) * cs], wp_ref[j])
            dbr[...] = acc.astype(BF16)

    return pl.pallas_call(
        body, name="mix_bwd", grid=(T // tm,),
        in_specs=[row(D), wo, row(3 * D), bg, row(3 * D), wp, wp, wp],
        out_specs=[row(3 * D), row(3 * D), row(BRANCH_W), row(BRANCH_W), row(BRANCH_W),
                   pl.BlockSpec((1, 3 * D), lambda i: (0, 0))],
        out_shape=[SDS((T, 3 * D), BF16), SDS((T, width), BF16), SDS((T, BRANCH_W), BF16),
                   SDS((T, BRANCH_W), BF16), SDS((T, BRANCH_W), BF16), SDS((1, 3 * D), F32)],
        compiler_params=_params("arbitrary"),
    )(dx1b, w_out, proj, b_gate, ys, wpa, wpp, wpc)


def loss_head(x2, gain, target):
    T, D = x2.shape
    tm = _tile(T, (512, 256, 128))

    def body(x_ref, g_ref, t_ref, loss_ref, dx_ref, dxb_ref, dg_ref):
        i = pl.program_id(0)
        xf = x_ref[...]
        g = g_ref[...]
        r = lax.rsqrt(jnp.mean(xf * xf, axis=-1, keepdims=True) + RMS_EPS)
        xhat = xf * r
        diff = xhat * g - t_ref[...]
        part_loss = 0.5 * jnp.sum(jnp.mean(diff * diff, axis=-1, keepdims=True), axis=0, keepdims=True)
        dy = diff * (1.0 / D)
        dhg = dy * g
        dx = r * (dhg - xhat * jnp.mean(dhg * xhat, axis=-1, keepdims=True))
        dx_ref[...] = dx
        dxb_ref[...] = dx.astype(BF16)
        part_g = jnp.sum(dy * xhat, axis=0, keepdims=True)
        part_l = jnp.broadcast_to(part_loss, (1, LANES))

        @pl.when(i == 0)
        def _():
            dg_ref[...] = part_g
            loss_ref[...] = part_l

        @pl.when(i > 0)
        def _():
            dg_ref[...] += part_g
            loss_ref[...] += part_l

    row = pl.BlockSpec((tm, D), lambda i: (i, 0))
    return pl.pallas_call(
        body, name="loss_head", grid=(T // tm,),
        in_specs=[row, pl.BlockSpec((1, D), lambda i: (0, 0)), row],
        out_specs=[pl.BlockSpec((1, LANES), lambda i: (0, 0)), row, row, pl.BlockSpec((1, D), lambda i: (0, 0))],
        out_shape=[SDS((1, LANES), F32), SDS((T, D), F32), SDS((T, D), BF16), SDS((1, D), F32)],
        compiler_params=_params("arbitrary"),
    )(x2, gain, target)


def _placement_constants():
    w = HEADS * HEAD_PAD
    pq = np.zeros((BRANCH_W, w), np.float32)
    pk = np.zeros((BRANCH_W, w), np.float32)
    pfq = np.zeros((3, LANES, w), np.float32)
    pfk = np.zeros((3, LANES, w), np.float32)
    cq = np.zeros((1, w), np.float32)
    ck = np.zeros((1, w), np.float32)
    eq = np.zeros((w, LANES), np.float32)
    ek = np.zeros((w, LANES), np.float32)
    for h in range(HEADS):
        for d in range(HEAD_DIM):
            pq[h * HEAD_DIM + d, h * HEAD_PAD + d] = HEAD_DIM ** -0.5
            pk[h * HEAD_DIM + d, h * HEAD_PAD + d] = 1.0
        for i in range(3):
            pfq[i, h, h * HEAD_PAD + HEAD_DIM + i] = 1.0
            pfk[i, h, h * HEAD_PAD + HEAD_DIM + 3 + i] = -1.0
            cq[0, h * HEAD_PAD + HEAD_DIM + 3 + i] = 1.0
            ck[0, h * HEAD_PAD + HEAD_DIM + i] = 1.0
        eq[h * HEAD_PAD + HEAD_DIM, h] = 1.0
        ek[h * HEAD_PAD + HEAD_DIM + 3, h] = -1.0
    bf = lambda a: jnp.asarray(a, BF16)
    return dict(pq=bf(pq), pk=bf(pk), pfq=bf(pfq), pfk=bf(pfk), cq=jnp.asarray(cq), ck=jnp.asarray(ck),
                pqkt=bf(np.concatenate([pq.T, pk.T], axis=0)), eq=bf(eq), ek=bf(ek))


def attn_prep(proj3, bf_rows, layer, cst, lay):
    Bl, S, _ = proj3.shape
    ts = ATTN_BLOCK
    w = HEADS * HEAD_PAD

    def body(q_ref, k_ref, f_ref, bf_ref, pq_ref, pk_ref, pfq_ref, pfk_ref, cq_ref, ck_ref,
             qa_ref, ka_ref, carry_ref):
        @pl.when(pl.program_id(1) == 0)
        def _():
            carry_ref[...] = jnp.zeros_like(carry_ref)

        z = f_ref[...].astype(F32) + bf_ref[...]
        logf = jnp.minimum(z, 0.0) - jnp.log(1.0 + jnp.exp(-jnp.abs(z)))
        r = lax.broadcasted_iota(jnp.int32, (ts, ts), 0)
        c = lax.broadcasted_iota(jnp.int32, (ts, ts), 1)
        tri = jnp.where(r >= c, 1.0, 0.0).astype(BF16)
        fcum = carry_ref[...]
        for part in _split3(logf):
            fcum = fcum + _dot(tri, part)
        carry_ref[...] = fcum[ts - 1:ts, :]
        qa = _dot(q_ref[...], pq_ref[...]) + cq_ref[...]
        ka = _dot(k_ref[...], pk_ref[...]) + ck_ref[...]
        for i, part in enumerate(_split3(fcum)):
            qa = qa + _dot(part, pfq_ref[i])
            ka = ka + _dot(part, pfk_ref[i])
        qa_ref[...] = qa.astype(BF16)
        ka_ref[...] = ka.astype(BF16)

    cfull = lambda shape: pl.BlockSpec(shape, lambda b, s: (0,) * len(shape))
    return pl.pallas_call(
        body, name="attn_prep", grid=(Bl, S // ts),
        in_specs=[pl.BlockSpec((None, ts, BRANCH_W), lambda b, s: (b, s, lay["q"] // BRANCH_W)),
                  pl.BlockSpec((None, ts, BRANCH_W), lambda b, s: (b, s, lay["k"] // BRANCH_W)),
                  pl.BlockSpec((None, ts, LANES), lambda b, s: (b, s, lay["f"] // LANES)),
                  pl.BlockSpec((None, 1, LANES), lambda b, s: (layer, 0, 0)),
                  cfull((BRANCH_W, w)), cfull((BRANCH_W, w)),
                  cfull((3, LANES, w)), cfull((3, LANES, w)), cfull((1, w)), cfull((1, w))],
        out_specs=[pl.BlockSpec((None, ts, w), lambda b, s: (b, s, 0)),
                   pl.BlockSpec((None, ts, w), lambda b, s: (b, s, 0))],
        out_shape=[SDS((Bl, S, w), BF16), SDS((Bl, S, w), BF16)],
        scratch_shapes=[pltpu.VMEM((1, LANES), F32)],
        compiler_params=_params("arbitrary", "arbitrary"),
    )(proj3, proj3, proj3, bf_rows, cst["pq"], cst["pk"], cst["pfq"], cst["pfk"], cst["cq"], cst["ck"])


def attn_fwd(qa, ka, proj3, lay):
    Bl, S, _ = qa.shape
    tq = ATTN_BLOCK
    nq = S // tq
    pairs = HEADS // 2
    pw = 2 * HEAD_PAD
    vw = 2 * HEAD_DIM

    def body(qa_ref, ka_ref, v_ref, o_ref, lse_ref):
        row = lax.broadcasted_iota(jnp.int32, (tq, tq), 0)
        col = lax.broadcasted_iota(jnp.int32, (tq, tq), 1)
        causal = row <= col
        for i in range(nq):
            nk = (i + 1) * tq
            rows = slice(i * tq, nk)
            o_t = []
            for h in range(2):
                hs = slice(h * HEAD_PAD, (h + 1) * HEAD_PAD)
                st = _dot_nt(ka_ref[0:nk, hs], qa_ref[rows, hs])
                diag = jnp.where(causal, st[nk - tq:], NEG_INF)
                m = jnp.max(diag, axis=0, keepdims=True)
                if i:
                    m = jnp.maximum(m, jnp.max(st[:nk - tq], axis=0, keepdims=True))
                p_diag = jnp.exp(diag - m)
                l = jnp.sum(p_diag, axis=0, keepdims=True)
                if i:
                    p_top = jnp.exp(st[:nk - tq] - m)
                    l = l + jnp.sum(p_top, axis=0, keepdims=True)
                    p = jnp.concatenate([p_top.astype(BF16), p_diag.astype(BF16)], axis=0)
                else:
                    p = p_diag.astype(BF16)
                acc = _dot_tn(v_ref[0:nk, :], p)
                o_t.append(acc[h * HEAD_DIM:(h + 1) * HEAD_DIM, :] / l)
                lse_ref[h:h + 1, rows] = m + jnp.log(l)
            o_ref[rows, :] = jnp.concatenate(o_t, axis=0).T.astype(BF16)

    return pl.pallas_call(
        body, name="attn_fwd", grid=(Bl, pairs),
        in_specs=[pl.BlockSpec((None, S, pw), lambda b, p: (b, 0, p)),
                  pl.BlockSpec((None, S, pw), lambda b, p: (b, 0, p)),
                  pl.BlockSpec((None, S, vw), lambda b, p: (b, 0, lay["v"] // vw + p))],
        out_specs=[pl.BlockSpec((None, S, vw), lambda b, p: (b, 0, p)),
                   pl.BlockSpec((None, None, 2, S), lambda b, p: (b, p, 0, 0))],
        out_shape=[SDS((Bl, S, BRANCH_W), BF16), SDS((Bl, pairs, 2, S), F32)],
        compiler_params=_params("arbitrary", "arbitrary"),
    )(qa, ka, proj3)


def attn_bwd(qa, ka, proj3, dao, ao, lse, dproj3, lay):
    Bl, S, _ = qa.shape
    tk = ATTN_BLOCK
    nq = S // tk
    pairs = HEADS // 2
    pw = 2 * HEAD_PAD
    vw = 2 * HEAD_DIM

    def body(qa_ref, ka_ref, v_ref, do_ref, o_ref, lse_ref, _, dqa_ref, dka_ref, dv_ref):
        row = lax.broadcasted_iota(jnp.int32, (tk, tk), 0)
        col = lax.broadcasted_iota(jnp.int32, (tk, tk), 1)
        causal = row <= col
        lane8 = lax.broadcasted_iota(jnp.int32, (8, vw), 1)
        lane_s = lax.broadcasted_iota(jnp.int32, (S, vw), 1)
        lane_k = lax.broadcasted_iota(jnp.int32, (tk, vw), 1)
        doo = do_ref[...].astype(F32) * o_ref[...].astype(F32)
        hi = doo.astype(BF16)
        lo = (doo - hi.astype(F32)).astype(BF16)
        delta, v_head = [], []
        for h in range(2):
            sel = jnp.where((lane8 >= h * HEAD_DIM) & (lane8 < (h + 1) * HEAD_DIM), 1.0, 0.0).astype(BF16)
            delta.append((_dot_nt(sel, hi) + _dot_nt(sel, lo))[0:1, :])
            in_head = (lane_s >= h * HEAD_DIM) & (lane_s < (h + 1) * HEAD_DIM)
            v_head.append(jnp.where(in_head, v_ref[...], jnp.zeros_like(v_ref[...])))
        dqa_ref[...] = jnp.zeros_like(dqa_ref)
        for j in range(nq):
            q0 = j * tk
            krows = slice(q0, q0 + tk)
            do = do_ref[q0:, :]
            dvs = []
            for h in range(2):
                hs = slice(h * HEAD_PAD, (h + 1) * HEAD_PAD)
                k = ka_ref[krows, hs]
                q = qa_ref[q0:, hs]
                st = _dot_nt(k, q)
                p = jnp.exp(st - lse_ref[h:h + 1, q0:])
                p_diag = jnp.where(causal, p[:, :tk], 0.0)
                p = jnp.concatenate([p_diag, p[:, tk:]], axis=1) if j < nq - 1 else p_diag
                dvs.append(_dot(p.astype(BF16), do))
                dpt = _dot_nt(v_head[h][krows, :], do)
                ds = (p * (dpt - delta[h][:, q0:])).astype(BF16)
                dka_ref[krows, hs] = _dot(ds, q)
                dqa_ref[q0:, hs] += _dot_tn(ds, k)
            dv_ref[krows, :] = jnp.where(lane_k < HEAD_DIM, dvs[0], dvs[1]).astype(BF16)

    seq = lambda w, c0=0: pl.BlockSpec((None, S, w), lambda b, p: (b, 0, c0 + p))
    return pl.pallas_call(
        body, name="attn_bwd", grid=(Bl, pairs),
        in_specs=[seq(pw), seq(pw), seq(vw, lay["v"] // vw), seq(vw), seq(vw),
                  pl.BlockSpec((None, None, 2, S), lambda b, p: (b, p, 0, 0)), _ANY],
        out_specs=[seq(pw), seq(pw), seq(vw, lay["v"] // vw)],
        out_shape=[SDS((Bl, S, HEADS * HEAD_PAD), F32), SDS((Bl, S, HEADS * HEAD_PAD), F32),
                   SDS(dproj3.shape, BF16)],
        input_output_aliases={6: 2},
        compiler_params=_params("arbitrary", "arbitrary"),
    )(qa, ka, proj3, dao, ao, lse, dproj3)


def attn_post(dqa, dka, proj3, bf_rows, layer, dproj3, cst, lay):
    Bl, S, w = dqa.shape
    ts = ATTN_BLOCK
    ns = S // ts
    qkf = 2 * BRANCH_W + F_PAD

    def body(dqa_ref, dka_ref, f_ref, bf_ref, pqkt_ref, eq_ref, ek_ref, _, dqkf_ref, dbf_ref, carry_ref):
        b, s = pl.program_id(0), pl.program_id(1)

        @pl.when(s == 0)
        def _():
            carry_ref[...] = jnp.zeros_like(carry_ref)

        dqa_v, dka_v = dqa_ref[...], dka_ref[...]
        qh = dqa_v.astype(BF16)
        kh = dka_v.astype(BF16)
        dqkf_ref[:, :BRANCH_W] = _dot(qh, pqkt_ref[:w, :]).astype(BF16)
        dqkf_ref[:, BRANCH_W:2 * BRANCH_W] = _dot(kh, pqkt_ref[w:, :]).astype(BF16)
        ql = (dqa_v - qh.astype(F32)).astype(BF16)
        kl = (dka_v - kh.astype(F32)).astype(BF16)
        d_f = (_dot(qh, eq_ref[...]) + _dot(ql, eq_ref[...])) + (_dot(kh, ek_ref[...]) + _dot(kl, ek_ref[...]))
        r = lax.broadcasted_iota(jnp.int32, (ts, ts), 0)
        c = lax.broadcasted_iota(jnp.int32, (ts, ts), 1)
        triu = jnp.where(c >= r, 1.0, 0.0).astype(BF16)
        rev = carry_ref[...]
        for part in _split3(d_f):
            rev = rev + _dot(triu, part)
        carry_ref[...] = rev[0:1, :]
        z = f_ref[...].astype(F32) + bf_ref[...]
        lane = lax.broadcasted_iota(jnp.int32, (ts, LANES), 1)
        dfl = jnp.where(lane < HEADS, rev / (1.0 + jnp.exp(z)), 0.0)
        dqkf_ref[:, 2 * BRANCH_W:] = jnp.concatenate(
            [dfl.astype(BF16), jnp.zeros((ts, F_PAD - LANES), BF16)], axis=1)
        part = jnp.sum(dfl, axis=0, keepdims=True)

        @pl.when((b == 0) & (s == 0))
        def _():
            dbf_ref[...] = part

        @pl.when((b > 0) | (s > 0))
        def _():
            dbf_ref[...] += part

    assert lay["q"] % qkf == 0
    cfull = lambda shape: pl.BlockSpec(shape, lambda b, s: (0,) * len(shape))
    rev_blk = lambda wd, c0=0: pl.BlockSpec((None, ts, wd), lambda b, s: (b, ns - 1 - s, c0))
    return pl.pallas_call(
        body, name="attn_post", grid=(Bl, ns),
        in_specs=[rev_blk(w), rev_blk(w), rev_blk(LANES, lay["f"] // LANES),
                  pl.BlockSpec((None, 1, LANES), lambda b, s: (layer, 0, 0)),
                  cfull((2 * w, BRANCH_W)), cfull((w, LANES)), cfull((w, LANES)), _ANY],
        out_specs=[rev_blk(qkf, lay["q"] // qkf), cfull((1, LANES))],
        out_shape=[SDS(dproj3.shape, BF16), SDS((1, LANES), F32)],
        scratch_shapes=[pltpu.VMEM((1, LANES), F32)],
        input_output_aliases={7: 0},
        compiler_params=_params("arbitrary", "arbitrary"),
    )(dqa, dka, proj3, bf_rows, cst["pqkt"], cst["eq"], cst["ek"], dproj3)


def _shift_down(x, k, row):
    return jnp.where(row >= k, pltpu.roll(x, k, axis=0), 0.0)


def _shift_up(x, k, row):
    n = x.shape[0]
    return jnp.where(row < n - k, pltpu.roll(x, n - k, axis=0), 0.0)


def _window_sum(x, g, row, shift):
    s2 = x + shift(x, 1, row)
    s4 = s2 + shift(s2, 2, row)
    s8 = s4 + shift(s4, 4, row)
    s16 = s8 + shift(s8, 8, row)
    return jnp.where(g == 0, s2, jnp.where(g == 1, s4, jnp.where(g == 2, s8, s16)))


def _window_count(g, row):
    wnd = jnp.where(g == 0, 2, jnp.where(g == 1, 4, jnp.where(g == 2, 8, 16)))
    return jnp.minimum(row + 1, wnd).astype(F32)


def _group_columns(ref):
    return [ref[:, n * GROUP_W:(n + 1) * GROUP_W].astype(F32) for n in range(4)]


def poolconv_fwd(proj3, pool_w, pool_scale, conv_w, layer, lay):
    Bl, S, _ = proj3.shape

    def body(x_ref, pw_ref, ps_ref, cw_ref, po_ref, co_ref):
        g = pl.program_id(1)
        row = lax.broadcasted_iota(jnp.int32, (S, GROUP_W), 0)
        u, cv, cb, cc = _group_columns(x_ref)
        d = _window_sum(u, g, row, _shift_down) / _window_count(g, row) - u
        po_ref[...] = (_dot(d.astype(BF16), pw_ref[...]) * ps_ref[...]).astype(BF16)
        z = cc * cv
        y = cw_ref[0:1, :] * _shift_down(z, 2, row) + cw_ref[1:2, :] * _shift_down(z, 1, row) + cw_ref[2:3, :] * z
        co_ref[...] = (cb * y).astype(BF16)

    out = pl.BlockSpec((None, S, GROUP_W), lambda b, g: (b, 0, g))
    return pl.pallas_call(
        body, name="poolconv_fwd", grid=(Bl, N_GROUPS),
        in_specs=[pl.BlockSpec((None, S, BRANCH_W), lambda b, g: (b, 0, lay["pc"] // BRANCH_W + g)),
                  pl.BlockSpec((None, None, GROUP_W, GROUP_W), lambda b, g: (layer, g, 0, 0)),
                  pl.BlockSpec((None, 1, GROUP_W), lambda b, g: (layer, 0, g)),
                  pl.BlockSpec((None, None, 3, GROUP_W), lambda b, g: (g, layer, 0, 0))],
        out_specs=[out, out],
        out_shape=[SDS((Bl, S, BRANCH_W), BF16), SDS((Bl, S, BRANCH_W), BF16)],
        compiler_params=_params("arbitrary", "arbitrary"),
    )(proj3, pool_w, pool_scale, conv_w)


def poolconv_bwd(proj3, dpo, dco, pool_w, pool_scale, conv_w, layer, dproj3, dcw_stacked, lay):
    Bl, S, _ = proj3.shape

    def body(x_ref, dpo_ref, dco_ref, pw_ref, ps_ref, cw_ref, *rest):
        dx_ref, dpw_ref, dps_ref, dcw_ref, dcw_acc = rest[-5:]
        g, b = pl.program_id(0), pl.program_id(1)
        row = lax.broadcasted_iota(jnp.int32, (S, GROUP_W), 0)
        cnt = _window_count(g, row)
        u, cv, cb, cc = _group_columns(x_ref)
        d = (_window_sum(u, g, row, _shift_down) / cnt - u).astype(BF16)
        pw = pw_ref[...]
        ypre = _dot(d, pw)
        dpo_v = dpo_ref[...].astype(F32)
        dps = jnp.sum(dpo_v * ypre, axis=0, keepdims=True)
        dyp = (dpo_v * ps_ref[...]).astype(BF16)
        dpw = _dot_tn(d, dyp)
        dd = _dot_nt(dyp, pw)
        dx_ref[:, 0:GROUP_W] = (_window_sum(dd / cnt, g, row, _shift_up) - dd).astype(BF16)

        z = cc * cv
        z1, z2 = _shift_down(z, 1, row), _shift_down(z, 2, row)
        w0, w1, w2 = cw_ref[0:1, :], cw_ref[1:2, :], cw_ref[2:3, :]
        y = w0 * z2 + w1 * z1 + w2 * z
        dco_v = dco_ref[...].astype(F32)
        dy = dco_v * cb
        dz = w0 * _shift_up(dy, 2, row) + w1 * _shift_up(dy, 1, row) + w2 * dy
        dx_ref[:, GROUP_W:2 * GROUP_W] = (dz * cc).astype(BF16)
        dx_ref[:, 2 * GROUP_W:3 * GROUP_W] = (dco_v * y).astype(BF16)
        dx_ref[:, 3 * GROUP_W:] = (dz * cv).astype(BF16)
        dcw = jnp.concatenate([jnp.sum(dy * z2, axis=0, keepdims=True),
                               jnp.sum(dy * z1, axis=0, keepdims=True),
                               jnp.sum(dy * z, axis=0, keepdims=True)], axis=0)

        @pl.when(b == 0)
        def _():
            dpw_ref[...] = dpw
            dps_ref[...] = dps
            dcw_acc[...] = dcw

        @pl.when(b > 0)
        def _():
            dpw_ref[...] += dpw
            dps_ref[...] += dps
            dcw_acc[...] += dcw

        @pl.when(b == Bl - 1)
        def _():
            dcw_ref[...] = dcw_acc[...].astype(BF16)

    blk = pl.BlockSpec((None, S, GROUP_W), lambda g, b: (b, 0, g))
    pc = pl.BlockSpec((None, S, BRANCH_W), lambda g, b: (b, 0, lay["pc"] // BRANCH_W + g))
    ins = [proj3, dpo, dco, pool_w, pool_scale, conv_w, dproj3]
    in_specs = [pc, blk, blk,
                pl.BlockSpec((None, None, GROUP_W, GROUP_W), lambda g, b: (layer, g, 0, 0)),
                pl.BlockSpec((None, 1, GROUP_W), lambda g, b: (layer, 0, g)),
                pl.BlockSpec((None, None, 3, GROUP_W), lambda g, b: (g, layer, 0, 0)), _ANY]
    aliases = {6: 0}
    if dcw_stacked is not None:
        ins.append(dcw_stacked)
        in_specs.append(_ANY)
        aliases[7] = 3
    return pl.pallas_call(
        body, name="poolconv_bwd", grid=(N_GROUPS, Bl),
        in_specs=in_specs,
        out_specs=[pc, pl.BlockSpec((None, GROUP_W, GROUP_W), lambda g, b: (g, 0, 0)),
                   pl.BlockSpec((1, GROUP_W), lambda g, b: (0, g)),
                   pl.BlockSpec((None, None, 3, GROUP_W), lambda g, b: (layer, g, 0, 0))],
        out_shape=[SDS(dproj3.shape, BF16), SDS((N_GROUPS, GROUP_W, GROUP_W), F32), SDS((1, BRANCH_W), F32),
                   SDS((N_LAYERS, N_CHIPS, 3, GROUP_W), BF16)],
        scratch_shapes=[pltpu.VMEM((3, GROUP_W), F32)],
        input_output_aliases=aliases,
        compiler_params=_params("arbitrary", "arbitrary"),
    )(*ins)


def _tile_2d(rows, cols, n_arrays):
    budget = VMEM_LIMIT // 2
    lanes = -(-cols // LANES) * LANES
    if rows % 8 == 0:
        for t in (2048, 1024, 512, 256, 128, 64, 32, 16, 8):
            if rows % t == 0 and 2 * n_arrays * t * lanes * 4 <= budget:
                return t, cols
    for t in (1024, 512, 256, 128):
        if cols % t == 0 and 2 * n_arrays * (rows + 8) * t * 4 <= budget:
            return rows, t
    return rows, cols


def add_pair(kept, core, received, name):
    _, n, R, C = kept.shape
    tr, tc = _tile_2d(R, C, 3)

    def body(core_ref, a_ref, b_ref, o_ref):
        o_ref[...] = (a_ref[...].astype(F32) + b_ref[...].astype(F32)).astype(BF16)

    blk = pl.BlockSpec((None, tr, tc), lambda d, i, j, core_ref: (d, i, j))
    grid_spec = pltpu.PrefetchScalarGridSpec(
        num_scalar_prefetch=1, grid=(n, R // tr, C // tc),
        in_specs=[pl.BlockSpec((None, None, tr, tc), lambda d, i, j, core_ref: (core_ref[0], d, i, j)), blk],
        out_specs=blk)
    return pl.pallas_call(body, name=name, grid_spec=grid_spec, out_shape=SDS((n, R, C), BF16),
                          compiler_params=_params("arbitrary", "arbitrary", "arbitrary"))(core, kept, received)


def add_chips(parts, core, name):
    _, R, C = parts.shape
    tr, tc = _tile_2d(R, C, 4)

    def body(core_ref, p_ref, o_ref):
        acc = p_ref[0].astype(F32)
        for j in range(1, N_CHIPS):
            acc = acc + p_ref[j].astype(F32)
        o_ref[...] = acc

    grid_spec = pltpu.PrefetchScalarGridSpec(
        num_scalar_prefetch=1, grid=(R // tr, C // tc),
        in_specs=[pl.BlockSpec((N_CHIPS, tr, tc), lambda i, j, core_ref: (0, i, j))],
        out_specs=pl.BlockSpec((None, tr, tc), lambda i, j, core_ref: (core_ref[0], i, j)))
    return pl.pallas_call(body, name=name, grid_spec=grid_spec, out_shape=SDS((2, R, C), F32),
                          compiler_params=_params("arbitrary", "arbitrary"))(core, parts)


def adamw(w, g, m, v, name):
    if w.ndim == 2:
        R, C = w.shape
        tr, _ = _tile_2d(R, C, 7)
        grid, blk = (R // tr,), pl.BlockSpec((tr, C), lambda i: (i, 0))
    else:
        N, r, C = w.shape
        tn = max(t for t in range(1, N + 1) if N % t == 0 and t * r * C * 4 <= 512 * 1024)
        grid, blk = (N // tn,), pl.BlockSpec((tn, r, C), lambda i: (i, 0, 0))

    def body(w_ref, g_ref, m_ref, v_ref, d_ref, nm_ref, nv_ref):
        gv = g_ref[...]
        m_new = ADAM_B1 * m_ref[...] + (1.0 - ADAM_B1) * gv
        v_new = ADAM_B2 * v_ref[...] + (1.0 - ADAM_B2) * (gv * gv)
        m_hat = m_new / (1.0 - ADAM_B1 ** ADAM_STEP)
        v_hat = v_new / (1.0 - ADAM_B2 ** ADAM_STEP)
        d_ref[...] = -ADAM_LR * (m_hat / (jnp.sqrt(v_hat) + ADAM_EPS) + ADAM_WD * w_ref[...])
        nm_ref[...] = m_new
        nv_ref[...] = v_new

    out = SDS(w.shape, F32)
    return pl.pallas_call(body, name=name, grid=grid, in_specs=[blk] * 4, out_specs=[blk] * 3,
                          out_shape=[out, out, out], compiler_params=_params("arbitrary"))(w, g, m, v)


def _position():
    return lax.axis_index("x"), lax.axis_index("y"), lax.axis_index("c")


def _other_chips(x, y):
    return [(1 - x, y), (x, 1 - y), (1 - x, 1 - y)]


def _remote(src, dst, send_sem, recv_sem, device):
    return pltpu.make_async_remote_copy(src_ref=src, dst_ref=dst, send_sem=send_sem, recv_sem=recv_sem,
                                        device_id=device, device_id_type=MESH)


_COMM = pltpu.CompilerParams(has_side_effects=True)


def allgather_weights(shards, row_sharded):
    n = len(shards)
    me_chip = 2 * lax.axis_index("x") + lax.axis_index("y")
    bufs = []
    for t, sh in enumerate(shards):
        L, r, c = sh.shape
        if row_sharded[t]:
            bufs.append(lax.dynamic_update_slice(lax.empty((L, N_CHIPS, r, c), sh.dtype), sh[:, None],
                                                 (0, me_chip, 0, 0)))
        else:
            bufs.append(lax.dynamic_update_slice(lax.empty((N_CHIPS, L, r, c), sh.dtype), sh[None],
                                                 (me_chip, 0, 0, 0)))

    def window(ref, t, chip, layer):
        return ref.at[layer, chip] if row_sharded[t] else ref.at[chip, layer]

    def body(*refs):
        outs = refs[n:2 * n]
        send_sems, recv_sems, fwd_send, fwd_recv = refs[2 * n:]
        x, y, c = _position()
        me = 2 * x + y
        others = _other_chips(x, y)
        sends = []
        for t in range(n):
            for k, (px, py) in enumerate(others):
                win = window(outs[t], t, me, c)
                cp = _remote(win, win, send_sems.at[t, k], recv_sems.at[t, k], (px, py, c))
                cp.start()
                sends.append(cp)
        for t in range(n):
            for k, (px, py) in enumerate(others):
                win = window(outs[t], t, 2 * px + py, c)
                _remote(win, win, send_sems.at[t, k], recv_sems.at[t, k], (px, py, c)).wait_recv()
                cp = _remote(win, win, fwd_send.at[t, k], fwd_recv.at[t, k], (x, y, 1 - c))
                cp.start()
                sends.append(cp)
        for t in range(n):
            for k, (px, py) in enumerate(others):
                win = window(outs[t], t, 2 * px + py, 1 - c)
                _remote(win, win, fwd_send.at[t, k], fwd_recv.at[t, k], (x, y, 1 - c)).wait_recv()
        for cp in sends:
            cp.wait_send()

    sem = pltpu.SemaphoreType.DMA
    return pl.pallas_call(
        body, name="allgather_weights", in_specs=[_ANY] * n, out_specs=[_ANY] * n,
        out_shape=[SDS(b.shape, b.dtype) for b in bufs],
        scratch_shapes=[sem((n, 3)), sem((n, 3)), sem((n, 3)), sem((n, 3))],
        input_output_aliases={t: t for t in range(n)},
        compiler_params=_COMM,
    )(*bufs)


def allgather_chips(buf, name):
    def body(src_ref, out_ref, send_sems, recv_sems, local_sem):
        x, y, c = _position()
        me = 2 * x + y
        mine = pltpu.make_async_copy(src_ref, out_ref.at[me], local_sem)
        mine.start()
        sends = []
        for k, (px, py) in enumerate(_other_chips(x, y)):
            cp = _remote(src_ref, out_ref.at[me], send_sems.at[k], recv_sems.at[k], (px, py, c))
            cp.start()
            sends.append(cp)
        for k, (px, py) in enumerate(_other_chips(x, y)):
            _remote(src_ref, out_ref.at[2 * px + py], send_sems.at[k], recv_sems.at[k], (px, py, c)).wait_recv()
        for cp in sends:
            cp.wait_send()
        mine.wait()

    sem = pltpu.SemaphoreType.DMA
    return pl.pallas_call(
        body, name=name, in_specs=[_ANY], out_specs=_ANY, out_shape=SDS((N_CHIPS,) + buf.shape, buf.dtype),
        scratch_shapes=[sem((3,)), sem((3,)), sem], compiler_params=_COMM,
    )(buf)


def swap_sibling(tensors, name):
    n = len(tensors)

    def body(*refs):
        srcs, outs, send_sems, recv_sems = refs[:n], refs[n:2 * n], refs[2 * n], refs[2 * n + 1]
        x, y, c = _position()
        cps = [_remote(srcs[t].at[1 - c], outs[t], send_sems.at[t], recv_sems.at[t], (x, y, 1 - c))
               for t in range(n)]
        for cp in cps:
            cp.start()
        for cp in cps:
            cp.wait()

    sem = pltpu.SemaphoreType.DMA
    return pl.pallas_call(
        body, name=name, in_specs=[_ANY] * n, out_specs=[_ANY] * n,
        out_shape=[SDS(t.shape[1:], t.dtype) for t in tensors],
        scratch_shapes=[sem((n,)), sem((n,))], compiler_params=_COMM,
    )(*tensors)


def exchange_chips(tensors, name):
    n = len(tensors)

    def body(*refs):
        srcs, outs = refs[:n], refs[n:2 * n]
        send_sems, recv_sems, local_sems = refs[2 * n:]
        x, y, c = _position()
        me = 2 * x + y
        others = _other_chips(x, y)
        cps = []
        for t in range(n):
            cp = pltpu.make_async_copy(srcs[t].at[me], outs[t].at[me], local_sems.at[t])
            cp.start()
            cps.append(cp)
        sends = []
        for t in range(n):
            for k, (px, py) in enumerate(others):
                cp = _remote(srcs[t].at[2 * px + py], outs[t].at[me], send_sems.at[t, k], recv_sems.at[t, k],
                             (px, py, c))
                cp.start()
                sends.append(cp)
        for t in range(n):
            for k, (px, py) in enumerate(others):
                _remote(srcs[t].at[me], outs[t].at[2 * px + py], send_sems.at[t, k], recv_sems.at[t, k],
                        (px, py, c)).wait_recv()
        for cp in sends:
            cp.wait_send()
        for cp in cps:
            cp.wait()

    sem = pltpu.SemaphoreType.DMA
    return pl.pallas_call(
        body, name=name, in_specs=[_ANY] * n, out_specs=[_ANY] * n,
        out_shape=[SDS(t.shape, t.dtype) for t in tensors],
        scratch_shapes=[sem((n, 3)), sem((n, 3)), sem((n,))], compiler_params=_COMM,
    )(*tensors)


def join_halves(tensors, name):
    n = len(tensors)

    def body(*refs):
        outs, send_sems, recv_sems = refs[n:2 * n], refs[2 * n], refs[2 * n + 1]
        x, y, c = _position()
        sib = (x, y, 1 - c)
        sends = []
        for t in range(n):
            cp = _remote(outs[t].at[c], outs[t].at[c], send_sems.at[t], recv_sems.at[t], sib)
            cp.start()
            sends.append(cp)
        for t in range(n):
            _remote(outs[t].at[c], outs[t].at[1 - c], send_sems.at[t], recv_sems.at[t], sib).wait_recv()
        for cp in sends:
            cp.wait_send()

    sem = pltpu.SemaphoreType.DMA
    return pl.pallas_call(
        body, name=name, in_specs=[_ANY] * n, out_specs=[_ANY] * n,
        out_shape=[SDS(t.shape, t.dtype) for t in tensors],
        scratch_shapes=[sem((n,)), sem((n,))], input_output_aliases={t: t for t in range(n)},
        compiler_params=_COMM,
    )(*tensors)


BIG = ("w_in", "w_proj_attn", "w_proj_pool", "w_proj_conv", "conv_w", "w_out", "w_gate_up", "w_down")
ROW_SHARDED = ("w_out", "w_down")
REPLICATED = ("attn_norm", "b_forget", "b_gate", "pool_w", "pool_scale", "ffn_norm", "final_norm")
ORDER = ("attn_norm", "w_in", "b_forget", "b_gate", "w_proj_attn", "pool_w", "pool_scale", "w_proj_pool",
         "conv_w", "w_proj_conv", "w_out", "ffn_norm", "w_gate_up", "w_down", "final_norm")


def _proj_layout(D):
    lay = {"g": 0, "q": 3 * D}
    lay["k"] = lay["q"] + BRANCH_W
    lay["f"] = lay["k"] + BRANCH_W
    lay["v"] = lay["f"] + F_PAD
    lay["pc"] = lay["v"] + BRANCH_W
    lay["width"] = lay["pc"] + 4 * BRANCH_W
    return lay


_REF = dict(q=0, k=512, v=1024, f=1536, u=1544, cv=2056, cb=2568, cc=3080, g=3592)


def _packed_pieces(D):
    pieces = [(_REF["g"], 3 * D), (_REF["q"], BRANCH_W), (_REF["k"], BRANCH_W), (_REF["f"], HEADS),
              (None, F_PAD - HEADS), (_REF["v"], BRANCH_W)]
    for gi in range(N_GROUPS):
        pieces += [(_REF[name] + gi * GROUP_W, GROUP_W) for name in ("u", "cv", "cb", "cc")]
    return pieces


def _pack_w_in_rows(shards):
    _, L, cs, D = shards.shape
    parts = []
    for start, n in _packed_pieces(D):
        if start is None:
            parts.append(jnp.zeros((L, n, D), shards.dtype))
        while start is not None and n:
            chip, off = divmod(start, cs)
            take = min(n, cs - off)
            parts.append(shards[chip, :, off:off + take, :])
            start, n = start + take, n - take
    return jnp.concatenate(parts, axis=1)


def _unpack_w_in_rows(p, D):
    lay = _proj_layout(D)
    rows = lambda a, n: p[:, a:a + n, :]
    kinds = []
    for kind in range(4):
        kinds += [rows(lay["pc"] + gi * BRANCH_W + kind * GROUP_W, GROUP_W) for gi in range(N_GROUPS)]
    return jnp.concatenate([rows(lay["q"], BRANCH_W), rows(lay["k"], BRANCH_W), rows(lay["v"], BRANCH_W),
                            rows(lay["f"], HEADS)] + kinds + [rows(0, 3 * D)], axis=1)


def _split_flat(vec, shapes):
    out, at = [], 0
    for shp in shapes:
        n = int(np.prod(shp))
        out.append(vec[at:at + n].reshape(shp))
        at += n
    return out


def kernel(x, attn_norm, w_in, b_forget, b_gate, w_proj_attn, pool_w, pool_scale, w_proj_pool, conv_w, w_proj_conv, w_out, ffn_norm, w_gate_up, w_down, final_norm, loss_target, m_attn_norm, m_w_in, m_b_forget, m_b_gate, m_w_proj_attn, m_pool_w, m_pool_scale, m_w_proj_pool, m_conv_w, m_w_proj_conv, m_w_out, m_ffn_norm, m_w_gate_up, m_w_down, m_final_norm, v_attn_norm, v_w_in, v_b_forget, v_b_gate, v_w_proj_attn, v_pool_w, v_pool_scale, v_w_proj_pool, v_conv_w, v_w_proj_conv, v_w_out, v_ffn_norm, v_w_gate_up, v_w_down, v_final_norm):
    weights = dict(attn_norm=attn_norm, w_in=w_in, b_forget=b_forget, b_gate=b_gate, w_proj_attn=w_proj_attn,
                   pool_w=pool_w, pool_scale=pool_scale, w_proj_pool=w_proj_pool, conv_w=conv_w,
                   w_proj_conv=w_proj_conv, w_out=w_out, ffn_norm=ffn_norm, w_gate_up=w_gate_up, w_down=w_down,
                   final_norm=final_norm)
    mom_m = dict(attn_norm=m_attn_norm, w_in=m_w_in, b_forget=m_b_forget, b_gate=m_b_gate, w_proj_attn=m_w_proj_attn,
                 pool_w=m_pool_w, pool_scale=m_pool_scale, w_proj_pool=m_w_proj_pool, conv_w=m_conv_w,
                 w_proj_conv=m_w_proj_conv, w_out=m_w_out, ffn_norm=m_ffn_norm, w_gate_up=m_w_gate_up,
                 w_down=m_w_down, final_norm=m_final_norm)
    mom_v = dict(attn_norm=v_attn_norm, w_in=v_w_in, b_forget=v_b_forget, b_gate=v_b_gate, w_proj_attn=v_w_proj_attn,
                 pool_w=v_pool_w, pool_scale=v_pool_scale, w_proj_pool=v_w_proj_pool, conv_w=v_conv_w,
                 w_proj_conv=v_w_proj_conv, w_out=v_w_out, ffn_norm=v_ffn_norm, w_gate_up=v_w_gate_up,
                 w_down=v_w_down, final_norm=v_final_norm)

    Bl, S, D = x.shape
    T = Bl * S
    L = w_in.shape[0]
    F = w_down.shape[1] * N_CHIPS
    lay = _proj_layout(D)
    cst = _placement_constants()
    assert L == N_LAYERS and S % ATTN_BLOCK == 0 and F % (2 * LANES) == 0 and D % BRANCH_W == 0
    assert w_in.shape[2] * N_CHIPS == _REF["g"] + 3 * D and conv_w.shape[2] == GROUP_W

    send = {n: weights[n].astype(BF16) for n in BIG}
    send["conv_w"] = conv_w
    send["w_in"] = w_in.transpose(0, 2, 1).astype(BF16)
    gathered = dict(zip(BIG, allgather_weights([send[n] for n in BIG], [n in ROW_SHARDED for n in BIG])))
    w_in_p = _pack_w_in_rows(gathered["w_in"])
    w_down_f = gathered["w_down"].reshape(L, F, D)
    w_gu, w_o, conv_w_g = gathered["w_gate_up"], gathered["w_out"], gathered["conv_w"]
    wpa, wpp, wpc = gathered["w_proj_attn"], gathered["w_proj_pool"], gathered["w_proj_conv"]
    pool_w_b = pool_w.astype(BF16)
    an3, fn3 = attn_norm.reshape(L, 1, D), ffn_norm.reshape(L, 1, D)
    bg3, ps3 = b_gate.reshape(L, 1, 3 * D), pool_scale.reshape(L, 1, BRANCH_W)
    bf3 = jnp.pad(b_forget, ((0, 0), (0, LANES - HEADS))).reshape(L, 1, LANES)

    xs = x.reshape(T, D)
    saved = []
    for l in range(L):
        proj, h = norm_matmul(xs, an3, w_in_p, l, "rows", "in_proj")
        proj3 = proj.reshape(Bl, S, lay["width"])
        qa, ka = attn_prep(proj3, bf3, l, cst, lay)
        ao, lse = attn_fwd(qa, ka, proj3, lay)
        po, co = poolconv_fwd(proj3, pool_w_b, ps3, conv_w_g, l, lay)
        ao2, po2, co2 = (a.reshape(T, BRANCH_W) for a in (ao, po, co))
        x1, ys, mixed = mix_fwd(ao2, po2, co2, proj, bg3, wpa, wpp, wpc, w_o, l, xs)
        ab, h2 = norm_matmul(x1, fn3, w_gu, l, "by_shard", "gate_up_proj")
        x2, s_act = ffn_down_fwd(ab, w_down_f, l, x1)
        saved.append(dict(x=xs, proj=proj, proj3=proj3, h=h, qa=qa, ka=ka, ao=ao, lse=lse, ao2=ao2, po2=po2,
                          co2=co2, ys=ys, mixed=mixed, x1=x1, ab=ab, h2=h2, s=s_act))
        xs = x2

    loss_row, dx, dxb, g_final = loss_head(xs, final_norm.reshape(1, D), loss_target.reshape(T, D))
    loss = lax.psum(loss_row[0, 0], AXES)

    stacked = {n: None for n in BIG}
    small = {n: [None] * L for n in REPLICATED if n != "final_norm"}
    to3 = lambda a: a.reshape(Bl, S, -1)
    for l in reversed(range(L)):
        sv = saved[l]
        da, db = ffn_down_bwd(dxb, w_down_f, l, sv["ab"])
        stacked["w_down"] = matmul_tn(sv["s"], [dxb], "grad_w_down", l, stacked["w_down"])
        stacked["w_gate_up"] = matmul_tn(sv["h2"], [da, db], "grad_w_gate_up", l, stacked["w_gate_up"],
                                         by_dest=True, tn=2 * F // N_CHIPS, tk=_tile(T, (1024, 512, 256)))
        dx1, dx1b, g_fn = matmul_nt_normbwd([da, db], w_gu, l, "by_shard", sv["x1"], fn3, dx, "gate_up_bwd")
        small["ffn_norm"][l] = g_fn[0]
        dys, dproj, dao, dpo, dco, g_bg = mix_bwd(dx1b, w_o, sv["proj"], bg3, sv["ys"], wpa, wpp, wpc, l,
                                                  lay["width"])
        small["b_gate"][l] = g_bg[0]
        stacked["w_out"] = matmul_tn(sv["mixed"], [dx1b], "grad_w_out", l, stacked["w_out"])
        for n, (name, br) in enumerate((("w_proj_attn", sv["ao2"]), ("w_proj_pool", sv["po2"]),
                                        ("w_proj_conv", sv["co2"]))):
            stacked[name] = matmul_tn(br, [dys], "grad_" + name, l, stacked[name], b_col0=n * D, n_cols=D,
                                      by_dest=True, tn=D // N_CHIPS)
        dqa, dka, dproj3 = attn_bwd(sv["qa"], sv["ka"], sv["proj3"], to3(dao), sv["ao"], sv["lse"], to3(dproj), lay)
        dproj3, g_bf = attn_post(dqa, dka, sv["proj3"], bf3, l, dproj3, cst, lay)
        small["b_forget"][l] = g_bf[0, :HEADS]
        dproj3, g_pw, g_ps, stacked["conv_w"] = poolconv_bwd(sv["proj3"], to3(dpo), to3(dco), pool_w_b, ps3,
                                                             conv_w_g, l, dproj3, stacked["conv_w"], lay)
        small["pool_w"][l], small["pool_scale"][l] = g_pw, g_ps[0]
        dproj = dproj3.reshape(T, lay["width"])
        stacked["w_in"] = matmul_tn(dproj, [sv["h"]], "grad_w_in", l, stacked["w_in"])
        dx, dxb, g_an = matmul_nt_normbwd([dproj], w_in_p, l, "rows", sv["x"], an3, dx1, "in_proj_bwd")
        small["attn_norm"][l] = g_an[0]
    grad_x = dx.reshape(Bl, S, D)

    by_dest = dict(stacked)
    by_dest["w_in"] = _unpack_w_in_rows(stacked["w_in"], D).reshape(L, N_CHIPS, -1, D)
    by_dest["w_out"] = stacked["w_out"].reshape(L, N_CHIPS, D // N_CHIPS, D)
    by_dest["w_down"] = stacked["w_down"].reshape(L, N_CHIPS, F // N_CHIPS, D)
    small_shapes = [weights[n].shape for n in REPLICATED]
    small_vec = jnp.concatenate([jnp.stack(small[n]).reshape(-1) for n in REPLICATED[:-1]] + [g_final[0]])
    n_small = small_vec.shape[0]
    small_vec = jnp.pad(small_vec, (0, -n_small % (2 * N_CHIPS * 16 * LANES))).astype(BF16)
    names = BIG + ("small",)
    by_dest["small"] = small_vec.reshape(N_CHIPS, 2, -1, LANES).transpose(1, 0, 2, 3)
    core = lax.axis_index("c").astype(jnp.int32).reshape(1)
    received = swap_sibling([by_dest[n] for n in names], "swap_grad_layers")
    chip_sum = [add_pair(by_dest[n], core, r, "add_pair_" + n) for n, r in zip(names, received)]
    arrived = exchange_chips(chip_sum, "exchange_grad_chips")
    reduced = join_halves([add_chips(a, core, "add_chips_" + n) for n, a in zip(names, arrived)],
                          "join_grad_layers")
    shard_grads = dict(zip(names, reduced))
    small_all = allgather_chips(shard_grads.pop("small"), "allgather_small_grads").reshape(-1)[:n_small]
    rep_grads = dict(zip(REPLICATED, _split_flat(small_all, small_shapes)))

    delta, new_m, new_v = {}, {}, {}
    for n in BIG:
        shp = weights[n].shape
        if n == "w_in":
            view, back = (lambda a: a.transpose(2, 0, 1)), (lambda a: a.transpose(1, 2, 0))
            g = shard_grads[n].transpose(1, 0, 2)
        else:
            view, back = (lambda a: a.reshape(-1, shp[-1])), (lambda a: a.reshape(shp))
            g = view(shard_grads[n])
        d, nm, nv = adamw(view(weights[n]), g, view(mom_m[n]), view(mom_v[n]), "adamw_" + n)
        delta[n], new_m[n], new_v[n], shard_grads[n] = back(d), back(nm), back(nv), back(g)

    def rows(d):
        vec = jnp.concatenate([d[n].reshape(-1) for n in REPLICATED])
        return jnp.pad(vec, (0, -n_small % (8 * LANES))).reshape(-1, LANES)

    outs = adamw(rows(weights), rows(rep_grads), rows(mom_m), rows(mom_v), "adamw_replicated")
    for res, o in zip((delta, new_m, new_v), outs):
        res.update(zip(REPLICATED, _split_flat(o.reshape(-1), small_shapes)))
    all_grads = {**shard_grads, **rep_grads}

    return (loss, grad_x, *[all_grads[n] for n in ORDER], *[delta[n] for n in ORDER],
            *[new_m[n] for n in ORDER], *[new_v[n] for n in ORDER])
```

```python
import numpy as np
import jax
import jax.numpy as jnp
from jax import lax
from jax.experimental import pallas as pl
from jax.experimental.pallas import tpu as pltpu

F32, BF16 = jnp.float32, jnp.bfloat16
SDS = jax.ShapeDtypeStruct
MESH = pl.DeviceIdType.MESH
AXES = ("x", "y", "c")
N_CHIPS = 4
N_LAYERS = 2
LANES = 128
VMEM_LIMIT = 48 * 1024 * 1024

HEADS, HEAD_DIM = 8, 64
HEAD_PAD = 128
BRANCH_W = 512
GROUP_W = 128
N_GROUPS = BRANCH_W // GROUP_W
POOL_WINDOWS = (2, 4, 8, 16)
F_PAD = 512
ATTN_BLOCK = 256
RMS_EPS = 1e-6
NEG_INF = -1e30
ADAM_LR, ADAM_B1, ADAM_B2, ADAM_EPS, ADAM_WD, ADAM_STEP = 0.001, 0.9, 0.999, 1e-08, 0.01, 10

NT = (((1,), (1,)), ((), ()))
TN = (((0,), (0,)), ((), ()))
_ANY = pl.BlockSpec(memory_space=pl.ANY)


def _tile(n, prefs):
    for p in prefs:
        if n % p == 0:
            return p
    raise ValueError(f"no tile of {prefs} divides {n}")


def _params(*sem):
    return pltpu.CompilerParams(dimension_semantics=sem, vmem_limit_bytes=VMEM_LIMIT)


def _sigmoid(z):
    return 0.5 * jnp.tanh(0.5 * z) + 0.5


def _split3(x):
    h1 = x.astype(BF16)
    r1 = x - h1.astype(F32)
    h2 = r1.astype(BF16)
    h3 = (r1 - h2.astype(F32)).astype(BF16)
    return h1, h2, h3


def _position():
    return lax.axis_index("x"), lax.axis_index("y"), lax.axis_index("c")


def _other_chips(x, y):
    return [(1 - x, y), (x, 1 - y), (1 - x, 1 - y)]


def _remote(src, dst, send_sem, recv_sem, device):
    return pltpu.make_async_remote_copy(src_ref=src, dst_ref=dst, send_sem=send_sem, recv_sem=recv_sem,
                                        device_id=device, device_id_type=MESH)


ROW_SHARDED = ("w_out", "w_down")
FETCHER = dict(w_in=0, w_out=0, w_proj_attn=0, w_proj_pool=0, w_gate_up=1, w_down=1, w_proj_conv=1, conv_w=1)


class Hosted:
    def __init__(self, pool, jobs):
        self.pool, self.jobs = pool, list(jobs)
        self.names = sorted({name for _, name, _ in self.jobs})


def _hosted_plan(hosted, refs, send_sems, recv_sems):
    x, y, c = _position()
    me = 2 * x + y
    others = _other_chips(x, y)
    plan = []
    for j, (kind, name, layer) in enumerate(hosted.jobs):
        ref = refs[name]
        win = (lambda chip, ref=ref, layer=layer: ref.at[layer, chip]) if name in ROW_SHARDED else (
            lambda chip, ref=ref, layer=layer: ref.at[chip, layer])
        mine = c == FETCHER[name]
        if kind == "ici":
            sends = [_remote(win(me), win(me), send_sems.at[j, k], recv_sems.at[j, k], (px, py, c))
                     for k, (px, py) in enumerate(others)]
            arrivals = [_remote(win(2 * px + py), win(2 * px + py), send_sems.at[j, k], recv_sems.at[j, k], (px, py, c))
                        for k, (px, py) in enumerate(others)]
            plan.append((mine, sends, arrivals, []))
        else:
            sends = [_remote(win(2 * px + py), win(2 * px + py), send_sems.at[j, k], recv_sems.at[j, k], (x, y, 1 - c))
                     for k, (px, py) in enumerate(others)]
            plan.append((mine, sends, [], sends))
    return plan


def _hosted_start(plan, now):
    for mine, sends, _, _ in plan:
        @pl.when(now & mine)
        def _(sends=sends):
            for cp in sends:
                cp.start()


def _hosted_finish(plan, now):
    for mine, sends, arrivals, sibling_arrivals in plan:
        @pl.when(now & mine)
        def _(sends=sends, arrivals=arrivals):
            for cp in arrivals:
                cp.wait_recv()
            for cp in sends:
                cp.wait_send()

        if sibling_arrivals:
            @pl.when(now & jnp.logical_not(mine))
            def _(sibling_arrivals=sibling_arrivals):
                for cp in sibling_arrivals:
                    cp.wait_recv()


def _pcall(body, hosted, *, name, grid, in_specs, out_specs, out_shape, semantics, scratch_shapes=()):
    if hosted is None or not hosted.jobs:
        return pl.pallas_call(body, name=name, grid=grid, in_specs=in_specs, out_specs=out_specs,
                              out_shape=out_shape, scratch_shapes=list(scratch_shapes),
                              compiler_params=_params(*semantics))
    single = not isinstance(out_shape, (list, tuple))
    out_specs_l = [out_specs] if single else list(out_specs)
    out_shape_l = [out_shape] if single else list(out_shape)
    n_in, n_out, n_buf, n_job = len(in_specs), len(out_specs_l), len(hosted.names), len(hosted.jobs)

    def carrying(*refs):
        ins, outs = refs[:n_in], refs[n_in + n_buf:n_in + n_buf + n_out]
        bufs = refs[n_in + n_buf + n_out:n_in + 2 * n_buf + n_out]
        rest = refs[n_in + 2 * n_buf + n_out:]
        scratch, send_sems, recv_sems = rest[:-2], rest[-2], rest[-1]
        first, last = True, True
        for axis, size in enumerate(grid):
            first = first & (pl.program_id(axis) == 0)
            last = last & (pl.program_id(axis) == size - 1)
        plan = _hosted_plan(hosted, dict(zip(hosted.names, bufs)), send_sems, recv_sems)
        _hosted_start(plan, first)
        body(*ins, *outs, *scratch)
        _hosted_finish(plan, last)

    def run(*args):
        bufs = [hosted.pool[n] for n in hosted.names]
        sem = pltpu.SemaphoreType.DMA
        res = pl.pallas_call(
            carrying, name=name, grid=grid, in_specs=list(in_specs) + [_ANY] * n_buf,
            out_specs=out_specs_l + [_ANY] * n_buf,
            out_shape=out_shape_l + [SDS(b.shape, b.dtype) for b in bufs],
            scratch_shapes=list(scratch_shapes) + [sem((n_job, 3)), sem((n_job, 3))],
            input_output_aliases={n_in + i: n_out + i for i in range(n_buf)},
            compiler_params=pltpu.CompilerParams(dimension_semantics=semantics, vmem_limit_bytes=VMEM_LIMIT,
                                                 has_side_effects=True),
        )(*args, *bufs)
        hosted.pool.update(zip(hosted.names, res[n_out:]))
        return res[0] if single else res[:n_out]

    return run


def _dot(a, b):
    return jnp.dot(a, b, preferred_element_type=F32)


def _dot_nt(a, b):
    return lax.dot_general(a, b, NT, preferred_element_type=F32)


def _dot_tn(a, b):
    return lax.dot_general(a, b, TN, preferred_element_type=F32)


def norm_matmul(x, gain, w, layer, kind, name, hosted=None):
    T, D = x.shape
    if kind == "by_shard":
        tn = w.shape[3]
        N = N_CHIPS * tn
        w_spec = pl.BlockSpec((None, None, D, tn), lambda i, j: (j, layer, 0, 0))
        mm = _dot
    else:
        N = w.shape[0]
        tn = _tile(N, (1024, 512, 256, 128))
        w_spec = pl.BlockSpec((tn, D), lambda i, j: (j, 0))
        mm = _dot_nt
    tm = _tile(T, (1024, 512, 256, 128))

    def body(x_ref, g_ref, w_ref, y_ref, h_ref):
        @pl.when(pl.program_id(1) == 0)
        def _():
            xf = x_ref[...]
            r = lax.rsqrt(jnp.mean(xf * xf, axis=-1, keepdims=True) + RMS_EPS)
            h_ref[...] = ((xf * r) * g_ref[...]).astype(BF16)

        y_ref[...] = mm(h_ref[...], w_ref[...]).astype(BF16)

    return _pcall(
        body, hosted, name=name, grid=(T // tm, N // tn),
        in_specs=[pl.BlockSpec((tm, D), lambda i, j: (i, 0)),
                  pl.BlockSpec((None, 1, D), lambda i, j: (layer, 0, 0)),
                  w_spec],
        out_specs=[pl.BlockSpec((tm, tn), lambda i, j: (i, j)),
                   pl.BlockSpec((tm, D), lambda i, j: (i, 0))],
        out_shape=[SDS((T, N), BF16), SDS((T, D), BF16)],
        semantics=("arbitrary", "arbitrary"),
    )(x, gain, w)


def matmul_nt_normbwd(dys, w, layer, kind, x, gain, dres, name):
    T, D = x.shape
    width = dys[0].shape[1]
    if kind == "by_shard":
        tk = w.shape[3]
        w_spec = pl.BlockSpec((None, None, D, tk), lambda i, k: (k, layer, 0, 0))
        mm = _dot_nt
    else:
        tk = _tile(width, (1024, 512, 256, 128))
        w_spec = pl.BlockSpec((tk, D), lambda i, k: (k, 0))
        mm = _dot
    per = width // tk
    nk = per * len(dys)
    tm = _tile(T, (512, 256, 128))
    n_dy = len(dys)

    def dy_spec(p):
        return pl.BlockSpec((tm, tk), lambda i, k: (i, jnp.clip(k - p * per, 0, per - 1)))

    def body(*refs):
        dy_refs = refs[:n_dy]
        w_ref, x_ref, g_ref, dres_ref, dx_ref, dxb_ref, dg_ref, acc_ref = refs[n_dy:]
        i, k = pl.program_id(0), pl.program_id(1)

        @pl.when(k == 0)
        def _():
            acc_ref[...] = jnp.zeros_like(acc_ref)

        for p in range(n_dy):
            @pl.when((k >= p * per) & (k < (p + 1) * per))
            def _(p=p):
                acc_ref[...] += mm(dy_refs[p][...], w_ref[...])

        @pl.when(k == nk - 1)
        def _():
            xf = x_ref[...]
            r = lax.rsqrt(jnp.mean(xf * xf, axis=-1, keepdims=True) + RMS_EPS)
            xhat = xf * r
            dh = acc_ref[...]
            dhg = dh * g_ref[...]
            dx = dres_ref[...] + r * (dhg - xhat * jnp.mean(dhg * xhat, axis=-1, keepdims=True))
            dx_ref[...] = dx
            dxb_ref[...] = dx.astype(BF16)
            part = jnp.sum(dh * xhat, axis=0, keepdims=True)

            @pl.when(i == 0)
            def _():
                dg_ref[...] = part

            @pl.when(i > 0)
            def _():
                dg_ref[...] += part

    row = pl.BlockSpec((tm, D), lambda i, k: (i, 0))
    return pl.pallas_call(
        body, name=name, grid=(T // tm, nk),
        in_specs=[dy_spec(p) for p in range(n_dy)] + [
            w_spec, row, pl.BlockSpec((None, 1, D), lambda i, k: (layer, 0, 0)), row],
        out_specs=[row, row, pl.BlockSpec((1, D), lambda i, k: (0, 0))],
        out_shape=[SDS((T, D), F32), SDS((T, D), BF16), SDS((1, D), F32)],
        scratch_shapes=[pltpu.VMEM((tm, D), F32)],
        compiler_params=_params("arbitrary", "arbitrary"),
    )(*dys, w, x, gain, dres)


def matmul_tn(a, bs, name, layer, stacked, b_col0=0, n_cols=None, by_dest=False, tn=None, tk=None):
    T, M = a.shape
    width = bs[0].shape[1]
    N = n_cols if n_cols else width * len(bs)
    tm = _tile(M, (1024, 512, 256, 128))
    tn = tn or _tile(N, (512, 256, 128))
    tk = tk or _tile(T, (4096, 2048, 1024, 512, 256))
    assert b_col0 % tn == 0 and width % tn == 0
    j0, per, nk, n_b = b_col0 // tn, width // tn, T // tk, len(bs)

    def b_spec(p):
        return pl.BlockSpec((tk, tn), lambda i, j, k: (k, jnp.clip(j0 + j - p * per, 0, per - 1)))

    def body(*refs):
        a_ref, b_refs = refs[0], refs[1:1 + n_b]
        o_ref, acc_ref = refs[-2], refs[-1]
        j, k = pl.program_id(1), pl.program_id(2)

        @pl.when(k == 0)
        def _():
            acc_ref[...] = jnp.zeros_like(acc_ref)

        for p in range(n_b):
            @pl.when((j0 + j >= p * per) & (j0 + j < (p + 1) * per))
            def _(p=p):
                acc_ref[...] += _dot_tn(a_ref[...], b_refs[p][...])

        @pl.when(k == nk - 1)
        def _():
            o_ref[...] = acc_ref[...].astype(BF16)

    if by_dest:
        cs = N // N_CHIPS
        npd = cs // tn
        out_shape = SDS((N_LAYERS, N_CHIPS, M, cs), BF16)
        out_spec = pl.BlockSpec((None, None, tm, tn), lambda i, j, k: (layer, j // npd, i, j % npd))
    else:
        out_shape = SDS((N_LAYERS, M, N), BF16)
        out_spec = pl.BlockSpec((None, tm, tn), lambda i, j, k: (layer, i, j))
    ins = [a] + list(bs)
    in_specs = [pl.BlockSpec((tk, tm), lambda i, j, k: (k, i))] + [b_spec(p) for p in range(n_b)]
    aliases = {}
    if stacked is not None:
        ins.append(stacked)
        in_specs.append(_ANY)
        aliases = {len(ins) - 1: 0}

    def body_wrap(*refs):
        if stacked is not None:
            refs = refs[:1 + n_b] + refs[2 + n_b:]
        body(*refs)

    return pl.pallas_call(
        body_wrap, name=name, grid=(M // tm, N // tn, nk),
        in_specs=in_specs, out_specs=out_spec, out_shape=out_shape,
        scratch_shapes=[pltpu.VMEM((tm, tn), F32)], input_output_aliases=aliases,
        compiler_params=_params("arbitrary", "arbitrary", "arbitrary"),
    )(*ins)


def ffn_down_fwd(ab, w_down, layer, x1, hosted=None):
    T, D = x1.shape
    F = w_down.shape[1]
    tm = _tile(T, (512, 256, 128))
    tk = F // 2
    nk = F // tk

    def body(a_ref, b_ref, w_ref, x_ref, x2_ref, s_ref, acc_ref):
        k = pl.program_id(1)

        @pl.when(k == 0)
        def _():
            acc_ref[...] = x_ref[...]

        a = a_ref[...].astype(F32)
        s = (a * _sigmoid(a) * b_ref[...].astype(F32)).astype(BF16)
        s_ref[...] = s
        acc_ref[...] += _dot(s, w_ref[...])

        @pl.when(k == nk - 1)
        def _():
            x2_ref[...] = acc_ref[...]

    return _pcall(
        body, hosted, name="ffn_down_fwd", grid=(T // tm, nk),
        in_specs=[pl.BlockSpec((tm, tk), lambda i, k: (i, k)),
                  pl.BlockSpec((tm, tk), lambda i, k: (i, nk + k)),
                  pl.BlockSpec((None, tk, D), lambda i, k: (layer, k, 0)),
                  pl.BlockSpec((tm, D), lambda i, k: (i, 0))],
        out_specs=[pl.BlockSpec((tm, D), lambda i, k: (i, 0)),
                   pl.BlockSpec((tm, tk), lambda i, k: (i, k))],
        out_shape=[SDS((T, D), F32), SDS((T, F), BF16)],
        scratch_shapes=[pltpu.VMEM((tm, D), F32)],
        semantics=("arbitrary", "arbitrary"),
    )(ab, ab, w_down, x1)


def ffn_down_bwd(dx2b, w_down, layer, ab):
    T, D = dx2b.shape
    F = w_down.shape[1]
    tm = _tile(T, (512, 256, 128))
    tn = F // 2
    nj = F // tn

    def body(dx_ref, w_ref, a_ref, b_ref, da_ref, db_ref):
        ds = _dot_nt(dx_ref[...], w_ref[...])
        a = a_ref[...].astype(F32)
        sg = _sigmoid(a)
        da_ref[...] = (ds * b_ref[...].astype(F32) * (sg * (1.0 + a * (1.0 - sg)))).astype(BF16)
        db_ref[...] = (ds * (a * sg)).astype(BF16)

    blk = pl.BlockSpec((tm, tn), lambda i, j: (i, j))
    return pl.pallas_call(
        body, name="ffn_down_bwd", grid=(T // tm, nj),
        in_specs=[pl.BlockSpec((tm, D), lambda i, j: (i, 0)),
                  pl.BlockSpec((None, tn, D), lambda i, j: (layer, j, 0)),
                  blk, pl.BlockSpec((tm, tn), lambda i, j: (i, nj + j))],
        out_specs=[blk, blk],
        out_shape=[SDS((T, F), BF16), SDS((T, F), BF16)],
        compiler_params=_params("arbitrary", "arbitrary"),
    )(dx2b, w_down, ab, ab)


def _mix_specs(tm, D, layer):
    cs = D // N_CHIPS
    row = lambda w: pl.BlockSpec((tm, w), lambda i: (i, 0))
    wp = pl.BlockSpec((N_CHIPS, None, BRANCH_W, cs), lambda i: (0, layer, 0, 0))
    wo = pl.BlockSpec((None, N_CHIPS, cs, D), lambda i: (layer, 0, 0, 0))
    bg = pl.BlockSpec((None, 1, 3 * D), lambda i: (layer, 0, 0))
    return row, wp, wo, bg


def mix_fwd(ao, po, co, proj, b_gate, wpa, wpp, wpc, w_out, layer, x, hosted=None):
    T, D = x.shape
    cs = D // N_CHIPS
    tm = _tile(T, (256, 128))
    row, wp, wo, bg = _mix_specs(tm, D, layer)

    def body(ao_ref, po_ref, co_ref, g_ref, bg_ref, wpa_ref, wpp_ref, wpc_ref, wo_ref, x_ref,
             x1_ref, ys_ref, mixed_ref):
        mixed = jnp.zeros((tm, D), F32)
        for n, (br, wp_ref) in enumerate(((ao_ref, wpa_ref), (po_ref, wpp_ref), (co_ref, wpc_ref))):
            y = jnp.concatenate([_dot(br[...], wp_ref[j]) for j in range(N_CHIPS)], axis=1)
            cols = slice(n * D, (n + 1) * D)
            gate = _sigmoid(g_ref[:, cols].astype(F32) + bg_ref[:, cols])
            ys_ref[:, cols] = y.astype(BF16)
            mixed = mixed + gate * y
        mb = mixed.astype(BF16)
        mixed_ref[...] = mb
        acc = x_ref[...]
        for j in range(N_CHIPS):
            acc = acc + _dot(mb[:, j * cs:(j + 1) * cs], wo_ref[j])
        x1_ref[...] = acc

    return _pcall(
        body, hosted, name="mix_fwd", grid=(T // tm,),
        in_specs=[row(BRANCH_W), row(BRANCH_W), row(BRANCH_W), row(3 * D), bg, wp, wp, wp, wo, row(D)],
        out_specs=[row(D), row(3 * D), row(D)],
        out_shape=[SDS((T, D), F32), SDS((T, 3 * D), BF16), SDS((T, D), BF16)],
        semantics=("arbitrary",),
    )(ao, po, co, proj, b_gate, wpa, wpp, wpc, w_out, x)


def mix_bwd(dx1b, w_out, proj, b_gate, ys, wpa, wpp, wpc, layer, width):
    T, D = dx1b.shape
    cs = D // N_CHIPS
    tm = _tile(T, (256, 128))
    row, wp, wo, bg = _mix_specs(tm, D, layer)

    def body(dx_ref, wo_ref, g_ref, bg_ref, ys_ref, wpa_ref, wpp_ref, wpc_ref,
             dys_ref, dg_ref, dao_ref, dpo_ref, dco_ref, dbg_ref):
        i = pl.program_id(0)
        dx = dx_ref[...]
        dmixed = jnp.concatenate([_dot_nt(dx, wo_ref[j]) for j in range(N_CHIPS)], axis=1)
        for n, (wp_ref, dbr) in enumerate(((wpa_ref, dao_ref), (wpp_ref, dpo_ref), (wpc_ref, dco_ref))):
            cols = slice(n * D, (n + 1) * D)
            gate = _sigmoid(g_ref[:, cols].astype(F32) + bg_ref[:, cols])
            dy = (dmixed * gate).astype(BF16)
            dys_ref[:, cols] = dy
            dgp = dmixed * ys_ref[:, cols].astype(F32) * gate * (1.0 - gate)
            dg_ref[:, cols] = dgp.astype(BF16)
            part = jnp.sum(dgp, axis=0, keepdims=True)

            @pl.when(i == 0)
            def _():
                dbg_ref[:, cols] = part

            @pl.when(i > 0)
            def _():
                dbg_ref[:, cols] += part

            acc = jnp.zeros((tm, BRANCH_W), F32)
            for j in range(N_CHIPS):
                acc = acc + _dot_nt(dy[:, j * cs:(j + 1) * cs], wp_ref[j])
            dbr[...] = acc.astype(BF16)

    return pl.pallas_call(
        body, name="mix_bwd", grid=(T // tm,),
        in_specs=[row(D), wo, row(3 * D), bg, row(3 * D), wp, wp, wp],
        out_specs=[row(3 * D), row(3 * D), row(BRANCH_W), row(BRANCH_W), row(BRANCH_W),
                   pl.BlockSpec((1, 3 * D), lambda i: (0, 0))],
        out_shape=[SDS((T, 3 * D), BF16), SDS((T, width), BF16), SDS((T, BRANCH_W), BF16),
                   SDS((T, BRANCH_W), BF16), SDS((T, BRANCH_W), BF16), SDS((1, 3 * D), F32)],
        compiler_params=_params("arbitrary"),
    )(dx1b, w_out, proj, b_gate, ys, wpa, wpp, wpc)


def loss_head(x2, gain, target):
    T, D = x2.shape
    tm = _tile(T, (512, 256, 128))

    def body(x_ref, g_ref, t_ref, loss_ref, dx_ref, dxb_ref, dg_ref):
        i = pl.program_id(0)
        xf = x_ref[...]
        g = g_ref[...]
        r = lax.rsqrt(jnp.mean(xf * xf, axis=-1, keepdims=True) + RMS_EPS)
        xhat = xf * r
        diff = xhat * g - t_ref[...]
        part_loss = 0.5 * jnp.sum(jnp.mean(diff * diff, axis=-1, keepdims=True), axis=0, keepdims=True)
        dy = diff * (1.0 / D)
        dhg = dy * g
        dx = r * (dhg - xhat * jnp.mean(dhg * xhat, axis=-1, keepdims=True))
        dx_ref[...] = dx
        dxb_ref[...] = dx.astype(BF16)
        part_g = jnp.sum(dy * xhat, axis=0, keepdims=True)
        part_l = jnp.broadcast_to(part_loss, (1, LANES))

        @pl.when(i == 0)
        def _():
            dg_ref[...] = part_g
            loss_ref[...] = part_l

        @pl.when(i > 0)
        def _():
            dg_ref[...] += part_g
            loss_ref[...] += part_l

    row = pl.BlockSpec((tm, D), lambda i: (i, 0))
    return pl.pallas_call(
        body, name="loss_head", grid=(T // tm,),
        in_specs=[row, pl.BlockSpec((1, D), lambda i: (0, 0)), row],
        out_specs=[pl.BlockSpec((1, LANES), lambda i: (0, 0)), row, row, pl.BlockSpec((1, D), lambda i: (0, 0))],
        out_shape=[SDS((1, LANES), F32), SDS((T, D), F32), SDS((T, D), BF16), SDS((1, D), F32)],
        compiler_params=_params("arbitrary"),
    )(x2, gain, target)


def _placement_constants():
    w = HEADS * HEAD_PAD
    pq = np.zeros((BRANCH_W, w), np.float32)
    pk = np.zeros((BRANCH_W, w), np.float32)
    pfq = np.zeros((3, LANES, w), np.float32)
    pfk = np.zeros((3, LANES, w), np.float32)
    cq = np.zeros((1, w), np.float32)
    ck = np.zeros((1, w), np.float32)
    eq = np.zeros((w, LANES), np.float32)
    ek = np.zeros((w, LANES), np.float32)
    for h in range(HEADS):
        for d in range(HEAD_DIM):
            pq[h * HEAD_DIM + d, h * HEAD_PAD + d] = HEAD_DIM ** -0.5
            pk[h * HEAD_DIM + d, h * HEAD_PAD + d] = 1.0
        for i in range(3):
            pfq[i, h, h * HEAD_PAD + HEAD_DIM + i] = 1.0
            pfk[i, h, h * HEAD_PAD + HEAD_DIM + 3 + i] = -1.0
            cq[0, h * HEAD_PAD + HEAD_DIM + 3 + i] = 1.0
            ck[0, h * HEAD_PAD + HEAD_DIM + i] = 1.0
        eq[h * HEAD_PAD + HEAD_DIM, h] = 1.0
        ek[h * HEAD_PAD + HEAD_DIM + 3, h] = -1.0
    bf = lambda a: jnp.asarray(a, BF16)
    return dict(pq=bf(pq), pk=bf(pk), pfq=bf(pfq), pfk=bf(pfk), cq=jnp.asarray(cq), ck=jnp.asarray(ck),
                pqkt=bf(np.concatenate([pq.T, pk.T], axis=0)), eq=bf(eq), ek=bf(ek))


def attn_prep(proj3, bf_rows, layer, cst, lay, hosted=None):
    Bl, S, _ = proj3.shape
    ts = ATTN_BLOCK
    w = HEADS * HEAD_PAD

    def body(q_ref, k_ref, f_ref, bf_ref, pq_ref, pk_ref, pfq_ref, pfk_ref, cq_ref, ck_ref,
             qa_ref, ka_ref, carry_ref):
        @pl.when(pl.program_id(1) == 0)
        def _():
            carry_ref[...] = jnp.zeros_like(carry_ref)

        z = f_ref[...].astype(F32) + bf_ref[...]
        logf = jnp.minimum(z, 0.0) - jnp.log(1.0 + jnp.exp(-jnp.abs(z)))
        r = lax.broadcasted_iota(jnp.int32, (ts, ts), 0)
        c = lax.broadcasted_iota(jnp.int32, (ts, ts), 1)
        tri = jnp.where(r >= c, 1.0, 0.0).astype(BF16)
        fcum = carry_ref[...]
        for part in _split3(logf):
            fcum = fcum + _dot(tri, part)
        carry_ref[...] = fcum[ts - 1:ts, :]
        qa = _dot(q_ref[...], pq_ref[...]) + cq_ref[...]
        ka = _dot(k_ref[...], pk_ref[...]) + ck_ref[...]
        for i, part in enumerate(_split3(fcum)):
            qa = qa + _dot(part, pfq_ref[i])
            ka = ka + _dot(part, pfk_ref[i])
        qa_ref[...] = qa.astype(BF16)
        ka_ref[...] = ka.astype(BF16)

    cfull = lambda shape: pl.BlockSpec(shape, lambda b, s: (0,) * len(shape))
    return _pcall(
        body, hosted, name="attn_prep", grid=(Bl, S // ts),
        in_specs=[pl.BlockSpec((None, ts, BRANCH_W), lambda b, s: (b, s, lay["q"] // BRANCH_W)),
                  pl.BlockSpec((None, ts, BRANCH_W), lambda b, s: (b, s, lay["k"] // BRANCH_W)),
                  pl.BlockSpec((None, ts, LANES), lambda b, s: (b, s, lay["f"] // LANES)),
                  pl.BlockSpec((None, 1, LANES), lambda b, s: (layer, 0, 0)),
                  cfull((BRANCH_W, w)), cfull((BRANCH_W, w)),
                  cfull((3, LANES, w)), cfull((3, LANES, w)), cfull((1, w)), cfull((1, w))],
        out_specs=[pl.BlockSpec((None, ts, w), lambda b, s: (b, s, 0)),
                   pl.BlockSpec((None, ts, w), lambda b, s: (b, s, 0))],
        out_shape=[SDS((Bl, S, w), BF16), SDS((Bl, S, w), BF16)],
        scratch_shapes=[pltpu.VMEM((1, LANES), F32)],
        semantics=("arbitrary", "arbitrary"),
    )(proj3, proj3, proj3, bf_rows, cst["pq"], cst["pk"], cst["pfq"], cst["pfk"], cst["cq"], cst["ck"])


def attn_fwd(qa, ka, proj3, lay, hosted=None):
    Bl, S, _ = qa.shape
    tq = ATTN_BLOCK
    nq = S // tq
    pairs = HEADS // 2
    pw = 2 * HEAD_PAD
    vw = 2 * HEAD_DIM

    def body(qa_ref, ka_ref, v_ref, o_ref, lse_ref):
        row = lax.broadcasted_iota(jnp.int32, (tq, tq), 0)
        col = lax.broadcasted_iota(jnp.int32, (tq, tq), 1)
        causal = row <= col
        for i in range(nq):
            nk = (i + 1) * tq
            rows = slice(i * tq, nk)
            o_t = []
            for h in range(2):
                hs = slice(h * HEAD_PAD, (h + 1) * HEAD_PAD)
                st = _dot_nt(ka_ref[0:nk, hs], qa_ref[rows, hs])
                diag = jnp.where(causal, st[nk - tq:], NEG_INF)
                m = jnp.max(diag, axis=0, keepdims=True)
                if i:
                    m = jnp.maximum(m, jnp.max(st[:nk - tq], axis=0, keepdims=True))
                p_diag = jnp.exp(diag - m)
                l = jnp.sum(p_diag, axis=0, keepdims=True)
                if i:
                    p_top = jnp.exp(st[:nk - tq] - m)
                    l = l + jnp.sum(p_top, axis=0, keepdims=True)
                    p = jnp.concatenate([p_top.astype(BF16), p_diag.astype(BF16)], axis=0)
                else:
                    p = p_diag.astype(BF16)
                acc = _dot_tn(v_ref[0:nk, :], p)
                o_t.append(acc[h * HEAD_DIM:(h + 1) * HEAD_DIM, :] / l)
                lse_ref[h:h + 1, rows] = m + jnp.log(l)
            o_ref[rows, :] = jnp.concatenate(o_t, axis=0).T.astype(BF16)

    return _pcall(
        body, hosted, name="attn_fwd", grid=(Bl, pairs),
        in_specs=[pl.BlockSpec((None, S, pw), lambda b, p: (b, 0, p)),
                  pl.BlockSpec((None, S, pw), lambda b, p: (b, 0, p)),
                  pl.BlockSpec((None, S, vw), lambda b, p: (b, 0, lay["v"] // vw + p))],
        out_specs=[pl.BlockSpec((None, S, vw), lambda b, p: (b, 0, p)),
                   pl.BlockSpec((None, None, 2, S), lambda b, p: (b, p, 0, 0))],
        out_shape=[SDS((Bl, S, BRANCH_W), BF16), SDS((Bl, pairs, 2, S), F32)],
        semantics=("arbitrary", "arbitrary"),
    )(qa, ka, proj3)


def attn_bwd(qa, ka, proj3, dao, ao, lse, dproj3, lay):
    Bl, S, _ = qa.shape
    tk = ATTN_BLOCK
    nq = S // tk
    pairs = HEADS // 2
    pw = 2 * HEAD_PAD
    vw = 2 * HEAD_DIM

    def body(qa_ref, ka_ref, v_ref, do_ref, o_ref, lse_ref, _, dqa_ref, dka_ref, dv_ref):
        row = lax.broadcasted_iota(jnp.int32, (tk, tk), 0)
        col = lax.broadcasted_iota(jnp.int32, (tk, tk), 1)
        causal = row <= col
        lane8 = lax.broadcasted_iota(jnp.int32, (8, vw), 1)
        lane_s = lax.broadcasted_iota(jnp.int32, (S, vw), 1)
        lane_k = lax.broadcasted_iota(jnp.int32, (tk, vw), 1)
        doo = do_ref[...].astype(F32) * o_ref[...].astype(F32)
        hi = doo.astype(BF16)
        lo = (doo - hi.astype(F32)).astype(BF16)
        delta, v_head = [], []
        for h in range(2):
            sel = jnp.where((lane8 >= h * HEAD_DIM) & (lane8 < (h + 1) * HEAD_DIM), 1.0, 0.0).astype(BF16)
            delta.append((_dot_nt(sel, hi) + _dot_nt(sel, lo))[0:1, :])
            in_head = (lane_s >= h * HEAD_DIM) & (lane_s < (h + 1) * HEAD_DIM)
            v_head.append(jnp.where(in_head, v_ref[...], jnp.zeros_like(v_ref[...])))
        dqa_ref[...] = jnp.zeros_like(dqa_ref)
        for j in range(nq):
            q0 = j * tk
            krows = slice(q0, q0 + tk)
            do = do_ref[q0:, :]
            dvs = []
            for h in range(2):
                hs = slice(h * HEAD_PAD, (h + 1) * HEAD_PAD)
                k = ka_ref[krows, hs]
                q = qa_ref[q0:, hs]
                st = _dot_nt(k, q)
                p = jnp.exp(st - lse_ref[h:h + 1, q0:])
                p_diag = jnp.where(causal, p[:, :tk], 0.0)
                p = jnp.concatenate([p_diag, p[:, tk:]], axis=1) if j < nq - 1 else p_diag
                dvs.append(_dot(p.astype(BF16), do))
                dpt = _dot_nt(v_head[h][krows, :], do)
                ds = (p * (dpt - delta[h][:, q0:])).astype(BF16)
                dka_ref[krows, hs] = _dot(ds, q)
                dqa_ref[q0:, hs] += _dot_tn(ds, k)
            dv_ref[krows, :] = jnp.where(lane_k < HEAD_DIM, dvs[0], dvs[1]).astype(BF16)

    seq = lambda w, c0=0: pl.BlockSpec((None, S, w), lambda b, p: (b, 0, c0 + p))
    return pl.pallas_call(
        body, name="attn_bwd", grid=(Bl, pairs),
        in_specs=[seq(pw), seq(pw), seq(vw, lay["v"] // vw), seq(vw), seq(vw),
                  pl.BlockSpec((None, None, 2, S), lambda b, p: (b, p, 0, 0)), _ANY],
        out_specs=[seq(pw), seq(pw), seq(vw, lay["v"] // vw)],
        out_shape=[SDS((Bl, S, HEADS * HEAD_PAD), F32), SDS((Bl, S, HEADS * HEAD_PAD), F32),
                   SDS(dproj3.shape, BF16)],
        input_output_aliases={6: 2},
        compiler_params=_params("arbitrary", "arbitrary"),
    )(qa, ka, proj3, dao, ao, lse, dproj3)


def attn_post(dqa, dka, proj3, bf_rows, layer, dproj3, cst, lay):
    Bl, S, w = dqa.shape
    ts = ATTN_BLOCK
    ns = S // ts
    qkf = 2 * BRANCH_W + F_PAD

    def body(dqa_ref, dka_ref, f_ref, bf_ref, pqkt_ref, eq_ref, ek_ref, _, dqkf_ref, dbf_ref, carry_ref):
        b, s = pl.program_id(0), pl.program_id(1)

        @pl.when(s == 0)
        def _():
            carry_ref[...] = jnp.zeros_like(carry_ref)

        dqa_v, dka_v = dqa_ref[...], dka_ref[...]
        qh = dqa_v.astype(BF16)
        kh = dka_v.astype(BF16)
        dqkf_ref[:, :BRANCH_W] = _dot(qh, pqkt_ref[:w, :]).astype(BF16)
        dqkf_ref[:, BRANCH_W:2 * BRANCH_W] = _dot(kh, pqkt_ref[w:, :]).astype(BF16)
        ql = (dqa_v - qh.astype(F32)).astype(BF16)
        kl = (dka_v - kh.astype(F32)).astype(BF16)
        d_f = (_dot(qh, eq_ref[...]) + _dot(ql, eq_ref[...])) + (_dot(kh, ek_ref[...]) + _dot(kl, ek_ref[...]))
        r = lax.broadcasted_iota(jnp.int32, (ts, ts), 0)
        c = lax.broadcasted_iota(jnp.int32, (ts, ts), 1)
        triu = jnp.where(c >= r, 1.0, 0.0).astype(BF16)
        rev = carry_ref[...]
        for part in _split3(d_f):
            rev = rev + _dot(triu, part)
        carry_ref[...] = rev[0:1, :]
        z = f_ref[...].astype(F32) + bf_ref[...]
        lane = lax.broadcasted_iota(jnp.int32, (ts, LANES), 1)
        dfl = jnp.where(lane < HEADS, rev / (1.0 + jnp.exp(z)), 0.0)
        dqkf_ref[:, 2 * BRANCH_W:] = jnp.concatenate(
            [dfl.astype(BF16), jnp.zeros((ts, F_PAD - LANES), BF16)], axis=1)
        part = jnp.sum(dfl, axis=0, keepdims=True)

        @pl.when((b == 0) & (s == 0))
        def _():
            dbf_ref[...] = part

        @pl.when((b > 0) | (s > 0))
        def _():
            dbf_ref[...] += part

    assert lay["q"] % qkf == 0
    cfull = lambda shape: pl.BlockSpec(shape, lambda b, s: (0,) * len(shape))
    rev_blk = lambda wd, c0=0: pl.BlockSpec((None, ts, wd), lambda b, s: (b, ns - 1 - s, c0))
    return pl.pallas_call(
        body, name="attn_post", grid=(Bl, ns),
        in_specs=[rev_blk(w), rev_blk(w), rev_blk(LANES, lay["f"] // LANES),
                  pl.BlockSpec((None, 1, LANES), lambda b, s: (layer, 0, 0)),
                  cfull((2 * w, BRANCH_W)), cfull((w, LANES)), cfull((w, LANES)), _ANY],
        out_specs=[rev_blk(qkf, lay["q"] // qkf), cfull((1, LANES))],
        out_shape=[SDS(dproj3.shape, BF16), SDS((1, LANES), F32)],
        scratch_shapes=[pltpu.VMEM((1, LANES), F32)],
        input_output_aliases={7: 0},
        compiler_params=_params("arbitrary", "arbitrary"),
    )(dqa, dka, proj3, bf_rows, cst["pqkt"], cst["eq"], cst["ek"], dproj3)


def _shift_down(x, k, row):
    return jnp.where(row >= k, pltpu.roll(x, k, axis=0), 0.0)


def _shift_up(x, k, row):
    n = x.shape[0]
    return jnp.where(row < n - k, pltpu.roll(x, n - k, axis=0), 0.0)


def _window_sum(x, g, row, shift):
    s2 = x + shift(x, 1, row)
    s4 = s2 + shift(s2, 2, row)
    s8 = s4 + shift(s4, 4, row)
    s16 = s8 + shift(s8, 8, row)
    return jnp.where(g == 0, s2, jnp.where(g == 1, s4, jnp.where(g == 2, s8, s16)))


def _window_count(g, row):
    wnd = jnp.where(g == 0, 2, jnp.where(g == 1, 4, jnp.where(g == 2, 8, 16)))
    return jnp.minimum(row + 1, wnd).astype(F32)


def _group_columns(ref):
    return [ref[:, n * GROUP_W:(n + 1) * GROUP_W].astype(F32) for n in range(4)]


def poolconv_fwd(proj3, pool_w, pool_scale, conv_w, layer, lay, hosted=None):
    Bl, S, _ = proj3.shape

    def body(x_ref, pw_ref, ps_ref, cw_ref, po_ref, co_ref):
        g = pl.program_id(1)
        row = lax.broadcasted_iota(jnp.int32, (S, GROUP_W), 0)
        u, cv, cb, cc = _group_columns(x_ref)
        d = _window_sum(u, g, row, _shift_down) / _window_count(g, row) - u
        po_ref[...] = (_dot(d.astype(BF16), pw_ref[...]) * ps_ref[...]).astype(BF16)
        z = cc * cv
        y = cw_ref[0:1, :] * _shift_down(z, 2, row) + cw_ref[1:2, :] * _shift_down(z, 1, row) + cw_ref[2:3, :] * z
        co_ref[...] = (cb * y).astype(BF16)

    out = pl.BlockSpec((None, S, GROUP_W), lambda b, g: (b, 0, g))
    return _pcall(
        body, hosted, name="poolconv_fwd", grid=(Bl, N_GROUPS),
        in_specs=[pl.BlockSpec((None, S, BRANCH_W), lambda b, g: (b, 0, lay["pc"] // BRANCH_W + g)),
                  pl.BlockSpec((None, None, GROUP_W, GROUP_W), lambda b, g: (layer, g, 0, 0)),
                  pl.BlockSpec((None, 1, GROUP_W), lambda b, g: (layer, 0, g)),
                  pl.BlockSpec((None, None, 3, GROUP_W), lambda b, g: (g, layer, 0, 0))],
        out_specs=[out, out],
        out_shape=[SDS((Bl, S, BRANCH_W), BF16), SDS((Bl, S, BRANCH_W), BF16)],
        semantics=("arbitrary", "arbitrary"),
    )(proj3, pool_w, pool_scale, conv_w)


def poolconv_bwd(proj3, dpo, dco, pool_w, pool_scale, conv_w, layer, dproj3, dcw_stacked, lay):
    Bl, S, _ = proj3.shape

    def body(x_ref, dpo_ref, dco_ref, pw_ref, ps_ref, cw_ref, *rest):
        dx_ref, dpw_ref, dps_ref, dcw_ref, dcw_acc = rest[-5:]
        g, b = pl.program_id(0), pl.program_id(1)
        row = lax.broadcasted_iota(jnp.int32, (S, GROUP_W), 0)
        cnt = _window_count(g, row)
        u, cv, cb, cc = _group_columns(x_ref)
        d = (_window_sum(u, g, row, _shift_down) / cnt - u).astype(BF16)
        pw = pw_ref[...]
        ypre = _dot(d, pw)
        dpo_v = dpo_ref[...].astype(F32)
        dps = jnp.sum(dpo_v * ypre, axis=0, keepdims=True)
        dyp = (dpo_v * ps_ref[...]).astype(BF16)
        dpw = _dot_tn(d, dyp)
        dd = _dot_nt(dyp, pw)
        dx_ref[:, 0:GROUP_W] = (_window_sum(dd / cnt, g, row, _shift_up) - dd).astype(BF16)

        z = cc * cv
        z1, z2 = _shift_down(z, 1, row), _shift_down(z, 2, row)
        w0, w1, w2 = cw_ref[0:1, :], cw_ref[1:2, :], cw_ref[2:3, :]
        y = w0 * z2 + w1 * z1 + w2 * z
        dco_v = dco_ref[...].astype(F32)
        dy = dco_v * cb
        dz = w0 * _shift_up(dy, 2, row) + w1 * _shift_up(dy, 1, row) + w2 * dy
        dx_ref[:, GROUP_W:2 * GROUP_W] = (dz * cc).astype(BF16)
        dx_ref[:, 2 * GROUP_W:3 * GROUP_W] = (dco_v * y).astype(BF16)
        dx_ref[:, 3 * GROUP_W:] = (dz * cv).astype(BF16)
        dcw = jnp.concatenate([jnp.sum(dy * z2, axis=0, keepdims=True),
                               jnp.sum(dy * z1, axis=0, keepdims=True),
                               jnp.sum(dy * z, axis=0, keepdims=True)], axis=0)

        @pl.when(b == 0)
        def _():
            dpw_ref[...] = dpw
            dps_ref[...] = dps
            dcw_acc[...] = dcw

        @pl.when(b > 0)
        def _():
            dpw_ref[...] += dpw
            dps_ref[...] += dps
            dcw_acc[...] += dcw

        @pl.when(b == Bl - 1)
        def _():
            dcw_ref[...] = dcw_acc[...].astype(BF16)

    blk = pl.BlockSpec((None, S, GROUP_W), lambda g, b: (b, 0, g))
    pc = pl.BlockSpec((None, S, BRANCH_W), lambda g, b: (b, 0, lay["pc"] // BRANCH_W + g))
    ins = [proj3, dpo, dco, pool_w, pool_scale, conv_w, dproj3]
    in_specs = [pc, blk, blk,
                pl.BlockSpec((None, None, GROUP_W, GROUP_W), lambda g, b: (layer, g, 0, 0)),
                pl.BlockSpec((None, 1, GROUP_W), lambda g, b: (layer, 0, g)),
                pl.BlockSpec((None, None, 3, GROUP_W), lambda g, b: (g, layer, 0, 0)), _ANY]
    aliases = {6: 0}
    if dcw_stacked is not None:
        ins.append(dcw_stacked)
        in_specs.append(_ANY)
        aliases[7] = 3
    return pl.pallas_call(
        body, name="poolconv_bwd", grid=(N_GROUPS, Bl),
        in_specs=in_specs,
        out_specs=[pc, pl.BlockSpec((None, GROUP_W, GROUP_W), lambda g, b: (g, 0, 0)),
                   pl.BlockSpec((1, GROUP_W), lambda g, b: (0, g)),
                   pl.BlockSpec((None, None, 3, GROUP_W), lambda g, b: (layer, g, 0, 0))],
        out_shape=[SDS(dproj3.shape, BF16), SDS((N_GROUPS, GROUP_W, GROUP_W), F32), SDS((1, BRANCH_W), F32),
                   SDS((N_LAYERS, N_CHIPS, 3, GROUP_W), BF16)],
        scratch_shapes=[pltpu.VMEM((3, GROUP_W), F32)],
        input_output_aliases=aliases,
        compiler_params=_params("arbitrary", "arbitrary"),
    )(*ins)


def _tile_2d(rows, cols, n_arrays):
    budget = VMEM_LIMIT // 2
    lanes = -(-cols // LANES) * LANES
    if rows % 8 == 0:
        for t in (2048, 1024, 512, 256, 128, 64, 32, 16, 8):
            if rows % t == 0 and 2 * n_arrays * t * lanes * 4 <= budget:
                return t, cols
    for t in (1024, 512, 256, 128):
        if cols % t == 0 and 2 * n_arrays * (rows + 8) * t * 4 <= budget:
            return rows, t
    return rows, cols


def add_pair(kept, core, received, name):
    _, n, R, C = kept.shape
    tr, tc = _tile_2d(R, C, 3)

    def body(core_ref, a_ref, b_ref, o_ref):
        o_ref[...] = (a_ref[...].astype(F32) + b_ref[...].astype(F32)).astype(BF16)

    blk = pl.BlockSpec((None, tr, tc), lambda d, i, j, core_ref: (d, i, j))
    grid_spec = pltpu.PrefetchScalarGridSpec(
        num_scalar_prefetch=1, grid=(n, R // tr, C // tc),
        in_specs=[pl.BlockSpec((None, None, tr, tc), lambda d, i, j, core_ref: (core_ref[0], d, i, j)), blk],
        out_specs=blk)
    return pl.pallas_call(body, name=name, grid_spec=grid_spec, out_shape=SDS((n, R, C), BF16),
                          compiler_params=_params("arbitrary", "arbitrary", "arbitrary"))(core, kept, received)


def add_chips(parts, core, name):
    _, R, C = parts.shape
    tr, tc = _tile_2d(R, C, 4)

    def body(core_ref, p_ref, o_ref):
        acc = p_ref[0].astype(F32)
        for j in range(1, N_CHIPS):
            acc = acc + p_ref[j].astype(F32)
        o_ref[...] = acc

    grid_spec = pltpu.PrefetchScalarGridSpec(
        num_scalar_prefetch=1, grid=(R // tr, C // tc),
        in_specs=[pl.BlockSpec((N_CHIPS, tr, tc), lambda i, j, core_ref: (0, i, j))],
        out_specs=pl.BlockSpec((None, tr, tc), lambda i, j, core_ref: (core_ref[0], i, j)))
    return pl.pallas_call(body, name=name, grid_spec=grid_spec, out_shape=SDS((2, R, C), F32),
                          compiler_params=_params("arbitrary", "arbitrary"))(core, parts)


def adamw(w, g, m, v, name):
    if w.ndim == 2:
        R, C = w.shape
        tr, _ = _tile_2d(R, C, 7)
        grid, blk = (R // tr,), pl.BlockSpec((tr, C), lambda i: (i, 0))
    else:
        N, r, C = w.shape
        tn = max(t for t in range(1, N + 1) if N % t == 0 and t * r * C * 4 <= 512 * 1024)
        grid, blk = (N // tn,), pl.BlockSpec((tn, r, C), lambda i: (i, 0, 0))

    def body(w_ref, g_ref, m_ref, v_ref, d_ref, nm_ref, nv_ref):
        gv = g_ref[...]
        m_new = ADAM_B1 * m_ref[...] + (1.0 - ADAM_B1) * gv
        v_new = ADAM_B2 * v_ref[...] + (1.0 - ADAM_B2) * (gv * gv)
        m_hat = m_new / (1.0 - ADAM_B1 ** ADAM_STEP)
        v_hat = v_new / (1.0 - ADAM_B2 ** ADAM_STEP)
        d_ref[...] = -ADAM_LR * (m_hat / (jnp.sqrt(v_hat) + ADAM_EPS) + ADAM_WD * w_ref[...])
        nm_ref[...] = m_new
        nv_ref[...] = v_new

    out = SDS(w.shape, F32)
    return pl.pallas_call(body, name=name, grid=grid, in_specs=[blk] * 4, out_specs=[blk] * 3,
                          out_shape=[out, out, out], compiler_params=_params("arbitrary"))(w, g, m, v)


_COMM = pltpu.CompilerParams(has_side_effects=True)


def gather_buffers(shards):
    me_chip = 2 * lax.axis_index("x") + lax.axis_index("y")
    pool = {}
    for name, sh in shards.items():
        L, r, c = sh.shape
        if name in ROW_SHARDED:
            pool[name] = lax.dynamic_update_slice(lax.empty((L, N_CHIPS, r, c), sh.dtype), sh[:, None],
                                                  (0, me_chip, 0, 0))
        else:
            pool[name] = lax.dynamic_update_slice(lax.empty((N_CHIPS, L, r, c), sh.dtype), sh[None],
                                                  (me_chip, 0, 0, 0))
    return pool


def gather_now(pool, units):
    ici = Hosted(pool, [("ici", name, layer) for name, layer in units])
    fwd = Hosted(pool, [("fwd", name, layer) for name, layer in units])
    names, n = ici.names, len(ici.names)

    def body(*refs):
        bufs = dict(zip(names, refs[n:2 * n]))
        send_sems, recv_sems, fwd_send, fwd_recv = refs[2 * n:]
        for hosted, s_sems, r_sems in ((ici, send_sems, recv_sems), (fwd, fwd_send, fwd_recv)):
            plan = _hosted_plan(hosted, bufs, s_sems, r_sems)
            _hosted_start(plan, True)
            _hosted_finish(plan, True)

    sem = pltpu.SemaphoreType.DMA
    n_job = len(units)
    res = pl.pallas_call(
        body, name="gather_now", in_specs=[_ANY] * n, out_specs=[_ANY] * n,
        out_shape=[SDS(pool[m].shape, pool[m].dtype) for m in names],
        scratch_shapes=[sem((n_job, 3)), sem((n_job, 3)), sem((n_job, 3)), sem((n_job, 3))],
        input_output_aliases={t: t for t in range(n)},
        compiler_params=_COMM,
    )(*[pool[m] for m in names])
    pool.update(zip(names, res))


def allgather_chips(buf, name):
    def body(src_ref, out_ref, send_sems, recv_sems, local_sem):
        x, y, c = _position()
        me = 2 * x + y
        mine = pltpu.make_async_copy(src_ref, out_ref.at[me], local_sem)
        mine.start()
        sends = []
        for k, (px, py) in enumerate(_other_chips(x, y)):
            cp = _remote(src_ref, out_ref.at[me], send_sems.at[k], recv_sems.at[k], (px, py, c))
            cp.start()
            sends.append(cp)
        for k, (px, py) in enumerate(_other_chips(x, y)):
            _remote(src_ref, out_ref.at[2 * px + py], send_sems.at[k], recv_sems.at[k], (px, py, c)).wait_recv()
        for cp in sends:
            cp.wait_send()
        mine.wait()

    sem = pltpu.SemaphoreType.DMA
    return pl.pallas_call(
        body, name=name, in_specs=[_ANY], out_specs=_ANY, out_shape=SDS((N_CHIPS,) + buf.shape, buf.dtype),
        scratch_shapes=[sem((3,)), sem((3,)), sem], compiler_params=_COMM,
    )(buf)


def swap_sibling(tensors, name):
    n = len(tensors)

    def body(*refs):
        srcs, outs, send_sems, recv_sems = refs[:n], refs[n:2 * n], refs[2 * n], refs[2 * n + 1]
        x, y, c = _position()
        cps = [_remote(srcs[t].at[1 - c], outs[t], send_sems.at[t], recv_sems.at[t], (x, y, 1 - c))
               for t in range(n)]
        for cp in cps:
            cp.start()
        for cp in cps:
            cp.wait()

    sem = pltpu.SemaphoreType.DMA
    return pl.pallas_call(
        body, name=name, in_specs=[_ANY] * n, out_specs=[_ANY] * n,
        out_shape=[SDS(t.shape[1:], t.dtype) for t in tensors],
        scratch_shapes=[sem((n,)), sem((n,))], compiler_params=_COMM,
    )(*tensors)


def exchange_chips(tensors, name):
    n = len(tensors)

    def body(*refs):
        srcs, outs = refs[:n], refs[n:2 * n]
        send_sems, recv_sems, local_sems = refs[2 * n:]
        x, y, c = _position()
        me = 2 * x + y
        others = _other_chips(x, y)
        cps = []
        for t in range(n):
            cp = pltpu.make_async_copy(srcs[t].at[me], outs[t].at[me], local_sems.at[t])
            cp.start()
            cps.append(cp)
        sends = []
        for t in range(n):
            for k, (px, py) in enumerate(others):
                cp = _remote(srcs[t].at[2 * px + py], outs[t].at[me], send_sems.at[t, k], recv_sems.at[t, k],
                             (px, py, c))
                cp.start()
                sends.append(cp)
        for t in range(n):
            for k, (px, py) in enumerate(others):
                _remote(srcs[t].at[me], outs[t].at[2 * px + py], send_sems.at[t, k], recv_sems.at[t, k],
                        (px, py, c)).wait_recv()
        for cp in sends:
            cp.wait_send()
        for cp in cps:
            cp.wait()

    sem = pltpu.SemaphoreType.DMA
    return pl.pallas_call(
        body, name=name, in_specs=[_ANY] * n, out_specs=[_ANY] * n,
        out_shape=[SDS(t.shape, t.dtype) for t in tensors],
        scratch_shapes=[sem((n, 3)), sem((n, 3)), sem((n,))], compiler_params=_COMM,
    )(*tensors)


def join_halves(tensors, name):
    n = len(tensors)

    def body(*refs):
        outs, send_sems, recv_sems = refs[n:2 * n], refs[2 * n], refs[2 * n + 1]
        x, y, c = _position()
        sib = (x, y, 1 - c)
        sends = []
        for t in range(n):
            cp = _remote(outs[t].at[c], outs[t].at[c], send_sems.at[t], recv_sems.at[t], sib)
            cp.start()
            sends.append(cp)
        for t in range(n):
            _remote(outs[t].at[c], outs[t].at[1 - c], send_sems.at[t], recv_sems.at[t], sib).wait_recv()
        for cp in sends:
            cp.wait_send()

    sem = pltpu.SemaphoreType.DMA
    return pl.pallas_call(
        body, name=name, in_specs=[_ANY] * n, out_specs=[_ANY] * n,
        out_shape=[SDS(t.shape, t.dtype) for t in tensors],
        scratch_shapes=[sem((n,)), sem((n,))], input_output_aliases={t: t for t in range(n)},
        compiler_params=_COMM,
    )(*tensors)


BIG = ("w_in", "w_proj_attn", "w_proj_pool", "w_proj_conv", "conv_w", "w_out", "w_gate_up", "w_down")
REPLICATED = ("attn_norm", "b_forget", "b_gate", "pool_w", "pool_scale", "ffn_norm", "final_norm")
ORDER = ("attn_norm", "w_in", "b_forget", "b_gate", "w_proj_attn", "pool_w", "pool_scale", "w_proj_pool",
         "conv_w", "w_proj_conv", "w_out", "ffn_norm", "w_gate_up", "w_down", "final_norm")


def _proj_layout(D):
    lay = {"g": 0, "q": 3 * D}
    lay["k"] = lay["q"] + BRANCH_W
    lay["f"] = lay["k"] + BRANCH_W
    lay["v"] = lay["f"] + F_PAD
    lay["pc"] = lay["v"] + BRANCH_W
    lay["width"] = lay["pc"] + 4 * BRANCH_W
    return lay


_REF = dict(q=0, k=512, v=1024, f=1536, u=1544, cv=2056, cb=2568, cc=3080, g=3592)


def _packed_pieces(D):
    pieces = [(_REF["g"], 3 * D), (_REF["q"], BRANCH_W), (_REF["k"], BRANCH_W), (_REF["f"], HEADS),
              (None, F_PAD - HEADS), (_REF["v"], BRANCH_W)]
    for gi in range(N_GROUPS):
        pieces += [(_REF[name] + gi * GROUP_W, GROUP_W) for name in ("u", "cv", "cb", "cc")]
    return pieces


def _pack_w_in_rows(shards, layer):
    _, _, cs, D = shards.shape
    parts = []
    for start, n in _packed_pieces(D):
        if start is None:
            parts.append(jnp.zeros((n, D), shards.dtype))
        while start is not None and n:
            chip, off = divmod(start, cs)
            take = min(n, cs - off)
            parts.append(shards[chip, layer, off:off + take, :])
            start, n = start + take, n - take
    return jnp.concatenate(parts, axis=0)


def _unpack_w_in_rows(p, D):
    lay = _proj_layout(D)
    rows = lambda a, n: p[:, a:a + n, :]
    kinds = []
    for kind in range(4):
        kinds += [rows(lay["pc"] + gi * BRANCH_W + kind * GROUP_W, GROUP_W) for gi in range(N_GROUPS)]
    return jnp.concatenate([rows(lay["q"], BRANCH_W), rows(lay["k"], BRANCH_W), rows(lay["v"], BRANCH_W),
                            rows(lay["f"], HEADS)] + kinds + [rows(0, 3 * D)], axis=1)


def _split_flat(vec, shapes):
    out, at = [], 0
    for shp in shapes:
        n = int(np.prod(shp))
        out.append(vec[at:at + n].reshape(shp))
        at += n
    return out


def kernel(x, attn_norm, w_in, b_forget, b_gate, w_proj_attn, pool_w, pool_scale, w_proj_pool, conv_w, w_proj_conv, w_out, ffn_norm, w_gate_up, w_down, final_norm, loss_target, m_attn_norm, m_w_in, m_b_forget, m_b_gate, m_w_proj_attn, m_pool_w, m_pool_scale, m_w_proj_pool, m_conv_w, m_w_proj_conv, m_w_out, m_ffn_norm, m_w_gate_up, m_w_down, m_final_norm, v_attn_norm, v_w_in, v_b_forget, v_b_gate, v_w_proj_attn, v_pool_w, v_pool_scale, v_w_proj_pool, v_conv_w, v_w_proj_conv, v_w_out, v_ffn_norm, v_w_gate_up, v_w_down, v_final_norm):
    weights = dict(attn_norm=attn_norm, w_in=w_in, b_forget=b_forget, b_gate=b_gate, w_proj_attn=w_proj_attn,
                   pool_w=pool_w, pool_scale=pool_scale, w_proj_pool=w_proj_pool, conv_w=conv_w,
                   w_proj_conv=w_proj_conv, w_out=w_out, ffn_norm=ffn_norm, w_gate_up=w_gate_up, w_down=w_down,
                   final_norm=final_norm)
    mom_m = dict(attn_norm=m_attn_norm, w_in=m_w_in, b_forget=m_b_forget, b_gate=m_b_gate, w_proj_attn=m_w_proj_attn,
                 pool_w=m_pool_w, pool_scale=m_pool_scale, w_proj_pool=m_w_proj_pool, conv_w=m_conv_w,
                 w_proj_conv=m_w_proj_conv, w_out=m_w_out, ffn_norm=m_ffn_norm, w_gate_up=m_w_gate_up,
                 w_down=m_w_down, final_norm=m_final_norm)
    mom_v = dict(attn_norm=v_attn_norm, w_in=v_w_in, b_forget=v_b_forget, b_gate=v_b_gate, w_proj_attn=v_w_proj_attn,
                 pool_w=v_pool_w, pool_scale=v_pool_scale, w_proj_pool=v_w_proj_pool, conv_w=v_conv_w,
                 w_proj_conv=v_w_proj_conv, w_out=v_w_out, ffn_norm=v_ffn_norm, w_gate_up=v_w_gate_up,
                 w_down=v_w_down, final_norm=v_final_norm)

    Bl, S, D = x.shape
    T = Bl * S
    L = w_in.shape[0]
    F = w_down.shape[1] * N_CHIPS
    lay = _proj_layout(D)
    cst = _placement_constants()
    assert L == N_LAYERS and S % ATTN_BLOCK == 0 and F % (2 * LANES) == 0 and D % BRANCH_W == 0
    assert w_in.shape[2] * N_CHIPS == _REF["g"] + 3 * D and conv_w.shape[2] == GROUP_W

    send = {n: weights[n].astype(BF16) for n in BIG}
    send["conv_w"] = conv_w
    send["w_in"] = w_in.transpose(0, 2, 1).astype(BF16)
    pool = gather_buffers(send)
    gather_now(pool, [("w_in", 0)])
    rest = ("w_out", "w_proj_attn", "w_proj_pool", "w_gate_up", "w_proj_conv", "conv_w")
    late = ("w_out", "w_proj_attn", "w_proj_pool", "w_proj_conv", "conv_w")
    jobs = lambda kind, names, layer: [(kind, n, layer) for n in names]
    carried = {
        "in_proj": jobs("ici", rest, 0),
        "attn_prep": jobs("fwd", rest, 0) + jobs("ici", ("w_down",), 0),
        "attn_fwd": jobs("fwd", ("w_down",), 0) + jobs("ici", ("w_in", "w_gate_up"), 1),
        "poolconv_fwd": jobs("fwd", ("w_in", "w_gate_up"), 1),
        "mix_fwd": jobs("ici", ("w_down",), 1),
        "gate_up_proj": jobs("fwd", ("w_down",), 1) + jobs("ici", late, 1),
        "ffn_down_fwd": jobs("fwd", late, 1),
    }
    carry = lambda call, layer: Hosted(pool, carried[call]) if layer == 0 else None
    w_down_f = lambda: pool["w_down"].reshape(L, F, D)
    pool_w_b = pool_w.astype(BF16)
    an3, fn3 = attn_norm.reshape(L, 1, D), ffn_norm.reshape(L, 1, D)
    bg3, ps3 = b_gate.reshape(L, 1, 3 * D), pool_scale.reshape(L, 1, BRANCH_W)
    bf3 = jnp.pad(b_forget, ((0, 0), (0, LANES - HEADS))).reshape(L, 1, LANES)

    xs = x.reshape(T, D)
    saved = []
    w_in_p = []
    for l in range(L):
        w_in_p.append(_pack_w_in_rows(pool["w_in"], l))
        proj, h = norm_matmul(xs, an3, w_in_p[l], l, "rows", "in_proj", carry("in_proj", l))
        proj3 = proj.reshape(Bl, S, lay["width"])
        qa, ka = attn_prep(proj3, bf3, l, cst, lay, carry("attn_prep", l))
        ao, lse = attn_fwd(qa, ka, proj3, lay, carry("attn_fwd", l))
        po, co = poolconv_fwd(proj3, pool_w_b, ps3, pool["conv_w"], l, lay, carry("poolconv_fwd", l))
        ao2, po2, co2 = (a.reshape(T, BRANCH_W) for a in (ao, po, co))
        x1, ys, mixed = mix_fwd(ao2, po2, co2, proj, bg3, pool["w_proj_attn"], pool["w_proj_pool"],
                                pool["w_proj_conv"], pool["w_out"], l, xs, carry("mix_fwd", l))
        ab, h2 = norm_matmul(x1, fn3, pool["w_gate_up"], l, "by_shard", "gate_up_proj", carry("gate_up_proj", l))
        x2, s_act = ffn_down_fwd(ab, w_down_f(), l, x1, carry("ffn_down_fwd", l))
        saved.append(dict(x=xs, proj=proj, proj3=proj3, h=h, qa=qa, ka=ka, ao=ao, lse=lse, ao2=ao2, po2=po2,
                          co2=co2, ys=ys, mixed=mixed, x1=x1, ab=ab, h2=h2, s=s_act))
        xs = x2
    w_gu, w_o, conv_w_g = pool["w_gate_up"], pool["w_out"], pool["conv_w"]
    wpa, wpp, wpc = pool["w_proj_attn"], pool["w_proj_pool"], pool["w_proj_conv"]
    w_down_f = w_down_f()

    loss_row, dx, dxb, g_final = loss_head(xs, final_norm.reshape(1, D), loss_target.reshape(T, D))
    loss = lax.psum(loss_row[0, 0], AXES)

    stacked = {n: None for n in BIG}
    small = {n: [None] * L for n in REPLICATED if n != "final_norm"}
    to3 = lambda a: a.reshape(Bl, S, -1)
    for l in reversed(range(L)):
        sv = saved[l]
        da, db = ffn_down_bwd(dxb, w_down_f, l, sv["ab"])
        stacked["w_down"] = matmul_tn(sv["s"], [dxb], "grad_w_down", l, stacked["w_down"])
        stacked["w_gate_up"] = matmul_tn(sv["h2"], [da, db], "grad_w_gate_up", l, stacked["w_gate_up"],
                                         by_dest=True, tn=2 * F // N_CHIPS, tk=_tile(T, (1024, 512, 256)))
        dx1, dx1b, g_fn = matmul_nt_normbwd([da, db], w_gu, l, "by_shard", sv["x1"], fn3, dx, "gate_up_bwd")
        small["ffn_norm"][l] = g_fn[0]
        dys, dproj, dao, dpo, dco, g_bg = mix_bwd(dx1b, w_o, sv["proj"], bg3, sv["ys"], wpa, wpp, wpc, l,
                                                  lay["width"])
        small["b_gate"][l] = g_bg[0]
        stacked["w_out"] = matmul_tn(sv["mixed"], [dx1b], "grad_w_out", l, stacked["w_out"])
        for n, (name, br) in enumerate((("w_proj_attn", sv["ao2"]), ("w_proj_pool", sv["po2"]),
                                        ("w_proj_conv", sv["co2"]))):
            stacked[name] = matmul_tn(br, [dys], "grad_" + name, l, stacked[name], b_col0=n * D, n_cols=D,
                                      by_dest=True, tn=D // N_CHIPS)
        dqa, dka, dproj3 = attn_bwd(sv["qa"], sv["ka"], sv["proj3"], to3(dao), sv["ao"], sv["lse"], to3(dproj), lay)
        dproj3, g_bf = attn_post(dqa, dka, sv["proj3"], bf3, l, dproj3, cst, lay)
        small["b_forget"][l] = g_bf[0, :HEADS]
        dproj3, g_pw, g_ps, stacked["conv_w"] = poolconv_bwd(sv["proj3"], to3(dpo), to3(dco), pool_w_b, ps3,
                                                             conv_w_g, l, dproj3, stacked["conv_w"], lay)
        small["pool_w"][l], small["pool_scale"][l] = g_pw, g_ps[0]
        dproj = dproj3.reshape(T, lay["width"])
        stacked["w_in"] = matmul_tn(dproj, [sv["h"]], "grad_w_in", l, stacked["w_in"])
        dx, dxb, g_an = matmul_nt_normbwd([dproj], w_in_p[l], l, "rows", sv["x"], an3, dx1, "in_proj_bwd")
        small["attn_norm"][l] = g_an[0]
    grad_x = dx.reshape(Bl, S, D)

    by_dest = dict(stacked)
    by_dest["w_in"] = _unpack_w_in_rows(stacked["w_in"], D).reshape(L, N_CHIPS, -1, D)
    by_dest["w_out"] = stacked["w_out"].reshape(L, N_CHIPS, D // N_CHIPS, D)
    by_dest["w_down"] = stacked["w_down"].reshape(L, N_CHIPS, F // N_CHIPS, D)
    small_shapes = [weights[n].shape for n in REPLICATED]
    small_vec = jnp.concatenate([jnp.stack(small[n]).reshape(-1) for n in REPLICATED[:-1]] + [g_final[0]])
    n_small = small_vec.shape[0]
    small_vec = jnp.pad(small_vec, (0, -n_small % (2 * N_CHIPS * 16 * LANES))).astype(BF16)
    names = BIG + ("small",)
    by_dest["small"] = small_vec.reshape(N_CHIPS, 2, -1, LANES).transpose(1, 0, 2, 3)
    core = lax.axis_index("c").astype(jnp.int32).reshape(1)
    received = swap_sibling([by_dest[n] for n in names], "swap_grad_layers")
    chip_sum = [add_pair(by_dest[n], core, r, "add_pair_" + n) for n, r in zip(names, received)]
    arrived = exchange_chips(chip_sum, "exchange_grad_chips")
    reduced = join_halves([add_chips(a, core, "add_chips_" + n) for n, a in zip(names, arrived)],
                          "join_grad_layers")
    shard_grads = dict(zip(names, reduced))
    small_all = allgather_chips(shard_grads.pop("small"), "allgather_small_grads").reshape(-1)[:n_small]
    rep_grads = dict(zip(REPLICATED, _split_flat(small_all, small_shapes)))

    delta, new_m, new_v = {}, {}, {}
    for n in BIG:
        shp = weights[n].shape
        if n == "w_in":
            view, back = (lambda a: a.transpose(2, 0, 1)), (lambda a: a.transpose(1, 2, 0))
            g = shard_grads[n].transpose(1, 0, 2)
        else:
            view, back = (lambda a: a.reshape(-1, shp[-1])), (lambda a: a.reshape(shp))
            g = view(shard_grads[n])
        d, nm, nv = adamw(view(weights[n]), g, view(mom_m[n]), view(mom_v[n]), "adamw_" + n)
        delta[n], new_m[n], new_v[n], shard_grads[n] = back(d), back(nm), back(nv), back(g)

    def rows(d):
        vec = jnp.concatenate([d[n].reshape(-1) for n in REPLICATED])
        return jnp.pad(vec, (0, -n_small % (8 * LANES))).reshape(-1, LANES)

    outs = adamw(rows(weights), rows(rep_grads), rows(mom_m), rows(mom_v), "adamw_replicated")
    for res, o in zip((delta, new_m, new_v), outs):
        res.update(zip(REPLICATED, _split_flat(o.reshape(-1), small_shapes)))
    all_grads = {**shard_grads, **rep_grads}

    return (loss, grad_x, *[all_grads[n] for n in ORDER], *[delta[n] for n in ORDER],
            *[new_m[n] for n in ORDER], *[new_v[n] for n in ORDER])
```

```python
import numpy as np
import jax
import jax.numpy as jnp
from jax import lax
from jax.experimental import pallas as pl
from jax.experimental.pallas import tpu as pltpu

F32, BF16 = jnp.float32, jnp.bfloat16
SDS = jax.ShapeDtypeStruct
MESH = pl.DeviceIdType.MESH
AXES = ("x", "y", "c")
N_CHIPS = 4
N_LAYERS = 2
LANES = 128
VMEM_LIMIT = 48 * 1024 * 1024

HEADS, HEAD_DIM = 8, 64
HEAD_PAD = 128
BRANCH_W = 512
GROUP_W = 128
N_GROUPS = BRANCH_W // GROUP_W
POOL_WINDOWS = (2, 4, 8, 16)
F_PAD = 512
ATTN_BLOCK = 256
RMS_EPS = 1e-6
NEG_INF = -1e30
ADAM_LR, ADAM_B1, ADAM_B2, ADAM_EPS, ADAM_WD, ADAM_STEP = 0.001, 0.9, 0.999, 1e-08, 0.01, 10

NT = (((1,), (1,)), ((), ()))
TN = (((0,), (0,)), ((), ()))
_ANY = pl.BlockSpec(memory_space=pl.ANY)


def _tile(n, prefs):
    for p in prefs:
        if n % p == 0:
            return p
    raise ValueError(f"no tile of {prefs} divides {n}")


def _params(*sem):
    return pltpu.CompilerParams(dimension_semantics=sem, vmem_limit_bytes=VMEM_LIMIT)


def _sigmoid(z):
    return 0.5 * jnp.tanh(0.5 * z) + 0.5


def _split3(x):
    h1 = x.astype(BF16)
    r1 = x - h1.astype(F32)
    h2 = r1.astype(BF16)
    h3 = (r1 - h2.astype(F32)).astype(BF16)
    return h1, h2, h3


def _position():
    return lax.axis_index("x"), lax.axis_index("y"), lax.axis_index("c")


def _other_chips(x, y):
    return [(1 - x, y), (x, 1 - y), (1 - x, 1 - y)]


def _remote(src, dst, send_sem, recv_sem, device):
    return pltpu.make_async_remote_copy(src_ref=src, dst_ref=dst, send_sem=send_sem, recv_sem=recv_sem,
                                        device_id=device, device_id_type=MESH)


ROW_SHARDED = ("w_out", "w_down")
FETCHER = dict(w_in=0, w_out=0, w_proj_attn=0, w_proj_pool=0, w_gate_up=1, w_down=1, w_proj_conv=1, conv_w=1)


class Hosted:
    def __init__(self, pool, jobs):
        self.pool, self.jobs = pool, list(jobs)
        names = set()
        for job in self.jobs:
            names.update(job[1:3] if job[0] in ("swap", "xchg") else job[1:2])
        self.names = sorted(names)


def _hosted_plan(hosted, refs, send_sems, recv_sems):
    x, y, c = _position()
    me = 2 * x + y
    others = _other_chips(x, y)
    sibling = (x, y, 1 - c)
    plan = []
    for j, job in enumerate(hosted.jobs):
        kind = job[0]
        sems = lambda k, j=j: (send_sems.at[j, k], recv_sems.at[j, k])
        if kind in ("ici", "fwd"):
            _, name, layer = job
            ref = refs[name]
            win = (lambda chip, ref=ref, layer=layer: ref.at[layer, chip]) if name in ROW_SHARDED else (
                lambda chip, ref=ref, layer=layer: ref.at[chip, layer])
            mine = c == FETCHER[name]
            if kind == "ici":
                sends = [_remote(win(me), win(me), *sems(k), (px, py, c)) for k, (px, py) in enumerate(others)]
                arrivals = [_remote(win(2 * px + py), win(2 * px + py), *sems(k), (px, py, c))
                            for k, (px, py) in enumerate(others)]
                plan.append((mine, sends, arrivals, []))
            else:
                sends = [_remote(win(2 * px + py), win(2 * px + py), *sems(k), sibling)
                         for k, (px, py) in enumerate(others)]
                plan.append((mine, sends, [], sends))
        elif kind == "swap":
            _, src, dst, layer = job
            cp = _remote(refs[src].at[layer, :, 1 - c], refs[dst], *sems(0), sibling)
            plan.append((True, [cp], [cp], []))
        elif kind == "xchg":
            _, src, dst = job
            sends = [_remote(refs[src].at[2 * px + py], refs[dst].at[me], *sems(k), (px, py, c))
                     for k, (px, py) in enumerate(others)]
            arrivals = [_remote(refs[src].at[me], refs[dst].at[2 * px + py], *sems(k), (px, py, c))
                        for k, (px, py) in enumerate(others)]
            plan.append((True, sends, arrivals, []))
        else:
            _, name, layer = job
            ref = refs[name]
            cp = _remote(ref.at[layer, c], ref.at[layer, c], *sems(0), sibling)
            arrival = _remote(ref.at[layer, c], ref.at[layer, 1 - c], *sems(0), sibling)
            plan.append((True, [cp], [arrival], []))
    return plan


def _hosted_start(plan, now):
    for mine, sends, _, _ in plan:
        @pl.when(now & mine)
        def _(sends=sends):
            for cp in sends:
                cp.start()


def _hosted_finish(plan, now):
    for mine, sends, arrivals, sibling_arrivals in plan:
        @pl.when(now & mine)
        def _(sends=sends, arrivals=arrivals):
            for cp in arrivals:
                cp.wait_recv()
            for cp in sends:
                cp.wait_send()

        if sibling_arrivals:
            @pl.when(now & jnp.logical_not(mine))
            def _(sibling_arrivals=sibling_arrivals):
                for cp in sibling_arrivals:
                    cp.wait_recv()


def _pcall(body, hosted, *, name, grid, in_specs, out_specs, out_shape, semantics, scratch_shapes=(), aliases=None):
    aliases = dict(aliases or {})
    if hosted is None or not hosted.jobs:
        return pl.pallas_call(body, name=name, grid=grid, in_specs=in_specs, out_specs=out_specs,
                              out_shape=out_shape, scratch_shapes=list(scratch_shapes),
                              input_output_aliases=aliases, compiler_params=_params(*semantics))
    single = not isinstance(out_shape, (list, tuple))
    out_specs_l = [out_specs] if single else list(out_specs)
    out_shape_l = [out_shape] if single else list(out_shape)
    n_in, n_out, n_buf, n_job = len(in_specs), len(out_specs_l), len(hosted.names), len(hosted.jobs)

    def carrying(*refs):
        ins, outs = refs[:n_in], refs[n_in + n_buf:n_in + n_buf + n_out]
        bufs = refs[n_in + n_buf + n_out:n_in + 2 * n_buf + n_out]
        rest = refs[n_in + 2 * n_buf + n_out:]
        scratch, send_sems, recv_sems = rest[:-2], rest[-2], rest[-1]
        first, last = True, True
        for axis, size in enumerate(grid):
            first = first & (pl.program_id(axis) == 0)
            last = last & (pl.program_id(axis) == size - 1)
        plan = _hosted_plan(hosted, dict(zip(hosted.names, bufs)), send_sems, recv_sems)
        _hosted_start(plan, first)
        body(*ins, *outs, *scratch)
        _hosted_finish(plan, last)

    def run(*args):
        bufs = [hosted.pool[n] for n in hosted.names]
        sem = pltpu.SemaphoreType.DMA
        res = pl.pallas_call(
            carrying, name=name, grid=grid, in_specs=list(in_specs) + [_ANY] * n_buf,
            out_specs=out_specs_l + [_ANY] * n_buf,
            out_shape=out_shape_l + [SDS(b.shape, b.dtype) for b in bufs],
            scratch_shapes=list(scratch_shapes) + [sem((n_job, 3)), sem((n_job, 3))],
            input_output_aliases={**aliases, **{n_in + i: n_out + i for i in range(n_buf)}},
            compiler_params=pltpu.CompilerParams(dimension_semantics=semantics, vmem_limit_bytes=VMEM_LIMIT,
                                                 has_side_effects=True),
        )(*args, *bufs)
        hosted.pool.update(zip(hosted.names, res[n_out:]))
        return res[0] if single else res[:n_out]

    return run


def _dot(a, b):
    return jnp.dot(a, b, preferred_element_type=F32)


def _dot_nt(a, b):
    return lax.dot_general(a, b, NT, preferred_element_type=F32)


def _dot_tn(a, b):
    return lax.dot_general(a, b, TN, preferred_element_type=F32)


def norm_matmul(x, gain, w, layer, kind, name, hosted=None):
    T, D = x.shape
    if kind == "by_shard":
        tn = w.shape[3]
        N = N_CHIPS * tn
        w_spec = pl.BlockSpec((None, None, D, tn), lambda i, j: (j, layer, 0, 0))
        mm = _dot
    else:
        N = w.shape[0]
        tn = _tile(N, (1024, 512, 256, 128))
        w_spec = pl.BlockSpec((tn, D), lambda i, j: (j, 0))
        mm = _dot_nt
    tm = _tile(T, (1024, 512, 256, 128))

    def body(x_ref, g_ref, w_ref, y_ref, h_ref):
        @pl.when(pl.program_id(1) == 0)
        def _():
            xf = x_ref[...]
            r = lax.rsqrt(jnp.mean(xf * xf, axis=-1, keepdims=True) + RMS_EPS)
            h_ref[...] = ((xf * r) * g_ref[...]).astype(BF16)

        y_ref[...] = mm(h_ref[...], w_ref[...]).astype(BF16)

    return _pcall(
        body, hosted, name=name, grid=(T // tm, N // tn),
        in_specs=[pl.BlockSpec((tm, D), lambda i, j: (i, 0)),
                  pl.BlockSpec((None, 1, D), lambda i, j: (layer, 0, 0)),
                  w_spec],
        out_specs=[pl.BlockSpec((tm, tn), lambda i, j: (i, j)),
                   pl.BlockSpec((tm, D), lambda i, j: (i, 0))],
        out_shape=[SDS((T, N), BF16), SDS((T, D), BF16)],
        semantics=("arbitrary", "arbitrary"),
    )(x, gain, w)


def matmul_nt_normbwd(dys, w, layer, kind, x, gain, dres, name, hosted=None):
    T, D = x.shape
    width = dys[0].shape[1]
    if kind == "by_shard":
        tk = w.shape[3]
        w_spec = pl.BlockSpec((None, None, D, tk), lambda i, k: (k, layer, 0, 0))
        mm = _dot_nt
    else:
        tk = _tile(width, (1024, 512, 256, 128))
        w_spec = pl.BlockSpec((tk, D), lambda i, k: (k, 0))
        mm = _dot
    per = width // tk
    nk = per * len(dys)
    tm = _tile(T, (512, 256, 128))
    n_dy = len(dys)

    def dy_spec(p):
        return pl.BlockSpec((tm, tk), lambda i, k: (i, jnp.clip(k - p * per, 0, per - 1)))

    def body(*refs):
        dy_refs = refs[:n_dy]
        w_ref, x_ref, g_ref, dres_ref, dx_ref, dxb_ref, dg_ref, acc_ref = refs[n_dy:]
        i, k = pl.program_id(0), pl.program_id(1)

        @pl.when(k == 0)
        def _():
            acc_ref[...] = jnp.zeros_like(acc_ref)

        for p in range(n_dy):
            @pl.when((k >= p * per) & (k < (p + 1) * per))
            def _(p=p):
                acc_ref[...] += mm(dy_refs[p][...], w_ref[...])

        @pl.when(k == nk - 1)
        def _():
            xf = x_ref[...]
            r = lax.rsqrt(jnp.mean(xf * xf, axis=-1, keepdims=True) + RMS_EPS)
            xhat = xf * r
            dh = acc_ref[...]
            dhg = dh * g_ref[...]
            dx = dres_ref[...] + r * (dhg - xhat * jnp.mean(dhg * xhat, axis=-1, keepdims=True))
            dx_ref[...] = dx
            dxb_ref[...] = dx.astype(BF16)
            part = jnp.sum(dh * xhat, axis=0, keepdims=True)

            @pl.when(i == 0)
            def _():
                dg_ref[...] = part

            @pl.when(i > 0)
            def _():
                dg_ref[...] += part

    row = pl.BlockSpec((tm, D), lambda i, k: (i, 0))
    return _pcall(
        body, hosted, name=name, grid=(T // tm, nk),
        in_specs=[dy_spec(p) for p in range(n_dy)] + [
            w_spec, row, pl.BlockSpec((None, 1, D), lambda i, k: (layer, 0, 0)), row],
        out_specs=[row, row, pl.BlockSpec((1, D), lambda i, k: (0, 0))],
        out_shape=[SDS((T, D), F32), SDS((T, D), BF16), SDS((1, D), F32)],
        scratch_shapes=[pltpu.VMEM((tm, D), F32)],
        semantics=("arbitrary", "arbitrary"),
    )(*dys, w, x, gain, dres)


def matmul_tn(a, bs, name, b_col0=0, n_cols=None, by_dest=False, tn=None, tk=None, hosted=None):
    T, M = a.shape
    width = bs[0].shape[1]
    N = n_cols if n_cols else width * len(bs)
    tm = _tile(M, (1024, 512, 256, 128))
    tn = tn or _tile(N, (512, 256, 128))
    tk = tk or _tile(T, (4096, 2048, 1024, 512, 256))
    assert b_col0 % tn == 0 and width % tn == 0
    j0, per, nk, n_b = b_col0 // tn, width // tn, T // tk, len(bs)

    def b_spec(p):
        return pl.BlockSpec((tk, tn), lambda i, j, k: (k, jnp.clip(j0 + j - p * per, 0, per - 1)))

    def body(*refs):
        a_ref, b_refs = refs[0], refs[1:1 + n_b]
        o_ref, acc_ref = refs[-2], refs[-1]
        j, k = pl.program_id(1), pl.program_id(2)

        @pl.when(k == 0)
        def _():
            acc_ref[...] = jnp.zeros_like(acc_ref)

        for p in range(n_b):
            @pl.when((j0 + j >= p * per) & (j0 + j < (p + 1) * per))
            def _(p=p):
                acc_ref[...] += _dot_tn(a_ref[...], b_refs[p][...])

        @pl.when(k == nk - 1)
        def _():
            o_ref[...] = acc_ref[...].astype(BF16)

    if by_dest:
        cs = N // N_CHIPS
        npd = cs // tn
        out_shape = SDS((N_CHIPS, M, cs), BF16)
        out_spec = pl.BlockSpec((None, tm, tn), lambda i, j, k: (j // npd, i, j % npd))
    else:
        out_shape = SDS((M, N), BF16)
        out_spec = pl.BlockSpec((tm, tn), lambda i, j, k: (i, j))
    return _pcall(
        body, hosted, name=name, grid=(M // tm, N // tn, nk),
        in_specs=[pl.BlockSpec((tk, tm), lambda i, j, k: (k, i))] + [b_spec(p) for p in range(n_b)],
        out_specs=out_spec, out_shape=out_shape,
        scratch_shapes=[pltpu.VMEM((tm, tn), F32)],
        semantics=("arbitrary", "arbitrary", "arbitrary"),
    )(a, *bs)


def ffn_down_fwd(ab, w_down, layer, x1, hosted=None):
    T, D = x1.shape
    F = w_down.shape[1]
    tm = _tile(T, (512, 256, 128))
    tk = F // 2
    nk = F // tk

    def body(a_ref, b_ref, w_ref, x_ref, x2_ref, s_ref, acc_ref):
        k = pl.program_id(1)

        @pl.when(k == 0)
        def _():
            acc_ref[...] = x_ref[...]

        a = a_ref[...].astype(F32)
        s = (a * _sigmoid(a) * b_ref[...].astype(F32)).astype(BF16)
        s_ref[...] = s
        acc_ref[...] += _dot(s, w_ref[...])

        @pl.when(k == nk - 1)
        def _():
            x2_ref[...] = acc_ref[...]

    return _pcall(
        body, hosted, name="ffn_down_fwd", grid=(T // tm, nk),
        in_specs=[pl.BlockSpec((tm, tk), lambda i, k: (i, k)),
                  pl.BlockSpec((tm, tk), lambda i, k: (i, nk + k)),
                  pl.BlockSpec((None, tk, D), lambda i, k: (layer, k, 0)),
                  pl.BlockSpec((tm, D), lambda i, k: (i, 0))],
        out_specs=[pl.BlockSpec((tm, D), lambda i, k: (i, 0)),
                   pl.BlockSpec((tm, tk), lambda i, k: (i, k))],
        out_shape=[SDS((T, D), F32), SDS((T, F), BF16)],
        scratch_shapes=[pltpu.VMEM((tm, D), F32)],
        semantics=("arbitrary", "arbitrary"),
    )(ab, ab, w_down, x1)


def ffn_down_bwd(dx2b, w_down, layer, ab, hosted=None):
    T, D = dx2b.shape
    F = w_down.shape[1]
    tm = _tile(T, (512, 256, 128))
    tn = F // 2
    nj = F // tn

    def body(dx_ref, w_ref, a_ref, b_ref, da_ref, db_ref):
        ds = _dot_nt(dx_ref[...], w_ref[...])
        a = a_ref[...].astype(F32)
        sg = _sigmoid(a)
        da_ref[...] = (ds * b_ref[...].astype(F32) * (sg * (1.0 + a * (1.0 - sg)))).astype(BF16)
        db_ref[...] = (ds * (a * sg)).astype(BF16)

    blk = pl.BlockSpec((tm, tn), lambda i, j: (i, j))
    return _pcall(
        body, hosted, name="ffn_down_bwd", grid=(T // tm, nj),
        in_specs=[pl.BlockSpec((tm, D), lambda i, j: (i, 0)),
                  pl.BlockSpec((None, tn, D), lambda i, j: (layer, j, 0)),
                  blk, pl.BlockSpec((tm, tn), lambda i, j: (i, nj + j))],
        out_specs=[blk, blk],
        out_shape=[SDS((T, F), BF16), SDS((T, F), BF16)],
        semantics=("arbitrary", "arbitrary"),
    )(dx2b, w_down, ab, ab)


def _mix_specs(tm, D, layer):
    cs = D // N_CHIPS
    row = lambda w: pl.BlockSpec((tm, w), lambda i: (i, 0))
    wp = pl.BlockSpec((N_CHIPS, None, BRANCH_W, cs), lambda i: (0, layer, 0, 0))
    wo = pl.BlockSpec((None, N_CHIPS, cs, D), lambda i: (layer, 0, 0, 0))
    bg = pl.BlockSpec((None, 1, 3 * D), lambda i: (layer, 0, 0))
    return row, wp, wo, bg


def mix_fwd(ao, po, co, proj, b_gate, wpa, wpp, wpc, w_out, layer, x, hosted=None):
    T, D = x.shape
    cs = D // N_CHIPS
    tm = _tile(T, (256, 128))
    row, wp, wo, bg = _mix_specs(tm, D, layer)

    def body(ao_ref, po_ref, co_ref, g_ref, bg_ref, wpa_ref, wpp_ref, wpc_ref, wo_ref, x_ref,
             x1_ref, ys_ref, mixed_ref):
        mixed = jnp.zeros((tm, D), F32)
        for n, (br, wp_ref) in enumerate(((ao_ref, wpa_ref), (po_ref, wpp_ref), (co_ref, wpc_ref))):
            y = jnp.concatenate([_dot(br[...], wp_ref[j]) for j in range(N_CHIPS)], axis=1)
            cols = slice(n * D, (n + 1) * D)
            gate = _sigmoid(g_ref[:, cols].astype(F32) + bg_ref[:, cols])
            ys_ref[:, cols] = y.astype(BF16)
            mixed = mixed + gate * y
        mb = mixed.astype(BF16)
        mixed_ref[...] = mb
        acc = x_ref[...]
        for j in range(N_CHIPS):
            acc = acc + _dot(mb[:, j * cs:(j + 1) * cs], wo_ref[j])
        x1_ref[...] = acc

    return _pcall(
        body, hosted, name="mix_fwd", grid=(T // tm,),
        in_specs=[row(BRANCH_W), row(BRANCH_W), row(BRANCH_W), row(3 * D), bg, wp, wp, wp, wo, row(D)],
        out_specs=[row(D), row(3 * D), row(D)],
        out_shape=[SDS((T, D), F32), SDS((T, 3 * D), BF16), SDS((T, D), BF16)],
        semantics=("arbitrary",),
    )(ao, po, co, proj, b_gate, wpa, wpp, wpc, w_out, x)


def mix_bwd(dx1b, w_out, proj, b_gate, ys, wpa, wpp, wpc, layer, width, hosted=None):
    T, D = dx1b.shape
    cs = D // N_CHIPS
    tm = _tile(T, (256, 128))
    row, wp, wo, bg = _mix_specs(tm, D, layer)

    def body(dx_ref, wo_ref, g_ref, bg_ref, ys_ref, wpa_ref, wpp_ref, wpc_ref,
             dys_ref, dg_ref, dao_ref, dpo_ref, dco_ref, dbg_ref):
        i = pl.program_id(0)
        dx = dx_ref[...]
        dmixed = jnp.concatenate([_dot_nt(dx, wo_ref[j]) for j in range(N_CHIPS)], axis=1)
        for n, (wp_ref, dbr) in enumerate(((wpa_ref, dao_ref), (wpp_ref, dpo_ref), (wpc_ref, dco_ref))):
            cols = slice(n * D, (n + 1) * D)
            gate = _sigmoid(g_ref[:, cols].astype(F32) + bg_ref[:, cols])
            dy = (dmixed * gate).astype(BF16)
            dys_ref[:, cols] = dy
            dgp = dmixed * ys_ref[:, cols].astype(F32) * gate * (1.0 - gate)
            dg_ref[:, cols] = dgp.astype(BF16)
            part = jnp.sum(dgp, axis=0, keepdims=True)

            @pl.when(i == 0)
            def _():
                dbg_ref[:, cols] = part

            @pl.when(i > 0)
            def _():
                dbg_ref[:, cols] += part

            acc = jnp.zeros((tm, BRANCH_W), F32)
            for j in range(N_CHIPS):
                acc = acc + _dot_nt(dy[:, j * cs:(j + 1) * cs], wp_ref[j])
            dbr[...] = acc.astype(BF16)

    return _pcall(
        body, hosted, name="mix_bwd", grid=(T // tm,),
        in_specs=[row(D), wo, row(3 * D), bg, row(3 * D), wp, wp, wp],
        out_specs=[row(3 * D), row(3 * D), row(BRANCH_W), row(BRANCH_W), row(BRANCH_W),
                   pl.BlockSpec((1, 3 * D), lambda i: (0, 0))],
        out_shape=[SDS((T, 3 * D), BF16), SDS((T, width), BF16), SDS((T, BRANCH_W), BF16),
                   SDS((T, BRANCH_W), BF16), SDS((T, BRANCH_W), BF16), SDS((1, 3 * D), F32)],
        semantics=("arbitrary",),
    )(dx1b, w_out, proj, b_gate, ys, wpa, wpp, wpc)


def loss_head(x2, gain, target):
    T, D = x2.shape
    tm = _tile(T, (512, 256, 128))

    def body(x_ref, g_ref, t_ref, loss_ref, dx_ref, dxb_ref, dg_ref):
        i = pl.program_id(0)
        xf = x_ref[...]
        g = g_ref[...]
        r = lax.rsqrt(jnp.mean(xf * xf, axis=-1, keepdims=True) + RMS_EPS)
        xhat = xf * r
        diff = xhat * g - t_ref[...]
        part_loss = 0.5 * jnp.sum(jnp.mean(diff * diff, axis=-1, keepdims=True), axis=0, keepdims=True)
        dy = diff * (1.0 / D)
        dhg = dy * g
        dx = r * (dhg - xhat * jnp.mean(dhg * xhat, axis=-1, keepdims=True))
        dx_ref[...] = dx
        dxb_ref[...] = dx.astype(BF16)
        part_g = jnp.sum(dy * xhat, axis=0, keepdims=True)
        part_l = jnp.broadcast_to(part_loss, (1, LANES))

        @pl.when(i == 0)
        def _():
            dg_ref[...] = part_g
            loss_ref[...] = part_l

        @pl.when(i > 0)
        def _():
            dg_ref[...] += part_g
            loss_ref[...] += part_l

    row = pl.BlockSpec((tm, D), lambda i: (i, 0))
    return pl.pallas_call(
        body, name="loss_head", grid=(T // tm,),
        in_specs=[row, pl.BlockSpec((1, D), lambda i: (0, 0)), row],
        out_specs=[pl.BlockSpec((1, LANES), lambda i: (0, 0)), row, row, pl.BlockSpec((1, D), lambda i: (0, 0))],
        out_shape=[SDS((1, LANES), F32), SDS((T, D), F32), SDS((T, D), BF16), SDS((1, D), F32)],
        compiler_params=_params("arbitrary"),
    )(x2, gain, target)


def _placement_constants():
    w = HEADS * HEAD_PAD
    pq = np.zeros((BRANCH_W, w), np.float32)
    pk = np.zeros((BRANCH_W, w), np.float32)
    pfq = np.zeros((3, LANES, w), np.float32)
    pfk = np.zeros((3, LANES, w), np.float32)
    cq = np.zeros((1, w), np.float32)
    ck = np.zeros((1, w), np.float32)
    eq = np.zeros((w, LANES), np.float32)
    ek = np.zeros((w, LANES), np.float32)
    for h in range(HEADS):
        for d in range(HEAD_DIM):
            pq[h * HEAD_DIM + d, h * HEAD_PAD + d] = HEAD_DIM ** -0.5
            pk[h * HEAD_DIM + d, h * HEAD_PAD + d] = 1.0
        for i in range(3):
            pfq[i, h, h * HEAD_PAD + HEAD_DIM + i] = 1.0
            pfk[i, h, h * HEAD_PAD + HEAD_DIM + 3 + i] = -1.0
            cq[0, h * HEAD_PAD + HEAD_DIM + 3 + i] = 1.0
            ck[0, h * HEAD_PAD + HEAD_DIM + i] = 1.0
        eq[h * HEAD_PAD + HEAD_DIM, h] = 1.0
        ek[h * HEAD_PAD + HEAD_DIM + 3, h] = -1.0
    bf = lambda a: jnp.asarray(a, BF16)
    return dict(pq=bf(pq), pk=bf(pk), pfq=bf(pfq), pfk=bf(pfk), cq=jnp.asarray(cq), ck=jnp.asarray(ck),
                pqkt=bf(np.concatenate([pq.T, pk.T], axis=0)), eq=bf(eq), ek=bf(ek))


def attn_prep(proj3, bf_rows, layer, cst, lay, hosted=None):
    Bl, S, _ = proj3.shape
    ts = ATTN_BLOCK
    w = HEADS * HEAD_PAD

    def body(q_ref, k_ref, f_ref, bf_ref, pq_ref, pk_ref, pfq_ref, pfk_ref, cq_ref, ck_ref,
             qa_ref, ka_ref, carry_ref):
        @pl.when(pl.program_id(1) == 0)
        def _():
            carry_ref[...] = jnp.zeros_like(carry_ref)

        z = f_ref[...].astype(F32) + bf_ref[...]
        logf = jnp.minimum(z, 0.0) - jnp.log(1.0 + jnp.exp(-jnp.abs(z)))
        r = lax.broadcasted_iota(jnp.int32, (ts, ts), 0)
        c = lax.broadcasted_iota(jnp.int32, (ts, ts), 1)
        tri = jnp.where(r >= c, 1.0, 0.0).astype(BF16)
        fcum = carry_ref[...]
        for part in _split3(logf):
            fcum = fcum + _dot(tri, part)
        carry_ref[...] = fcum[ts - 1:ts, :]
        qa = _dot(q_ref[...], pq_ref[...]) + cq_ref[...]
        ka = _dot(k_ref[...], pk_ref[...]) + ck_ref[...]
        for i, part in enumerate(_split3(fcum)):
            qa = qa + _dot(part, pfq_ref[i])
            ka = ka + _dot(part, pfk_ref[i])
        qa_ref[...] = qa.astype(BF16)
        ka_ref[...] = ka.astype(BF16)

    cfull = lambda shape: pl.BlockSpec(shape, lambda b, s: (0,) * len(shape))
    return _pcall(
        body, hosted, name="attn_prep", grid=(Bl, S // ts),
        in_specs=[pl.BlockSpec((None, ts, BRANCH_W), lambda b, s: (b, s, lay["q"] // BRANCH_W)),
                  pl.BlockSpec((None, ts, BRANCH_W), lambda b, s: (b, s, lay["k"] // BRANCH_W)),
                  pl.BlockSpec((None, ts, LANES), lambda b, s: (b, s, lay["f"] // LANES)),
                  pl.BlockSpec((None, 1, LANES), lambda b, s: (layer, 0, 0)),
                  cfull((BRANCH_W, w)), cfull((BRANCH_W, w)),
                  cfull((3, LANES, w)), cfull((3, LANES, w)), cfull((1, w)), cfull((1, w))],
        out_specs=[pl.BlockSpec((None, ts, w), lambda b, s: (b, s, 0)),
                   pl.BlockSpec((None, ts, w), lambda b, s: (b, s, 0))],
        out_shape=[SDS((Bl, S, w), BF16), SDS((Bl, S, w), BF16)],
        scratch_shapes=[pltpu.VMEM((1, LANES), F32)],
        semantics=("arbitrary", "arbitrary"),
    )(proj3, proj3, proj3, bf_rows, cst["pq"], cst["pk"], cst["pfq"], cst["pfk"], cst["cq"], cst["ck"])


def attn_fwd(qa, ka, proj3, lay, hosted=None):
    Bl, S, _ = qa.shape
    tq = ATTN_BLOCK
    nq = S // tq
    pairs = HEADS // 2
    pw = 2 * HEAD_PAD
    vw = 2 * HEAD_DIM

    def body(qa_ref, ka_ref, v_ref, o_ref, lse_ref):
        row = lax.broadcasted_iota(jnp.int32, (tq, tq), 0)
        col = lax.broadcasted_iota(jnp.int32, (tq, tq), 1)
        causal = row <= col
        for i in range(nq):
            nk = (i + 1) * tq
            rows = slice(i * tq, nk)
            o_t = []
            for h in range(2):
                hs = slice(h * HEAD_PAD, (h + 1) * HEAD_PAD)
                st = _dot_nt(ka_ref[0:nk, hs], qa_ref[rows, hs])
                diag = jnp.where(causal, st[nk - tq:], NEG_INF)
                m = jnp.max(diag, axis=0, keepdims=True)
                if i:
                    m = jnp.maximum(m, jnp.max(st[:nk - tq], axis=0, keepdims=True))
                p_diag = jnp.exp(diag - m)
                l = jnp.sum(p_diag, axis=0, keepdims=True)
                if i:
                    p_top = jnp.exp(st[:nk - tq] - m)
                    l = l + jnp.sum(p_top, axis=0, keepdims=True)
                    p = jnp.concatenate([p_top.astype(BF16), p_diag.astype(BF16)], axis=0)
                else:
                    p = p_diag.astype(BF16)
                acc = _dot_tn(v_ref[0:nk, :], p)
                o_t.append(acc[h * HEAD_DIM:(h + 1) * HEAD_DIM, :] / l)
                lse_ref[h:h + 1, rows] = m + jnp.log(l)
            o_ref[rows, :] = jnp.concatenate(o_t, axis=0).T.astype(BF16)

    return _pcall(
        body, hosted, name="attn_fwd", grid=(Bl, pairs),
        in_specs=[pl.BlockSpec((None, S, pw), lambda b, p: (b, 0, p)),
                  pl.BlockSpec((None, S, pw), lambda b, p: (b, 0, p)),
                  pl.BlockSpec((None, S, vw), lambda b, p: (b, 0, lay["v"] // vw + p))],
        out_specs=[pl.BlockSpec((None, S, vw), lambda b, p: (b, 0, p)),
                   pl.BlockSpec((None, None, 2, S), lambda b, p: (b, p, 0, 0))],
        out_shape=[SDS((Bl, S, BRANCH_W), BF16), SDS((Bl, pairs, 2, S), F32)],
        semantics=("arbitrary", "arbitrary"),
    )(qa, ka, proj3)


def attn_bwd(qa, ka, proj3, dao, ao, lse, dproj3, lay):
    Bl, S, _ = qa.shape
    tk = ATTN_BLOCK
    nq = S // tk
    pairs = HEADS // 2
    pw = 2 * HEAD_PAD
    vw = 2 * HEAD_DIM

    def body(qa_ref, ka_ref, v_ref, do_ref, o_ref, lse_ref, _, dqa_ref, dka_ref, dv_ref):
        row = lax.broadcasted_iota(jnp.int32, (tk, tk), 0)
        col = lax.broadcasted_iota(jnp.int32, (tk, tk), 1)
        causal = row <= col
        lane8 = lax.broadcasted_iota(jnp.int32, (8, vw), 1)
        lane_s = lax.broadcasted_iota(jnp.int32, (S, vw), 1)
        lane_k = lax.broadcasted_iota(jnp.int32, (tk, vw), 1)
        doo = do_ref[...].astype(F32) * o_ref[...].astype(F32)
        hi = doo.astype(BF16)
        lo = (doo - hi.astype(F32)).astype(BF16)
        delta, v_head = [], []
        for h in range(2):
            sel = jnp.where((lane8 >= h * HEAD_DIM) & (lane8 < (h + 1) * HEAD_DIM), 1.0, 0.0).astype(BF16)
            delta.append((_dot_nt(sel, hi) + _dot_nt(sel, lo))[0:1, :])
            in_head = (lane_s >= h * HEAD_DIM) & (lane_s < (h + 1) * HEAD_DIM)
            v_head.append(jnp.where(in_head, v_ref[...], jnp.zeros_like(v_ref[...])))
        dqa_ref[...] = jnp.zeros_like(dqa_ref)
        for j in range(nq):
            q0 = j * tk
            krows = slice(q0, q0 + tk)
            do = do_ref[q0:, :]
            dvs = []
            for h in range(2):
                hs = slice(h * HEAD_PAD, (h + 1) * HEAD_PAD)
                k = ka_ref[krows, hs]
                q = qa_ref[q0:, hs]
                st = _dot_nt(k, q)
                p = jnp.exp(st - lse_ref[h:h + 1, q0:])
                p_diag = jnp.where(causal, p[:, :tk], 0.0)
                p = jnp.concatenate([p_diag, p[:, tk:]], axis=1) if j < nq - 1 else p_diag
                dvs.append(_dot(p.astype(BF16), do))
                dpt = _dot_nt(v_head[h][krows, :], do)
                ds = (p * (dpt - delta[h][:, q0:])).astype(BF16)
                dka_ref[krows, hs] = _dot(ds, q)
                dqa_ref[q0:, hs] += _dot_tn(ds, k)
            dv_ref[krows, :] = jnp.where(lane_k < HEAD_DIM, dvs[0], dvs[1]).astype(BF16)

    seq = lambda w, c0=0: pl.BlockSpec((None, S, w), lambda b, p: (b, 0, c0 + p))
    return pl.pallas_call(
        body, name="attn_bwd", grid=(Bl, pairs),
        in_specs=[seq(pw), seq(pw), seq(vw, lay["v"] // vw), seq(vw), seq(vw),
                  pl.BlockSpec((None, None, 2, S), lambda b, p: (b, p, 0, 0)), _ANY],
        out_specs=[seq(pw), seq(pw), seq(vw, lay["v"] // vw)],
        out_shape=[SDS((Bl, S, HEADS * HEAD_PAD), F32), SDS((Bl, S, HEADS * HEAD_PAD), F32),
                   SDS(dproj3.shape, BF16)],
        input_output_aliases={6: 2},
        compiler_params=_params("arbitrary", "arbitrary"),
    )(qa, ka, proj3, dao, ao, lse, dproj3)


def attn_post(dqa, dka, proj3, bf_rows, layer, dproj3, cst, lay):
    Bl, S, w = dqa.shape
    ts = ATTN_BLOCK
    ns = S // ts
    qkf = 2 * BRANCH_W + F_PAD

    def body(dqa_ref, dka_ref, f_ref, bf_ref, pqkt_ref, eq_ref, ek_ref, _, dqkf_ref, dbf_ref, carry_ref):
        b, s = pl.program_id(0), pl.program_id(1)

        @pl.when(s == 0)
        def _():
            carry_ref[...] = jnp.zeros_like(carry_ref)

        dqa_v, dka_v = dqa_ref[...], dka_ref[...]
        qh = dqa_v.astype(BF16)
        kh = dka_v.astype(BF16)
        dqkf_ref[:, :BRANCH_W] = _dot(qh, pqkt_ref[:w, :]).astype(BF16)
        dqkf_ref[:, BRANCH_W:2 * BRANCH_W] = _dot(kh, pqkt_ref[w:, :]).astype(BF16)
        ql = (dqa_v - qh.astype(F32)).astype(BF16)
        kl = (dka_v - kh.astype(F32)).astype(BF16)
        d_f = (_dot(qh, eq_ref[...]) + _dot(ql, eq_ref[...])) + (_dot(kh, ek_ref[...]) + _dot(kl, ek_ref[...]))
        r = lax.broadcasted_iota(jnp.int32, (ts, ts), 0)
        c = lax.broadcasted_iota(jnp.int32, (ts, ts), 1)
        triu = jnp.where(c >= r, 1.0, 0.0).astype(BF16)
        rev = carry_ref[...]
        for part in _split3(d_f):
            rev = rev + _dot(triu, part)
        carry_ref[...] = rev[0:1, :]
        z = f_ref[...].astype(F32) + bf_ref[...]
        lane = lax.broadcasted_iota(jnp.int32, (ts, LANES), 1)
        dfl = jnp.where(lane < HEADS, rev / (1.0 + jnp.exp(z)), 0.0)
        dqkf_ref[:, 2 * BRANCH_W:] = jnp.concatenate(
            [dfl.astype(BF16), jnp.zeros((ts, F_PAD - LANES), BF16)], axis=1)
        part = jnp.sum(dfl, axis=0, keepdims=True)

        @pl.when((b == 0) & (s == 0))
        def _():
            dbf_ref[...] = part

        @pl.when((b > 0) | (s > 0))
        def _():
            dbf_ref[...] += part

    assert lay["q"] % qkf == 0
    cfull = lambda shape: pl.BlockSpec(shape, lambda b, s: (0,) * len(shape))
    rev_blk = lambda wd, c0=0: pl.BlockSpec((None, ts, wd), lambda b, s: (b, ns - 1 - s, c0))
    return pl.pallas_call(
        body, name="attn_post", grid=(Bl, ns),
        in_specs=[rev_blk(w), rev_blk(w), rev_blk(LANES, lay["f"] // LANES),
                  pl.BlockSpec((None, 1, LANES), lambda b, s: (layer, 0, 0)),
                  cfull((2 * w, BRANCH_W)), cfull((w, LANES)), cfull((w, LANES)), _ANY],
        out_specs=[rev_blk(qkf, lay["q"] // qkf), cfull((1, LANES))],
        out_shape=[SDS(dproj3.shape, BF16), SDS((1, LANES), F32)],
        scratch_shapes=[pltpu.VMEM((1, LANES), F32)],
        input_output_aliases={7: 0},
        compiler_params=_params("arbitrary", "arbitrary"),
    )(dqa, dka, proj3, bf_rows, cst["pqkt"], cst["eq"], cst["ek"], dproj3)


def _shift_down(x, k, row):
    return jnp.where(row >= k, pltpu.roll(x, k, axis=0), 0.0)


def _shift_up(x, k, row):
    n = x.shape[0]
    return jnp.where(row < n - k, pltpu.roll(x, n - k, axis=0), 0.0)


def _window_sum(x, g, row, shift):
    s2 = x + shift(x, 1, row)
    s4 = s2 + shift(s2, 2, row)
    s8 = s4 + shift(s4, 4, row)
    s16 = s8 + shift(s8, 8, row)
    return jnp.where(g == 0, s2, jnp.where(g == 1, s4, jnp.where(g == 2, s8, s16)))


def _window_count(g, row):
    wnd = jnp.where(g == 0, 2, jnp.where(g == 1, 4, jnp.where(g == 2, 8, 16)))
    return jnp.minimum(row + 1, wnd).astype(F32)


def _group_columns(ref):
    return [ref[:, n * GROUP_W:(n + 1) * GROUP_W].astype(F32) for n in range(4)]


def poolconv_fwd(proj3, pool_w, pool_scale, conv_w, layer, lay, hosted=None):
    Bl, S, _ = proj3.shape

    def body(x_ref, pw_ref, ps_ref, cw_ref, po_ref, co_ref):
        g = pl.program_id(1)
        row = lax.broadcasted_iota(jnp.int32, (S, GROUP_W), 0)
        u, cv, cb, cc = _group_columns(x_ref)
        d = _window_sum(u, g, row, _shift_down) / _window_count(g, row) - u
        po_ref[...] = (_dot(d.astype(BF16), pw_ref[...]) * ps_ref[...]).astype(BF16)
        z = cc * cv
        y = cw_ref[0:1, :] * _shift_down(z, 2, row) + cw_ref[1:2, :] * _shift_down(z, 1, row) + cw_ref[2:3, :] * z
        co_ref[...] = (cb * y).astype(BF16)

    out = pl.BlockSpec((None, S, GROUP_W), lambda b, g: (b, 0, g))
    return _pcall(
        body, hosted, name="poolconv_fwd", grid=(Bl, N_GROUPS),
        in_specs=[pl.BlockSpec((None, S, BRANCH_W), lambda b, g: (b, 0, lay["pc"] // BRANCH_W + g)),
                  pl.BlockSpec((None, None, GROUP_W, GROUP_W), lambda b, g: (layer, g, 0, 0)),
                  pl.BlockSpec((None, 1, GROUP_W), lambda b, g: (layer, 0, g)),
                  pl.BlockSpec((None, None, 3, GROUP_W), lambda b, g: (g, layer, 0, 0))],
        out_specs=[out, out],
        out_shape=[SDS((Bl, S, BRANCH_W), BF16), SDS((Bl, S, BRANCH_W), BF16)],
        semantics=("arbitrary", "arbitrary"),
    )(proj3, pool_w, pool_scale, conv_w)


def poolconv_bwd(proj3, dpo, dco, pool_w, pool_scale, conv_w, layer, dproj3, lay):
    Bl, S, _ = proj3.shape

    def body(x_ref, dpo_ref, dco_ref, pw_ref, ps_ref, cw_ref, _, dx_ref, dpw_ref, dps_ref, dcw_ref):
        g, b = pl.program_id(0), pl.program_id(1)
        row = lax.broadcasted_iota(jnp.int32, (S, GROUP_W), 0)
        cnt = _window_count(g, row)
        u, cv, cb, cc = _group_columns(x_ref)
        d = (_window_sum(u, g, row, _shift_down) / cnt - u).astype(BF16)
        pw = pw_ref[...]
        ypre = _dot(d, pw)
        dpo_v = dpo_ref[...].astype(F32)
        dps = jnp.sum(dpo_v * ypre, axis=0, keepdims=True)
        dyp = (dpo_v * ps_ref[...]).astype(BF16)
        dpw = _dot_tn(d, dyp)
        dd = _dot_nt(dyp, pw)
        dx_ref[:, 0:GROUP_W] = (_window_sum(dd / cnt, g, row, _shift_up) - dd).astype(BF16)

        z = cc * cv
        z1, z2 = _shift_down(z, 1, row), _shift_down(z, 2, row)
        w0, w1, w2 = cw_ref[0:1, :], cw_ref[1:2, :], cw_ref[2:3, :]
        y = w0 * z2 + w1 * z1 + w2 * z
        dco_v = dco_ref[...].astype(F32)
        dy = dco_v * cb
        dz = w0 * _shift_up(dy, 2, row) + w1 * _shift_up(dy, 1, row) + w2 * dy
        dx_ref[:, GROUP_W:2 * GROUP_W] = (dz * cc).astype(BF16)
        dx_ref[:, 2 * GROUP_W:3 * GROUP_W] = (dco_v * y).astype(BF16)
        dx_ref[:, 3 * GROUP_W:] = (dz * cv).astype(BF16)
        dcw = jnp.concatenate([jnp.sum(dy * z2, axis=0, keepdims=True),
                               jnp.sum(dy * z1, axis=0, keepdims=True),
                               jnp.sum(dy * z, axis=0, keepdims=True)], axis=0)

        @pl.when(b == 0)
        def _():
            dpw_ref[...] = dpw
            dps_ref[...] = dps
            dcw_ref[...] = dcw

        @pl.when(b > 0)
        def _():
            dpw_ref[...] += dpw
            dps_ref[...] += dps
            dcw_ref[...] += dcw

    blk = pl.BlockSpec((None, S, GROUP_W), lambda g, b: (b, 0, g))
    pc = pl.BlockSpec((None, S, BRANCH_W), lambda g, b: (b, 0, lay["pc"] // BRANCH_W + g))
    return pl.pallas_call(
        body, name="poolconv_bwd", grid=(N_GROUPS, Bl),
        in_specs=[pc, blk, blk,
                  pl.BlockSpec((None, None, GROUP_W, GROUP_W), lambda g, b: (layer, g, 0, 0)),
                  pl.BlockSpec((None, 1, GROUP_W), lambda g, b: (layer, 0, g)),
                  pl.BlockSpec((None, None, 3, GROUP_W), lambda g, b: (g, layer, 0, 0)), _ANY],
        out_specs=[pc, pl.BlockSpec((None, GROUP_W, GROUP_W), lambda g, b: (g, 0, 0)),
                   pl.BlockSpec((1, GROUP_W), lambda g, b: (0, g)),
                   pl.BlockSpec((None, 3, GROUP_W), lambda g, b: (g, 0, 0))],
        out_shape=[SDS(dproj3.shape, BF16), SDS((N_GROUPS, GROUP_W, GROUP_W), F32), SDS((1, BRANCH_W), F32),
                   SDS((N_GROUPS, 3, GROUP_W), F32)],
        input_output_aliases={6: 0},
        compiler_params=_params("arbitrary", "arbitrary"),
    )(proj3, dpo, dco, pool_w, pool_scale, conv_w, dproj3)


def _tile_2d(rows, cols, n_arrays):
    budget = VMEM_LIMIT // 2
    lanes = -(-cols // LANES) * LANES
    if rows % 8 == 0:
        for t in (2048, 1024, 512, 256, 128, 64, 32, 16, 8):
            if rows % t == 0 and 2 * n_arrays * t * lanes * 4 <= budget:
                return t, cols
    for t in (1024, 512, 256, 128):
        if cols % t == 0 and 2 * n_arrays * (rows + 8) * t * 4 <= budget:
            return rows, t
    return rows, cols


def add_pair(kept, layer, where, received, name):
    _, n, _, R, C = kept.shape
    tr, tc = _tile_2d(R, C, 3)

    def body(where_ref, a_ref, b_ref, o_ref):
        o_ref[...] = (a_ref[...].astype(F32) + b_ref[...].astype(F32)).astype(BF16)

    blk = pl.BlockSpec((None, tr, tc), lambda d, i, j, where_ref: (d, i, j))
    grid_spec = pltpu.PrefetchScalarGridSpec(
        num_scalar_prefetch=1, grid=(n, R // tr, C // tc),
        in_specs=[pl.BlockSpec((None, None, None, tr, tc),
                               lambda d, i, j, where_ref: (layer, d, where_ref[0], i, j)), blk],
        out_specs=blk)
    return pl.pallas_call(body, name=name, grid_spec=grid_spec, out_shape=SDS((n, R, C), BF16),
                          compiler_params=_params("arbitrary", "arbitrary", "arbitrary"))(where, kept, received)


def add_chips(arrived, own, layer, where, n_layers, prev, name):
    _, R, C = arrived.shape
    tr, tc = _tile_2d(R, C, 6)

    def body(where_ref, a0, a1, a2, a3, own_ref, *rest):
        o_ref = rest[-1]
        chip = where_ref[1]
        acc = None
        for j, a_ref in enumerate((a0, a1, a2, a3)):
            term = jnp.where(chip == j, own_ref[...], a_ref[...]).astype(F32)
            acc = term if acc is None else acc + term
        o_ref[...] = acc

    def slot(j):
        return pl.BlockSpec((None, tr, tc), lambda i, k, where_ref, j=j: (
            jnp.where(where_ref[1] == j, (j + 1) % N_CHIPS, j), i, k))

    in_specs = [slot(j) for j in range(N_CHIPS)] + [
        pl.BlockSpec((None, tr, tc), lambda i, k, where_ref: (where_ref[1], i, k))]
    args = [where, arrived, arrived, arrived, arrived, own]
    aliases = {}
    if prev is not None:
        in_specs.append(_ANY)
        args.append(prev)
        aliases = {len(args) - 1: 0}
    grid_spec = pltpu.PrefetchScalarGridSpec(
        num_scalar_prefetch=1, grid=(R // tr, C // tc), in_specs=in_specs,
        out_specs=pl.BlockSpec((None, None, tr, tc), lambda i, k, where_ref: (layer, where_ref[0], i, k)))
    return pl.pallas_call(body, name=name, grid_spec=grid_spec, out_shape=SDS((n_layers, 2, R, C), F32),
                          input_output_aliases=aliases,
                          compiler_params=_params("arbitrary", "arbitrary"))(*args)


def adamw(w, g, m, v, name):
    if w.ndim == 2:
        R, C = w.shape
        tr, _ = _tile_2d(R, C, 7)
        grid, blk = (R // tr,), pl.BlockSpec((tr, C), lambda i: (i, 0))
    else:
        N, r, C = w.shape
        tn = max(t for t in range(1, N + 1) if N % t == 0 and t * r * C * 4 <= 512 * 1024)
        grid, blk = (N // tn,), pl.BlockSpec((tn, r, C), lambda i: (i, 0, 0))

    def body(w_ref, g_ref, m_ref, v_ref, d_ref, nm_ref, nv_ref):
        gv = g_ref[...]
        m_new = ADAM_B1 * m_ref[...] + (1.0 - ADAM_B1) * gv
        v_new = ADAM_B2 * v_ref[...] + (1.0 - ADAM_B2) * (gv * gv)
        m_hat = m_new / (1.0 - ADAM_B1 ** ADAM_STEP)
        v_hat = v_new / (1.0 - ADAM_B2 ** ADAM_STEP)
        d_ref[...] = -ADAM_LR * (m_hat / (jnp.sqrt(v_hat) + ADAM_EPS) + ADAM_WD * w_ref[...])
        nm_ref[...] = m_new
        nv_ref[...] = v_new

    out = SDS(w.shape, F32)
    return pl.pallas_call(body, name=name, grid=grid, in_specs=[blk] * 4, out_specs=[blk] * 3,
                          out_shape=[out, out, out], compiler_params=_params("arbitrary"))(w, g, m, v)


_COMM = pltpu.CompilerParams(has_side_effects=True)


def gather_buffers(shards):
    me_chip = 2 * lax.axis_index("x") + lax.axis_index("y")
    pool = {}
    for name, sh in shards.items():
        L, r, c = sh.shape
        if name in ROW_SHARDED:
            pool[name] = lax.dynamic_update_slice(lax.empty((L, N_CHIPS, r, c), sh.dtype), sh[:, None],
                                                  (0, me_chip, 0, 0))
        else:
            pool[name] = lax.dynamic_update_slice(lax.empty((N_CHIPS, L, r, c), sh.dtype), sh[None],
                                                  (me_chip, 0, 0, 0))
    return pool


def comm_now(pool, stages, name):
    stages = [Hosted(pool, jobs) for jobs in stages]
    names = sorted({m for st in stages for m in st.names})
    n = len(names)

    def body(*refs):
        bufs = dict(zip(names, refs[n:2 * n]))
        sems = refs[2 * n:]
        for i, st in enumerate(stages):
            plan = _hosted_plan(st, bufs, sems[2 * i], sems[2 * i + 1])
            _hosted_start(plan, True)
            _hosted_finish(plan, True)

    sem = pltpu.SemaphoreType.DMA
    scratch = []
    for st in stages:
        scratch += [sem((len(st.jobs), 3)), sem((len(st.jobs), 3))]
    res = pl.pallas_call(
        body, name=name, in_specs=[_ANY] * n, out_specs=[_ANY] * n,
        out_shape=[SDS(pool[m].shape, pool[m].dtype) for m in names],
        scratch_shapes=scratch, input_output_aliases={t: t for t in range(n)},
        compiler_params=_COMM,
    )(*[pool[m] for m in names])
    pool.update(zip(names, res))


def gather_now(pool, units):
    comm_now(pool, [[("ici", name, layer) for name, layer in units],
                    [("fwd", name, layer) for name, layer in units]], "gather_now")


def allgather_chips(buf, name):
    def body(src_ref, out_ref, send_sems, recv_sems, local_sem):
        x, y, c = _position()
        me = 2 * x + y
        mine = pltpu.make_async_copy(src_ref, out_ref.at[me], local_sem)
        mine.start()
        sends = []
        for k, (px, py) in enumerate(_other_chips(x, y)):
            cp = _remote(src_ref, out_ref.at[me], send_sems.at[k], recv_sems.at[k], (px, py, c))
            cp.start()
            sends.append(cp)
        for k, (px, py) in enumerate(_other_chips(x, y)):
            _remote(src_ref, out_ref.at[2 * px + py], send_sems.at[k], recv_sems.at[k], (px, py, c)).wait_recv()
        for cp in sends:
            cp.wait_send()
        mine.wait()

    sem = pltpu.SemaphoreType.DMA
    return pl.pallas_call(
        body, name=name, in_specs=[_ANY], out_specs=_ANY, out_shape=SDS((N_CHIPS,) + buf.shape, buf.dtype),
        scratch_shapes=[sem((3,)), sem((3,)), sem], compiler_params=_COMM,
    )(buf)


def swap_sibling(tensors, name):
    n = len(tensors)

    def body(*refs):
        srcs, outs, send_sems, recv_sems = refs[:n], refs[n:2 * n], refs[2 * n], refs[2 * n + 1]
        x, y, c = _position()
        cps = [_remote(srcs[t].at[1 - c], outs[t], send_sems.at[t], recv_sems.at[t], (x, y, 1 - c))
               for t in range(n)]
        for cp in cps:
            cp.start()
        for cp in cps:
            cp.wait()

    sem = pltpu.SemaphoreType.DMA
    return pl.pallas_call(
        body, name=name, in_specs=[_ANY] * n, out_specs=[_ANY] * n,
        out_shape=[SDS(t.shape[1:], t.dtype) for t in tensors],
        scratch_shapes=[sem((n,)), sem((n,))], compiler_params=_COMM,
    )(*tensors)


def exchange_chips(tensors, name):
    n = len(tensors)

    def body(*refs):
        srcs, outs = refs[:n], refs[n:2 * n]
        send_sems, recv_sems, local_sems = refs[2 * n:]
        x, y, c = _position()
        me = 2 * x + y
        others = _other_chips(x, y)
        cps = []
        for t in range(n):
            cp = pltpu.make_async_copy(srcs[t].at[me], outs[t].at[me], local_sems.at[t])
            cp.start()
            cps.append(cp)
        sends = []
        for t in range(n):
            for k, (px, py) in enumerate(others):
                cp = _remote(srcs[t].at[2 * px + py], outs[t].at[me], send_sems.at[t, k], recv_sems.at[t, k],
                             (px, py, c))
                cp.start()
                sends.append(cp)
        for t in range(n):
            for k, (px, py) in enumerate(others):
                _remote(srcs[t].at[me], outs[t].at[2 * px + py], send_sems.at[t, k], recv_sems.at[t, k],
                        (px, py, c)).wait_recv()
        for cp in sends:
            cp.wait_send()
        for cp in cps:
            cp.wait()

    sem = pltpu.SemaphoreType.DMA
    return pl.pallas_call(
        body, name=name, in_specs=[_ANY] * n, out_specs=[_ANY] * n,
        out_shape=[SDS(t.shape, t.dtype) for t in tensors],
        scratch_shapes=[sem((n, 3)), sem((n, 3)), sem((n,))], compiler_params=_COMM,
    )(*tensors)


def join_halves(tensors, name):
    n = len(tensors)

    def body(*refs):
        outs, send_sems, recv_sems = refs[n:2 * n], refs[2 * n], refs[2 * n + 1]
        x, y, c = _position()
        sib = (x, y, 1 - c)
        sends = []
        for t in range(n):
            cp = _remote(outs[t].at[c], outs[t].at[c], send_sems.at[t], recv_sems.at[t], sib)
            cp.start()
            sends.append(cp)
        for t in range(n):
            _remote(outs[t].at[c], outs[t].at[1 - c], send_sems.at[t], recv_sems.at[t], sib).wait_recv()
        for cp in sends:
            cp.wait_send()

    sem = pltpu.SemaphoreType.DMA
    return pl.pallas_call(
        body, name=name, in_specs=[_ANY] * n, out_specs=[_ANY] * n,
        out_shape=[SDS(t.shape, t.dtype) for t in tensors],
        scratch_shapes=[sem((n,)), sem((n,))], input_output_aliases={t: t for t in range(n)},
        compiler_params=_COMM,
    )(*tensors)


BIG = ("w_in", "w_proj_attn", "w_proj_pool", "w_proj_conv", "conv_w", "w_out", "w_gate_up", "w_down")
REPLICATED = ("attn_norm", "b_forget", "b_gate", "pool_w", "pool_scale", "ffn_norm", "final_norm")
ORDER = ("attn_norm", "w_in", "b_forget", "b_gate", "w_proj_attn", "pool_w", "pool_scale", "w_proj_pool",
         "conv_w", "w_proj_conv", "w_out", "ffn_norm", "w_gate_up", "w_down", "final_norm")


def _proj_layout(D):
    lay = {"g": 0, "q": 3 * D}
    lay["k"] = lay["q"] + BRANCH_W
    lay["f"] = lay["k"] + BRANCH_W
    lay["v"] = lay["f"] + F_PAD
    lay["pc"] = lay["v"] + BRANCH_W
    lay["width"] = lay["pc"] + 4 * BRANCH_W
    return lay


_REF = dict(q=0, k=512, v=1024, f=1536, u=1544, cv=2056, cb=2568, cc=3080, g=3592)


def _packed_pieces(D):
    pieces = [(_REF["g"], 3 * D), (_REF["q"], BRANCH_W), (_REF["k"], BRANCH_W), (_REF["f"], HEADS),
              (None, F_PAD - HEADS), (_REF["v"], BRANCH_W)]
    for gi in range(N_GROUPS):
        pieces += [(_REF[name] + gi * GROUP_W, GROUP_W) for name in ("u", "cv", "cb", "cc")]
    return pieces


def _pack_w_in_rows(shards, layer):
    _, _, cs, D = shards.shape
    parts = []
    for start, n in _packed_pieces(D):
        if start is None:
            parts.append(jnp.zeros((n, D), shards.dtype))
        while start is not None and n:
            chip, off = divmod(start, cs)
            take = min(n, cs - off)
            parts.append(shards[chip, layer, off:off + take, :])
            start, n = start + take, n - take
    return jnp.concatenate(parts, axis=0)


def _unpack_w_in_rows(p, D):
    lay = _proj_layout(D)
    rows = lambda a, n: p[a:a + n, :]
    kinds = []
    for kind in range(4):
        kinds += [rows(lay["pc"] + gi * BRANCH_W + kind * GROUP_W, GROUP_W) for gi in range(N_GROUPS)]
    return jnp.concatenate([rows(lay["q"], BRANCH_W), rows(lay["k"], BRANCH_W), rows(lay["v"], BRANCH_W),
                            rows(lay["f"], HEADS)] + kinds + [rows(0, 3 * D)], axis=0)


def _split_flat(vec, shapes):
    out, at = [], 0
    for shp in shapes:
        n = int(np.prod(shp))
        out.append(vec[at:at + n].reshape(shp))
        at += n
    return out


def kernel(x, attn_norm, w_in, b_forget, b_gate, w_proj_attn, pool_w, pool_scale, w_proj_pool, conv_w, w_proj_conv, w_out, ffn_norm, w_gate_up, w_down, final_norm, loss_target, m_attn_norm, m_w_in, m_b_forget, m_b_gate, m_w_proj_attn, m_pool_w, m_pool_scale, m_w_proj_pool, m_conv_w, m_w_proj_conv, m_w_out, m_ffn_norm, m_w_gate_up, m_w_down, m_final_norm, v_attn_norm, v_w_in, v_b_forget, v_b_gate, v_w_proj_attn, v_pool_w, v_pool_scale, v_w_proj_pool, v_conv_w, v_w_proj_conv, v_w_out, v_ffn_norm, v_w_gate_up, v_w_down, v_final_norm):
    weights = dict(attn_norm=attn_norm, w_in=w_in, b_forget=b_forget, b_gate=b_gate, w_proj_attn=w_proj_attn,
                   pool_w=pool_w, pool_scale=pool_scale, w_proj_pool=w_proj_pool, conv_w=conv_w,
                   w_proj_conv=w_proj_conv, w_out=w_out, ffn_norm=ffn_norm, w_gate_up=w_gate_up, w_down=w_down,
                   final_norm=final_norm)
    mom_m = dict(attn_norm=m_attn_norm, w_in=m_w_in, b_forget=m_b_forget, b_gate=m_b_gate, w_proj_attn=m_w_proj_attn,
                 pool_w=m_pool_w, pool_scale=m_pool_scale, w_proj_pool=m_w_proj_pool, conv_w=m_conv_w,
                 w_proj_conv=m_w_proj_conv, w_out=m_w_out, ffn_norm=m_ffn_norm, w_gate_up=m_w_gate_up,
                 w_down=m_w_down, final_norm=m_final_norm)
    mom_v = dict(attn_norm=v_attn_norm, w_in=v_w_in, b_forget=v_b_forget, b_gate=v_b_gate, w_proj_attn=v_w_proj_attn,
                 pool_w=v_pool_w, pool_scale=v_pool_scale, w_proj_pool=v_w_proj_pool, conv_w=v_conv_w,
                 w_proj_conv=v_w_proj_conv, w_out=v_w_out, ffn_norm=v_ffn_norm, w_gate_up=v_w_gate_up,
                 w_down=v_w_down, final_norm=v_final_norm)

    Bl, S, D = x.shape
    T = Bl * S
    L = w_in.shape[0]
    F = w_down.shape[1] * N_CHIPS
    lay = _proj_layout(D)
    cst = _placement_constants()
    assert L == N_LAYERS and S % ATTN_BLOCK == 0 and F % (2 * LANES) == 0 and D % BRANCH_W == 0
    assert w_in.shape[2] * N_CHIPS == _REF["g"] + 3 * D and conv_w.shape[2] == GROUP_W

    send = {n: weights[n].astype(BF16) for n in BIG}
    send["conv_w"] = conv_w
    send["w_in"] = w_in.transpose(0, 2, 1).astype(BF16)
    pool = gather_buffers(send)
    gather_now(pool, [("w_in", 0)])
    rest = ("w_out", "w_proj_attn", "w_proj_pool", "w_gate_up", "w_proj_conv", "conv_w")
    late = ("w_out", "w_proj_attn", "w_proj_pool", "w_proj_conv", "conv_w")
    jobs = lambda kind, names, layer: [(kind, n, layer) for n in names]
    carried = {
        "in_proj": jobs("ici", rest, 0),
        "attn_prep": jobs("fwd", rest, 0) + jobs("ici", ("w_down",), 0),
        "attn_fwd": jobs("fwd", ("w_down",), 0) + jobs("ici", ("w_in", "w_gate_up"), 1),
        "poolconv_fwd": jobs("fwd", ("w_in", "w_gate_up"), 1),
        "mix_fwd": jobs("ici", ("w_down",), 1),
        "gate_up_proj": jobs("fwd", ("w_down",), 1) + jobs("ici", late, 1),
        "ffn_down_fwd": jobs("fwd", late, 1),
    }
    carry = lambda call, layer: Hosted(pool, carried[call]) if layer == 0 else None
    w_down_f = lambda: pool["w_down"].reshape(L, F, D)
    pool_w_b = pool_w.astype(BF16)
    an3, fn3 = attn_norm.reshape(L, 1, D), ffn_norm.reshape(L, 1, D)
    bg3, ps3 = b_gate.reshape(L, 1, 3 * D), pool_scale.reshape(L, 1, BRANCH_W)
    bf3 = jnp.pad(b_forget, ((0, 0), (0, LANES - HEADS))).reshape(L, 1, LANES)

    xs = x.reshape(T, D)
    saved = []
    w_in_p = []
    for l in range(L):
        w_in_p.append(_pack_w_in_rows(pool["w_in"], l))
        proj, h = norm_matmul(xs, an3, w_in_p[l], l, "rows", "in_proj", carry("in_proj", l))
        proj3 = proj.reshape(Bl, S, lay["width"])
        qa, ka = attn_prep(proj3, bf3, l, cst, lay, carry("attn_prep", l))
        ao, lse = attn_fwd(qa, ka, proj3, lay, carry("attn_fwd", l))
        po, co = poolconv_fwd(proj3, pool_w_b, ps3, pool["conv_w"], l, lay, carry("poolconv_fwd", l))
        ao2, po2, co2 = (a.reshape(T, BRANCH_W) for a in (ao, po, co))
        x1, ys, mixed = mix_fwd(ao2, po2, co2, proj, bg3, pool["w_proj_attn"], pool["w_proj_pool"],
                                pool["w_proj_conv"], pool["w_out"], l, xs, carry("mix_fwd", l))
        ab, h2 = norm_matmul(x1, fn3, pool["w_gate_up"], l, "by_shard", "gate_up_proj", carry("gate_up_proj", l))
        x2, s_act = ffn_down_fwd(ab, w_down_f(), l, x1, carry("ffn_down_fwd", l))
        saved.append(dict(x=xs, proj=proj, proj3=proj3, h=h, qa=qa, ka=ka, ao=ao, lse=lse, ao2=ao2, po2=po2,
                          co2=co2, ys=ys, mixed=mixed, x1=x1, ab=ab, h2=h2, s=s_act))
        xs = x2
    w_gu, w_o, conv_w_g = pool["w_gate_up"], pool["w_out"], pool["conv_w"]
    wpa, wpp, wpc = pool["w_proj_attn"], pool["w_proj_pool"], pool["w_proj_conv"]
    w_down_f = w_down_f()

    loss_row, dx, dxb, g_final = loss_head(xs, final_norm.reshape(1, D), loss_target.reshape(T, D))
    loss = lax.psum(loss_row[0, 0], AXES)

    reduced_names = tuple(n for n in BIG if n != "conv_w")
    me_chip = 2 * lax.axis_index("x") + lax.axis_index("y")
    where = jnp.stack([lax.axis_index("c"), me_chip]).astype(jnp.int32)
    rs = {}

    def reduce_begin(layer, grads):
        for n, g in grads.items():
            g5 = g.reshape((1, N_CHIPS, 2, -1) + g.shape[-1:])
            rs["g%d:%s" % (layer, n)] = g5
            for role in "ra":
                rs["%s%d:%s" % (role, layer, n)] = lax.empty((N_CHIPS,) + g5.shape[3:], BF16)

    swap_jobs = lambda layer, names: [("swap", "g%d:%s" % (layer, n), "r%d:%s" % (layer, n), 0) for n in names]
    xchg_jobs = lambda layer, names: [("xchg", "s%d:%s" % (layer, n), "a%d:%s" % (layer, n)) for n in names]
    join_jobs = lambda layer, names: [("join", "o:" + n, layer) for n in names]

    def pair_sums(layer, names):
        for n in names:
            rs["s%d:%s" % (layer, n)] = add_pair(rs["g%d:%s" % (layer, n)], 0, where, rs["r%d:%s" % (layer, n)],
                                                 "add_pair_" + n)

    def chip_sums(layer, names, slot, n_slots):
        for n in names:
            rs["o:" + n] = add_chips(rs["a%d:%s" % (layer, n)], rs["s%d:%s" % (layer, n)], slot, where, n_slots,
                                     rs.get("o:" + n), "add_chips_" + n)

    small = {n: [None] * L for n in REPLICATED if n != "final_norm"}
    g_conv = [None] * L
    to3 = lambda a: a.reshape(Bl, S, -1)
    for l in reversed(range(L)):
        sv = saved[l]
        behind = (lambda jobs: Hosted(rs, jobs)) if l == 0 else (lambda jobs: None)
        grads = {}
        da, db = ffn_down_bwd(dxb, w_down_f, l, sv["ab"], behind(swap_jobs(1, reduced_names)))
        if l == 0:
            pair_sums(1, reduced_names)
        grads["w_down"] = matmul_tn(sv["s"], [dxb], "grad_w_down", hosted=behind(xchg_jobs(
            1, ("w_down", "w_out", "w_proj_attn", "w_proj_pool", "w_proj_conv"))))
        grads["w_gate_up"] = matmul_tn(sv["h2"], [da, db], "grad_w_gate_up", by_dest=True, tn=2 * F // N_CHIPS,
                                       tk=_tile(T, (1024, 512, 256)), hosted=behind(xchg_jobs(1, ("w_gate_up",))))
        dx1, dx1b, g_fn = matmul_nt_normbwd([da, db], w_gu, l, "by_shard", sv["x1"], fn3, dx, "gate_up_bwd",
                                            behind(xchg_jobs(1, ("w_in",))))
        small["ffn_norm"][l] = g_fn[0]
        if l == 0:
            chip_sums(1, reduced_names, 1, L)
        dys, dproj, dao, dpo, dco, g_bg = mix_bwd(dx1b, w_o, sv["proj"], bg3, sv["ys"], wpa, wpp, wpc, l,
                                                  lay["width"], behind(join_jobs(1, reduced_names)))
        small["b_gate"][l] = g_bg[0]
        grads["w_out"] = matmul_tn(sv["mixed"], [dx1b], "grad_w_out")
        for n, (name, br) in enumerate((("w_proj_attn", sv["ao2"]), ("w_proj_pool", sv["po2"]),
                                        ("w_proj_conv", sv["co2"]))):
            grads[name] = matmul_tn(br, [dys], "grad_" + name, b_col0=n * D, n_cols=D, by_dest=True,
                                    tn=D // N_CHIPS)
        dqa, dka, dproj3 = attn_bwd(sv["qa"], sv["ka"], sv["proj3"], to3(dao), sv["ao"], sv["lse"], to3(dproj), lay)
        dproj3, g_bf = attn_post(dqa, dka, sv["proj3"], bf3, l, dproj3, cst, lay)
        small["b_forget"][l] = g_bf[0, :HEADS]
        dproj3, g_pw, g_ps, g_conv[l] = poolconv_bwd(sv["proj3"], to3(dpo), to3(dco), pool_w_b, ps3, conv_w_g, l,
                                                     dproj3, lay)
        small["pool_w"][l], small["pool_scale"][l] = g_pw, g_ps[0]
        dproj = dproj3.reshape(T, lay["width"])
        grads["w_in"] = _unpack_w_in_rows(matmul_tn(dproj, [sv["h"]], "grad_w_in"), D)
        dx, dxb, g_an = matmul_nt_normbwd([dproj], w_in_p[l], l, "rows", sv["x"], an3, dx1, "in_proj_bwd")
        small["attn_norm"][l] = g_an[0]
        reduce_begin(l, grads)
    grad_x = dx.reshape(Bl, S, D)

    small_shapes = [weights[n].shape for n in REPLICATED] + [(L, N_CHIPS) + conv_w.shape[1:]]
    small_vec = jnp.concatenate([jnp.stack(small[n]).reshape(-1) for n in REPLICATED[:-1]]
                                + [g_final[0], jnp.stack(g_conv).reshape(-1)])
    n_small = small_vec.shape[0]
    small_vec = jnp.pad(small_vec, (0, -n_small % (2 * N_CHIPS * 16 * LANES))).astype(BF16)
    rs["g0:small"] = small_vec.reshape(1, N_CHIPS, 2, -1, LANES)
    for role in "ra":
        rs[role + "0:small"] = lax.empty((N_CHIPS,) + rs["g0:small"].shape[3:], BF16)
    last = reduced_names + ("small",)
    comm_now(rs, [swap_jobs(0, last)], "swap_grad_halves")
    pair_sums(0, last)
    comm_now(rs, [xchg_jobs(0, last)], "exchange_grad_chips")
    chip_sums(0, reduced_names, 0, L)
    chip_sums(0, ("small",), 0, 1)
    comm_now(rs, [join_jobs(0, last)], "join_grad_halves")
    shard_grads = {n: rs["o:" + n].reshape((L, -1) + rs["o:" + n].shape[-1:]) for n in reduced_names}
    small_all = allgather_chips(rs["o:small"].reshape(-1, LANES), "allgather_small_grads").reshape(-1)[:n_small]
    *rep_list, conv_all = _split_flat(small_all, small_shapes)
    rep_grads = dict(zip(REPLICATED, rep_list))
    shard_grads["conv_w"] = lax.dynamic_index_in_dim(conv_all, me_chip, 1, keepdims=False)

    delta, new_m, new_v = {}, {}, {}
    for n in BIG:
        shp = weights[n].shape
        if n == "w_in":
            view, back = (lambda a: a.transpose(2, 0, 1)), (lambda a: a.transpose(1, 2, 0))
            g = shard_grads[n].transpose(1, 0, 2)
        else:
            view, back = (lambda a: a.reshape(-1, shp[-1])), (lambda a: a.reshape(shp))
            g = view(shard_grads[n])
        d, nm, nv = adamw(view(weights[n]), g, view(mom_m[n]), view(mom_v[n]), "adamw_" + n)
        delta[n], new_m[n], new_v[n], shard_grads[n] = back(d), back(nm), back(nv), back(g)

    def rows(d):
        vec = jnp.concatenate([d[n].reshape(-1) for n in REPLICATED])
        return jnp.pad(vec, (0, -vec.shape[0] % (8 * LANES))).reshape(-1, LANES)

    outs = adamw(rows(weights), rows(rep_grads), rows(mom_m), rows(mom_v), "adamw_replicated")
    for res, o in zip((delta, new_m, new_v), outs):
        res.update(zip(REPLICATED, _split_flat(o.reshape(-1), small_shapes[:len(REPLICATED)])))
    all_grads = {**shard_grads, **rep_grads}

    return (loss, grad_x, *[all_grads[n] for n in ORDER], *[delta[n] for n in ORDER],
            *[new_m[n] for n in ORDER], *[new_v[n] for n in ORDER])
```

```python
import numpy as np
import jax
import jax.numpy as jnp
from jax import lax
from jax.experimental import pallas as pl
from jax.experimental.pallas import tpu as pltpu

F32, BF16 = jnp.float32, jnp.bfloat16
SDS = jax.ShapeDtypeStruct
MESH = pl.DeviceIdType.MESH
AXES = ("x", "y", "c")
N_CHIPS = 4
N_LAYERS = 2
LANES = 128
VMEM_LIMIT = 48 * 1024 * 1024

HEADS, HEAD_DIM = 8, 64
HEAD_PAD = 128
BRANCH_W = 512
GROUP_W = 128
N_GROUPS = BRANCH_W // GROUP_W
POOL_WINDOWS = (2, 4, 8, 16)
F_PAD = 512
ATTN_BLOCK = 256
RMS_EPS = 1e-6
NEG_INF = -1e30
ADAM_LR, ADAM_B1, ADAM_B2, ADAM_EPS, ADAM_WD, ADAM_STEP = 0.001, 0.9, 0.999, 1e-08, 0.01, 10

NT = (((1,), (1,)), ((), ()))
TN = (((0,), (0,)), ((), ()))
_ANY = pl.BlockSpec(memory_space=pl.ANY)


def _tile(n, prefs):
    for p in prefs:
        if n % p == 0:
            return p
    raise ValueError(f"no tile of {prefs} divides {n}")


def _params(*sem):
    return pltpu.CompilerParams(dimension_semantics=sem, vmem_limit_bytes=VMEM_LIMIT)


def _sigmoid(z):
    return 0.5 * jnp.tanh(0.5 * z) + 0.5


def _split3(x):
    h1 = x.astype(BF16)
    r1 = x - h1.astype(F32)
    h2 = r1.astype(BF16)
    h3 = (r1 - h2.astype(F32)).astype(BF16)
    return h1, h2, h3


def _position():
    return lax.axis_index("x"), lax.axis_index("y"), lax.axis_index("c")


def _other_chips(x, y):
    return [(1 - x, y), (x, 1 - y), (1 - x, 1 - y)]


def _remote(src, dst, send_sem, recv_sem, device):
    return pltpu.make_async_remote_copy(src_ref=src, dst_ref=dst, send_sem=send_sem, recv_sem=recv_sem,
                                        device_id=device, device_id_type=MESH)


ROW_SHARDED = ("w_out", "w_down")
FETCHER = dict(w_in=0, w_out=0, w_proj_attn=0, w_proj_pool=0, w_gate_up=1, w_down=1, w_proj_conv=1, conv_w=1)


class Hosted:
    def __init__(self, pool, jobs):
        self.pool, self.jobs = pool, list(jobs)
        names = set()
        for job in self.jobs:
            names.update(job[1:3] if job[0] in ("swap", "xchg") else job[1:2])
        self.names = sorted(names)


def _hosted_plan(hosted, refs, send_sems, recv_sems):
    x, y, c = _position()
    me = 2 * x + y
    others = _other_chips(x, y)
    sibling = (x, y, 1 - c)
    plan = []
    for j, job in enumerate(hosted.jobs):
        kind = job[0]
        sems = lambda k, j=j: (send_sems.at[j, k], recv_sems.at[j, k])
        if kind in ("ici", "fwd"):
            _, name, layer = job
            ref = refs[name]
            win = (lambda chip, ref=ref, layer=layer: ref.at[layer, chip]) if name in ROW_SHARDED else (
                lambda chip, ref=ref, layer=layer: ref.at[chip, layer])
            mine = c == FETCHER[name]
            if kind == "ici":
                sends = [_remote(win(me), win(me), *sems(k), (px, py, c)) for k, (px, py) in enumerate(others)]
                arrivals = [_remote(win(2 * px + py), win(2 * px + py), *sems(k), (px, py, c))
                            for k, (px, py) in enumerate(others)]
                plan.append((mine, sends, arrivals, []))
            else:
                sends = [_remote(win(2 * px + py), win(2 * px + py), *sems(k), sibling)
                         for k, (px, py) in enumerate(others)]
                plan.append((mine, sends, [], sends))
        elif kind == "swap":
            _, src, dst, layer = job
            cp = _remote(refs[src].at[layer, :, 1 - c], refs[dst], *sems(0), sibling)
            plan.append((True, [cp], [cp], []))
        elif kind == "xchg":
            _, src, dst = job
            sends = [_remote(refs[src].at[2 * px + py], refs[dst].at[me], *sems(k), (px, py, c))
                     for k, (px, py) in enumerate(others)]
            arrivals = [_remote(refs[src].at[me], refs[dst].at[2 * px + py], *sems(k), (px, py, c))
                        for k, (px, py) in enumerate(others)]
            plan.append((True, sends, arrivals, []))
        else:
            _, name, layer = job
            ref = refs[name]
            cp = _remote(ref.at[layer, c], ref.at[layer, c], *sems(0), sibling)
            arrival = _remote(ref.at[layer, c], ref.at[layer, 1 - c], *sems(0), sibling)
            plan.append((True, [cp], [arrival], []))
    return plan


def _hosted_start(plan, now):
    for mine, sends, _, _ in plan:
        @pl.when(now & mine)
        def _(sends=sends):
            for cp in sends:
                cp.start()


def _hosted_finish(plan, now):
    for mine, sends, arrivals, sibling_arrivals in plan:
        @pl.when(now & mine)
        def _(sends=sends, arrivals=arrivals):
            for cp in arrivals:
                cp.wait_recv()
            for cp in sends:
                cp.wait_send()

        if sibling_arrivals:
            @pl.when(now & jnp.logical_not(mine))
            def _(sibling_arrivals=sibling_arrivals):
                for cp in sibling_arrivals:
                    cp.wait_recv()


def _pcall(body, hosted, *, name, grid, in_specs, out_specs, out_shape, semantics, scratch_shapes=(), aliases=None):
    aliases = dict(aliases or {})
    if hosted is None or not hosted.jobs:
        return pl.pallas_call(body, name=name, grid=grid, in_specs=in_specs, out_specs=out_specs,
                              out_shape=out_shape, scratch_shapes=list(scratch_shapes),
                              input_output_aliases=aliases, compiler_params=_params(*semantics))
    single = not isinstance(out_shape, (list, tuple))
    out_specs_l = [out_specs] if single else list(out_specs)
    out_shape_l = [out_shape] if single else list(out_shape)
    n_in, n_out, n_buf, n_job = len(in_specs), len(out_specs_l), len(hosted.names), len(hosted.jobs)

    def carrying(*refs):
        ins, outs = refs[:n_in], refs[n_in + n_buf:n_in + n_buf + n_out]
        bufs = refs[n_in + n_buf + n_out:n_in + 2 * n_buf + n_out]
        rest = refs[n_in + 2 * n_buf + n_out:]
        scratch, send_sems, recv_sems = rest[:-2], rest[-2], rest[-1]
        first, last = True, True
        for axis, size in enumerate(grid):
            first = first & (pl.program_id(axis) == 0)
            last = last & (pl.program_id(axis) == size - 1)
        plan = _hosted_plan(hosted, dict(zip(hosted.names, bufs)), send_sems, recv_sems)
        _hosted_start(plan, first)
        body(*ins, *outs, *scratch)
        _hosted_finish(plan, last)

    def run(*args):
        bufs = [hosted.pool[n] for n in hosted.names]
        sem = pltpu.SemaphoreType.DMA
        res = pl.pallas_call(
            carrying, name=name, grid=grid, in_specs=list(in_specs) + [_ANY] * n_buf,
            out_specs=out_specs_l + [_ANY] * n_buf,
            out_shape=out_shape_l + [SDS(b.shape, b.dtype) for b in bufs],
            scratch_shapes=list(scratch_shapes) + [sem((n_job, 3)), sem((n_job, 3))],
            input_output_aliases={**aliases, **{n_in + i: n_out + i for i in range(n_buf)}},
            compiler_params=pltpu.CompilerParams(dimension_semantics=semantics, vmem_limit_bytes=VMEM_LIMIT,
                                                 has_side_effects=True),
        )(*args, *bufs)
        hosted.pool.update(zip(hosted.names, res[n_out:]))
        return res[0] if single else res[:n_out]

    return run


def _dot(a, b):
    return jnp.dot(a, b, preferred_element_type=F32)


def _dot_nt(a, b):
    return lax.dot_general(a, b, NT, preferred_element_type=F32)


def _dot_tn(a, b):
    return lax.dot_general(a, b, TN, preferred_element_type=F32)


def norm_matmul(x, gain, w, layer, kind, name, hosted=None):
    T, D = x.shape
    if kind == "by_shard":
        tn = w.shape[3]
        N = N_CHIPS * tn
        w_spec = pl.BlockSpec((None, None, D, tn), lambda i, j: (j, layer, 0, 0))
        mm = _dot
    else:
        N = w.shape[0]
        tn = _tile(N, (1024, 512, 256, 128))
        w_spec = pl.BlockSpec((tn, D), lambda i, j: (j, 0))
        mm = _dot_nt
    tm = _tile(T, (1024, 512, 256, 128))

    def body(x_ref, g_ref, w_ref, y_ref, h_ref):
        @pl.when(pl.program_id(1) == 0)
        def _():
            xf = x_ref[...]
            r = lax.rsqrt(jnp.mean(xf * xf, axis=-1, keepdims=True) + RMS_EPS)
            h_ref[...] = ((xf * r) * g_ref[...]).astype(BF16)

        y_ref[...] = mm(h_ref[...], w_ref[...]).astype(BF16)

    return _pcall(
        body, hosted, name=name, grid=(T // tm, N // tn),
        in_specs=[pl.BlockSpec((tm, D), lambda i, j: (i, 0)),
                  pl.BlockSpec((None, 1, D), lambda i, j: (layer, 0, 0)),
                  w_spec],
        out_specs=[pl.BlockSpec((tm, tn), lambda i, j: (i, j)),
                   pl.BlockSpec((tm, D), lambda i, j: (i, 0))],
        out_shape=[SDS((T, N), BF16), SDS((T, D), BF16)],
        semantics=("arbitrary", "arbitrary"),
    )(x, gain, w)


def matmul_nt_normbwd(dys, w, layer, kind, x, gain, dres, name, hosted=None):
    T, D = x.shape
    width = dys[0].shape[1]
    if kind == "by_shard":
        tk = w.shape[3]
        w_spec = pl.BlockSpec((None, None, D, tk), lambda i, k: (k, layer, 0, 0))
        mm = _dot_nt
    else:
        tk = _tile(width, (1024, 512, 256, 128))
        w_spec = pl.BlockSpec((tk, D), lambda i, k: (k, 0))
        mm = _dot
    per = width // tk
    nk = per * len(dys)
    tm = _tile(T, (512, 256, 128))
    n_dy = len(dys)

    def dy_spec(p):
        return pl.BlockSpec((tm, tk), lambda i, k: (i, jnp.clip(k - p * per, 0, per - 1)))

    def body(*refs):
        dy_refs = refs[:n_dy]
        w_ref, x_ref, g_ref, dres_ref, dx_ref, dxb_ref, dg_ref, acc_ref = refs[n_dy:]
        i, k = pl.program_id(0), pl.program_id(1)

        @pl.when(k == 0)
        def _():
            acc_ref[...] = jnp.zeros_like(acc_ref)

        for p in range(n_dy):
            @pl.when((k >= p * per) & (k < (p + 1) * per))
            def _(p=p):
                acc_ref[...] += mm(dy_refs[p][...], w_ref[...])

        @pl.when(k == nk - 1)
        def _():
            xf = x_ref[...]
            r = lax.rsqrt(jnp.mean(xf * xf, axis=-1, keepdims=True) + RMS_EPS)
            xhat = xf * r
            dh = acc_ref[...]
            dhg = dh * g_ref[...]
            dx = dres_ref[...] + r * (dhg - xhat * jnp.mean(dhg * xhat, axis=-1, keepdims=True))
            dx_ref[...] = dx
            dxb_ref[...] = dx.astype(BF16)
            part = jnp.sum(dh * xhat, axis=0, keepdims=True)

            @pl.when(i == 0)
            def _():
                dg_ref[...] = part

            @pl.when(i > 0)
            def _():
                dg_ref[...] += part

    row = pl.BlockSpec((tm, D), lambda i, k: (i, 0))
    return _pcall(
        body, hosted, name=name, grid=(T // tm, nk),
        in_specs=[dy_spec(p) for p in range(n_dy)] + [
            w_spec, row, pl.BlockSpec((None, 1, D), lambda i, k: (layer, 0, 0)), row],
        out_specs=[row, row, pl.BlockSpec((1, D), lambda i, k: (0, 0))],
        out_shape=[SDS((T, D), F32), SDS((T, D), BF16), SDS((1, D), F32)],
        scratch_shapes=[pltpu.VMEM((tm, D), F32)],
        semantics=("arbitrary", "arbitrary"),
    )(*dys, w, x, gain, dres)


def matmul_tn(a, bs, name, b_col0=0, n_cols=None, by_dest=False, tn=None, tk=None, hosted=None):
    T, M = a.shape
    width = bs[0].shape[1]
    N = n_cols if n_cols else width * len(bs)
    tm = _tile(M, (1024, 512, 256, 128))
    tn = tn or _tile(N, (512, 256, 128))
    tk = tk or _tile(T, (4096, 2048, 1024, 512, 256))
    assert b_col0 % tn == 0 and width % tn == 0
    j0, per, nk, n_b = b_col0 // tn, width // tn, T // tk, len(bs)

    def b_spec(p):
        return pl.BlockSpec((tk, tn), lambda i, j, k: (k, jnp.clip(j0 + j - p * per, 0, per - 1)))

    def body(*refs):
        a_ref, b_refs = refs[0], refs[1:1 + n_b]
        o_ref, acc_ref = refs[-2], refs[-1]
        j, k = pl.program_id(1), pl.program_id(2)

        @pl.when(k == 0)
        def _():
            acc_ref[...] = jnp.zeros_like(acc_ref)

        for p in range(n_b):
            @pl.when((j0 + j >= p * per) & (j0 + j < (p + 1) * per))
            def _(p=p):
                acc_ref[...] += _dot_tn(a_ref[...], b_refs[p][...])

        @pl.when(k == nk - 1)
        def _():
            o_ref[...] = acc_ref[...].astype(BF16)

    if by_dest:
        cs = N // N_CHIPS
        npd = cs // tn
        out_shape = SDS((N_CHIPS, M, cs), BF16)
        out_spec = pl.BlockSpec((None, tm, tn), lambda i, j, k: (j // npd, i, j % npd))
    else:
        out_shape = SDS((M, N), BF16)
        out_spec = pl.BlockSpec((tm, tn), lambda i, j, k: (i, j))
    return _pcall(
        body, hosted, name=name, grid=(M // tm, N // tn, nk),
        in_specs=[pl.BlockSpec((tk, tm), lambda i, j, k: (k, i))] + [b_spec(p) for p in range(n_b)],
        out_specs=out_spec, out_shape=out_shape,
        scratch_shapes=[pltpu.VMEM((tm, tn), F32)],
        semantics=("arbitrary", "arbitrary", "arbitrary"),
    )(a, *bs)


def ffn_down_fwd(ab, w_down, layer, x1, hosted=None):
    T, D = x1.shape
    F = w_down.shape[1]
    tm = _tile(T, (512, 256, 128))
    tk = F // 2
    nk = F // tk

    def body(a_ref, b_ref, w_ref, x_ref, x2_ref, s_ref, acc_ref):
        k = pl.program_id(1)

        @pl.when(k == 0)
        def _():
            acc_ref[...] = x_ref[...]

        a = a_ref[...].astype(F32)
        s = (a * _sigmoid(a) * b_ref[...].astype(F32)).astype(BF16)
        s_ref[...] = s
        acc_ref[...] += _dot(s, w_ref[...])

        @pl.when(k == nk - 1)
        def _():
            x2_ref[...] = acc_ref[...]

    return _pcall(
        body, hosted, name="ffn_down_fwd", grid=(T // tm, nk),
        in_specs=[pl.BlockSpec((tm, tk), lambda i, k: (i, k)),
                  pl.BlockSpec((tm, tk), lambda i, k: (i, nk + k)),
                  pl.BlockSpec((None, tk, D), lambda i, k: (layer, k, 0)),
                  pl.BlockSpec((tm, D), lambda i, k: (i, 0))],
        out_specs=[pl.BlockSpec((tm, D), lambda i, k: (i, 0)),
                   pl.BlockSpec((tm, tk), lambda i, k: (i, k))],
        out_shape=[SDS((T, D), F32), SDS((T, F), BF16)],
        scratch_shapes=[pltpu.VMEM((tm, D), F32)],
        semantics=("arbitrary", "arbitrary"),
    )(ab, ab, w_down, x1)


def ffn_down_bwd(dx2b, w_down, layer, ab, hosted=None):
    T, D = dx2b.shape
    F = w_down.shape[1]
    tm = _tile(T, (512, 256, 128))
    tn = F // 2
    nj = F // tn

    def body(dx_ref, w_ref, a_ref, b_ref, da_ref, db_ref):
        ds = _dot_nt(dx_ref[...], w_ref[...])
        a = a_ref[...].astype(F32)
        sg = _sigmoid(a)
        da_ref[...] = (ds * b_ref[...].astype(F32) * (sg * (1.0 + a * (1.0 - sg)))).astype(BF16)
        db_ref[...] = (ds * (a * sg)).astype(BF16)

    blk = pl.BlockSpec((tm, tn), lambda i, j: (i, j))
    return _pcall(
        body, hosted, name="ffn_down_bwd", grid=(T // tm, nj),
        in_specs=[pl.BlockSpec((tm, D), lambda i, j: (i, 0)),
                  pl.BlockSpec((None, tn, D), lambda i, j: (layer, j, 0)),
                  blk, pl.BlockSpec((tm, tn), lambda i, j: (i, nj + j))],
        out_specs=[blk, blk],
        out_shape=[SDS((T, F), BF16), SDS((T, F), BF16)],
        semantics=("arbitrary", "arbitrary"),
    )(dx2b, w_down, ab, ab)


def _mix_specs(tm, D, layer):
    cs = D // N_CHIPS
    row = lambda w: pl.BlockSpec((tm, w), lambda i: (i, 0))
    wp = pl.BlockSpec((N_CHIPS, None, BRANCH_W, cs), lambda i: (0, layer, 0, 0))
    wo = pl.BlockSpec((None, N_CHIPS, cs, D), lambda i: (layer, 0, 0, 0))
    bg = pl.BlockSpec((None, 1, 3 * D), lambda i: (layer, 0, 0))
    return row, wp, wo, bg


def mix_fwd(ao, po, co, proj, b_gate, wpa, wpp, wpc, w_out, layer, x, hosted=None):
    T, D = x.shape
    cs = D // N_CHIPS
    tm = _tile(T, (256, 128))
    row, wp, wo, bg = _mix_specs(tm, D, layer)

    def body(ao_ref, po_ref, co_ref, g_ref, bg_ref, wpa_ref, wpp_ref, wpc_ref, wo_ref, x_ref,
             x1_ref, ys_ref, mixed_ref):
        mixed = jnp.zeros((tm, D), F32)
        for n, (br, wp_ref) in enumerate(((ao_ref, wpa_ref), (po_ref, wpp_ref), (co_ref, wpc_ref))):
            y = jnp.concatenate([_dot(br[...], wp_ref[j]) for j in range(N_CHIPS)], axis=1)
            cols = slice(n * D, (n + 1) * D)
            gate = _sigmoid(g_ref[:, cols].astype(F32) + bg_ref[:, cols])
            ys_ref[:, cols] = y.astype(BF16)
            mixed = mixed + gate * y
        mb = mixed.astype(BF16)
        mixed_ref[...] = mb
        acc = x_ref[...]
        for j in range(N_CHIPS):
            acc = acc + _dot(mb[:, j * cs:(j + 1) * cs], wo_ref[j])
        x1_ref[...] = acc

    return _pcall(
        body, hosted, name="mix_fwd", grid=(T // tm,),
        in_specs=[row(BRANCH_W), row(BRANCH_W), row(BRANCH_W), row(3 * D), bg, wp, wp, wp, wo, row(D)],
        out_specs=[row(D), row(3 * D), row(D)],
        out_shape=[SDS((T, D), F32), SDS((T, 3 * D), BF16), SDS((T, D), BF16)],
        semantics=("arbitrary",),
    )(ao, po, co, proj, b_gate, wpa, wpp, wpc, w_out, x)


def mix_bwd(dx1b, w_out, proj, b_gate, ys, wpa, wpp, wpc, layer, width, hosted=None):
    T, D = dx1b.shape
    cs = D // N_CHIPS
    tm = _tile(T, (256, 128))
    row, wp, wo, bg = _mix_specs(tm, D, layer)

    def body(dx_ref, wo_ref, g_ref, bg_ref, ys_ref, wpa_ref, wpp_ref, wpc_ref,
             dys_ref, dg_ref, dao_ref, dpo_ref, dco_ref, dbg_ref):
        i = pl.program_id(0)
        dx = dx_ref[...]
        dmixed = jnp.concatenate([_dot_nt(dx, wo_ref[j]) for j in range(N_CHIPS)], axis=1)
        for n, (wp_ref, dbr) in enumerate(((wpa_ref, dao_ref), (wpp_ref, dpo_ref), (wpc_ref, dco_ref))):
            cols = slice(n * D, (n + 1) * D)
            gate = _sigmoid(g_ref[:, cols].astype(F32) + bg_ref[:, cols])
            dy = (dmixed * gate).astype(BF16)
            dys_ref[:, cols] = dy
            dgp = dmixed * ys_ref[:, cols].astype(F32) * gate * (1.0 - gate)
            dg_ref[:, cols] = dgp.astype(BF16)
            part = jnp.sum(dgp, axis=0, keepdims=True)

            @pl.when(i == 0)
            def _():
                dbg_ref[:, cols] = part

            @pl.when(i > 0)
            def _():
                dbg_ref[:, cols] += part

            acc = jnp.zeros((tm, BRANCH_W), F32)
            for j in range(N_CHIPS):
                acc = acc + _dot_nt(dy[:, j * cs:(j + 1) * cs], wp_ref[j])
            dbr[...] = acc.astype(BF16)

    return _pcall(
        body, hosted, name="mix_bwd", grid=(T // tm,),
        in_specs=[row(D), wo, row(3 * D), bg, row(3 * D), wp, wp, wp],
        out_specs=[row(3 * D), row(3 * D), row(BRANCH_W), row(BRANCH_W), row(BRANCH_W),
                   pl.BlockSpec((1, 3 * D), lambda i: (0, 0))],
        out_shape=[SDS((T, 3 * D), BF16), SDS((T, width), BF16), SDS((T, BRANCH_W), BF16),
                   SDS((T, BRANCH_W), BF16), SDS((T, BRANCH_W), BF16), SDS((1, 3 * D), F32)],
        semantics=("arbitrary",),
    )(dx1b, w_out, proj, b_gate, ys, wpa, wpp, wpc)


def loss_head(x2, gain, target):
    T, D = x2.shape
    tm = _tile(T, (512, 256, 128))

    def body(x_ref, g_ref, t_ref, loss_ref, dx_ref, dxb_ref, dg_ref):
        i = pl.program_id(0)
        xf = x_ref[...]
        g = g_ref[...]
        r = lax.rsqrt(jnp.mean(xf * xf, axis=-1, keepdims=True) + RMS_EPS)
        xhat = xf * r
        diff = xhat * g - t_ref[...]
        part_loss = 0.5 * jnp.sum(jnp.mean(diff * diff, axis=-1, keepdims=True), axis=0, keepdims=True)
        dy = diff * (1.0 / D)
        dhg = dy * g
        dx = r * (dhg - xhat * jnp.mean(dhg * xhat, axis=-1, keepdims=True))
        dx_ref[...] = dx
        dxb_ref[...] = dx.astype(BF16)
        part_g = jnp.sum(dy * xhat, axis=0, keepdims=True)
        part_l = jnp.broadcast_to(part_loss, (1, LANES))

        @pl.when(i == 0)
        def _():
            dg_ref[...] = part_g
            loss_ref[...] = part_l

        @pl.when(i > 0)
        def _():
            dg_ref[...] += part_g
            loss_ref[...] += part_l

    row = pl.BlockSpec((tm, D), lambda i: (i, 0))
    return pl.pallas_call(
        body, name="loss_head", grid=(T // tm,),
        in_specs=[row, pl.BlockSpec((1, D), lambda i: (0, 0)), row],
        out_specs=[pl.BlockSpec((1, LANES), lambda i: (0, 0)), row, row, pl.BlockSpec((1, D), lambda i: (0, 0))],
        out_shape=[SDS((1, LANES), F32), SDS((T, D), F32), SDS((T, D), BF16), SDS((1, D), F32)],
        compiler_params=_params("arbitrary"),
    )(x2, gain, target)


def _placement_constants():
    w = HEADS * HEAD_PAD
    pq = np.zeros((BRANCH_W, w), np.float32)
    pk = np.zeros((BRANCH_W, w), np.float32)
    pfq = np.zeros((3, LANES, w), np.float32)
    pfk = np.zeros((3, LANES, w), np.float32)
    cq = np.zeros((1, w), np.float32)
    ck = np.zeros((1, w), np.float32)
    eq = np.zeros((w, LANES), np.float32)
    ek = np.zeros((w, LANES), np.float32)
    for h in range(HEADS):
        for d in range(HEAD_DIM):
            pq[h * HEAD_DIM + d, h * HEAD_PAD + d] = HEAD_DIM ** -0.5
            pk[h * HEAD_DIM + d, h * HEAD_PAD + d] = 1.0
        for i in range(3):
            pfq[i, h, h * HEAD_PAD + HEAD_DIM + i] = 1.0
            pfk[i, h, h * HEAD_PAD + HEAD_DIM + 3 + i] = -1.0
            cq[0, h * HEAD_PAD + HEAD_DIM + 3 + i] = 1.0
            ck[0, h * HEAD_PAD + HEAD_DIM + i] = 1.0
        eq[h * HEAD_PAD + HEAD_DIM, h] = 1.0
        ek[h * HEAD_PAD + HEAD_DIM + 3, h] = -1.0
    bf = lambda a: jnp.asarray(a, BF16)
    return dict(pq=bf(pq), pk=bf(pk), pfq=bf(pfq), pfk=bf(pfk), cq=jnp.asarray(cq), ck=jnp.asarray(ck),
                pqkt=bf(np.concatenate([pq.T, pk.T], axis=0)), eq=bf(eq), ek=bf(ek))


def attn_prep(proj3, bf_rows, layer, cst, lay, hosted=None):
    Bl, S, _ = proj3.shape
    ts = ATTN_BLOCK
    w = HEADS * HEAD_PAD

    def body(q_ref, k_ref, f_ref, bf_ref, pq_ref, pk_ref, pfq_ref, pfk_ref, cq_ref, ck_ref,
             qa_ref, ka_ref, carry_ref):
        @pl.when(pl.program_id(1) == 0)
        def _():
            carry_ref[...] = jnp.zeros_like(carry_ref)

        z = f_ref[...].astype(F32) + bf_ref[...]
        logf = jnp.minimum(z, 0.0) - jnp.log(1.0 + jnp.exp(-jnp.abs(z)))
        r = lax.broadcasted_iota(jnp.int32, (ts, ts), 0)
        c = lax.broadcasted_iota(jnp.int32, (ts, ts), 1)
        tri = jnp.where(r >= c, 1.0, 0.0).astype(BF16)
        fcum = carry_ref[...]
        for part in _split3(logf):
            fcum = fcum + _dot(tri, part)
        carry_ref[...] = fcum[ts - 1:ts, :]
        qa = _dot(q_ref[...], pq_ref[...]) + cq_ref[...]
        ka = _dot(k_ref[...], pk_ref[...]) + ck_ref[...]
        for i, part in enumerate(_split3(fcum)):
            qa = qa + _dot(part, pfq_ref[i])
            ka = ka + _dot(part, pfk_ref[i])
        qa_ref[...] = qa.astype(BF16)
        ka_ref[...] = ka.astype(BF16)

    cfull = lambda shape: pl.BlockSpec(shape, lambda b, s: (0,) * len(shape))
    return _pcall(
        body, hosted, name="attn_prep", grid=(Bl, S // ts),
        in_specs=[pl.BlockSpec((None, ts, BRANCH_W), lambda b, s: (b, s, lay["q"] // BRANCH_W)),
                  pl.BlockSpec((None, ts, BRANCH_W), lambda b, s: (b, s, lay["k"] // BRANCH_W)),
                  pl.BlockSpec((None, ts, LANES), lambda b, s: (b, s, lay["f"] // LANES)),
                  pl.BlockSpec((None, 1, LANES), lambda b, s: (layer, 0, 0)),
                  cfull((BRANCH_W, w)), cfull((BRANCH_W, w)),
                  cfull((3, LANES, w)), cfull((3, LANES, w)), cfull((1, w)), cfull((1, w))],
        out_specs=[pl.BlockSpec((None, ts, w), lambda b, s: (b, s, 0)),
                   pl.BlockSpec((None, ts, w), lambda b, s: (b, s, 0))],
        out_shape=[SDS((Bl, S, w), BF16), SDS((Bl, S, w), BF16)],
        scratch_shapes=[pltpu.VMEM((1, LANES), F32)],
        semantics=("arbitrary", "arbitrary"),
    )(proj3, proj3, proj3, bf_rows, cst["pq"], cst["pk"], cst["pfq"], cst["pfk"], cst["cq"], cst["ck"])


def attn_fwd(qa, ka, proj3, lay, hosted=None):
    Bl, S, _ = qa.shape
    tq = ATTN_BLOCK
    nq = S // tq
    pairs = HEADS // 2
    pw = 2 * HEAD_PAD
    vw = 2 * HEAD_DIM

    def body(qa_ref, ka_ref, v_ref, o_ref, lse_ref):
        row = lax.broadcasted_iota(jnp.int32, (tq, tq), 0)
        col = lax.broadcasted_iota(jnp.int32, (tq, tq), 1)
        causal = row <= col
        for i in range(nq):
            nk = (i + 1) * tq
            rows = slice(i * tq, nk)
            o_t = []
            for h in range(2):
                hs = slice(h * HEAD_PAD, (h + 1) * HEAD_PAD)
                st = _dot_nt(ka_ref[0:nk, hs], qa_ref[rows, hs])
                diag = jnp.where(causal, st[nk - tq:], NEG_INF)
                m = jnp.max(diag, axis=0, keepdims=True)
                if i:
                    m = jnp.maximum(m, jnp.max(st[:nk - tq], axis=0, keepdims=True))
                p_diag = jnp.exp(diag - m)
                l = jnp.sum(p_diag, axis=0, keepdims=True)
                if i:
                    p_top = jnp.exp(st[:nk - tq] - m)
                    l = l + jnp.sum(p_top, axis=0, keepdims=True)
                    p = jnp.concatenate([p_top.astype(BF16), p_diag.astype(BF16)], axis=0)
                else:
                    p = p_diag.astype(BF16)
                acc = _dot_tn(v_ref[0:nk, :], p)
                o_t.append(acc[h * HEAD_DIM:(h + 1) * HEAD_DIM, :] / l)
                lse_ref[h:h + 1, rows] = m + jnp.log(l)
            o_ref[rows, :] = jnp.concatenate(o_t, axis=0).T.astype(BF16)

    return _pcall(
        body, hosted, name="attn_fwd", grid=(Bl, pairs),
        in_specs=[pl.BlockSpec((None, S, pw), lambda b, p: (b, 0, p)),
                  pl.BlockSpec((None, S, pw), lambda b, p: (b, 0, p)),
                  pl.BlockSpec((None, S, vw), lambda b, p: (b, 0, lay["v"] // vw + p))],
        out_specs=[pl.BlockSpec((None, S, vw), lambda b, p: (b, 0, p)),
                   pl.BlockSpec((None, None, 2, S), lambda b, p: (b, p, 0, 0))],
        out_shape=[SDS((Bl, S, BRANCH_W), BF16), SDS((Bl, pairs, 2, S), F32)],
        semantics=("arbitrary", "arbitrary"),
    )(qa, ka, proj3)


def attn_bwd(qa, ka, proj3, dao, ao, lse, dproj3, lay, hosted=None):
    Bl, S, _ = qa.shape
    tk = ATTN_BLOCK
    nq = S // tk
    pairs = HEADS // 2
    pw = 2 * HEAD_PAD
    vw = 2 * HEAD_DIM

    def body(qa_ref, ka_ref, v_ref, do_ref, o_ref, lse_ref, _, dqa_ref, dka_ref, dv_ref):
        row = lax.broadcasted_iota(jnp.int32, (tk, tk), 0)
        col = lax.broadcasted_iota(jnp.int32, (tk, tk), 1)
        causal = row <= col
        lane8 = lax.broadcasted_iota(jnp.int32, (8, vw), 1)
        lane_s = lax.broadcasted_iota(jnp.int32, (S, vw), 1)
        lane_k = lax.broadcasted_iota(jnp.int32, (tk, vw), 1)
        doo = do_ref[...].astype(F32) * o_ref[...].astype(F32)
        hi = doo.astype(BF16)
        lo = (doo - hi.astype(F32)).astype(BF16)
        delta, v_head = [], []
        for h in range(2):
            sel = jnp.where((lane8 >= h * HEAD_DIM) & (lane8 < (h + 1) * HEAD_DIM), 1.0, 0.0).astype(BF16)
            delta.append((_dot_nt(sel, hi) + _dot_nt(sel, lo))[0:1, :])
            in_head = (lane_s >= h * HEAD_DIM) & (lane_s < (h + 1) * HEAD_DIM)
            v_head.append(jnp.where(in_head, v_ref[...], jnp.zeros_like(v_ref[...])))
        dqa_ref[...] = jnp.zeros_like(dqa_ref)
        for j in range(nq):
            q0 = j * tk
            krows = slice(q0, q0 + tk)
            do = do_ref[q0:, :]
            dvs = []
            for h in range(2):
                hs = slice(h * HEAD_PAD, (h + 1) * HEAD_PAD)
                k = ka_ref[krows, hs]
                q = qa_ref[q0:, hs]
                st = _dot_nt(k, q)
                p = jnp.exp(st - lse_ref[h:h + 1, q0:])
                p_diag = jnp.where(causal, p[:, :tk], 0.0)
                p = jnp.concatenate([p_diag, p[:, tk:]], axis=1) if j < nq - 1 else p_diag
                dvs.append(_dot(p.astype(BF16), do))
                dpt = _dot_nt(v_head[h][krows, :], do)
                ds = (p * (dpt - delta[h][:, q0:])).astype(BF16)
                dka_ref[krows, hs] = _dot(ds, q)
                dqa_ref[q0:, hs] += _dot_tn(ds, k)
            dv_ref[krows, :] = jnp.where(lane_k < HEAD_DIM, dvs[0], dvs[1]).astype(BF16)

    seq = lambda w, c0=0: pl.BlockSpec((None, S, w), lambda b, p: (b, 0, c0 + p))
    return _pcall(
        body, hosted, name="attn_bwd", grid=(Bl, pairs),
        in_specs=[seq(pw), seq(pw), seq(vw, lay["v"] // vw), seq(vw), seq(vw),
                  pl.BlockSpec((None, None, 2, S), lambda b, p: (b, p, 0, 0)), _ANY],
        out_specs=[seq(pw), seq(pw), seq(vw, lay["v"] // vw)],
        out_shape=[SDS((Bl, S, HEADS * HEAD_PAD), F32), SDS((Bl, S, HEADS * HEAD_PAD), F32),
                   SDS(dproj3.shape, BF16)],
        aliases={6: 2}, semantics=("arbitrary", "arbitrary"),
    )(qa, ka, proj3, dao, ao, lse, dproj3)


def attn_post(dqa, dka, proj3, bf_rows, layer, dproj3, cst, lay):
    Bl, S, w = dqa.shape
    ts = ATTN_BLOCK
    ns = S // ts
    qkf = 2 * BRANCH_W + F_PAD

    def body(dqa_ref, dka_ref, f_ref, bf_ref, pqkt_ref, eq_ref, ek_ref, _, dqkf_ref, dbf_ref, carry_ref):
        b, s = pl.program_id(0), pl.program_id(1)

        @pl.when(s == 0)
        def _():
            carry_ref[...] = jnp.zeros_like(carry_ref)

        dqa_v, dka_v = dqa_ref[...], dka_ref[...]
        qh = dqa_v.astype(BF16)
        kh = dka_v.astype(BF16)
        dqkf_ref[:, :BRANCH_W] = _dot(qh, pqkt_ref[:w, :]).astype(BF16)
        dqkf_ref[:, BRANCH_W:2 * BRANCH_W] = _dot(kh, pqkt_ref[w:, :]).astype(BF16)
        ql = (dqa_v - qh.astype(F32)).astype(BF16)
        kl = (dka_v - kh.astype(F32)).astype(BF16)
        d_f = (_dot(qh, eq_ref[...]) + _dot(ql, eq_ref[...])) + (_dot(kh, ek_ref[...]) + _dot(kl, ek_ref[...]))
        r = lax.broadcasted_iota(jnp.int32, (ts, ts), 0)
        c = lax.broadcasted_iota(jnp.int32, (ts, ts), 1)
        triu = jnp.where(c >= r, 1.0, 0.0).astype(BF16)
        rev = carry_ref[...]
        for part in _split3(d_f):
            rev = rev + _dot(triu, part)
        carry_ref[...] = rev[0:1, :]
        z = f_ref[...].astype(F32) + bf_ref[...]
        lane = lax.broadcasted_iota(jnp.int32, (ts, LANES), 1)
        dfl = jnp.where(lane < HEADS, rev / (1.0 + jnp.exp(z)), 0.0)
        dqkf_ref[:, 2 * BRANCH_W:] = jnp.concatenate(
            [dfl.astype(BF16), jnp.zeros((ts, F_PAD - LANES), BF16)], axis=1)
        part = jnp.sum(dfl, axis=0, keepdims=True)

        @pl.when((b == 0) & (s == 0))
        def _():
            dbf_ref[...] = part

        @pl.when((b > 0) | (s > 0))
        def _():
            dbf_ref[...] += part

    assert lay["q"] % qkf == 0
    cfull = lambda shape: pl.BlockSpec(shape, lambda b, s: (0,) * len(shape))
    rev_blk = lambda wd, c0=0: pl.BlockSpec((None, ts, wd), lambda b, s: (b, ns - 1 - s, c0))
    return pl.pallas_call(
        body, name="attn_post", grid=(Bl, ns),
        in_specs=[rev_blk(w), rev_blk(w), rev_blk(LANES, lay["f"] // LANES),
                  pl.BlockSpec((None, 1, LANES), lambda b, s: (layer, 0, 0)),
                  cfull((2 * w, BRANCH_W)), cfull((w, LANES)), cfull((w, LANES)), _ANY],
        out_specs=[rev_blk(qkf, lay["q"] // qkf), cfull((1, LANES))],
        out_shape=[SDS(dproj3.shape, BF16), SDS((1, LANES), F32)],
        scratch_shapes=[pltpu.VMEM((1, LANES), F32)],
        input_output_aliases={7: 0},
        compiler_params=_params("arbitrary", "arbitrary"),
    )(dqa, dka, proj3, bf_rows, cst["pqkt"], cst["eq"], cst["ek"], dproj3)


def _shift_down(x, k, row):
    return jnp.where(row >= k, pltpu.roll(x, k, axis=0), 0.0)


def _shift_up(x, k, row):
    n = x.shape[0]
    return jnp.where(row < n - k, pltpu.roll(x, n - k, axis=0), 0.0)


def _window_sum(x, g, row, shift):
    s2 = x + shift(x, 1, row)
    s4 = s2 + shift(s2, 2, row)
    s8 = s4 + shift(s4, 4, row)
    s16 = s8 + shift(s8, 8, row)
    return jnp.where(g == 0, s2, jnp.where(g == 1, s4, jnp.where(g == 2, s8, s16)))


def _window_count(g, row):
    wnd = jnp.where(g == 0, 2, jnp.where(g == 1, 4, jnp.where(g == 2, 8, 16)))
    return jnp.minimum(row + 1, wnd).astype(F32)


def _group_columns(ref):
    return [ref[:, n * GROUP_W:(n + 1) * GROUP_W].astype(F32) for n in range(4)]


def poolconv_fwd(proj3, pool_w, pool_scale, conv_w, layer, lay, hosted=None):
    Bl, S, _ = proj3.shape

    def body(x_ref, pw_ref, ps_ref, cw_ref, po_ref, co_ref):
        g = pl.program_id(1)
        row = lax.broadcasted_iota(jnp.int32, (S, GROUP_W), 0)
        u, cv, cb, cc = _group_columns(x_ref)
        d = _window_sum(u, g, row, _shift_down) / _window_count(g, row) - u
        po_ref[...] = (_dot(d.astype(BF16), pw_ref[...]) * ps_ref[...]).astype(BF16)
        z = cc * cv
        y = cw_ref[0:1, :] * _shift_down(z, 2, row) + cw_ref[1:2, :] * _shift_down(z, 1, row) + cw_ref[2:3, :] * z
        co_ref[...] = (cb * y).astype(BF16)

    out = pl.BlockSpec((None, S, GROUP_W), lambda b, g: (b, 0, g))
    return _pcall(
        body, hosted, name="poolconv_fwd", grid=(Bl, N_GROUPS),
        in_specs=[pl.BlockSpec((None, S, BRANCH_W), lambda b, g: (b, 0, lay["pc"] // BRANCH_W + g)),
                  pl.BlockSpec((None, None, GROUP_W, GROUP_W), lambda b, g: (layer, g, 0, 0)),
                  pl.BlockSpec((None, 1, GROUP_W), lambda b, g: (layer, 0, g)),
                  pl.BlockSpec((None, None, 3, GROUP_W), lambda b, g: (g, layer, 0, 0))],
        out_specs=[out, out],
        out_shape=[SDS((Bl, S, BRANCH_W), BF16), SDS((Bl, S, BRANCH_W), BF16)],
        semantics=("arbitrary", "arbitrary"),
    )(proj3, pool_w, pool_scale, conv_w)


def poolconv_bwd(proj3, dpo, dco, pool_w, pool_scale, conv_w, layer, dproj3, lay):
    Bl, S, _ = proj3.shape

    def body(x_ref, dpo_ref, dco_ref, pw_ref, ps_ref, cw_ref, _, dx_ref, dpw_ref, dps_ref, dcw_ref):
        g, b = pl.program_id(0), pl.program_id(1)
        row = lax.broadcasted_iota(jnp.int32, (S, GROUP_W), 0)
        cnt = _window_count(g, row)
        u, cv, cb, cc = _group_columns(x_ref)
        d = (_window_sum(u, g, row, _shift_down) / cnt - u).astype(BF16)
        pw = pw_ref[...]
        ypre = _dot(d, pw)
        dpo_v = dpo_ref[...].astype(F32)
        dps = jnp.sum(dpo_v * ypre, axis=0, keepdims=True)
        dyp = (dpo_v * ps_ref[...]).astype(BF16)
        dpw = _dot_tn(d, dyp)
        dd = _dot_nt(dyp, pw)
        dx_ref[:, 0:GROUP_W] = (_window_sum(dd / cnt, g, row, _shift_up) - dd).astype(BF16)

        z = cc * cv
        z1, z2 = _shift_down(z, 1, row), _shift_down(z, 2, row)
        w0, w1, w2 = cw_ref[0:1, :], cw_ref[1:2, :], cw_ref[2:3, :]
        y = w0 * z2 + w1 * z1 + w2 * z
        dco_v = dco_ref[...].astype(F32)
        dy = dco_v * cb
        dz = w0 * _shift_up(dy, 2, row) + w1 * _shift_up(dy, 1, row) + w2 * dy
        dx_ref[:, GROUP_W:2 * GROUP_W] = (dz * cc).astype(BF16)
        dx_ref[:, 2 * GROUP_W:3 * GROUP_W] = (dco_v * y).astype(BF16)
        dx_ref[:, 3 * GROUP_W:] = (dz * cv).astype(BF16)
        dcw = jnp.concatenate([jnp.sum(dy * z2, axis=0, keepdims=True),
                               jnp.sum(dy * z1, axis=0, keepdims=True),
                               jnp.sum(dy * z, axis=0, keepdims=True)], axis=0)

        @pl.when(b == 0)
        def _():
            dpw_ref[...] = dpw
            dps_ref[...] = dps
            dcw_ref[...] = dcw

        @pl.when(b > 0)
        def _():
            dpw_ref[...] += dpw
            dps_ref[...] += dps
            dcw_ref[...] += dcw

    blk = pl.BlockSpec((None, S, GROUP_W), lambda g, b: (b, 0, g))
    pc = pl.BlockSpec((None, S, BRANCH_W), lambda g, b: (b, 0, lay["pc"] // BRANCH_W + g))
    return pl.pallas_call(
        body, name="poolconv_bwd", grid=(N_GROUPS, Bl),
        in_specs=[pc, blk, blk,
                  pl.BlockSpec((None, None, GROUP_W, GROUP_W), lambda g, b: (layer, g, 0, 0)),
                  pl.BlockSpec((None, 1, GROUP_W), lambda g, b: (layer, 0, g)),
                  pl.BlockSpec((None, None, 3, GROUP_W), lambda g, b: (g, layer, 0, 0)), _ANY],
        out_specs=[pc, pl.BlockSpec((None, GROUP_W, GROUP_W), lambda g, b: (g, 0, 0)),
                   pl.BlockSpec((1, GROUP_W), lambda g, b: (0, g)),
                   pl.BlockSpec((None, 3, GROUP_W), lambda g, b: (g, 0, 0))],
        out_shape=[SDS(dproj3.shape, BF16), SDS((N_GROUPS, GROUP_W, GROUP_W), F32), SDS((1, BRANCH_W), F32),
                   SDS((N_GROUPS, 3, GROUP_W), F32)],
        input_output_aliases={6: 0},
        compiler_params=_params("arbitrary", "arbitrary"),
    )(proj3, dpo, dco, pool_w, pool_scale, conv_w, dproj3)


def _tile_2d(rows, cols, n_arrays):
    budget = VMEM_LIMIT // 2
    lanes = -(-cols // LANES) * LANES
    if rows % 8 == 0:
        for t in (2048, 1024, 512, 256, 128, 64, 32, 16, 8):
            if rows % t == 0 and 2 * n_arrays * t * lanes * 4 <= budget:
                return t, cols
    for t in (1024, 512, 256, 128):
        if cols % t == 0 and 2 * n_arrays * (rows + 8) * t * 4 <= budget:
            return rows, t
    return rows, cols


def add_pair(kept, layer, where, received, name):
    _, n, _, R, C = kept.shape
    tr, tc = _tile_2d(R, C, 3)

    def body(where_ref, a_ref, b_ref, o_ref):
        o_ref[...] = (a_ref[...].astype(F32) + b_ref[...].astype(F32)).astype(BF16)

    blk = pl.BlockSpec((None, tr, tc), lambda d, i, j, where_ref: (d, i, j))
    grid_spec = pltpu.PrefetchScalarGridSpec(
        num_scalar_prefetch=1, grid=(n, R // tr, C // tc),
        in_specs=[pl.BlockSpec((None, None, None, tr, tc),
                               lambda d, i, j, where_ref: (layer, d, where_ref[0], i, j)), blk],
        out_specs=blk)
    return pl.pallas_call(body, name=name, grid_spec=grid_spec, out_shape=SDS((n, R, C), BF16),
                          compiler_params=_params("arbitrary", "arbitrary", "arbitrary"))(where, kept, received)


def add_chips(arrived, own, layer, where, n_layers, prev, name):
    _, R, C = arrived.shape
    tr, tc = _tile_2d(R, C, 6)

    def body(where_ref, a0, a1, a2, a3, own_ref, *rest):
        o_ref = rest[-1]
        chip = where_ref[1]
        acc = None
        for j, a_ref in enumerate((a0, a1, a2, a3)):
            term = jnp.where(chip == j, own_ref[...], a_ref[...]).astype(F32)
            acc = term if acc is None else acc + term
        o_ref[...] = acc

    def slot(j):
        return pl.BlockSpec((None, tr, tc), lambda i, k, where_ref, j=j: (
            jnp.where(where_ref[1] == j, (j + 1) % N_CHIPS, j), i, k))

    in_specs = [slot(j) for j in range(N_CHIPS)] + [
        pl.BlockSpec((None, tr, tc), lambda i, k, where_ref: (where_ref[1], i, k))]
    args = [where, arrived, arrived, arrived, arrived, own]
    aliases = {}
    if prev is not None:
        in_specs.append(_ANY)
        args.append(prev)
        aliases = {len(args) - 1: 0}
    grid_spec = pltpu.PrefetchScalarGridSpec(
        num_scalar_prefetch=1, grid=(R // tr, C // tc), in_specs=in_specs,
        out_specs=pl.BlockSpec((None, None, tr, tc), lambda i, k, where_ref: (layer, where_ref[0], i, k)))
    return pl.pallas_call(body, name=name, grid_spec=grid_spec, out_shape=SDS((n_layers, 2, R, C), F32),
                          input_output_aliases=aliases,
                          compiler_params=_params("arbitrary", "arbitrary"))(*args)


def adamw(w, g, m, v, name):
    if w.ndim == 2:
        R, C = w.shape
        tr, _ = _tile_2d(R, C, 7)
        grid, blk = (R // tr,), pl.BlockSpec((tr, C), lambda i: (i, 0))
    else:
        N, r, C = w.shape
        tn = max(t for t in range(1, N + 1) if N % t == 0 and t * r * C * 4 <= 512 * 1024)
        grid, blk = (N // tn,), pl.BlockSpec((tn, r, C), lambda i: (i, 0, 0))

    def body(w_ref, g_ref, m_ref, v_ref, d_ref, nm_ref, nv_ref):
        gv = g_ref[...]
        m_new = ADAM_B1 * m_ref[...] + (1.0 - ADAM_B1) * gv
        v_new = ADAM_B2 * v_ref[...] + (1.0 - ADAM_B2) * (gv * gv)
        m_hat = m_new / (1.0 - ADAM_B1 ** ADAM_STEP)
        v_hat = v_new / (1.0 - ADAM_B2 ** ADAM_STEP)
        d_ref[...] = -ADAM_LR * (m_hat / (jnp.sqrt(v_hat) + ADAM_EPS) + ADAM_WD * w_ref[...])
        nm_ref[...] = m_new
        nv_ref[...] = v_new

    out = SDS(w.shape, F32)
    return pl.pallas_call(body, name=name, grid=grid, in_specs=[blk] * 4, out_specs=[blk] * 3,
                          out_shape=[out, out, out], compiler_params=_params("arbitrary"))(w, g, m, v)


_COMM = pltpu.CompilerParams(has_side_effects=True)


def gather_buffers(shards):
    me_chip = 2 * lax.axis_index("x") + lax.axis_index("y")
    pool = {}
    for name, sh in shards.items():
        L, r, c = sh.shape
        if name in ROW_SHARDED:
            pool[name] = lax.dynamic_update_slice(lax.empty((L, N_CHIPS, r, c), sh.dtype), sh[:, None],
                                                  (0, me_chip, 0, 0))
        else:
            pool[name] = lax.dynamic_update_slice(lax.empty((N_CHIPS, L, r, c), sh.dtype), sh[None],
                                                  (me_chip, 0, 0, 0))
    return pool


def comm_now(pool, stages, name):
    stages = [Hosted(pool, jobs) for jobs in stages]
    names = sorted({m for st in stages for m in st.names})
    n = len(names)

    def body(*refs):
        bufs = dict(zip(names, refs[n:2 * n]))
        sems = refs[2 * n:]
        for i, st in enumerate(stages):
            plan = _hosted_plan(st, bufs, sems[2 * i], sems[2 * i + 1])
            _hosted_start(plan, True)
            _hosted_finish(plan, True)

    sem = pltpu.SemaphoreType.DMA
    scratch = []
    for st in stages:
        scratch += [sem((len(st.jobs), 3)), sem((len(st.jobs), 3))]
    res = pl.pallas_call(
        body, name=name, in_specs=[_ANY] * n, out_specs=[_ANY] * n,
        out_shape=[SDS(pool[m].shape, pool[m].dtype) for m in names],
        scratch_shapes=scratch, input_output_aliases={t: t for t in range(n)},
        compiler_params=_COMM,
    )(*[pool[m] for m in names])
    pool.update(zip(names, res))


def gather_now(pool, units):
    comm_now(pool, [[("ici", name, layer) for name, layer in units],
                    [("fwd", name, layer) for name, layer in units]], "gather_now")


def allgather_chips(buf, name):
    def body(src_ref, out_ref, send_sems, recv_sems, local_sem):
        x, y, c = _position()
        me = 2 * x + y
        mine = pltpu.make_async_copy(src_ref, out_ref.at[me], local_sem)
        mine.start()
        sends = []
        for k, (px, py) in enumerate(_other_chips(x, y)):
            cp = _remote(src_ref, out_ref.at[me], send_sems.at[k], recv_sems.at[k], (px, py, c))
            cp.start()
            sends.append(cp)
        for k, (px, py) in enumerate(_other_chips(x, y)):
            _remote(src_ref, out_ref.at[2 * px + py], send_sems.at[k], recv_sems.at[k], (px, py, c)).wait_recv()
        for cp in sends:
            cp.wait_send()
        mine.wait()

    sem = pltpu.SemaphoreType.DMA
    return pl.pallas_call(
        body, name=name, in_specs=[_ANY], out_specs=_ANY, out_shape=SDS((N_CHIPS,) + buf.shape, buf.dtype),
        scratch_shapes=[sem((3,)), sem((3,)), sem], compiler_params=_COMM,
    )(buf)


def swap_sibling(tensors, name):
    n = len(tensors)

    def body(*refs):
        srcs, outs, send_sems, recv_sems = refs[:n], refs[n:2 * n], refs[2 * n], refs[2 * n + 1]
        x, y, c = _position()
        cps = [_remote(srcs[t].at[1 - c], outs[t], send_sems.at[t], recv_sems.at[t], (x, y, 1 - c))
               for t in range(n)]
        for cp in cps:
            cp.start()
        for cp in cps:
            cp.wait()

    sem = pltpu.SemaphoreType.DMA
    return pl.pallas_call(
        body, name=name, in_specs=[_ANY] * n, out_specs=[_ANY] * n,
        out_shape=[SDS(t.shape[1:], t.dtype) for t in tensors],
        scratch_shapes=[sem((n,)), sem((n,))], compiler_params=_COMM,
    )(*tensors)


def exchange_chips(tensors, name):
    n = len(tensors)

    def body(*refs):
        srcs, outs = refs[:n], refs[n:2 * n]
        send_sems, recv_sems, local_sems = refs[2 * n:]
        x, y, c = _position()
        me = 2 * x + y
        others = _other_chips(x, y)
        cps = []
        for t in range(n):
            cp = pltpu.make_async_copy(srcs[t].at[me], outs[t].at[me], local_sems.at[t])
            cp.start()
            cps.append(cp)
        sends = []
        for t in range(n):
            for k, (px, py) in enumerate(others):
                cp = _remote(srcs[t].at[2 * px + py], outs[t].at[me], send_sems.at[t, k], recv_sems.at[t, k],
                             (px, py, c))
                cp.start()
                sends.append(cp)
        for t in range(n):
            for k, (px, py) in enumerate(others):
                _remote(srcs[t].at[me], outs[t].at[2 * px + py], send_sems.at[t, k], recv_sems.at[t, k],
                        (px, py, c)).wait_recv()
        for cp in sends:
            cp.wait_send()
        for cp in cps:
            cp.wait()

    sem = pltpu.SemaphoreType.DMA
    return pl.pallas_call(
        body, name=name, in_specs=[_ANY] * n, out_specs=[_ANY] * n,
        out_shape=[SDS(t.shape, t.dtype) for t in tensors],
        scratch_shapes=[sem((n, 3)), sem((n, 3)), sem((n,))], compiler_params=_COMM,
    )(*tensors)


def join_halves(tensors, name):
    n = len(tensors)

    def body(*refs):
        outs, send_sems, recv_sems = refs[n:2 * n], refs[2 * n], refs[2 * n + 1]
        x, y, c = _position()
        sib = (x, y, 1 - c)
        sends = []
        for t in range(n):
            cp = _remote(outs[t].at[c], outs[t].at[c], send_sems.at[t], recv_sems.at[t], sib)
            cp.start()
            sends.append(cp)
        for t in range(n):
            _remote(outs[t].at[c], outs[t].at[1 - c], send_sems.at[t], recv_sems.at[t], sib).wait_recv()
        for cp in sends:
            cp.wait_send()

    sem = pltpu.SemaphoreType.DMA
    return pl.pallas_call(
        body, name=name, in_specs=[_ANY] * n, out_specs=[_ANY] * n,
        out_shape=[SDS(t.shape, t.dtype) for t in tensors],
        scratch_shapes=[sem((n,)), sem((n,))], input_output_aliases={t: t for t in range(n)},
        compiler_params=_COMM,
    )(*tensors)


BIG = ("w_in", "w_proj_attn", "w_proj_pool", "w_proj_conv", "conv_w", "w_out", "w_gate_up", "w_down")
REPLICATED = ("attn_norm", "b_forget", "b_gate", "pool_w", "pool_scale", "ffn_norm", "final_norm")
ORDER = ("attn_norm", "w_in", "b_forget", "b_gate", "w_proj_attn", "pool_w", "pool_scale", "w_proj_pool",
         "conv_w", "w_proj_conv", "w_out", "ffn_norm", "w_gate_up", "w_down", "final_norm")


def _proj_layout(D):
    lay = {"g": 0, "q": 3 * D}
    lay["k"] = lay["q"] + BRANCH_W
    lay["f"] = lay["k"] + BRANCH_W
    lay["v"] = lay["f"] + F_PAD
    lay["pc"] = lay["v"] + BRANCH_W
    lay["width"] = lay["pc"] + 4 * BRANCH_W
    return lay


_REF = dict(q=0, k=512, v=1024, f=1536, u=1544, cv=2056, cb=2568, cc=3080, g=3592)


def _packed_pieces(D):
    pieces = [(_REF["g"], 3 * D), (_REF["q"], BRANCH_W), (_REF["k"], BRANCH_W), (_REF["f"], HEADS),
              (None, F_PAD - HEADS), (_REF["v"], BRANCH_W)]
    for gi in range(N_GROUPS):
        pieces += [(_REF[name] + gi * GROUP_W, GROUP_W) for name in ("u", "cv", "cb", "cc")]
    return pieces


STAGE_TILE = 16


def _staged(c):
    return c + (8 if c >= _REF["u"] else 0)


def _stage_base(chip, cs):
    return _staged(chip * cs) // STAGE_TILE * STAGE_TILE


def _stage_rows(cs):
    return -(-(cs + 8 + STAGE_TILE) // STAGE_TILE) * STAGE_TILE


def _shard_runs(chip, cs):
    a, b = chip * cs, (chip + 1) * cs
    cut = _REF["u"]
    return [(a, cut - a), (cut, b - cut)] if a < cut < b else [(a, b - a)]


def _stage_shard(x, chip):
    L, cs, D = x.shape
    variants = []
    for s in range(N_CHIPS):
        parts, at = [], 0
        for start, n in _shard_runs(s, cs):
            row = _staged(start) - _stage_base(s, cs)
            parts += [jnp.zeros((L, row - at, D), x.dtype), x[:, start - s * cs:start - s * cs + n, :]]
            at = row + n
        parts.append(jnp.zeros((L, _stage_rows(cs) - at, D), x.dtype))
        variants.append(jnp.concatenate(parts, axis=1))
    return lax.select_n(chip, *variants)


def _unstage_shard(y, chip, cs):
    variants = []
    for s in range(N_CHIPS):
        rows = [(_staged(start) - _stage_base(s, cs), n) for start, n in _shard_runs(s, cs)]
        variants.append(jnp.concatenate([y[:, r:r + n, :] for r, n in rows], axis=1))
    return lax.select_n(chip, *variants)


def _pack_w_in_rows(shards, layer, cs):
    D = shards.shape[3]
    parts = []
    for start, n in _packed_pieces(D):
        if start is None:
            parts.append(jnp.zeros((n, D), shards.dtype))
        while start is not None and n:
            chip = start // cs
            take = min(n, (chip + 1) * cs - start)
            row = _staged(start) - _stage_base(chip, cs)
            parts.append(shards[chip, layer, row:row + take, :])
            start, n = start + take, n - take
    return jnp.concatenate(parts, axis=0)


def _unpack_w_in_rows(p, D, cs):
    where, at = {}, 0
    for start, n in _packed_pieces(D):
        if start is not None:
            where[start] = (n, at)
        at += n
    chips = []
    for s in range(N_CHIPS):
        parts, filled = [], 0
        for start in sorted(where):
            n, row0 = where[start]
            a, b = max(start, s * cs), min(start + n, (s + 1) * cs)
            if a < b:
                row = _staged(a) - _stage_base(s, cs)
                parts += [jnp.zeros((row - filled, D), p.dtype), p[row0 + a - start:row0 + b - start, :]]
                filled = row + b - a
        parts.append(jnp.zeros((_stage_rows(cs) - filled, D), p.dtype))
        chips.append(jnp.concatenate(parts, axis=0))
    return jnp.stack(chips)


def _split_flat(vec, shapes):
    out, at = [], 0
    for shp in shapes:
        n = int(np.prod(shp))
        out.append(vec[at:at + n].reshape(shp))
        at += n
    return out


def kernel(x, attn_norm, w_in, b_forget, b_gate, w_proj_attn, pool_w, pool_scale, w_proj_pool, conv_w, w_proj_conv, w_out, ffn_norm, w_gate_up, w_down, final_norm, loss_target, m_attn_norm, m_w_in, m_b_forget, m_b_gate, m_w_proj_attn, m_pool_w, m_pool_scale, m_w_proj_pool, m_conv_w, m_w_proj_conv, m_w_out, m_ffn_norm, m_w_gate_up, m_w_down, m_final_norm, v_attn_norm, v_w_in, v_b_forget, v_b_gate, v_w_proj_attn, v_pool_w, v_pool_scale, v_w_proj_pool, v_conv_w, v_w_proj_conv, v_w_out, v_ffn_norm, v_w_gate_up, v_w_down, v_final_norm):
    weights = dict(attn_norm=attn_norm, w_in=w_in, b_forget=b_forget, b_gate=b_gate, w_proj_attn=w_proj_attn,
                   pool_w=pool_w, pool_scale=pool_scale, w_proj_pool=w_proj_pool, conv_w=conv_w,
                   w_proj_conv=w_proj_conv, w_out=w_out, ffn_norm=ffn_norm, w_gate_up=w_gate_up, w_down=w_down,
                   final_norm=final_norm)
    mom_m = dict(attn_norm=m_attn_norm, w_in=m_w_in, b_forget=m_b_forget, b_gate=m_b_gate, w_proj_attn=m_w_proj_attn,
                 pool_w=m_pool_w, pool_scale=m_pool_scale, w_proj_pool=m_w_proj_pool, conv_w=m_conv_w,
                 w_proj_conv=m_w_proj_conv, w_out=m_w_out, ffn_norm=m_ffn_norm, w_gate_up=m_w_gate_up,
                 w_down=m_w_down, final_norm=m_final_norm)
    mom_v = dict(attn_norm=v_attn_norm, w_in=v_w_in, b_forget=v_b_forget, b_gate=v_b_gate, w_proj_attn=v_w_proj_attn,
                 pool_w=v_pool_w, pool_scale=v_pool_scale, w_proj_pool=v_w_proj_pool, conv_w=v_conv_w,
                 w_proj_conv=v_w_proj_conv, w_out=v_w_out, ffn_norm=v_ffn_norm, w_gate_up=v_w_gate_up,
                 w_down=v_w_down, final_norm=v_final_norm)

    Bl, S, D = x.shape
    T = Bl * S
    L = w_in.shape[0]
    F = w_down.shape[1] * N_CHIPS
    lay = _proj_layout(D)
    cst = _placement_constants()
    assert L == N_LAYERS and S % ATTN_BLOCK == 0 and F % (2 * LANES) == 0 and D % BRANCH_W == 0
    assert w_in.shape[2] * N_CHIPS == _REF["g"] + 3 * D and conv_w.shape[2] == GROUP_W

    send = {n: weights[n].astype(BF16) for n in BIG}
    send["conv_w"] = conv_w
    me_chip = 2 * lax.axis_index("x") + lax.axis_index("y")
    cs_in = w_in.shape[2]
    send["w_in"] = _stage_shard(w_in.transpose(0, 2, 1).astype(BF16), me_chip)
    pool = gather_buffers(send)
    gather_now(pool, [("w_in", 0)])
    rest = ("w_out", "w_proj_attn", "w_proj_pool", "w_gate_up", "w_proj_conv", "conv_w")
    late = ("w_out", "w_proj_attn", "w_proj_pool", "w_proj_conv", "conv_w")
    jobs = lambda kind, names, layer: [(kind, n, layer) for n in names]
    carried = {
        ("in_proj", 0): jobs("ici", rest, 0),
        ("attn_prep", 0): jobs("fwd", rest, 0),
        ("attn_fwd", 0): jobs("ici", ("w_in",), 1) + jobs("ici", ("w_down",), 0),
        ("poolconv_fwd", 0): jobs("fwd", ("w_in",), 1) + jobs("fwd", ("w_down",), 0),
        ("mix_fwd", 0): jobs("ici", ("w_gate_up",), 1),
        ("gate_up_proj", 0): jobs("ici", ("w_down",) + late, 1),
        ("ffn_down_fwd", 0): jobs("fwd", ("w_gate_up",), 1),
        ("in_proj", 1): jobs("fwd", ("w_down",) + late, 1),
    }
    carry = lambda call, layer: Hosted(pool, carried[call, layer]) if (call, layer) in carried else None
    w_down_f = lambda: pool["w_down"].reshape(L, F, D)
    pool_w_b = pool_w.astype(BF16)
    an3, fn3 = attn_norm.reshape(L, 1, D), ffn_norm.reshape(L, 1, D)
    bg3, ps3 = b_gate.reshape(L, 1, 3 * D), pool_scale.reshape(L, 1, BRANCH_W)
    bf3 = jnp.pad(b_forget, ((0, 0), (0, LANES - HEADS))).reshape(L, 1, LANES)

    xs = x.reshape(T, D)
    saved = []
    w_in_p = []
    for l in range(L):
        w_in_p.append(_pack_w_in_rows(pool["w_in"], l, cs_in))
        proj, h = norm_matmul(xs, an3, w_in_p[l], l, "rows", "in_proj", carry("in_proj", l))
        proj3 = proj.reshape(Bl, S, lay["width"])
        qa, ka = attn_prep(proj3, bf3, l, cst, lay, carry("attn_prep", l))
        ao, lse = attn_fwd(qa, ka, proj3, lay, carry("attn_fwd", l))
        po, co = poolconv_fwd(proj3, pool_w_b, ps3, pool["conv_w"], l, lay, carry("poolconv_fwd", l))
        ao2, po2, co2 = (a.reshape(T, BRANCH_W) for a in (ao, po, co))
        x1, ys, mixed = mix_fwd(ao2, po2, co2, proj, bg3, pool["w_proj_attn"], pool["w_proj_pool"],
                                pool["w_proj_conv"], pool["w_out"], l, xs, carry("mix_fwd", l))
        ab, h2 = norm_matmul(x1, fn3, pool["w_gate_up"], l, "by_shard", "gate_up_proj", carry("gate_up_proj", l))
        x2, s_act = ffn_down_fwd(ab, w_down_f(), l, x1, carry("ffn_down_fwd", l))
        saved.append(dict(x=xs, proj=proj, proj3=proj3, h=h, qa=qa, ka=ka, ao=ao, lse=lse, ao2=ao2, po2=po2,
                          co2=co2, ys=ys, mixed=mixed, x1=x1, ab=ab, h2=h2, s=s_act))
        xs = x2
    w_gu, w_o, conv_w_g = pool["w_gate_up"], pool["w_out"], pool["conv_w"]
    wpa, wpp, wpc = pool["w_proj_attn"], pool["w_proj_pool"], pool["w_proj_conv"]
    w_down_f = w_down_f()

    loss_row, dx, dxb, g_final = loss_head(xs, final_norm.reshape(1, D), loss_target.reshape(T, D))
    loss = lax.psum(loss_row[0, 0], AXES)

    reduced_names = tuple(n for n in BIG if n != "conv_w")
    early_names = tuple(n for n in reduced_names if n != "w_in")
    where = jnp.stack([lax.axis_index("c"), me_chip]).astype(jnp.int32)
    rs = {}

    def reduce_begin(layer, grads):
        for n, g in grads.items():
            g5 = g.reshape((1, N_CHIPS, 2, -1) + g.shape[-1:])
            rs["g%d:%s" % (layer, n)] = g5
            for role in "ra":
                rs["%s%d:%s" % (role, layer, n)] = lax.empty((N_CHIPS,) + g5.shape[3:], BF16)

    swap_jobs = lambda layer, names: [("swap", "g%d:%s" % (layer, n), "r%d:%s" % (layer, n), 0) for n in names]
    xchg_jobs = lambda layer, names: [("xchg", "s%d:%s" % (layer, n), "a%d:%s" % (layer, n)) for n in names]
    join_jobs = lambda layer, names: [("join", "o:" + n, layer) for n in names]

    def pair_sums(layer, names):
        for n in names:
            rs["s%d:%s" % (layer, n)] = add_pair(rs["g%d:%s" % (layer, n)], 0, where, rs["r%d:%s" % (layer, n)],
                                                 "add_pair_" + n)

    def chip_sums(layer, names, slot, n_slots):
        for n in names:
            rs["o:" + n] = add_chips(rs["a%d:%s" % (layer, n)], rs["s%d:%s" % (layer, n)], slot, where, n_slots,
                                     rs.get("o:" + n), "add_chips_" + n)

    small = {n: [None] * L for n in REPLICATED if n != "final_norm"}
    g_conv = [None] * L
    to3 = lambda a: a.reshape(Bl, S, -1)
    for l in reversed(range(L)):
        sv = saved[l]
        behind = (lambda jobs: Hosted(rs, jobs)) if l == 0 else (lambda jobs: None)
        grads = {}
        da, db = ffn_down_bwd(dxb, w_down_f, l, sv["ab"], behind(swap_jobs(1, reduced_names)))
        if l == 0:
            pair_sums(1, reduced_names)
        grads["w_down"] = matmul_tn(sv["s"], [dxb], "grad_w_down", hosted=behind(xchg_jobs(
            1, ("w_down", "w_out", "w_proj_attn", "w_proj_pool", "w_proj_conv"))))
        grads["w_gate_up"] = matmul_tn(sv["h2"], [da, db], "grad_w_gate_up", by_dest=True, tn=2 * F // N_CHIPS,
                                       tk=_tile(T, (1024, 512, 256)), hosted=behind(xchg_jobs(1, ("w_gate_up",))))
        dx1, dx1b, g_fn = matmul_nt_normbwd([da, db], w_gu, l, "by_shard", sv["x1"], fn3, dx, "gate_up_bwd",
                                            behind(xchg_jobs(1, ("w_in",))))
        small["ffn_norm"][l] = g_fn[0]
        if l == 0:
            chip_sums(1, reduced_names, 1, L)
        dys, dproj, dao, dpo, dco, g_bg = mix_bwd(dx1b, w_o, sv["proj"], bg3, sv["ys"], wpa, wpp, wpc, l,
                                                  lay["width"], behind(join_jobs(1, reduced_names)))
        small["b_gate"][l] = g_bg[0]
        grads["w_out"] = matmul_tn(sv["mixed"], [dx1b], "grad_w_out")
        for n, (name, br) in enumerate((("w_proj_attn", sv["ao2"]), ("w_proj_pool", sv["po2"]),
                                        ("w_proj_conv", sv["co2"]))):
            grads[name] = matmul_tn(br, [dys], "grad_" + name, b_col0=n * D, n_cols=D, by_dest=True,
                                    tn=D // N_CHIPS)
        if l == 0:
            reduce_begin(0, grads)
        dqa, dka, dproj3 = attn_bwd(sv["qa"], sv["ka"], sv["proj3"], to3(dao), sv["ao"], sv["lse"], to3(dproj), lay,
                                    behind(swap_jobs(0, early_names)))
        if l == 0:
            pair_sums(0, early_names)
        dproj3, g_bf = attn_post(dqa, dka, sv["proj3"], bf3, l, dproj3, cst, lay)
        small["b_forget"][l] = g_bf[0, :HEADS]
        dproj3, g_pw, g_ps, g_conv[l] = poolconv_bwd(sv["proj3"], to3(dpo), to3(dco), pool_w_b, ps3, conv_w_g, l,
                                                     dproj3, lay)
        small["pool_w"][l], small["pool_scale"][l] = g_pw, g_ps[0]
        dproj = dproj3.reshape(T, lay["width"])
        grads["w_in"] = _unpack_w_in_rows(matmul_tn(dproj, [sv["h"]], "grad_w_in"), D, cs_in)
        dx, dxb, g_an = matmul_nt_normbwd([dproj], w_in_p[l], l, "rows", sv["x"], an3, dx1, "in_proj_bwd",
                                          behind(xchg_jobs(0, early_names)))
        small["attn_norm"][l] = g_an[0]
        reduce_begin(l, grads if l else {"w_in": grads["w_in"]})
    grad_x = dx.reshape(Bl, S, D)

    small_shapes = [weights[n].shape for n in REPLICATED] + [(L, N_CHIPS) + conv_w.shape[1:]]
    small_vec = jnp.concatenate([jnp.stack(small[n]).reshape(-1) for n in REPLICATED[:-1]]
                                + [g_final[0], jnp.stack(g_conv).reshape(-1)])
    n_small = small_vec.shape[0]
    small_vec = jnp.pad(small_vec, (0, -n_small % (2 * N_CHIPS * 16 * LANES))).astype(BF16)
    rs["g0:small"] = small_vec.reshape(1, N_CHIPS, 2, -1, LANES)
    for role in "ra":
        rs[role + "0:small"] = lax.empty((N_CHIPS,) + rs["g0:small"].shape[3:], BF16)
    last = ("w_in", "small")
    comm_now(rs, [swap_jobs(0, last)], "swap_grad_halves")
    pair_sums(0, last)
    comm_now(rs, [xchg_jobs(0, last)], "exchange_grad_chips")
    chip_sums(0, reduced_names, 0, L)
    chip_sums(0, ("small",), 0, 1)
    comm_now(rs, [join_jobs(0, reduced_names + ("small",))], "join_grad_halves")
    shard_grads = {n: rs["o:" + n].reshape((L, -1) + rs["o:" + n].shape[-1:]) for n in reduced_names}
    small_all = allgather_chips(rs["o:small"].reshape(-1, LANES), "allgather_small_grads").reshape(-1)[:n_small]
    *rep_list, conv_all = _split_flat(small_all, small_shapes)
    rep_grads = dict(zip(REPLICATED, rep_list))
    shard_grads["conv_w"] = lax.dynamic_index_in_dim(conv_all, me_chip, 1, keepdims=False)

    delta, new_m, new_v = {}, {}, {}
    for n in BIG:
        shp = weights[n].shape
        if n == "w_in":
            view, back = (lambda a: a.transpose(2, 0, 1)), (lambda a: a.transpose(1, 2, 0))
            g = _unstage_shard(shard_grads[n], me_chip, cs_in).transpose(1, 0, 2)
        else:
            view, back = (lambda a: a.reshape(-1, shp[-1])), (lambda a: a.reshape(shp))
            g = view(shard_grads[n])
        d, nm, nv = adamw(view(weights[n]), g, view(mom_m[n]), view(mom_v[n]), "adamw_" + n)
        delta[n], new_m[n], new_v[n], shard_grads[n] = back(d), back(nm), back(nv), back(g)

    def rows(d):
        vec = jnp.concatenate([d[n].reshape(-1) for n in REPLICATED])
        return jnp.pad(vec, (0, -vec.shape[0] % (8 * LANES))).reshape(-1, LANES)

    outs = adamw(rows(weights), rows(rep_grads), rows(mom_m), rows(mom_v), "adamw_replicated")
    for res, o in zip((delta, new_m, new_v), outs):
        res.update(zip(REPLICATED, _split_flat(o.reshape(-1), small_shapes[:len(REPLICATED)])))
    all_grads = {**shard_grads, **rep_grads}

    return (loss, grad_x, *[all_grads[n] for n in ORDER], *[delta[n] for n in ORDER],
            *[new_m[n] for n in ORDER], *[new_v[n] for n in ORDER])
```

```python
import numpy as np
import jax
import jax.numpy as jnp
from jax import lax
from jax.experimental import pallas as pl
from jax.experimental.pallas import tpu as pltpu

F32, BF16 = jnp.float32, jnp.bfloat16
SDS = jax.ShapeDtypeStruct
MESH = pl.DeviceIdType.MESH
AXES = ("x", "y", "c")
N_CHIPS = 4
N_LAYERS = 2
LANES = 128
VMEM_LIMIT = 48 * 1024 * 1024

HEADS, HEAD_DIM = 8, 64
HEAD_PAD = 128
BRANCH_W = 512
GROUP_W = 128
N_GROUPS = BRANCH_W // GROUP_W
POOL_WINDOWS = (2, 4, 8, 16)
F_PAD = 512
ATTN_BLOCK = 256
RMS_EPS = 1e-6
NEG_INF = -1e30
ADAM_LR, ADAM_B1, ADAM_B2, ADAM_EPS, ADAM_WD, ADAM_STEP = 0.001, 0.9, 0.999, 1e-08, 0.01, 10

NT = (((1,), (1,)), ((), ()))
TN = (((0,), (0,)), ((), ()))
_ANY = pl.BlockSpec(memory_space=pl.ANY)


def _tile(n, prefs):
    for p in prefs:
        if n % p == 0:
            return p
    raise ValueError(f"no tile of {prefs} divides {n}")


def _params(*sem):
    return pltpu.CompilerParams(dimension_semantics=sem, vmem_limit_bytes=VMEM_LIMIT)


def _sigmoid(z):
    return 0.5 * jnp.tanh(0.5 * z) + 0.5


def _split3(x):
    h1 = x.astype(BF16)
    r1 = x - h1.astype(F32)
    h2 = r1.astype(BF16)
    h3 = (r1 - h2.astype(F32)).astype(BF16)
    return h1, h2, h3


def _position():
    return lax.axis_index("x"), lax.axis_index("y"), lax.axis_index("c")


def _other_chips(x, y):
    return [(1 - x, y), (x, 1 - y), (1 - x, 1 - y)]


def _remote(src, dst, send_sem, recv_sem, device):
    return pltpu.make_async_remote_copy(src_ref=src, dst_ref=dst, send_sem=send_sem, recv_sem=recv_sem,
                                        device_id=device, device_id_type=MESH)


ROW_SHARDED = ("w_out", "w_down")
FETCHER = dict(w_in=0, w_out=0, w_proj_attn=0, w_proj_pool=0, w_gate_up=1, w_down=1, w_proj_conv=1, conv_w=1)


class Hosted:
    def __init__(self, pool, jobs):
        self.pool, self.jobs = pool, list(jobs)
        names = set()
        for job in self.jobs:
            names.update(job[1:3] if job[0] in ("swap", "xchg") else job[1:2])
        self.names = sorted(names)


def _hosted_plan(hosted, refs, send_sems, recv_sems):
    x, y, c = _position()
    me = 2 * x + y
    others = _other_chips(x, y)
    sibling = (x, y, 1 - c)
    plan = []
    for j, job in enumerate(hosted.jobs):
        kind = job[0]
        sems = lambda k, j=j: (send_sems.at[j, k], recv_sems.at[j, k])
        if kind in ("ici", "fwd"):
            _, name, layer = job
            ref = refs[name]
            win = (lambda chip, ref=ref, layer=layer: ref.at[layer, chip]) if name in ROW_SHARDED else (
                lambda chip, ref=ref, layer=layer: ref.at[chip, layer])
            mine = c == FETCHER[name]
            if kind == "ici":
                sends = [_remote(win(me), win(me), *sems(k), (px, py, c)) for k, (px, py) in enumerate(others)]
                arrivals = [_remote(win(2 * px + py), win(2 * px + py), *sems(k), (px, py, c))
                            for k, (px, py) in enumerate(others)]
                plan.append((mine, sends, arrivals, []))
            else:
                sends = [_remote(win(2 * px + py), win(2 * px + py), *sems(k), sibling)
                         for k, (px, py) in enumerate(others)]
                plan.append((mine, sends, [], sends))
        elif kind == "swap":
            _, src, dst, layer = job
            cp = _remote(refs[src].at[layer, :, 1 - c], refs[dst], *sems(0), sibling)
            plan.append((True, [cp], [cp], []))
        elif kind == "xchg":
            _, src, dst = job
            sends = [_remote(refs[src].at[2 * px + py], refs[dst].at[me], *sems(k), (px, py, c))
                     for k, (px, py) in enumerate(others)]
            arrivals = [_remote(refs[src].at[me], refs[dst].at[2 * px + py], *sems(k), (px, py, c))
                        for k, (px, py) in enumerate(others)]
            plan.append((True, sends, arrivals, []))
        else:
            _, name, layer = job
            ref = refs[name]
            cp = _remote(ref.at[layer, c], ref.at[layer, c], *sems(0), sibling)
            arrival = _remote(ref.at[layer, c], ref.at[layer, 1 - c], *sems(0), sibling)
            plan.append((True, [cp], [arrival], []))
    return plan


def _hosted_start(plan, now):
    for mine, sends, _, _ in plan:
        @pl.when(now & mine)
        def _(sends=sends):
            for cp in sends:
                cp.start()


def _hosted_finish(plan, now):
    for mine, sends, arrivals, sibling_arrivals in plan:
        @pl.when(now & mine)
        def _(sends=sends, arrivals=arrivals):
            for cp in arrivals:
                cp.wait_recv()
            for cp in sends:
                cp.wait_send()

        if sibling_arrivals:
            @pl.when(now & jnp.logical_not(mine))
            def _(sibling_arrivals=sibling_arrivals):
                for cp in sibling_arrivals:
                    cp.wait_recv()


def _pcall(body, hosted, *, name, grid, in_specs, out_specs, out_shape, semantics, scratch_shapes=(), aliases=None):
    aliases = dict(aliases or {})
    if hosted is None or not hosted.jobs:
        return pl.pallas_call(body, name=name, grid=grid, in_specs=in_specs, out_specs=out_specs,
                              out_shape=out_shape, scratch_shapes=list(scratch_shapes),
                              input_output_aliases=aliases, compiler_params=_params(*semantics))
    single = not isinstance(out_shape, (list, tuple))
    out_specs_l = [out_specs] if single else list(out_specs)
    out_shape_l = [out_shape] if single else list(out_shape)
    n_in, n_out, n_buf, n_job = len(in_specs), len(out_specs_l), len(hosted.names), len(hosted.jobs)

    def carrying(*refs):
        ins, outs = refs[:n_in], refs[n_in + n_buf:n_in + n_buf + n_out]
        bufs = refs[n_in + n_buf + n_out:n_in + 2 * n_buf + n_out]
        rest = refs[n_in + 2 * n_buf + n_out:]
        scratch, send_sems, recv_sems = rest[:-2], rest[-2], rest[-1]
        first, last = True, True
        for axis, size in enumerate(grid):
            first = first & (pl.program_id(axis) == 0)
            last = last & (pl.program_id(axis) == size - 1)
        plan = _hosted_plan(hosted, dict(zip(hosted.names, bufs)), send_sems, recv_sems)
        _hosted_start(plan, first)
        body(*ins, *outs, *scratch)
        _hosted_finish(plan, last)

    def run(*args):
        bufs = [hosted.pool[n] for n in hosted.names]
        sem = pltpu.SemaphoreType.DMA
        res = pl.pallas_call(
            carrying, name=name, grid=grid, in_specs=list(in_specs) + [_ANY] * n_buf,
            out_specs=out_specs_l + [_ANY] * n_buf,
            out_shape=out_shape_l + [SDS(b.shape, b.dtype) for b in bufs],
            scratch_shapes=list(scratch_shapes) + [sem((n_job, 3)), sem((n_job, 3))],
            input_output_aliases={**aliases, **{n_in + i: n_out + i for i in range(n_buf)}},
            compiler_params=pltpu.CompilerParams(dimension_semantics=semantics, vmem_limit_bytes=VMEM_LIMIT,
                                                 has_side_effects=True),
        )(*args, *bufs)
        hosted.pool.update(zip(hosted.names, res[n_out:]))
        return res[0] if single else res[:n_out]

    return run


def _dot(a, b):
    return jnp.dot(a, b, preferred_element_type=F32)


def _dot_nt(a, b):
    return lax.dot_general(a, b, NT, preferred_element_type=F32)


def _dot_tn(a, b):
    return lax.dot_general(a, b, TN, preferred_element_type=F32)


def norm_matmul(x, gain, w, layer, kind, name, hosted=None):
    T, D = x.shape
    if kind == "by_shard":
        tn = w.shape[3]
        N = N_CHIPS * tn
        w_spec = pl.BlockSpec((None, None, D, tn), lambda i, j: (j, layer, 0, 0))
        mm = _dot
    else:
        N = w.shape[0]
        tn = _tile(N, (1024, 512, 256, 128))
        w_spec = pl.BlockSpec((tn, D), lambda i, j: (j, 0))
        mm = _dot_nt
    tm = _tile(T, (1024, 512, 256, 128))

    def body(x_ref, g_ref, w_ref, y_ref, h_ref):
        @pl.when(pl.program_id(1) == 0)
        def _():
            xf = x_ref[...]
            r = lax.rsqrt(jnp.mean(xf * xf, axis=-1, keepdims=True) + RMS_EPS)
            h_ref[...] = ((xf * r) * g_ref[...]).astype(BF16)

        y_ref[...] = mm(h_ref[...], w_ref[...]).astype(BF16)

    return _pcall(
        body, hosted, name=name, grid=(T // tm, N // tn),
        in_specs=[pl.BlockSpec((tm, D), lambda i, j: (i, 0)),
                  pl.BlockSpec((None, 1, D), lambda i, j: (layer, 0, 0)),
                  w_spec],
        out_specs=[pl.BlockSpec((tm, tn), lambda i, j: (i, j)),
                   pl.BlockSpec((tm, D), lambda i, j: (i, 0))],
        out_shape=[SDS((T, N), BF16), SDS((T, D), BF16)],
        semantics=("arbitrary", "arbitrary"),
    )(x, gain, w)


def matmul_nt_normbwd(dys, w, layer, kind, x, gain, dres, name, hosted=None):
    T, D = x.shape
    width = dys[0].shape[1]
    if kind == "by_shard":
        tk = w.shape[3]
        w_spec = pl.BlockSpec((None, None, D, tk), lambda i, k: (k, layer, 0, 0))
        mm = _dot_nt
    else:
        tk = _tile(width, (1024, 512, 256, 128))
        w_spec = pl.BlockSpec((tk, D), lambda i, k: (k, 0))
        mm = _dot
    per = width // tk
    nk = per * len(dys)
    tm = _tile(T, (512, 256, 128))
    n_dy = len(dys)

    def dy_spec(p):
        return pl.BlockSpec((tm, tk), lambda i, k: (i, jnp.clip(k - p * per, 0, per - 1)))

    def body(*refs):
        dy_refs = refs[:n_dy]
        w_ref, x_ref, g_ref, dres_ref, dx_ref, dxb_ref, dg_ref, acc_ref = refs[n_dy:]
        i, k = pl.program_id(0), pl.program_id(1)

        @pl.when(k == 0)
        def _():
            acc_ref[...] = jnp.zeros_like(acc_ref)

        for p in range(n_dy):
            @pl.when((k >= p * per) & (k < (p + 1) * per))
            def _(p=p):
                acc_ref[...] += mm(dy_refs[p][...], w_ref[...])

        @pl.when(k == nk - 1)
        def _():
            xf = x_ref[...]
            r = lax.rsqrt(jnp.mean(xf * xf, axis=-1, keepdims=True) + RMS_EPS)
            xhat = xf * r
            dh = acc_ref[...]
            dhg = dh * g_ref[...]
            dx = dres_ref[...] + r * (dhg - xhat * jnp.mean(dhg * xhat, axis=-1, keepdims=True))
            dx_ref[...] = dx
            dxb_ref[...] = dx.astype(BF16)
            part = jnp.sum(dh * xhat, axis=0, keepdims=True)

            @pl.when(i == 0)
            def _():
                dg_ref[...] = part

            @pl.when(i > 0)
            def _():
                dg_ref[...] += part

    row = pl.BlockSpec((tm, D), lambda i, k: (i, 0))
    return _pcall(
        body, hosted, name=name, grid=(T // tm, nk),
        in_specs=[dy_spec(p) for p in range(n_dy)] + [
            w_spec, row, pl.BlockSpec((None, 1, D), lambda i, k: (layer, 0, 0)), row],
        out_specs=[row, row, pl.BlockSpec((1, D), lambda i, k: (0, 0))],
        out_shape=[SDS((T, D), F32), SDS((T, D), BF16), SDS((1, D), F32)],
        scratch_shapes=[pltpu.VMEM((tm, D), F32)],
        semantics=("arbitrary", "arbitrary"),
    )(*dys, w, x, gain, dres)


def matmul_tn(a, bs, name, b_col0=0, n_cols=None, by_dest=False, tn=None, tk=None, hosted=None):
    T, M = a.shape
    width = bs[0].shape[1]
    N = n_cols if n_cols else width * len(bs)
    tm = _tile(M, (1024, 512, 256, 128))
    tn = tn or _tile(N, (512, 256, 128))
    tk = tk or _tile(T, (4096, 2048, 1024, 512, 256))
    assert b_col0 % tn == 0 and width % tn == 0
    j0, per, nk, n_b = b_col0 // tn, width // tn, T // tk, len(bs)

    def b_spec(p):
        return pl.BlockSpec((tk, tn), lambda i, j, k: (k, jnp.clip(j0 + j - p * per, 0, per - 1)))

    def body(*refs):
        a_ref, b_refs = refs[0], refs[1:1 + n_b]
        o_ref, acc_ref = refs[-2], refs[-1]
        j, k = pl.program_id(1), pl.program_id(2)

        @pl.when(k == 0)
        def _():
            acc_ref[...] = jnp.zeros_like(acc_ref)

        for p in range(n_b):
            @pl.when((j0 + j >= p * per) & (j0 + j < (p + 1) * per))
            def _(p=p):
                acc_ref[...] += _dot_tn(a_ref[...], b_refs[p][...])

        @pl.when(k == nk - 1)
        def _():
            o_ref[...] = acc_ref[...].astype(BF16)

    if by_dest:
        cs = N // N_CHIPS
        npd = cs // tn
        out_shape = SDS((N_CHIPS, M, cs), BF16)
        out_spec = pl.BlockSpec((None, tm, tn), lambda i, j, k: (j // npd, i, j % npd))
    else:
        out_shape = SDS((M, N), BF16)
        out_spec = pl.BlockSpec((tm, tn), lambda i, j, k: (i, j))
    return _pcall(
        body, hosted, name=name, grid=(M // tm, N // tn, nk),
        in_specs=[pl.BlockSpec((tk, tm), lambda i, j, k: (k, i))] + [b_spec(p) for p in range(n_b)],
        out_specs=out_spec, out_shape=out_shape,
        scratch_shapes=[pltpu.VMEM((tm, tn), F32)],
        semantics=("arbitrary", "arbitrary", "arbitrary"),
    )(a, *bs)


def ffn_down_fwd(ab, w_down, layer, x1, hosted=None):
    T, D = x1.shape
    F = w_down.shape[1]
    tm = _tile(T, (512, 256, 128))
    tk = F // 2
    nk = F // tk

    def body(a_ref, b_ref, w_ref, x_ref, x2_ref, s_ref, acc_ref):
        k = pl.program_id(1)

        @pl.when(k == 0)
        def _():
            acc_ref[...] = x_ref[...]

        a = a_ref[...].astype(F32)
        s = (a * _sigmoid(a) * b_ref[...].astype(F32)).astype(BF16)
        s_ref[...] = s
        acc_ref[...] += _dot(s, w_ref[...])

        @pl.when(k == nk - 1)
        def _():
            x2_ref[...] = acc_ref[...]

    return _pcall(
        body, hosted, name="ffn_down_fwd", grid=(T // tm, nk),
        in_specs=[pl.BlockSpec((tm, tk), lambda i, k: (i, k)),
                  pl.BlockSpec((tm, tk), lambda i, k: (i, nk + k)),
                  pl.BlockSpec((None, tk, D), lambda i, k: (layer, k, 0)),
                  pl.BlockSpec((tm, D), lambda i, k: (i, 0))],
        out_specs=[pl.BlockSpec((tm, D), lambda i, k: (i, 0)),
                   pl.BlockSpec((tm, tk), lambda i, k: (i, k))],
        out_shape=[SDS((T, D), F32), SDS((T, F), BF16)],
        scratch_shapes=[pltpu.VMEM((tm, D), F32)],
        semantics=("arbitrary", "arbitrary"),
    )(ab, ab, w_down, x1)


def ffn_down_bwd(dx2b, w_down, layer, ab, hosted=None):
    T, D = dx2b.shape
    F = w_down.shape[1]
    tm = _tile(T, (512, 256, 128))
    tn = F // 2
    nj = F // tn

    def body(dx_ref, w_ref, a_ref, b_ref, da_ref, db_ref):
        ds = _dot_nt(dx_ref[...], w_ref[...])
        a = a_ref[...].astype(F32)
        sg = _sigmoid(a)
        da_ref[...] = (ds * b_ref[...].astype(F32) * (sg * (1.0 + a * (1.0 - sg)))).astype(BF16)
        db_ref[...] = (ds * (a * sg)).astype(BF16)

    blk = pl.BlockSpec((tm, tn), lambda i, j: (i, j))
    return _pcall(
        body, hosted, name="ffn_down_bwd", grid=(T // tm, nj),
        in_specs=[pl.BlockSpec((tm, D), lambda i, j: (i, 0)),
                  pl.BlockSpec((None, tn, D), lambda i, j: (layer, j, 0)),
                  blk, pl.BlockSpec((tm, tn), lambda i, j: (i, nj + j))],
        out_specs=[blk, blk],
        out_shape=[SDS((T, F), BF16), SDS((T, F), BF16)],
        semantics=("arbitrary", "arbitrary"),
    )(dx2b, w_down, ab, ab)


def _mix_specs(tm, D, layer):
    cs = D // N_CHIPS
    row = lambda w: pl.BlockSpec((tm, w), lambda i: (i, 0))
    wp = pl.BlockSpec((N_CHIPS, None, BRANCH_W, cs), lambda i: (0, layer, 0, 0))
    wo = pl.BlockSpec((None, N_CHIPS, cs, D), lambda i: (layer, 0, 0, 0))
    bg = pl.BlockSpec((None, 1, 3 * D), lambda i: (layer, 0, 0))
    return row, wp, wo, bg


def mix_fwd(ao, po, co, proj, b_gate, wpa, wpp, wpc, w_out, layer, x, hosted=None):
    T, D = x.shape
    cs = D // N_CHIPS
    tm = _tile(T, (256, 128))
    row, wp, wo, bg = _mix_specs(tm, D, layer)

    def body(ao_ref, po_ref, co_ref, g_ref, bg_ref, wpa_ref, wpp_ref, wpc_ref, wo_ref, x_ref,
             x1_ref, ys_ref, mixed_ref):
        mixed = jnp.zeros((tm, D), F32)
        for n, (br, wp_ref) in enumerate(((ao_ref, wpa_ref), (po_ref, wpp_ref), (co_ref, wpc_ref))):
            y = jnp.concatenate([_dot(br[...], wp_ref[j]) for j in range(N_CHIPS)], axis=1)
            cols = slice(n * D, (n + 1) * D)
            gate = _sigmoid(g_ref[:, cols].astype(F32) + bg_ref[:, cols])
            ys_ref[:, cols] = y.astype(BF16)
            mixed = mixed + gate * y
        mb = mixed.astype(BF16)
        mixed_ref[...] = mb
        acc = x_ref[...]
        for j in range(N_CHIPS):
            acc = acc + _dot(mb[:, j * cs:(j + 1) * cs], wo_ref[j])
        x1_ref[...] = acc

    return _pcall(
        body, hosted, name="mix_fwd", grid=(T // tm,),
        in_specs=[row(BRANCH_W), row(BRANCH_W), row(BRANCH_W), row(3 * D), bg, wp, wp, wp, wo, row(D)],
        out_specs=[row(D), row(3 * D), row(D)],
        out_shape=[SDS((T, D), F32), SDS((T, 3 * D), BF16), SDS((T, D), BF16)],
        semantics=("arbitrary",),
    )(ao, po, co, proj, b_gate, wpa, wpp, wpc, w_out, x)


def mix_bwd(dx1b, w_out, proj, b_gate, ys, wpa, wpp, wpc, layer, width, hosted=None):
    T, D = dx1b.shape
    cs = D // N_CHIPS
    tm = _tile(T, (256, 128))
    row, wp, wo, bg = _mix_specs(tm, D, layer)

    def body(dx_ref, wo_ref, g_ref, bg_ref, ys_ref, wpa_ref, wpp_ref, wpc_ref,
             dys_ref, dg_ref, dao_ref, dpo_ref, dco_ref, dbg_ref):
        i = pl.program_id(0)
        dx = dx_ref[...]
        dmixed = jnp.concatenate([_dot_nt(dx, wo_ref[j]) for j in range(N_CHIPS)], axis=1)
        for n, (wp_ref, dbr) in enumerate(((wpa_ref, dao_ref), (wpp_ref, dpo_ref), (wpc_ref, dco_ref))):
            cols = slice(n * D, (n + 1) * D)
            gate = _sigmoid(g_ref[:, cols].astype(F32) + bg_ref[:, cols])
            dy = (dmixed * gate).astype(BF16)
            dys_ref[:, cols] = dy
            dgp = dmixed * ys_ref[:, cols].astype(F32) * gate * (1.0 - gate)
            dg_ref[:, cols] = dgp.astype(BF16)
            part = jnp.sum(dgp, axis=0, keepdims=True)

            @pl.when(i == 0)
            def _():
                dbg_ref[:, cols] = part

            @pl.when(i > 0)
            def _():
                dbg_ref[:, cols] += part

            acc = jnp.zeros((tm, BRANCH_W), F32)
            for j in range(N_CHIPS):
                acc = acc + _dot_nt(dy[:, j * cs:(j + 1) * cs], wp_ref[j])
            dbr[...] = acc.astype(BF16)

    return _pcall(
        body, hosted, name="mix_bwd", grid=(T // tm,),
        in_specs=[row(D), wo, row(3 * D), bg, row(3 * D), wp, wp, wp],
        out_specs=[row(3 * D), row(3 * D), row(BRANCH_W), row(BRANCH_W), row(BRANCH_W),
                   pl.BlockSpec((1, 3 * D), lambda i: (0, 0))],
        out_shape=[SDS((T, 3 * D), BF16), SDS((T, width), BF16), SDS((T, BRANCH_W), BF16),
                   SDS((T, BRANCH_W), BF16), SDS((T, BRANCH_W), BF16), SDS((1, 3 * D), F32)],
        semantics=("arbitrary",),
    )(dx1b, w_out, proj, b_gate, ys, wpa, wpp, wpc)


def loss_head(x2, gain, target):
    T, D = x2.shape
    tm = _tile(T, (512, 256, 128))

    def body(x_ref, g_ref, t_ref, loss_ref, dx_ref, dxb_ref, dg_ref):
        i = pl.program_id(0)
        xf = x_ref[...]
        g = g_ref[...]
        r = lax.rsqrt(jnp.mean(xf * xf, axis=-1, keepdims=True) + RMS_EPS)
        xhat = xf * r
        diff = xhat * g - t_ref[...]
        part_loss = 0.5 * jnp.sum(jnp.mean(diff * diff, axis=-1, keepdims=True), axis=0, keepdims=True)
        dy = diff * (1.0 / D)
        dhg = dy * g
        dx = r * (dhg - xhat * jnp.mean(dhg * xhat, axis=-1, keepdims=True))
        dx_ref[...] = dx
        dxb_ref[...] = dx.astype(BF16)
        part_g = jnp.sum(dy * xhat, axis=0, keepdims=True)
        part_l = jnp.broadcast_to(part_loss, (1, LANES))

        @pl.when(i == 0)
        def _():
            dg_ref[...] = part_g
            loss_ref[...] = part_l

        @pl.when(i > 0)
        def _():
            dg_ref[...] += part_g
            loss_ref[...] += part_l

    row = pl.BlockSpec((tm, D), lambda i: (i, 0))
    return pl.pallas_call(
        body, name="loss_head", grid=(T // tm,),
        in_specs=[row, pl.BlockSpec((1, D), lambda i: (0, 0)), row],
        out_specs=[pl.BlockSpec((1, LANES), lambda i: (0, 0)), row, row, pl.BlockSpec((1, D), lambda i: (0, 0))],
        out_shape=[SDS((1, LANES), F32), SDS((T, D), F32), SDS((T, D), BF16), SDS((1, D), F32)],
        compiler_params=_params("arbitrary"),
    )(x2, gain, target)


def _placement_constants():
    w = HEADS * HEAD_PAD
    pq = np.zeros((BRANCH_W, w), np.float32)
    pk = np.zeros((BRANCH_W, w), np.float32)
    pfq = np.zeros((3, LANES, w), np.float32)
    pfk = np.zeros((3, LANES, w), np.float32)
    cq = np.zeros((1, w), np.float32)
    ck = np.zeros((1, w), np.float32)
    eq = np.zeros((w, LANES), np.float32)
    ek = np.zeros((w, LANES), np.float32)
    for h in range(HEADS):
        for d in range(HEAD_DIM):
            pq[h * HEAD_DIM + d, h * HEAD_PAD + d] = HEAD_DIM ** -0.5
            pk[h * HEAD_DIM + d, h * HEAD_PAD + d] = 1.0
        for i in range(3):
            pfq[i, h, h * HEAD_PAD + HEAD_DIM + i] = 1.0
            pfk[i, h, h * HEAD_PAD + HEAD_DIM + 3 + i] = -1.0
            cq[0, h * HEAD_PAD + HEAD_DIM + 3 + i] = 1.0
            ck[0, h * HEAD_PAD + HEAD_DIM + i] = 1.0
        eq[h * HEAD_PAD + HEAD_DIM, h] = 1.0
        ek[h * HEAD_PAD + HEAD_DIM + 3, h] = -1.0
    bf = lambda a: jnp.asarray(a, BF16)
    return dict(pq=bf(pq), pk=bf(pk), pfq=bf(pfq), pfk=bf(pfk), cq=jnp.asarray(cq), ck=jnp.asarray(ck),
                pqkt=bf(np.concatenate([pq.T, pk.T], axis=0)), eq=bf(eq), ek=bf(ek))


def attn_prep(proj3, bf_rows, layer, cst, lay, hosted=None):
    Bl, S, _ = proj3.shape
    ts = ATTN_BLOCK
    w = HEADS * HEAD_PAD

    def body(q_ref, k_ref, f_ref, bf_ref, pq_ref, pk_ref, pfq_ref, pfk_ref, cq_ref, ck_ref,
             qa_ref, ka_ref, carry_ref):
        @pl.when(pl.program_id(1) == 0)
        def _():
            carry_ref[...] = jnp.zeros_like(carry_ref)

        z = f_ref[...].astype(F32) + bf_ref[...]
        logf = jnp.minimum(z, 0.0) - jnp.log(1.0 + jnp.exp(-jnp.abs(z)))
        r = lax.broadcasted_iota(jnp.int32, (ts, ts), 0)
        c = lax.broadcasted_iota(jnp.int32, (ts, ts), 1)
        tri = jnp.where(r >= c, 1.0, 0.0).astype(BF16)
        fcum = carry_ref[...]
        for part in _split3(logf):
            fcum = fcum + _dot(tri, part)
        carry_ref[...] = fcum[ts - 1:ts, :]
        qa = _dot(q_ref[...], pq_ref[...]) + cq_ref[...]
        ka = _dot(k_ref[...], pk_ref[...]) + ck_ref[...]
        for i, part in enumerate(_split3(fcum)):
            qa = qa + _dot(part, pfq_ref[i])
            ka = ka + _dot(part, pfk_ref[i])
        qa_ref[...] = qa.astype(BF16)
        ka_ref[...] = ka.astype(BF16)

    cfull = lambda shape: pl.BlockSpec(shape, lambda b, s: (0,) * len(shape))
    return _pcall(
        body, hosted, name="attn_prep", grid=(Bl, S // ts),
        in_specs=[pl.BlockSpec((None, ts, BRANCH_W), lambda b, s: (b, s, lay["q"] // BRANCH_W)),
                  pl.BlockSpec((None, ts, BRANCH_W), lambda b, s: (b, s, lay["k"] // BRANCH_W)),
                  pl.BlockSpec((None, ts, LANES), lambda b, s: (b, s, lay["f"] // LANES)),
                  pl.BlockSpec((None, 1, LANES), lambda b, s: (layer, 0, 0)),
                  cfull((BRANCH_W, w)), cfull((BRANCH_W, w)),
                  cfull((3, LANES, w)), cfull((3, LANES, w)), cfull((1, w)), cfull((1, w))],
        out_specs=[pl.BlockSpec((None, ts, w), lambda b, s: (b, s, 0)),
                   pl.BlockSpec((None, ts, w), lambda b, s: (b, s, 0))],
        out_shape=[SDS((Bl, S, w), BF16), SDS((Bl, S, w), BF16)],
        scratch_shapes=[pltpu.VMEM((1, LANES), F32)],
        semantics=("arbitrary", "arbitrary"),
    )(proj3, proj3, proj3, bf_rows, cst["pq"], cst["pk"], cst["pfq"], cst["pfk"], cst["cq"], cst["ck"])


def attn_fwd(qa, ka, proj3, lay, hosted=None):
    Bl, S, _ = qa.shape
    tq = ATTN_BLOCK
    nq = S // tq
    pairs = HEADS // 2
    pw = 2 * HEAD_PAD
    vw = 2 * HEAD_DIM

    def body(qa_ref, ka_ref, v_ref, o_ref, lse_ref):
        row = lax.broadcasted_iota(jnp.int32, (tq, tq), 0)
        col = lax.broadcasted_iota(jnp.int32, (tq, tq), 1)
        causal = row <= col
        for i in range(nq):
            nk = (i + 1) * tq
            rows = slice(i * tq, nk)
            o_t = []
            for h in range(2):
                hs = slice(h * HEAD_PAD, (h + 1) * HEAD_PAD)
                st = _dot_nt(ka_ref[0:nk, hs], qa_ref[rows, hs])
                diag = jnp.where(causal, st[nk - tq:], NEG_INF)
                m = jnp.max(diag, axis=0, keepdims=True)
                if i:
                    m = jnp.maximum(m, jnp.max(st[:nk - tq], axis=0, keepdims=True))
                p_diag = jnp.exp(diag - m)
                l = jnp.sum(p_diag, axis=0, keepdims=True)
                if i:
                    p_top = jnp.exp(st[:nk - tq] - m)
                    l = l + jnp.sum(p_top, axis=0, keepdims=True)
                    p = jnp.concatenate([p_top.astype(BF16), p_diag.astype(BF16)], axis=0)
                else:
                    p = p_diag.astype(BF16)
                acc = _dot_tn(v_ref[0:nk, :], p)
                o_t.append(acc[h * HEAD_DIM:(h + 1) * HEAD_DIM, :] / l)
                lse_ref[h:h + 1, rows] = m + jnp.log(l)
            o_ref[rows, :] = jnp.concatenate(o_t, axis=0).T.astype(BF16)

    return _pcall(
        body, hosted, name="attn_fwd", grid=(Bl, pairs),
        in_specs=[pl.BlockSpec((None, S, pw), lambda b, p: (b, 0, p)),
                  pl.BlockSpec((None, S, pw), lambda b, p: (b, 0, p)),
                  pl.BlockSpec((None, S, vw), lambda b, p: (b, 0, lay["v"] // vw + p))],
        out_specs=[pl.BlockSpec((None, S, vw), lambda b, p: (b, 0, p)),
                   pl.BlockSpec((None, None, 2, S), lambda b, p: (b, p, 0, 0))],
        out_shape=[SDS((Bl, S, BRANCH_W), BF16), SDS((Bl, pairs, 2, S), F32)],
        semantics=("arbitrary", "arbitrary"),
    )(qa, ka, proj3)


def attn_bwd(qa, ka, proj3, dao, ao, lse, dproj3, lay, hosted=None):
    Bl, S, _ = qa.shape
    tk = ATTN_BLOCK
    nq = S // tk
    pairs = HEADS // 2
    pw = 2 * HEAD_PAD
    vw = 2 * HEAD_DIM

    def body(qa_ref, ka_ref, v_ref, do_ref, o_ref, lse_ref, _, dqa_ref, dka_ref, dv_ref):
        row = lax.broadcasted_iota(jnp.int32, (tk, tk), 0)
        col = lax.broadcasted_iota(jnp.int32, (tk, tk), 1)
        causal = row <= col
        lane8 = lax.broadcasted_iota(jnp.int32, (8, vw), 1)
        lane_s = lax.broadcasted_iota(jnp.int32, (S, vw), 1)
        lane_k = lax.broadcasted_iota(jnp.int32, (tk, vw), 1)
        doo = do_ref[...].astype(F32) * o_ref[...].astype(F32)
        hi = doo.astype(BF16)
        lo = (doo - hi.astype(F32)).astype(BF16)
        delta, v_head = [], []
        for h in range(2):
            sel = jnp.where((lane8 >= h * HEAD_DIM) & (lane8 < (h + 1) * HEAD_DIM), 1.0, 0.0).astype(BF16)
            delta.append((_dot_nt(sel, hi) + _dot_nt(sel, lo))[0:1, :])
            in_head = (lane_s >= h * HEAD_DIM) & (lane_s < (h + 1) * HEAD_DIM)
            v_head.append(jnp.where(in_head, v_ref[...], jnp.zeros_like(v_ref[...])))
        dqa_ref[...] = jnp.zeros_like(dqa_ref)
        for j in range(nq):
            q0 = j * tk
            krows = slice(q0, q0 + tk)
            do = do_ref[q0:, :]
            dvs = []
            for h in range(2):
                hs = slice(h * HEAD_PAD, (h + 1) * HEAD_PAD)
                k = ka_ref[krows, hs]
                q = qa_ref[q0:, hs]
                st = _dot_nt(k, q)
                p = jnp.exp(st - lse_ref[h:h + 1, q0:])
                p_diag = jnp.where(causal, p[:, :tk], 0.0)
                p = jnp.concatenate([p_diag, p[:, tk:]], axis=1) if j < nq - 1 else p_diag
                dvs.append(_dot(p.astype(BF16), do))
                dpt = _dot_nt(v_head[h][krows, :], do)
                ds = (p * (dpt - delta[h][:, q0:])).astype(BF16)
                dka_ref[krows, hs] = _dot(ds, q)
                dqa_ref[q0:, hs] += _dot_tn(ds, k)
            dv_ref[krows, :] = jnp.where(lane_k < HEAD_DIM, dvs[0], dvs[1]).astype(BF16)

    seq = lambda w, c0=0: pl.BlockSpec((None, S, w), lambda b, p: (b, 0, c0 + p))
    return _pcall(
        body, hosted, name="attn_bwd", grid=(Bl, pairs),
        in_specs=[seq(pw), seq(pw), seq(vw, lay["v"] // vw), seq(vw), seq(vw),
                  pl.BlockSpec((None, None, 2, S), lambda b, p: (b, p, 0, 0)), _ANY],
        out_specs=[seq(pw), seq(pw), seq(vw, lay["v"] // vw)],
        out_shape=[SDS((Bl, S, HEADS * HEAD_PAD), F32), SDS((Bl, S, HEADS * HEAD_PAD), F32),
                   SDS(dproj3.shape, BF16)],
        aliases={6: 2}, semantics=("arbitrary", "arbitrary"),
    )(qa, ka, proj3, dao, ao, lse, dproj3)


def attn_post(dqa, dka, proj3, bf_rows, layer, dproj3, cst, lay):
    Bl, S, w = dqa.shape
    ts = ATTN_BLOCK
    ns = S // ts
    qkf = 2 * BRANCH_W + F_PAD

    def body(dqa_ref, dka_ref, f_ref, bf_ref, pqkt_ref, eq_ref, ek_ref, _, dqkf_ref, dbf_ref, carry_ref):
        b, s = pl.program_id(0), pl.program_id(1)

        @pl.when(s == 0)
        def _():
            carry_ref[...] = jnp.zeros_like(carry_ref)

        dqa_v, dka_v = dqa_ref[...], dka_ref[...]
        qh = dqa_v.astype(BF16)
        kh = dka_v.astype(BF16)
        dqkf_ref[:, :BRANCH_W] = _dot(qh, pqkt_ref[:w, :]).astype(BF16)
        dqkf_ref[:, BRANCH_W:2 * BRANCH_W] = _dot(kh, pqkt_ref[w:, :]).astype(BF16)
        ql = (dqa_v - qh.astype(F32)).astype(BF16)
        kl = (dka_v - kh.astype(F32)).astype(BF16)
        d_f = (_dot(qh, eq_ref[...]) + _dot(ql, eq_ref[...])) + (_dot(kh, ek_ref[...]) + _dot(kl, ek_ref[...]))
        r = lax.broadcasted_iota(jnp.int32, (ts, ts), 0)
        c = lax.broadcasted_iota(jnp.int32, (ts, ts), 1)
        triu = jnp.where(c >= r, 1.0, 0.0).astype(BF16)
        rev = carry_ref[...]
        for part in _split3(d_f):
            rev = rev + _dot(triu, part)
        carry_ref[...] = rev[0:1, :]
        z = f_ref[...].astype(F32) + bf_ref[...]
        lane = lax.broadcasted_iota(jnp.int32, (ts, LANES), 1)
        dfl = jnp.where(lane < HEADS, rev / (1.0 + jnp.exp(z)), 0.0)
        dqkf_ref[:, 2 * BRANCH_W:] = jnp.concatenate(
            [dfl.astype(BF16), jnp.zeros((ts, F_PAD - LANES), BF16)], axis=1)
        part = jnp.sum(dfl, axis=0, keepdims=True)

        @pl.when((b == 0) & (s == 0))
        def _():
            dbf_ref[...] = part

        @pl.when((b > 0) | (s > 0))
        def _():
            dbf_ref[...] += part

    assert lay["q"] % qkf == 0
    cfull = lambda shape: pl.BlockSpec(shape, lambda b, s: (0,) * len(shape))
    rev_blk = lambda wd, c0=0: pl.BlockSpec((None, ts, wd), lambda b, s: (b, ns - 1 - s, c0))
    return pl.pallas_call(
        body, name="attn_post", grid=(Bl, ns),
        in_specs=[rev_blk(w), rev_blk(w), rev_blk(LANES, lay["f"] // LANES),
                  pl.BlockSpec((None, 1, LANES), lambda b, s: (layer, 0, 0)),
                  cfull((2 * w, BRANCH_W)), cfull((w, LANES)), cfull((w, LANES)), _ANY],
        out_specs=[rev_blk(qkf, lay["q"] // qkf), cfull((1, LANES))],
        out_shape=[SDS(dproj3.shape, BF16), SDS((1, LANES), F32)],
        scratch_shapes=[pltpu.VMEM((1, LANES), F32)],
        input_output_aliases={7: 0},
        compiler_params=_params("arbitrary", "arbitrary"),
    )(dqa, dka, proj3, bf_rows, cst["pqkt"], cst["eq"], cst["ek"], dproj3)


def _shift_down(x, k, row):
    return jnp.where(row >= k, pltpu.roll(x, k, axis=0), 0.0)


def _shift_up(x, k, row):
    n = x.shape[0]
    return jnp.where(row < n - k, pltpu.roll(x, n - k, axis=0), 0.0)


def _window_sum(x, g, row, shift):
    s2 = x + shift(x, 1, row)
    s4 = s2 + shift(s2, 2, row)
    s8 = s4 + shift(s4, 4, row)
    s16 = s8 + shift(s8, 8, row)
    return jnp.where(g == 0, s2, jnp.where(g == 1, s4, jnp.where(g == 2, s8, s16)))


def _window_count(g, row):
    wnd = jnp.where(g == 0, 2, jnp.where(g == 1, 4, jnp.where(g == 2, 8, 16)))
    return jnp.minimum(row + 1, wnd).astype(F32)


def _group_columns(ref):
    return [ref[:, n * GROUP_W:(n + 1) * GROUP_W].astype(F32) for n in range(4)]


def poolconv_fwd(proj3, pool_w, pool_scale, conv_w, layer, lay, hosted=None):
    Bl, S, _ = proj3.shape

    def body(x_ref, pw_ref, ps_ref, cw_ref, po_ref, co_ref):
        g = pl.program_id(1)
        row = lax.broadcasted_iota(jnp.int32, (S, GROUP_W), 0)
        u, cv, cb, cc = _group_columns(x_ref)
        d = _window_sum(u, g, row, _shift_down) / _window_count(g, row) - u
        po_ref[...] = (_dot(d.astype(BF16), pw_ref[...]) * ps_ref[...]).astype(BF16)
        z = cc * cv
        y = cw_ref[0:1, :] * _shift_down(z, 2, row) + cw_ref[1:2, :] * _shift_down(z, 1, row) + cw_ref[2:3, :] * z
        co_ref[...] = (cb * y).astype(BF16)

    out = pl.BlockSpec((None, S, GROUP_W), lambda b, g: (b, 0, g))
    return _pcall(
        body, hosted, name="poolconv_fwd", grid=(Bl, N_GROUPS),
        in_specs=[pl.BlockSpec((None, S, BRANCH_W), lambda b, g: (b, 0, lay["pc"] // BRANCH_W + g)),
                  pl.BlockSpec((None, None, GROUP_W, GROUP_W), lambda b, g: (layer, g, 0, 0)),
                  pl.BlockSpec((None, 1, GROUP_W), lambda b, g: (layer, 0, g)),
                  pl.BlockSpec((None, None, 3, GROUP_W), lambda b, g: (g, layer, 0, 0))],
        out_specs=[out, out],
        out_shape=[SDS((Bl, S, BRANCH_W), BF16), SDS((Bl, S, BRANCH_W), BF16)],
        semantics=("arbitrary", "arbitrary"),
    )(proj3, pool_w, pool_scale, conv_w)


def poolconv_bwd(proj3, dpo, dco, pool_w, pool_scale, conv_w, layer, dproj3, lay):
    Bl, S, _ = proj3.shape

    def body(x_ref, dpo_ref, dco_ref, pw_ref, ps_ref, cw_ref, _, dx_ref, dpw_ref, dps_ref, dcw_ref):
        g, b = pl.program_id(0), pl.program_id(1)
        row = lax.broadcasted_iota(jnp.int32, (S, GROUP_W), 0)
        cnt = _window_count(g, row)
        u, cv, cb, cc = _group_columns(x_ref)
        d = (_window_sum(u, g, row, _shift_down) / cnt - u).astype(BF16)
        pw = pw_ref[...]
        ypre = _dot(d, pw)
        dpo_v = dpo_ref[...].astype(F32)
        dps = jnp.sum(dpo_v * ypre, axis=0, keepdims=True)
        dyp = (dpo_v * ps_ref[...]).astype(BF16)
        dpw = _dot_tn(d, dyp)
        dd = _dot_nt(dyp, pw)
        dx_ref[:, 0:GROUP_W] = (_window_sum(dd / cnt, g, row, _shift_up) - dd).astype(BF16)

        z = cc * cv
        z1, z2 = _shift_down(z, 1, row), _shift_down(z, 2, row)
        w0, w1, w2 = cw_ref[0:1, :], cw_ref[1:2, :], cw_ref[2:3, :]
        y = w0 * z2 + w1 * z1 + w2 * z
        dco_v = dco_ref[...].astype(F32)
        dy = dco_v * cb
        dz = w0 * _shift_up(dy, 2, row) + w1 * _shift_up(dy, 1, row) + w2 * dy
        dx_ref[:, GROUP_W:2 * GROUP_W] = (dz * cc).astype(BF16)
        dx_ref[:, 2 * GROUP_W:3 * GROUP_W] = (dco_v * y).astype(BF16)
        dx_ref[:, 3 * GROUP_W:] = (dz * cv).astype(BF16)
        dcw = jnp.concatenate([jnp.sum(dy * z2, axis=0, keepdims=True),
                               jnp.sum(dy * z1, axis=0, keepdims=True),
                               jnp.sum(dy * z, axis=0, keepdims=True)], axis=0)

        @pl.when(b == 0)
        def _():
            dpw_ref[...] = dpw
            dps_ref[...] = dps
            dcw_ref[...] = dcw

        @pl.when(b > 0)
        def _():
            dpw_ref[...] += dpw
            dps_ref[...] += dps
            dcw_ref[...] += dcw

    blk = pl.BlockSpec((None, S, GROUP_W), lambda g, b: (b, 0, g))
    pc = pl.BlockSpec((None, S, BRANCH_W), lambda g, b: (b, 0, lay["pc"] // BRANCH_W + g))
    return pl.pallas_call(
        body, name="poolconv_bwd", grid=(N_GROUPS, Bl),
        in_specs=[pc, blk, blk,
                  pl.BlockSpec((None, None, GROUP_W, GROUP_W), lambda g, b: (layer, g, 0, 0)),
                  pl.BlockSpec((None, 1, GROUP_W), lambda g, b: (layer, 0, g)),
                  pl.BlockSpec((None, None, 3, GROUP_W), lambda g, b: (g, layer, 0, 0)), _ANY],
        out_specs=[pc, pl.BlockSpec((None, GROUP_W, GROUP_W), lambda g, b: (g, 0, 0)),
                   pl.BlockSpec((1, GROUP_W), lambda g, b: (0, g)),
                   pl.BlockSpec((None, 3, GROUP_W), lambda g, b: (g, 0, 0))],
        out_shape=[SDS(dproj3.shape, BF16), SDS((N_GROUPS, GROUP_W, GROUP_W), F32), SDS((1, BRANCH_W), F32),
                   SDS((N_GROUPS, 3, GROUP_W), F32)],
        input_output_aliases={6: 0},
        compiler_params=_params("arbitrary", "arbitrary"),
    )(proj3, dpo, dco, pool_w, pool_scale, conv_w, dproj3)


def _tile_2d(rows, cols, n_arrays):
    budget = VMEM_LIMIT // 2
    lanes = -(-cols // LANES) * LANES
    if rows % 8 == 0:
        for t in (2048, 1024, 512, 256, 128, 64, 32, 16, 8):
            if rows % t == 0 and 2 * n_arrays * t * lanes * 4 <= budget:
                return t, cols
    for t in (1024, 512, 256, 128):
        if cols % t == 0 and 2 * n_arrays * (rows + 8) * t * 4 <= budget:
            return rows, t
    return rows, cols


def add_pair(kept, layer, where, received, name):
    _, n, _, R, C = kept.shape
    tr, tc = _tile_2d(R, C, 3)

    def body(where_ref, a_ref, b_ref, o_ref):
        o_ref[...] = (a_ref[...].astype(F32) + b_ref[...].astype(F32)).astype(BF16)

    blk = pl.BlockSpec((None, tr, tc), lambda d, i, j, where_ref: (d, i, j))
    grid_spec = pltpu.PrefetchScalarGridSpec(
        num_scalar_prefetch=1, grid=(n, R // tr, C // tc),
        in_specs=[pl.BlockSpec((None, None, None, tr, tc),
                               lambda d, i, j, where_ref: (layer, d, where_ref[0], i, j)), blk],
        out_specs=blk)
    return pl.pallas_call(body, name=name, grid_spec=grid_spec, out_shape=SDS((n, R, C), BF16),
                          compiler_params=_params("arbitrary", "arbitrary", "arbitrary"))(where, kept, received)


def add_chips(arrived, own, layer, where, n_layers, prev, name):
    _, R, C = arrived.shape
    tr, tc = _tile_2d(R, C, 6)

    def body(where_ref, a0, a1, a2, a3, own_ref, *rest):
        o_ref = rest[-1]
        chip = where_ref[1]
        acc = None
        for j, a_ref in enumerate((a0, a1, a2, a3)):
            term = jnp.where(chip == j, own_ref[...], a_ref[...]).astype(F32)
            acc = term if acc is None else acc + term
        o_ref[...] = acc

    def slot(j):
        return pl.BlockSpec((None, tr, tc), lambda i, k, where_ref, j=j: (
            jnp.where(where_ref[1] == j, (j + 1) % N_CHIPS, j), i, k))

    in_specs = [slot(j) for j in range(N_CHIPS)] + [
        pl.BlockSpec((None, tr, tc), lambda i, k, where_ref: (where_ref[1], i, k))]
    args = [where, arrived, arrived, arrived, arrived, own]
    aliases = {}
    if prev is not None:
        in_specs.append(_ANY)
        args.append(prev)
        aliases = {len(args) - 1: 0}
    grid_spec = pltpu.PrefetchScalarGridSpec(
        num_scalar_prefetch=1, grid=(R // tr, C // tc), in_specs=in_specs,
        out_specs=pl.BlockSpec((None, None, tr, tc), lambda i, k, where_ref: (layer, where_ref[0], i, k)))
    return pl.pallas_call(body, name=name, grid_spec=grid_spec, out_shape=SDS((n_layers, 2, R, C), F32),
                          input_output_aliases=aliases,
                          compiler_params=_params("arbitrary", "arbitrary"))(*args)


def adamw(w, g, m, v, name):
    if w.ndim == 2:
        R, C = w.shape
        tr, _ = _tile_2d(R, C, 7)
        grid, blk = (R // tr,), pl.BlockSpec((tr, C), lambda i: (i, 0))
    else:
        N, r, C = w.shape
        tn = max(t for t in range(1, N + 1) if N % t == 0 and t * r * C * 4 <= 512 * 1024)
        grid, blk = (N // tn,), pl.BlockSpec((tn, r, C), lambda i: (i, 0, 0))

    def body(w_ref, g_ref, m_ref, v_ref, d_ref, nm_ref, nv_ref):
        gv = g_ref[...]
        m_new = ADAM_B1 * m_ref[...] + (1.0 - ADAM_B1) * gv
        v_new = ADAM_B2 * v_ref[...] + (1.0 - ADAM_B2) * (gv * gv)
        m_hat = m_new / (1.0 - ADAM_B1 ** ADAM_STEP)
        v_hat = v_new / (1.0 - ADAM_B2 ** ADAM_STEP)
        d_ref[...] = -ADAM_LR * (m_hat / (jnp.sqrt(v_hat) + ADAM_EPS) + ADAM_WD * w_ref[...])
        nm_ref[...] = m_new
        nv_ref[...] = v_new

    out = SDS(w.shape, F32)
    return pl.pallas_call(body, name=name, grid=grid, in_specs=[blk] * 4, out_specs=[blk] * 3,
                          out_shape=[out, out, out], compiler_params=_params("arbitrary"))(w, g, m, v)


_COMM = pltpu.CompilerParams(has_side_effects=True)


def gather_buffers(shards):
    me_chip = 2 * lax.axis_index("x") + lax.axis_index("y")
    pool = {}
    for name, sh in shards.items():
        L, r, c = sh.shape
        if name in ROW_SHARDED:
            pool[name] = lax.dynamic_update_slice(lax.empty((L, N_CHIPS, r, c), sh.dtype), sh[:, None],
                                                  (0, me_chip, 0, 0))
        else:
            pool[name] = lax.dynamic_update_slice(lax.empty((N_CHIPS, L, r, c), sh.dtype), sh[None],
                                                  (me_chip, 0, 0, 0))
    return pool


def comm_now(pool, stages, name):
    stages = [Hosted(pool, jobs) for jobs in stages]
    names = sorted({m for st in stages for m in st.names})
    n = len(names)

    def body(*refs):
        bufs = dict(zip(names, refs[n:2 * n]))
        sems = refs[2 * n:]
        for i, st in enumerate(stages):
            plan = _hosted_plan(st, bufs, sems[2 * i], sems[2 * i + 1])
            _hosted_start(plan, True)
            _hosted_finish(plan, True)

    sem = pltpu.SemaphoreType.DMA
    scratch = []
    for st in stages:
        scratch += [sem((len(st.jobs), 3)), sem((len(st.jobs), 3))]
    res = pl.pallas_call(
        body, name=name, in_specs=[_ANY] * n, out_specs=[_ANY] * n,
        out_shape=[SDS(pool[m].shape, pool[m].dtype) for m in names],
        scratch_shapes=scratch, input_output_aliases={t: t for t in range(n)},
        compiler_params=_COMM,
    )(*[pool[m] for m in names])
    pool.update(zip(names, res))


def gather_now(pool, units):
    comm_now(pool, [[("ici", name, layer) for name, layer in units],
                    [("fwd", name, layer) for name, layer in units]], "gather_now")


def allgather_chips(buf, name):
    def body(src_ref, out_ref, send_sems, recv_sems, local_sem):
        x, y, c = _position()
        me = 2 * x + y
        mine = pltpu.make_async_copy(src_ref, out_ref.at[me], local_sem)
        mine.start()
        sends = []
        for k, (px, py) in enumerate(_other_chips(x, y)):
            cp = _remote(src_ref, out_ref.at[me], send_sems.at[k], recv_sems.at[k], (px, py, c))
            cp.start()
            sends.append(cp)
        for k, (px, py) in enumerate(_other_chips(x, y)):
            _remote(src_ref, out_ref.at[2 * px + py], send_sems.at[k], recv_sems.at[k], (px, py, c)).wait_recv()
        for cp in sends:
            cp.wait_send()
        mine.wait()

    sem = pltpu.SemaphoreType.DMA
    return pl.pallas_call(
        body, name=name, in_specs=[_ANY], out_specs=_ANY, out_shape=SDS((N_CHIPS,) + buf.shape, buf.dtype),
        scratch_shapes=[sem((3,)), sem((3,)), sem], compiler_params=_COMM,
    )(buf)


def swap_sibling(tensors, name):
    n = len(tensors)

    def body(*refs):
        srcs, outs, send_sems, recv_sems = refs[:n], refs[n:2 * n], refs[2 * n], refs[2 * n + 1]
        x, y, c = _position()
        cps = [_remote(srcs[t].at[1 - c], outs[t], send_sems.at[t], recv_sems.at[t], (x, y, 1 - c))
               for t in range(n)]
        for cp in cps:
            cp.start()
        for cp in cps:
            cp.wait()

    sem = pltpu.SemaphoreType.DMA
    return pl.pallas_call(
        body, name=name, in_specs=[_ANY] * n, out_specs=[_ANY] * n,
        out_shape=[SDS(t.shape[1:], t.dtype) for t in tensors],
        scratch_shapes=[sem((n,)), sem((n,))], compiler_params=_COMM,
    )(*tensors)


def exchange_chips(tensors, name):
    n = len(tensors)

    def body(*refs):
        srcs, outs = refs[:n], refs[n:2 * n]
        send_sems, recv_sems, local_sems = refs[2 * n:]
        x, y, c = _position()
        me = 2 * x + y
        others = _other_chips(x, y)
        cps = []
        for t in range(n):
            cp = pltpu.make_async_copy(srcs[t].at[me], outs[t].at[me], local_sems.at[t])
            cp.start()
            cps.append(cp)
        sends = []
        for t in range(n):
            for k, (px, py) in enumerate(others):
                cp = _remote(srcs[t].at[2 * px + py], outs[t].at[me], send_sems.at[t, k], recv_sems.at[t, k],
                             (px, py, c))
                cp.start()
                sends.append(cp)
        for t in range(n):
            for k, (px, py) in enumerate(others):
                _remote(srcs[t].at[me], outs[t].at[2 * px + py], send_sems.at[t, k], recv_sems.at[t, k],
                        (px, py, c)).wait_recv()
        for cp in sends:
            cp.wait_send()
        for cp in cps:
            cp.wait()

    sem = pltpu.SemaphoreType.DMA
    return pl.pallas_call(
        body, name=name, in_specs=[_ANY] * n, out_specs=[_ANY] * n,
        out_shape=[SDS(t.shape, t.dtype) for t in tensors],
        scratch_shapes=[sem((n, 3)), sem((n, 3)), sem((n,))], compiler_params=_COMM,
    )(*tensors)


def join_halves(tensors, name):
    n = len(tensors)

    def body(*refs):
        outs, send_sems, recv_sems = refs[n:2 * n], refs[2 * n], refs[2 * n + 1]
        x, y, c = _position()
        sib = (x, y, 1 - c)
        sends = []
        for t in range(n):
            cp = _remote(outs[t].at[c], outs[t].at[c], send_sems.at[t], recv_sems.at[t], sib)
            cp.start()
            sends.append(cp)
        for t in range(n):
            _remote(outs[t].at[c], outs[t].at[1 - c], send_sems.at[t], recv_sems.at[t], sib).wait_recv()
        for cp in sends:
            cp.wait_send()

    sem = pltpu.SemaphoreType.DMA
    return pl.pallas_call(
        body, name=name, in_specs=[_ANY] * n, out_specs=[_ANY] * n,
        out_shape=[SDS(t.shape, t.dtype) for t in tensors],
        scratch_shapes=[sem((n,)), sem((n,))], input_output_aliases={t: t for t in range(n)},
        compiler_params=_COMM,
    )(*tensors)


BIG = ("w_in", "w_proj_attn", "w_proj_pool", "w_proj_conv", "conv_w", "w_out", "w_gate_up", "w_down")
REPLICATED = ("attn_norm", "b_forget", "b_gate", "pool_w", "pool_scale", "ffn_norm", "final_norm")
ORDER = ("attn_norm", "w_in", "b_forget", "b_gate", "w_proj_attn", "pool_w", "pool_scale", "w_proj_pool",
         "conv_w", "w_proj_conv", "w_out", "ffn_norm", "w_gate_up", "w_down", "final_norm")


def _proj_layout(D):
    lay = {"g": 0, "q": 3 * D}
    lay["k"] = lay["q"] + BRANCH_W
    lay["f"] = lay["k"] + BRANCH_W
    lay["v"] = lay["f"] + F_PAD
    lay["pc"] = lay["v"] + BRANCH_W
    lay["width"] = lay["pc"] + 4 * BRANCH_W
    return lay


_REF = dict(q=0, k=512, v=1024, f=1536, u=1544, cv=2056, cb=2568, cc=3080, g=3592)


def _packed_pieces(D):
    pieces = [(_REF["g"], 3 * D), (_REF["q"], BRANCH_W), (_REF["k"], BRANCH_W), (_REF["f"], HEADS),
              (None, F_PAD - HEADS), (_REF["v"], BRANCH_W)]
    for gi in range(N_GROUPS):
        pieces += [(_REF[name] + gi * GROUP_W, GROUP_W) for name in ("u", "cv", "cb", "cc")]
    return pieces


def _pack_w_in_rows(shards, layer):
    _, _, cs, D = shards.shape
    parts = []
    for start, n in _packed_pieces(D):
        if start is None:
            parts.append(jnp.zeros((n, D), shards.dtype))
        while start is not None and n:
            chip, off = divmod(start, cs)
            take = min(n, cs - off)
            parts.append(shards[chip, layer, off:off + take, :])
            start, n = start + take, n - take
    return jnp.concatenate(parts, axis=0)


def _unpack_w_in_rows(p, D):
    lay = _proj_layout(D)
    rows = lambda a, n: p[a:a + n, :]
    kinds = []
    for kind in range(4):
        kinds += [rows(lay["pc"] + gi * BRANCH_W + kind * GROUP_W, GROUP_W) for gi in range(N_GROUPS)]
    return jnp.concatenate([rows(lay["q"], BRANCH_W), rows(lay["k"], BRANCH_W), rows(lay["v"], BRANCH_W),
                            rows(lay["f"], HEADS)] + kinds + [rows(0, 3 * D)], axis=0)


def _split_flat(vec, shapes):
    out, at = [], 0
    for shp in shapes:
        n = int(np.prod(shp))
        out.append(vec[at:at + n].reshape(shp))
        at += n
    return out


def kernel(x, attn_norm, w_in, b_forget, b_gate, w_proj_attn, pool_w, pool_scale, w_proj_pool, conv_w, w_proj_conv, w_out, ffn_norm, w_gate_up, w_down, final_norm, loss_target, m_attn_norm, m_w_in, m_b_forget, m_b_gate, m_w_proj_attn, m_pool_w, m_pool_scale, m_w_proj_pool, m_conv_w, m_w_proj_conv, m_w_out, m_ffn_norm, m_w_gate_up, m_w_down, m_final_norm, v_attn_norm, v_w_in, v_b_forget, v_b_gate, v_w_proj_attn, v_pool_w, v_pool_scale, v_w_proj_pool, v_conv_w, v_w_proj_conv, v_w_out, v_ffn_norm, v_w_gate_up, v_w_down, v_final_norm):
    weights = dict(attn_norm=attn_norm, w_in=w_in, b_forget=b_forget, b_gate=b_gate, w_proj_attn=w_proj_attn,
                   pool_w=pool_w, pool_scale=pool_scale, w_proj_pool=w_proj_pool, conv_w=conv_w,
                   w_proj_conv=w_proj_conv, w_out=w_out, ffn_norm=ffn_norm, w_gate_up=w_gate_up, w_down=w_down,
                   final_norm=final_norm)
    mom_m = dict(attn_norm=m_attn_norm, w_in=m_w_in, b_forget=m_b_forget, b_gate=m_b_gate, w_proj_attn=m_w_proj_attn,
                 pool_w=m_pool_w, pool_scale=m_pool_scale, w_proj_pool=m_w_proj_pool, conv_w=m_conv_w,
                 w_proj_conv=m_w_proj_conv, w_out=m_w_out, ffn_norm=m_ffn_norm, w_gate_up=m_w_gate_up,
                 w_down=m_w_down, final_norm=m_final_norm)
    mom_v = dict(attn_norm=v_attn_norm, w_in=v_w_in, b_forget=v_b_forget, b_gate=v_b_gate, w_proj_attn=v_w_proj_attn,
                 pool_w=v_pool_w, pool_scale=v_pool_scale, w_proj_pool=v_w_proj_pool, conv_w=v_conv_w,
                 w_proj_conv=v_w_proj_conv, w_out=v_w_out, ffn_norm=v_ffn_norm, w_gate_up=v_w_gate_up,
                 w_down=v_w_down, final_norm=v_final_norm)

    Bl, S, D = x.shape
    T = Bl * S
    L = w_in.shape[0]
    F = w_down.shape[1] * N_CHIPS
    lay = _proj_layout(D)
    cst = _placement_constants()
    assert L == N_LAYERS and S % ATTN_BLOCK == 0 and F % (2 * LANES) == 0 and D % BRANCH_W == 0
    assert w_in.shape[2] * N_CHIPS == _REF["g"] + 3 * D and conv_w.shape[2] == GROUP_W

    send = {n: weights[n].astype(BF16) for n in BIG}
    send["conv_w"] = conv_w
    me_chip = 2 * lax.axis_index("x") + lax.axis_index("y")
    send["w_in"] = w_in.transpose(0, 2, 1).astype(BF16)
    pool = gather_buffers(send)
    gather_now(pool, [("w_in", 0)])
    rest = ("w_out", "w_proj_attn", "w_proj_pool", "w_gate_up", "w_proj_conv", "conv_w")
    late = ("w_out", "w_proj_attn", "w_proj_pool", "w_proj_conv", "conv_w")
    jobs = lambda kind, names, layer: [(kind, n, layer) for n in names]
    carried = {
        ("in_proj", 0): jobs("ici", rest, 0),
        ("attn_prep", 0): jobs("fwd", rest, 0),
        ("attn_fwd", 0): jobs("ici", ("w_in",), 1) + jobs("ici", ("w_down",), 0),
        ("poolconv_fwd", 0): jobs("fwd", ("w_in",), 1) + jobs("fwd", ("w_down",), 0),
        ("mix_fwd", 0): jobs("ici", ("w_gate_up",), 1),
        ("gate_up_proj", 0): jobs("ici", ("w_down",) + late, 1),
        ("ffn_down_fwd", 0): jobs("fwd", ("w_gate_up",), 1),
        ("in_proj", 1): jobs("fwd", ("w_down",) + late, 1),
    }
    carry = lambda call, layer: Hosted(pool, carried[call, layer]) if (call, layer) in carried else None
    w_down_f = lambda: pool["w_down"].reshape(L, F, D)
    pool_w_b = pool_w.astype(BF16)
    an3, fn3 = attn_norm.reshape(L, 1, D), ffn_norm.reshape(L, 1, D)
    bg3, ps3 = b_gate.reshape(L, 1, 3 * D), pool_scale.reshape(L, 1, BRANCH_W)
    bf3 = jnp.pad(b_forget, ((0, 0), (0, LANES - HEADS))).reshape(L, 1, LANES)

    xs = x.reshape(T, D)
    saved = []
    w_in_p = []
    for l in range(L):
        w_in_p.append(_pack_w_in_rows(pool["w_in"], l))
        proj, h = norm_matmul(xs, an3, w_in_p[l], l, "rows", "in_proj", carry("in_proj", l))
        proj3 = proj.reshape(Bl, S, lay["width"])
        qa, ka = attn_prep(proj3, bf3, l, cst, lay, carry("attn_prep", l))
        ao, lse = attn_fwd(qa, ka, proj3, lay, carry("attn_fwd", l))
        po, co = poolconv_fwd(proj3, pool_w_b, ps3, pool["conv_w"], l, lay, carry("poolconv_fwd", l))
        ao2, po2, co2 = (a.reshape(T, BRANCH_W) for a in (ao, po, co))
        x1, ys, mixed = mix_fwd(ao2, po2, co2, proj, bg3, pool["w_proj_attn"], pool["w_proj_pool"],
                                pool["w_proj_conv"], pool["w_out"], l, xs, carry("mix_fwd", l))
        ab, h2 = norm_matmul(x1, fn3, pool["w_gate_up"], l, "by_shard", "gate_up_proj", carry("gate_up_proj", l))
        x2, s_act = ffn_down_fwd(ab, w_down_f(), l, x1, carry("ffn_down_fwd", l))
        saved.append(dict(x=xs, proj=proj, proj3=proj3, h=h, qa=qa, ka=ka, ao=ao, lse=lse, ao2=ao2, po2=po2,
                          co2=co2, ys=ys, mixed=mixed, x1=x1, ab=ab, h2=h2, s=s_act))
        xs = x2
    w_gu, w_o, conv_w_g = pool["w_gate_up"], pool["w_out"], pool["conv_w"]
    wpa, wpp, wpc = pool["w_proj_attn"], pool["w_proj_pool"], pool["w_proj_conv"]
    w_down_f = w_down_f()

    loss_row, dx, dxb, g_final = loss_head(xs, final_norm.reshape(1, D), loss_target.reshape(T, D))
    loss = lax.psum(loss_row[0, 0], AXES)

    reduced_names = tuple(n for n in BIG if n != "conv_w")
    early_names = tuple(n for n in reduced_names if n != "w_in")
    where = jnp.stack([lax.axis_index("c"), me_chip]).astype(jnp.int32)
    rs = {}

    def reduce_begin(layer, grads):
        for n, g in grads.items():
            g5 = g.reshape((1, N_CHIPS, 2, -1) + g.shape[-1:])
            rs["g%d:%s" % (layer, n)] = g5
            for role in "ra":
                rs["%s%d:%s" % (role, layer, n)] = lax.empty((N_CHIPS,) + g5.shape[3:], BF16)

    swap_jobs = lambda layer, names: [("swap", "g%d:%s" % (layer, n), "r%d:%s" % (layer, n), 0) for n in names]
    xchg_jobs = lambda layer, names: [("xchg", "s%d:%s" % (layer, n), "a%d:%s" % (layer, n)) for n in names]
    join_jobs = lambda layer, names: [("join", "o:" + n, layer) for n in names]

    def pair_sums(layer, names):
        for n in names:
            rs["s%d:%s" % (layer, n)] = add_pair(rs["g%d:%s" % (layer, n)], 0, where, rs["r%d:%s" % (layer, n)],
                                                 "add_pair_" + n)

    def chip_sums(layer, names, slot, n_slots):
        for n in names:
            rs["o:" + n] = add_chips(rs["a%d:%s" % (layer, n)], rs["s%d:%s" % (layer, n)], slot, where, n_slots,
                                     rs.get("o:" + n), "add_chips_" + n)

    small = {n: [None] * L for n in REPLICATED if n != "final_norm"}
    g_conv = [None] * L
    to3 = lambda a: a.reshape(Bl, S, -1)
    for l in reversed(range(L)):
        sv = saved[l]
        behind = (lambda jobs: Hosted(rs, jobs)) if l == 0 else (lambda jobs: None)
        grads = {}
        da, db = ffn_down_bwd(dxb, w_down_f, l, sv["ab"], behind(swap_jobs(1, reduced_names)))
        if l == 0:
            pair_sums(1, reduced_names)
        grads["w_down"] = matmul_tn(sv["s"], [dxb], "grad_w_down", hosted=behind(xchg_jobs(
            1, ("w_down", "w_out", "w_proj_attn", "w_proj_pool", "w_proj_conv"))))
        grads["w_gate_up"] = matmul_tn(sv["h2"], [da, db], "grad_w_gate_up", by_dest=True, tn=2 * F // N_CHIPS,
                                       tk=_tile(T, (1024, 512, 256)), hosted=behind(xchg_jobs(1, ("w_gate_up",))))
        dx1, dx1b, g_fn = matmul_nt_normbwd([da, db], w_gu, l, "by_shard", sv["x1"], fn3, dx, "gate_up_bwd",
                                            behind(xchg_jobs(1, ("w_in",))))
        small["ffn_norm"][l] = g_fn[0]
        if l == 0:
            chip_sums(1, reduced_names, 1, L)
        dys, dproj, dao, dpo, dco, g_bg = mix_bwd(dx1b, w_o, sv["proj"], bg3, sv["ys"], wpa, wpp, wpc, l,
                                                  lay["width"], behind(join_jobs(1, reduced_names)))
        small["b_gate"][l] = g_bg[0]
        grads["w_out"] = matmul_tn(sv["mixed"], [dx1b], "grad_w_out")
        for n, (name, br) in enumerate((("w_proj_attn", sv["ao2"]), ("w_proj_pool", sv["po2"]),
                                        ("w_proj_conv", sv["co2"]))):
            grads[name] = matmul_tn(br, [dys], "grad_" + name, b_col0=n * D, n_cols=D, by_dest=True,
                                    tn=D // N_CHIPS)
        if l == 0:
            reduce_begin(0, grads)
        dqa, dka, dproj3 = attn_bwd(sv["qa"], sv["ka"], sv["proj3"], to3(dao), sv["ao"], sv["lse"], to3(dproj), lay,
                                    behind(swap_jobs(0, early_names)))
        if l == 0:
            pair_sums(0, early_names)
        dproj3, g_bf = attn_post(dqa, dka, sv["proj3"], bf3, l, dproj3, cst, lay)
        small["b_forget"][l] = g_bf[0, :HEADS]
        dproj3, g_pw, g_ps, g_conv[l] = poolconv_bwd(sv["proj3"], to3(dpo), to3(dco), pool_w_b, ps3, conv_w_g, l,
                                                     dproj3, lay)
        small["pool_w"][l], small["pool_scale"][l] = g_pw, g_ps[0]
        dproj = dproj3.reshape(T, lay["width"])
        grads["w_in"] = _unpack_w_in_rows(matmul_tn(dproj, [sv["h"]], "grad_w_in"), D)
        dx, dxb, g_an = matmul_nt_normbwd([dproj], w_in_p[l], l, "rows", sv["x"], an3, dx1, "in_proj_bwd",
                                          behind(xchg_jobs(0, early_names)))
        small["attn_norm"][l] = g_an[0]
        reduce_begin(l, grads if l else {"w_in": grads["w_in"]})
    grad_x = dx.reshape(Bl, S, D)

    small_shapes = [weights[n].shape for n in REPLICATED] + [(L, N_CHIPS) + conv_w.shape[1:]]
    small_vec = jnp.concatenate([jnp.stack(small[n]).reshape(-1) for n in REPLICATED[:-1]]
                                + [g_final[0], jnp.stack(g_conv).reshape(-1)])
    n_small = small_vec.shape[0]
    small_vec = jnp.pad(small_vec, (0, -n_small % (2 * N_CHIPS * 16 * LANES))).astype(BF16)
    rs["g0:small"] = small_vec.reshape(1, N_CHIPS, 2, -1, LANES)
    for role in "ra":
        rs[role + "0:small"] = lax.empty((N_CHIPS,) + rs["g0:small"].shape[3:], BF16)
    last = ("w_in", "small")
    comm_now(rs, [swap_jobs(0, last)], "swap_grad_halves")
    pair_sums(0, last)
    comm_now(rs, [xchg_jobs(0, last)], "exchange_grad_chips")
    chip_sums(0, reduced_names, 0, L)
    chip_sums(0, ("small",), 0, 1)
    comm_now(rs, [join_jobs(0, reduced_names + ("small",))], "join_grad_halves")
    shard_grads = {n: rs["o:" + n].reshape((L, -1) + rs["o:" + n].shape[-1:]) for n in reduced_names}
    small_all = allgather_chips(rs["o:small"].reshape(-1, LANES), "allgather_small_grads").reshape(-1)[:n_small]
    *rep_list, conv_all = _split_flat(small_all, small_shapes)
    rep_grads = dict(zip(REPLICATED, rep_list))
    shard_grads["conv_w"] = lax.dynamic_index_in_dim(conv_all, me_chip, 1, keepdims=False)

    delta, new_m, new_v = {}, {}, {}
    for n in BIG:
        shp = weights[n].shape
        if n == "w_in":
            view, back = (lambda a: a.transpose(2, 0, 1)), (lambda a: a.transpose(1, 2, 0))
            g = shard_grads[n].transpose(1, 0, 2)
        else:
            view, back = (lambda a: a.reshape(-1, shp[-1])), (lambda a: a.reshape(shp))
            g = view(shard_grads[n])
        d, nm, nv = adamw(view(weights[n]), g, view(mom_m[n]), view(mom_v[n]), "adamw_" + n)
        delta[n], new_m[n], new_v[n], shard_grads[n] = back(d), back(nm), back(nv), back(g)

    def rows(d):
        vec = jnp.concatenate([d[n].reshape(-1) for n in REPLICATED])
        return jnp.pad(vec, (0, -vec.shape[0] % (8 * LANES))).reshape(-1, LANES)

    outs = adamw(rows(weights), rows(rep_grads), rows(mom_m), rows(mom_v), "adamw_replicated")
    for res, o in zip((delta, new_m, new_v), outs):
        res.update(zip(REPLICATED, _split_flat(o.reshape(-1), small_shapes[:len(REPLICATED)])))
    all_grads = {**shard_grads, **rep_grads}

    return (loss, grad_x, *[all_grads[n] for n in ORDER], *[delta[n] for n in ORDER],
            *[new_m[n] for n in ORDER], *[new_v[n] for n in ORDER])
```

```python
import numpy as np
import jax
import jax.numpy as jnp
from jax import lax
from jax.experimental import pallas as pl
from jax.experimental.pallas import tpu as pltpu

F32, BF16 = jnp.float32, jnp.bfloat16
SDS = jax.ShapeDtypeStruct
MESH = pl.DeviceIdType.MESH
AXES = ("x", "y", "c")
N_CHIPS = 4
N_LAYERS = 2
LANES = 128
VMEM_LIMIT = 48 * 1024 * 1024

HEADS, HEAD_DIM = 8, 64
HEAD_PAD = 128
BRANCH_W = 512
GROUP_W = 128
N_GROUPS = BRANCH_W // GROUP_W
POOL_WINDOWS = (2, 4, 8, 16)
F_PAD = 512
ATTN_BLOCK = 256
RMS_EPS = 1e-6
NEG_INF = -1e30
ADAM_LR, ADAM_B1, ADAM_B2, ADAM_EPS, ADAM_WD, ADAM_STEP = 0.001, 0.9, 0.999, 1e-08, 0.01, 10

NT = (((1,), (1,)), ((), ()))
TN = (((0,), (0,)), ((), ()))
_ANY = pl.BlockSpec(memory_space=pl.ANY)


def _tile(n, prefs):
    for p in prefs:
        if n % p == 0:
            return p
    raise ValueError(f"no tile of {prefs} divides {n}")


def _params(*sem):
    return pltpu.CompilerParams(dimension_semantics=sem, vmem_limit_bytes=VMEM_LIMIT)


def _sigmoid(z):
    return 0.5 * jnp.tanh(0.5 * z) + 0.5


def _split3(x):
    h1 = x.astype(BF16)
    r1 = x - h1.astype(F32)
    h2 = r1.astype(BF16)
    h3 = (r1 - h2.astype(F32)).astype(BF16)
    return h1, h2, h3


def _position():
    return lax.axis_index("x"), lax.axis_index("y"), lax.axis_index("c")


def _other_chips(x, y):
    return [(1 - x, y), (x, 1 - y), (1 - x, 1 - y)]


def _remote(src, dst, send_sem, recv_sem, device):
    return pltpu.make_async_remote_copy(src_ref=src, dst_ref=dst, send_sem=send_sem, recv_sem=recv_sem,
                                        device_id=device, device_id_type=MESH)


ROW_SHARDED = ("w_out", "w_down")
FETCHER = dict(w_in=0, w_out=0, w_proj_attn=0, w_proj_pool=0, w_gate_up=1, w_down=1, w_proj_conv=1, conv_w=1)


class Hosted:
    def __init__(self, pool, jobs):
        self.pool, self.jobs = pool, list(jobs)
        names = set()
        for job in self.jobs:
            names.update(job[1:3] if job[0] in ("swap", "xchg") else job[1:2])
        self.names = sorted(names)


def _hosted_plan(hosted, refs, send_sems, recv_sems):
    x, y, c = _position()
    me = 2 * x + y
    others = _other_chips(x, y)
    sibling = (x, y, 1 - c)
    plan = []
    for j, job in enumerate(hosted.jobs):
        kind = job[0]
        sems = lambda k, j=j: (send_sems.at[j, k], recv_sems.at[j, k])
        if kind in ("ici", "fwd"):
            _, name, layer = job
            ref = refs[name]
            win = (lambda chip, ref=ref, layer=layer: ref.at[layer, chip]) if name in ROW_SHARDED else (
                lambda chip, ref=ref, layer=layer: ref.at[chip, layer])
            mine = c == FETCHER[name]
            if kind == "ici":
                sends = [_remote(win(me), win(me), *sems(k), (px, py, c)) for k, (px, py) in enumerate(others)]
                arrivals = [_remote(win(2 * px + py), win(2 * px + py), *sems(k), (px, py, c))
                            for k, (px, py) in enumerate(others)]
                plan.append((mine, sends, arrivals, []))
            else:
                sends = [_remote(win(2 * px + py), win(2 * px + py), *sems(k), sibling)
                         for k, (px, py) in enumerate(others)]
                plan.append((mine, sends, [], sends))
        elif kind == "swap":
            _, src, dst, layer = job
            cp = _remote(refs[src].at[layer, :, 1 - c], refs[dst], *sems(0), sibling)
            plan.append((True, [cp], [cp], []))
        elif kind == "xchg":
            _, src, dst = job
            sends = [_remote(refs[src].at[2 * px + py], refs[dst].at[me], *sems(k), (px, py, c))
                     for k, (px, py) in enumerate(others)]
            arrivals = [_remote(refs[src].at[me], refs[dst].at[2 * px + py], *sems(k), (px, py, c))
                        for k, (px, py) in enumerate(others)]
            plan.append((True, sends, arrivals, []))
        else:
            _, name, layer = job
            ref = refs[name]
            cp = _remote(ref.at[layer, c], ref.at[layer, c], *sems(0), sibling)
            arrival = _remote(ref.at[layer, c], ref.at[layer, 1 - c], *sems(0), sibling)
            plan.append((True, [cp], [arrival], []))
    return plan


def _hosted_start(plan, now):
    for mine, sends, _, _ in plan:
        @pl.when(now & mine)
        def _(sends=sends):
            for cp in sends:
                cp.start()


def _hosted_finish(plan, now):
    for mine, sends, arrivals, sibling_arrivals in plan:
        @pl.when(now & mine)
        def _(sends=sends, arrivals=arrivals):
            for cp in arrivals:
                cp.wait_recv()
            for cp in sends:
                cp.wait_send()

        if sibling_arrivals:
            @pl.when(now & jnp.logical_not(mine))
            def _(sibling_arrivals=sibling_arrivals):
                for cp in sibling_arrivals:
                    cp.wait_recv()


def _pcall(body, hosted, *, name, grid, in_specs, out_specs, out_shape, semantics, scratch_shapes=(), aliases=None):
    aliases = dict(aliases or {})
    if hosted is None or not hosted.jobs:
        return pl.pallas_call(body, name=name, grid=grid, in_specs=in_specs, out_specs=out_specs,
                              out_shape=out_shape, scratch_shapes=list(scratch_shapes),
                              input_output_aliases=aliases, compiler_params=_params(*semantics))
    single = not isinstance(out_shape, (list, tuple))
    out_specs_l = [out_specs] if single else list(out_specs)
    out_shape_l = [out_shape] if single else list(out_shape)
    n_in, n_out, n_buf, n_job = len(in_specs), len(out_specs_l), len(hosted.names), len(hosted.jobs)

    def carrying(*refs):
        ins, outs = refs[:n_in], refs[n_in + n_buf:n_in + n_buf + n_out]
        bufs = refs[n_in + n_buf + n_out:n_in + 2 * n_buf + n_out]
        rest = refs[n_in + 2 * n_buf + n_out:]
        scratch, send_sems, recv_sems = rest[:-2], rest[-2], rest[-1]
        first, last = True, True
        for axis, size in enumerate(grid):
            first = first & (pl.program_id(axis) == 0)
            last = last & (pl.program_id(axis) == size - 1)
        plan = _hosted_plan(hosted, dict(zip(hosted.names, bufs)), send_sems, recv_sems)
        _hosted_start(plan, first)
        body(*ins, *outs, *scratch)
        _hosted_finish(plan, last)

    def run(*args):
        bufs = [hosted.pool[n] for n in hosted.names]
        sem = pltpu.SemaphoreType.DMA
        res = pl.pallas_call(
            carrying, name=name, grid=grid, in_specs=list(in_specs) + [_ANY] * n_buf,
            out_specs=out_specs_l + [_ANY] * n_buf,
            out_shape=out_shape_l + [SDS(b.shape, b.dtype) for b in bufs],
            scratch_shapes=list(scratch_shapes) + [sem((n_job, 3)), sem((n_job, 3))],
            input_output_aliases={**aliases, **{n_in + i: n_out + i for i in range(n_buf)}},
            compiler_params=pltpu.CompilerParams(dimension_semantics=semantics, vmem_limit_bytes=VMEM_LIMIT,
                                                 has_side_effects=True),
        )(*args, *bufs)
        hosted.pool.update(zip(hosted.names, res[n_out:]))
        return res[0] if single else res[:n_out]

    return run


def _dot(a, b):
    return jnp.dot(a, b, preferred_element_type=F32)


def _dot_nt(a, b):
    return lax.dot_general(a, b, NT, preferred_element_type=F32)


def _dot_tn(a, b):
    return lax.dot_general(a, b, TN, preferred_element_type=F32)


def norm_matmul(x, gain, w, layer, kind, name, hosted=None):
    T, D = x.shape
    if kind == "by_shard":
        tn = w.shape[3]
        N = N_CHIPS * tn
        w_spec = pl.BlockSpec((None, None, D, tn), lambda i, j: (j, layer, 0, 0))
        mm = _dot
    else:
        N = w.shape[0]
        tn = _tile(N, (1024, 512, 256, 128))
        w_spec = pl.BlockSpec((tn, D), lambda i, j: (j, 0))
        mm = _dot_nt
    tm = _tile(T, (1024, 512, 256, 128))

    def body(x_ref, g_ref, w_ref, y_ref, h_ref):
        @pl.when(pl.program_id(1) == 0)
        def _():
            xf = x_ref[...]
            r = lax.rsqrt(jnp.mean(xf * xf, axis=-1, keepdims=True) + RMS_EPS)
            h_ref[...] = ((xf * r) * g_ref[...]).astype(BF16)

        y_ref[...] = mm(h_ref[...], w_ref[...]).astype(BF16)

    return _pcall(
        body, hosted, name=name, grid=(T // tm, N // tn),
        in_specs=[pl.BlockSpec((tm, D), lambda i, j: (i, 0)),
                  pl.BlockSpec((None, 1, D), lambda i, j: (layer, 0, 0)),
                  w_spec],
        out_specs=[pl.BlockSpec((tm, tn), lambda i, j: (i, j)),
                   pl.BlockSpec((tm, D), lambda i, j: (i, 0))],
        out_shape=[SDS((T, N), BF16), SDS((T, D), BF16)],
        semantics=("arbitrary", "arbitrary"),
    )(x, gain, w)


def matmul_nt_normbwd(dys, w, layer, kind, x, gain, dres, name, hosted=None):
    T, D = x.shape
    width = dys[0].shape[1]
    if kind == "by_shard":
        tk = w.shape[3]
        w_spec = pl.BlockSpec((None, None, D, tk), lambda i, k: (k, layer, 0, 0))
        mm = _dot_nt
    else:
        tk = _tile(width, (1024, 512, 256, 128))
        w_spec = pl.BlockSpec((tk, D), lambda i, k: (k, 0))
        mm = _dot
    per = width // tk
    nk = per * len(dys)
    tm = _tile(T, (512, 256, 128))
    n_dy = len(dys)

    def dy_spec(p):
        return pl.BlockSpec((tm, tk), lambda i, k: (i, jnp.clip(k - p * per, 0, per - 1)))

    def body(*refs):
        dy_refs = refs[:n_dy]
        w_ref, x_ref, g_ref, dres_ref, dx_ref, dxb_ref, dg_ref, acc_ref = refs[n_dy:]
        i, k = pl.program_id(0), pl.program_id(1)

        @pl.when(k == 0)
        def _():
            acc_ref[...] = jnp.zeros_like(acc_ref)

        for p in range(n_dy):
            @pl.when((k >= p * per) & (k < (p + 1) * per))
            def _(p=p):
                acc_ref[...] += mm(dy_refs[p][...], w_ref[...])

        @pl.when(k == nk - 1)
        def _():
            xf = x_ref[...]
            r = lax.rsqrt(jnp.mean(xf * xf, axis=-1, keepdims=True) + RMS_EPS)
            xhat = xf * r
            dh = acc_ref[...]
            dhg = dh * g_ref[...]
            dx = dres_ref[...] + r * (dhg - xhat * jnp.mean(dhg * xhat, axis=-1, keepdims=True))
            dx_ref[...] = dx
            dxb_ref[...] = dx.astype(BF16)
            part = jnp.sum(dh * xhat, axis=0, keepdims=True)

            @pl.when(i == 0)
            def _():
                dg_ref[...] = part

            @pl.when(i > 0)
            def _():
                dg_ref[...] += part

    row = pl.BlockSpec((tm, D), lambda i, k: (i, 0))
    return _pcall(
        body, hosted, name=name, grid=(T // tm, nk),
        in_specs=[dy_spec(p) for p in range(n_dy)] + [
            w_spec, row, pl.BlockSpec((None, 1, D), lambda i, k: (layer, 0, 0)), row],
        out_specs=[row, row, pl.BlockSpec((1, D), lambda i, k: (0, 0))],
        out_shape=[SDS((T, D), F32), SDS((T, D), BF16), SDS((1, D), F32)],
        scratch_shapes=[pltpu.VMEM((tm, D), F32)],
        semantics=("arbitrary", "arbitrary"),
    )(*dys, w, x, gain, dres)


def matmul_tn(a, bs, name, b_col0=0, n_cols=None, by_dest=False, tn=None, tk=None, hosted=None):
    T, M = a.shape
    width = bs[0].shape[1]
    N = n_cols if n_cols else width * len(bs)
    tm = _tile(M, (1024, 512, 256, 128))
    tn = tn or _tile(N, (512, 256, 128))
    tk = tk or _tile(T, (4096, 2048, 1024, 512, 256))
    assert b_col0 % tn == 0 and width % tn == 0
    j0, per, nk, n_b = b_col0 // tn, width // tn, T // tk, len(bs)

    def b_spec(p):
        return pl.BlockSpec((tk, tn), lambda i, j, k: (k, jnp.clip(j0 + j - p * per, 0, per - 1)))

    def body(*refs):
        a_ref, b_refs = refs[0], refs[1:1 + n_b]
        o_ref, acc_ref = refs[-2], refs[-1]
        j, k = pl.program_id(1), pl.program_id(2)

        @pl.when(k == 0)
        def _():
            acc_ref[...] = jnp.zeros_like(acc_ref)

        for p in range(n_b):
            @pl.when((j0 + j >= p * per) & (j0 + j < (p + 1) * per))
            def _(p=p):
                acc_ref[...] += _dot_tn(a_ref[...], b_refs[p][...])

        @pl.when(k == nk - 1)
        def _():
            o_ref[...] = acc_ref[...].astype(BF16)

    if by_dest:
        cs = N // N_CHIPS
        npd = cs // tn
        out_shape = SDS((N_CHIPS, M, cs), BF16)
        out_spec = pl.BlockSpec((None, tm, tn), lambda i, j, k: (j // npd, i, j % npd))
    else:
        out_shape = SDS((M, N), BF16)
        out_spec = pl.BlockSpec((tm, tn), lambda i, j, k: (i, j))
    return _pcall(
        body, hosted, name=name, grid=(M // tm, N // tn, nk),
        in_specs=[pl.BlockSpec((tk, tm), lambda i, j, k: (k, i))] + [b_spec(p) for p in range(n_b)],
        out_specs=out_spec, out_shape=out_shape,
        scratch_shapes=[pltpu.VMEM((tm, tn), F32)],
        semantics=("arbitrary", "arbitrary", "arbitrary"),
    )(a, *bs)


def ffn_down_fwd(ab, w_down, layer, x1, hosted=None):
    T, D = x1.shape
    F = w_down.shape[1]
    tm = _tile(T, (512, 256, 128))
    tk = F // 2
    nk = F // tk

    def body(a_ref, b_ref, w_ref, x_ref, x2_ref, s_ref, acc_ref):
        k = pl.program_id(1)

        @pl.when(k == 0)
        def _():
            acc_ref[...] = x_ref[...]

        a = a_ref[...].astype(F32)
        s = (a * _sigmoid(a) * b_ref[...].astype(F32)).astype(BF16)
        s_ref[...] = s
        acc_ref[...] += _dot(s, w_ref[...])

        @pl.when(k == nk - 1)
        def _():
            x2_ref[...] = acc_ref[...]

    return _pcall(
        body, hosted, name="ffn_down_fwd", grid=(T // tm, nk),
        in_specs=[pl.BlockSpec((tm, tk), lambda i, k: (i, k)),
                  pl.BlockSpec((tm, tk), lambda i, k: (i, nk + k)),
                  pl.BlockSpec((None, tk, D), lambda i, k: (layer, k, 0)),
                  pl.BlockSpec((tm, D), lambda i, k: (i, 0))],
        out_specs=[pl.BlockSpec((tm, D), lambda i, k: (i, 0)),
                   pl.BlockSpec((tm, tk), lambda i, k: (i, k))],
        out_shape=[SDS((T, D), F32), SDS((T, F), BF16)],
        scratch_shapes=[pltpu.VMEM((tm, D), F32)],
        semantics=("arbitrary", "arbitrary"),
    )(ab, ab, w_down, x1)


def ffn_down_bwd(dx2b, w_down, layer, ab, hosted=None):
    T, D = dx2b.shape
    F = w_down.shape[1]
    tm = _tile(T, (512, 256, 128))
    tn = F // 2
    nj = F // tn

    def body(dx_ref, w_ref, a_ref, b_ref, da_ref, db_ref):
        ds = _dot_nt(dx_ref[...], w_ref[...])
        a = a_ref[...].astype(F32)
        sg = _sigmoid(a)
        da_ref[...] = (ds * b_ref[...].astype(F32) * (sg * (1.0 + a * (1.0 - sg)))).astype(BF16)
        db_ref[...] = (ds * (a * sg)).astype(BF16)

    blk = pl.BlockSpec((tm, tn), lambda i, j: (i, j))
    return _pcall(
        body, hosted, name="ffn_down_bwd", grid=(T // tm, nj),
        in_specs=[pl.BlockSpec((tm, D), lambda i, j: (i, 0)),
                  pl.BlockSpec((None, tn, D), lambda i, j: (layer, j, 0)),
                  blk, pl.BlockSpec((tm, tn), lambda i, j: (i, nj + j))],
        out_specs=[blk, blk],
        out_shape=[SDS((T, F), BF16), SDS((T, F), BF16)],
        semantics=("arbitrary", "arbitrary"),
    )(dx2b, w_down, ab, ab)


def _mix_specs(tm, D, layer):
    cs = D // N_CHIPS
    row = lambda w: pl.BlockSpec((tm, w), lambda i: (i, 0))
    wp = pl.BlockSpec((N_CHIPS, None, BRANCH_W, cs), lambda i: (0, layer, 0, 0))
    wo = pl.BlockSpec((None, N_CHIPS, cs, D), lambda i: (layer, 0, 0, 0))
    bg = pl.BlockSpec((None, 1, 3 * D), lambda i: (layer, 0, 0))
    return row, wp, wo, bg


def mix_fwd(ao, po, co, proj, b_gate, wpa, wpp, wpc, w_out, layer, x, hosted=None):
    T, D = x.shape
    cs = D // N_CHIPS
    tm = _tile(T, (256, 128))
    row, wp, wo, bg = _mix_specs(tm, D, layer)

    def body(ao_ref, po_ref, co_ref, g_ref, bg_ref, wpa_ref, wpp_ref, wpc_ref, wo_ref, x_ref,
             x1_ref, ys_ref, mixed_ref):
        mixed = jnp.zeros((tm, D), F32)
        for n, (br, wp_ref) in enumerate(((ao_ref, wpa_ref), (po_ref, wpp_ref), (co_ref, wpc_ref))):
            y = jnp.concatenate([_dot(br[...], wp_ref[j]) for j in range(N_CHIPS)], axis=1)
            cols = slice(n * D, (n + 1) * D)
            gate = _sigmoid(g_ref[:, cols].astype(F32) + bg_ref[:, cols])
            ys_ref[:, cols] = y.astype(BF16)
            mixed = mixed + gate * y
        mb = mixed.astype(BF16)
        mixed_ref[...] = mb
        acc = x_ref[...]
        for j in range(N_CHIPS):
            acc = acc + _dot(mb[:, j * cs:(j + 1) * cs], wo_ref[j])
        x1_ref[...] = acc

    return _pcall(
        body, hosted, name="mix_fwd", grid=(T // tm,),
        in_specs=[row(BRANCH_W), row(BRANCH_W), row(BRANCH_W), row(3 * D), bg, wp, wp, wp, wo, row(D)],
        out_specs=[row(D), row(3 * D), row(D)],
        out_shape=[SDS((T, D), F32), SDS((T, 3 * D), BF16), SDS((T, D), BF16)],
        semantics=("arbitrary",),
    )(ao, po, co, proj, b_gate, wpa, wpp, wpc, w_out, x)


def mix_bwd(dx1b, w_out, proj, b_gate, ys, wpa, wpp, wpc, layer, width, hosted=None):
    T, D = dx1b.shape
    cs = D // N_CHIPS
    tm = _tile(T, (256, 128))
    row, wp, wo, bg = _mix_specs(tm, D, layer)

    def body(dx_ref, wo_ref, g_ref, bg_ref, ys_ref, wpa_ref, wpp_ref, wpc_ref,
             dys_ref, dg_ref, dao_ref, dpo_ref, dco_ref, dbg_ref):
        i = pl.program_id(0)
        dx = dx_ref[...]
        dmixed = jnp.concatenate([_dot_nt(dx, wo_ref[j]) for j in range(N_CHIPS)], axis=1)
        for n, (wp_ref, dbr) in enumerate(((wpa_ref, dao_ref), (wpp_ref, dpo_ref), (wpc_ref, dco_ref))):
            cols = slice(n * D, (n + 1) * D)
            gate = _sigmoid(g_ref[:, cols].astype(F32) + bg_ref[:, cols])
            dy = (dmixed * gate).astype(BF16)
            dys_ref[:, cols] = dy
            dgp = dmixed * ys_ref[:, cols].astype(F32) * gate * (1.0 - gate)
            dg_ref[:, cols] = dgp.astype(BF16)
            part = jnp.sum(dgp, axis=0, keepdims=True)

            @pl.when(i == 0)
            def _():
                dbg_ref[:, cols] = part

            @pl.when(i > 0)
            def _():
                dbg_ref[:, cols] += part

            acc = jnp.zeros((tm, BRANCH_W), F32)
            for j in range(N_CHIPS):
                acc = acc + _dot_nt(dy[:, j * cs:(j + 1) * cs], wp_ref[j])
            dbr[...] = acc.astype(BF16)

    return _pcall(
        body, hosted, name="mix_bwd", grid=(T // tm,),
        in_specs=[row(D), wo, row(3 * D), bg, row(3 * D), wp, wp, wp],
        out_specs=[row(3 * D), row(3 * D), row(BRANCH_W), row(BRANCH_W), row(BRANCH_W),
                   pl.BlockSpec((1, 3 * D), lambda i: (0, 0))],
        out_shape=[SDS((T, 3 * D), BF16), SDS((T, width), BF16), SDS((T, BRANCH_W), BF16),
                   SDS((T, BRANCH_W), BF16), SDS((T, BRANCH_W), BF16), SDS((1, 3 * D), F32)],
        semantics=("arbitrary",),
    )(dx1b, w_out, proj, b_gate, ys, wpa, wpp, wpc)


def loss_head(x2, gain, target):
    T, D = x2.shape
    tm = _tile(T, (512, 256, 128))

    def body(x_ref, g_ref, t_ref, loss_ref, dx_ref, dxb_ref, dg_ref):
        i = pl.program_id(0)
        xf = x_ref[...]
        g = g_ref[...]
        r = lax.rsqrt(jnp.mean(xf * xf, axis=-1, keepdims=True) + RMS_EPS)
        xhat = xf * r
        diff = xhat * g - t_ref[...]
        part_loss = 0.5 * jnp.sum(jnp.mean(diff * diff, axis=-1, keepdims=True), axis=0, keepdims=True)
        dy = diff * (1.0 / D)
        dhg = dy * g
        dx = r * (dhg - xhat * jnp.mean(dhg * xhat, axis=-1, keepdims=True))
        dx_ref[...] = dx
        dxb_ref[...] = dx.astype(BF16)
        part_g = jnp.sum(dy * xhat, axis=0, keepdims=True)
        part_l = jnp.broadcast_to(part_loss, (1, LANES))

        @pl.when(i == 0)
        def _():
            dg_ref[...] = part_g
            loss_ref[...] = part_l

        @pl.when(i > 0)
        def _():
            dg_ref[...] += part_g
            loss_ref[...] += part_l

    row = pl.BlockSpec((tm, D), lambda i: (i, 0))
    return pl.pallas_call(
        body, name="loss_head", grid=(T // tm,),
        in_specs=[row, pl.BlockSpec((1, D), lambda i: (0, 0)), row],
        out_specs=[pl.BlockSpec((1, LANES), lambda i: (0, 0)), row, row, pl.BlockSpec((1, D), lambda i: (0, 0))],
        out_shape=[SDS((1, LANES), F32), SDS((T, D), F32), SDS((T, D), BF16), SDS((1, D), F32)],
        compiler_params=_params("arbitrary"),
    )(x2, gain, target)


def _placement_constants():
    w = HEADS * HEAD_PAD
    pq = np.zeros((BRANCH_W, w), np.float32)
    pk = np.zeros((BRANCH_W, w), np.float32)
    pfq = np.zeros((3, LANES, w), np.float32)
    pfk = np.zeros((3, LANES, w), np.float32)
    cq = np.zeros((1, w), np.float32)
    ck = np.zeros((1, w), np.float32)
    eq = np.zeros((w, LANES), np.float32)
    ek = np.zeros((w, LANES), np.float32)
    for h in range(HEADS):
        for d in range(HEAD_DIM):
            pq[h * HEAD_DIM + d, h * HEAD_PAD + d] = HEAD_DIM ** -0.5
            pk[h * HEAD_DIM + d, h * HEAD_PAD + d] = 1.0
        for i in range(3):
            pfq[i, h, h * HEAD_PAD + HEAD_DIM + i] = 1.0
            pfk[i, h, h * HEAD_PAD + HEAD_DIM + 3 + i] = -1.0
            cq[0, h * HEAD_PAD + HEAD_DIM + 3 + i] = 1.0
            ck[0, h * HEAD_PAD + HEAD_DIM + i] = 1.0
        eq[h * HEAD_PAD + HEAD_DIM, h] = 1.0
        ek[h * HEAD_PAD + HEAD_DIM + 3, h] = -1.0
    bf = lambda a: jnp.asarray(a, BF16)
    return dict(pq=bf(pq), pk=bf(pk), pfq=bf(pfq), pfk=bf(pfk), cq=jnp.asarray(cq), ck=jnp.asarray(ck),
                pqkt=bf(np.concatenate([pq.T, pk.T], axis=0)), eq=bf(eq), ek=bf(ek))


def attn_prep(proj3, bf_rows, layer, cst, lay, hosted=None):
    Bl, S, _ = proj3.shape
    ts = ATTN_BLOCK
    w = HEADS * HEAD_PAD

    def body(q_ref, k_ref, f_ref, bf_ref, pq_ref, pk_ref, pfq_ref, pfk_ref, cq_ref, ck_ref,
             qa_ref, ka_ref, carry_ref):
        @pl.when(pl.program_id(1) == 0)
        def _():
            carry_ref[...] = jnp.zeros_like(carry_ref)

        z = f_ref[...].astype(F32) + bf_ref[...]
        logf = jnp.minimum(z, 0.0) - jnp.log(1.0 + jnp.exp(-jnp.abs(z)))
        r = lax.broadcasted_iota(jnp.int32, (ts, ts), 0)
        c = lax.broadcasted_iota(jnp.int32, (ts, ts), 1)
        tri = jnp.where(r >= c, 1.0, 0.0).astype(BF16)
        fcum = carry_ref[...]
        for part in _split3(logf):
            fcum = fcum + _dot(tri, part)
        carry_ref[...] = fcum[ts - 1:ts, :]
        qa = _dot(q_ref[...], pq_ref[...]) + cq_ref[...]
        ka = _dot(k_ref[...], pk_ref[...]) + ck_ref[...]
        for i, part in enumerate(_split3(fcum)):
            qa = qa + _dot(part, pfq_ref[i])
            ka = ka + _dot(part, pfk_ref[i])
        qa_ref[...] = qa.astype(BF16)
        ka_ref[...] = ka.astype(BF16)

    cfull = lambda shape: pl.BlockSpec(shape, lambda b, s: (0,) * len(shape))
    return _pcall(
        body, hosted, name="attn_prep", grid=(Bl, S // ts),
        in_specs=[pl.BlockSpec((None, ts, BRANCH_W), lambda b, s: (b, s, lay["q"] // BRANCH_W)),
                  pl.BlockSpec((None, ts, BRANCH_W), lambda b, s: (b, s, lay["k"] // BRANCH_W)),
                  pl.BlockSpec((None, ts, LANES), lambda b, s: (b, s, lay["f"] // LANES)),
                  pl.BlockSpec((None, 1, LANES), lambda b, s: (layer, 0, 0)),
                  cfull((BRANCH_W, w)), cfull((BRANCH_W, w)),
                  cfull((3, LANES, w)), cfull((3, LANES, w)), cfull((1, w)), cfull((1, w))],
        out_specs=[pl.BlockSpec((None, ts, w), lambda b, s: (b, s, 0)),
                   pl.BlockSpec((None, ts, w), lambda b, s: (b, s, 0))],
        out_shape=[SDS((Bl, S, w), BF16), SDS((Bl, S, w), BF16)],
        scratch_shapes=[pltpu.VMEM((1, LANES), F32)],
        semantics=("arbitrary", "arbitrary"),
    )(proj3, proj3, proj3, bf_rows, cst["pq"], cst["pk"], cst["pfq"], cst["pfk"], cst["cq"], cst["ck"])


def attn_fwd(qa, ka, proj3, lay, hosted=None):
    Bl, S, _ = qa.shape
    tq = ATTN_BLOCK
    nq = S // tq
    pairs = HEADS // 2
    pw = 2 * HEAD_PAD
    vw = 2 * HEAD_DIM

    def body(qa_ref, ka_ref, v_ref, o_ref, lse_ref):
        row = lax.broadcasted_iota(jnp.int32, (tq, tq), 0)
        col = lax.broadcasted_iota(jnp.int32, (tq, tq), 1)
        causal = row <= col
        for i in range(nq):
            nk = (i + 1) * tq
            rows = slice(i * tq, nk)
            o_t = []
            for h in range(2):
                hs = slice(h * HEAD_PAD, (h + 1) * HEAD_PAD)
                st = _dot_nt(ka_ref[0:nk, hs], qa_ref[rows, hs])
                diag = jnp.where(causal, st[nk - tq:], NEG_INF)
                m = jnp.max(diag, axis=0, keepdims=True)
                if i:
                    m = jnp.maximum(m, jnp.max(st[:nk - tq], axis=0, keepdims=True))
                p_diag = jnp.exp(diag - m)
                l = jnp.sum(p_diag, axis=0, keepdims=True)
                if i:
                    p_top = jnp.exp(st[:nk - tq] - m)
                    l = l + jnp.sum(p_top, axis=0, keepdims=True)
                    p = jnp.concatenate([p_top.astype(BF16), p_diag.astype(BF16)], axis=0)
                else:
                    p = p_diag.astype(BF16)
                acc = _dot_tn(v_ref[0:nk, :], p)
                o_t.append(acc[h * HEAD_DIM:(h + 1) * HEAD_DIM, :] / l)
                lse_ref[h:h + 1, rows] = m + jnp.log(l)
            o_ref[rows, :] = jnp.concatenate(o_t, axis=0).T.astype(BF16)

    return _pcall(
        body, hosted, name="attn_fwd", grid=(Bl, pairs),
        in_specs=[pl.BlockSpec((None, S, pw), lambda b, p: (b, 0, p)),
                  pl.BlockSpec((None, S, pw), lambda b, p: (b, 0, p)),
                  pl.BlockSpec((None, S, vw), lambda b, p: (b, 0, lay["v"] // vw + p))],
        out_specs=[pl.BlockSpec((None, S, vw), lambda b, p: (b, 0, p)),
                   pl.BlockSpec((None, None, 2, S), lambda b, p: (b, p, 0, 0))],
        out_shape=[SDS((Bl, S, BRANCH_W), BF16), SDS((Bl, pairs, 2, S), F32)],
        semantics=("arbitrary", "arbitrary"),
    )(qa, ka, proj3)


def attn_bwd(qa, ka, proj3, dao, ao, lse, dproj3, lay, hosted=None):
    Bl, S, _ = qa.shape
    tk = ATTN_BLOCK
    nq = S // tk
    pairs = HEADS // 2
    pw = 2 * HEAD_PAD
    vw = 2 * HEAD_DIM

    def body(qa_ref, ka_ref, v_ref, do_ref, o_ref, lse_ref, _, dqa_ref, dka_ref, dv_ref):
        row = lax.broadcasted_iota(jnp.int32, (tk, tk), 0)
        col = lax.broadcasted_iota(jnp.int32, (tk, tk), 1)
        causal = row <= col
        lane8 = lax.broadcasted_iota(jnp.int32, (8, vw), 1)
        lane_s = lax.broadcasted_iota(jnp.int32, (S, vw), 1)
        lane_k = lax.broadcasted_iota(jnp.int32, (tk, vw), 1)
        doo = do_ref[...].astype(F32) * o_ref[...].astype(F32)
        hi = doo.astype(BF16)
        lo = (doo - hi.astype(F32)).astype(BF16)
        delta, v_head = [], []
        for h in range(2):
            sel = jnp.where((lane8 >= h * HEAD_DIM) & (lane8 < (h + 1) * HEAD_DIM), 1.0, 0.0).astype(BF16)
            delta.append((_dot_nt(sel, hi) + _dot_nt(sel, lo))[0:1, :])
            in_head = (lane_s >= h * HEAD_DIM) & (lane_s < (h + 1) * HEAD_DIM)
            v_head.append(jnp.where(in_head, v_ref[...], jnp.zeros_like(v_ref[...])))
        dqa_ref[...] = jnp.zeros_like(dqa_ref)
        for j in range(nq):
            q0 = j * tk
            krows = slice(q0, q0 + tk)
            do = do_ref[q0:, :]
            dvs = []
            for h in range(2):
                hs = slice(h * HEAD_PAD, (h + 1) * HEAD_PAD)
                k = ka_ref[krows, hs]
                q = qa_ref[q0:, hs]
                st = _dot_nt(k, q)
                p = jnp.exp(st - lse_ref[h:h + 1, q0:])
                p_diag = jnp.where(causal, p[:, :tk], 0.0)
                p = jnp.concatenate([p_diag, p[:, tk:]], axis=1) if j < nq - 1 else p_diag
                dvs.append(_dot(p.astype(BF16), do))
                dpt = _dot_nt(v_head[h][krows, :], do)
                ds = (p * (dpt - delta[h][:, q0:])).astype(BF16)
                dka_ref[krows, hs] = _dot(ds, q)
                dqa_ref[q0:, hs] += _dot_tn(ds, k)
            dv_ref[krows, :] = jnp.where(lane_k < HEAD_DIM, dvs[0], dvs[1]).astype(BF16)

    seq = lambda w, c0=0: pl.BlockSpec((None, S, w), lambda b, p: (b, 0, c0 + p))
    return _pcall(
        body, hosted, name="attn_bwd", grid=(Bl, pairs),
        in_specs=[seq(pw), seq(pw), seq(vw, lay["v"] // vw), seq(vw), seq(vw),
                  pl.BlockSpec((None, None, 2, S), lambda b, p: (b, p, 0, 0)), _ANY],
        out_specs=[seq(pw), seq(pw), seq(vw, lay["v"] // vw)],
        out_shape=[SDS((Bl, S, HEADS * HEAD_PAD), F32), SDS((Bl, S, HEADS * HEAD_PAD), F32),
                   SDS(dproj3.shape, BF16)],
        aliases={6: 2}, semantics=("arbitrary", "arbitrary"),
    )(qa, ka, proj3, dao, ao, lse, dproj3)


def attn_post(dqa, dka, proj3, bf_rows, layer, dproj3, cst, lay):
    Bl, S, w = dqa.shape
    ts = ATTN_BLOCK
    ns = S // ts
    qkf = 2 * BRANCH_W + F_PAD

    def body(dqa_ref, dka_ref, f_ref, bf_ref, pqkt_ref, eq_ref, ek_ref, _, dqkf_ref, dbf_ref, carry_ref):
        b, s = pl.program_id(0), pl.program_id(1)

        @pl.when(s == 0)
        def _():
            carry_ref[...] = jnp.zeros_like(carry_ref)

        dqa_v, dka_v = dqa_ref[...], dka_ref[...]
        qh = dqa_v.astype(BF16)
        kh = dka_v.astype(BF16)
        dqkf_ref[:, :BRANCH_W] = _dot(qh, pqkt_ref[:w, :]).astype(BF16)
        dqkf_ref[:, BRANCH_W:2 * BRANCH_W] = _dot(kh, pqkt_ref[w:, :]).astype(BF16)
        ql = (dqa_v - qh.astype(F32)).astype(BF16)
        kl = (dka_v - kh.astype(F32)).astype(BF16)
        d_f = (_dot(qh, eq_ref[...]) + _dot(ql, eq_ref[...])) + (_dot(kh, ek_ref[...]) + _dot(kl, ek_ref[...]))
        r = lax.broadcasted_iota(jnp.int32, (ts, ts), 0)
        c = lax.broadcasted_iota(jnp.int32, (ts, ts), 1)
        triu = jnp.where(c >= r, 1.0, 0.0).astype(BF16)
        rev = carry_ref[...]
        for part in _split3(d_f):
            rev = rev + _dot(triu, part)
        carry_ref[...] = rev[0:1, :]
        z = f_ref[...].astype(F32) + bf_ref[...]
        lane = lax.broadcasted_iota(jnp.int32, (ts, LANES), 1)
        dfl = jnp.where(lane < HEADS, rev / (1.0 + jnp.exp(z)), 0.0)
        dqkf_ref[:, 2 * BRANCH_W:] = jnp.concatenate(
            [dfl.astype(BF16), jnp.zeros((ts, F_PAD - LANES), BF16)], axis=1)
        part = jnp.sum(dfl, axis=0, keepdims=True)

        @pl.when((b == 0) & (s == 0))
        def _():
            dbf_ref[...] = part

        @pl.when((b > 0) | (s > 0))
        def _():
            dbf_ref[...] += part

    assert lay["q"] % qkf == 0
    cfull = lambda shape: pl.BlockSpec(shape, lambda b, s: (0,) * len(shape))
    rev_blk = lambda wd, c0=0: pl.BlockSpec((None, ts, wd), lambda b, s: (b, ns - 1 - s, c0))
    return pl.pallas_call(
        body, name="attn_post", grid=(Bl, ns),
        in_specs=[rev_blk(w), rev_blk(w), rev_blk(LANES, lay["f"] // LANES),
                  pl.BlockSpec((None, 1, LANES), lambda b, s: (layer, 0, 0)),
                  cfull((2 * w, BRANCH_W)), cfull((w, LANES)), cfull((w, LANES)), _ANY],
        out_specs=[rev_blk(qkf, lay["q"] // qkf), cfull((1, LANES))],
        out_shape=[SDS(dproj3.shape, BF16), SDS((1, LANES), F32)],
        scratch_shapes=[pltpu.VMEM((1, LANES), F32)],
        input_output_aliases={7: 0},
        compiler_params=_params("arbitrary", "arbitrary"),
    )(dqa, dka, proj3, bf_rows, cst["pqkt"], cst["eq"], cst["ek"], dproj3)


def _shift_down(x, k, row):
    return jnp.where(row >= k, pltpu.roll(x, k, axis=0), 0.0)


def _shift_up(x, k, row):
    n = x.shape[0]
    return jnp.where(row < n - k, pltpu.roll(x, n - k, axis=0), 0.0)


def _window_sum(x, g, row, shift):
    s2 = x + shift(x, 1, row)
    s4 = s2 + shift(s2, 2, row)
    s8 = s4 + shift(s4, 4, row)
    s16 = s8 + shift(s8, 8, row)
    return jnp.where(g == 0, s2, jnp.where(g == 1, s4, jnp.where(g == 2, s8, s16)))


def _window_count(g, row):
    wnd = jnp.where(g == 0, 2, jnp.where(g == 1, 4, jnp.where(g == 2, 8, 16)))
    return jnp.minimum(row + 1, wnd).astype(F32)


def _group_columns(ref):
    return [ref[:, n * GROUP_W:(n + 1) * GROUP_W].astype(F32) for n in range(4)]


def poolconv_fwd(proj3, pool_w, pool_scale, conv_w, layer, lay, hosted=None):
    Bl, S, _ = proj3.shape

    def body(x_ref, pw_ref, ps_ref, cw_ref, po_ref, co_ref):
        g = pl.program_id(1)
        row = lax.broadcasted_iota(jnp.int32, (S, GROUP_W), 0)
        u, cv, cb, cc = _group_columns(x_ref)
        d = _window_sum(u, g, row, _shift_down) / _window_count(g, row) - u
        po_ref[...] = (_dot(d.astype(BF16), pw_ref[...]) * ps_ref[...]).astype(BF16)
        z = cc * cv
        y = cw_ref[0:1, :] * _shift_down(z, 2, row) + cw_ref[1:2, :] * _shift_down(z, 1, row) + cw_ref[2:3, :] * z
        co_ref[...] = (cb * y).astype(BF16)

    out = pl.BlockSpec((None, S, GROUP_W), lambda b, g: (b, 0, g))
    return _pcall(
        body, hosted, name="poolconv_fwd", grid=(Bl, N_GROUPS),
        in_specs=[pl.BlockSpec((None, S, BRANCH_W), lambda b, g: (b, 0, lay["pc"] // BRANCH_W + g)),
                  pl.BlockSpec((None, None, GROUP_W, GROUP_W), lambda b, g: (layer, g, 0, 0)),
                  pl.BlockSpec((None, 1, GROUP_W), lambda b, g: (layer, 0, g)),
                  pl.BlockSpec((None, None, 3, GROUP_W), lambda b, g: (g, layer, 0, 0))],
        out_specs=[out, out],
        out_shape=[SDS((Bl, S, BRANCH_W), BF16), SDS((Bl, S, BRANCH_W), BF16)],
        semantics=("arbitrary", "arbitrary"),
    )(proj3, pool_w, pool_scale, conv_w)


def poolconv_bwd(proj3, dpo, dco, pool_w, pool_scale, conv_w, layer, dproj3, lay):
    Bl, S, _ = proj3.shape

    def body(x_ref, dpo_ref, dco_ref, pw_ref, ps_ref, cw_ref, _, dx_ref, dpw_ref, dps_ref, dcw_ref):
        g, b = pl.program_id(0), pl.program_id(1)
        row = lax.broadcasted_iota(jnp.int32, (S, GROUP_W), 0)
        cnt = _window_count(g, row)
        u, cv, cb, cc = _group_columns(x_ref)
        d = (_window_sum(u, g, row, _shift_down) / cnt - u).astype(BF16)
        pw = pw_ref[...]
        ypre = _dot(d, pw)
        dpo_v = dpo_ref[...].astype(F32)
        dps = jnp.sum(dpo_v * ypre, axis=0, keepdims=True)
        dyp = (dpo_v * ps_ref[...]).astype(BF16)
        dpw = _dot_tn(d, dyp)
        dd = _dot_nt(dyp, pw)
        dx_ref[:, 0:GROUP_W] = (_window_sum(dd / cnt, g, row, _shift_up) - dd).astype(BF16)

        z = cc * cv
        z1, z2 = _shift_down(z, 1, row), _shift_down(z, 2, row)
        w0, w1, w2 = cw_ref[0:1, :], cw_ref[1:2, :], cw_ref[2:3, :]
        y = w0 * z2 + w1 * z1 + w2 * z
        dco_v = dco_ref[...].astype(F32)
        dy = dco_v * cb
        dz = w0 * _shift_up(dy, 2, row) + w1 * _shift_up(dy, 1, row) + w2 * dy
        dx_ref[:, GROUP_W:2 * GROUP_W] = (dz * cc).astype(BF16)
        dx_ref[:, 2 * GROUP_W:3 * GROUP_W] = (dco_v * y).astype(BF16)
        dx_ref[:, 3 * GROUP_W:] = (dz * cv).astype(BF16)
        dcw = jnp.concatenate([jnp.sum(dy * z2, axis=0, keepdims=True),
                               jnp.sum(dy * z1, axis=0, keepdims=True),
                               jnp.sum(dy * z, axis=0, keepdims=True)], axis=0)

        @pl.when(b == 0)
        def _():
            dpw_ref[...] = dpw
            dps_ref[...] = dps
            dcw_ref[...] = dcw

        @pl.when(b > 0)
        def _():
            dpw_ref[...] += dpw
            dps_ref[...] += dps
            dcw_ref[...] += dcw

    blk = pl.BlockSpec((None, S, GROUP_W), lambda g, b: (b, 0, g))
    pc = pl.BlockSpec((None, S, BRANCH_W), lambda g, b: (b, 0, lay["pc"] // BRANCH_W + g))
    return pl.pallas_call(
        body, name="poolconv_bwd", grid=(N_GROUPS, Bl),
        in_specs=[pc, blk, blk,
                  pl.BlockSpec((None, None, GROUP_W, GROUP_W), lambda g, b: (layer, g, 0, 0)),
                  pl.BlockSpec((None, 1, GROUP_W), lambda g, b: (layer, 0, g)),
                  pl.BlockSpec((None, None, 3, GROUP_W), lambda g, b: (g, layer, 0, 0)), _ANY],
        out_specs=[pc, pl.BlockSpec((None, GROUP_W, GROUP_W), lambda g, b: (g, 0, 0)),
                   pl.BlockSpec((1, GROUP_W), lambda g, b: (0, g)),
                   pl.BlockSpec((None, 3, GROUP_W), lambda g, b: (g, 0, 0))],
        out_shape=[SDS(dproj3.shape, BF16), SDS((N_GROUPS, GROUP_W, GROUP_W), F32), SDS((1, BRANCH_W), F32),
                   SDS((N_GROUPS, 3, GROUP_W), F32)],
        input_output_aliases={6: 0},
        compiler_params=_params("arbitrary", "arbitrary"),
    )(proj3, dpo, dco, pool_w, pool_scale, conv_w, dproj3)


def _tile_2d(rows, cols, n_arrays):
    budget = VMEM_LIMIT // 2
    lanes = -(-cols // LANES) * LANES
    if rows % 8 == 0:
        for t in (2048, 1024, 512, 256, 128, 64, 32, 16, 8):
            if rows % t == 0 and 2 * n_arrays * t * lanes * 4 <= budget:
                return t, cols
    for t in (1024, 512, 256, 128):
        if cols % t == 0 and 2 * n_arrays * (rows + 8) * t * 4 <= budget:
            return rows, t
    return rows, cols


def add_pair(kept, layer, where, received, name):
    _, n, _, R, C = kept.shape
    tr, tc = _tile_2d(R, C, 3)

    def body(where_ref, a_ref, b_ref, o_ref):
        o_ref[...] = (a_ref[...].astype(F32) + b_ref[...].astype(F32)).astype(BF16)

    blk = pl.BlockSpec((None, tr, tc), lambda d, i, j, where_ref: (d, i, j))
    grid_spec = pltpu.PrefetchScalarGridSpec(
        num_scalar_prefetch=1, grid=(n, R // tr, C // tc),
        in_specs=[pl.BlockSpec((None, None, None, tr, tc),
                               lambda d, i, j, where_ref: (layer, d, where_ref[0], i, j)), blk],
        out_specs=blk)
    return pl.pallas_call(body, name=name, grid_spec=grid_spec, out_shape=SDS((n, R, C), BF16),
                          compiler_params=_params("arbitrary", "arbitrary", "arbitrary"))(where, kept, received)


def add_chips(arrived, own, layer, where, n_layers, prev, name):
    _, R, C = arrived.shape
    tr, tc = _tile_2d(R, C, 6)

    def body(where_ref, a0, a1, a2, a3, own_ref, *rest):
        o_ref = rest[-1]
        chip = where_ref[1]
        acc = None
        for j, a_ref in enumerate((a0, a1, a2, a3)):
            term = jnp.where(chip == j, own_ref[...], a_ref[...]).astype(F32)
            acc = term if acc is None else acc + term
        o_ref[...] = acc

    def slot(j):
        return pl.BlockSpec((None, tr, tc), lambda i, k, where_ref, j=j: (
            jnp.where(where_ref[1] == j, (j + 1) % N_CHIPS, j), i, k))

    in_specs = [slot(j) for j in range(N_CHIPS)] + [
        pl.BlockSpec((None, tr, tc), lambda i, k, where_ref: (where_ref[1], i, k))]
    args = [where, arrived, arrived, arrived, arrived, own]
    aliases = {}
    if prev is not None:
        in_specs.append(_ANY)
        args.append(prev)
        aliases = {len(args) - 1: 0}
    grid_spec = pltpu.PrefetchScalarGridSpec(
        num_scalar_prefetch=1, grid=(R // tr, C // tc), in_specs=in_specs,
        out_specs=pl.BlockSpec((None, None, tr, tc), lambda i, k, where_ref: (layer, where_ref[0], i, k)))
    return pl.pallas_call(body, name=name, grid_spec=grid_spec, out_shape=SDS((n_layers, 2, R, C), F32),
                          input_output_aliases=aliases,
                          compiler_params=_params("arbitrary", "arbitrary"))(*args)


def adamw(w, g, m, v, name):
    if w.ndim == 2:
        R, C = w.shape
        tr, _ = _tile_2d(R, C, 7)
        grid, blk = (R // tr,), pl.BlockSpec((tr, C), lambda i: (i, 0))
    else:
        N, r, C = w.shape
        tn = max(t for t in range(1, N + 1) if N % t == 0 and t * r * C * 4 <= 512 * 1024)
        grid, blk = (N // tn,), pl.BlockSpec((tn, r, C), lambda i: (i, 0, 0))

    def body(w_ref, g_ref, m_ref, v_ref, d_ref, nm_ref, nv_ref):
        gv = g_ref[...]
        m_new = ADAM_B1 * m_ref[...] + (1.0 - ADAM_B1) * gv
        v_new = ADAM_B2 * v_ref[...] + (1.0 - ADAM_B2) * (gv * gv)
        m_hat = m_new / (1.0 - ADAM_B1 ** ADAM_STEP)
        v_hat = v_new / (1.0 - ADAM_B2 ** ADAM_STEP)
        d_ref[...] = -ADAM_LR * (m_hat / (jnp.sqrt(v_hat) + ADAM_EPS) + ADAM_WD * w_ref[...])
        nm_ref[...] = m_new
        nv_ref[...] = v_new

    out = SDS(w.shape, F32)
    return pl.pallas_call(body, name=name, grid=grid, in_specs=[blk] * 4, out_specs=[blk] * 3,
                          out_shape=[out, out, out], compiler_params=_params("arbitrary"))(w, g, m, v)


_COMM = pltpu.CompilerParams(has_side_effects=True)


def gather_buffers(shards):
    me_chip = 2 * lax.axis_index("x") + lax.axis_index("y")
    pool = {}
    for name, sh in shards.items():
        L, r, c = sh.shape
        if name in ROW_SHARDED:
            pool[name] = lax.dynamic_update_slice(lax.empty((L, N_CHIPS, r, c), sh.dtype), sh[:, None],
                                                  (0, me_chip, 0, 0))
        else:
            pool[name] = lax.dynamic_update_slice(lax.empty((N_CHIPS, L, r, c), sh.dtype), sh[None],
                                                  (me_chip, 0, 0, 0))
    return pool


def comm_now(pool, stages, name):
    stages = [Hosted(pool, jobs) for jobs in stages]
    names = sorted({m for st in stages for m in st.names})
    n = len(names)

    def body(*refs):
        bufs = dict(zip(names, refs[n:2 * n]))
        sems = refs[2 * n:]
        for i, st in enumerate(stages):
            plan = _hosted_plan(st, bufs, sems[2 * i], sems[2 * i + 1])
            _hosted_start(plan, True)
            _hosted_finish(plan, True)

    sem = pltpu.SemaphoreType.DMA
    scratch = []
    for st in stages:
        scratch += [sem((len(st.jobs), 3)), sem((len(st.jobs), 3))]
    res = pl.pallas_call(
        body, name=name, in_specs=[_ANY] * n, out_specs=[_ANY] * n,
        out_shape=[SDS(pool[m].shape, pool[m].dtype) for m in names],
        scratch_shapes=scratch, input_output_aliases={t: t for t in range(n)},
        compiler_params=_COMM,
    )(*[pool[m] for m in names])
    pool.update(zip(names, res))


def gather_now(pool, units):
    comm_now(pool, [[("ici", name, layer) for name, layer in units],
                    [("fwd", name, layer) for name, layer in units]], "gather_now")


def allgather_chips(buf, name):
    def body(src_ref, out_ref, send_sems, recv_sems, local_sem):
        x, y, c = _position()
        me = 2 * x + y
        mine = pltpu.make_async_copy(src_ref, out_ref.at[me], local_sem)
        mine.start()
        sends = []
        for k, (px, py) in enumerate(_other_chips(x, y)):
            cp = _remote(src_ref, out_ref.at[me], send_sems.at[k], recv_sems.at[k], (px, py, c))
            cp.start()
            sends.append(cp)
        for k, (px, py) in enumerate(_other_chips(x, y)):
            _remote(src_ref, out_ref.at[2 * px + py], send_sems.at[k], recv_sems.at[k], (px, py, c)).wait_recv()
        for cp in sends:
            cp.wait_send()
        mine.wait()

    sem = pltpu.SemaphoreType.DMA
    return pl.pallas_call(
        body, name=name, in_specs=[_ANY], out_specs=_ANY, out_shape=SDS((N_CHIPS,) + buf.shape, buf.dtype),
        scratch_shapes=[sem((3,)), sem((3,)), sem], compiler_params=_COMM,
    )(buf)


def swap_sibling(tensors, name):
    n = len(tensors)

    def body(*refs):
        srcs, outs, send_sems, recv_sems = refs[:n], refs[n:2 * n], refs[2 * n], refs[2 * n + 1]
        x, y, c = _position()
        cps = [_remote(srcs[t].at[1 - c], outs[t], send_sems.at[t], recv_sems.at[t], (x, y, 1 - c))
               for t in range(n)]
        for cp in cps:
            cp.start()
        for cp in cps:
            cp.wait()

    sem = pltpu.SemaphoreType.DMA
    return pl.pallas_call(
        body, name=name, in_specs=[_ANY] * n, out_specs=[_ANY] * n,
        out_shape=[SDS(t.shape[1:], t.dtype) for t in tensors],
        scratch_shapes=[sem((n,)), sem((n,))], compiler_params=_COMM,
    )(*tensors)


def exchange_chips(tensors, name):
    n = len(tensors)

    def body(*refs):
        srcs, outs = refs[:n], refs[n:2 * n]
        send_sems, recv_sems, local_sems = refs[2 * n:]
        x, y, c = _position()
        me = 2 * x + y
        others = _other_chips(x, y)
        cps = []
        for t in range(n):
            cp = pltpu.make_async_copy(srcs[t].at[me], outs[t].at[me], local_sems.at[t])
            cp.start()
            cps.append(cp)
        sends = []
        for t in range(n):
            for k, (px, py) in enumerate(others):
                cp = _remote(srcs[t].at[2 * px + py], outs[t].at[me], send_sems.at[t, k], recv_sems.at[t, k],
                             (px, py, c))
                cp.start()
                sends.append(cp)
        for t in range(n):
            for k, (px, py) in enumerate(others):
                _remote(srcs[t].at[me], outs[t].at[2 * px + py], send_sems.at[t, k], recv_sems.at[t, k],
                        (px, py, c)).wait_recv()
        for cp in sends:
            cp.wait_send()
        for cp in cps:
            cp.wait()

    sem = pltpu.SemaphoreType.DMA
    return pl.pallas_call(
        body, name=name, in_specs=[_ANY] * n, out_specs=[_ANY] * n,
        out_shape=[SDS(t.shape, t.dtype) for t in tensors],
        scratch_shapes=[sem((n, 3)), sem((n, 3)), sem((n,))], compiler_params=_COMM,
    )(*tensors)


def join_halves(tensors, name):
    n = len(tensors)

    def body(*refs):
        outs, send_sems, recv_sems = refs[n:2 * n], refs[2 * n], refs[2 * n + 1]
        x, y, c = _position()
        sib = (x, y, 1 - c)
        sends = []
        for t in range(n):
            cp = _remote(outs[t].at[c], outs[t].at[c], send_sems.at[t], recv_sems.at[t], sib)
            cp.start()
            sends.append(cp)
        for t in range(n):
            _remote(outs[t].at[c], outs[t].at[1 - c], send_sems.at[t], recv_sems.at[t], sib).wait_recv()
        for cp in sends:
            cp.wait_send()

    sem = pltpu.SemaphoreType.DMA
    return pl.pallas_call(
        body, name=name, in_specs=[_ANY] * n, out_specs=[_ANY] * n,
        out_shape=[SDS(t.shape, t.dtype) for t in tensors],
        scratch_shapes=[sem((n,)), sem((n,))], input_output_aliases={t: t for t in range(n)},
        compiler_params=_COMM,
    )(*tensors)


BIG = ("w_in", "w_proj_attn", "w_proj_pool", "w_proj_conv", "conv_w", "w_out", "w_gate_up", "w_down")
REPLICATED = ("attn_norm", "b_forget", "b_gate", "pool_w", "pool_scale", "ffn_norm", "final_norm")
ORDER = ("attn_norm", "w_in", "b_forget", "b_gate", "w_proj_attn", "pool_w", "pool_scale", "w_proj_pool",
         "conv_w", "w_proj_conv", "w_out", "ffn_norm", "w_gate_up", "w_down", "final_norm")


def _proj_layout(D):
    lay = {"g": 0, "q": 3 * D}
    lay["k"] = lay["q"] + BRANCH_W
    lay["f"] = lay["k"] + BRANCH_W
    lay["v"] = lay["f"] + F_PAD
    lay["pc"] = lay["v"] + BRANCH_W
    lay["width"] = lay["pc"] + 4 * BRANCH_W
    return lay


_REF = dict(q=0, k=512, v=1024, f=1536, u=1544, cv=2056, cb=2568, cc=3080, g=3592)


def _packed_pieces(D):
    pieces = [(_REF["g"], 3 * D), (_REF["q"], BRANCH_W), (_REF["k"], BRANCH_W), (_REF["f"], HEADS),
              (None, F_PAD - HEADS), (_REF["v"], BRANCH_W)]
    for gi in range(N_GROUPS):
        pieces += [(_REF[name] + gi * GROUP_W, GROUP_W) for name in ("u", "cv", "cb", "cc")]
    return pieces


def _packed_runs(D, cs):
    runs, at = [], 0
    for start, n in _packed_pieces(D):
        if start is None:
            runs.append((at, None, 0, n))
            at += n
        while start is not None and n:
            chip, off = divmod(start, cs)
            take = min(n, cs - off)
            runs.append((at, chip, off, take))
            at, start, n = at + take, start + take, n - take
    return runs


def pack_w_in(shards, layer):
    _, _, cs, D = shards.shape
    runs = _packed_runs(D, cs)
    width = runs[-1][0] + runs[-1][3]
    tc = _tile(D, (256, 128))

    def body(s_ref, o_ref):
        for dst, chip, off, rows in runs:
            if chip is None:
                o_ref[dst:dst + rows, :] = jnp.zeros((rows, tc), s_ref.dtype)
            else:
                o_ref[dst:dst + rows, :] = s_ref[chip, off:off + rows, :]

    return pl.pallas_call(
        body, name="pack_w_in", grid=(D // tc,),
        in_specs=[pl.BlockSpec((N_CHIPS, None, cs, tc), lambda j: (0, layer, 0, j))],
        out_specs=pl.BlockSpec((width, tc), lambda j: (0, j)),
        out_shape=SDS((width, D), shards.dtype), compiler_params=_params("arbitrary"),
    )(shards)


def unpack_w_in(p, cs):
    width, D = p.shape
    runs = _packed_runs(D, cs)
    tc = _tile(D, (256, 128))

    def body(p_ref, o_ref):
        for src, chip, off, rows in runs:
            if chip is not None:
                o_ref[chip, off:off + rows, :] = p_ref[src:src + rows, :]

    return pl.pallas_call(
        body, name="unpack_w_in", grid=(D // tc,),
        in_specs=[pl.BlockSpec((width, tc), lambda j: (0, j))],
        out_specs=pl.BlockSpec((N_CHIPS, cs, tc), lambda j: (0, 0, j)),
        out_shape=SDS((N_CHIPS, cs, D), p.dtype), compiler_params=_params("arbitrary"),
    )(p)


def _split_flat(vec, shapes):
    out, at = [], 0
    for shp in shapes:
        n = int(np.prod(shp))
        out.append(vec[at:at + n].reshape(shp))
        at += n
    return out


def kernel(x, attn_norm, w_in, b_forget, b_gate, w_proj_attn, pool_w, pool_scale, w_proj_pool, conv_w, w_proj_conv, w_out, ffn_norm, w_gate_up, w_down, final_norm, loss_target, m_attn_norm, m_w_in, m_b_forget, m_b_gate, m_w_proj_attn, m_pool_w, m_pool_scale, m_w_proj_pool, m_conv_w, m_w_proj_conv, m_w_out, m_ffn_norm, m_w_gate_up, m_w_down, m_final_norm, v_attn_norm, v_w_in, v_b_forget, v_b_gate, v_w_proj_attn, v_pool_w, v_pool_scale, v_w_proj_pool, v_conv_w, v_w_proj_conv, v_w_out, v_ffn_norm, v_w_gate_up, v_w_down, v_final_norm):
    weights = dict(attn_norm=attn_norm, w_in=w_in, b_forget=b_forget, b_gate=b_gate, w_proj_attn=w_proj_attn,
                   pool_w=pool_w, pool_scale=pool_scale, w_proj_pool=w_proj_pool, conv_w=conv_w,
                   w_proj_conv=w_proj_conv, w_out=w_out, ffn_norm=ffn_norm, w_gate_up=w_gate_up, w_down=w_down,
                   final_norm=final_norm)
    mom_m = dict(attn_norm=m_attn_norm, w_in=m_w_in, b_forget=m_b_forget, b_gate=m_b_gate, w_proj_attn=m_w_proj_attn,
                 pool_w=m_pool_w, pool_scale=m_pool_scale, w_proj_pool=m_w_proj_pool, conv_w=m_conv_w,
                 w_proj_conv=m_w_proj_conv, w_out=m_w_out, ffn_norm=m_ffn_norm, w_gate_up=m_w_gate_up,
                 w_down=m_w_down, final_norm=m_final_norm)
    mom_v = dict(attn_norm=v_attn_norm, w_in=v_w_in, b_forget=v_b_forget, b_gate=v_b_gate, w_proj_attn=v_w_proj_attn,
                 pool_w=v_pool_w, pool_scale=v_pool_scale, w_proj_pool=v_w_proj_pool, conv_w=v_conv_w,
                 w_proj_conv=v_w_proj_conv, w_out=v_w_out, ffn_norm=v_ffn_norm, w_gate_up=v_w_gate_up,
                 w_down=v_w_down, final_norm=v_final_norm)

    Bl, S, D = x.shape
    T = Bl * S
    L = w_in.shape[0]
    F = w_down.shape[1] * N_CHIPS
    lay = _proj_layout(D)
    cst = _placement_constants()
    assert L == N_LAYERS and S % ATTN_BLOCK == 0 and F % (2 * LANES) == 0 and D % BRANCH_W == 0
    assert w_in.shape[2] * N_CHIPS == _REF["g"] + 3 * D and conv_w.shape[2] == GROUP_W

    send = {n: weights[n].astype(BF16) for n in BIG}
    send["conv_w"] = conv_w
    me_chip = 2 * lax.axis_index("x") + lax.axis_index("y")
    send["w_in"] = w_in.transpose(0, 2, 1).astype(BF16)
    pool = gather_buffers(send)
    gather_now(pool, [("w_in", 0)])
    rest = ("w_out", "w_proj_attn", "w_proj_pool", "w_gate_up", "w_proj_conv", "conv_w")
    late = ("w_out", "w_proj_attn", "w_proj_pool", "w_proj_conv", "conv_w")
    jobs = lambda kind, names, layer: [(kind, n, layer) for n in names]
    carried = {
        ("in_proj", 0): jobs("ici", rest, 0),
        ("attn_prep", 0): jobs("fwd", rest, 0),
        ("attn_fwd", 0): jobs("ici", ("w_in",), 1) + jobs("ici", ("w_down",), 0),
        ("poolconv_fwd", 0): jobs("fwd", ("w_in",), 1) + jobs("fwd", ("w_down",), 0),
        ("mix_fwd", 0): jobs("ici", ("w_gate_up",), 1),
        ("gate_up_proj", 0): jobs("ici", ("w_down",) + late, 1),
        ("ffn_down_fwd", 0): jobs("fwd", ("w_gate_up",), 1),
        ("in_proj", 1): jobs("fwd", ("w_down",) + late, 1),
    }
    carry = lambda call, layer: Hosted(pool, carried[call, layer]) if (call, layer) in carried else None
    w_down_f = lambda: pool["w_down"].reshape(L, F, D)
    pool_w_b = pool_w.astype(BF16)
    an3, fn3 = attn_norm.reshape(L, 1, D), ffn_norm.reshape(L, 1, D)
    bg3, ps3 = b_gate.reshape(L, 1, 3 * D), pool_scale.reshape(L, 1, BRANCH_W)
    bf3 = jnp.pad(b_forget, ((0, 0), (0, LANES - HEADS))).reshape(L, 1, LANES)

    xs = x.reshape(T, D)
    saved = []
    w_in_p = []
    for l in range(L):
        w_in_p.append(pack_w_in(pool["w_in"], l))
        proj, h = norm_matmul(xs, an3, w_in_p[l], l, "rows", "in_proj", carry("in_proj", l))
        proj3 = proj.reshape(Bl, S, lay["width"])
        qa, ka = attn_prep(proj3, bf3, l, cst, lay, carry("attn_prep", l))
        ao, lse = attn_fwd(qa, ka, proj3, lay, carry("attn_fwd", l))
        po, co = poolconv_fwd(proj3, pool_w_b, ps3, pool["conv_w"], l, lay, carry("poolconv_fwd", l))
        ao2, po2, co2 = (a.reshape(T, BRANCH_W) for a in (ao, po, co))
        x1, ys, mixed = mix_fwd(ao2, po2, co2, proj, bg3, pool["w_proj_attn"], pool["w_proj_pool"],
                                pool["w_proj_conv"], pool["w_out"], l, xs, carry("mix_fwd", l))
        ab, h2 = norm_matmul(x1, fn3, pool["w_gate_up"], l, "by_shard", "gate_up_proj", carry("gate_up_proj", l))
        x2, s_act = ffn_down_fwd(ab, w_down_f(), l, x1, carry("ffn_down_fwd", l))
        saved.append(dict(x=xs, proj=proj, proj3=proj3, h=h, qa=qa, ka=ka, ao=ao, lse=lse, ao2=ao2, po2=po2,
                          co2=co2, ys=ys, mixed=mixed, x1=x1, ab=ab, h2=h2, s=s_act))
        xs = x2
    w_gu, w_o, conv_w_g = pool["w_gate_up"], pool["w_out"], pool["conv_w"]
    wpa, wpp, wpc = pool["w_proj_attn"], pool["w_proj_pool"], pool["w_proj_conv"]
    w_down_f = w_down_f()

    loss_row, dx, dxb, g_final = loss_head(xs, final_norm.reshape(1, D), loss_target.reshape(T, D))
    loss = lax.psum(loss_row[0, 0], AXES)

    reduced_names = tuple(n for n in BIG if n != "conv_w")
    early_names = tuple(n for n in reduced_names if n != "w_in")
    where = jnp.stack([lax.axis_index("c"), me_chip]).astype(jnp.int32)
    rs = {}

    def reduce_begin(layer, grads):
        for n, g in grads.items():
            g5 = g.reshape((1, N_CHIPS, 2, -1) + g.shape[-1:])
            rs["g%d:%s" % (layer, n)] = g5
            for role in "ra":
                rs["%s%d:%s" % (role, layer, n)] = lax.empty((N_CHIPS,) + g5.shape[3:], BF16)

    swap_jobs = lambda layer, names: [("swap", "g%d:%s" % (layer, n), "r%d:%s" % (layer, n), 0) for n in names]
    xchg_jobs = lambda layer, names: [("xchg", "s%d:%s" % (layer, n), "a%d:%s" % (layer, n)) for n in names]
    join_jobs = lambda layer, names: [("join", "o:" + n, layer) for n in names]

    def pair_sums(layer, names):
        for n in names:
            rs["s%d:%s" % (layer, n)] = add_pair(rs["g%d:%s" % (layer, n)], 0, where, rs["r%d:%s" % (layer, n)],
                                                 "add_pair_" + n)

    def chip_sums(layer, names, slot, n_slots):
        for n in names:
            rs["o:" + n] = add_chips(rs["a%d:%s" % (layer, n)], rs["s%d:%s" % (layer, n)], slot, where, n_slots,
                                     rs.get("o:" + n), "add_chips_" + n)

    small = {n: [None] * L for n in REPLICATED if n != "final_norm"}
    g_conv = [None] * L
    to3 = lambda a: a.reshape(Bl, S, -1)
    for l in reversed(range(L)):
        sv = saved[l]
        behind = (lambda jobs: Hosted(rs, jobs)) if l == 0 else (lambda jobs: None)
        grads = {}
        da, db = ffn_down_bwd(dxb, w_down_f, l, sv["ab"], behind(swap_jobs(1, reduced_names)))
        if l == 0:
            pair_sums(1, reduced_names)
        grads["w_down"] = matmul_tn(sv["s"], [dxb], "grad_w_down", hosted=behind(xchg_jobs(
            1, ("w_down", "w_out", "w_proj_attn", "w_proj_pool", "w_proj_conv"))))
        grads["w_gate_up"] = matmul_tn(sv["h2"], [da, db], "grad_w_gate_up", by_dest=True, tn=2 * F // N_CHIPS,
                                       tk=_tile(T, (1024, 512, 256)), hosted=behind(xchg_jobs(1, ("w_gate_up",))))
        dx1, dx1b, g_fn = matmul_nt_normbwd([da, db], w_gu, l, "by_shard", sv["x1"], fn3, dx, "gate_up_bwd",
                                            behind(xchg_jobs(1, ("w_in",))))
        small["ffn_norm"][l] = g_fn[0]
        if l == 0:
            chip_sums(1, reduced_names, 1, L)
        dys, dproj, dao, dpo, dco, g_bg = mix_bwd(dx1b, w_o, sv["proj"], bg3, sv["ys"], wpa, wpp, wpc, l,
                                                  lay["width"], behind(join_jobs(1, reduced_names)))
        small["b_gate"][l] = g_bg[0]
        grads["w_out"] = matmul_tn(sv["mixed"], [dx1b], "grad_w_out")
        for n, (name, br) in enumerate((("w_proj_attn", sv["ao2"]), ("w_proj_pool", sv["po2"]),
                                        ("w_proj_conv", sv["co2"]))):
            grads[name] = matmul_tn(br, [dys], "grad_" + name, b_col0=n * D, n_cols=D, by_dest=True,
                                    tn=D // N_CHIPS)
        if l == 0:
            reduce_begin(0, grads)
        dqa, dka, dproj3 = attn_bwd(sv["qa"], sv["ka"], sv["proj3"], to3(dao), sv["ao"], sv["lse"], to3(dproj), lay,
                                    behind(swap_jobs(0, early_names)))
        if l == 0:
            pair_sums(0, early_names)
        dproj3, g_bf = attn_post(dqa, dka, sv["proj3"], bf3, l, dproj3, cst, lay)
        small["b_forget"][l] = g_bf[0, :HEADS]
        dproj3, g_pw, g_ps, g_conv[l] = poolconv_bwd(sv["proj3"], to3(dpo), to3(dco), pool_w_b, ps3, conv_w_g, l,
                                                     dproj3, lay)
        small["pool_w"][l], small["pool_scale"][l] = g_pw, g_ps[0]
        dproj = dproj3.reshape(T, lay["width"])
        grads["w_in"] = unpack_w_in(matmul_tn(dproj, [sv["h"]], "grad_w_in"), w_in.shape[2])
        dx, dxb, g_an = matmul_nt_normbwd([dproj], w_in_p[l], l, "rows", sv["x"], an3, dx1, "in_proj_bwd",
                                          behind(xchg_jobs(0, early_names)))
        small["attn_norm"][l] = g_an[0]
        reduce_begin(l, grads if l else {"w_in": grads["w_in"]})
    grad_x = dx.reshape(Bl, S, D)

    small_shapes = [weights[n].shape for n in REPLICATED] + [(L, N_CHIPS) + conv_w.shape[1:]]
    small_vec = jnp.concatenate([jnp.stack(small[n]).reshape(-1) for n in REPLICATED[:-1]]
                                + [g_final[0], jnp.stack(g_conv).reshape(-1)])
    n_small = small_vec.shape[0]
    small_vec = jnp.pad(small_vec, (0, -n_small % (2 * N_CHIPS * 16 * LANES))).astype(BF16)
    rs["g0:small"] = small_vec.reshape(1, N_CHIPS, 2, -1, LANES)
    for role in "ra":
        rs[role + "0:small"] = lax.empty((N_CHIPS,) + rs["g0:small"].shape[3:], BF16)
    last = ("w_in", "small")
    comm_now(rs, [swap_jobs(0, last)], "swap_grad_halves")
    pair_sums(0, last)
    comm_now(rs, [xchg_jobs(0, last)], "exchange_grad_chips")
    chip_sums(0, reduced_names, 0, L)
    chip_sums(0, ("small",), 0, 1)
    comm_now(rs, [join_jobs(0, reduced_names + ("small",))], "join_grad_halves")
    shard_grads = {n: rs["o:" + n].reshape((L, -1) + rs["o:" + n].shape[-1:]) for n in reduced_names}
    small_all = allgather_chips(rs["o:small"].reshape(-1, LANES), "allgather_small_grads").reshape(-1)[:n_small]
    *rep_list, conv_all = _split_flat(small_all, small_shapes)
    rep_grads = dict(zip(REPLICATED, rep_list))
    shard_grads["conv_w"] = lax.dynamic_index_in_dim(conv_all, me_chip, 1, keepdims=False)

    delta, new_m, new_v = {}, {}, {}
    for n in BIG:
        shp = weights[n].shape
        if n == "w_in":
            view, back = (lambda a: a.transpose(2, 0, 1)), (lambda a: a.transpose(1, 2, 0))
            g = shard_grads[n].transpose(1, 0, 2)
        else:
            view, back = (lambda a: a.reshape(-1, shp[-1])), (lambda a: a.reshape(shp))
            g = view(shard_grads[n])
        d, nm, nv = adamw(view(weights[n]), g, view(mom_m[n]), view(mom_v[n]), "adamw_" + n)
        delta[n], new_m[n], new_v[n], shard_grads[n] = back(d), back(nm), back(nv), back(g)

    def rows(d):
        vec = jnp.concatenate([d[n].reshape(-1) for n in REPLICATED])
        return jnp.pad(vec, (0, -vec.shape[0] % (8 * LANES))).reshape(-1, LANES)

    outs = adamw(rows(weights), rows(rep_grads), rows(mom_m), rows(mom_v), "adamw_replicated")
    for res, o in zip((delta, new_m, new_v), outs):
        res.update(zip(REPLICATED, _split_flat(o.reshape(-1), small_shapes[:len(REPLICATED)])))
    all_grads = {**shard_grads, **rep_grads}

    return (loss, grad_x, *[all_grads[n] for n in ORDER], *[delta[n] for n in ORDER],
            *[new_m[n] for n in ORDER], *[new_v[n] for n in ORDER])
```

```python
import numpy as np
import jax
import jax.numpy as jnp
from jax import lax
from jax.experimental import pallas as pl
from jax.experimental.pallas import tpu as pltpu

F32, BF16 = jnp.float32, jnp.bfloat16
SDS = jax.ShapeDtypeStruct
MESH = pl.DeviceIdType.MESH
AXES = ("x", "y", "c")
N_CHIPS = 4
N_LAYERS = 2
LANES = 128
VMEM_LIMIT = 48 * 1024 * 1024

HEADS, HEAD_DIM = 8, 64
HEAD_PAD = 128
BRANCH_W = 512
GROUP_W = 128
N_GROUPS = BRANCH_W // GROUP_W
POOL_WINDOWS = (2, 4, 8, 16)
F_PAD = 512
ATTN_BLOCK = 256
RMS_EPS = 1e-6
NEG_INF = -1e30
ADAM_LR, ADAM_B1, ADAM_B2, ADAM_EPS, ADAM_WD, ADAM_STEP = 0.001, 0.9, 0.999, 1e-08, 0.01, 10

NT = (((1,), (1,)), ((), ()))
TN = (((0,), (0,)), ((), ()))
_ANY = pl.BlockSpec(memory_space=pl.ANY)


def _tile(n, prefs):
    for p in prefs:
        if n % p == 0:
            return p
    raise ValueError(f"no tile of {prefs} divides {n}")


def _params(*sem):
    return pltpu.CompilerParams(dimension_semantics=sem, vmem_limit_bytes=VMEM_LIMIT)


def _sigmoid(z):
    return 0.5 * jnp.tanh(0.5 * z) + 0.5


def _split3(x):
    h1 = x.astype(BF16)
    r1 = x - h1.astype(F32)
    h2 = r1.astype(BF16)
    h3 = (r1 - h2.astype(F32)).astype(BF16)
    return h1, h2, h3


def _position():
    return lax.axis_index("x"), lax.axis_index("y"), lax.axis_index("c")


def _other_chips(x, y):
    return [(1 - x, y), (x, 1 - y), (1 - x, 1 - y)]


def _remote(src, dst, send_sem, recv_sem, device):
    return pltpu.make_async_remote_copy(src_ref=src, dst_ref=dst, send_sem=send_sem, recv_sem=recv_sem,
                                        device_id=device, device_id_type=MESH)


ROW_SHARDED = ("w_out", "w_down")
FETCHER = dict(w_in=0, w_out=0, w_proj_attn=0, w_proj_pool=0, w_gate_up=1, w_down=1, w_proj_conv=1, conv_w=1)


class Hosted:
    def __init__(self, pool, jobs):
        self.pool, self.jobs = pool, list(jobs)
        names = set()
        for job in self.jobs:
            names.update(job[1:3] if job[0] in ("swap", "xchg") else job[1:2])
        self.names = sorted(names)


def _hosted_plan(hosted, refs, send_sems, recv_sems):
    x, y, c = _position()
    me = 2 * x + y
    others = _other_chips(x, y)
    sibling = (x, y, 1 - c)
    plan = []
    for j, job in enumerate(hosted.jobs):
        kind = job[0]
        sems = lambda k, j=j: (send_sems.at[j, k], recv_sems.at[j, k])
        if kind in ("ici", "fwd"):
            _, name, layer = job
            ref = refs[name]
            win = (lambda chip, ref=ref, layer=layer: ref.at[layer, chip]) if name in ROW_SHARDED else (
                lambda chip, ref=ref, layer=layer: ref.at[chip, layer])
            mine = c == FETCHER[name]
            if kind == "ici":
                sends = [_remote(win(me), win(me), *sems(k), (px, py, c)) for k, (px, py) in enumerate(others)]
                arrivals = [_remote(win(2 * px + py), win(2 * px + py), *sems(k), (px, py, c))
                            for k, (px, py) in enumerate(others)]
                plan.append((mine, sends, arrivals, []))
            else:
                sends = [_remote(win(2 * px + py), win(2 * px + py), *sems(k), sibling)
                         for k, (px, py) in enumerate(others)]
                plan.append((mine, sends, [], sends))
        elif kind == "swap":
            _, src, dst, layer = job
            cp = _remote(refs[src].at[layer, :, 1 - c], refs[dst], *sems(0), sibling)
            plan.append((True, [cp], [cp], []))
        elif kind == "xchg":
            _, src, dst = job
            sends = [_remote(refs[src].at[2 * px + py], refs[dst].at[me], *sems(k), (px, py, c))
                     for k, (px, py) in enumerate(others)]
            arrivals = [_remote(refs[src].at[me], refs[dst].at[2 * px + py], *sems(k), (px, py, c))
                        for k, (px, py) in enumerate(others)]
            plan.append((True, sends, arrivals, []))
        else:
            _, name, layer = job
            ref = refs[name]
            cp = _remote(ref.at[layer, c], ref.at[layer, c], *sems(0), sibling)
            arrival = _remote(ref.at[layer, c], ref.at[layer, 1 - c], *sems(0), sibling)
            plan.append((True, [cp], [arrival], []))
    return plan


def _hosted_start(plan, now):
    for mine, sends, _, _ in plan:
        @pl.when(now & mine)
        def _(sends=sends):
            for cp in sends:
                cp.start()


def _hosted_finish(plan, now):
    for mine, sends, arrivals, sibling_arrivals in plan:
        @pl.when(now & mine)
        def _(sends=sends, arrivals=arrivals):
            for cp in arrivals:
                cp.wait_recv()
            for cp in sends:
                cp.wait_send()

        if sibling_arrivals:
            @pl.when(now & jnp.logical_not(mine))
            def _(sibling_arrivals=sibling_arrivals):
                for cp in sibling_arrivals:
                    cp.wait_recv()


def _pcall(body, hosted, *, name, grid, in_specs, out_specs, out_shape, semantics, scratch_shapes=(), aliases=None):
    aliases = dict(aliases or {})
    if hosted is None or not hosted.jobs:
        return pl.pallas_call(body, name=name, grid=grid, in_specs=in_specs, out_specs=out_specs,
                              out_shape=out_shape, scratch_shapes=list(scratch_shapes),
                              input_output_aliases=aliases, compiler_params=_params(*semantics))
    single = not isinstance(out_shape, (list, tuple))
    out_specs_l = [out_specs] if single else list(out_specs)
    out_shape_l = [out_shape] if single else list(out_shape)
    n_in, n_out, n_buf, n_job = len(in_specs), len(out_specs_l), len(hosted.names), len(hosted.jobs)

    def carrying(*refs):
        ins, outs = refs[:n_in], refs[n_in + n_buf:n_in + n_buf + n_out]
        bufs = refs[n_in + n_buf + n_out:n_in + 2 * n_buf + n_out]
        rest = refs[n_in + 2 * n_buf + n_out:]
        scratch, send_sems, recv_sems = rest[:-2], rest[-2], rest[-1]
        first, last = True, True
        for axis, size in enumerate(grid):
            first = first & (pl.program_id(axis) == 0)
            last = last & (pl.program_id(axis) == size - 1)
        plan = _hosted_plan(hosted, dict(zip(hosted.names, bufs)), send_sems, recv_sems)
        _hosted_start(plan, first)
        body(*ins, *outs, *scratch)
        _hosted_finish(plan, last)

    def run(*args):
        bufs = [hosted.pool[n] for n in hosted.names]
        sem = pltpu.SemaphoreType.DMA
        res = pl.pallas_call(
            carrying, name=name, grid=grid, in_specs=list(in_specs) + [_ANY] * n_buf,
            out_specs=out_specs_l + [_ANY] * n_buf,
            out_shape=out_shape_l + [SDS(b.shape, b.dtype) for b in bufs],
            scratch_shapes=list(scratch_shapes) + [sem((n_job, 3)), sem((n_job, 3))],
            input_output_aliases={**aliases, **{n_in + i: n_out + i for i in range(n_buf)}},
            compiler_params=pltpu.CompilerParams(dimension_semantics=semantics, vmem_limit_bytes=VMEM_LIMIT,
                                                 has_side_effects=True),
        )(*args, *bufs)
        hosted.pool.update(zip(hosted.names, res[n_out:]))
        return res[0] if single else res[:n_out]

    return run


def _dot(a, b):
    return jnp.dot(a, b, preferred_element_type=F32)


def _dot_nt(a, b):
    return lax.dot_general(a, b, NT, preferred_element_type=F32)


def _dot_tn(a, b):
    return lax.dot_general(a, b, TN, preferred_element_type=F32)


def norm_matmul(x, gain, w, layer, kind, name, hosted=None):
    T, D = x.shape
    if kind == "by_shard":
        tn = w.shape[3]
        N = N_CHIPS * tn
        w_spec = pl.BlockSpec((None, None, D, tn), lambda i, j: (j, layer, 0, 0))
        mm = _dot
    else:
        N = w.shape[0]
        tn = _tile(N, (1024, 512, 256, 128))
        w_spec = pl.BlockSpec((tn, D), lambda i, j: (j, 0))
        mm = _dot_nt
    tm = _tile(T, (1024, 512, 256, 128))

    def body(x_ref, g_ref, w_ref, y_ref, h_ref):
        @pl.when(pl.program_id(1) == 0)
        def _():
            xf = x_ref[...]
            r = lax.rsqrt(jnp.mean(xf * xf, axis=-1, keepdims=True) + RMS_EPS)
            h_ref[...] = ((xf * r) * g_ref[...]).astype(BF16)

        y_ref[...] = mm(h_ref[...], w_ref[...]).astype(BF16)

    return _pcall(
        body, hosted, name=name, grid=(T // tm, N // tn),
        in_specs=[pl.BlockSpec((tm, D), lambda i, j: (i, 0)),
                  pl.BlockSpec((None, 1, D), lambda i, j: (layer, 0, 0)),
                  w_spec],
        out_specs=[pl.BlockSpec((tm, tn), lambda i, j: (i, j)),
                   pl.BlockSpec((tm, D), lambda i, j: (i, 0))],
        out_shape=[SDS((T, N), BF16), SDS((T, D), BF16)],
        semantics=("arbitrary", "arbitrary"),
    )(x, gain, w)


def matmul_nt_normbwd(dys, w, layer, kind, x, gain, dres, name, hosted=None):
    T, D = x.shape
    width = dys[0].shape[1]
    if kind == "by_shard":
        tk = w.shape[3]
        w_spec = pl.BlockSpec((None, None, D, tk), lambda i, k: (k, layer, 0, 0))
        mm = _dot_nt
    else:
        tk = _tile(width, (1024, 512, 256, 128))
        w_spec = pl.BlockSpec((tk, D), lambda i, k: (k, 0))
        mm = _dot
    per = width // tk
    nk = per * len(dys)
    tm = _tile(T, (512, 256, 128))
    n_dy = len(dys)

    def dy_spec(p):
        return pl.BlockSpec((tm, tk), lambda i, k: (i, jnp.clip(k - p * per, 0, per - 1)))

    def body(*refs):
        dy_refs = refs[:n_dy]
        w_ref, x_ref, g_ref, dres_ref, dx_ref, dxb_ref, dg_ref, acc_ref = refs[n_dy:]
        i, k = pl.program_id(0), pl.program_id(1)

        @pl.when(k == 0)
        def _():
            acc_ref[...] = jnp.zeros_like(acc_ref)

        for p in range(n_dy):
            @pl.when((k >= p * per) & (k < (p + 1) * per))
            def _(p=p):
                acc_ref[...] += mm(dy_refs[p][...], w_ref[...])

        @pl.when(k == nk - 1)
        def _():
            xf = x_ref[...]
            r = lax.rsqrt(jnp.mean(xf * xf, axis=-1, keepdims=True) + RMS_EPS)
            xhat = xf * r
            dh = acc_ref[...]
            dhg = dh * g_ref[...]
            dx = dres_ref[...] + r * (dhg - xhat * jnp.mean(dhg * xhat, axis=-1, keepdims=True))
            dx_ref[...] = dx
            dxb_ref[...] = dx.astype(BF16)
            part = jnp.sum(dh * xhat, axis=0, keepdims=True)

            @pl.when(i == 0)
            def _():
                dg_ref[...] = part

            @pl.when(i > 0)
            def _():
                dg_ref[...] += part

    row = pl.BlockSpec((tm, D), lambda i, k: (i, 0))
    return _pcall(
        body, hosted, name=name, grid=(T // tm, nk),
        in_specs=[dy_spec(p) for p in range(n_dy)] + [
            w_spec, row, pl.BlockSpec((None, 1, D), lambda i, k: (layer, 0, 0)), row],
        out_specs=[row, row, pl.BlockSpec((1, D), lambda i, k: (0, 0))],
        out_shape=[SDS((T, D), F32), SDS((T, D), BF16), SDS((1, D), F32)],
        scratch_shapes=[pltpu.VMEM((tm, D), F32)],
        semantics=("arbitrary", "arbitrary"),
    )(*dys, w, x, gain, dres)


def matmul_tn(a, bs, name, b_col0=0, n_cols=None, by_dest=False, tn=None, tk=None, hosted=None):
    T, M = a.shape
    width = bs[0].shape[1]
    N = n_cols if n_cols else width * len(bs)
    tm = _tile(M, (1024, 512, 256, 128))
    tn = tn or _tile(N, (512, 256, 128))
    tk = tk or _tile(T, (4096, 2048, 1024, 512, 256))
    assert b_col0 % tn == 0 and width % tn == 0
    j0, per, nk, n_b = b_col0 // tn, width // tn, T // tk, len(bs)

    def b_spec(p):
        return pl.BlockSpec((tk, tn), lambda i, j, k: (k, jnp.clip(j0 + j - p * per, 0, per - 1)))

    def body(*refs):
        a_ref, b_refs = refs[0], refs[1:1 + n_b]
        o_ref, acc_ref = refs[-2], refs[-1]
        j, k = pl.program_id(1), pl.program_id(2)

        @pl.when(k == 0)
        def _():
            acc_ref[...] = jnp.zeros_like(acc_ref)

        for p in range(n_b):
            @pl.when((j0 + j >= p * per) & (j0 + j < (p + 1) * per))
            def _(p=p):
                acc_ref[...] += _dot_tn(a_ref[...], b_refs[p][...])

        @pl.when(k == nk - 1)
        def _():
            o_ref[...] = acc_ref[...].astype(BF16)

    if by_dest:
        cs = N // N_CHIPS
        npd = cs // tn
        out_shape = SDS((N_CHIPS, M, cs), BF16)
        out_spec = pl.BlockSpec((None, tm, tn), lambda i, j, k: (j // npd, i, j % npd))
    else:
        out_shape = SDS((M, N), BF16)
        out_spec = pl.BlockSpec((tm, tn), lambda i, j, k: (i, j))
    return _pcall(
        body, hosted, name=name, grid=(M // tm, N // tn, nk),
        in_specs=[pl.BlockSpec((tk, tm), lambda i, j, k: (k, i))] + [b_spec(p) for p in range(n_b)],
        out_specs=out_spec, out_shape=out_shape,
        scratch_shapes=[pltpu.VMEM((tm, tn), F32)],
        semantics=("arbitrary", "arbitrary", "arbitrary"),
    )(a, *bs)


def ffn_down_fwd(ab, w_down, layer, x1, hosted=None):
    T, D = x1.shape
    F = w_down.shape[1]
    tm = _tile(T, (512, 256, 128))
    tk = F // 2
    nk = F // tk

    def body(a_ref, b_ref, w_ref, x_ref, x2_ref, s_ref, acc_ref):
        k = pl.program_id(1)

        @pl.when(k == 0)
        def _():
            acc_ref[...] = x_ref[...]

        a = a_ref[...].astype(F32)
        s = (a * _sigmoid(a) * b_ref[...].astype(F32)).astype(BF16)
        s_ref[...] = s
        acc_ref[...] += _dot(s, w_ref[...])

        @pl.when(k == nk - 1)
        def _():
            x2_ref[...] = acc_ref[...]

    return _pcall(
        body, hosted, name="ffn_down_fwd", grid=(T // tm, nk),
        in_specs=[pl.BlockSpec((tm, tk), lambda i, k: (i, k)),
                  pl.BlockSpec((tm, tk), lambda i, k: (i, nk + k)),
                  pl.BlockSpec((None, tk, D), lambda i, k: (layer, k, 0)),
                  pl.BlockSpec((tm, D), lambda i, k: (i, 0))],
        out_specs=[pl.BlockSpec((tm, D), lambda i, k: (i, 0)),
                   pl.BlockSpec((tm, tk), lambda i, k: (i, k))],
        out_shape=[SDS((T, D), F32), SDS((T, F), BF16)],
        scratch_shapes=[pltpu.VMEM((tm, D), F32)],
        semantics=("arbitrary", "arbitrary"),
    )(ab, ab, w_down, x1)


def ffn_down_bwd(dx2b, w_down, layer, ab, hosted=None):
    T, D = dx2b.shape
    F = w_down.shape[1]
    tm = _tile(T, (512, 256, 128))
    tn = F // 2
    nj = F // tn

    def body(dx_ref, w_ref, a_ref, b_ref, da_ref, db_ref):
        ds = _dot_nt(dx_ref[...], w_ref[...])
        a = a_ref[...].astype(F32)
        sg = _sigmoid(a)
        da_ref[...] = (ds * b_ref[...].astype(F32) * (sg * (1.0 + a * (1.0 - sg)))).astype(BF16)
        db_ref[...] = (ds * (a * sg)).astype(BF16)

    blk = pl.BlockSpec((tm, tn), lambda i, j: (i, j))
    return _pcall(
        body, hosted, name="ffn_down_bwd", grid=(T // tm, nj),
        in_specs=[pl.BlockSpec((tm, D), lambda i, j: (i, 0)),
                  pl.BlockSpec((None, tn, D), lambda i, j: (layer, j, 0)),
                  blk, pl.BlockSpec((tm, tn), lambda i, j: (i, nj + j))],
        out_specs=[blk, blk],
        out_shape=[SDS((T, F), BF16), SDS((T, F), BF16)],
        semantics=("arbitrary", "arbitrary"),
    )(dx2b, w_down, ab, ab)


def _mix_specs(tm, D, layer):
    cs = D // N_CHIPS
    row = lambda w: pl.BlockSpec((tm, w), lambda i: (i, 0))
    wp = pl.BlockSpec((N_CHIPS, None, BRANCH_W, cs), lambda i: (0, layer, 0, 0))
    wo = pl.BlockSpec((None, N_CHIPS, cs, D), lambda i: (layer, 0, 0, 0))
    bg = pl.BlockSpec((None, 1, 3 * D), lambda i: (layer, 0, 0))
    return row, wp, wo, bg


def mix_fwd(ao, po, co, proj, b_gate, wpa, wpp, wpc, w_out, layer, x, hosted=None):
    T, D = x.shape
    cs = D // N_CHIPS
    tm = _tile(T, (256, 128))
    row, wp, wo, bg = _mix_specs(tm, D, layer)

    def body(ao_ref, po_ref, co_ref, g_ref, bg_ref, wpa_ref, wpp_ref, wpc_ref, wo_ref, x_ref,
             x1_ref, ys_ref, mixed_ref):
        mixed = jnp.zeros((tm, D), F32)
        for n, (br, wp_ref) in enumerate(((ao_ref, wpa_ref), (po_ref, wpp_ref), (co_ref, wpc_ref))):
            y = jnp.concatenate([_dot(br[...], wp_ref[j]) for j in range(N_CHIPS)], axis=1)
            cols = slice(n * D, (n + 1) * D)
            gate = _sigmoid(g_ref[:, cols].astype(F32) + bg_ref[:, cols])
            ys_ref[:, cols] = y.astype(BF16)
            mixed = mixed + gate * y
        mb = mixed.astype(BF16)
        mixed_ref[...] = mb
        acc = x_ref[...]
        for j in range(N_CHIPS):
            acc = acc + _dot(mb[:, j * cs:(j + 1) * cs], wo_ref[j])
        x1_ref[...] = acc

    return _pcall(
        body, hosted, name="mix_fwd", grid=(T // tm,),
        in_specs=[row(BRANCH_W), row(BRANCH_W), row(BRANCH_W), row(3 * D), bg, wp, wp, wp, wo, row(D)],
        out_specs=[row(D), row(3 * D), row(D)],
        out_shape=[SDS((T, D), F32), SDS((T, 3 * D), BF16), SDS((T, D), BF16)],
        semantics=("arbitrary",),
    )(ao, po, co, proj, b_gate, wpa, wpp, wpc, w_out, x)


def mix_bwd(dx1b, w_out, proj, b_gate, ys, wpa, wpp, wpc, layer, width, hosted=None):
    T, D = dx1b.shape
    cs = D // N_CHIPS
    tm = _tile(T, (256, 128))
    row, wp, wo, bg = _mix_specs(tm, D, layer)

    def body(dx_ref, wo_ref, g_ref, bg_ref, ys_ref, wpa_ref, wpp_ref, wpc_ref,
             dys_ref, dg_ref, dao_ref, dpo_ref, dco_ref, dbg_ref):
        i = pl.program_id(0)
        dx = dx_ref[...]
        dmixed = jnp.concatenate([_dot_nt(dx, wo_ref[j]) for j in range(N_CHIPS)], axis=1)
        for n, (wp_ref, dbr) in enumerate(((wpa_ref, dao_ref), (wpp_ref, dpo_ref), (wpc_ref, dco_ref))):
            cols = slice(n * D, (n + 1) * D)
            gate = _sigmoid(g_ref[:, cols].astype(F32) + bg_ref[:, cols])
            dy = (dmixed * gate).astype(BF16)
            dys_ref[:, cols] = dy
            dgp = dmixed * ys_ref[:, cols].astype(F32) * gate * (1.0 - gate)
            dg_ref[:, cols] = dgp.astype(BF16)
            part = jnp.sum(dgp, axis=0, keepdims=True)

            @pl.when(i == 0)
            def _():
                dbg_ref[:, cols] = part

            @pl.when(i > 0)
            def _():
                dbg_ref[:, cols] += part

            acc = jnp.zeros((tm, BRANCH_W), F32)
            for j in range(N_CHIPS):
                acc = acc + _dot_nt(dy[:, j * cs:(j + 1) * cs], wp_ref[j])
            dbr[...] = acc.astype(BF16)

    return _pcall(
        body, hosted, name="mix_bwd", grid=(T // tm,),
        in_specs=[row(D), wo, row(3 * D), bg, row(3 * D), wp, wp, wp],
        out_specs=[row(3 * D), row(3 * D), row(BRANCH_W), row(BRANCH_W), row(BRANCH_W),
                   pl.BlockSpec((1, 3 * D), lambda i: (0, 0))],
        out_shape=[SDS((T, 3 * D), BF16), SDS((T, width), BF16), SDS((T, BRANCH_W), BF16),
                   SDS((T, BRANCH_W), BF16), SDS((T, BRANCH_W), BF16), SDS((1, 3 * D), F32)],
        semantics=("arbitrary",),
    )(dx1b, w_out, proj, b_gate, ys, wpa, wpp, wpc)


def loss_head(x2, gain, target):
    T, D = x2.shape
    tm = _tile(T, (512, 256, 128))

    def body(x_ref, g_ref, t_ref, loss_ref, dx_ref, dxb_ref, dg_ref):
        i = pl.program_id(0)
        xf = x_ref[...]
        g = g_ref[...]
        r = lax.rsqrt(jnp.mean(xf * xf, axis=-1, keepdims=True) + RMS_EPS)
        xhat = xf * r
        diff = xhat * g - t_ref[...]
        part_loss = 0.5 * jnp.sum(jnp.mean(diff * diff, axis=-1, keepdims=True), axis=0, keepdims=True)
        dy = diff * (1.0 / D)
        dhg = dy * g
        dx = r * (dhg - xhat * jnp.mean(dhg * xhat, axis=-1, keepdims=True))
        dx_ref[...] = dx
        dxb_ref[...] = dx.astype(BF16)
        part_g = jnp.sum(dy * xhat, axis=0, keepdims=True)
        part_l = jnp.broadcast_to(part_loss, (1, LANES))

        @pl.when(i == 0)
        def _():
            dg_ref[...] = part_g
            loss_ref[...] = part_l

        @pl.when(i > 0)
        def _():
            dg_ref[...] += part_g
            loss_ref[...] += part_l

    row = pl.BlockSpec((tm, D), lambda i: (i, 0))
    return pl.pallas_call(
        body, name="loss_head", grid=(T // tm,),
        in_specs=[row, pl.BlockSpec((1, D), lambda i: (0, 0)), row],
        out_specs=[pl.BlockSpec((1, LANES), lambda i: (0, 0)), row, row, pl.BlockSpec((1, D), lambda i: (0, 0))],
        out_shape=[SDS((1, LANES), F32), SDS((T, D), F32), SDS((T, D), BF16), SDS((1, D), F32)],
        compiler_params=_params("arbitrary"),
    )(x2, gain, target)


def _placement_constants():
    w = HEADS * HEAD_PAD
    pq = np.zeros((BRANCH_W, w), np.float32)
    pk = np.zeros((BRANCH_W, w), np.float32)
    pfq = np.zeros((3, LANES, w), np.float32)
    pfk = np.zeros((3, LANES, w), np.float32)
    cq = np.zeros((1, w), np.float32)
    ck = np.zeros((1, w), np.float32)
    eq = np.zeros((w, LANES), np.float32)
    ek = np.zeros((w, LANES), np.float32)
    for h in range(HEADS):
        for d in range(HEAD_DIM):
            pq[h * HEAD_DIM + d, h * HEAD_PAD + d] = HEAD_DIM ** -0.5
            pk[h * HEAD_DIM + d, h * HEAD_PAD + d] = 1.0
        for i in range(3):
            pfq[i, h, h * HEAD_PAD + HEAD_DIM + i] = 1.0
            pfk[i, h, h * HEAD_PAD + HEAD_DIM + 3 + i] = -1.0
            cq[0, h * HEAD_PAD + HEAD_DIM + 3 + i] = 1.0
            ck[0, h * HEAD_PAD + HEAD_DIM + i] = 1.0
        eq[h * HEAD_PAD + HEAD_DIM, h] = 1.0
        ek[h * HEAD_PAD + HEAD_DIM + 3, h] = -1.0
    bf = lambda a: jnp.asarray(a, BF16)
    return dict(pq=bf(pq), pk=bf(pk), pfq=bf(pfq), pfk=bf(pfk), cq=jnp.asarray(cq), ck=jnp.asarray(ck),
                pqkt=bf(np.concatenate([pq.T, pk.T], axis=0)), eq=bf(eq), ek=bf(ek))


def attn_prep(proj3, bf_rows, layer, cst, lay, hosted=None):
    Bl, S, _ = proj3.shape
    ts = ATTN_BLOCK
    w = HEADS * HEAD_PAD

    def body(q_ref, k_ref, f_ref, bf_ref, pq_ref, pk_ref, pfq_ref, pfk_ref, cq_ref, ck_ref,
             qa_ref, ka_ref, carry_ref):
        @pl.when(pl.program_id(1) == 0)
        def _():
            carry_ref[...] = jnp.zeros_like(carry_ref)

        z = f_ref[...].astype(F32) + bf_ref[...]
        logf = jnp.minimum(z, 0.0) - jnp.log(1.0 + jnp.exp(-jnp.abs(z)))
        r = lax.broadcasted_iota(jnp.int32, (ts, ts), 0)
        c = lax.broadcasted_iota(jnp.int32, (ts, ts), 1)
        tri = jnp.where(r >= c, 1.0, 0.0).astype(BF16)
        fcum = carry_ref[...]
        for part in _split3(logf):
            fcum = fcum + _dot(tri, part)
        carry_ref[...] = fcum[ts - 1:ts, :]
        qa = _dot(q_ref[...], pq_ref[...]) + cq_ref[...]
        ka = _dot(k_ref[...], pk_ref[...]) + ck_ref[...]
        for i, part in enumerate(_split3(fcum)):
            qa = qa + _dot(part, pfq_ref[i])
            ka = ka + _dot(part, pfk_ref[i])
        qa_ref[...] = qa.astype(BF16)
        ka_ref[...] = ka.astype(BF16)

    cfull = lambda shape: pl.BlockSpec(shape, lambda b, s: (0,) * len(shape))
    return _pcall(
        body, hosted, name="attn_prep", grid=(Bl, S // ts),
        in_specs=[pl.BlockSpec((None, ts, BRANCH_W), lambda b, s: (b, s, lay["q"] // BRANCH_W)),
                  pl.BlockSpec((None, ts, BRANCH_W), lambda b, s: (b, s, lay["k"] // BRANCH_W)),
                  pl.BlockSpec((None, ts, LANES), lambda b, s: (b, s, lay["f"] // LANES)),
                  pl.BlockSpec((None, 1, LANES), lambda b, s: (layer, 0, 0)),
                  cfull((BRANCH_W, w)), cfull((BRANCH_W, w)),
                  cfull((3, LANES, w)), cfull((3, LANES, w)), cfull((1, w)), cfull((1, w))],
        out_specs=[pl.BlockSpec((None, ts, w), lambda b, s: (b, s, 0)),
                   pl.BlockSpec((None, ts, w), lambda b, s: (b, s, 0))],
        out_shape=[SDS((Bl, S, w), BF16), SDS((Bl, S, w), BF16)],
        scratch_shapes=[pltpu.VMEM((1, LANES), F32)],
        semantics=("arbitrary", "arbitrary"),
    )(proj3, proj3, proj3, bf_rows, cst["pq"], cst["pk"], cst["pfq"], cst["pfk"], cst["cq"], cst["ck"])


def attn_fwd(qa, ka, proj3, lay, hosted=None):
    Bl, S, _ = qa.shape
    tq = ATTN_BLOCK
    nq = S // tq
    pairs = HEADS // 2
    pw = 2 * HEAD_PAD
    vw = 2 * HEAD_DIM

    def body(qa_ref, ka_ref, v_ref, o_ref, lse_ref):
        row = lax.broadcasted_iota(jnp.int32, (tq, tq), 0)
        col = lax.broadcasted_iota(jnp.int32, (tq, tq), 1)
        causal = row <= col
        for i in range(nq):
            nk = (i + 1) * tq
            rows = slice(i * tq, nk)
            o_t = []
            for h in range(2):
                hs = slice(h * HEAD_PAD, (h + 1) * HEAD_PAD)
                st = _dot_nt(ka_ref[0:nk, hs], qa_ref[rows, hs])
                diag = jnp.where(causal, st[nk - tq:], NEG_INF)
                m = jnp.max(diag, axis=0, keepdims=True)
                if i:
                    m = jnp.maximum(m, jnp.max(st[:nk - tq], axis=0, keepdims=True))
                p_diag = jnp.exp(diag - m)
                l = jnp.sum(p_diag, axis=0, keepdims=True)
                if i:
                    p_top = jnp.exp(st[:nk - tq] - m)
                    l = l + jnp.sum(p_top, axis=0, keepdims=True)
                    p = jnp.concatenate([p_top.astype(BF16), p_diag.astype(BF16)], axis=0)
                else:
                    p = p_diag.astype(BF16)
                acc = _dot_tn(v_ref[0:nk, :], p)
                o_t.append(acc[h * HEAD_DIM:(h + 1) * HEAD_DIM, :] / l)
                lse_ref[h:h + 1, rows] = m + jnp.log(l)
            o_ref[rows, :] = jnp.concatenate(o_t, axis=0).T.astype(BF16)

    return _pcall(
        body, hosted, name="attn_fwd", grid=(Bl, pairs),
        in_specs=[pl.BlockSpec((None, S, pw), lambda b, p: (b, 0, p)),
                  pl.BlockSpec((None, S, pw), lambda b, p: (b, 0, p)),
                  pl.BlockSpec((None, S, vw), lambda b, p: (b, 0, lay["v"] // vw + p))],
        out_specs=[pl.BlockSpec((None, S, vw), lambda b, p: (b, 0, p)),
                   pl.BlockSpec((None, None, 2, S), lambda b, p: (b, p, 0, 0))],
        out_shape=[SDS((Bl, S, BRANCH_W), BF16), SDS((Bl, pairs, 2, S), F32)],
        semantics=("arbitrary", "arbitrary"),
    )(qa, ka, proj3)


def attn_bwd(qa, ka, proj3, dao, ao, lse, dproj3, lay, hosted=None):
    Bl, S, _ = qa.shape
    tk = ATTN_BLOCK
    nq = S // tk
    pairs = HEADS // 2
    pw = 2 * HEAD_PAD
    vw = 2 * HEAD_DIM

    def body(qa_ref, ka_ref, v_ref, do_ref, o_ref, lse_ref, _, dqa_ref, dka_ref, dv_ref):
        row = lax.broadcasted_iota(jnp.int32, (tk, tk), 0)
        col = lax.broadcasted_iota(jnp.int32, (tk, tk), 1)
        causal = row <= col
        lane8 = lax.broadcasted_iota(jnp.int32, (8, vw), 1)
        lane_s = lax.broadcasted_iota(jnp.int32, (S, vw), 1)
        lane_k = lax.broadcasted_iota(jnp.int32, (tk, vw), 1)
        doo = do_ref[...].astype(F32) * o_ref[...].astype(F32)
        hi = doo.astype(BF16)
        lo = (doo - hi.astype(F32)).astype(BF16)
        delta, v_head = [], []
        for h in range(2):
            sel = jnp.where((lane8 >= h * HEAD_DIM) & (lane8 < (h + 1) * HEAD_DIM), 1.0, 0.0).astype(BF16)
            delta.append((_dot_nt(sel, hi) + _dot_nt(sel, lo))[0:1, :])
            in_head = (lane_s >= h * HEAD_DIM) & (lane_s < (h + 1) * HEAD_DIM)
            v_head.append(jnp.where(in_head, v_ref[...], jnp.zeros_like(v_ref[...])))
        dqa_ref[...] = jnp.zeros_like(dqa_ref)
        for j in range(nq):
            q0 = j * tk
            krows = slice(q0, q0 + tk)
            do = do_ref[q0:, :]
            dvs = []
            for h in range(2):
                hs = slice(h * HEAD_PAD, (h + 1) * HEAD_PAD)
                k = ka_ref[krows, hs]
                q = qa_ref[q0:, hs]
                st = _dot_nt(k, q)
                p = jnp.exp(st - lse_ref[h:h + 1, q0:])
                p_diag = jnp.where(causal, p[:, :tk], 0.0)
                p = jnp.concatenate([p_diag, p[:, tk:]], axis=1) if j < nq - 1 else p_diag
                dvs.append(_dot(p.astype(BF16), do))
                dpt = _dot_nt(v_head[h][krows, :], do)
                ds = (p * (dpt - delta[h][:, q0:])).astype(BF16)
                dka_ref[krows, hs] = _dot(ds, q)
                dqa_ref[q0:, hs] += _dot_tn(ds, k)
            dv_ref[krows, :] = jnp.where(lane_k < HEAD_DIM, dvs[0], dvs[1]).astype(BF16)

    seq = lambda w, c0=0: pl.BlockSpec((None, S, w), lambda b, p: (b, 0, c0 + p))
    return _pcall(
        body, hosted, name="attn_bwd", grid=(Bl, pairs),
        in_specs=[seq(pw), seq(pw), seq(vw, lay["v"] // vw), seq(vw), seq(vw),
                  pl.BlockSpec((None, None, 2, S), lambda b, p: (b, p, 0, 0)), _ANY],
        out_specs=[seq(pw), seq(pw), seq(vw, lay["v"] // vw)],
        out_shape=[SDS((Bl, S, HEADS * HEAD_PAD), F32), SDS((Bl, S, HEADS * HEAD_PAD), F32),
                   SDS(dproj3.shape, BF16)],
        aliases={6: 2}, semantics=("arbitrary", "arbitrary"),
    )(qa, ka, proj3, dao, ao, lse, dproj3)


def attn_post(dqa, dka, proj3, bf_rows, layer, dproj3, cst, lay):
    Bl, S, w = dqa.shape
    ts = ATTN_BLOCK
    ns = S // ts
    qkf = 2 * BRANCH_W + F_PAD

    def body(dqa_ref, dka_ref, f_ref, bf_ref, pqkt_ref, eq_ref, ek_ref, _, dqkf_ref, dbf_ref, carry_ref):
        b, s = pl.program_id(0), pl.program_id(1)

        @pl.when(s == 0)
        def _():
            carry_ref[...] = jnp.zeros_like(carry_ref)

        dqa_v, dka_v = dqa_ref[...], dka_ref[...]
        qh = dqa_v.astype(BF16)
        kh = dka_v.astype(BF16)
        dqkf_ref[:, :BRANCH_W] = _dot(qh, pqkt_ref[:w, :]).astype(BF16)
        dqkf_ref[:, BRANCH_W:2 * BRANCH_W] = _dot(kh, pqkt_ref[w:, :]).astype(BF16)
        ql = (dqa_v - qh.astype(F32)).astype(BF16)
        kl = (dka_v - kh.astype(F32)).astype(BF16)
        d_f = (_dot(qh, eq_ref[...]) + _dot(ql, eq_ref[...])) + (_dot(kh, ek_ref[...]) + _dot(kl, ek_ref[...]))
        r = lax.broadcasted_iota(jnp.int32, (ts, ts), 0)
        c = lax.broadcasted_iota(jnp.int32, (ts, ts), 1)
        triu = jnp.where(c >= r, 1.0, 0.0).astype(BF16)
        rev = carry_ref[...]
        for part in _split3(d_f):
            rev = rev + _dot(triu, part)
        carry_ref[...] = rev[0:1, :]
        z = f_ref[...].astype(F32) + bf_ref[...]
        lane = lax.broadcasted_iota(jnp.int32, (ts, LANES), 1)
        dfl = jnp.where(lane < HEADS, rev / (1.0 + jnp.exp(z)), 0.0)
        dqkf_ref[:, 2 * BRANCH_W:] = jnp.concatenate(
            [dfl.astype(BF16), jnp.zeros((ts, F_PAD - LANES), BF16)], axis=1)
        part = jnp.sum(dfl, axis=0, keepdims=True)

        @pl.when((b == 0) & (s == 0))
        def _():
            dbf_ref[...] = part

        @pl.when((b > 0) | (s > 0))
        def _():
            dbf_ref[...] += part

    assert lay["q"] % qkf == 0
    cfull = lambda shape: pl.BlockSpec(shape, lambda b, s: (0,) * len(shape))
    rev_blk = lambda wd, c0=0: pl.BlockSpec((None, ts, wd), lambda b, s: (b, ns - 1 - s, c0))
    return pl.pallas_call(
        body, name="attn_post", grid=(Bl, ns),
        in_specs=[rev_blk(w), rev_blk(w), rev_blk(LANES, lay["f"] // LANES),
                  pl.BlockSpec((None, 1, LANES), lambda b, s: (layer, 0, 0)),
                  cfull((2 * w, BRANCH_W)), cfull((w, LANES)), cfull((w, LANES)), _ANY],
        out_specs=[rev_blk(qkf, lay["q"] // qkf), cfull((1, LANES))],
        out_shape=[SDS(dproj3.shape, BF16), SDS((1, LANES), F32)],
        scratch_shapes=[pltpu.VMEM((1, LANES), F32)],
        input_output_aliases={7: 0},
        compiler_params=_params("arbitrary", "arbitrary"),
    )(dqa, dka, proj3, bf_rows, cst["pqkt"], cst["eq"], cst["ek"], dproj3)


def _shift_down(x, k, row):
    return jnp.where(row >= k, pltpu.roll(x, k, axis=0), 0.0)


def _shift_up(x, k, row):
    n = x.shape[0]
    return jnp.where(row < n - k, pltpu.roll(x, n - k, axis=0), 0.0)


def _window_sum(x, g, row, shift):
    s2 = x + shift(x, 1, row)
    s4 = s2 + shift(s2, 2, row)
    s8 = s4 + shift(s4, 4, row)
    s16 = s8 + shift(s8, 8, row)
    return jnp.where(g == 0, s2, jnp.where(g == 1, s4, jnp.where(g == 2, s8, s16)))


def _window_count(g, row):
    wnd = jnp.where(g == 0, 2, jnp.where(g == 1, 4, jnp.where(g == 2, 8, 16)))
    return jnp.minimum(row + 1, wnd).astype(F32)


def _group_columns(ref):
    return [ref[:, n * GROUP_W:(n + 1) * GROUP_W].astype(F32) for n in range(4)]


def poolconv_fwd(proj3, pool_w, pool_scale, conv_w, layer, lay, hosted=None):
    Bl, S, _ = proj3.shape

    def body(x_ref, pw_ref, ps_ref, cw_ref, po_ref, co_ref):
        g = pl.program_id(1)
        row = lax.broadcasted_iota(jnp.int32, (S, GROUP_W), 0)
        u, cv, cb, cc = _group_columns(x_ref)
        d = _window_sum(u, g, row, _shift_down) / _window_count(g, row) - u
        po_ref[...] = (_dot(d.astype(BF16), pw_ref[...]) * ps_ref[...]).astype(BF16)
        z = cc * cv
        y = cw_ref[0:1, :] * _shift_down(z, 2, row) + cw_ref[1:2, :] * _shift_down(z, 1, row) + cw_ref[2:3, :] * z
        co_ref[...] = (cb * y).astype(BF16)

    out = pl.BlockSpec((None, S, GROUP_W), lambda b, g: (b, 0, g))
    return _pcall(
        body, hosted, name="poolconv_fwd", grid=(Bl, N_GROUPS),
        in_specs=[pl.BlockSpec((None, S, BRANCH_W), lambda b, g: (b, 0, lay["pc"] // BRANCH_W + g)),
                  pl.BlockSpec((None, None, GROUP_W, GROUP_W), lambda b, g: (layer, g, 0, 0)),
                  pl.BlockSpec((None, 1, GROUP_W), lambda b, g: (layer, 0, g)),
                  pl.BlockSpec((None, None, 3, GROUP_W), lambda b, g: (g, layer, 0, 0))],
        out_specs=[out, out],
        out_shape=[SDS((Bl, S, BRANCH_W), BF16), SDS((Bl, S, BRANCH_W), BF16)],
        semantics=("arbitrary", "arbitrary"),
    )(proj3, pool_w, pool_scale, conv_w)


def poolconv_bwd(proj3, dpo, dco, pool_w, pool_scale, conv_w, layer, dproj3, lay):
    Bl, S, _ = proj3.shape

    def body(x_ref, dpo_ref, dco_ref, pw_ref, ps_ref, cw_ref, _, dx_ref, dpw_ref, dps_ref, dcw_ref):
        g, b = pl.program_id(0), pl.program_id(1)
        row = lax.broadcasted_iota(jnp.int32, (S, GROUP_W), 0)
        cnt = _window_count(g, row)
        u, cv, cb, cc = _group_columns(x_ref)
        d = (_window_sum(u, g, row, _shift_down) / cnt - u).astype(BF16)
        pw = pw_ref[...]
        ypre = _dot(d, pw)
        dpo_v = dpo_ref[...].astype(F32)
        dps = jnp.sum(dpo_v * ypre, axis=0, keepdims=True)
        dyp = (dpo_v * ps_ref[...]).astype(BF16)
        dpw = _dot_tn(d, dyp)
        dd = _dot_nt(dyp, pw)
        dx_ref[:, 0:GROUP_W] = (_window_sum(dd / cnt, g, row, _shift_up) - dd).astype(BF16)

        z = cc * cv
        z1, z2 = _shift_down(z, 1, row), _shift_down(z, 2, row)
        w0, w1, w2 = cw_ref[0:1, :], cw_ref[1:2, :], cw_ref[2:3, :]
        y = w0 * z2 + w1 * z1 + w2 * z
        dco_v = dco_ref[...].astype(F32)
        dy = dco_v * cb
        dz = w0 * _shift_up(dy, 2, row) + w1 * _shift_up(dy, 1, row) + w2 * dy
        dx_ref[:, GROUP_W:2 * GROUP_W] = (dz * cc).astype(BF16)
        dx_ref[:, 2 * GROUP_W:3 * GROUP_W] = (dco_v * y).astype(BF16)
        dx_ref[:, 3 * GROUP_W:] = (dz * cv).astype(BF16)
        dcw = jnp.concatenate([jnp.sum(dy * z2, axis=0, keepdims=True),
                               jnp.sum(dy * z1, axis=0, keepdims=True),
                               jnp.sum(dy * z, axis=0, keepdims=True)], axis=0)

        @pl.when(b == 0)
        def _():
            dpw_ref[...] = dpw
            dps_ref[...] = dps
            dcw_ref[...] = dcw

        @pl.when(b > 0)
        def _():
            dpw_ref[...] += dpw
            dps_ref[...] += dps
            dcw_ref[...] += dcw

    blk = pl.BlockSpec((None, S, GROUP_W), lambda g, b: (b, 0, g))
    pc = pl.BlockSpec((None, S, BRANCH_W), lambda g, b: (b, 0, lay["pc"] // BRANCH_W + g))
    return pl.pallas_call(
        body, name="poolconv_bwd", grid=(N_GROUPS, Bl),
        in_specs=[pc, blk, blk,
                  pl.BlockSpec((None, None, GROUP_W, GROUP_W), lambda g, b: (layer, g, 0, 0)),
                  pl.BlockSpec((None, 1, GROUP_W), lambda g, b: (layer, 0, g)),
                  pl.BlockSpec((None, None, 3, GROUP_W), lambda g, b: (g, layer, 0, 0)), _ANY],
        out_specs=[pc, pl.BlockSpec((None, GROUP_W, GROUP_W), lambda g, b: (g, 0, 0)),
                   pl.BlockSpec((1, GROUP_W), lambda g, b: (0, g)),
                   pl.BlockSpec((None, 3, GROUP_W), lambda g, b: (g, 0, 0))],
        out_shape=[SDS(dproj3.shape, BF16), SDS((N_GROUPS, GROUP_W, GROUP_W), F32), SDS((1, BRANCH_W), F32),
                   SDS((N_GROUPS, 3, GROUP_W), F32)],
        input_output_aliases={6: 0},
        compiler_params=_params("arbitrary", "arbitrary"),
    )(proj3, dpo, dco, pool_w, pool_scale, conv_w, dproj3)


def _tile_2d(rows, cols, n_arrays):
    budget = VMEM_LIMIT // 2
    lanes = -(-cols // LANES) * LANES
    if rows % 8 == 0:
        for t in (2048, 1024, 512, 256, 128, 64, 32, 16, 8):
            if rows % t == 0 and 2 * n_arrays * t * lanes * 4 <= budget:
                return t, cols
    for t in (1024, 512, 256, 128):
        if cols % t == 0 and 2 * n_arrays * (rows + 8) * t * 4 <= budget:
            return rows, t
    return rows, cols


def add_pair(kept, layer, where, received, name):
    _, n, _, R, C = kept.shape
    tr, tc = _tile_2d(R, C, 3)

    def body(where_ref, a_ref, b_ref, o_ref):
        o_ref[...] = (a_ref[...].astype(F32) + b_ref[...].astype(F32)).astype(BF16)

    blk = pl.BlockSpec((None, tr, tc), lambda d, i, j, where_ref: (d, i, j))
    grid_spec = pltpu.PrefetchScalarGridSpec(
        num_scalar_prefetch=1, grid=(n, R // tr, C // tc),
        in_specs=[pl.BlockSpec((None, None, None, tr, tc),
                               lambda d, i, j, where_ref: (layer, d, where_ref[0], i, j)), blk],
        out_specs=blk)
    return pl.pallas_call(body, name=name, grid_spec=grid_spec, out_shape=SDS((n, R, C), BF16),
                          compiler_params=_params("arbitrary", "arbitrary", "arbitrary"))(where, kept, received)


def add_chips(arrived, own, layer, where, n_layers, prev, name):
    _, R, C = arrived.shape
    tr, tc = _tile_2d(R, C, 6)

    def body(where_ref, a0, a1, a2, a3, own_ref, *rest):
        o_ref = rest[-1]
        chip = where_ref[1]
        acc = None
        for j, a_ref in enumerate((a0, a1, a2, a3)):
            term = jnp.where(chip == j, own_ref[...], a_ref[...]).astype(F32)
            acc = term if acc is None else acc + term
        o_ref[...] = acc

    def slot(j):
        return pl.BlockSpec((None, tr, tc), lambda i, k, where_ref, j=j: (
            jnp.where(where_ref[1] == j, (j + 1) % N_CHIPS, j), i, k))

    in_specs = [slot(j) for j in range(N_CHIPS)] + [
        pl.BlockSpec((None, tr, tc), lambda i, k, where_ref: (where_ref[1], i, k))]
    args = [where, arrived, arrived, arrived, arrived, own]
    aliases = {}
    if prev is not None:
        in_specs.append(_ANY)
        args.append(prev)
        aliases = {len(args) - 1: 0}
    grid_spec = pltpu.PrefetchScalarGridSpec(
        num_scalar_prefetch=1, grid=(R // tr, C // tc), in_specs=in_specs,
        out_specs=pl.BlockSpec((None, None, tr, tc), lambda i, k, where_ref: (layer, where_ref[0], i, k)))
    return pl.pallas_call(body, name=name, grid_spec=grid_spec, out_shape=SDS((n_layers, 2, R, C), F32),
                          input_output_aliases=aliases,
                          compiler_params=_params("arbitrary", "arbitrary"))(*args)


def adamw(w, g, m, v, name):
    if w.ndim == 2:
        R, C = w.shape
        tr, _ = _tile_2d(R, C, 7)
        grid, blk = (R // tr,), pl.BlockSpec((tr, C), lambda i: (i, 0))
    else:
        N, r, C = w.shape
        tn = max(t for t in range(1, N + 1) if N % t == 0 and t * r * C * 4 <= 512 * 1024)
        grid, blk = (N // tn,), pl.BlockSpec((tn, r, C), lambda i: (i, 0, 0))

    def body(w_ref, g_ref, m_ref, v_ref, d_ref, nm_ref, nv_ref):
        gv = g_ref[...]
        m_new = ADAM_B1 * m_ref[...] + (1.0 - ADAM_B1) * gv
        v_new = ADAM_B2 * v_ref[...] + (1.0 - ADAM_B2) * (gv * gv)
        m_hat = m_new / (1.0 - ADAM_B1 ** ADAM_STEP)
        v_hat = v_new / (1.0 - ADAM_B2 ** ADAM_STEP)
        d_ref[...] = -ADAM_LR * (m_hat / (jnp.sqrt(v_hat) + ADAM_EPS) + ADAM_WD * w_ref[...])
        nm_ref[...] = m_new
        nv_ref[...] = v_new

    out = SDS(w.shape, F32)
    return pl.pallas_call(body, name=name, grid=grid, in_specs=[blk] * 4, out_specs=[blk] * 3,
                          out_shape=[out, out, out], compiler_params=_params("arbitrary"))(w, g, m, v)


_COMM = pltpu.CompilerParams(has_side_effects=True)


def gather_buffers(shards):
    me_chip = 2 * lax.axis_index("x") + lax.axis_index("y")
    pool = {}
    for name, sh in shards.items():
        L, r, c = sh.shape
        if name in ROW_SHARDED:
            pool[name] = lax.dynamic_update_slice(lax.empty((L, N_CHIPS, r, c), sh.dtype), sh[:, None],
                                                  (0, me_chip, 0, 0))
        else:
            pool[name] = lax.dynamic_update_slice(lax.empty((N_CHIPS, L, r, c), sh.dtype), sh[None],
                                                  (me_chip, 0, 0, 0))
    return pool


def comm_now(pool, stages, name):
    stages = [Hosted(pool, jobs) for jobs in stages]
    names = sorted({m for st in stages for m in st.names})
    n = len(names)

    def body(*refs):
        bufs = dict(zip(names, refs[n:2 * n]))
        sems = refs[2 * n:]
        for i, st in enumerate(stages):
            plan = _hosted_plan(st, bufs, sems[2 * i], sems[2 * i + 1])
            _hosted_start(plan, True)
            _hosted_finish(plan, True)

    sem = pltpu.SemaphoreType.DMA
    scratch = []
    for st in stages:
        scratch += [sem((len(st.jobs), 3)), sem((len(st.jobs), 3))]
    res = pl.pallas_call(
        body, name=name, in_specs=[_ANY] * n, out_specs=[_ANY] * n,
        out_shape=[SDS(pool[m].shape, pool[m].dtype) for m in names],
        scratch_shapes=scratch, input_output_aliases={t: t for t in range(n)},
        compiler_params=_COMM,
    )(*[pool[m] for m in names])
    pool.update(zip(names, res))


def gather_now(pool, units):
    comm_now(pool, [[("ici", name, layer) for name, layer in units],
                    [("fwd", name, layer) for name, layer in units]], "gather_now")


def allgather_chips(buf, name):
    def body(src_ref, out_ref, send_sems, recv_sems, local_sem):
        x, y, c = _position()
        me = 2 * x + y
        mine = pltpu.make_async_copy(src_ref, out_ref.at[me], local_sem)
        mine.start()
        sends = []
        for k, (px, py) in enumerate(_other_chips(x, y)):
            cp = _remote(src_ref, out_ref.at[me], send_sems.at[k], recv_sems.at[k], (px, py, c))
            cp.start()
            sends.append(cp)
        for k, (px, py) in enumerate(_other_chips(x, y)):
            _remote(src_ref, out_ref.at[2 * px + py], send_sems.at[k], recv_sems.at[k], (px, py, c)).wait_recv()
        for cp in sends:
            cp.wait_send()
        mine.wait()

    sem = pltpu.SemaphoreType.DMA
    return pl.pallas_call(
        body, name=name, in_specs=[_ANY], out_specs=_ANY, out_shape=SDS((N_CHIPS,) + buf.shape, buf.dtype),
        scratch_shapes=[sem((3,)), sem((3,)), sem], compiler_params=_COMM,
    )(buf)


def swap_sibling(tensors, name):
    n = len(tensors)

    def body(*refs):
        srcs, outs, send_sems, recv_sems = refs[:n], refs[n:2 * n], refs[2 * n], refs[2 * n + 1]
        x, y, c = _position()
        cps = [_remote(srcs[t].at[1 - c], outs[t], send_sems.at[t], recv_sems.at[t], (x, y, 1 - c))
               for t in range(n)]
        for cp in cps:
            cp.start()
        for cp in cps:
            cp.wait()

    sem = pltpu.SemaphoreType.DMA
    return pl.pallas_call(
        body, name=name, in_specs=[_ANY] * n, out_specs=[_ANY] * n,
        out_shape=[SDS(t.shape[1:], t.dtype) for t in tensors],
        scratch_shapes=[sem((n,)), sem((n,))], compiler_params=_COMM,
    )(*tensors)


def exchange_chips(tensors, name):
    n = len(tensors)

    def body(*refs):
        srcs, outs = refs[:n], refs[n:2 * n]
        send_sems, recv_sems, local_sems = refs[2 * n:]
        x, y, c = _position()
        me = 2 * x + y
        others = _other_chips(x, y)
        cps = []
        for t in range(n):
            cp = pltpu.make_async_copy(srcs[t].at[me], outs[t].at[me], local_sems.at[t])
            cp.start()
            cps.append(cp)
        sends = []
        for t in range(n):
            for k, (px, py) in enumerate(others):
                cp = _remote(srcs[t].at[2 * px + py], outs[t].at[me], send_sems.at[t, k], recv_sems.at[t, k],
                             (px, py, c))
                cp.start()
                sends.append(cp)
        for t in range(n):
            for k, (px, py) in enumerate(others):
                _remote(srcs[t].at[me], outs[t].at[2 * px + py], send_sems.at[t, k], recv_sems.at[t, k],
                        (px, py, c)).wait_recv()
        for cp in sends:
            cp.wait_send()
        for cp in cps:
            cp.wait()

    sem = pltpu.SemaphoreType.DMA
    return pl.pallas_call(
        body, name=name, in_specs=[_ANY] * n, out_specs=[_ANY] * n,
        out_shape=[SDS(t.shape, t.dtype) for t in tensors],
        scratch_shapes=[sem((n, 3)), sem((n, 3)), sem((n,))], compiler_params=_COMM,
    )(*tensors)


def join_halves(tensors, name):
    n = len(tensors)

    def body(*refs):
        outs, send_sems, recv_sems = refs[n:2 * n], refs[2 * n], refs[2 * n + 1]
        x, y, c = _position()
        sib = (x, y, 1 - c)
        sends = []
        for t in range(n):
            cp = _remote(outs[t].at[c], outs[t].at[c], send_sems.at[t], recv_sems.at[t], sib)
            cp.start()
            sends.append(cp)
        for t in range(n):
            _remote(outs[t].at[c], outs[t].at[1 - c], send_sems.at[t], recv_sems.at[t], sib).wait_recv()
        for cp in sends:
            cp.wait_send()

    sem = pltpu.SemaphoreType.DMA
    return pl.pallas_call(
        body, name=name, in_specs=[_ANY] * n, out_specs=[_ANY] * n,
        out_shape=[SDS(t.shape, t.dtype) for t in tensors],
        scratch_shapes=[sem((n,)), sem((n,))], input_output_aliases={t: t for t in range(n)},
        compiler_params=_COMM,
    )(*tensors)


BIG = ("w_in", "w_proj_attn", "w_proj_pool", "w_proj_conv", "conv_w", "w_out", "w_gate_up", "w_down")
REPLICATED = ("attn_norm", "b_forget", "b_gate", "pool_w", "pool_scale", "ffn_norm", "final_norm")
ORDER = ("attn_norm", "w_in", "b_forget", "b_gate", "w_proj_attn", "pool_w", "pool_scale", "w_proj_pool",
         "conv_w", "w_proj_conv", "w_out", "ffn_norm", "w_gate_up", "w_down", "final_norm")


def _proj_layout(D):
    lay = {"g": 0, "q": 3 * D}
    lay["k"] = lay["q"] + BRANCH_W
    lay["f"] = lay["k"] + BRANCH_W
    lay["v"] = lay["f"] + F_PAD
    lay["pc"] = lay["v"] + BRANCH_W
    lay["width"] = lay["pc"] + 4 * BRANCH_W
    return lay


_REF = dict(q=0, k=512, v=1024, f=1536, u=1544, cv=2056, cb=2568, cc=3080, g=3592)


def _packed_pieces(D):
    pieces = [(_REF["g"], 3 * D), (_REF["q"], BRANCH_W), (_REF["k"], BRANCH_W), (_REF["f"], HEADS),
              (None, F_PAD - HEADS), (_REF["v"], BRANCH_W)]
    for gi in range(N_GROUPS):
        pieces += [(_REF[name] + gi * GROUP_W, GROUP_W) for name in ("u", "cv", "cb", "cc")]
    return pieces


def _packed_runs(D, cs):
    runs, at = [], 0
    for start, n in _packed_pieces(D):
        if start is None:
            runs.append((at, None, 0, n))
            at += n
        while start is not None and n:
            chip, off = divmod(start, cs)
            take = min(n, cs - off)
            runs.append((at, chip, off, take))
            at, start, n = at + take, start + take, n - take
    return runs


def pack_w_in(shards, layer):
    _, _, cs, D = shards.shape
    runs = _packed_runs(D, cs)
    width = runs[-1][0] + runs[-1][3]
    tc = _tile(D, (256, 128))

    def body(s_ref, o_ref):
        for dst, chip, off, rows in runs:
            if chip is None:
                o_ref[dst:dst + rows, :] = jnp.zeros((rows, tc), s_ref.dtype)
            else:
                o_ref[dst:dst + rows, :] = s_ref[chip, off:off + rows, :]

    return pl.pallas_call(
        body, name="pack_w_in", grid=(D // tc,),
        in_specs=[pl.BlockSpec((N_CHIPS, None, cs, tc), lambda j: (0, layer, 0, j))],
        out_specs=pl.BlockSpec((width, tc), lambda j: (0, j)),
        out_shape=SDS((width, D), shards.dtype), compiler_params=_params("arbitrary"),
    )(shards)


def unpack_w_in(p, cs):
    width, D = p.shape
    half = cs // 2
    runs = []
    for src, chip, off, rows in _packed_runs(D, cs):
        while chip is not None and rows:
            h, at = divmod(off, half)
            take = min(rows, half - at)
            runs.append((src, chip, h, at, take))
            src, off, rows = src + take, off + take, rows - take
    tc = _tile(D, (256, 128))

    def body(p_ref, o_ref):
        for src, chip, h, at, rows in runs:
            o_ref[chip, h, at:at + rows, :] = p_ref[src:src + rows, :]

    return pl.pallas_call(
        body, name="unpack_w_in", grid=(D // tc,),
        in_specs=[pl.BlockSpec((width, tc), lambda j: (0, j))],
        out_specs=pl.BlockSpec((N_CHIPS, 2, half, tc), lambda j: (0, 0, 0, j)),
        out_shape=SDS((N_CHIPS, 2, half, D), p.dtype), compiler_params=_params("arbitrary"),
    )(p)


def _split_flat(vec, shapes):
    out, at = [], 0
    for shp in shapes:
        n = int(np.prod(shp))
        out.append(vec[at:at + n].reshape(shp))
        at += n
    return out


def kernel(x, attn_norm, w_in, b_forget, b_gate, w_proj_attn, pool_w, pool_scale, w_proj_pool, conv_w, w_proj_conv, w_out, ffn_norm, w_gate_up, w_down, final_norm, loss_target, m_attn_norm, m_w_in, m_b_forget, m_b_gate, m_w_proj_attn, m_pool_w, m_pool_scale, m_w_proj_pool, m_conv_w, m_w_proj_conv, m_w_out, m_ffn_norm, m_w_gate_up, m_w_down, m_final_norm, v_attn_norm, v_w_in, v_b_forget, v_b_gate, v_w_proj_attn, v_pool_w, v_pool_scale, v_w_proj_pool, v_conv_w, v_w_proj_conv, v_w_out, v_ffn_norm, v_w_gate_up, v_w_down, v_final_norm):
    weights = dict(attn_norm=attn_norm, w_in=w_in, b_forget=b_forget, b_gate=b_gate, w_proj_attn=w_proj_attn,
                   pool_w=pool_w, pool_scale=pool_scale, w_proj_pool=w_proj_pool, conv_w=conv_w,
                   w_proj_conv=w_proj_conv, w_out=w_out, ffn_norm=ffn_norm, w_gate_up=w_gate_up, w_down=w_down,
                   final_norm=final_norm)
    mom_m = dict(attn_norm=m_attn_norm, w_in=m_w_in, b_forget=m_b_forget, b_gate=m_b_gate, w_proj_attn=m_w_proj_attn,
                 pool_w=m_pool_w, pool_scale=m_pool_scale, w_proj_pool=m_w_proj_pool, conv_w=m_conv_w,
                 w_proj_conv=m_w_proj_conv, w_out=m_w_out, ffn_norm=m_ffn_norm, w_gate_up=m_w_gate_up,
                 w_down=m_w_down, final_norm=m_final_norm)
    mom_v = dict(attn_norm=v_attn_norm, w_in=v_w_in, b_forget=v_b_forget, b_gate=v_b_gate, w_proj_attn=v_w_proj_attn,
                 pool_w=v_pool_w, pool_scale=v_pool_scale, w_proj_pool=v_w_proj_pool, conv_w=v_conv_w,
                 w_proj_conv=v_w_proj_conv, w_out=v_w_out, ffn_norm=v_ffn_norm, w_gate_up=v_w_gate_up,
                 w_down=v_w_down, final_norm=v_final_norm)

    Bl, S, D = x.shape
    T = Bl * S
    L = w_in.shape[0]
    F = w_down.shape[1] * N_CHIPS
    lay = _proj_layout(D)
    cst = _placement_constants()
    assert L == N_LAYERS and S % ATTN_BLOCK == 0 and F % (2 * LANES) == 0 and D % BRANCH_W == 0
    assert w_in.shape[2] * N_CHIPS == _REF["g"] + 3 * D and conv_w.shape[2] == GROUP_W

    send = {n: weights[n].astype(BF16) for n in BIG}
    send["conv_w"] = conv_w
    me_chip = 2 * lax.axis_index("x") + lax.axis_index("y")
    send["w_in"] = w_in.transpose(0, 2, 1).astype(BF16)
    pool = gather_buffers(send)
    gather_now(pool, [("w_in", 0)])
    rest = ("w_out", "w_proj_attn", "w_proj_pool", "w_gate_up", "w_proj_conv", "conv_w")
    late = ("w_out", "w_proj_attn", "w_proj_pool", "w_proj_conv", "conv_w")
    jobs = lambda kind, names, layer: [(kind, n, layer) for n in names]
    carried = {
        ("in_proj", 0): jobs("ici", rest, 0),
        ("attn_prep", 0): jobs("fwd", rest, 0),
        ("attn_fwd", 0): jobs("ici", ("w_in",), 1) + jobs("ici", ("w_down",), 0),
        ("poolconv_fwd", 0): jobs("fwd", ("w_in",), 1) + jobs("fwd", ("w_down",), 0),
        ("mix_fwd", 0): jobs("ici", ("w_gate_up",), 1),
        ("gate_up_proj", 0): jobs("ici", ("w_down",) + late, 1),
        ("ffn_down_fwd", 0): jobs("fwd", ("w_gate_up",), 1),
        ("in_proj", 1): jobs("fwd", ("w_down",) + late, 1),
    }
    carry = lambda call, layer: Hosted(pool, carried[call, layer]) if (call, layer) in carried else None
    w_down_f = lambda: pool["w_down"].reshape(L, F, D)
    pool_w_b = pool_w.astype(BF16)
    an3, fn3 = attn_norm.reshape(L, 1, D), ffn_norm.reshape(L, 1, D)
    bg3, ps3 = b_gate.reshape(L, 1, 3 * D), pool_scale.reshape(L, 1, BRANCH_W)
    bf3 = jnp.pad(b_forget, ((0, 0), (0, LANES - HEADS))).reshape(L, 1, LANES)

    xs = x.reshape(T, D)
    saved = []
    w_in_p = []
    for l in range(L):
        w_in_p.append(pack_w_in(pool["w_in"], l))
        proj, h = norm_matmul(xs, an3, w_in_p[l], l, "rows", "in_proj", carry("in_proj", l))
        proj3 = proj.reshape(Bl, S, lay["width"])
        qa, ka = attn_prep(proj3, bf3, l, cst, lay, carry("attn_prep", l))
        ao, lse = attn_fwd(qa, ka, proj3, lay, carry("attn_fwd", l))
        po, co = poolconv_fwd(proj3, pool_w_b, ps3, pool["conv_w"], l, lay, carry("poolconv_fwd", l))
        ao2, po2, co2 = (a.reshape(T, BRANCH_W) for a in (ao, po, co))
        x1, ys, mixed = mix_fwd(ao2, po2, co2, proj, bg3, pool["w_proj_attn"], pool["w_proj_pool"],
                                pool["w_proj_conv"], pool["w_out"], l, xs, carry("mix_fwd", l))
        ab, h2 = norm_matmul(x1, fn3, pool["w_gate_up"], l, "by_shard", "gate_up_proj", carry("gate_up_proj", l))
        x2, s_act = ffn_down_fwd(ab, w_down_f(), l, x1, carry("ffn_down_fwd", l))
        saved.append(dict(x=xs, proj=proj, proj3=proj3, h=h, qa=qa, ka=ka, ao=ao, lse=lse, ao2=ao2, po2=po2,
                          co2=co2, ys=ys, mixed=mixed, x1=x1, ab=ab, h2=h2, s=s_act))
        xs = x2
    w_gu, w_o, conv_w_g = pool["w_gate_up"], pool["w_out"], pool["conv_w"]
    wpa, wpp, wpc = pool["w_proj_attn"], pool["w_proj_pool"], pool["w_proj_conv"]
    w_down_f = w_down_f()

    loss_row, dx, dxb, g_final = loss_head(xs, final_norm.reshape(1, D), loss_target.reshape(T, D))
    loss = lax.psum(loss_row[0, 0], AXES)

    reduced_names = tuple(n for n in BIG if n != "conv_w")
    early_names = tuple(n for n in reduced_names if n != "w_in")
    where = jnp.stack([lax.axis_index("c"), me_chip]).astype(jnp.int32)
    rs = {}

    def reduce_begin(layer, grads):
        for n, g in grads.items():
            g5 = g.reshape((1, N_CHIPS, 2, -1) + g.shape[-1:])
            rs["g%d:%s" % (layer, n)] = g5
            for role in "ra":
                rs["%s%d:%s" % (role, layer, n)] = lax.empty((N_CHIPS,) + g5.shape[3:], BF16)

    swap_jobs = lambda layer, names: [("swap", "g%d:%s" % (layer, n), "r%d:%s" % (layer, n), 0) for n in names]
    xchg_jobs = lambda layer, names: [("xchg", "s%d:%s" % (layer, n), "a%d:%s" % (layer, n)) for n in names]
    join_jobs = lambda layer, names: [("join", "o:" + n, layer) for n in names]

    def pair_sums(layer, names):
        for n in names:
            rs["s%d:%s" % (layer, n)] = add_pair(rs["g%d:%s" % (layer, n)], 0, where, rs["r%d:%s" % (layer, n)],
                                                 "add_pair_" + n)

    def chip_sums(layer, names, slot, n_slots):
        for n in names:
            rs["o:" + n] = add_chips(rs["a%d:%s" % (layer, n)], rs["s%d:%s" % (layer, n)], slot, where, n_slots,
                                     rs.get("o:" + n), "add_chips_" + n)

    small = {n: [None] * L for n in REPLICATED if n != "final_norm"}
    g_conv = [None] * L
    to3 = lambda a: a.reshape(Bl, S, -1)
    for l in reversed(range(L)):
        sv = saved[l]
        behind = (lambda jobs: Hosted(rs, jobs)) if l == 0 else (lambda jobs: None)
        grads = {}
        da, db = ffn_down_bwd(dxb, w_down_f, l, sv["ab"], behind(swap_jobs(1, reduced_names)))
        if l == 0:
            pair_sums(1, reduced_names)
        grads["w_down"] = matmul_tn(sv["s"], [dxb], "grad_w_down", hosted=behind(xchg_jobs(
            1, ("w_down", "w_out", "w_proj_attn", "w_proj_pool", "w_proj_conv"))))
        grads["w_gate_up"] = matmul_tn(sv["h2"], [da, db], "grad_w_gate_up", by_dest=True, tn=2 * F // N_CHIPS,
                                       tk=_tile(T, (1024, 512, 256)), hosted=behind(xchg_jobs(1, ("w_gate_up",))))
        dx1, dx1b, g_fn = matmul_nt_normbwd([da, db], w_gu, l, "by_shard", sv["x1"], fn3, dx, "gate_up_bwd",
                                            behind(xchg_jobs(1, ("w_in",))))
        small["ffn_norm"][l] = g_fn[0]
        if l == 0:
            chip_sums(1, reduced_names, 1, L)
        dys, dproj, dao, dpo, dco, g_bg = mix_bwd(dx1b, w_o, sv["proj"], bg3, sv["ys"], wpa, wpp, wpc, l,
                                                  lay["width"], behind(join_jobs(1, reduced_names)))
        small["b_gate"][l] = g_bg[0]
        grads["w_out"] = matmul_tn(sv["mixed"], [dx1b], "grad_w_out")
        for n, (name, br) in enumerate((("w_proj_attn", sv["ao2"]), ("w_proj_pool", sv["po2"]),
                                        ("w_proj_conv", sv["co2"]))):
            grads[name] = matmul_tn(br, [dys], "grad_" + name, b_col0=n * D, n_cols=D, by_dest=True,
                                    tn=D // N_CHIPS)
        if l == 0:
            reduce_begin(0, grads)
        dqa, dka, dproj3 = attn_bwd(sv["qa"], sv["ka"], sv["proj3"], to3(dao), sv["ao"], sv["lse"], to3(dproj), lay,
                                    behind(swap_jobs(0, early_names)))
        if l == 0:
            pair_sums(0, early_names)
        dproj3, g_bf = attn_post(dqa, dka, sv["proj3"], bf3, l, dproj3, cst, lay)
        small["b_forget"][l] = g_bf[0, :HEADS]
        dproj3, g_pw, g_ps, g_conv[l] = poolconv_bwd(sv["proj3"], to3(dpo), to3(dco), pool_w_b, ps3, conv_w_g, l,
                                                     dproj3, lay)
        small["pool_w"][l], small["pool_scale"][l] = g_pw, g_ps[0]
        dproj = dproj3.reshape(T, lay["width"])
        g_w_in = unpack_w_in(matmul_tn(dproj, [sv["h"]], "grad_w_in", hosted=behind(xchg_jobs(
            0, ("w_gate_up", "w_down")))), w_in.shape[2])
        if l:
            reduce_begin(l, {**grads, "w_in": g_w_in})
        else:
            reduce_begin(0, {"w_in": g_w_in})
            comm_now(rs, [swap_jobs(0, ("w_in",))], "swap_w_in_halves")
            pair_sums(0, ("w_in",))
        dx, dxb, g_an = matmul_nt_normbwd([dproj], w_in_p[l], l, "rows", sv["x"], an3, dx1, "in_proj_bwd",
                                          behind(xchg_jobs(0, ("w_out", "w_proj_attn", "w_proj_pool",
                                                               "w_proj_conv", "w_in"))))
        small["attn_norm"][l] = g_an[0]
    grad_x = dx.reshape(Bl, S, D)

    small_shapes = [weights[n].shape for n in REPLICATED] + [(L, N_CHIPS) + conv_w.shape[1:]]
    small_vec = jnp.concatenate([jnp.stack(small[n]).reshape(-1) for n in REPLICATED[:-1]]
                                + [g_final[0], jnp.stack(g_conv).reshape(-1)])
    n_small = small_vec.shape[0]
    small_vec = jnp.pad(small_vec, (0, -n_small % (2 * N_CHIPS * 16 * LANES))).astype(BF16)
    rs["g0:small"] = small_vec.reshape(1, N_CHIPS, 2, -1, LANES)
    for role in "ra":
        rs[role + "0:small"] = lax.empty((N_CHIPS,) + rs["g0:small"].shape[3:], BF16)
    last = ("small",)
    comm_now(rs, [swap_jobs(0, last)], "swap_grad_halves")
    pair_sums(0, last)
    comm_now(rs, [xchg_jobs(0, last)], "exchange_grad_chips")
    chip_sums(0, reduced_names, 0, L)
    chip_sums(0, ("small",), 0, 1)
    comm_now(rs, [join_jobs(0, reduced_names + ("small",))], "join_grad_halves")
    shard_grads = {n: rs["o:" + n].reshape((L, -1) + rs["o:" + n].shape[-1:]) for n in reduced_names}
    small_all = allgather_chips(rs["o:small"].reshape(-1, LANES), "allgather_small_grads").reshape(-1)[:n_small]
    *rep_list, conv_all = _split_flat(small_all, small_shapes)
    rep_grads = dict(zip(REPLICATED, rep_list))
    shard_grads["conv_w"] = lax.dynamic_index_in_dim(conv_all, me_chip, 1, keepdims=False)

    delta, new_m, new_v = {}, {}, {}
    for n in BIG:
        shp = weights[n].shape
        if n == "w_in":
            view, back = (lambda a: a.transpose(2, 0, 1)), (lambda a: a.transpose(1, 2, 0))
            g = shard_grads[n].transpose(1, 0, 2)
        else:
            view, back = (lambda a: a.reshape(-1, shp[-1])), (lambda a: a.reshape(shp))
            g = view(shard_grads[n])
        d, nm, nv = adamw(view(weights[n]), g, view(mom_m[n]), view(mom_v[n]), "adamw_" + n)
        delta[n], new_m[n], new_v[n], shard_grads[n] = back(d), back(nm), back(nv), back(g)

    def rows(d):
        vec = jnp.concatenate([d[n].reshape(-1) for n in REPLICATED])
        return jnp.pad(vec, (0, -vec.shape[0] % (8 * LANES))).reshape(-1, LANES)

    outs = adamw(rows(weights), rows(rep_grads), rows(mom_m), rows(mom_v), "adamw_replicated")
    for res, o in zip((delta, new_m, new_v), outs):
        res.update(zip(REPLICATED, _split_flat(o.reshape(-1), small_shapes[:len(REPLICATED)])))
    all_grads = {**shard_grads, **rep_grads}

    return (loss, grad_x, *[all_grads[n] for n in ORDER], *[delta[n] for n in ORDER],
            *[new_m[n] for n in ORDER], *[new_v[n] for n in ORDER])
```

```python
import numpy as np
import jax
import jax.numpy as jnp
from jax import lax
from jax.experimental import pallas as pl
from jax.experimental.pallas import tpu as pltpu

F32, BF16 = jnp.float32, jnp.bfloat16
SDS = jax.ShapeDtypeStruct
MESH = pl.DeviceIdType.MESH
AXES = ("x", "y", "c")
N_CHIPS = 4
N_LAYERS = 2
LANES = 128
VMEM_LIMIT = 48 * 1024 * 1024

HEADS, HEAD_DIM = 8, 64
HEAD_PAD = 128
BRANCH_W = 512
GROUP_W = 128
N_GROUPS = BRANCH_W // GROUP_W
POOL_WINDOWS = (2, 4, 8, 16)
F_PAD = 512
ATTN_BLOCK = 256
RMS_EPS = 1e-6
NEG_INF = -1e30
ADAM_LR, ADAM_B1, ADAM_B2, ADAM_EPS, ADAM_WD, ADAM_STEP = 0.001, 0.9, 0.999, 1e-08, 0.01, 10

NT = (((1,), (1,)), ((), ()))
TN = (((0,), (0,)), ((), ()))
_ANY = pl.BlockSpec(memory_space=pl.ANY)


def _tile(n, prefs):
    for p in prefs:
        if n % p == 0:
            return p
    raise ValueError(f"no tile of {prefs} divides {n}")


def _params(*sem):
    return pltpu.CompilerParams(dimension_semantics=sem, vmem_limit_bytes=VMEM_LIMIT)


def _sigmoid(z):
    return 0.5 * jnp.tanh(0.5 * z) + 0.5


def _split3(x):
    h1 = x.astype(BF16)
    r1 = x - h1.astype(F32)
    h2 = r1.astype(BF16)
    h3 = (r1 - h2.astype(F32)).astype(BF16)
    return h1, h2, h3


def _position():
    return lax.axis_index("x"), lax.axis_index("y"), lax.axis_index("c")


def _other_chips(x, y):
    return [(1 - x, y), (x, 1 - y), (1 - x, 1 - y)]


def _remote(src, dst, send_sem, recv_sem, device):
    return pltpu.make_async_remote_copy(src_ref=src, dst_ref=dst, send_sem=send_sem, recv_sem=recv_sem,
                                        device_id=device, device_id_type=MESH)


ROW_SHARDED = ("w_out", "w_down")
FETCHER = dict(w_in=0, w_out=0, w_proj_attn=0, w_proj_pool=0, w_gate_up=1, w_down=1, w_proj_conv=1, conv_w=1)


class Hosted:
    def __init__(self, pool, jobs):
        self.pool, self.jobs = pool, list(jobs)
        names = set()
        for job in self.jobs:
            names.update(job[1:3] if job[0] in ("swap", "xchg") else job[1:2])
        self.names = sorted(names)


def _hosted_plan(hosted, refs, send_sems, recv_sems):
    x, y, c = _position()
    me = 2 * x + y
    others = _other_chips(x, y)
    sibling = (x, y, 1 - c)
    plan = []
    for j, job in enumerate(hosted.jobs):
        kind = job[0]
        sems = lambda k, j=j: (send_sems.at[j, k], recv_sems.at[j, k])
        if kind in ("ici", "fwd"):
            _, name, layer = job
            ref = refs[name]
            win = (lambda chip, ref=ref, layer=layer: ref.at[layer, chip]) if name in ROW_SHARDED else (
                lambda chip, ref=ref, layer=layer: ref.at[chip, layer])
            mine = c == FETCHER[name]
            if kind == "ici":
                sends = [_remote(win(me), win(me), *sems(k), (px, py, c)) for k, (px, py) in enumerate(others)]
                arrivals = [_remote(win(2 * px + py), win(2 * px + py), *sems(k), (px, py, c))
                            for k, (px, py) in enumerate(others)]
                plan.append((mine, sends, arrivals, []))
            else:
                sends = [_remote(win(2 * px + py), win(2 * px + py), *sems(k), sibling)
                         for k, (px, py) in enumerate(others)]
                plan.append((mine, sends, [], sends))
        elif kind == "swap":
            _, src, dst, layer = job
            cp = _remote(refs[src].at[layer, :, 1 - c], refs[dst], *sems(0), sibling)
            plan.append((True, [cp], [cp], []))
        elif kind == "xchg":
            _, src, dst = job
            sends = [_remote(refs[src].at[2 * px + py], refs[dst].at[me], *sems(k), (px, py, c))
                     for k, (px, py) in enumerate(others)]
            arrivals = [_remote(refs[src].at[me], refs[dst].at[2 * px + py], *sems(k), (px, py, c))
                        for k, (px, py) in enumerate(others)]
            plan.append((True, sends, arrivals, []))
        else:
            _, name, layer = job
            ref = refs[name]
            cp = _remote(ref.at[layer, c], ref.at[layer, c], *sems(0), sibling)
            arrival = _remote(ref.at[layer, c], ref.at[layer, 1 - c], *sems(0), sibling)
            plan.append((True, [cp], [arrival], []))
    return plan


def _hosted_start(plan, now):
    for mine, sends, _, _ in plan:
        @pl.when(now & mine)
        def _(sends=sends):
            for cp in sends:
                cp.start()


def _hosted_finish(plan, now):
    for mine, sends, arrivals, sibling_arrivals in plan:
        @pl.when(now & mine)
        def _(sends=sends, arrivals=arrivals):
            for cp in arrivals:
                cp.wait_recv()
            for cp in sends:
                cp.wait_send()

        if sibling_arrivals:
            @pl.when(now & jnp.logical_not(mine))
            def _(sibling_arrivals=sibling_arrivals):
                for cp in sibling_arrivals:
                    cp.wait_recv()


def _pcall(body, hosted, *, name, grid, in_specs, out_specs, out_shape, semantics, scratch_shapes=(), aliases=None):
    aliases = dict(aliases or {})
    if hosted is None or not hosted.jobs:
        return pl.pallas_call(body, name=name, grid=grid, in_specs=in_specs, out_specs=out_specs,
                              out_shape=out_shape, scratch_shapes=list(scratch_shapes),
                              input_output_aliases=aliases, compiler_params=_params(*semantics))
    single = not isinstance(out_shape, (list, tuple))
    out_specs_l = [out_specs] if single else list(out_specs)
    out_shape_l = [out_shape] if single else list(out_shape)
    n_in, n_out, n_buf, n_job = len(in_specs), len(out_specs_l), len(hosted.names), len(hosted.jobs)

    def carrying(*refs):
        ins, outs = refs[:n_in], refs[n_in + n_buf:n_in + n_buf + n_out]
        bufs = refs[n_in + n_buf + n_out:n_in + 2 * n_buf + n_out]
        rest = refs[n_in + 2 * n_buf + n_out:]
        scratch, send_sems, recv_sems = rest[:-2], rest[-2], rest[-1]
        first, last = True, True
        for axis, size in enumerate(grid):
            first = first & (pl.program_id(axis) == 0)
            last = last & (pl.program_id(axis) == size - 1)
        plan = _hosted_plan(hosted, dict(zip(hosted.names, bufs)), send_sems, recv_sems)
        _hosted_start(plan, first)
        body(*ins, *outs, *scratch)
        _hosted_finish(plan, last)

    def run(*args):
        bufs = [hosted.pool[n] for n in hosted.names]
        sem = pltpu.SemaphoreType.DMA
        res = pl.pallas_call(
            carrying, name=name, grid=grid, in_specs=list(in_specs) + [_ANY] * n_buf,
            out_specs=out_specs_l + [_ANY] * n_buf,
            out_shape=out_shape_l + [SDS(b.shape, b.dtype) for b in bufs],
            scratch_shapes=list(scratch_shapes) + [sem((n_job, 3)), sem((n_job, 3))],
            input_output_aliases={**aliases, **{n_in + i: n_out + i for i in range(n_buf)}},
            compiler_params=pltpu.CompilerParams(dimension_semantics=semantics, vmem_limit_bytes=VMEM_LIMIT,
                                                 has_side_effects=True),
        )(*args, *bufs)
        hosted.pool.update(zip(hosted.names, res[n_out:]))
        return res[0] if single else res[:n_out]

    return run


def _dot(a, b):
    return jnp.dot(a, b, preferred_element_type=F32)


def _dot_nt(a, b):
    return lax.dot_general(a, b, NT, preferred_element_type=F32)


def _dot_tn(a, b):
    return lax.dot_general(a, b, TN, preferred_element_type=F32)


def norm_matmul(x, gain, w, layer, kind, name, hosted=None):
    T, D = x.shape
    if kind == "by_shard":
        tn = w.shape[3]
        N = N_CHIPS * tn
        w_spec = pl.BlockSpec((None, None, D, tn), lambda i, j: (j, layer, 0, 0))
        mm = _dot
    else:
        N = w.shape[0]
        tn = _tile(N, (1024, 512, 256, 128))
        w_spec = pl.BlockSpec((tn, D), lambda i, j: (j, 0))
        mm = _dot_nt
    tm = _tile(T, (1024, 512, 256, 128))

    def body(x_ref, g_ref, w_ref, y_ref, h_ref):
        @pl.when(pl.program_id(1) == 0)
        def _():
            xf = x_ref[...]
            r = lax.rsqrt(jnp.mean(xf * xf, axis=-1, keepdims=True) + RMS_EPS)
            h_ref[...] = ((xf * r) * g_ref[...]).astype(BF16)

        y_ref[...] = mm(h_ref[...], w_ref[...]).astype(BF16)

    return _pcall(
        body, hosted, name=name, grid=(T // tm, N // tn),
        in_specs=[pl.BlockSpec((tm, D), lambda i, j: (i, 0)),
                  pl.BlockSpec((None, 1, D), lambda i, j: (layer, 0, 0)),
                  w_spec],
        out_specs=[pl.BlockSpec((tm, tn), lambda i, j: (i, j)),
                   pl.BlockSpec((tm, D), lambda i, j: (i, 0))],
        out_shape=[SDS((T, N), BF16), SDS((T, D), BF16)],
        semantics=("arbitrary", "arbitrary"),
    )(x, gain, w)


def matmul_nt_normbwd(dys, w, layer, kind, x, gain, dres, name, hosted=None):
    T, D = x.shape
    width = dys[0].shape[1]
    if kind == "by_shard":
        tk = w.shape[3]
        w_spec = pl.BlockSpec((None, None, D, tk), lambda i, k: (k, layer, 0, 0))
        mm = _dot_nt
    else:
        tk = _tile(width, (3584, 1024, 512, 256, 128))
        w_spec = pl.BlockSpec((tk, D), lambda i, k: (k, 0))
        mm = _dot
    per = width // tk
    nk = per * len(dys)
    tm = _tile(T, (512, 256, 128))
    n_dy = len(dys)

    def dy_spec(p):
        return pl.BlockSpec((tm, tk), lambda i, k: (i, jnp.clip(k - p * per, 0, per - 1)))

    def body(*refs):
        dy_refs = refs[:n_dy]
        w_ref, x_ref, g_ref, dres_ref, dx_ref, dxb_ref, dg_ref, acc_ref = refs[n_dy:]
        i, k = pl.program_id(0), pl.program_id(1)

        @pl.when(k == 0)
        def _():
            acc_ref[...] = jnp.zeros_like(acc_ref)

        for p in range(n_dy):
            @pl.when((k >= p * per) & (k < (p + 1) * per))
            def _(p=p):
                acc_ref[...] += mm(dy_refs[p][...], w_ref[...])

        @pl.when(k == nk - 1)
        def _():
            xf = x_ref[...]
            r = lax.rsqrt(jnp.mean(xf * xf, axis=-1, keepdims=True) + RMS_EPS)
            xhat = xf * r
            dh = acc_ref[...]
            dhg = dh * g_ref[...]
            dx = dres_ref[...] + r * (dhg - xhat * jnp.mean(dhg * xhat, axis=-1, keepdims=True))
            dx_ref[...] = dx
            dxb_ref[...] = dx.astype(BF16)
            part = jnp.sum(dh * xhat, axis=0, keepdims=True)

            @pl.when(i == 0)
            def _():
                dg_ref[...] = part

            @pl.when(i > 0)
            def _():
                dg_ref[...] += part

    row = pl.BlockSpec((tm, D), lambda i, k: (i, 0))
    return _pcall(
        body, hosted, name=name, grid=(T // tm, nk),
        in_specs=[dy_spec(p) for p in range(n_dy)] + [
            w_spec, row, pl.BlockSpec((None, 1, D), lambda i, k: (layer, 0, 0)), row],
        out_specs=[row, row, pl.BlockSpec((1, D), lambda i, k: (0, 0))],
        out_shape=[SDS((T, D), F32), SDS((T, D), BF16), SDS((1, D), F32)],
        scratch_shapes=[pltpu.VMEM((tm, D), F32)],
        semantics=("arbitrary", "arbitrary"),
    )(*dys, w, x, gain, dres)


def matmul_tn(a, bs, name, b_col0=0, n_cols=None, by_dest=False, tn=None, tk=None, hosted=None):
    T, M = a.shape
    width = bs[0].shape[1]
    N = n_cols if n_cols else width * len(bs)
    tm = _tile(M, (1024, 512, 256, 128))
    tn = tn or _tile(N, (512, 256, 128))
    tk = tk or _tile(T, (4096, 2048, 1024, 512, 256))
    assert b_col0 % tn == 0 and width % tn == 0
    j0, per, nk, n_b = b_col0 // tn, width // tn, T // tk, len(bs)

    def b_spec(p):
        return pl.BlockSpec((tk, tn), lambda i, j, k: (k, jnp.clip(j0 + j - p * per, 0, per - 1)))

    def body(*refs):
        a_ref, b_refs = refs[0], refs[1:1 + n_b]
        o_ref, acc_ref = refs[-2], refs[-1]
        j, k = pl.program_id(1), pl.program_id(2)

        @pl.when(k == 0)
        def _():
            acc_ref[...] = jnp.zeros_like(acc_ref)

        for p in range(n_b):
            @pl.when((j0 + j >= p * per) & (j0 + j < (p + 1) * per))
            def _(p=p):
                acc_ref[...] += _dot_tn(a_ref[...], b_refs[p][...])

        @pl.when(k == nk - 1)
        def _():
            o_ref[...] = acc_ref[...].astype(BF16)

    if by_dest:
        cs = N // N_CHIPS
        npd = cs // tn
        out_shape = SDS((N_CHIPS, M, cs), BF16)
        out_spec = pl.BlockSpec((None, tm, tn), lambda i, j, k: (j // npd, i, j % npd))
    else:
        out_shape = SDS((M, N), BF16)
        out_spec = pl.BlockSpec((tm, tn), lambda i, j, k: (i, j))
    return _pcall(
        body, hosted, name=name, grid=(M // tm, N // tn, nk),
        in_specs=[pl.BlockSpec((tk, tm), lambda i, j, k: (k, i))] + [b_spec(p) for p in range(n_b)],
        out_specs=out_spec, out_shape=out_shape,
        scratch_shapes=[pltpu.VMEM((tm, tn), F32)],
        semantics=("arbitrary", "arbitrary", "arbitrary"),
    )(a, *bs)


def ffn_down_fwd(ab, w_down, layer, x1, hosted=None):
    T, D = x1.shape
    F = w_down.shape[1]
    tm = _tile(T, (512, 256, 128))
    tk = F // 2
    nk = F // tk

    def body(a_ref, b_ref, w_ref, x_ref, x2_ref, s_ref, acc_ref):
        k = pl.program_id(1)

        @pl.when(k == 0)
        def _():
            acc_ref[...] = x_ref[...]

        a = a_ref[...].astype(F32)
        s = (a * _sigmoid(a) * b_ref[...].astype(F32)).astype(BF16)
        s_ref[...] = s
        acc_ref[...] += _dot(s, w_ref[...])

        @pl.when(k == nk - 1)
        def _():
            x2_ref[...] = acc_ref[...]

    return _pcall(
        body, hosted, name="ffn_down_fwd", grid=(T // tm, nk),
        in_specs=[pl.BlockSpec((tm, tk), lambda i, k: (i, k)),
                  pl.BlockSpec((tm, tk), lambda i, k: (i, nk + k)),
                  pl.BlockSpec((None, tk, D), lambda i, k: (layer, k, 0)),
                  pl.BlockSpec((tm, D), lambda i, k: (i, 0))],
        out_specs=[pl.BlockSpec((tm, D), lambda i, k: (i, 0)),
                   pl.BlockSpec((tm, tk), lambda i, k: (i, k))],
        out_shape=[SDS((T, D), F32), SDS((T, F), BF16)],
        scratch_shapes=[pltpu.VMEM((tm, D), F32)],
        semantics=("arbitrary", "arbitrary"),
    )(ab, ab, w_down, x1)


def ffn_down_bwd(dx2b, w_down, layer, ab, hosted=None):
    T, D = dx2b.shape
    F = w_down.shape[1]
    tm = _tile(T, (512, 256, 128))
    tn = F // 2
    nj = F // tn

    def body(dx_ref, w_ref, a_ref, b_ref, da_ref, db_ref):
        ds = _dot_nt(dx_ref[...], w_ref[...])
        a = a_ref[...].astype(F32)
        sg = _sigmoid(a)
        da_ref[...] = (ds * b_ref[...].astype(F32) * (sg * (1.0 + a * (1.0 - sg)))).astype(BF16)
        db_ref[...] = (ds * (a * sg)).astype(BF16)

    blk = pl.BlockSpec((tm, tn), lambda i, j: (i, j))
    return _pcall(
        body, hosted, name="ffn_down_bwd", grid=(T // tm, nj),
        in_specs=[pl.BlockSpec((tm, D), lambda i, j: (i, 0)),
                  pl.BlockSpec((None, tn, D), lambda i, j: (layer, j, 0)),
                  blk, pl.BlockSpec((tm, tn), lambda i, j: (i, nj + j))],
        out_specs=[blk, blk],
        out_shape=[SDS((T, F), BF16), SDS((T, F), BF16)],
        semantics=("arbitrary", "arbitrary"),
    )(dx2b, w_down, ab, ab)


def _mix_specs(tm, D, layer):
    cs = D // N_CHIPS
    row = lambda w: pl.BlockSpec((tm, w), lambda i: (i, 0))
    wp = pl.BlockSpec((N_CHIPS, None, BRANCH_W, cs), lambda i: (0, layer, 0, 0))
    wo = pl.BlockSpec((None, N_CHIPS, cs, D), lambda i: (layer, 0, 0, 0))
    bg = pl.BlockSpec((None, 1, 3 * D), lambda i: (layer, 0, 0))
    return row, wp, wo, bg


def mix_fwd(ao, po, co, proj, b_gate, wpa, wpp, wpc, w_out, layer, x, hosted=None):
    T, D = x.shape
    cs = D // N_CHIPS
    tm = _tile(T, (256, 128))
    row, wp, wo, bg = _mix_specs(tm, D, layer)

    def body(ao_ref, po_ref, co_ref, g_ref, bg_ref, wpa_ref, wpp_ref, wpc_ref, wo_ref, x_ref,
             x1_ref, ys_ref, mixed_ref):
        mixed = jnp.zeros((tm, D), F32)
        for n, (br, wp_ref) in enumerate(((ao_ref, wpa_ref), (po_ref, wpp_ref), (co_ref, wpc_ref))):
            y = jnp.concatenate([_dot(br[...], wp_ref[j]) for j in range(N_CHIPS)], axis=1)
            cols = slice(n * D, (n + 1) * D)
            gate = _sigmoid(g_ref[:, cols].astype(F32) + bg_ref[:, cols])
            ys_ref[:, cols] = y.astype(BF16)
            mixed = mixed + gate * y
        mb = mixed.astype(BF16)
        mixed_ref[...] = mb
        acc = x_ref[...]
        for j in range(N_CHIPS):
            acc = acc + _dot(mb[:, j * cs:(j + 1) * cs], wo_ref[j])
        x1_ref[...] = acc

    return _pcall(
        body, hosted, name="mix_fwd", grid=(T // tm,),
        in_specs=[row(BRANCH_W), row(BRANCH_W), row(BRANCH_W), row(3 * D), bg, wp, wp, wp, wo, row(D)],
        out_specs=[row(D), row(3 * D), row(D)],
        out_shape=[SDS((T, D), F32), SDS((T, 3 * D), BF16), SDS((T, D), BF16)],
        semantics=("arbitrary",),
    )(ao, po, co, proj, b_gate, wpa, wpp, wpc, w_out, x)


def mix_bwd(dx1b, w_out, proj, b_gate, ys, wpa, wpp, wpc, layer, width, hosted=None):
    T, D = dx1b.shape
    cs = D // N_CHIPS
    tm = _tile(T, (256, 128))
    row, wp, wo, bg = _mix_specs(tm, D, layer)

    def body(dx_ref, wo_ref, g_ref, bg_ref, ys_ref, wpa_ref, wpp_ref, wpc_ref,
             dys_ref, dg_ref, dao_ref, dpo_ref, dco_ref, dbg_ref):
        i = pl.program_id(0)
        dx = dx_ref[...]
        dmixed = jnp.concatenate([_dot_nt(dx, wo_ref[j]) for j in range(N_CHIPS)], axis=1)
        for n, (wp_ref, dbr) in enumerate(((wpa_ref, dao_ref), (wpp_ref, dpo_ref), (wpc_ref, dco_ref))):
            cols = slice(n * D, (n + 1) * D)
            gate = _sigmoid(g_ref[:, cols].astype(F32) + bg_ref[:, cols])
            dy = (dmixed * gate).astype(BF16)
            dys_ref[:, cols] = dy
            dgp = dmixed * ys_ref[:, cols].astype(F32) * gate * (1.0 - gate)
            dg_ref[:, cols] = dgp.astype(BF16)
            part = jnp.sum(dgp, axis=0, keepdims=True)

            @pl.when(i == 0)
            def _():
                dbg_ref[:, cols] = part

            @pl.when(i > 0)
            def _():
                dbg_ref[:, cols] += part

            acc = jnp.zeros((tm, BRANCH_W), F32)
            for j in range(N_CHIPS):
                acc = acc + _dot_nt(dy[:, j * cs:(j + 1) * cs], wp_ref[j])
            dbr[...] = acc.astype(BF16)

    return _pcall(
        body, hosted, name="mix_bwd", grid=(T // tm,),
        in_specs=[row(D), wo, row(3 * D), bg, row(3 * D), wp, wp, wp],
        out_specs=[row(3 * D), row(3 * D), row(BRANCH_W), row(BRANCH_W), row(BRANCH_W),
                   pl.BlockSpec((1, 3 * D), lambda i: (0, 0))],
        out_shape=[SDS((T, 3 * D), BF16), SDS((T, width), BF16), SDS((T, BRANCH_W), BF16),
                   SDS((T, BRANCH_W), BF16), SDS((T, BRANCH_W), BF16), SDS((1, 3 * D), F32)],
        semantics=("arbitrary",),
    )(dx1b, w_out, proj, b_gate, ys, wpa, wpp, wpc)


def loss_head(x2, gain, target):
    T, D = x2.shape
    tm = _tile(T, (512, 256, 128))

    def body(x_ref, g_ref, t_ref, loss_ref, dx_ref, dxb_ref, dg_ref):
        i = pl.program_id(0)
        xf = x_ref[...]
        g = g_ref[...]
        r = lax.rsqrt(jnp.mean(xf * xf, axis=-1, keepdims=True) + RMS_EPS)
        xhat = xf * r
        diff = xhat * g - t_ref[...]
        part_loss = 0.5 * jnp.sum(jnp.mean(diff * diff, axis=-1, keepdims=True), axis=0, keepdims=True)
        dy = diff * (1.0 / D)
        dhg = dy * g
        dx = r * (dhg - xhat * jnp.mean(dhg * xhat, axis=-1, keepdims=True))
        dx_ref[...] = dx
        dxb_ref[...] = dx.astype(BF16)
        part_g = jnp.sum(dy * xhat, axis=0, keepdims=True)
        part_l = jnp.broadcast_to(part_loss, (1, LANES))

        @pl.when(i == 0)
        def _():
            dg_ref[...] = part_g
            loss_ref[...] = part_l

        @pl.when(i > 0)
        def _():
            dg_ref[...] += part_g
            loss_ref[...] += part_l

    row = pl.BlockSpec((tm, D), lambda i: (i, 0))
    return pl.pallas_call(
        body, name="loss_head", grid=(T // tm,),
        in_specs=[row, pl.BlockSpec((1, D), lambda i: (0, 0)), row],
        out_specs=[pl.BlockSpec((1, LANES), lambda i: (0, 0)), row, row, pl.BlockSpec((1, D), lambda i: (0, 0))],
        out_shape=[SDS((1, LANES), F32), SDS((T, D), F32), SDS((T, D), BF16), SDS((1, D), F32)],
        compiler_params=_params("arbitrary"),
    )(x2, gain, target)


def _placement_constants():
    w = HEADS * HEAD_PAD
    pq = np.zeros((BRANCH_W, w), np.float32)
    pk = np.zeros((BRANCH_W, w), np.float32)
    pfq = np.zeros((3, LANES, w), np.float32)
    pfk = np.zeros((3, LANES, w), np.float32)
    cq = np.zeros((1, w), np.float32)
    ck = np.zeros((1, w), np.float32)
    eq = np.zeros((w, LANES), np.float32)
    ek = np.zeros((w, LANES), np.float32)
    for h in range(HEADS):
        for d in range(HEAD_DIM):
            pq[h * HEAD_DIM + d, h * HEAD_PAD + d] = HEAD_DIM ** -0.5
            pk[h * HEAD_DIM + d, h * HEAD_PAD + d] = 1.0
        for i in range(3):
            pfq[i, h, h * HEAD_PAD + HEAD_DIM + i] = 1.0
            pfk[i, h, h * HEAD_PAD + HEAD_DIM + 3 + i] = -1.0
            cq[0, h * HEAD_PAD + HEAD_DIM + 3 + i] = 1.0
            ck[0, h * HEAD_PAD + HEAD_DIM + i] = 1.0
        eq[h * HEAD_PAD + HEAD_DIM, h] = 1.0
        ek[h * HEAD_PAD + HEAD_DIM + 3, h] = -1.0
    bf = lambda a: jnp.asarray(a, BF16)
    return dict(pq=bf(pq), pk=bf(pk), pfq=bf(pfq), pfk=bf(pfk), cq=jnp.asarray(cq), ck=jnp.asarray(ck),
                pqkt=bf(np.concatenate([pq.T, pk.T], axis=0)), eq=bf(eq), ek=bf(ek))


def attn_prep(proj3, bf_rows, layer, cst, lay, hosted=None):
    Bl, S, _ = proj3.shape
    ts = ATTN_BLOCK
    w = HEADS * HEAD_PAD

    def body(q_ref, k_ref, f_ref, bf_ref, pq_ref, pk_ref, pfq_ref, pfk_ref, cq_ref, ck_ref,
             qa_ref, ka_ref, carry_ref):
        @pl.when(pl.program_id(1) == 0)
        def _():
            carry_ref[...] = jnp.zeros_like(carry_ref)

        z = f_ref[...].astype(F32) + bf_ref[...]
        logf = jnp.minimum(z, 0.0) - jnp.log(1.0 + jnp.exp(-jnp.abs(z)))
        r = lax.broadcasted_iota(jnp.int32, (ts, ts), 0)
        c = lax.broadcasted_iota(jnp.int32, (ts, ts), 1)
        tri = jnp.where(r >= c, 1.0, 0.0).astype(BF16)
        fcum = carry_ref[...]
        for part in _split3(logf):
            fcum = fcum + _dot(tri, part)
        carry_ref[...] = fcum[ts - 1:ts, :]
        qa = _dot(q_ref[...], pq_ref[...]) + cq_ref[...]
        ka = _dot(k_ref[...], pk_ref[...]) + ck_ref[...]
        for i, part in enumerate(_split3(fcum)):
            qa = qa + _dot(part, pfq_ref[i])
            ka = ka + _dot(part, pfk_ref[i])
        qa_ref[...] = qa.astype(BF16)
        ka_ref[...] = ka.astype(BF16)

    cfull = lambda shape: pl.BlockSpec(shape, lambda b, s: (0,) * len(shape))
    return _pcall(
        body, hosted, name="attn_prep", grid=(Bl, S // ts),
        in_specs=[pl.BlockSpec((None, ts, BRANCH_W), lambda b, s: (b, s, lay["q"] // BRANCH_W)),
                  pl.BlockSpec((None, ts, BRANCH_W), lambda b, s: (b, s, lay["k"] // BRANCH_W)),
                  pl.BlockSpec((None, ts, LANES), lambda b, s: (b, s, lay["f"] // LANES)),
                  pl.BlockSpec((None, 1, LANES), lambda b, s: (layer, 0, 0)),
                  cfull((BRANCH_W, w)), cfull((BRANCH_W, w)),
                  cfull((3, LANES, w)), cfull((3, LANES, w)), cfull((1, w)), cfull((1, w))],
        out_specs=[pl.BlockSpec((None, ts, w), lambda b, s: (b, s, 0)),
                   pl.BlockSpec((None, ts, w), lambda b, s: (b, s, 0))],
        out_shape=[SDS((Bl, S, w), BF16), SDS((Bl, S, w), BF16)],
        scratch_shapes=[pltpu.VMEM((1, LANES), F32)],
        semantics=("arbitrary", "arbitrary"),
    )(proj3, proj3, proj3, bf_rows, cst["pq"], cst["pk"], cst["pfq"], cst["pfk"], cst["cq"], cst["ck"])


def attn_fwd(qa, ka, proj3, lay, hosted=None):
    Bl, S, _ = qa.shape
    tq = ATTN_BLOCK
    nq = S // tq
    pairs = HEADS // 2
    pw = 2 * HEAD_PAD
    vw = 2 * HEAD_DIM

    def body(qa_ref, ka_ref, v_ref, o_ref, lse_ref):
        row = lax.broadcasted_iota(jnp.int32, (tq, tq), 0)
        col = lax.broadcasted_iota(jnp.int32, (tq, tq), 1)
        causal = row <= col
        for i in range(nq):
            nk = (i + 1) * tq
            rows = slice(i * tq, nk)
            o_t = []
            for h in range(2):
                hs = slice(h * HEAD_PAD, (h + 1) * HEAD_PAD)
                st = _dot_nt(ka_ref[0:nk, hs], qa_ref[rows, hs])
                diag = jnp.where(causal, st[nk - tq:], NEG_INF)
                m = jnp.max(diag, axis=0, keepdims=True)
                if i:
                    m = jnp.maximum(m, jnp.max(st[:nk - tq], axis=0, keepdims=True))
                p_diag = jnp.exp(diag - m)
                l = jnp.sum(p_diag, axis=0, keepdims=True)
                if i:
                    p_top = jnp.exp(st[:nk - tq] - m)
                    l = l + jnp.sum(p_top, axis=0, keepdims=True)
                    p = jnp.concatenate([p_top.astype(BF16), p_diag.astype(BF16)], axis=0)
                else:
                    p = p_diag.astype(BF16)
                acc = _dot_tn(v_ref[0:nk, :], p)
                o_t.append(acc[h * HEAD_DIM:(h + 1) * HEAD_DIM, :] / l)
                lse_ref[h:h + 1, rows] = m + jnp.log(l)
            o_ref[rows, :] = jnp.concatenate(o_t, axis=0).T.astype(BF16)

    return _pcall(
        body, hosted, name="attn_fwd", grid=(Bl, pairs),
        in_specs=[pl.BlockSpec((None, S, pw), lambda b, p: (b, 0, p)),
                  pl.BlockSpec((None, S, pw), lambda b, p: (b, 0, p)),
                  pl.BlockSpec((None, S, vw), lambda b, p: (b, 0, lay["v"] // vw + p))],
        out_specs=[pl.BlockSpec((None, S, vw), lambda b, p: (b, 0, p)),
                   pl.BlockSpec((None, None, 2, S), lambda b, p: (b, p, 0, 0))],
        out_shape=[SDS((Bl, S, BRANCH_W), BF16), SDS((Bl, pairs, 2, S), F32)],
        semantics=("arbitrary", "arbitrary"),
    )(qa, ka, proj3)


def attn_bwd(qa, ka, proj3, dao, ao, lse, dproj3, lay, hosted=None):
    Bl, S, _ = qa.shape
    tk = ATTN_BLOCK
    nq = S // tk
    pairs = HEADS // 2
    pw = 2 * HEAD_PAD
    vw = 2 * HEAD_DIM

    def body(qa_ref, ka_ref, v_ref, do_ref, o_ref, lse_ref, _, dqa_ref, dka_ref, dv_ref):
        row = lax.broadcasted_iota(jnp.int32, (tk, tk), 0)
        col = lax.broadcasted_iota(jnp.int32, (tk, tk), 1)
        causal = row <= col
        lane8 = lax.broadcasted_iota(jnp.int32, (8, vw), 1)
        lane_s = lax.broadcasted_iota(jnp.int32, (S, vw), 1)
        lane_k = lax.broadcasted_iota(jnp.int32, (tk, vw), 1)
        doo = do_ref[...].astype(F32) * o_ref[...].astype(F32)
        hi = doo.astype(BF16)
        lo = (doo - hi.astype(F32)).astype(BF16)
        delta, v_head = [], []
        for h in range(2):
            sel = jnp.where((lane8 >= h * HEAD_DIM) & (lane8 < (h + 1) * HEAD_DIM), 1.0, 0.0).astype(BF16)
            delta.append((_dot_nt(sel, hi) + _dot_nt(sel, lo))[0:1, :])
            in_head = (lane_s >= h * HEAD_DIM) & (lane_s < (h + 1) * HEAD_DIM)
            v_head.append(jnp.where(in_head, v_ref[...], jnp.zeros_like(v_ref[...])))
        dqa_ref[...] = jnp.zeros_like(dqa_ref)
        for j in range(nq):
            q0 = j * tk
            krows = slice(q0, q0 + tk)
            do = do_ref[q0:, :]
            dvs = []
            for h in range(2):
                hs = slice(h * HEAD_PAD, (h + 1) * HEAD_PAD)
                k = ka_ref[krows, hs]
                q = qa_ref[q0:, hs]
                st = _dot_nt(k, q)
                p = jnp.exp(st - lse_ref[h:h + 1, q0:])
                p_diag = jnp.where(causal, p[:, :tk], 0.0)
                p = jnp.concatenate([p_diag, p[:, tk:]], axis=1) if j < nq - 1 else p_diag
                dvs.append(_dot(p.astype(BF16), do))
                dpt = _dot_nt(v_head[h][krows, :], do)
                ds = (p * (dpt - delta[h][:, q0:])).astype(BF16)
                dka_ref[krows, hs] = _dot(ds, q)
                dqa_ref[q0:, hs] += _dot_tn(ds, k)
            dv_ref[krows, :] = jnp.where(lane_k < HEAD_DIM, dvs[0], dvs[1]).astype(BF16)

    seq = lambda w, c0=0: pl.BlockSpec((None, S, w), lambda b, p: (b, 0, c0 + p))
    return _pcall(
        body, hosted, name="attn_bwd", grid=(Bl, pairs),
        in_specs=[seq(pw), seq(pw), seq(vw, lay["v"] // vw), seq(vw), seq(vw),
                  pl.BlockSpec((None, None, 2, S), lambda b, p: (b, p, 0, 0)), _ANY],
        out_specs=[seq(pw), seq(pw), seq(vw, lay["v"] // vw)],
        out_shape=[SDS((Bl, S, HEADS * HEAD_PAD), F32), SDS((Bl, S, HEADS * HEAD_PAD), F32),
                   SDS(dproj3.shape, BF16)],
        aliases={6: 2}, semantics=("arbitrary", "arbitrary"),
    )(qa, ka, proj3, dao, ao, lse, dproj3)


def attn_post(dqa, dka, proj3, bf_rows, layer, dproj3, cst, lay):
    Bl, S, w = dqa.shape
    ts = ATTN_BLOCK
    ns = S // ts
    qkf = 2 * BRANCH_W + F_PAD

    def body(dqa_ref, dka_ref, f_ref, bf_ref, pqkt_ref, eq_ref, ek_ref, _, dqkf_ref, dbf_ref, carry_ref):
        b, s = pl.program_id(0), pl.program_id(1)

        @pl.when(s == 0)
        def _():
            carry_ref[...] = jnp.zeros_like(carry_ref)

        dqa_v, dka_v = dqa_ref[...], dka_ref[...]
        qh = dqa_v.astype(BF16)
        kh = dka_v.astype(BF16)
        dqkf_ref[:, :BRANCH_W] = _dot(qh, pqkt_ref[:w, :]).astype(BF16)
        dqkf_ref[:, BRANCH_W:2 * BRANCH_W] = _dot(kh, pqkt_ref[w:, :]).astype(BF16)
        ql = (dqa_v - qh.astype(F32)).astype(BF16)
        kl = (dka_v - kh.astype(F32)).astype(BF16)
        d_f = (_dot(qh, eq_ref[...]) + _dot(ql, eq_ref[...])) + (_dot(kh, ek_ref[...]) + _dot(kl, ek_ref[...]))
        r = lax.broadcasted_iota(jnp.int32, (ts, ts), 0)
        c = lax.broadcasted_iota(jnp.int32, (ts, ts), 1)
        triu = jnp.where(c >= r, 1.0, 0.0).astype(BF16)
        rev = carry_ref[...]
        for part in _split3(d_f):
            rev = rev + _dot(triu, part)
        carry_ref[...] = rev[0:1, :]
        z = f_ref[...].astype(F32) + bf_ref[...]
        lane = lax.broadcasted_iota(jnp.int32, (ts, LANES), 1)
        dfl = jnp.where(lane < HEADS, rev / (1.0 + jnp.exp(z)), 0.0)
        dqkf_ref[:, 2 * BRANCH_W:] = jnp.concatenate(
            [dfl.astype(BF16), jnp.zeros((ts, F_PAD - LANES), BF16)], axis=1)
        part = jnp.sum(dfl, axis=0, keepdims=True)

        @pl.when((b == 0) & (s == 0))
        def _():
            dbf_ref[...] = part

        @pl.when((b > 0) | (s > 0))
        def _():
            dbf_ref[...] += part

    assert lay["q"] % qkf == 0
    cfull = lambda shape: pl.BlockSpec(shape, lambda b, s: (0,) * len(shape))
    rev_blk = lambda wd, c0=0: pl.BlockSpec((None, ts, wd), lambda b, s: (b, ns - 1 - s, c0))
    return pl.pallas_call(
        body, name="attn_post", grid=(Bl, ns),
        in_specs=[rev_blk(w), rev_blk(w), rev_blk(LANES, lay["f"] // LANES),
                  pl.BlockSpec((None, 1, LANES), lambda b, s: (layer, 0, 0)),
                  cfull((2 * w, BRANCH_W)), cfull((w, LANES)), cfull((w, LANES)), _ANY],
        out_specs=[rev_blk(qkf, lay["q"] // qkf), cfull((1, LANES))],
        out_shape=[SDS(dproj3.shape, BF16), SDS((1, LANES), F32)],
        scratch_shapes=[pltpu.VMEM((1, LANES), F32)],
        input_output_aliases={7: 0},
        compiler_params=_params("arbitrary", "arbitrary"),
    )(dqa, dka, proj3, bf_rows, cst["pqkt"], cst["eq"], cst["ek"], dproj3)


def _shift_down(x, k, row):
    return jnp.where(row >= k, pltpu.roll(x, k, axis=0), 0.0)


def _shift_up(x, k, row):
    n = x.shape[0]
    return jnp.where(row < n - k, pltpu.roll(x, n - k, axis=0), 0.0)


def _window_sum(x, g, row, shift):
    s2 = x + shift(x, 1, row)
    s4 = s2 + shift(s2, 2, row)
    s8 = s4 + shift(s4, 4, row)
    s16 = s8 + shift(s8, 8, row)
    return jnp.where(g == 0, s2, jnp.where(g == 1, s4, jnp.where(g == 2, s8, s16)))


def _window_count(g, row):
    wnd = jnp.where(g == 0, 2, jnp.where(g == 1, 4, jnp.where(g == 2, 8, 16)))
    return jnp.minimum(row + 1, wnd).astype(F32)


def _group_columns(ref):
    return [ref[:, n * GROUP_W:(n + 1) * GROUP_W].astype(F32) for n in range(4)]


def poolconv_fwd(proj3, pool_w, pool_scale, conv_w, layer, lay, hosted=None):
    Bl, S, _ = proj3.shape

    def body(x_ref, pw_ref, ps_ref, cw_ref, po_ref, co_ref):
        g = pl.program_id(1)
        row = lax.broadcasted_iota(jnp.int32, (S, GROUP_W), 0)
        u, cv, cb, cc = _group_columns(x_ref)
        d = _window_sum(u, g, row, _shift_down) / _window_count(g, row) - u
        po_ref[...] = (_dot(d.astype(BF16), pw_ref[...]) * ps_ref[...]).astype(BF16)
        z = cc * cv
        y = cw_ref[0:1, :] * _shift_down(z, 2, row) + cw_ref[1:2, :] * _shift_down(z, 1, row) + cw_ref[2:3, :] * z
        co_ref[...] = (cb * y).astype(BF16)

    out = pl.BlockSpec((None, S, GROUP_W), lambda b, g: (b, 0, g))
    return _pcall(
        body, hosted, name="poolconv_fwd", grid=(Bl, N_GROUPS),
        in_specs=[pl.BlockSpec((None, S, BRANCH_W), lambda b, g: (b, 0, lay["pc"] // BRANCH_W + g)),
                  pl.BlockSpec((None, None, GROUP_W, GROUP_W), lambda b, g: (layer, g, 0, 0)),
                  pl.BlockSpec((None, 1, GROUP_W), lambda b, g: (layer, 0, g)),
                  pl.BlockSpec((None, None, 3, GROUP_W), lambda b, g: (g, layer, 0, 0))],
        out_specs=[out, out],
        out_shape=[SDS((Bl, S, BRANCH_W), BF16), SDS((Bl, S, BRANCH_W), BF16)],
        semantics=("arbitrary", "arbitrary"),
    )(proj3, pool_w, pool_scale, conv_w)


def poolconv_bwd(proj3, dpo, dco, pool_w, pool_scale, conv_w, layer, dproj3, lay):
    Bl, S, _ = proj3.shape

    def body(x_ref, dpo_ref, dco_ref, pw_ref, ps_ref, cw_ref, _, dx_ref, dpw_ref, dps_ref, dcw_ref):
        g, b = pl.program_id(0), pl.program_id(1)
        row = lax.broadcasted_iota(jnp.int32, (S, GROUP_W), 0)
        cnt = _window_count(g, row)
        u, cv, cb, cc = _group_columns(x_ref)
        d = (_window_sum(u, g, row, _shift_down) / cnt - u).astype(BF16)
        pw = pw_ref[...]
        ypre = _dot(d, pw)
        dpo_v = dpo_ref[...].astype(F32)
        dps = jnp.sum(dpo_v * ypre, axis=0, keepdims=True)
        dyp = (dpo_v * ps_ref[...]).astype(BF16)
        dpw = _dot_tn(d, dyp)
        dd = _dot_nt(dyp, pw)
        dx_ref[:, 0:GROUP_W] = (_window_sum(dd / cnt, g, row, _shift_up) - dd).astype(BF16)

        z = cc * cv
        z1, z2 = _shift_down(z, 1, row), _shift_down(z, 2, row)
        w0, w1, w2 = cw_ref[0:1, :], cw_ref[1:2, :], cw_ref[2:3, :]
        y = w0 * z2 + w1 * z1 + w2 * z
        dco_v = dco_ref[...].astype(F32)
        dy = dco_v * cb
        dz = w0 * _shift_up(dy, 2, row) + w1 * _shift_up(dy, 1, row) + w2 * dy
        dx_ref[:, GROUP_W:2 * GROUP_W] = (dz * cc).astype(BF16)
        dx_ref[:, 2 * GROUP_W:3 * GROUP_W] = (dco_v * y).astype(BF16)
        dx_ref[:, 3 * GROUP_W:] = (dz * cv).astype(BF16)
        dcw = jnp.concatenate([jnp.sum(dy * z2, axis=0, keepdims=True),
                               jnp.sum(dy * z1, axis=0, keepdims=True),
                               jnp.sum(dy * z, axis=0, keepdims=True)], axis=0)

        @pl.when(b == 0)
        def _():
            dpw_ref[...] = dpw
            dps_ref[...] = dps
            dcw_ref[...] = dcw

        @pl.when(b > 0)
        def _():
            dpw_ref[...] += dpw
            dps_ref[...] += dps
            dcw_ref[...] += dcw

    blk = pl.BlockSpec((None, S, GROUP_W), lambda g, b: (b, 0, g))
    pc = pl.BlockSpec((None, S, BRANCH_W), lambda g, b: (b, 0, lay["pc"] // BRANCH_W + g))
    return pl.pallas_call(
        body, name="poolconv_bwd", grid=(N_GROUPS, Bl),
        in_specs=[pc, blk, blk,
                  pl.BlockSpec((None, None, GROUP_W, GROUP_W), lambda g, b: (layer, g, 0, 0)),
                  pl.BlockSpec((None, 1, GROUP_W), lambda g, b: (layer, 0, g)),
                  pl.BlockSpec((None, None, 3, GROUP_W), lambda g, b: (g, layer, 0, 0)), _ANY],
        out_specs=[pc, pl.BlockSpec((None, GROUP_W, GROUP_W), lambda g, b: (g, 0, 0)),
                   pl.BlockSpec((1, GROUP_W), lambda g, b: (0, g)),
                   pl.BlockSpec((None, 3, GROUP_W), lambda g, b: (g, 0, 0))],
        out_shape=[SDS(dproj3.shape, BF16), SDS((N_GROUPS, GROUP_W, GROUP_W), F32), SDS((1, BRANCH_W), F32),
                   SDS((N_GROUPS, 3, GROUP_W), F32)],
        input_output_aliases={6: 0},
        compiler_params=_params("arbitrary", "arbitrary"),
    )(proj3, dpo, dco, pool_w, pool_scale, conv_w, dproj3)


def _tile_2d(rows, cols, n_arrays):
    budget = VMEM_LIMIT // 2
    lanes = -(-cols // LANES) * LANES
    if rows % 8 == 0:
        for t in (2048, 1024, 512, 256, 128, 64, 32, 16, 8):
            if rows % t == 0 and 2 * n_arrays * t * lanes * 4 <= budget:
                return t, cols
    for t in (1024, 512, 256, 128):
        if cols % t == 0 and 2 * n_arrays * (rows + 8) * t * 4 <= budget:
            return rows, t
    return rows, cols


def add_pair(kept, layer, where, received, name):
    _, n, _, R, C = kept.shape
    tr, tc = _tile_2d(R, C, 3)

    def body(where_ref, a_ref, b_ref, o_ref):
        o_ref[...] = (a_ref[...].astype(F32) + b_ref[...].astype(F32)).astype(BF16)

    blk = pl.BlockSpec((None, tr, tc), lambda d, i, j, where_ref: (d, i, j))
    grid_spec = pltpu.PrefetchScalarGridSpec(
        num_scalar_prefetch=1, grid=(n, R // tr, C // tc),
        in_specs=[pl.BlockSpec((None, None, None, tr, tc),
                               lambda d, i, j, where_ref: (layer, d, where_ref[0], i, j)), blk],
        out_specs=blk)
    return pl.pallas_call(body, name=name, grid_spec=grid_spec, out_shape=SDS((n, R, C), BF16),
                          compiler_params=_params("arbitrary", "arbitrary", "arbitrary"))(where, kept, received)


def add_chips(arrived, own, layer, where, n_layers, prev, name):
    _, R, C = arrived.shape
    tr, tc = _tile_2d(R, C, 6)

    def body(where_ref, a0, a1, a2, a3, own_ref, *rest):
        o_ref = rest[-1]
        chip = where_ref[1]
        acc = None
        for j, a_ref in enumerate((a0, a1, a2, a3)):
            term = jnp.where(chip == j, own_ref[...], a_ref[...]).astype(F32)
            acc = term if acc is None else acc + term
        o_ref[...] = acc

    def slot(j):
        return pl.BlockSpec((None, tr, tc), lambda i, k, where_ref, j=j: (
            jnp.where(where_ref[1] == j, (j + 1) % N_CHIPS, j), i, k))

    in_specs = [slot(j) for j in range(N_CHIPS)] + [
        pl.BlockSpec((None, tr, tc), lambda i, k, where_ref: (where_ref[1], i, k))]
    args = [where, arrived, arrived, arrived, arrived, own]
    aliases = {}
    if prev is not None:
        in_specs.append(_ANY)
        args.append(prev)
        aliases = {len(args) - 1: 0}
    grid_spec = pltpu.PrefetchScalarGridSpec(
        num_scalar_prefetch=1, grid=(R // tr, C // tc), in_specs=in_specs,
        out_specs=pl.BlockSpec((None, None, tr, tc), lambda i, k, where_ref: (layer, where_ref[0], i, k)))
    return pl.pallas_call(body, name=name, grid_spec=grid_spec, out_shape=SDS((n_layers, 2, R, C), F32),
                          input_output_aliases=aliases,
                          compiler_params=_params("arbitrary", "arbitrary"))(*args)


def adamw(w, g, m, v, name):
    if w.ndim == 2:
        R, C = w.shape
        tr, _ = _tile_2d(R, C, 7)
        grid, blk = (R // tr,), pl.BlockSpec((tr, C), lambda i: (i, 0))
    else:
        N, r, C = w.shape
        tn = max(t for t in range(1, N + 1) if N % t == 0 and t * r * C * 4 <= 512 * 1024)
        grid, blk = (N // tn,), pl.BlockSpec((tn, r, C), lambda i: (i, 0, 0))

    def body(w_ref, g_ref, m_ref, v_ref, d_ref, nm_ref, nv_ref):
        gv = g_ref[...]
        m_new = ADAM_B1 * m_ref[...] + (1.0 - ADAM_B1) * gv
        v_new = ADAM_B2 * v_ref[...] + (1.0 - ADAM_B2) * (gv * gv)
        m_hat = m_new / (1.0 - ADAM_B1 ** ADAM_STEP)
        v_hat = v_new / (1.0 - ADAM_B2 ** ADAM_STEP)
        d_ref[...] = -ADAM_LR * (m_hat / (jnp.sqrt(v_hat) + ADAM_EPS) + ADAM_WD * w_ref[...])
        nm_ref[...] = m_new
        nv_ref[...] = v_new

    out = SDS(w.shape, F32)
    return pl.pallas_call(body, name=name, grid=grid, in_specs=[blk] * 4, out_specs=[blk] * 3,
                          out_shape=[out, out, out], compiler_params=_params("arbitrary"))(w, g, m, v)


_COMM = pltpu.CompilerParams(has_side_effects=True)


def gather_buffers(shards):
    me_chip = 2 * lax.axis_index("x") + lax.axis_index("y")
    pool = {}
    for name, sh in shards.items():
        L, r, c = sh.shape
        if name in ROW_SHARDED:
            pool[name] = lax.dynamic_update_slice(lax.empty((L, N_CHIPS, r, c), sh.dtype), sh[:, None],
                                                  (0, me_chip, 0, 0))
        else:
            pool[name] = lax.dynamic_update_slice(lax.empty((N_CHIPS, L, r, c), sh.dtype), sh[None],
                                                  (me_chip, 0, 0, 0))
    return pool


def comm_now(pool, stages, name):
    stages = [Hosted(pool, jobs) for jobs in stages]
    names = sorted({m for st in stages for m in st.names})
    n = len(names)

    def body(*refs):
        bufs = dict(zip(names, refs[n:2 * n]))
        sems = refs[2 * n:]
        for i, st in enumerate(stages):
            plan = _hosted_plan(st, bufs, sems[2 * i], sems[2 * i + 1])
            _hosted_start(plan, True)
            _hosted_finish(plan, True)

    sem = pltpu.SemaphoreType.DMA
    scratch = []
    for st in stages:
        scratch += [sem((len(st.jobs), 3)), sem((len(st.jobs), 3))]
    res = pl.pallas_call(
        body, name=name, in_specs=[_ANY] * n, out_specs=[_ANY] * n,
        out_shape=[SDS(pool[m].shape, pool[m].dtype) for m in names],
        scratch_shapes=scratch, input_output_aliases={t: t for t in range(n)},
        compiler_params=_COMM,
    )(*[pool[m] for m in names])
    pool.update(zip(names, res))


def gather_now(pool, units):
    comm_now(pool, [[("ici", name, layer) for name, layer in units],
                    [("fwd", name, layer) for name, layer in units]], "gather_now")


def allgather_chips(buf, name):
    def body(src_ref, out_ref, send_sems, recv_sems, local_sem):
        x, y, c = _position()
        me = 2 * x + y
        mine = pltpu.make_async_copy(src_ref, out_ref.at[me], local_sem)
        mine.start()
        sends = []
        for k, (px, py) in enumerate(_other_chips(x, y)):
            cp = _remote(src_ref, out_ref.at[me], send_sems.at[k], recv_sems.at[k], (px, py, c))
            cp.start()
            sends.append(cp)
        for k, (px, py) in enumerate(_other_chips(x, y)):
            _remote(src_ref, out_ref.at[2 * px + py], send_sems.at[k], recv_sems.at[k], (px, py, c)).wait_recv()
        for cp in sends:
            cp.wait_send()
        mine.wait()

    sem = pltpu.SemaphoreType.DMA
    return pl.pallas_call(
        body, name=name, in_specs=[_ANY], out_specs=_ANY, out_shape=SDS((N_CHIPS,) + buf.shape, buf.dtype),
        scratch_shapes=[sem((3,)), sem((3,)), sem], compiler_params=_COMM,
    )(buf)


def swap_sibling(tensors, name):
    n = len(tensors)

    def body(*refs):
        srcs, outs, send_sems, recv_sems = refs[:n], refs[n:2 * n], refs[2 * n], refs[2 * n + 1]
        x, y, c = _position()
        cps = [_remote(srcs[t].at[1 - c], outs[t], send_sems.at[t], recv_sems.at[t], (x, y, 1 - c))
               for t in range(n)]
        for cp in cps:
            cp.start()
        for cp in cps:
            cp.wait()

    sem = pltpu.SemaphoreType.DMA
    return pl.pallas_call(
        body, name=name, in_specs=[_ANY] * n, out_specs=[_ANY] * n,
        out_shape=[SDS(t.shape[1:], t.dtype) for t in tensors],
        scratch_shapes=[sem((n,)), sem((n,))], compiler_params=_COMM,
    )(*tensors)


def exchange_chips(tensors, name):
    n = len(tensors)

    def body(*refs):
        srcs, outs = refs[:n], refs[n:2 * n]
        send_sems, recv_sems, local_sems = refs[2 * n:]
        x, y, c = _position()
        me = 2 * x + y
        others = _other_chips(x, y)
        cps = []
        for t in range(n):
            cp = pltpu.make_async_copy(srcs[t].at[me], outs[t].at[me], local_sems.at[t])
            cp.start()
            cps.append(cp)
        sends = []
        for t in range(n):
            for k, (px, py) in enumerate(others):
                cp = _remote(srcs[t].at[2 * px + py], outs[t].at[me], send_sems.at[t, k], recv_sems.at[t, k],
                             (px, py, c))
                cp.start()
                sends.append(cp)
        for t in range(n):
            for k, (px, py) in enumerate(others):
                _remote(srcs[t].at[me], outs[t].at[2 * px + py], send_sems.at[t, k], recv_sems.at[t, k],
                        (px, py, c)).wait_recv()
        for cp in sends:
            cp.wait_send()
        for cp in cps:
            cp.wait()

    sem = pltpu.SemaphoreType.DMA
    return pl.pallas_call(
        body, name=name, in_specs=[_ANY] * n, out_specs=[_ANY] * n,
        out_shape=[SDS(t.shape, t.dtype) for t in tensors],
        scratch_shapes=[sem((n, 3)), sem((n, 3)), sem((n,))], compiler_params=_COMM,
    )(*tensors)


def join_halves(tensors, name):
    n = len(tensors)

    def body(*refs):
        outs, send_sems, recv_sems = refs[n:2 * n], refs[2 * n], refs[2 * n + 1]
        x, y, c = _position()
        sib = (x, y, 1 - c)
        sends = []
        for t in range(n):
            cp = _remote(outs[t].at[c], outs[t].at[c], send_sems.at[t], recv_sems.at[t], sib)
            cp.start()
            sends.append(cp)
        for t in range(n):
            _remote(outs[t].at[c], outs[t].at[1 - c], send_sems.at[t], recv_sems.at[t], sib).wait_recv()
        for cp in sends:
            cp.wait_send()

    sem = pltpu.SemaphoreType.DMA
    return pl.pallas_call(
        body, name=name, in_specs=[_ANY] * n, out_specs=[_ANY] * n,
        out_shape=[SDS(t.shape, t.dtype) for t in tensors],
        scratch_shapes=[sem((n,)), sem((n,))], input_output_aliases={t: t for t in range(n)},
        compiler_params=_COMM,
    )(*tensors)


BIG = ("w_in", "w_proj_attn", "w_proj_pool", "w_proj_conv", "conv_w", "w_out", "w_gate_up", "w_down")
REPLICATED = ("attn_norm", "b_forget", "b_gate", "pool_w", "pool_scale", "ffn_norm", "final_norm")
ORDER = ("attn_norm", "w_in", "b_forget", "b_gate", "w_proj_attn", "pool_w", "pool_scale", "w_proj_pool",
         "conv_w", "w_proj_conv", "w_out", "ffn_norm", "w_gate_up", "w_down", "final_norm")


def _proj_layout(D):
    lay = {"g": 0, "q": 3 * D}
    lay["k"] = lay["q"] + BRANCH_W
    lay["f"] = lay["k"] + BRANCH_W
    lay["v"] = lay["f"] + F_PAD
    lay["pc"] = lay["v"] + BRANCH_W
    lay["width"] = lay["pc"] + 4 * BRANCH_W
    return lay


_REF = dict(q=0, k=512, v=1024, f=1536, u=1544, cv=2056, cb=2568, cc=3080, g=3592)


def _packed_pieces(D):
    pieces = [(_REF["g"], 3 * D), (_REF["q"], BRANCH_W), (_REF["k"], BRANCH_W), (_REF["f"], HEADS),
              (None, F_PAD - HEADS), (_REF["v"], BRANCH_W)]
    for gi in range(N_GROUPS):
        pieces += [(_REF[name] + gi * GROUP_W, GROUP_W) for name in ("u", "cv", "cb", "cc")]
    return pieces


def _packed_runs(D, cs):
    runs, at = [], 0
    for start, n in _packed_pieces(D):
        if start is None:
            runs.append((at, None, 0, n))
            at += n
        while start is not None and n:
            chip, off = divmod(start, cs)
            take = min(n, cs - off)
            runs.append((at, chip, off, take))
            at, start, n = at + take, start + take, n - take
    return runs


def pack_w_in(shards, layer):
    _, _, cs, D = shards.shape
    runs = _packed_runs(D, cs)
    width = runs[-1][0] + runs[-1][3]
    tc = _tile(D, (256, 128))

    def body(s_ref, o_ref):
        for dst, chip, off, rows in runs:
            if chip is None:
                o_ref[dst:dst + rows, :] = jnp.zeros((rows, tc), s_ref.dtype)
            else:
                o_ref[dst:dst + rows, :] = s_ref[chip, off:off + rows, :]

    return pl.pallas_call(
        body, name="pack_w_in", grid=(D // tc,),
        in_specs=[pl.BlockSpec((N_CHIPS, None, cs, tc), lambda j: (0, layer, 0, j))],
        out_specs=pl.BlockSpec((width, tc), lambda j: (0, j)),
        out_shape=SDS((width, D), shards.dtype), compiler_params=_params("arbitrary"),
    )(shards)


def unpack_w_in(p, cs):
    width, D = p.shape
    half = cs // 2
    runs = []
    for src, chip, off, rows in _packed_runs(D, cs):
        while chip is not None and rows:
            h, at = divmod(off, half)
            take = min(rows, half - at)
            runs.append((src, chip, h, at, take))
            src, off, rows = src + take, off + take, rows - take
    tc = _tile(D, (256, 128))

    def body(p_ref, o_ref):
        for src, chip, h, at, rows in runs:
            o_ref[chip, h, at:at + rows, :] = p_ref[src:src + rows, :]

    return pl.pallas_call(
        body, name="unpack_w_in", grid=(D // tc,),
        in_specs=[pl.BlockSpec((width, tc), lambda j: (0, j))],
        out_specs=pl.BlockSpec((N_CHIPS, 2, half, tc), lambda j: (0, 0, 0, j)),
        out_shape=SDS((N_CHIPS, 2, half, D), p.dtype), compiler_params=_params("arbitrary"),
    )(p)


def _split_flat(vec, shapes):
    out, at = [], 0
    for shp in shapes:
        n = int(np.prod(shp))
        out.append(vec[at:at + n].reshape(shp))
        at += n
    return out


def kernel(x, attn_norm, w_in, b_forget, b_gate, w_proj_attn, pool_w, pool_scale, w_proj_pool, conv_w, w_proj_conv, w_out, ffn_norm, w_gate_up, w_down, final_norm, loss_target, m_attn_norm, m_w_in, m_b_forget, m_b_gate, m_w_proj_attn, m_pool_w, m_pool_scale, m_w_proj_pool, m_conv_w, m_w_proj_conv, m_w_out, m_ffn_norm, m_w_gate_up, m_w_down, m_final_norm, v_attn_norm, v_w_in, v_b_forget, v_b_gate, v_w_proj_attn, v_pool_w, v_pool_scale, v_w_proj_pool, v_conv_w, v_w_proj_conv, v_w_out, v_ffn_norm, v_w_gate_up, v_w_down, v_final_norm):
    weights = dict(attn_norm=attn_norm, w_in=w_in, b_forget=b_forget, b_gate=b_gate, w_proj_attn=w_proj_attn,
                   pool_w=pool_w, pool_scale=pool_scale, w_proj_pool=w_proj_pool, conv_w=conv_w,
                   w_proj_conv=w_proj_conv, w_out=w_out, ffn_norm=ffn_norm, w_gate_up=w_gate_up, w_down=w_down,
                   final_norm=final_norm)
    mom_m = dict(attn_norm=m_attn_norm, w_in=m_w_in, b_forget=m_b_forget, b_gate=m_b_gate, w_proj_attn=m_w_proj_attn,
                 pool_w=m_pool_w, pool_scale=m_pool_scale, w_proj_pool=m_w_proj_pool, conv_w=m_conv_w,
                 w_proj_conv=m_w_proj_conv, w_out=m_w_out, ffn_norm=m_ffn_norm, w_gate_up=m_w_gate_up,
                 w_down=m_w_down, final_norm=m_final_norm)
    mom_v = dict(attn_norm=v_attn_norm, w_in=v_w_in, b_forget=v_b_forget, b_gate=v_b_gate, w_proj_attn=v_w_proj_attn,
                 pool_w=v_pool_w, pool_scale=v_pool_scale, w_proj_pool=v_w_proj_pool, conv_w=v_conv_w,
                 w_proj_conv=v_w_proj_conv, w_out=v_w_out, ffn_norm=v_ffn_norm, w_gate_up=v_w_gate_up,
                 w_down=v_w_down, final_norm=v_final_norm)

    Bl, S, D = x.shape
    T = Bl * S
    L = w_in.shape[0]
    F = w_down.shape[1] * N_CHIPS
    lay = _proj_layout(D)
    cst = _placement_constants()
    assert L == N_LAYERS and S % ATTN_BLOCK == 0 and F % (2 * LANES) == 0 and D % BRANCH_W == 0
    assert w_in.shape[2] * N_CHIPS == _REF["g"] + 3 * D and conv_w.shape[2] == GROUP_W

    send = {n: weights[n].astype(BF16) for n in BIG}
    send["conv_w"] = conv_w
    me_chip = 2 * lax.axis_index("x") + lax.axis_index("y")
    send["w_in"] = w_in.transpose(0, 2, 1).astype(BF16)
    pool = gather_buffers(send)
    gather_now(pool, [("w_in", 0)])
    rest = ("w_out", "w_proj_attn", "w_proj_pool", "w_gate_up", "w_proj_conv", "conv_w")
    late = ("w_out", "w_proj_attn", "w_proj_pool", "w_proj_conv", "conv_w")
    jobs = lambda kind, names, layer: [(kind, n, layer) for n in names]
    carried = {
        ("in_proj", 0): jobs("ici", rest, 0),
        ("attn_prep", 0): jobs("fwd", late, 0),
        ("attn_fwd", 0): jobs("fwd", ("w_gate_up",), 0) + jobs("ici", ("w_in",), 1) + jobs("ici", ("w_down",), 0),
        ("poolconv_fwd", 0): jobs("fwd", ("w_in",), 1) + jobs("fwd", ("w_down",), 0),
        ("mix_fwd", 0): jobs("ici", ("w_gate_up",), 1),
        ("gate_up_proj", 0): jobs("ici", late, 1),
        ("ffn_down_fwd", 0): jobs("fwd", ("w_gate_up",), 1),
        ("in_proj", 1): jobs("fwd", late, 1) + jobs("ici", ("w_down",), 1),
        ("attn_prep", 1): jobs("fwd", ("w_down",), 1),
    }
    carry = lambda call, layer: Hosted(pool, carried[call, layer]) if (call, layer) in carried else None
    w_down_f = lambda: pool["w_down"].reshape(L, F, D)
    pool_w_b = pool_w.astype(BF16)
    an3, fn3 = attn_norm.reshape(L, 1, D), ffn_norm.reshape(L, 1, D)
    bg3, ps3 = b_gate.reshape(L, 1, 3 * D), pool_scale.reshape(L, 1, BRANCH_W)
    bf3 = jnp.pad(b_forget, ((0, 0), (0, LANES - HEADS))).reshape(L, 1, LANES)

    xs = x.reshape(T, D)
    saved = []
    w_in_p = []
    for l in range(L):
        w_in_p.append(pack_w_in(pool["w_in"], l))
        proj, h = norm_matmul(xs, an3, w_in_p[l], l, "rows", "in_proj", carry("in_proj", l))
        proj3 = proj.reshape(Bl, S, lay["width"])
        qa, ka = attn_prep(proj3, bf3, l, cst, lay, carry("attn_prep", l))
        ao, lse = attn_fwd(qa, ka, proj3, lay, carry("attn_fwd", l))
        po, co = poolconv_fwd(proj3, pool_w_b, ps3, pool["conv_w"], l, lay, carry("poolconv_fwd", l))
        ao2, po2, co2 = (a.reshape(T, BRANCH_W) for a in (ao, po, co))
        x1, ys, mixed = mix_fwd(ao2, po2, co2, proj, bg3, pool["w_proj_attn"], pool["w_proj_pool"],
                                pool["w_proj_conv"], pool["w_out"], l, xs, carry("mix_fwd", l))
        ab, h2 = norm_matmul(x1, fn3, pool["w_gate_up"], l, "by_shard", "gate_up_proj", carry("gate_up_proj", l))
        x2, s_act = ffn_down_fwd(ab, w_down_f(), l, x1, carry("ffn_down_fwd", l))
        saved.append(dict(x=xs, proj=proj, proj3=proj3, h=h, qa=qa, ka=ka, ao=ao, lse=lse, ao2=ao2, po2=po2,
                          co2=co2, ys=ys, mixed=mixed, x1=x1, ab=ab, h2=h2, s=s_act))
        xs = x2
    w_gu, w_o, conv_w_g = pool["w_gate_up"], pool["w_out"], pool["conv_w"]
    wpa, wpp, wpc = pool["w_proj_attn"], pool["w_proj_pool"], pool["w_proj_conv"]
    w_down_f = w_down_f()

    loss_row, dx, dxb, g_final = loss_head(xs, final_norm.reshape(1, D), loss_target.reshape(T, D))
    loss = lax.psum(loss_row[0, 0], AXES)

    reduced_names = tuple(n for n in BIG if n != "conv_w")
    early_names = tuple(n for n in reduced_names if n != "w_in")
    where = jnp.stack([lax.axis_index("c"), me_chip]).astype(jnp.int32)
    rs = {}

    def reduce_begin(layer, grads):
        for n, g in grads.items():
            g5 = g.reshape((1, N_CHIPS, 2, -1) + g.shape[-1:])
            rs["g%d:%s" % (layer, n)] = g5
            for role in "ra":
                rs["%s%d:%s" % (role, layer, n)] = lax.empty((N_CHIPS,) + g5.shape[3:], BF16)

    swap_jobs = lambda layer, names: [("swap", "g%d:%s" % (layer, n), "r%d:%s" % (layer, n), 0) for n in names]
    xchg_jobs = lambda layer, names: [("xchg", "s%d:%s" % (layer, n), "a%d:%s" % (layer, n)) for n in names]
    join_jobs = lambda layer, names: [("join", "o:" + n, layer) for n in names]

    def pair_sums(layer, names):
        for n in names:
            rs["s%d:%s" % (layer, n)] = add_pair(rs["g%d:%s" % (layer, n)], 0, where, rs["r%d:%s" % (layer, n)],
                                                 "add_pair_" + n)

    def chip_sums(layer, names, slot, n_slots):
        for n in names:
            rs["o:" + n] = add_chips(rs["a%d:%s" % (layer, n)], rs["s%d:%s" % (layer, n)], slot, where, n_slots,
                                     rs.get("o:" + n), "add_chips_" + n)

    small = {n: [None] * L for n in REPLICATED if n != "final_norm"}
    g_conv = [None] * L
    to3 = lambda a: a.reshape(Bl, S, -1)
    for l in reversed(range(L)):
        sv = saved[l]
        behind = (lambda jobs: Hosted(rs, jobs)) if l == 0 else (lambda jobs: None)
        grads = {}
        da, db = ffn_down_bwd(dxb, w_down_f, l, sv["ab"], behind(swap_jobs(1, reduced_names)))
        if l == 0:
            pair_sums(1, reduced_names)
        grads["w_down"] = matmul_tn(sv["s"], [dxb], "grad_w_down", hosted=behind(xchg_jobs(1, ("w_down",))))
        grads["w_gate_up"] = matmul_tn(sv["h2"], [da, db], "grad_w_gate_up", by_dest=True, tn=2 * F // N_CHIPS,
                                       tk=_tile(T, (1024, 512, 256)), hosted=behind(xchg_jobs(1, ("w_gate_up",))))
        dx1, dx1b, g_fn = matmul_nt_normbwd([da, db], w_gu, l, "by_shard", sv["x1"], fn3, dx, "gate_up_bwd",
                                            behind(xchg_jobs(1, ("w_in", "w_out", "w_proj_attn", "w_proj_pool",
                                                                 "w_proj_conv"))))
        small["ffn_norm"][l] = g_fn[0]
        if l == 0:
            chip_sums(1, reduced_names, 1, L)
        dys, dproj, dao, dpo, dco, g_bg = mix_bwd(dx1b, w_o, sv["proj"], bg3, sv["ys"], wpa, wpp, wpc, l,
                                                  lay["width"], behind(join_jobs(1, reduced_names)))
        small["b_gate"][l] = g_bg[0]
        grads["w_out"] = matmul_tn(sv["mixed"], [dx1b], "grad_w_out")
        for n, (name, br) in enumerate((("w_proj_attn", sv["ao2"]), ("w_proj_pool", sv["po2"]),
                                        ("w_proj_conv", sv["co2"]))):
            grads[name] = matmul_tn(br, [dys], "grad_" + name, b_col0=n * D, n_cols=D, by_dest=True,
                                    tn=D // N_CHIPS)
        if l == 0:
            reduce_begin(0, grads)
        dqa, dka, dproj3 = attn_bwd(sv["qa"], sv["ka"], sv["proj3"], to3(dao), sv["ao"], sv["lse"], to3(dproj), lay,
                                    behind(swap_jobs(0, early_names)))
        if l == 0:
            pair_sums(0, early_names)
        dproj3, g_bf = attn_post(dqa, dka, sv["proj3"], bf3, l, dproj3, cst, lay)
        small["b_forget"][l] = g_bf[0, :HEADS]
        dproj3, g_pw, g_ps, g_conv[l] = poolconv_bwd(sv["proj3"], to3(dpo), to3(dco), pool_w_b, ps3, conv_w_g, l,
                                                     dproj3, lay)
        small["pool_w"][l], small["pool_scale"][l] = g_pw, g_ps[0]
        dproj = dproj3.reshape(T, lay["width"])
        g_w_in = unpack_w_in(matmul_tn(dproj, [sv["h"]], "grad_w_in", hosted=behind(xchg_jobs(
            0, ("w_gate_up",)))), w_in.shape[2])
        if l:
            reduce_begin(l, {**grads, "w_in": g_w_in})
        else:
            reduce_begin(0, {"w_in": g_w_in})
            comm_now(rs, [swap_jobs(0, ("w_in",))], "swap_w_in_halves")
            pair_sums(0, ("w_in",))
        dx, dxb, g_an = matmul_nt_normbwd([dproj], w_in_p[l], l, "rows", sv["x"], an3, dx1, "in_proj_bwd",
                                          behind(xchg_jobs(0, ("w_down", "w_out", "w_proj_attn", "w_proj_pool",
                                                               "w_proj_conv", "w_in"))))
        small["attn_norm"][l] = g_an[0]
    grad_x = dx.reshape(Bl, S, D)

    small_shapes = [weights[n].shape for n in REPLICATED] + [(L, N_CHIPS) + conv_w.shape[1:]]
    small_vec = jnp.concatenate([jnp.stack(small[n]).reshape(-1) for n in REPLICATED[:-1]]
                                + [g_final[0], jnp.stack(g_conv).reshape(-1)])
    n_small = small_vec.shape[0]
    small_vec = jnp.pad(small_vec, (0, -n_small % (2 * N_CHIPS * 16 * LANES))).astype(BF16)
    rs["g0:small"] = small_vec.reshape(1, N_CHIPS, 2, -1, LANES)
    for role in "ra":
        rs[role + "0:small"] = lax.empty((N_CHIPS,) + rs["g0:small"].shape[3:], BF16)
    last = ("small",)
    comm_now(rs, [swap_jobs(0, last)], "swap_grad_halves")
    pair_sums(0, last)
    comm_now(rs, [xchg_jobs(0, last)], "exchange_grad_chips")
    chip_sums(0, reduced_names, 0, L)
    chip_sums(0, ("small",), 0, 1)
    comm_now(rs, [join_jobs(0, reduced_names + ("small",))], "join_grad_halves")
    shard_grads = {n: rs["o:" + n].reshape((L, -1) + rs["o:" + n].shape[-1:]) for n in reduced_names}
    small_all = allgather_chips(rs["o:small"].reshape(-1, LANES), "allgather_small_grads").reshape(-1)[:n_small]
    *rep_list, conv_all = _split_flat(small_all, small_shapes)
    rep_grads = dict(zip(REPLICATED, rep_list))
    shard_grads["conv_w"] = lax.dynamic_index_in_dim(conv_all, me_chip, 1, keepdims=False)

    delta, new_m, new_v = {}, {}, {}
    for n in BIG:
        shp = weights[n].shape
        if n == "w_in":
            view, back = (lambda a: a.transpose(2, 0, 1)), (lambda a: a.transpose(1, 2, 0))
            g = shard_grads[n].transpose(1, 0, 2)
        else:
            view, back = (lambda a: a.reshape(-1, shp[-1])), (lambda a: a.reshape(shp))
            g = view(shard_grads[n])
        d, nm, nv = adamw(view(weights[n]), g, view(mom_m[n]), view(mom_v[n]), "adamw_" + n)
        delta[n], new_m[n], new_v[n], shard_grads[n] = back(d), back(nm), back(nv), back(g)

    def rows(d):
        vec = jnp.concatenate([d[n].reshape(-1) for n in REPLICATED])
        return jnp.pad(vec, (0, -vec.shape[0] % (8 * LANES))).reshape(-1, LANES)

    outs = adamw(rows(weights), rows(rep_grads), rows(mom_m), rows(mom_v), "adamw_replicated")
    for res, o in zip((delta, new_m, new_v), outs):
        res.update(zip(REPLICATED, _split_flat(o.reshape(-1), small_shapes[:len(REPLICATED)])))
    all_grads = {**shard_grads, **rep_grads}

    return (loss, grad_x, *[all_grads[n] for n in ORDER], *[delta[n] for n in ORDER],
            *[new_m[n] for n in ORDER], *[new_v[n] for n in ORDER])
```

```python
import numpy as np
import jax
import jax.numpy as jnp
from jax import lax
from jax.experimental import pallas as pl
from jax.experimental.pallas import tpu as pltpu

F32, BF16 = jnp.float32, jnp.bfloat16
SDS = jax.ShapeDtypeStruct
MESH = pl.DeviceIdType.MESH
AXES = ("x", "y", "c")
N_CHIPS = 4
N_LAYERS = 2
LANES = 128
VMEM_LIMIT = 48 * 1024 * 1024

HEADS, HEAD_DIM = 8, 64
HEAD_PAD = 128
BRANCH_W = 512
GROUP_W = 128
N_GROUPS = BRANCH_W // GROUP_W
POOL_WINDOWS = (2, 4, 8, 16)
F_PAD = 512
ATTN_BLOCK = 256
RMS_EPS = 1e-6
NEG_INF = -1e30
ADAM_LR, ADAM_B1, ADAM_B2, ADAM_EPS, ADAM_WD, ADAM_STEP = 0.001, 0.9, 0.999, 1e-08, 0.01, 10

NT = (((1,), (1,)), ((), ()))
TN = (((0,), (0,)), ((), ()))
_ANY = pl.BlockSpec(memory_space=pl.ANY)


def _tile(n, prefs):
    for p in prefs:
        if n % p == 0:
            return p
    raise ValueError(f"no tile of {prefs} divides {n}")


def _params(*sem):
    return pltpu.CompilerParams(dimension_semantics=sem, vmem_limit_bytes=VMEM_LIMIT)


def _sigmoid(z):
    return 0.5 * jnp.tanh(0.5 * z) + 0.5


def _split3(x):
    h1 = x.astype(BF16)
    r1 = x - h1.astype(F32)
    h2 = r1.astype(BF16)
    h3 = (r1 - h2.astype(F32)).astype(BF16)
    return h1, h2, h3


def _position():
    return lax.axis_index("x"), lax.axis_index("y"), lax.axis_index("c")


def _other_chips(x, y):
    return [(1 - x, y), (x, 1 - y), (1 - x, 1 - y)]


def _remote(src, dst, send_sem, recv_sem, device):
    return pltpu.make_async_remote_copy(src_ref=src, dst_ref=dst, send_sem=send_sem, recv_sem=recv_sem,
                                        device_id=device, device_id_type=MESH)


ROW_SHARDED = ("w_out", "w_down")
FETCHER = dict(w_in=0, w_out=0, w_proj_attn=0, w_proj_pool=0, w_gate_up=1, w_down=1, w_proj_conv=1, conv_w=1)


class Hosted:
    def __init__(self, pool, jobs):
        self.pool, self.jobs = pool, list(jobs)
        names = set()
        for job in self.jobs:
            names.update(job[1:3] if job[0] in ("swap", "xchg") else job[1:2])
        self.names = sorted(names)


def _hosted_plan(hosted, refs, send_sems, recv_sems):
    x, y, c = _position()
    me = 2 * x + y
    others = _other_chips(x, y)
    sibling = (x, y, 1 - c)
    plan = []
    for j, job in enumerate(hosted.jobs):
        kind = job[0]
        sems = lambda k, j=j: (send_sems.at[j, k], recv_sems.at[j, k])
        if kind in ("ici", "fwd"):
            _, name, layer = job
            ref = refs[name]
            win = (lambda chip, ref=ref, layer=layer: ref.at[layer, chip]) if name in ROW_SHARDED else (
                lambda chip, ref=ref, layer=layer: ref.at[chip, layer])
            mine = c == FETCHER[name]
            if kind == "ici":
                sends = [_remote(win(me), win(me), *sems(k), (px, py, c)) for k, (px, py) in enumerate(others)]
                arrivals = [_remote(win(2 * px + py), win(2 * px + py), *sems(k), (px, py, c))
                            for k, (px, py) in enumerate(others)]
                plan.append((mine, sends, arrivals, []))
            else:
                sends = [_remote(win(2 * px + py), win(2 * px + py), *sems(k), sibling)
                         for k, (px, py) in enumerate(others)]
                plan.append((mine, sends, [], sends))
        elif kind == "swap":
            _, src, dst, layer = job
            cp = _remote(refs[src].at[layer, :, 1 - c], refs[dst], *sems(0), sibling)
            plan.append((True, [cp], [cp], []))
        elif kind == "xchg":
            _, src, dst = job
            sends = [_remote(refs[src].at[2 * px + py], refs[dst].at[me], *sems(k), (px, py, c))
                     for k, (px, py) in enumerate(others)]
            arrivals = [_remote(refs[src].at[me], refs[dst].at[2 * px + py], *sems(k), (px, py, c))
                        for k, (px, py) in enumerate(others)]
            plan.append((True, sends, arrivals, []))
        else:
            _, name, layer = job
            ref = refs[name]
            cp = _remote(ref.at[layer, c], ref.at[layer, c], *sems(0), sibling)
            arrival = _remote(ref.at[layer, c], ref.at[layer, 1 - c], *sems(0), sibling)
            plan.append((True, [cp], [arrival], []))
    return plan


def _hosted_start(plan, now):
    for mine, sends, _, _ in plan:
        @pl.when(now & mine)
        def _(sends=sends):
            for cp in sends:
                cp.start()


def _hosted_finish(plan, now):
    for mine, sends, arrivals, sibling_arrivals in plan:
        @pl.when(now & mine)
        def _(sends=sends, arrivals=arrivals):
            for cp in arrivals:
                cp.wait_recv()
            for cp in sends:
                cp.wait_send()

        if sibling_arrivals:
            @pl.when(now & jnp.logical_not(mine))
            def _(sibling_arrivals=sibling_arrivals):
                for cp in sibling_arrivals:
                    cp.wait_recv()


def _pcall(body, hosted, *, name, grid, in_specs, out_specs, out_shape, semantics, scratch_shapes=(), aliases=None):
    aliases = dict(aliases or {})
    if hosted is None or not hosted.jobs:
        return pl.pallas_call(body, name=name, grid=grid, in_specs=in_specs, out_specs=out_specs,
                              out_shape=out_shape, scratch_shapes=list(scratch_shapes),
                              input_output_aliases=aliases, compiler_params=_params(*semantics))
    single = not isinstance(out_shape, (list, tuple))
    out_specs_l = [out_specs] if single else list(out_specs)
    out_shape_l = [out_shape] if single else list(out_shape)
    n_in, n_out, n_buf, n_job = len(in_specs), len(out_specs_l), len(hosted.names), len(hosted.jobs)

    def carrying(*refs):
        ins, outs = refs[:n_in], refs[n_in + n_buf:n_in + n_buf + n_out]
        bufs = refs[n_in + n_buf + n_out:n_in + 2 * n_buf + n_out]
        rest = refs[n_in + 2 * n_buf + n_out:]
        scratch, send_sems, recv_sems = rest[:-2], rest[-2], rest[-1]
        first, last = True, True
        for axis, size in enumerate(grid):
            first = first & (pl.program_id(axis) == 0)
            last = last & (pl.program_id(axis) == size - 1)
        plan = _hosted_plan(hosted, dict(zip(hosted.names, bufs)), send_sems, recv_sems)
        _hosted_start(plan, first)
        body(*ins, *outs, *scratch)
        _hosted_finish(plan, last)

    def run(*args):
        bufs = [hosted.pool[n] for n in hosted.names]
        sem = pltpu.SemaphoreType.DMA
        res = pl.pallas_call(
            carrying, name=name, grid=grid, in_specs=list(in_specs) + [_ANY] * n_buf,
            out_specs=out_specs_l + [_ANY] * n_buf,
            out_shape=out_shape_l + [SDS(b.shape, b.dtype) for b in bufs],
            scratch_shapes=list(scratch_shapes) + [sem((n_job, 3)), sem((n_job, 3))],
            input_output_aliases={**aliases, **{n_in + i: n_out + i for i in range(n_buf)}},
            compiler_params=pltpu.CompilerParams(dimension_semantics=semantics, vmem_limit_bytes=VMEM_LIMIT,
                                                 has_side_effects=True),
        )(*args, *bufs)
        hosted.pool.update(zip(hosted.names, res[n_out:]))
        return res[0] if single else res[:n_out]

    return run


def _dot(a, b):
    return jnp.dot(a, b, preferred_element_type=F32)


def _dot_nt(a, b):
    return lax.dot_general(a, b, NT, preferred_element_type=F32)


def _dot_tn(a, b):
    return lax.dot_general(a, b, TN, preferred_element_type=F32)


def norm_matmul(x, gain, w, layer, kind, name, hosted=None):
    T, D = x.shape
    if kind == "by_shard":
        tn = w.shape[3]
        N = N_CHIPS * tn
        w_spec = pl.BlockSpec((None, None, D, tn), lambda i, j: (j, layer, 0, 0))
        mm = _dot
    else:
        N = w.shape[0]
        tn = _tile(N, (1024, 512, 256, 128))
        w_spec = pl.BlockSpec((tn, D), lambda i, j: (j, 0))
        mm = _dot_nt
    tm = _tile(T, (1024, 512, 256, 128))

    def body(x_ref, g_ref, w_ref, y_ref, h_ref):
        @pl.when(pl.program_id(1) == 0)
        def _():
            xf = x_ref[...]
            r = lax.rsqrt(jnp.mean(xf * xf, axis=-1, keepdims=True) + RMS_EPS)
            h_ref[...] = ((xf * r) * g_ref[...]).astype(BF16)

        y_ref[...] = mm(h_ref[...], w_ref[...]).astype(BF16)

    return _pcall(
        body, hosted, name=name, grid=(T // tm, N // tn),
        in_specs=[pl.BlockSpec((tm, D), lambda i, j: (i, 0)),
                  pl.BlockSpec((None, 1, D), lambda i, j: (layer, 0, 0)),
                  w_spec],
        out_specs=[pl.BlockSpec((tm, tn), lambda i, j: (i, j)),
                   pl.BlockSpec((tm, D), lambda i, j: (i, 0))],
        out_shape=[SDS((T, N), BF16), SDS((T, D), BF16)],
        semantics=("arbitrary", "arbitrary"),
    )(x, gain, w)


def matmul_nt_normbwd(dys, w, layer, kind, x, gain, dres, name, hosted=None):
    T, D = x.shape
    width = dys[0].shape[1]
    if kind == "by_shard":
        tk = w.shape[3]
        w_spec = pl.BlockSpec((None, None, D, tk), lambda i, k: (k, layer, 0, 0))
        mm = _dot_nt
    else:
        tk = _tile(width, (3584, 1024, 512, 256, 128))
        w_spec = pl.BlockSpec((tk, D), lambda i, k: (k, 0))
        mm = _dot
    per = width // tk
    nk = per * len(dys)
    tm = _tile(T, (512, 256, 128))
    n_dy = len(dys)

    def dy_spec(p):
        return pl.BlockSpec((tm, tk), lambda i, k: (i, jnp.clip(k - p * per, 0, per - 1)))

    def body(*refs):
        dy_refs = refs[:n_dy]
        w_ref, x_ref, g_ref, dres_ref, dx_ref, dxb_ref, dg_ref, acc_ref = refs[n_dy:]
        i, k = pl.program_id(0), pl.program_id(1)

        @pl.when(k == 0)
        def _():
            acc_ref[...] = jnp.zeros_like(acc_ref)

        for p in range(n_dy):
            @pl.when((k >= p * per) & (k < (p + 1) * per))
            def _(p=p):
                acc_ref[...] += mm(dy_refs[p][...], w_ref[...])

        @pl.when(k == nk - 1)
        def _():
            xf = x_ref[...]
            r = lax.rsqrt(jnp.mean(xf * xf, axis=-1, keepdims=True) + RMS_EPS)
            xhat = xf * r
            dh = acc_ref[...]
            dhg = dh * g_ref[...]
            dx = dres_ref[...] + r * (dhg - xhat * jnp.mean(dhg * xhat, axis=-1, keepdims=True))
            dx_ref[...] = dx
            dxb_ref[...] = dx.astype(BF16)
            part = jnp.sum(dh * xhat, axis=0, keepdims=True)

            @pl.when(i == 0)
            def _():
                dg_ref[...] = part

            @pl.when(i > 0)
            def _():
                dg_ref[...] += part

    row = pl.BlockSpec((tm, D), lambda i, k: (i, 0))
    return _pcall(
        body, hosted, name=name, grid=(T // tm, nk),
        in_specs=[dy_spec(p) for p in range(n_dy)] + [
            w_spec, row, pl.BlockSpec((None, 1, D), lambda i, k: (layer, 0, 0)), row],
        out_specs=[row, row, pl.BlockSpec((1, D), lambda i, k: (0, 0))],
        out_shape=[SDS((T, D), F32), SDS((T, D), BF16), SDS((1, D), F32)],
        scratch_shapes=[pltpu.VMEM((tm, D), F32)],
        semantics=("arbitrary", "arbitrary"),
    )(*dys, w, x, gain, dres)


def matmul_tn(a, bs, name, b_col0=0, n_cols=None, by_dest=False, tn=None, tk=None, hosted=None):
    T, M = a.shape
    width = bs[0].shape[1]
    N = n_cols if n_cols else width * len(bs)
    tm = _tile(M, (1024, 512, 256, 128))
    tn = tn or _tile(N, (512, 256, 128))
    tk = tk or _tile(T, (4096, 2048, 1024, 512, 256))
    assert b_col0 % tn == 0 and width % tn == 0
    j0, per, nk, n_b = b_col0 // tn, width // tn, T // tk, len(bs)

    def b_spec(p):
        return pl.BlockSpec((tk, tn), lambda i, j, k: (k, jnp.clip(j0 + j - p * per, 0, per - 1)))

    def body(*refs):
        a_ref, b_refs = refs[0], refs[1:1 + n_b]
        o_ref, acc_ref = refs[-2], refs[-1]
        j, k = pl.program_id(1), pl.program_id(2)

        @pl.when(k == 0)
        def _():
            acc_ref[...] = jnp.zeros_like(acc_ref)

        for p in range(n_b):
            @pl.when((j0 + j >= p * per) & (j0 + j < (p + 1) * per))
            def _(p=p):
                acc_ref[...] += _dot_tn(a_ref[...], b_refs[p][...])

        @pl.when(k == nk - 1)
        def _():
            o_ref[...] = acc_ref[...].astype(BF16)

    if by_dest:
        cs = N // N_CHIPS
        npd = cs // tn
        out_shape = SDS((N_CHIPS, M, cs), BF16)
        out_spec = pl.BlockSpec((None, tm, tn), lambda i, j, k: (j // npd, i, j % npd))
    else:
        out_shape = SDS((M, N), BF16)
        out_spec = pl.BlockSpec((tm, tn), lambda i, j, k: (i, j))
    return _pcall(
        body, hosted, name=name, grid=(M // tm, N // tn, nk),
        in_specs=[pl.BlockSpec((tk, tm), lambda i, j, k: (k, i))] + [b_spec(p) for p in range(n_b)],
        out_specs=out_spec, out_shape=out_shape,
        scratch_shapes=[pltpu.VMEM((tm, tn), F32)],
        semantics=("arbitrary", "arbitrary", "arbitrary"),
    )(a, *bs)


def ffn_down_fwd(ab, w_down, layer, x1, hosted=None):
    T, D = x1.shape
    F = w_down.shape[1]
    tm = _tile(T, (512, 256, 128))
    tk = F // 2
    nk = F // tk

    def body(a_ref, b_ref, w_ref, x_ref, x2_ref, s_ref, acc_ref):
        k = pl.program_id(1)

        @pl.when(k == 0)
        def _():
            acc_ref[...] = x_ref[...]

        a = a_ref[...].astype(F32)
        s = (a * _sigmoid(a) * b_ref[...].astype(F32)).astype(BF16)
        s_ref[...] = s
        acc_ref[...] += _dot(s, w_ref[...])

        @pl.when(k == nk - 1)
        def _():
            x2_ref[...] = acc_ref[...]

    return _pcall(
        body, hosted, name="ffn_down_fwd", grid=(T // tm, nk),
        in_specs=[pl.BlockSpec((tm, tk), lambda i, k: (i, k)),
                  pl.BlockSpec((tm, tk), lambda i, k: (i, nk + k)),
                  pl.BlockSpec((None, tk, D), lambda i, k: (layer, k, 0)),
                  pl.BlockSpec((tm, D), lambda i, k: (i, 0))],
        out_specs=[pl.BlockSpec((tm, D), lambda i, k: (i, 0)),
                   pl.BlockSpec((tm, tk), lambda i, k: (i, k))],
        out_shape=[SDS((T, D), F32), SDS((T, F), BF16)],
        scratch_shapes=[pltpu.VMEM((tm, D), F32)],
        semantics=("arbitrary", "arbitrary"),
    )(ab, ab, w_down, x1)


def ffn_down_bwd(dx2b, w_down, layer, ab, hosted=None):
    T, D = dx2b.shape
    F = w_down.shape[1]
    tm = _tile(T, (512, 256, 128))
    tn = F // 2
    nj = F // tn

    def body(dx_ref, w_ref, a_ref, b_ref, da_ref, db_ref):
        ds = _dot_nt(dx_ref[...], w_ref[...])
        a = a_ref[...].astype(F32)
        sg = _sigmoid(a)
        da_ref[...] = (ds * b_ref[...].astype(F32) * (sg * (1.0 + a * (1.0 - sg)))).astype(BF16)
        db_ref[...] = (ds * (a * sg)).astype(BF16)

    blk = pl.BlockSpec((tm, tn), lambda i, j: (i, j))
    return _pcall(
        body, hosted, name="ffn_down_bwd", grid=(T // tm, nj),
        in_specs=[pl.BlockSpec((tm, D), lambda i, j: (i, 0)),
                  pl.BlockSpec((None, tn, D), lambda i, j: (layer, j, 0)),
                  blk, pl.BlockSpec((tm, tn), lambda i, j: (i, nj + j))],
        out_specs=[blk, blk],
        out_shape=[SDS((T, F), BF16), SDS((T, F), BF16)],
        semantics=("arbitrary", "arbitrary"),
    )(dx2b, w_down, ab, ab)


def _mix_specs(tm, D, layer):
    cs = D // N_CHIPS
    row = lambda w: pl.BlockSpec((tm, w), lambda i: (i, 0))
    wp = pl.BlockSpec((N_CHIPS, None, BRANCH_W, cs), lambda i: (0, layer, 0, 0))
    wo = pl.BlockSpec((None, N_CHIPS, cs, D), lambda i: (layer, 0, 0, 0))
    bg = pl.BlockSpec((None, 1, 3 * D), lambda i: (layer, 0, 0))
    return row, wp, wo, bg


def mix_fwd(ao, po, co, proj, b_gate, wpa, wpp, wpc, w_out, layer, x, hosted=None):
    T, D = x.shape
    cs = D // N_CHIPS
    tm = _tile(T, (256, 128))
    row, wp, wo, bg = _mix_specs(tm, D, layer)

    def body(ao_ref, po_ref, co_ref, g_ref, bg_ref, wpa_ref, wpp_ref, wpc_ref, wo_ref, x_ref,
             x1_ref, ys_ref, mixed_ref):
        mixed = jnp.zeros((tm, D), F32)
        for n, (br, wp_ref) in enumerate(((ao_ref, wpa_ref), (po_ref, wpp_ref), (co_ref, wpc_ref))):
            y = jnp.concatenate([_dot(br[...], wp_ref[j]) for j in range(N_CHIPS)], axis=1)
            cols = slice(n * D, (n + 1) * D)
            gate = _sigmoid(g_ref[:, cols].astype(F32) + bg_ref[:, cols])
            ys_ref[:, cols] = y.astype(BF16)
            mixed = mixed + gate * y
        mb = mixed.astype(BF16)
        mixed_ref[...] = mb
        acc = x_ref[...]
        for j in range(N_CHIPS):
            acc = acc + _dot(mb[:, j * cs:(j + 1) * cs], wo_ref[j])
        x1_ref[...] = acc

    return _pcall(
        body, hosted, name="mix_fwd", grid=(T // tm,),
        in_specs=[row(BRANCH_W), row(BRANCH_W), row(BRANCH_W), row(3 * D), bg, wp, wp, wp, wo, row(D)],
        out_specs=[row(D), row(3 * D), row(D)],
        out_shape=[SDS((T, D), F32), SDS((T, 3 * D), BF16), SDS((T, D), BF16)],
        semantics=("arbitrary",),
    )(ao, po, co, proj, b_gate, wpa, wpp, wpc, w_out, x)


def mix_bwd(dx1b, w_out, proj, b_gate, ys, wpa, wpp, wpc, layer, width, hosted=None):
    T, D = dx1b.shape
    cs = D // N_CHIPS
    tm = _tile(T, (256, 128))
    row, wp, wo, bg = _mix_specs(tm, D, layer)

    def body(dx_ref, wo_ref, g_ref, bg_ref, ys_ref, wpa_ref, wpp_ref, wpc_ref,
             dys_ref, dg_ref, dao_ref, dpo_ref, dco_ref, dbg_ref):
        i = pl.program_id(0)
        dx = dx_ref[...]
        dmixed = jnp.concatenate([_dot_nt(dx, wo_ref[j]) for j in range(N_CHIPS)], axis=1)
        for n, (wp_ref, dbr) in enumerate(((wpa_ref, dao_ref), (wpp_ref, dpo_ref), (wpc_ref, dco_ref))):
            cols = slice(n * D, (n + 1) * D)
            gate = _sigmoid(g_ref[:, cols].astype(F32) + bg_ref[:, cols])
            dy = (dmixed * gate).astype(BF16)
            dys_ref[:, cols] = dy
            dgp = dmixed * ys_ref[:, cols].astype(F32) * gate * (1.0 - gate)
            dg_ref[:, cols] = dgp.astype(BF16)
            part = jnp.sum(dgp, axis=0, keepdims=True)

            @pl.when(i == 0)
            def _():
                dbg_ref[:, cols] = part

            @pl.when(i > 0)
            def _():
                dbg_ref[:, cols] += part

            acc = jnp.zeros((tm, BRANCH_W), F32)
            for j in range(N_CHIPS):
                acc = acc + _dot_nt(dy[:, j * cs:(j + 1) * cs], wp_ref[j])
            dbr[...] = acc.astype(BF16)

    return _pcall(
        body, hosted, name="mix_bwd", grid=(T // tm,),
        in_specs=[row(D), wo, row(3 * D), bg, row(3 * D), wp, wp, wp],
        out_specs=[row(3 * D), row(3 * D), row(BRANCH_W), row(BRANCH_W), row(BRANCH_W),
                   pl.BlockSpec((1, 3 * D), lambda i: (0, 0))],
        out_shape=[SDS((T, 3 * D), BF16), SDS((T, width), BF16), SDS((T, BRANCH_W), BF16),
                   SDS((T, BRANCH_W), BF16), SDS((T, BRANCH_W), BF16), SDS((1, 3 * D), F32)],
        semantics=("arbitrary",),
    )(dx1b, w_out, proj, b_gate, ys, wpa, wpp, wpc)


def loss_head(x2, gain, target):
    T, D = x2.shape
    tm = _tile(T, (512, 256, 128))

    def body(x_ref, g_ref, t_ref, loss_ref, dx_ref, dxb_ref, dg_ref):
        i = pl.program_id(0)
        xf = x_ref[...]
        g = g_ref[...]
        r = lax.rsqrt(jnp.mean(xf * xf, axis=-1, keepdims=True) + RMS_EPS)
        xhat = xf * r
        diff = xhat * g - t_ref[...]
        part_loss = 0.5 * jnp.sum(jnp.mean(diff * diff, axis=-1, keepdims=True), axis=0, keepdims=True)
        dy = diff * (1.0 / D)
        dhg = dy * g
        dx = r * (dhg - xhat * jnp.mean(dhg * xhat, axis=-1, keepdims=True))
        dx_ref[...] = dx
        dxb_ref[...] = dx.astype(BF16)
        part_g = jnp.sum(dy * xhat, axis=0, keepdims=True)
        part_l = jnp.broadcast_to(part_loss, (1, LANES))

        @pl.when(i == 0)
        def _():
            dg_ref[...] = part_g
            loss_ref[...] = part_l

        @pl.when(i > 0)
        def _():
            dg_ref[...] += part_g
            loss_ref[...] += part_l

    row = pl.BlockSpec((tm, D), lambda i: (i, 0))
    return pl.pallas_call(
        body, name="loss_head", grid=(T // tm,),
        in_specs=[row, pl.BlockSpec((1, D), lambda i: (0, 0)), row],
        out_specs=[pl.BlockSpec((1, LANES), lambda i: (0, 0)), row, row, pl.BlockSpec((1, D), lambda i: (0, 0))],
        out_shape=[SDS((1, LANES), F32), SDS((T, D), F32), SDS((T, D), BF16), SDS((1, D), F32)],
        compiler_params=_params("arbitrary"),
    )(x2, gain, target)


def _placement_constants():
    w = HEADS * HEAD_PAD
    pq = np.zeros((BRANCH_W, w), np.float32)
    pk = np.zeros((BRANCH_W, w), np.float32)
    pfq = np.zeros((3, LANES, w), np.float32)
    pfk = np.zeros((3, LANES, w), np.float32)
    cq = np.zeros((1, w), np.float32)
    ck = np.zeros((1, w), np.float32)
    eq = np.zeros((w, LANES), np.float32)
    ek = np.zeros((w, LANES), np.float32)
    for h in range(HEADS):
        for d in range(HEAD_DIM):
            pq[h * HEAD_DIM + d, h * HEAD_PAD + d] = HEAD_DIM ** -0.5
            pk[h * HEAD_DIM + d, h * HEAD_PAD + d] = 1.0
        for i in range(3):
            pfq[i, h, h * HEAD_PAD + HEAD_DIM + i] = 1.0
            pfk[i, h, h * HEAD_PAD + HEAD_DIM + 3 + i] = -1.0
            cq[0, h * HEAD_PAD + HEAD_DIM + 3 + i] = 1.0
            ck[0, h * HEAD_PAD + HEAD_DIM + i] = 1.0
        eq[h * HEAD_PAD + HEAD_DIM, h] = 1.0
        ek[h * HEAD_PAD + HEAD_DIM + 3, h] = -1.0
    bf = lambda a: jnp.asarray(a, BF16)
    return dict(pq=bf(pq), pk=bf(pk), pfq=bf(pfq), pfk=bf(pfk), cq=jnp.asarray(cq), ck=jnp.asarray(ck),
                pqkt=bf(np.concatenate([pq.T, pk.T], axis=0)), eq=bf(eq), ek=bf(ek))


def attn_prep(proj3, bf_rows, layer, cst, lay, hosted=None):
    Bl, S, _ = proj3.shape
    ts = ATTN_BLOCK
    w = HEADS * HEAD_PAD

    def body(q_ref, k_ref, f_ref, bf_ref, pq_ref, pk_ref, pfq_ref, pfk_ref, cq_ref, ck_ref,
             qa_ref, ka_ref, carry_ref):
        @pl.when(pl.program_id(1) == 0)
        def _():
            carry_ref[...] = jnp.zeros_like(carry_ref)

        z = f_ref[...].astype(F32) + bf_ref[...]
        logf = jnp.minimum(z, 0.0) - jnp.log(1.0 + jnp.exp(-jnp.abs(z)))
        r = lax.broadcasted_iota(jnp.int32, (ts, ts), 0)
        c = lax.broadcasted_iota(jnp.int32, (ts, ts), 1)
        tri = jnp.where(r >= c, 1.0, 0.0).astype(BF16)
        fcum = carry_ref[...]
        for part in _split3(logf):
            fcum = fcum + _dot(tri, part)
        carry_ref[...] = fcum[ts - 1:ts, :]
        qa = _dot(q_ref[...], pq_ref[...]) + cq_ref[...]
        ka = _dot(k_ref[...], pk_ref[...]) + ck_ref[...]
        for i, part in enumerate(_split3(fcum)):
            qa = qa + _dot(part, pfq_ref[i])
            ka = ka + _dot(part, pfk_ref[i])
        qa_ref[...] = qa.astype(BF16)
        ka_ref[...] = ka.astype(BF16)

    cfull = lambda shape: pl.BlockSpec(shape, lambda b, s: (0,) * len(shape))
    return _pcall(
        body, hosted, name="attn_prep", grid=(Bl, S // ts),
        in_specs=[pl.BlockSpec((None, ts, BRANCH_W), lambda b, s: (b, s, lay["q"] // BRANCH_W)),
                  pl.BlockSpec((None, ts, BRANCH_W), lambda b, s: (b, s, lay["k"] // BRANCH_W)),
                  pl.BlockSpec((None, ts, LANES), lambda b, s: (b, s, lay["f"] // LANES)),
                  pl.BlockSpec((None, 1, LANES), lambda b, s: (layer, 0, 0)),
                  cfull((BRANCH_W, w)), cfull((BRANCH_W, w)),
                  cfull((3, LANES, w)), cfull((3, LANES, w)), cfull((1, w)), cfull((1, w))],
        out_specs=[pl.BlockSpec((None, ts, w), lambda b, s: (b, s, 0)),
                   pl.BlockSpec((None, ts, w), lambda b, s: (b, s, 0))],
        out_shape=[SDS((Bl, S, w), BF16), SDS((Bl, S, w), BF16)],
        scratch_shapes=[pltpu.VMEM((1, LANES), F32)],
        semantics=("arbitrary", "arbitrary"),
    )(proj3, proj3, proj3, bf_rows, cst["pq"], cst["pk"], cst["pfq"], cst["pfk"], cst["cq"], cst["ck"])


def attn_fwd(qa, ka, proj3, lay, hosted=None):
    Bl, S, _ = qa.shape
    tq = ATTN_BLOCK
    nq = S // tq
    pairs = HEADS // 2
    pw = 2 * HEAD_PAD
    vw = 2 * HEAD_DIM

    def body(qa_ref, ka_ref, v_ref, o_ref, lse_ref):
        row = lax.broadcasted_iota(jnp.int32, (tq, tq), 0)
        col = lax.broadcasted_iota(jnp.int32, (tq, tq), 1)
        causal = row <= col
        for i in range(nq):
            nk = (i + 1) * tq
            rows = slice(i * tq, nk)
            o_t = []
            for h in range(2):
                hs = slice(h * HEAD_PAD, (h + 1) * HEAD_PAD)
                st = _dot_nt(ka_ref[0:nk, hs], qa_ref[rows, hs])
                diag = jnp.where(causal, st[nk - tq:], NEG_INF)
                m = jnp.max(diag, axis=0, keepdims=True)
                if i:
                    m = jnp.maximum(m, jnp.max(st[:nk - tq], axis=0, keepdims=True))
                p_diag = jnp.exp(diag - m)
                l = jnp.sum(p_diag, axis=0, keepdims=True)
                if i:
                    p_top = jnp.exp(st[:nk - tq] - m)
                    l = l + jnp.sum(p_top, axis=0, keepdims=True)
                    p = jnp.concatenate([p_top.astype(BF16), p_diag.astype(BF16)], axis=0)
                else:
                    p = p_diag.astype(BF16)
                acc = _dot_tn(v_ref[0:nk, :], p)
                o_t.append(acc[h * HEAD_DIM:(h + 1) * HEAD_DIM, :] / l)
                lse_ref[h:h + 1, rows] = m + jnp.log(l)
            o_ref[rows, :] = jnp.concatenate(o_t, axis=0).T.astype(BF16)

    return _pcall(
        body, hosted, name="attn_fwd", grid=(Bl, pairs),
        in_specs=[pl.BlockSpec((None, S, pw), lambda b, p: (b, 0, p)),
                  pl.BlockSpec((None, S, pw), lambda b, p: (b, 0, p)),
                  pl.BlockSpec((None, S, vw), lambda b, p: (b, 0, lay["v"] // vw + p))],
        out_specs=[pl.BlockSpec((None, S, vw), lambda b, p: (b, 0, p)),
                   pl.BlockSpec((None, None, 2, S), lambda b, p: (b, p, 0, 0))],
        out_shape=[SDS((Bl, S, BRANCH_W), BF16), SDS((Bl, pairs, 2, S), F32)],
        semantics=("arbitrary", "arbitrary"),
    )(qa, ka, proj3)


def attn_bwd(qa, ka, proj3, dao, ao, lse, dproj3, lay, hosted=None):
    Bl, S, _ = qa.shape
    tk = ATTN_BLOCK
    nq = S // tk
    pairs = HEADS // 2
    pw = 2 * HEAD_PAD
    vw = 2 * HEAD_DIM

    def body(qa_ref, ka_ref, v_ref, do_ref, o_ref, lse_ref, _, dqa_ref, dka_ref, dv_ref):
        row = lax.broadcasted_iota(jnp.int32, (tk, tk), 0)
        col = lax.broadcasted_iota(jnp.int32, (tk, tk), 1)
        causal = row <= col
        lane8 = lax.broadcasted_iota(jnp.int32, (8, vw), 1)
        lane_s = lax.broadcasted_iota(jnp.int32, (S, vw), 1)
        lane_k = lax.broadcasted_iota(jnp.int32, (tk, vw), 1)
        doo = do_ref[...].astype(F32) * o_ref[...].astype(F32)
        hi = doo.astype(BF16)
        lo = (doo - hi.astype(F32)).astype(BF16)
        delta, v_head = [], []
        for h in range(2):
            sel = jnp.where((lane8 >= h * HEAD_DIM) & (lane8 < (h + 1) * HEAD_DIM), 1.0, 0.0).astype(BF16)
            delta.append((_dot_nt(sel, hi) + _dot_nt(sel, lo))[0:1, :])
            in_head = (lane_s >= h * HEAD_DIM) & (lane_s < (h + 1) * HEAD_DIM)
            v_head.append(jnp.where(in_head, v_ref[...], jnp.zeros_like(v_ref[...])))
        dqa_ref[...] = jnp.zeros_like(dqa_ref)
        for j in range(nq):
            q0 = j * tk
            krows = slice(q0, q0 + tk)
            do = do_ref[q0:, :]
            dvs = []
            for h in range(2):
                hs = slice(h * HEAD_PAD, (h + 1) * HEAD_PAD)
                k = ka_ref[krows, hs]
                q = qa_ref[q0:, hs]
                st = _dot_nt(k, q)
                p = jnp.exp(st - lse_ref[h:h + 1, q0:])
                p_diag = jnp.where(causal, p[:, :tk], 0.0)
                p = jnp.concatenate([p_diag, p[:, tk:]], axis=1) if j < nq - 1 else p_diag
                dvs.append(_dot(p.astype(BF16), do))
                dpt = _dot_nt(v_head[h][krows, :], do)
                ds = (p * (dpt - delta[h][:, q0:])).astype(BF16)
                dka_ref[krows, hs] = _dot(ds, q)
                dqa_ref[q0:, hs] += _dot_tn(ds, k)
            dv_ref[krows, :] = jnp.where(lane_k < HEAD_DIM, dvs[0], dvs[1]).astype(BF16)

    seq = lambda w, c0=0: pl.BlockSpec((None, S, w), lambda b, p: (b, 0, c0 + p))
    return _pcall(
        body, hosted, name="attn_bwd", grid=(Bl, pairs),
        in_specs=[seq(pw), seq(pw), seq(vw, lay["v"] // vw), seq(vw), seq(vw),
                  pl.BlockSpec((None, None, 2, S), lambda b, p: (b, p, 0, 0)), _ANY],
        out_specs=[seq(pw), seq(pw), seq(vw, lay["v"] // vw)],
        out_shape=[SDS((Bl, S, HEADS * HEAD_PAD), F32), SDS((Bl, S, HEADS * HEAD_PAD), F32),
                   SDS(dproj3.shape, BF16)],
        aliases={6: 2}, semantics=("arbitrary", "arbitrary"),
    )(qa, ka, proj3, dao, ao, lse, dproj3)


def attn_post(dqa, dka, proj3, bf_rows, layer, dproj3, cst, lay, hosted=None):
    Bl, S, w = dqa.shape
    ts = ATTN_BLOCK
    ns = S // ts
    qkf = 2 * BRANCH_W + F_PAD

    def body(dqa_ref, dka_ref, f_ref, bf_ref, pqkt_ref, eq_ref, ek_ref, _, dqkf_ref, dbf_ref, carry_ref):
        b, s = pl.program_id(0), pl.program_id(1)

        @pl.when(s == 0)
        def _():
            carry_ref[...] = jnp.zeros_like(carry_ref)

        dqa_v, dka_v = dqa_ref[...], dka_ref[...]
        qh = dqa_v.astype(BF16)
        kh = dka_v.astype(BF16)
        dqkf_ref[:, :BRANCH_W] = _dot(qh, pqkt_ref[:w, :]).astype(BF16)
        dqkf_ref[:, BRANCH_W:2 * BRANCH_W] = _dot(kh, pqkt_ref[w:, :]).astype(BF16)
        ql = (dqa_v - qh.astype(F32)).astype(BF16)
        kl = (dka_v - kh.astype(F32)).astype(BF16)
        d_f = (_dot(qh, eq_ref[...]) + _dot(ql, eq_ref[...])) + (_dot(kh, ek_ref[...]) + _dot(kl, ek_ref[...]))
        r = lax.broadcasted_iota(jnp.int32, (ts, ts), 0)
        c = lax.broadcasted_iota(jnp.int32, (ts, ts), 1)
        triu = jnp.where(c >= r, 1.0, 0.0).astype(BF16)
        rev = carry_ref[...]
        for part in _split3(d_f):
            rev = rev + _dot(triu, part)
        carry_ref[...] = rev[0:1, :]
        z = f_ref[...].astype(F32) + bf_ref[...]
        lane = lax.broadcasted_iota(jnp.int32, (ts, LANES), 1)
        dfl = jnp.where(lane < HEADS, rev / (1.0 + jnp.exp(z)), 0.0)
        dqkf_ref[:, 2 * BRANCH_W:] = jnp.concatenate(
            [dfl.astype(BF16), jnp.zeros((ts, F_PAD - LANES), BF16)], axis=1)
        part = jnp.sum(dfl, axis=0, keepdims=True)

        @pl.when((b == 0) & (s == 0))
        def _():
            dbf_ref[...] = part

        @pl.when((b > 0) | (s > 0))
        def _():
            dbf_ref[...] += part

    assert lay["q"] % qkf == 0
    cfull = lambda shape: pl.BlockSpec(shape, lambda b, s: (0,) * len(shape))
    rev_blk = lambda wd, c0=0: pl.BlockSpec((None, ts, wd), lambda b, s: (b, ns - 1 - s, c0))
    return _pcall(
        body, hosted, name="attn_post", grid=(Bl, ns),
        in_specs=[rev_blk(w), rev_blk(w), rev_blk(LANES, lay["f"] // LANES),
                  pl.BlockSpec((None, 1, LANES), lambda b, s: (layer, 0, 0)),
                  cfull((2 * w, BRANCH_W)), cfull((w, LANES)), cfull((w, LANES)), _ANY],
        out_specs=[rev_blk(qkf, lay["q"] // qkf), cfull((1, LANES))],
        out_shape=[SDS(dproj3.shape, BF16), SDS((1, LANES), F32)],
        scratch_shapes=[pltpu.VMEM((1, LANES), F32)],
        aliases={7: 0}, semantics=("arbitrary", "arbitrary"),
    )(dqa, dka, proj3, bf_rows, cst["pqkt"], cst["eq"], cst["ek"], dproj3)


def _shift_down(x, k, row):
    return jnp.where(row >= k, pltpu.roll(x, k, axis=0), 0.0)


def _shift_up(x, k, row):
    n = x.shape[0]
    return jnp.where(row < n - k, pltpu.roll(x, n - k, axis=0), 0.0)


def _window_sum(x, g, row, shift):
    s2 = x + shift(x, 1, row)
    s4 = s2 + shift(s2, 2, row)
    s8 = s4 + shift(s4, 4, row)
    s16 = s8 + shift(s8, 8, row)
    return jnp.where(g == 0, s2, jnp.where(g == 1, s4, jnp.where(g == 2, s8, s16)))


def _window_count(g, row):
    wnd = jnp.where(g == 0, 2, jnp.where(g == 1, 4, jnp.where(g == 2, 8, 16)))
    return jnp.minimum(row + 1, wnd).astype(F32)


def _group_columns(ref):
    return [ref[:, n * GROUP_W:(n + 1) * GROUP_W].astype(F32) for n in range(4)]


def poolconv_fwd(proj3, pool_w, pool_scale, conv_w, layer, lay, hosted=None):
    Bl, S, _ = proj3.shape

    def body(x_ref, pw_ref, ps_ref, cw_ref, po_ref, co_ref):
        g = pl.program_id(1)
        row = lax.broadcasted_iota(jnp.int32, (S, GROUP_W), 0)
        u, cv, cb, cc = _group_columns(x_ref)
        d = _window_sum(u, g, row, _shift_down) / _window_count(g, row) - u
        po_ref[...] = (_dot(d.astype(BF16), pw_ref[...]) * ps_ref[...]).astype(BF16)
        z = cc * cv
        y = cw_ref[0:1, :] * _shift_down(z, 2, row) + cw_ref[1:2, :] * _shift_down(z, 1, row) + cw_ref[2:3, :] * z
        co_ref[...] = (cb * y).astype(BF16)

    out = pl.BlockSpec((None, S, GROUP_W), lambda b, g: (b, 0, g))
    return _pcall(
        body, hosted, name="poolconv_fwd", grid=(Bl, N_GROUPS),
        in_specs=[pl.BlockSpec((None, S, BRANCH_W), lambda b, g: (b, 0, lay["pc"] // BRANCH_W + g)),
                  pl.BlockSpec((None, None, GROUP_W, GROUP_W), lambda b, g: (layer, g, 0, 0)),
                  pl.BlockSpec((None, 1, GROUP_W), lambda b, g: (layer, 0, g)),
                  pl.BlockSpec((None, None, 3, GROUP_W), lambda b, g: (g, layer, 0, 0))],
        out_specs=[out, out],
        out_shape=[SDS((Bl, S, BRANCH_W), BF16), SDS((Bl, S, BRANCH_W), BF16)],
        semantics=("arbitrary", "arbitrary"),
    )(proj3, pool_w, pool_scale, conv_w)


def poolconv_bwd(proj3, dpo, dco, pool_w, pool_scale, conv_w, layer, dproj3, lay, hosted=None):
    Bl, S, _ = proj3.shape

    def body(x_ref, dpo_ref, dco_ref, pw_ref, ps_ref, cw_ref, _, dx_ref, dpw_ref, dps_ref, dcw_ref):
        g, b = pl.program_id(0), pl.program_id(1)
        row = lax.broadcasted_iota(jnp.int32, (S, GROUP_W), 0)
        cnt = _window_count(g, row)
        u, cv, cb, cc = _group_columns(x_ref)
        d = (_window_sum(u, g, row, _shift_down) / cnt - u).astype(BF16)
        pw = pw_ref[...]
        ypre = _dot(d, pw)
        dpo_v = dpo_ref[...].astype(F32)
        dps = jnp.sum(dpo_v * ypre, axis=0, keepdims=True)
        dyp = (dpo_v * ps_ref[...]).astype(BF16)
        dpw = _dot_tn(d, dyp)
        dd = _dot_nt(dyp, pw)
        dx_ref[:, 0:GROUP_W] = (_window_sum(dd / cnt, g, row, _shift_up) - dd).astype(BF16)

        z = cc * cv
        z1, z2 = _shift_down(z, 1, row), _shift_down(z, 2, row)
        w0, w1, w2 = cw_ref[0:1, :], cw_ref[1:2, :], cw_ref[2:3, :]
        y = w0 * z2 + w1 * z1 + w2 * z
        dco_v = dco_ref[...].astype(F32)
        dy = dco_v * cb
        dz = w0 * _shift_up(dy, 2, row) + w1 * _shift_up(dy, 1, row) + w2 * dy
        dx_ref[:, GROUP_W:2 * GROUP_W] = (dz * cc).astype(BF16)
        dx_ref[:, 2 * GROUP_W:3 * GROUP_W] = (dco_v * y).astype(BF16)
        dx_ref[:, 3 * GROUP_W:] = (dz * cv).astype(BF16)
        dcw = jnp.concatenate([jnp.sum(dy * z2, axis=0, keepdims=True),
                               jnp.sum(dy * z1, axis=0, keepdims=True),
                               jnp.sum(dy * z, axis=0, keepdims=True)], axis=0)

        @pl.when(b == 0)
        def _():
            dpw_ref[...] = dpw
            dps_ref[...] = dps
            dcw_ref[...] = dcw

        @pl.when(b > 0)
        def _():
            dpw_ref[...] += dpw
            dps_ref[...] += dps
            dcw_ref[...] += dcw

    blk = pl.BlockSpec((None, S, GROUP_W), lambda g, b: (b, 0, g))
    pc = pl.BlockSpec((None, S, BRANCH_W), lambda g, b: (b, 0, lay["pc"] // BRANCH_W + g))
    return _pcall(
        body, hosted, name="poolconv_bwd", grid=(N_GROUPS, Bl),
        in_specs=[pc, blk, blk,
                  pl.BlockSpec((None, None, GROUP_W, GROUP_W), lambda g, b: (layer, g, 0, 0)),
                  pl.BlockSpec((None, 1, GROUP_W), lambda g, b: (layer, 0, g)),
                  pl.BlockSpec((None, None, 3, GROUP_W), lambda g, b: (g, layer, 0, 0)), _ANY],
        out_specs=[pc, pl.BlockSpec((None, GROUP_W, GROUP_W), lambda g, b: (g, 0, 0)),
                   pl.BlockSpec((1, GROUP_W), lambda g, b: (0, g)),
                   pl.BlockSpec((None, 3, GROUP_W), lambda g, b: (g, 0, 0))],
        out_shape=[SDS(dproj3.shape, BF16), SDS((N_GROUPS, GROUP_W, GROUP_W), F32), SDS((1, BRANCH_W), F32),
                   SDS((N_GROUPS, 3, GROUP_W), F32)],
        aliases={6: 0}, semantics=("arbitrary", "arbitrary"),
    )(proj3, dpo, dco, pool_w, pool_scale, conv_w, dproj3)


def _tile_2d(rows, cols, n_arrays):
    budget = VMEM_LIMIT // 2
    lanes = -(-cols // LANES) * LANES
    if rows % 8 == 0:
        for t in (2048, 1024, 512, 256, 128, 64, 32, 16, 8):
            if rows % t == 0 and 2 * n_arrays * t * lanes * 4 <= budget:
                return t, cols
    for t in (1024, 512, 256, 128):
        if cols % t == 0 and 2 * n_arrays * (rows + 8) * t * 4 <= budget:
            return rows, t
    return rows, cols


def add_pair(kept, layer, where, received, name):
    _, n, _, R, C = kept.shape
    tr, tc = _tile_2d(R, C, 3)

    def body(where_ref, a_ref, b_ref, o_ref):
        o_ref[...] = (a_ref[...].astype(F32) + b_ref[...].astype(F32)).astype(BF16)

    blk = pl.BlockSpec((None, tr, tc), lambda d, i, j, where_ref: (d, i, j))
    grid_spec = pltpu.PrefetchScalarGridSpec(
        num_scalar_prefetch=1, grid=(n, R // tr, C // tc),
        in_specs=[pl.BlockSpec((None, None, None, tr, tc),
                               lambda d, i, j, where_ref: (layer, d, where_ref[0], i, j)), blk],
        out_specs=blk)
    return pl.pallas_call(body, name=name, grid_spec=grid_spec, out_shape=SDS((n, R, C), BF16),
                          compiler_params=_params("arbitrary", "arbitrary", "arbitrary"))(where, kept, received)


def add_chips(arrived, own, layer, where, n_layers, prev, name):
    _, R, C = arrived.shape
    tr, tc = _tile_2d(R, C, 6)

    def body(where_ref, a0, a1, a2, a3, own_ref, *rest):
        o_ref = rest[-1]
        chip = where_ref[1]
        acc = None
        for j, a_ref in enumerate((a0, a1, a2, a3)):
            term = jnp.where(chip == j, own_ref[...], a_ref[...]).astype(F32)
            acc = term if acc is None else acc + term
        o_ref[...] = acc

    def slot(j):
        return pl.BlockSpec((None, tr, tc), lambda i, k, where_ref, j=j: (
            jnp.where(where_ref[1] == j, (j + 1) % N_CHIPS, j), i, k))

    in_specs = [slot(j) for j in range(N_CHIPS)] + [
        pl.BlockSpec((None, tr, tc), lambda i, k, where_ref: (where_ref[1], i, k))]
    args = [where, arrived, arrived, arrived, arrived, own]
    aliases = {}
    if prev is not None:
        in_specs.append(_ANY)
        args.append(prev)
        aliases = {len(args) - 1: 0}
    grid_spec = pltpu.PrefetchScalarGridSpec(
        num_scalar_prefetch=1, grid=(R // tr, C // tc), in_specs=in_specs,
        out_specs=pl.BlockSpec((None, None, tr, tc), lambda i, k, where_ref: (layer, where_ref[0], i, k)))
    return pl.pallas_call(body, name=name, grid_spec=grid_spec, out_shape=SDS((n_layers, 2, R, C), F32),
                          input_output_aliases=aliases,
                          compiler_params=_params("arbitrary", "arbitrary"))(*args)


def adamw(w, g, m, v, name):
    if w.ndim == 2:
        R, C = w.shape
        tr, _ = _tile_2d(R, C, 7)
        grid, blk = (R // tr,), pl.BlockSpec((tr, C), lambda i: (i, 0))
    else:
        N, r, C = w.shape
        tn = max(t for t in range(1, N + 1) if N % t == 0 and t * r * C * 4 <= 512 * 1024)
        grid, blk = (N // tn,), pl.BlockSpec((tn, r, C), lambda i: (i, 0, 0))

    def body(w_ref, g_ref, m_ref, v_ref, d_ref, nm_ref, nv_ref):
        gv = g_ref[...]
        m_new = ADAM_B1 * m_ref[...] + (1.0 - ADAM_B1) * gv
        v_new = ADAM_B2 * v_ref[...] + (1.0 - ADAM_B2) * (gv * gv)
        m_hat = m_new / (1.0 - ADAM_B1 ** ADAM_STEP)
        v_hat = v_new / (1.0 - ADAM_B2 ** ADAM_STEP)
        d_ref[...] = -ADAM_LR * (m_hat / (jnp.sqrt(v_hat) + ADAM_EPS) + ADAM_WD * w_ref[...])
        nm_ref[...] = m_new
        nv_ref[...] = v_new

    out = SDS(w.shape, F32)
    return pl.pallas_call(body, name=name, grid=grid, in_specs=[blk] * 4, out_specs=[blk] * 3,
                          out_shape=[out, out, out], compiler_params=_params("arbitrary"))(w, g, m, v)


_COMM = pltpu.CompilerParams(has_side_effects=True)


def gather_buffers(shards):
    me_chip = 2 * lax.axis_index("x") + lax.axis_index("y")
    pool = {}
    for name, sh in shards.items():
        L, r, c = sh.shape
        if name in ROW_SHARDED:
            pool[name] = lax.dynamic_update_slice(lax.empty((L, N_CHIPS, r, c), sh.dtype), sh[:, None],
                                                  (0, me_chip, 0, 0))
        else:
            pool[name] = lax.dynamic_update_slice(lax.empty((N_CHIPS, L, r, c), sh.dtype), sh[None],
                                                  (me_chip, 0, 0, 0))
    return pool


def comm_now(pool, stages, name):
    stages = [Hosted(pool, jobs) for jobs in stages]
    names = sorted({m for st in stages for m in st.names})
    n = len(names)

    def body(*refs):
        bufs = dict(zip(names, refs[n:2 * n]))
        sems = refs[2 * n:]
        for i, st in enumerate(stages):
            plan = _hosted_plan(st, bufs, sems[2 * i], sems[2 * i + 1])
            _hosted_start(plan, True)
            _hosted_finish(plan, True)

    sem = pltpu.SemaphoreType.DMA
    scratch = []
    for st in stages:
        scratch += [sem((len(st.jobs), 3)), sem((len(st.jobs), 3))]
    res = pl.pallas_call(
        body, name=name, in_specs=[_ANY] * n, out_specs=[_ANY] * n,
        out_shape=[SDS(pool[m].shape, pool[m].dtype) for m in names],
        scratch_shapes=scratch, input_output_aliases={t: t for t in range(n)},
        compiler_params=_COMM,
    )(*[pool[m] for m in names])
    pool.update(zip(names, res))


def gather_now(pool, units):
    comm_now(pool, [[("ici", name, layer) for name, layer in units],
                    [("fwd", name, layer) for name, layer in units]], "gather_now")


def allgather_chips(buf, name):
    def body(src_ref, out_ref, send_sems, recv_sems, local_sem):
        x, y, c = _position()
        me = 2 * x + y
        mine = pltpu.make_async_copy(src_ref, out_ref.at[me], local_sem)
        mine.start()
        sends = []
        for k, (px, py) in enumerate(_other_chips(x, y)):
            cp = _remote(src_ref, out_ref.at[me], send_sems.at[k], recv_sems.at[k], (px, py, c))
            cp.start()
            sends.append(cp)
        for k, (px, py) in enumerate(_other_chips(x, y)):
            _remote(src_ref, out_ref.at[2 * px + py], send_sems.at[k], recv_sems.at[k], (px, py, c)).wait_recv()
        for cp in sends:
            cp.wait_send()
        mine.wait()

    sem = pltpu.SemaphoreType.DMA
    return pl.pallas_call(
        body, name=name, in_specs=[_ANY], out_specs=_ANY, out_shape=SDS((N_CHIPS,) + buf.shape, buf.dtype),
        scratch_shapes=[sem((3,)), sem((3,)), sem], compiler_params=_COMM,
    )(buf)


BIG = ("w_in", "w_proj_attn", "w_proj_pool", "w_proj_conv", "conv_w", "w_out", "w_gate_up", "w_down")
REPLICATED = ("attn_norm", "b_forget", "b_gate", "pool_w", "pool_scale", "ffn_norm", "final_norm")
ORDER = ("attn_norm", "w_in", "b_forget", "b_gate", "w_proj_attn", "pool_w", "pool_scale", "w_proj_pool",
         "conv_w", "w_proj_conv", "w_out", "ffn_norm", "w_gate_up", "w_down", "final_norm")


def _proj_layout(D):
    lay = {"g": 0, "q": 3 * D}
    lay["k"] = lay["q"] + BRANCH_W
    lay["f"] = lay["k"] + BRANCH_W
    lay["v"] = lay["f"] + F_PAD
    lay["pc"] = lay["v"] + BRANCH_W
    lay["width"] = lay["pc"] + 4 * BRANCH_W
    return lay


_REF = dict(q=0, k=512, v=1024, f=1536, u=1544, cv=2056, cb=2568, cc=3080, g=3592)


def _packed_pieces(D):
    pieces = [(_REF["g"], 3 * D), (_REF["q"], BRANCH_W), (_REF["k"], BRANCH_W), (_REF["f"], HEADS),
              (None, F_PAD - HEADS), (_REF["v"], BRANCH_W)]
    for gi in range(N_GROUPS):
        pieces += [(_REF[name] + gi * GROUP_W, GROUP_W) for name in ("u", "cv", "cb", "cc")]
    return pieces


def _packed_runs(D, cs):
    runs, at = [], 0
    for start, n in _packed_pieces(D):
        if start is None:
            runs.append((at, None, 0, n))
            at += n
        while start is not None and n:
            chip, off = divmod(start, cs)
            take = min(n, cs - off)
            runs.append((at, chip, off, take))
            at, start, n = at + take, start + take, n - take
    return runs


def pack_w_in(shards, layer):
    _, _, cs, D = shards.shape
    runs = _packed_runs(D, cs)
    width = runs[-1][0] + runs[-1][3]
    tc = _tile(D, (256, 128))

    def body(s_ref, o_ref):
        for dst, chip, off, rows in runs:
            if chip is None:
                o_ref[dst:dst + rows, :] = jnp.zeros((rows, tc), s_ref.dtype)
            else:
                o_ref[dst:dst + rows, :] = s_ref[chip, off:off + rows, :]

    return pl.pallas_call(
        body, name="pack_w_in", grid=(D // tc,),
        in_specs=[pl.BlockSpec((N_CHIPS, None, cs, tc), lambda j: (0, layer, 0, j))],
        out_specs=pl.BlockSpec((width, tc), lambda j: (0, j)),
        out_shape=SDS((width, D), shards.dtype), compiler_params=_params("arbitrary"),
    )(shards)


def unpack_w_in(p, cs):
    width, D = p.shape
    half = cs // 2
    runs = []
    for src, chip, off, rows in _packed_runs(D, cs):
        while chip is not None and rows:
            h, at = divmod(off, half)
            take = min(rows, half - at)
            runs.append((src, chip, h, at, take))
            src, off, rows = src + take, off + take, rows - take
    tc = _tile(D, (256, 128))

    def body(p_ref, o_ref):
        for src, chip, h, at, rows in runs:
            o_ref[chip, h, at:at + rows, :] = p_ref[src:src + rows, :]

    return pl.pallas_call(
        body, name="unpack_w_in", grid=(D // tc,),
        in_specs=[pl.BlockSpec((width, tc), lambda j: (0, j))],
        out_specs=pl.BlockSpec((N_CHIPS, 2, half, tc), lambda j: (0, 0, 0, j)),
        out_shape=SDS((N_CHIPS, 2, half, D), p.dtype), compiler_params=_params("arbitrary"),
    )(p)


def _split_flat(vec, shapes):
    out, at = [], 0
    for shp in shapes:
        n = int(np.prod(shp))
        out.append(vec[at:at + n].reshape(shp))
        at += n
    return out


def kernel(x, attn_norm, w_in, b_forget, b_gate, w_proj_attn, pool_w, pool_scale, w_proj_pool, conv_w, w_proj_conv, w_out, ffn_norm, w_gate_up, w_down, final_norm, loss_target, m_attn_norm, m_w_in, m_b_forget, m_b_gate, m_w_proj_attn, m_pool_w, m_pool_scale, m_w_proj_pool, m_conv_w, m_w_proj_conv, m_w_out, m_ffn_norm, m_w_gate_up, m_w_down, m_final_norm, v_attn_norm, v_w_in, v_b_forget, v_b_gate, v_w_proj_attn, v_pool_w, v_pool_scale, v_w_proj_pool, v_conv_w, v_w_proj_conv, v_w_out, v_ffn_norm, v_w_gate_up, v_w_down, v_final_norm):
    weights = dict(attn_norm=attn_norm, w_in=w_in, b_forget=b_forget, b_gate=b_gate, w_proj_attn=w_proj_attn,
                   pool_w=pool_w, pool_scale=pool_scale, w_proj_pool=w_proj_pool, conv_w=conv_w,
                   w_proj_conv=w_proj_conv, w_out=w_out, ffn_norm=ffn_norm, w_gate_up=w_gate_up, w_down=w_down,
                   final_norm=final_norm)
    mom_m = dict(attn_norm=m_attn_norm, w_in=m_w_in, b_forget=m_b_forget, b_gate=m_b_gate, w_proj_attn=m_w_proj_attn,
                 pool_w=m_pool_w, pool_scale=m_pool_scale, w_proj_pool=m_w_proj_pool, conv_w=m_conv_w,
                 w_proj_conv=m_w_proj_conv, w_out=m_w_out, ffn_norm=m_ffn_norm, w_gate_up=m_w_gate_up,
                 w_down=m_w_down, final_norm=m_final_norm)
    mom_v = dict(attn_norm=v_attn_norm, w_in=v_w_in, b_forget=v_b_forget, b_gate=v_b_gate, w_proj_attn=v_w_proj_attn,
                 pool_w=v_pool_w, pool_scale=v_pool_scale, w_proj_pool=v_w_proj_pool, conv_w=v_conv_w,
                 w_proj_conv=v_w_proj_conv, w_out=v_w_out, ffn_norm=v_ffn_norm, w_gate_up=v_w_gate_up,
                 w_down=v_w_down, final_norm=v_final_norm)

    Bl, S, D = x.shape
    T = Bl * S
    L = w_in.shape[0]
    F = w_down.shape[1] * N_CHIPS
    lay = _proj_layout(D)
    cst = _placement_constants()
    assert L == N_LAYERS and S % ATTN_BLOCK == 0 and F % (2 * LANES) == 0 and D % BRANCH_W == 0
    assert w_in.shape[2] * N_CHIPS == _REF["g"] + 3 * D and conv_w.shape[2] == GROUP_W

    send = {n: weights[n].astype(BF16) for n in BIG}
    send["conv_w"] = conv_w
    me_chip = 2 * lax.axis_index("x") + lax.axis_index("y")
    send["w_in"] = w_in.transpose(0, 2, 1).astype(BF16)
    pool = gather_buffers(send)
    gather_now(pool, [("w_in", 0)])
    rest = ("w_out", "w_proj_attn", "w_proj_pool", "w_gate_up", "w_proj_conv", "conv_w")
    late = ("w_out", "w_proj_attn", "w_proj_pool", "w_proj_conv", "conv_w")
    jobs = lambda kind, names, layer: [(kind, n, layer) for n in names]
    carried = {
        ("in_proj", 0): jobs("ici", rest, 0),
        ("attn_prep", 0): jobs("fwd", late, 0),
        ("attn_fwd", 0): jobs("fwd", ("w_gate_up",), 0) + jobs("ici", ("w_in",), 1) + jobs("ici", ("w_down",), 0),
        ("poolconv_fwd", 0): jobs("fwd", ("w_in",), 1) + jobs("fwd", ("w_down",), 0),
        ("mix_fwd", 0): jobs("ici", ("w_gate_up",), 1),
        ("gate_up_proj", 0): jobs("ici", late, 1),
        ("ffn_down_fwd", 0): jobs("fwd", ("w_gate_up",), 1),
        ("in_proj", 1): jobs("fwd", late, 1) + jobs("ici", ("w_down",), 1),
        ("attn_prep", 1): jobs("fwd", ("w_down",), 1),
    }
    carry = lambda call, layer: Hosted(pool, carried[call, layer]) if (call, layer) in carried else None
    w_down_f = lambda: pool["w_down"].reshape(L, F, D)
    pool_w_b = pool_w.astype(BF16)
    an3, fn3 = attn_norm.reshape(L, 1, D), ffn_norm.reshape(L, 1, D)
    bg3, ps3 = b_gate.reshape(L, 1, 3 * D), pool_scale.reshape(L, 1, BRANCH_W)
    bf3 = jnp.pad(b_forget, ((0, 0), (0, LANES - HEADS))).reshape(L, 1, LANES)

    xs = x.reshape(T, D)
    saved = []
    w_in_p = []
    for l in range(L):
        w_in_p.append(pack_w_in(pool["w_in"], l))
        proj, h = norm_matmul(xs, an3, w_in_p[l], l, "rows", "in_proj", carry("in_proj", l))
        proj3 = proj.reshape(Bl, S, lay["width"])
        qa, ka = attn_prep(proj3, bf3, l, cst, lay, carry("attn_prep", l))
        ao, lse = attn_fwd(qa, ka, proj3, lay, carry("attn_fwd", l))
        po, co = poolconv_fwd(proj3, pool_w_b, ps3, pool["conv_w"], l, lay, carry("poolconv_fwd", l))
        ao2, po2, co2 = (a.reshape(T, BRANCH_W) for a in (ao, po, co))
        x1, ys, mixed = mix_fwd(ao2, po2, co2, proj, bg3, pool["w_proj_attn"], pool["w_proj_pool"],
                                pool["w_proj_conv"], pool["w_out"], l, xs, carry("mix_fwd", l))
        ab, h2 = norm_matmul(x1, fn3, pool["w_gate_up"], l, "by_shard", "gate_up_proj", carry("gate_up_proj", l))
        x2, s_act = ffn_down_fwd(ab, w_down_f(), l, x1, carry("ffn_down_fwd", l))
        saved.append(dict(x=xs, proj=proj, proj3=proj3, h=h, qa=qa, ka=ka, ao=ao, lse=lse, ao2=ao2, po2=po2,
                          co2=co2, ys=ys, mixed=mixed, x1=x1, ab=ab, h2=h2, s=s_act))
        xs = x2
    w_gu, w_o, conv_w_g = pool["w_gate_up"], pool["w_out"], pool["conv_w"]
    wpa, wpp, wpc = pool["w_proj_attn"], pool["w_proj_pool"], pool["w_proj_conv"]
    w_down_f = w_down_f()

    loss_row, dx, dxb, g_final = loss_head(xs, final_norm.reshape(1, D), loss_target.reshape(T, D))
    loss = lax.psum(loss_row[0, 0], AXES)

    reduced_names = tuple(n for n in BIG if n != "conv_w")
    early_names = tuple(n for n in reduced_names if n != "w_in")
    where = jnp.stack([lax.axis_index("c"), me_chip]).astype(jnp.int32)
    rs = {}

    def reduce_begin(layer, grads):
        for n, g in grads.items():
            g5 = g.reshape((1, N_CHIPS, 2, -1) + g.shape[-1:])
            rs["g%d:%s" % (layer, n)] = g5
            for role in "ra":
                rs["%s%d:%s" % (role, layer, n)] = lax.empty((N_CHIPS,) + g5.shape[3:], BF16)

    swap_jobs = lambda layer, names: [("swap", "g%d:%s" % (layer, n), "r%d:%s" % (layer, n), 0) for n in names]
    xchg_jobs = lambda layer, names: [("xchg", "s%d:%s" % (layer, n), "a%d:%s" % (layer, n)) for n in names]
    join_jobs = lambda layer, names: [("join", "o:" + n, layer) for n in names]

    def pair_sums(layer, names):
        for n in names:
            rs["s%d:%s" % (layer, n)] = add_pair(rs["g%d:%s" % (layer, n)], 0, where, rs["r%d:%s" % (layer, n)],
                                                 "add_pair_" + n)

    def chip_sums(layer, names, slot, n_slots):
        for n in names:
            rs["o:" + n] = add_chips(rs["a%d:%s" % (layer, n)], rs["s%d:%s" % (layer, n)], slot, where, n_slots,
                                     rs.get("o:" + n), "add_chips_" + n)

    small = {n: [None] * L for n in REPLICATED if n != "final_norm"}
    g_conv = [None] * L
    to3 = lambda a: a.reshape(Bl, S, -1)
    for l in reversed(range(L)):
        sv = saved[l]
        behind = (lambda jobs: Hosted(rs, jobs)) if l == 0 else (lambda jobs: None)
        grads = {}
        da, db = ffn_down_bwd(dxb, w_down_f, l, sv["ab"], behind(swap_jobs(1, reduced_names)))
        if l == 0:
            pair_sums(1, reduced_names)
        grads["w_down"] = matmul_tn(sv["s"], [dxb], "grad_w_down", hosted=behind(xchg_jobs(1, ("w_down",))))
        grads["w_gate_up"] = matmul_tn(sv["h2"], [da, db], "grad_w_gate_up", by_dest=True, tn=2 * F // N_CHIPS,
                                       tk=_tile(T, (1024, 512, 256)), hosted=behind(xchg_jobs(1, ("w_gate_up",))))
        dx1, dx1b, g_fn = matmul_nt_normbwd([da, db], w_gu, l, "by_shard", sv["x1"], fn3, dx, "gate_up_bwd",
                                            behind(xchg_jobs(1, ("w_in", "w_out", "w_proj_attn", "w_proj_pool",
                                                                 "w_proj_conv"))))
        small["ffn_norm"][l] = g_fn[0]
        if l == 0:
            chip_sums(1, reduced_names, 1, L)
        dys, dproj, dao, dpo, dco, g_bg = mix_bwd(dx1b, w_o, sv["proj"], bg3, sv["ys"], wpa, wpp, wpc, l,
                                                  lay["width"], behind(join_jobs(1, reduced_names)))
        small["b_gate"][l] = g_bg[0]
        grads["w_out"] = matmul_tn(sv["mixed"], [dx1b], "grad_w_out")
        for n, (name, br) in enumerate((("w_proj_attn", sv["ao2"]), ("w_proj_pool", sv["po2"]),
                                        ("w_proj_conv", sv["co2"]))):
            grads[name] = matmul_tn(br, [dys], "grad_" + name, b_col0=n * D, n_cols=D, by_dest=True,
                                    tn=D // N_CHIPS)
        if l == 0:
            reduce_begin(0, grads)
        dqa, dka, dproj3 = attn_bwd(sv["qa"], sv["ka"], sv["proj3"], to3(dao), sv["ao"], sv["lse"], to3(dproj), lay,
                                    behind(swap_jobs(0, early_names)))
        if l == 0:
            pair_sums(0, early_names)
        dproj3, g_bf = attn_post(dqa, dka, sv["proj3"], bf3, l, dproj3, cst, lay, behind(xchg_jobs(
            0, ("w_out", "w_proj_attn", "w_proj_pool", "w_proj_conv"))))
        small["b_forget"][l] = g_bf[0, :HEADS]
        dproj3, g_pw, g_ps, g_conv[l] = poolconv_bwd(sv["proj3"], to3(dpo), to3(dco), pool_w_b, ps3, conv_w_g, l,
                                                     dproj3, lay, behind(xchg_jobs(0, ("w_down",))))
        small["pool_w"][l], small["pool_scale"][l] = g_pw, g_ps[0]
        dproj = dproj3.reshape(T, lay["width"])
        g_w_in = unpack_w_in(matmul_tn(dproj, [sv["h"]], "grad_w_in", hosted=behind(xchg_jobs(
            0, ("w_gate_up",)))), w_in.shape[2])
        if l:
            reduce_begin(l, {**grads, "w_in": g_w_in})
        else:
            reduce_begin(0, {"w_in": g_w_in})
            comm_now(rs, [swap_jobs(0, ("w_in",))], "swap_w_in_halves")
            pair_sums(0, ("w_in",))
        dx, dxb, g_an = matmul_nt_normbwd([dproj], w_in_p[l], l, "rows", sv["x"], an3, dx1, "in_proj_bwd",
                                          behind(xchg_jobs(0, ("w_in",))))
        small["attn_norm"][l] = g_an[0]
    grad_x = dx.reshape(Bl, S, D)

    small_shapes = [weights[n].shape for n in REPLICATED] + [(L, N_CHIPS) + conv_w.shape[1:]]
    small_vec = jnp.concatenate([jnp.stack(small[n]).reshape(-1) for n in REPLICATED[:-1]]
                                + [g_final[0], jnp.stack(g_conv).reshape(-1)])
    n_small = small_vec.shape[0]
    small_vec = jnp.pad(small_vec, (0, -n_small % (2 * N_CHIPS * 16 * LANES))).astype(BF16)
    rs["g0:small"] = small_vec.reshape(1, N_CHIPS, 2, -1, LANES)
    for role in "ra":
        rs[role + "0:small"] = lax.empty((N_CHIPS,) + rs["g0:small"].shape[3:], BF16)
    last = ("small",)
    comm_now(rs, [swap_jobs(0, last)], "swap_grad_halves")
    pair_sums(0, last)
    comm_now(rs, [xchg_jobs(0, last)], "exchange_grad_chips")
    chip_sums(0, reduced_names, 0, L)
    chip_sums(0, ("small",), 0, 1)
    comm_now(rs, [join_jobs(0, reduced_names + ("small",))], "join_grad_halves")
    shard_grads = {n: rs["o:" + n].reshape((L, -1) + rs["o:" + n].shape[-1:]) for n in reduced_names}
    small_all = allgather_chips(rs["o:small"].reshape(-1, LANES), "allgather_small_grads").reshape(-1)[:n_small]
    *rep_list, conv_all = _split_flat(small_all, small_shapes)
    rep_grads = dict(zip(REPLICATED, rep_list))
    shard_grads["conv_w"] = lax.dynamic_index_in_dim(conv_all, me_chip, 1, keepdims=False)

    delta, new_m, new_v = {}, {}, {}
    for n in BIG:
        shp = weights[n].shape
        if n == "w_in":
            view, back = (lambda a: a.transpose(2, 0, 1)), (lambda a: a.transpose(1, 2, 0))
            g = shard_grads[n].transpose(1, 0, 2)
        else:
            view, back = (lambda a: a.reshape(-1, shp[-1])), (lambda a: a.reshape(shp))
            g = view(shard_grads[n])
        d, nm, nv = adamw(view(weights[n]), g, view(mom_m[n]), view(mom_v[n]), "adamw_" + n)
        delta[n], new_m[n], new_v[n], shard_grads[n] = back(d), back(nm), back(nv), back(g)

    def rows(d):
        vec = jnp.concatenate([d[n].reshape(-1) for n in REPLICATED])
        return jnp.pad(vec, (0, -vec.shape[0] % (8 * LANES))).reshape(-1, LANES)

    outs = adamw(rows(weights), rows(rep_grads), rows(mom_m), rows(mom_v), "adamw_replicated")
    for res, o in zip((delta, new_m, new_v), outs):
        res.update(zip(REPLICATED, _split_flat(o.reshape(-1), small_shapes[:len(REPLICATED)])))
    all_grads = {**shard_grads, **rep_grads}

    return (loss, grad_x, *[all_grads[n] for n in ORDER], *[delta[n] for n in ORDER],
            *[new_m[n] for n in ORDER], *[new_v[n] for n in ORDER])
```

```python
import numpy as np
import jax
import jax.numpy as jnp
from jax import lax
from jax.experimental import pallas as pl
from jax.experimental.pallas import tpu as pltpu

F32, BF16 = jnp.float32, jnp.bfloat16
SDS = jax.ShapeDtypeStruct
MESH = pl.DeviceIdType.MESH
AXES = ("x", "y", "c")
N_CHIPS = 4
N_LAYERS = 2
LANES = 128
VMEM_LIMIT = 48 * 1024 * 1024

HEADS, HEAD_DIM = 8, 64
HEAD_PAD = 128
BRANCH_W = 512
GROUP_W = 128
N_GROUPS = BRANCH_W // GROUP_W
POOL_WINDOWS = (2, 4, 8, 16)
F_PAD = 512
ATTN_BLOCK = 256
RMS_EPS = 1e-6
NEG_INF = -1e30
ADAM_LR, ADAM_B1, ADAM_B2, ADAM_EPS, ADAM_WD, ADAM_STEP = 0.001, 0.9, 0.999, 1e-08, 0.01, 10

NT = (((1,), (1,)), ((), ()))
TN = (((0,), (0,)), ((), ()))
_ANY = pl.BlockSpec(memory_space=pl.ANY)


def _tile(n, prefs):
    for p in prefs:
        if n % p == 0:
            return p
    raise ValueError(f"no tile of {prefs} divides {n}")


def _params(*sem):
    return pltpu.CompilerParams(dimension_semantics=sem, vmem_limit_bytes=VMEM_LIMIT)


def _sigmoid(z):
    return 0.5 * jnp.tanh(0.5 * z) + 0.5


def _split3(x):
    h1 = x.astype(BF16)
    r1 = x - h1.astype(F32)
    h2 = r1.astype(BF16)
    h3 = (r1 - h2.astype(F32)).astype(BF16)
    return h1, h2, h3


def _position():
    return lax.axis_index("x"), lax.axis_index("y"), lax.axis_index("c")


def _other_chips(x, y):
    return [(1 - x, y), (x, 1 - y), (1 - x, 1 - y)]


def _remote(src, dst, send_sem, recv_sem, device):
    return pltpu.make_async_remote_copy(src_ref=src, dst_ref=dst, send_sem=send_sem, recv_sem=recv_sem,
                                        device_id=device, device_id_type=MESH)


ROW_SHARDED = ("w_out", "w_down")
FETCHER = dict(w_in=0, w_out=0, w_proj_attn=0, w_proj_pool=0, w_gate_up=1, w_down=1, w_proj_conv=1, conv_w=1)


class Hosted:
    def __init__(self, pool, jobs):
        self.pool, self.jobs = pool, list(jobs)
        names = set()
        for job in self.jobs:
            names.update(job[1:3] if job[0] in ("swap", "xchg") else job[1:2])
        self.names = sorted(names)


def _hosted_plan(hosted, refs, send_sems, recv_sems):
    x, y, c = _position()
    me = 2 * x + y
    others = _other_chips(x, y)
    sibling = (x, y, 1 - c)
    plan = []
    for j, job in enumerate(hosted.jobs):
        kind = job[0]
        sems = lambda k, j=j: (send_sems.at[j, k], recv_sems.at[j, k])
        if kind in ("ici", "fwd"):
            _, name, layer = job
            ref = refs[name]
            win = (lambda chip, ref=ref, layer=layer: ref.at[layer, chip]) if name in ROW_SHARDED else (
                lambda chip, ref=ref, layer=layer: ref.at[chip, layer])
            mine = c == FETCHER[name]
            if kind == "ici":
                sends = [_remote(win(me), win(me), *sems(k), (px, py, c)) for k, (px, py) in enumerate(others)]
                arrivals = [_remote(win(2 * px + py), win(2 * px + py), *sems(k), (px, py, c))
                            for k, (px, py) in enumerate(others)]
                plan.append((mine, sends, arrivals, []))
            else:
                sends = [_remote(win(2 * px + py), win(2 * px + py), *sems(k), sibling)
                         for k, (px, py) in enumerate(others)]
                plan.append((mine, sends, [], sends))
        elif kind == "swap":
            _, src, dst, layer = job
            cp = _remote(refs[src].at[layer, :, 1 - c], refs[dst], *sems(0), sibling)
            plan.append((True, [cp], [cp], []))
        elif kind == "xchg":
            _, src, dst = job
            sends = [_remote(refs[src].at[2 * px + py], refs[dst].at[me], *sems(k), (px, py, c))
                     for k, (px, py) in enumerate(others)]
            arrivals = [_remote(refs[src].at[me], refs[dst].at[2 * px + py], *sems(k), (px, py, c))
                        for k, (px, py) in enumerate(others)]
            plan.append((True, sends, arrivals, []))
        else:
            _, name, layer = job
            ref = refs[name]
            cp = _remote(ref.at[layer, c], ref.at[layer, c], *sems(0), sibling)
            arrival = _remote(ref.at[layer, c], ref.at[layer, 1 - c], *sems(0), sibling)
            plan.append((True, [cp], [arrival], []))
    return plan


def _hosted_start(plan, now):
    for mine, sends, _, _ in plan:
        @pl.when(now & mine)
        def _(sends=sends):
            for cp in sends:
                cp.start()


def _hosted_finish(plan, now):
    for mine, sends, arrivals, sibling_arrivals in plan:
        @pl.when(now & mine)
        def _(sends=sends, arrivals=arrivals):
            for cp in arrivals:
                cp.wait_recv()
            for cp in sends:
                cp.wait_send()

        if sibling_arrivals:
            @pl.when(now & jnp.logical_not(mine))
            def _(sibling_arrivals=sibling_arrivals):
                for cp in sibling_arrivals:
                    cp.wait_recv()


def _pcall(body, hosted, *, name, grid, in_specs, out_specs, out_shape, semantics, scratch_shapes=(), aliases=None):
    aliases = dict(aliases or {})
    if hosted is None or not hosted.jobs:
        return pl.pallas_call(body, name=name, grid=grid, in_specs=in_specs, out_specs=out_specs,
                              out_shape=out_shape, scratch_shapes=list(scratch_shapes),
                              input_output_aliases=aliases, compiler_params=_params(*semantics))
    single = not isinstance(out_shape, (list, tuple))
    out_specs_l = [out_specs] if single else list(out_specs)
    out_shape_l = [out_shape] if single else list(out_shape)
    n_in, n_out, n_buf, n_job = len(in_specs), len(out_specs_l), len(hosted.names), len(hosted.jobs)

    def carrying(*refs):
        ins, outs = refs[:n_in], refs[n_in + n_buf:n_in + n_buf + n_out]
        bufs = refs[n_in + n_buf + n_out:n_in + 2 * n_buf + n_out]
        rest = refs[n_in + 2 * n_buf + n_out:]
        scratch, send_sems, recv_sems = rest[:-2], rest[-2], rest[-1]
        first, last = True, True
        for axis, size in enumerate(grid):
            first = first & (pl.program_id(axis) == 0)
            last = last & (pl.program_id(axis) == size - 1)
        plan = _hosted_plan(hosted, dict(zip(hosted.names, bufs)), send_sems, recv_sems)
        _hosted_start(plan, first)
        body(*ins, *outs, *scratch)
        _hosted_finish(plan, last)

    def run(*args):
        bufs = [hosted.pool[n] for n in hosted.names]
        sem = pltpu.SemaphoreType.DMA
        res = pl.pallas_call(
            carrying, name=name, grid=grid, in_specs=list(in_specs) + [_ANY] * n_buf,
            out_specs=out_specs_l + [_ANY] * n_buf,
            out_shape=out_shape_l + [SDS(b.shape, b.dtype) for b in bufs],
            scratch_shapes=list(scratch_shapes) + [sem((n_job, 3)), sem((n_job, 3))],
            input_output_aliases={**aliases, **{n_in + i: n_out + i for i in range(n_buf)}},
            compiler_params=pltpu.CompilerParams(dimension_semantics=semantics, vmem_limit_bytes=VMEM_LIMIT,
                                                 has_side_effects=True),
        )(*args, *bufs)
        hosted.pool.update(zip(hosted.names, res[n_out:]))
        return res[0] if single else res[:n_out]

    return run


def _dot(a, b):
    return jnp.dot(a, b, preferred_element_type=F32)


def _dot_nt(a, b):
    return lax.dot_general(a, b, NT, preferred_element_type=F32)


def _dot_tn(a, b):
    return lax.dot_general(a, b, TN, preferred_element_type=F32)


def norm_matmul(x, gain, w, layer, kind, name, hosted=None):
    T, D = x.shape
    if kind == "by_shard":
        tn = w.shape[3]
        N = N_CHIPS * tn
        w_spec = pl.BlockSpec((None, None, D, tn), lambda i, j: (j, layer, 0, 0))
        mm = _dot
    else:
        N = w.shape[0]
        tn = _tile(N, (1024, 512, 256, 128))
        w_spec = pl.BlockSpec((tn, D), lambda i, j: (j, 0))
        mm = _dot_nt
    tm = _tile(T, (1024, 512, 256, 128))

    def body(x_ref, g_ref, w_ref, y_ref, h_ref):
        @pl.when(pl.program_id(1) == 0)
        def _():
            xf = x_ref[...]
            r = lax.rsqrt(jnp.mean(xf * xf, axis=-1, keepdims=True) + RMS_EPS)
            h_ref[...] = ((xf * r) * g_ref[...]).astype(BF16)

        y_ref[...] = mm(h_ref[...], w_ref[...]).astype(BF16)

    return _pcall(
        body, hosted, name=name, grid=(T // tm, N // tn),
        in_specs=[pl.BlockSpec((tm, D), lambda i, j: (i, 0)),
                  pl.BlockSpec((None, 1, D), lambda i, j: (layer, 0, 0)),
                  w_spec],
        out_specs=[pl.BlockSpec((tm, tn), lambda i, j: (i, j)),
                   pl.BlockSpec((tm, D), lambda i, j: (i, 0))],
        out_shape=[SDS((T, N), BF16), SDS((T, D), BF16)],
        semantics=("arbitrary", "arbitrary"),
    )(x, gain, w)


def matmul_nt_normbwd(dys, w, layer, kind, x, gain, dres, name, hosted=None):
    T, D = x.shape
    width = dys[0].shape[1]
    if kind == "by_shard":
        tk = w.shape[3]
        w_spec = pl.BlockSpec((None, None, D, tk), lambda i, k: (k, layer, 0, 0))
        mm = _dot_nt
    else:
        tk = _tile(width, (3584, 1024, 512, 256, 128))
        w_spec = pl.BlockSpec((tk, D), lambda i, k: (k, 0))
        mm = _dot
    per = width // tk
    nk = per * len(dys)
    tm = _tile(T, (512, 256, 128))
    n_dy = len(dys)

    def dy_spec(p):
        return pl.BlockSpec((tm, tk), lambda i, k: (i, jnp.clip(k - p * per, 0, per - 1)))

    def body(*refs):
        dy_refs = refs[:n_dy]
        w_ref, x_ref, g_ref, dres_ref, dx_ref, dxb_ref, dg_ref, acc_ref = refs[n_dy:]
        i, k = pl.program_id(0), pl.program_id(1)

        @pl.when(k == 0)
        def _():
            acc_ref[...] = jnp.zeros_like(acc_ref)

        for p in range(n_dy):
            @pl.when((k >= p * per) & (k < (p + 1) * per))
            def _(p=p):
                acc_ref[...] += mm(dy_refs[p][...], w_ref[...])

        @pl.when(k == nk - 1)
        def _():
            xf = x_ref[...]
            r = lax.rsqrt(jnp.mean(xf * xf, axis=-1, keepdims=True) + RMS_EPS)
            xhat = xf * r
            dh = acc_ref[...]
            dhg = dh * g_ref[...]
            dx = dres_ref[...] + r * (dhg - xhat * jnp.mean(dhg * xhat, axis=-1, keepdims=True))
            dx_ref[...] = dx
            dxb_ref[...] = dx.astype(BF16)
            part = jnp.sum(dh * xhat, axis=0, keepdims=True)

            @pl.when(i == 0)
            def _():
                dg_ref[...] = part

            @pl.when(i > 0)
            def _():
                dg_ref[...] += part

    row = pl.BlockSpec((tm, D), lambda i, k: (i, 0))
    return _pcall(
        body, hosted, name=name, grid=(T // tm, nk),
        in_specs=[dy_spec(p) for p in range(n_dy)] + [
            w_spec, row, pl.BlockSpec((None, 1, D), lambda i, k: (layer, 0, 0)), row],
        out_specs=[row, row, pl.BlockSpec((1, D), lambda i, k: (0, 0))],
        out_shape=[SDS((T, D), F32), SDS((T, D), BF16), SDS((1, D), F32)],
        scratch_shapes=[pltpu.VMEM((tm, D), F32)],
        semantics=("arbitrary", "arbitrary"),
    )(*dys, w, x, gain, dres)


def matmul_tn(a, bs, name, b_col0=0, n_cols=None, by_dest=False, tn=None, tk=None, hosted=None):
    T, M = a.shape
    width = bs[0].shape[1]
    N = n_cols if n_cols else width * len(bs)
    tm = _tile(M, (1024, 512, 256, 128))
    tn = tn or _tile(N, (512, 256, 128))
    tk = tk or _tile(T, (4096, 2048, 1024, 512, 256))
    assert b_col0 % tn == 0 and width % tn == 0
    j0, per, nk, n_b = b_col0 // tn, width // tn, T // tk, len(bs)

    def b_spec(p):
        return pl.BlockSpec((tk, tn), lambda i, j, k: (k, jnp.clip(j0 + j - p * per, 0, per - 1)))

    def body(*refs):
        a_ref, b_refs = refs[0], refs[1:1 + n_b]
        o_ref, acc_ref = refs[-2], refs[-1]
        j, k = pl.program_id(1), pl.program_id(2)

        @pl.when(k == 0)
        def _():
            acc_ref[...] = jnp.zeros_like(acc_ref)

        for p in range(n_b):
            @pl.when((j0 + j >= p * per) & (j0 + j < (p + 1) * per))
            def _(p=p):
                acc_ref[...] += _dot_tn(a_ref[...], b_refs[p][...])

        @pl.when(k == nk - 1)
        def _():
            o_ref[...] = acc_ref[...].astype(BF16)

    if by_dest:
        cs = N // N_CHIPS
        npd = cs // tn
        out_shape = SDS((N_CHIPS, M, cs), BF16)
        out_spec = pl.BlockSpec((None, tm, tn), lambda i, j, k: (j // npd, i, j % npd))
    else:
        out_shape = SDS((M, N), BF16)
        out_spec = pl.BlockSpec((tm, tn), lambda i, j, k: (i, j))
    return _pcall(
        body, hosted, name=name, grid=(M // tm, N // tn, nk),
        in_specs=[pl.BlockSpec((tk, tm), lambda i, j, k: (k, i))] + [b_spec(p) for p in range(n_b)],
        out_specs=out_spec, out_shape=out_shape,
        scratch_shapes=[pltpu.VMEM((tm, tn), F32)],
        semantics=("arbitrary", "arbitrary", "arbitrary"),
    )(a, *bs)


def ffn_down_fwd(ab, w_down, layer, x1, hosted=None):
    T, D = x1.shape
    F = w_down.shape[1]
    tm = _tile(T, (512, 256, 128))
    tk = F // 2
    nk = F // tk

    def body(a_ref, b_ref, w_ref, x_ref, x2_ref, s_ref, acc_ref):
        k = pl.program_id(1)

        @pl.when(k == 0)
        def _():
            acc_ref[...] = x_ref[...]

        a = a_ref[...].astype(F32)
        s = (a * _sigmoid(a) * b_ref[...].astype(F32)).astype(BF16)
        s_ref[...] = s
        acc_ref[...] += _dot(s, w_ref[...])

        @pl.when(k == nk - 1)
        def _():
            x2_ref[...] = acc_ref[...]

    return _pcall(
        body, hosted, name="ffn_down_fwd", grid=(T // tm, nk),
        in_specs=[pl.BlockSpec((tm, tk), lambda i, k: (i, k)),
                  pl.BlockSpec((tm, tk), lambda i, k: (i, nk + k)),
                  pl.BlockSpec((None, tk, D), lambda i, k: (layer, k, 0)),
                  pl.BlockSpec((tm, D), lambda i, k: (i, 0))],
        out_specs=[pl.BlockSpec((tm, D), lambda i, k: (i, 0)),
                   pl.BlockSpec((tm, tk), lambda i, k: (i, k))],
        out_shape=[SDS((T, D), F32), SDS((T, F), BF16)],
        scratch_shapes=[pltpu.VMEM((tm, D), F32)],
        semantics=("arbitrary", "arbitrary"),
    )(ab, ab, w_down, x1)


def ffn_down_bwd(dx2b, w_down, layer, ab, hosted=None):
    T, D = dx2b.shape
    F = w_down.shape[1]
    tm = _tile(T, (512, 256, 128))
    tn = F // 2
    nj = F // tn

    def body(dx_ref, w_ref, a_ref, b_ref, da_ref, db_ref):
        ds = _dot_nt(dx_ref[...], w_ref[...])
        a = a_ref[...].astype(F32)
        sg = _sigmoid(a)
        da_ref[...] = (ds * b_ref[...].astype(F32) * (sg * (1.0 + a * (1.0 - sg)))).astype(BF16)
        db_ref[...] = (ds * (a * sg)).astype(BF16)

    blk = pl.BlockSpec((tm, tn), lambda i, j: (i, j))
    return _pcall(
        body, hosted, name="ffn_down_bwd", grid=(T // tm, nj),
        in_specs=[pl.BlockSpec((tm, D), lambda i, j: (i, 0)),
                  pl.BlockSpec((None, tn, D), lambda i, j: (layer, j, 0)),
                  blk, pl.BlockSpec((tm, tn), lambda i, j: (i, nj + j))],
        out_specs=[blk, blk],
        out_shape=[SDS((T, F), BF16), SDS((T, F), BF16)],
        semantics=("arbitrary", "arbitrary"),
    )(dx2b, w_down, ab, ab)


def _mix_specs(tm, D, layer):
    cs = D // N_CHIPS
    row = lambda w: pl.BlockSpec((tm, w), lambda i: (i, 0))
    wp = pl.BlockSpec((N_CHIPS, None, BRANCH_W, cs), lambda i: (0, layer, 0, 0))
    wo = pl.BlockSpec((None, N_CHIPS, cs, D), lambda i: (layer, 0, 0, 0))
    bg = pl.BlockSpec((None, 1, 3 * D), lambda i: (layer, 0, 0))
    return row, wp, wo, bg


def mix_fwd(ao, po, co, proj, b_gate, wpa, wpp, wpc, w_out, layer, x, hosted=None):
    T, D = x.shape
    cs = D // N_CHIPS
    tm = _tile(T, (256, 128))
    row, wp, wo, bg = _mix_specs(tm, D, layer)

    def body(ao_ref, po_ref, co_ref, g_ref, bg_ref, wpa_ref, wpp_ref, wpc_ref, wo_ref, x_ref,
             x1_ref, ys_ref, mixed_ref):
        mixed = jnp.zeros((tm, D), F32)
        for n, (br, wp_ref) in enumerate(((ao_ref, wpa_ref), (po_ref, wpp_ref), (co_ref, wpc_ref))):
            y = jnp.concatenate([_dot(br[...], wp_ref[j]) for j in range(N_CHIPS)], axis=1)
            cols = slice(n * D, (n + 1) * D)
            gate = _sigmoid(g_ref[:, cols].astype(F32) + bg_ref[:, cols])
            ys_ref[:, cols] = y.astype(BF16)
            mixed = mixed + gate * y
        mb = mixed.astype(BF16)
        mixed_ref[...] = mb
        acc = x_ref[...]
        for j in range(N_CHIPS):
            acc = acc + _dot(mb[:, j * cs:(j + 1) * cs], wo_ref[j])
        x1_ref[...] = acc

    return _pcall(
        body, hosted, name="mix_fwd", grid=(T // tm,),
        in_specs=[row(BRANCH_W), row(BRANCH_W), row(BRANCH_W), row(3 * D), bg, wp, wp, wp, wo, row(D)],
        out_specs=[row(D), row(3 * D), row(D)],
        out_shape=[SDS((T, D), F32), SDS((T, 3 * D), BF16), SDS((T, D), BF16)],
        semantics=("arbitrary",),
    )(ao, po, co, proj, b_gate, wpa, wpp, wpc, w_out, x)


def mix_bwd(dx1b, w_out, proj, b_gate, ys, wpa, wpp, wpc, layer, width, hosted=None):
    T, D = dx1b.shape
    cs = D // N_CHIPS
    tm = _tile(T, (256, 128))
    row, wp, wo, bg = _mix_specs(tm, D, layer)

    def body(dx_ref, wo_ref, g_ref, bg_ref, ys_ref, wpa_ref, wpp_ref, wpc_ref,
             dys_ref, dg_ref, dao_ref, dpo_ref, dco_ref, dbg_ref):
        i = pl.program_id(0)
        dx = dx_ref[...]
        dmixed = jnp.concatenate([_dot_nt(dx, wo_ref[j]) for j in range(N_CHIPS)], axis=1)
        for n, (wp_ref, dbr) in enumerate(((wpa_ref, dao_ref), (wpp_ref, dpo_ref), (wpc_ref, dco_ref))):
            cols = slice(n * D, (n + 1) * D)
            gate = _sigmoid(g_ref[:, cols].astype(F32) + bg_ref[:, cols])
            dy = (dmixed * gate).astype(BF16)
            dys_ref[:, cols] = dy
            dgp = dmixed * ys_ref[:, cols].astype(F32) * gate * (1.0 - gate)
            dg_ref[:, cols] = dgp.astype(BF16)
            part = jnp.sum(dgp, axis=0, keepdims=True)

            @pl.when(i == 0)
            def _():
                dbg_ref[:, cols] = part

            @pl.when(i > 0)
            def _():
                dbg_ref[:, cols] += part

            acc = jnp.zeros((tm, BRANCH_W), F32)
            for j in range(N_CHIPS):
                acc = acc + _dot_nt(dy[:, j * cs:(j + 1) * cs], wp_ref[j])
            dbr[...] = acc.astype(BF16)

    return _pcall(
        body, hosted, name="mix_bwd", grid=(T // tm,),
        in_specs=[row(D), wo, row(3 * D), bg, row(3 * D), wp, wp, wp],
        out_specs=[row(3 * D), row(3 * D), row(BRANCH_W), row(BRANCH_W), row(BRANCH_W),
                   pl.BlockSpec((1, 3 * D), lambda i: (0, 0))],
        out_shape=[SDS((T, 3 * D), BF16), SDS((T, width), BF16), SDS((T, BRANCH_W), BF16),
                   SDS((T, BRANCH_W), BF16), SDS((T, BRANCH_W), BF16), SDS((1, 3 * D), F32)],
        semantics=("arbitrary",),
    )(dx1b, w_out, proj, b_gate, ys, wpa, wpp, wpc)


def loss_head(x2, gain, target):
    T, D = x2.shape
    tm = _tile(T, (512, 256, 128))

    def body(x_ref, g_ref, t_ref, loss_ref, dx_ref, dxb_ref, dg_ref):
        i = pl.program_id(0)
        xf = x_ref[...]
        g = g_ref[...]
        r = lax.rsqrt(jnp.mean(xf * xf, axis=-1, keepdims=True) + RMS_EPS)
        xhat = xf * r
        diff = xhat * g - t_ref[...]
        part_loss = 0.5 * jnp.sum(jnp.mean(diff * diff, axis=-1, keepdims=True), axis=0, keepdims=True)
        dy = diff * (1.0 / D)
        dhg = dy * g
        dx = r * (dhg - xhat * jnp.mean(dhg * xhat, axis=-1, keepdims=True))
        dx_ref[...] = dx
        dxb_ref[...] = dx.astype(BF16)
        part_g = jnp.sum(dy * xhat, axis=0, keepdims=True)
        part_l = jnp.broadcast_to(part_loss, (1, LANES))

        @pl.when(i == 0)
        def _():
            dg_ref[...] = part_g
            loss_ref[...] = part_l

        @pl.when(i > 0)
        def _():
            dg_ref[...] += part_g
            loss_ref[...] += part_l

    row = pl.BlockSpec((tm, D), lambda i: (i, 0))
    return pl.pallas_call(
        body, name="loss_head", grid=(T // tm,),
        in_specs=[row, pl.BlockSpec((1, D), lambda i: (0, 0)), row],
        out_specs=[pl.BlockSpec((1, LANES), lambda i: (0, 0)), row, row, pl.BlockSpec((1, D), lambda i: (0, 0))],
        out_shape=[SDS((1, LANES), F32), SDS((T, D), F32), SDS((T, D), BF16), SDS((1, D), F32)],
        compiler_params=_params("arbitrary"),
    )(x2, gain, target)


def _placement_constants():
    w = HEADS * HEAD_PAD
    pq = np.zeros((BRANCH_W, w), np.float32)
    pk = np.zeros((BRANCH_W, w), np.float32)
    pfq = np.zeros((3, LANES, w), np.float32)
    pfk = np.zeros((3, LANES, w), np.float32)
    cq = np.zeros((1, w), np.float32)
    ck = np.zeros((1, w), np.float32)
    eq = np.zeros((w, LANES), np.float32)
    ek = np.zeros((w, LANES), np.float32)
    for h in range(HEADS):
        for d in range(HEAD_DIM):
            pq[h * HEAD_DIM + d, h * HEAD_PAD + d] = HEAD_DIM ** -0.5
            pk[h * HEAD_DIM + d, h * HEAD_PAD + d] = 1.0
        for i in range(3):
            pfq[i, h, h * HEAD_PAD + HEAD_DIM + i] = 1.0
            pfk[i, h, h * HEAD_PAD + HEAD_DIM + 3 + i] = -1.0
            cq[0, h * HEAD_PAD + HEAD_DIM + 3 + i] = 1.0
            ck[0, h * HEAD_PAD + HEAD_DIM + i] = 1.0
        eq[h * HEAD_PAD + HEAD_DIM, h] = 1.0
        ek[h * HEAD_PAD + HEAD_DIM + 3, h] = -1.0
    bf = lambda a: jnp.asarray(a, BF16)
    return dict(pq=bf(pq), pk=bf(pk), pfq=bf(pfq), pfk=bf(pfk), cq=jnp.asarray(cq), ck=jnp.asarray(ck),
                pqkt=bf(np.concatenate([pq.T, pk.T], axis=0)), eq=bf(eq), ek=bf(ek))


def attn_prep(proj3, bf_rows, layer, cst, lay, hosted=None):
    Bl, S, _ = proj3.shape
    ts = ATTN_BLOCK
    w = HEADS * HEAD_PAD

    def body(q_ref, k_ref, f_ref, bf_ref, pq_ref, pk_ref, pfq_ref, pfk_ref, cq_ref, ck_ref,
             qa_ref, ka_ref, carry_ref):
        @pl.when(pl.program_id(1) == 0)
        def _():
            carry_ref[...] = jnp.zeros_like(carry_ref)

        z = f_ref[...].astype(F32) + bf_ref[...]
        logf = jnp.minimum(z, 0.0) - jnp.log(1.0 + jnp.exp(-jnp.abs(z)))
        r = lax.broadcasted_iota(jnp.int32, (ts, ts), 0)
        c = lax.broadcasted_iota(jnp.int32, (ts, ts), 1)
        tri = jnp.where(r >= c, 1.0, 0.0).astype(BF16)
        fcum = carry_ref[...]
        for part in _split3(logf):
            fcum = fcum + _dot(tri, part)
        carry_ref[...] = fcum[ts - 1:ts, :]
        qa = _dot(q_ref[...], pq_ref[...]) + cq_ref[...]
        ka = _dot(k_ref[...], pk_ref[...]) + ck_ref[...]
        for i, part in enumerate(_split3(fcum)):
            qa = qa + _dot(part, pfq_ref[i])
            ka = ka + _dot(part, pfk_ref[i])
        qa_ref[...] = qa.astype(BF16)
        ka_ref[...] = ka.astype(BF16)

    cfull = lambda shape: pl.BlockSpec(shape, lambda b, s: (0,) * len(shape))
    return _pcall(
        body, hosted, name="attn_prep", grid=(Bl, S // ts),
        in_specs=[pl.BlockSpec((None, ts, BRANCH_W), lambda b, s: (b, s, lay["q"] // BRANCH_W)),
                  pl.BlockSpec((None, ts, BRANCH_W), lambda b, s: (b, s, lay["k"] // BRANCH_W)),
                  pl.BlockSpec((None, ts, LANES), lambda b, s: (b, s, lay["f"] // LANES)),
                  pl.BlockSpec((None, 1, LANES), lambda b, s: (layer, 0, 0)),
                  cfull((BRANCH_W, w)), cfull((BRANCH_W, w)),
                  cfull((3, LANES, w)), cfull((3, LANES, w)), cfull((1, w)), cfull((1, w))],
        out_specs=[pl.BlockSpec((None, ts, w), lambda b, s: (b, s, 0)),
                   pl.BlockSpec((None, ts, w), lambda b, s: (b, s, 0))],
        out_shape=[SDS((Bl, S, w), BF16), SDS((Bl, S, w), BF16)],
        scratch_shapes=[pltpu.VMEM((1, LANES), F32)],
        semantics=("arbitrary", "arbitrary"),
    )(proj3, proj3, proj3, bf_rows, cst["pq"], cst["pk"], cst["pfq"], cst["pfk"], cst["cq"], cst["ck"])


def attn_fwd(qa, ka, proj3, lay, hosted=None):
    Bl, S, _ = qa.shape
    tq = ATTN_BLOCK
    nq = S // tq
    pairs = HEADS // 2
    pw = 2 * HEAD_PAD
    vw = 2 * HEAD_DIM

    def body(qa_ref, ka_ref, v_ref, o_ref, lse_ref):
        row = lax.broadcasted_iota(jnp.int32, (tq, tq), 0)
        col = lax.broadcasted_iota(jnp.int32, (tq, tq), 1)
        causal = row <= col
        for i in range(nq):
            nk = (i + 1) * tq
            rows = slice(i * tq, nk)
            o_t = []
            for h in range(2):
                hs = slice(h * HEAD_PAD, (h + 1) * HEAD_PAD)
                st = _dot_nt(ka_ref[0:nk, hs], qa_ref[rows, hs])
                diag = jnp.where(causal, st[nk - tq:], NEG_INF)
                m = jnp.max(diag, axis=0, keepdims=True)
                if i:
                    m = jnp.maximum(m, jnp.max(st[:nk - tq], axis=0, keepdims=True))
                p_diag = jnp.exp(diag - m)
                l = jnp.sum(p_diag, axis=0, keepdims=True)
                if i:
                    p_top = jnp.exp(st[:nk - tq] - m)
                    l = l + jnp.sum(p_top, axis=0, keepdims=True)
                    p = jnp.concatenate([p_top.astype(BF16), p_diag.astype(BF16)], axis=0)
                else:
                    p = p_diag.astype(BF16)
                acc = _dot_tn(v_ref[0:nk, :], p)
                o_t.append(acc[h * HEAD_DIM:(h + 1) * HEAD_DIM, :] / l)
                lse_ref[h:h + 1, rows] = m + jnp.log(l)
            o_ref[rows, :] = jnp.concatenate(o_t, axis=0).T.astype(BF16)

    return _pcall(
        body, hosted, name="attn_fwd", grid=(Bl, pairs),
        in_specs=[pl.BlockSpec((None, S, pw), lambda b, p: (b, 0, p)),
                  pl.BlockSpec((None, S, pw), lambda b, p: (b, 0, p)),
                  pl.BlockSpec((None, S, vw), lambda b, p: (b, 0, lay["v"] // vw + p))],
        out_specs=[pl.BlockSpec((None, S, vw), lambda b, p: (b, 0, p)),
                   pl.BlockSpec((None, None, 2, S), lambda b, p: (b, p, 0, 0))],
        out_shape=[SDS((Bl, S, BRANCH_W), BF16), SDS((Bl, pairs, 2, S), F32)],
        semantics=("arbitrary", "arbitrary"),
    )(qa, ka, proj3)


def attn_bwd(qa, ka, proj3, dao, ao, lse, dproj3, lay, hosted=None):
    Bl, S, _ = qa.shape
    tk = ATTN_BLOCK
    nq = S // tk
    pairs = HEADS // 2
    pw = 2 * HEAD_PAD
    vw = 2 * HEAD_DIM

    def body(qa_ref, ka_ref, v_ref, do_ref, o_ref, lse_ref, _, dqa_ref, dka_ref, dv_ref):
        row = lax.broadcasted_iota(jnp.int32, (tk, tk), 0)
        col = lax.broadcasted_iota(jnp.int32, (tk, tk), 1)
        causal = row <= col
        lane8 = lax.broadcasted_iota(jnp.int32, (8, vw), 1)
        lane_s = lax.broadcasted_iota(jnp.int32, (S, vw), 1)
        lane_k = lax.broadcasted_iota(jnp.int32, (tk, vw), 1)
        doo = do_ref[...].astype(F32) * o_ref[...].astype(F32)
        hi = doo.astype(BF16)
        lo = (doo - hi.astype(F32)).astype(BF16)
        delta, v_head = [], []
        for h in range(2):
            sel = jnp.where((lane8 >= h * HEAD_DIM) & (lane8 < (h + 1) * HEAD_DIM), 1.0, 0.0).astype(BF16)
            delta.append((_dot_nt(sel, hi) + _dot_nt(sel, lo))[0:1, :])
            in_head = (lane_s >= h * HEAD_DIM) & (lane_s < (h + 1) * HEAD_DIM)
            v_head.append(jnp.where(in_head, v_ref[...], jnp.zeros_like(v_ref[...])))
        dqa_ref[...] = jnp.zeros_like(dqa_ref)
        for j in range(nq):
            q0 = j * tk
            krows = slice(q0, q0 + tk)
            do = do_ref[q0:, :]
            dvs = []
            for h in range(2):
                hs = slice(h * HEAD_PAD, (h + 1) * HEAD_PAD)
                k = ka_ref[krows, hs]
                q = qa_ref[q0:, hs]
                st = _dot_nt(k, q)
                p = jnp.exp(st - lse_ref[h:h + 1, q0:])
                p_diag = jnp.where(causal, p[:, :tk], 0.0)
                p = jnp.concatenate([p_diag, p[:, tk:]], axis=1) if j < nq - 1 else p_diag
                dvs.append(_dot(p.astype(BF16), do))
                dpt = _dot_nt(v_head[h][krows, :], do)
                ds = (p * (dpt - delta[h][:, q0:])).astype(BF16)
                dka_ref[krows, hs] = _dot(ds, q)
                dqa_ref[q0:, hs] += _dot_tn(ds, k)
            dv_ref[krows, :] = jnp.where(lane_k < HEAD_DIM, dvs[0], dvs[1]).astype(BF16)

    seq = lambda w, c0=0: pl.BlockSpec((None, S, w), lambda b, p: (b, 0, c0 + p))
    return _pcall(
        body, hosted, name="attn_bwd", grid=(Bl, pairs),
        in_specs=[seq(pw), seq(pw), seq(vw, lay["v"] // vw), seq(vw), seq(vw),
                  pl.BlockSpec((None, None, 2, S), lambda b, p: (b, p, 0, 0)), _ANY],
        out_specs=[seq(pw), seq(pw), seq(vw, lay["v"] // vw)],
        out_shape=[SDS((Bl, S, HEADS * HEAD_PAD), F32), SDS((Bl, S, HEADS * HEAD_PAD), F32),
                   SDS(dproj3.shape, BF16)],
        aliases={6: 2}, semantics=("arbitrary", "arbitrary"),
    )(qa, ka, proj3, dao, ao, lse, dproj3)


def attn_post(dqa, dka, proj3, bf_rows, layer, dproj3, cst, lay, hosted=None):
    Bl, S, w = dqa.shape
    ts = ATTN_BLOCK
    ns = S // ts
    qkf = 2 * BRANCH_W + F_PAD

    def body(dqa_ref, dka_ref, f_ref, bf_ref, pqkt_ref, eq_ref, ek_ref, _, dqkf_ref, dbf_ref, carry_ref):
        b, s = pl.program_id(0), pl.program_id(1)

        @pl.when(s == 0)
        def _():
            carry_ref[...] = jnp.zeros_like(carry_ref)

        dqa_v, dka_v = dqa_ref[...], dka_ref[...]
        qh = dqa_v.astype(BF16)
        kh = dka_v.astype(BF16)
        dqkf_ref[:, :BRANCH_W] = _dot(qh, pqkt_ref[:w, :]).astype(BF16)
        dqkf_ref[:, BRANCH_W:2 * BRANCH_W] = _dot(kh, pqkt_ref[w:, :]).astype(BF16)
        ql = (dqa_v - qh.astype(F32)).astype(BF16)
        kl = (dka_v - kh.astype(F32)).astype(BF16)
        d_f = (_dot(qh, eq_ref[...]) + _dot(ql, eq_ref[...])) + (_dot(kh, ek_ref[...]) + _dot(kl, ek_ref[...]))
        r = lax.broadcasted_iota(jnp.int32, (ts, ts), 0)
        c = lax.broadcasted_iota(jnp.int32, (ts, ts), 1)
        triu = jnp.where(c >= r, 1.0, 0.0).astype(BF16)
        rev = carry_ref[...]
        for part in _split3(d_f):
            rev = rev + _dot(triu, part)
        carry_ref[...] = rev[0:1, :]
        z = f_ref[...].astype(F32) + bf_ref[...]
        lane = lax.broadcasted_iota(jnp.int32, (ts, LANES), 1)
        dfl = jnp.where(lane < HEADS, rev / (1.0 + jnp.exp(z)), 0.0)
        dqkf_ref[:, 2 * BRANCH_W:] = jnp.concatenate(
            [dfl.astype(BF16), jnp.zeros((ts, F_PAD - LANES), BF16)], axis=1)
        part = jnp.sum(dfl, axis=0, keepdims=True)

        @pl.when((b == 0) & (s == 0))
        def _():
            dbf_ref[...] = part

        @pl.when((b > 0) | (s > 0))
        def _():
            dbf_ref[...] += part

    assert lay["q"] % qkf == 0
    cfull = lambda shape: pl.BlockSpec(shape, lambda b, s: (0,) * len(shape))
    rev_blk = lambda wd, c0=0: pl.BlockSpec((None, ts, wd), lambda b, s: (b, ns - 1 - s, c0))
    return _pcall(
        body, hosted, name="attn_post", grid=(Bl, ns),
        in_specs=[rev_blk(w), rev_blk(w), rev_blk(LANES, lay["f"] // LANES),
                  pl.BlockSpec((None, 1, LANES), lambda b, s: (layer, 0, 0)),
                  cfull((2 * w, BRANCH_W)), cfull((w, LANES)), cfull((w, LANES)), _ANY],
        out_specs=[rev_blk(qkf, lay["q"] // qkf), cfull((1, LANES))],
        out_shape=[SDS(dproj3.shape, BF16), SDS((1, LANES), F32)],
        scratch_shapes=[pltpu.VMEM((1, LANES), F32)],
        aliases={7: 0}, semantics=("arbitrary", "arbitrary"),
    )(dqa, dka, proj3, bf_rows, cst["pqkt"], cst["eq"], cst["ek"], dproj3)


def _shift_down(x, k, row):
    return jnp.where(row >= k, pltpu.roll(x, k, axis=0), 0.0)


def _shift_up(x, k, row):
    n = x.shape[0]
    return jnp.where(row < n - k, pltpu.roll(x, n - k, axis=0), 0.0)


def _window_sum(x, g, row, shift):
    s2 = x + shift(x, 1, row)
    s4 = s2 + shift(s2, 2, row)
    s8 = s4 + shift(s4, 4, row)
    s16 = s8 + shift(s8, 8, row)
    return jnp.where(g == 0, s2, jnp.where(g == 1, s4, jnp.where(g == 2, s8, s16)))


def _window_count(g, row):
    wnd = jnp.where(g == 0, 2, jnp.where(g == 1, 4, jnp.where(g == 2, 8, 16)))
    return jnp.minimum(row + 1, wnd).astype(F32)


def _group_columns(ref):
    return [ref[:, n * GROUP_W:(n + 1) * GROUP_W].astype(F32) for n in range(4)]


def poolconv_fwd(proj3, pool_w, pool_scale, conv_w, layer, lay, hosted=None):
    Bl, S, _ = proj3.shape

    def body(x_ref, pw_ref, ps_ref, cw_ref, po_ref, co_ref):
        g = pl.program_id(1)
        row = lax.broadcasted_iota(jnp.int32, (S, GROUP_W), 0)
        u, cv, cb, cc = _group_columns(x_ref)
        d = _window_sum(u, g, row, _shift_down) / _window_count(g, row) - u
        po_ref[...] = (_dot(d.astype(BF16), pw_ref[...]) * ps_ref[...]).astype(BF16)
        z = cc * cv
        y = cw_ref[0:1, :] * _shift_down(z, 2, row) + cw_ref[1:2, :] * _shift_down(z, 1, row) + cw_ref[2:3, :] * z
        co_ref[...] = (cb * y).astype(BF16)

    out = pl.BlockSpec((None, S, GROUP_W), lambda b, g: (b, 0, g))
    return _pcall(
        body, hosted, name="poolconv_fwd", grid=(Bl, N_GROUPS),
        in_specs=[pl.BlockSpec((None, S, BRANCH_W), lambda b, g: (b, 0, lay["pc"] // BRANCH_W + g)),
                  pl.BlockSpec((None, None, GROUP_W, GROUP_W), lambda b, g: (layer, g, 0, 0)),
                  pl.BlockSpec((None, 1, GROUP_W), lambda b, g: (layer, 0, g)),
                  pl.BlockSpec((None, None, 3, GROUP_W), lambda b, g: (g, layer, 0, 0))],
        out_specs=[out, out],
        out_shape=[SDS((Bl, S, BRANCH_W), BF16), SDS((Bl, S, BRANCH_W), BF16)],
        semantics=("arbitrary", "arbitrary"),
    )(proj3, pool_w, pool_scale, conv_w)


def poolconv_bwd(proj3, dpo, dco, pool_w, pool_scale, conv_w, layer, dproj3, lay, hosted=None):
    Bl, S, _ = proj3.shape

    def body(x_ref, dpo_ref, dco_ref, pw_ref, ps_ref, cw_ref, _, dx_ref, dpw_ref, dps_ref, dcw_ref):
        g, b = pl.program_id(0), pl.program_id(1)
        row = lax.broadcasted_iota(jnp.int32, (S, GROUP_W), 0)
        cnt = _window_count(g, row)
        u, cv, cb, cc = _group_columns(x_ref)
        d = (_window_sum(u, g, row, _shift_down) / cnt - u).astype(BF16)
        pw = pw_ref[...]
        ypre = _dot(d, pw)
        dpo_v = dpo_ref[...].astype(F32)
        dps = jnp.sum(dpo_v * ypre, axis=0, keepdims=True)
        dyp = (dpo_v * ps_ref[...]).astype(BF16)
        dpw = _dot_tn(d, dyp)
        dd = _dot_nt(dyp, pw)
        dx_ref[:, 0:GROUP_W] = (_window_sum(dd / cnt, g, row, _shift_up) - dd).astype(BF16)

        z = cc * cv
        z1, z2 = _shift_down(z, 1, row), _shift_down(z, 2, row)
        w0, w1, w2 = cw_ref[0:1, :], cw_ref[1:2, :], cw_ref[2:3, :]
        y = w0 * z2 + w1 * z1 + w2 * z
        dco_v = dco_ref[...].astype(F32)
        dy = dco_v * cb
        dz = w0 * _shift_up(dy, 2, row) + w1 * _shift_up(dy, 1, row) + w2 * dy
        dx_ref[:, GROUP_W:2 * GROUP_W] = (dz * cc).astype(BF16)
        dx_ref[:, 2 * GROUP_W:3 * GROUP_W] = (dco_v * y).astype(BF16)
        dx_ref[:, 3 * GROUP_W:] = (dz * cv).astype(BF16)
        dcw = jnp.concatenate([jnp.sum(dy * z2, axis=0, keepdims=True),
                               jnp.sum(dy * z1, axis=0, keepdims=True),
                               jnp.sum(dy * z, axis=0, keepdims=True)], axis=0)

        @pl.when(b == 0)
        def _():
            dpw_ref[...] = dpw
            dps_ref[...] = dps
            dcw_ref[...] = dcw

        @pl.when(b > 0)
        def _():
            dpw_ref[...] += dpw
            dps_ref[...] += dps
            dcw_ref[...] += dcw

    blk = pl.BlockSpec((None, S, GROUP_W), lambda g, b: (b, 0, g))
    pc = pl.BlockSpec((None, S, BRANCH_W), lambda g, b: (b, 0, lay["pc"] // BRANCH_W + g))
    return _pcall(
        body, hosted, name="poolconv_bwd", grid=(N_GROUPS, Bl),
        in_specs=[pc, blk, blk,
                  pl.BlockSpec((None, None, GROUP_W, GROUP_W), lambda g, b: (layer, g, 0, 0)),
                  pl.BlockSpec((None, 1, GROUP_W), lambda g, b: (layer, 0, g)),
                  pl.BlockSpec((None, None, 3, GROUP_W), lambda g, b: (g, layer, 0, 0)), _ANY],
        out_specs=[pc, pl.BlockSpec((None, GROUP_W, GROUP_W), lambda g, b: (g, 0, 0)),
                   pl.BlockSpec((1, GROUP_W), lambda g, b: (0, g)),
                   pl.BlockSpec((None, 3, GROUP_W), lambda g, b: (g, 0, 0))],
        out_shape=[SDS(dproj3.shape, BF16), SDS((N_GROUPS, GROUP_W, GROUP_W), F32), SDS((1, BRANCH_W), F32),
                   SDS((N_GROUPS, 3, GROUP_W), F32)],
        aliases={6: 0}, semantics=("arbitrary", "arbitrary"),
    )(proj3, dpo, dco, pool_w, pool_scale, conv_w, dproj3)


def _tile_2d(rows, cols, n_arrays):
    budget = VMEM_LIMIT // 2
    lanes = -(-cols // LANES) * LANES
    if rows % 8 == 0:
        for t in (2048, 1024, 512, 256, 128, 64, 32, 16, 8):
            if rows % t == 0 and 2 * n_arrays * t * lanes * 4 <= budget:
                return t, cols
    for t in (1024, 512, 256, 128):
        if cols % t == 0 and 2 * n_arrays * (rows + 8) * t * 4 <= budget:
            return rows, t
    return rows, cols


def add_pair(kept, layer, where, received, name):
    _, n, _, R, C = kept.shape
    tr, tc = _tile_2d(R, C, 3)

    def body(where_ref, a_ref, b_ref, o_ref):
        o_ref[...] = (a_ref[...].astype(F32) + b_ref[...].astype(F32)).astype(BF16)

    blk = pl.BlockSpec((None, tr, tc), lambda d, i, j, where_ref: (d, i, j))
    grid_spec = pltpu.PrefetchScalarGridSpec(
        num_scalar_prefetch=1, grid=(n, R // tr, C // tc),
        in_specs=[pl.BlockSpec((None, None, None, tr, tc),
                               lambda d, i, j, where_ref: (layer, d, where_ref[0], i, j)), blk],
        out_specs=blk)
    return pl.pallas_call(body, name=name, grid_spec=grid_spec, out_shape=SDS((n, R, C), BF16),
                          compiler_params=_params("arbitrary", "arbitrary", "arbitrary"))(where, kept, received)


def add_chips(arrived, own, layer, where, n_layers, prev, name):
    _, R, C = arrived.shape
    tr, tc = _tile_2d(R, C, 6)

    def body(where_ref, a0, a1, a2, a3, own_ref, *rest):
        o_ref = rest[-1]
        chip = where_ref[1]
        acc = None
        for j, a_ref in enumerate((a0, a1, a2, a3)):
            term = jnp.where(chip == j, own_ref[...], a_ref[...]).astype(F32)
            acc = term if acc is None else acc + term
        o_ref[...] = acc

    def slot(j):
        return pl.BlockSpec((None, tr, tc), lambda i, k, where_ref, j=j: (
            jnp.where(where_ref[1] == j, (j + 1) % N_CHIPS, j), i, k))

    in_specs = [slot(j) for j in range(N_CHIPS)] + [
        pl.BlockSpec((None, tr, tc), lambda i, k, where_ref: (where_ref[1], i, k))]
    args = [where, arrived, arrived, arrived, arrived, own]
    aliases = {}
    if prev is not None:
        in_specs.append(_ANY)
        args.append(prev)
        aliases = {len(args) - 1: 0}
    grid_spec = pltpu.PrefetchScalarGridSpec(
        num_scalar_prefetch=1, grid=(R // tr, C // tc), in_specs=in_specs,
        out_specs=pl.BlockSpec((None, None, tr, tc), lambda i, k, where_ref: (layer, where_ref[0], i, k)))
    return pl.pallas_call(body, name=name, grid_spec=grid_spec, out_shape=SDS((n_layers, 2, R, C), F32),
                          input_output_aliases=aliases,
                          compiler_params=_params("arbitrary", "arbitrary"))(*args)


def adamw(w, g, m, v, name):
    if w.ndim == 2:
        R, C = w.shape
        tr, _ = _tile_2d(R, C, 7)
        grid, blk = (R // tr,), pl.BlockSpec((tr, C), lambda i: (i, 0))
    else:
        N, r, C = w.shape
        tn = max(t for t in range(1, N + 1) if N % t == 0 and t * r * C * 4 <= 512 * 1024)
        grid, blk = (N // tn,), pl.BlockSpec((tn, r, C), lambda i: (i, 0, 0))

    def body(w_ref, g_ref, m_ref, v_ref, d_ref, nm_ref, nv_ref):
        gv = g_ref[...]
        m_new = ADAM_B1 * m_ref[...] + (1.0 - ADAM_B1) * gv
        v_new = ADAM_B2 * v_ref[...] + (1.0 - ADAM_B2) * (gv * gv)
        m_hat = m_new / (1.0 - ADAM_B1 ** ADAM_STEP)
        v_hat = v_new / (1.0 - ADAM_B2 ** ADAM_STEP)
        d_ref[...] = -ADAM_LR * (m_hat / (jnp.sqrt(v_hat) + ADAM_EPS) + ADAM_WD * w_ref[...])
        nm_ref[...] = m_new
        nv_ref[...] = v_new

    out = SDS(w.shape, F32)
    return pl.pallas_call(body, name=name, grid=grid, in_specs=[blk] * 4, out_specs=[blk] * 3,
                          out_shape=[out, out, out], compiler_params=_params("arbitrary"))(w, g, m, v)


_COMM = pltpu.CompilerParams(has_side_effects=True)


def gather_buffers(shards):
    me_chip = 2 * lax.axis_index("x") + lax.axis_index("y")
    pool = {}
    for name, sh in shards.items():
        L, r, c = sh.shape
        if name in ROW_SHARDED:
            pool[name] = lax.dynamic_update_slice(lax.empty((L, N_CHIPS, r, c), sh.dtype), sh[:, None],
                                                  (0, me_chip, 0, 0))
        else:
            pool[name] = lax.dynamic_update_slice(lax.empty((N_CHIPS, L, r, c), sh.dtype), sh[None],
                                                  (me_chip, 0, 0, 0))
    return pool


def comm_now(pool, stages, name):
    stages = [Hosted(pool, jobs) for jobs in stages]
    names = sorted({m for st in stages for m in st.names})
    n = len(names)

    def body(*refs):
        bufs = dict(zip(names, refs[n:2 * n]))
        sems = refs[2 * n:]
        for i, st in enumerate(stages):
            plan = _hosted_plan(st, bufs, sems[2 * i], sems[2 * i + 1])
            _hosted_start(plan, True)
            _hosted_finish(plan, True)

    sem = pltpu.SemaphoreType.DMA
    scratch = []
    for st in stages:
        scratch += [sem((len(st.jobs), 3)), sem((len(st.jobs), 3))]
    res = pl.pallas_call(
        body, name=name, in_specs=[_ANY] * n, out_specs=[_ANY] * n,
        out_shape=[SDS(pool[m].shape, pool[m].dtype) for m in names],
        scratch_shapes=scratch, input_output_aliases={t: t for t in range(n)},
        compiler_params=_COMM,
    )(*[pool[m] for m in names])
    pool.update(zip(names, res))


def gather_now(pool, units):
    comm_now(pool, [[("ici", name, layer) for name, layer in units],
                    [("fwd", name, layer) for name, layer in units]], "gather_now")


def allgather_chips(buf, name):
    def body(src_ref, out_ref, send_sems, recv_sems, local_sem):
        x, y, c = _position()
        me = 2 * x + y
        mine = pltpu.make_async_copy(src_ref, out_ref.at[me], local_sem)
        mine.start()
        sends = []
        for k, (px, py) in enumerate(_other_chips(x, y)):
            cp = _remote(src_ref, out_ref.at[me], send_sems.at[k], recv_sems.at[k], (px, py, c))
            cp.start()
            sends.append(cp)
        for k, (px, py) in enumerate(_other_chips(x, y)):
            _remote(src_ref, out_ref.at[2 * px + py], send_sems.at[k], recv_sems.at[k], (px, py, c)).wait_recv()
        for cp in sends:
            cp.wait_send()
        mine.wait()

    sem = pltpu.SemaphoreType.DMA
    return pl.pallas_call(
        body, name=name, in_specs=[_ANY], out_specs=_ANY, out_shape=SDS((N_CHIPS,) + buf.shape, buf.dtype),
        scratch_shapes=[sem((3,)), sem((3,)), sem], compiler_params=_COMM,
    )(buf)


BIG = ("w_in", "w_proj_attn", "w_proj_pool", "w_proj_conv", "conv_w", "w_out", "w_gate_up", "w_down")
REPLICATED = ("attn_norm", "b_forget", "b_gate", "pool_w", "pool_scale", "ffn_norm", "final_norm")
ORDER = ("attn_norm", "w_in", "b_forget", "b_gate", "w_proj_attn", "pool_w", "pool_scale", "w_proj_pool",
         "conv_w", "w_proj_conv", "w_out", "ffn_norm", "w_gate_up", "w_down", "final_norm")


def _proj_layout(D):
    lay = {"g": 0, "q": 3 * D}
    lay["k"] = lay["q"] + BRANCH_W
    lay["f"] = lay["k"] + BRANCH_W
    lay["v"] = lay["f"] + F_PAD
    lay["pc"] = lay["v"] + BRANCH_W
    lay["width"] = lay["pc"] + 4 * BRANCH_W
    return lay


_REF = dict(q=0, k=512, v=1024, f=1536, u=1544, cv=2056, cb=2568, cc=3080, g=3592)


def _packed_pieces(D):
    pieces = [(_REF["g"], 3 * D), (_REF["q"], BRANCH_W), (_REF["k"], BRANCH_W), (_REF["f"], HEADS),
              (None, F_PAD - HEADS), (_REF["v"], BRANCH_W)]
    for gi in range(N_GROUPS):
        pieces += [(_REF[name] + gi * GROUP_W, GROUP_W) for name in ("u", "cv", "cb", "cc")]
    return pieces


def _packed_runs(D, cs):
    runs, at = [], 0
    for start, n in _packed_pieces(D):
        if start is None:
            runs.append((at, None, 0, n))
            at += n
        while start is not None and n:
            chip, off = divmod(start, cs)
            take = min(n, cs - off)
            runs.append((at, chip, off, take))
            at, start, n = at + take, start + take, n - take
    return runs


def pack_w_in(shards, layer):
    _, _, cs, D = shards.shape
    runs = _packed_runs(D, cs)
    width = runs[-1][0] + runs[-1][3]
    tc = _tile(D, (256, 128))

    def body(s_ref, o_ref):
        for dst, chip, off, rows in runs:
            if chip is None:
                o_ref[dst:dst + rows, :] = jnp.zeros((rows, tc), s_ref.dtype)
            else:
                o_ref[dst:dst + rows, :] = s_ref[chip, off:off + rows, :]

    return pl.pallas_call(
        body, name="pack_w_in", grid=(D // tc,),
        in_specs=[pl.BlockSpec((N_CHIPS, None, cs, tc), lambda j: (0, layer, 0, j))],
        out_specs=pl.BlockSpec((width, tc), lambda j: (0, j)),
        out_shape=SDS((width, D), shards.dtype), compiler_params=_params("arbitrary"),
    )(shards)


def unpack_w_in(p, cs):
    width, D = p.shape
    half = cs // 2
    runs = []
    for src, chip, off, rows in _packed_runs(D, cs):
        while chip is not None and rows:
            h, at = divmod(off, half)
            take = min(rows, half - at)
            runs.append((src, chip, h, at, take))
            src, off, rows = src + take, off + take, rows - take
    tc = _tile(D, (256, 128))

    def body(p_ref, o_ref):
        for src, chip, h, at, rows in runs:
            o_ref[chip, h, at:at + rows, :] = p_ref[src:src + rows, :]

    return pl.pallas_call(
        body, name="unpack_w_in", grid=(D // tc,),
        in_specs=[pl.BlockSpec((width, tc), lambda j: (0, j))],
        out_specs=pl.BlockSpec((N_CHIPS, 2, half, tc), lambda j: (0, 0, 0, j)),
        out_shape=SDS((N_CHIPS, 2, half, D), p.dtype), compiler_params=_params("arbitrary"),
    )(p)


def _split_flat(vec, shapes):
    out, at = [], 0
    for shp in shapes:
        n = int(np.prod(shp))
        out.append(vec[at:at + n].reshape(shp))
        at += n
    return out


def kernel(x, attn_norm, w_in, b_forget, b_gate, w_proj_attn, pool_w, pool_scale, w_proj_pool, conv_w, w_proj_conv, w_out, ffn_norm, w_gate_up, w_down, final_norm, loss_target, m_attn_norm, m_w_in, m_b_forget, m_b_gate, m_w_proj_attn, m_pool_w, m_pool_scale, m_w_proj_pool, m_conv_w, m_w_proj_conv, m_w_out, m_ffn_norm, m_w_gate_up, m_w_down, m_final_norm, v_attn_norm, v_w_in, v_b_forget, v_b_gate, v_w_proj_attn, v_pool_w, v_pool_scale, v_w_proj_pool, v_conv_w, v_w_proj_conv, v_w_out, v_ffn_norm, v_w_gate_up, v_w_down, v_final_norm):
    weights = dict(attn_norm=attn_norm, w_in=w_in, b_forget=b_forget, b_gate=b_gate, w_proj_attn=w_proj_attn,
                   pool_w=pool_w, pool_scale=pool_scale, w_proj_pool=w_proj_pool, conv_w=conv_w,
                   w_proj_conv=w_proj_conv, w_out=w_out, ffn_norm=ffn_norm, w_gate_up=w_gate_up, w_down=w_down,
                   final_norm=final_norm)
    mom_m = dict(attn_norm=m_attn_norm, w_in=m_w_in, b_forget=m_b_forget, b_gate=m_b_gate, w_proj_attn=m_w_proj_attn,
                 pool_w=m_pool_w, pool_scale=m_pool_scale, w_proj_pool=m_w_proj_pool, conv_w=m_conv_w,
                 w_proj_conv=m_w_proj_conv, w_out=m_w_out, ffn_norm=m_ffn_norm, w_gate_up=m_w_gate_up,
                 w_down=m_w_down, final_norm=m_final_norm)
    mom_v = dict(attn_norm=v_attn_norm, w_in=v_w_in, b_forget=v_b_forget, b_gate=v_b_gate, w_proj_attn=v_w_proj_attn,
                 pool_w=v_pool_w, pool_scale=v_pool_scale, w_proj_pool=v_w_proj_pool, conv_w=v_conv_w,
                 w_proj_conv=v_w_proj_conv, w_out=v_w_out, ffn_norm=v_ffn_norm, w_gate_up=v_w_gate_up,
                 w_down=v_w_down, final_norm=v_final_norm)

    Bl, S, D = x.shape
    T = Bl * S
    L = w_in.shape[0]
    F = w_down.shape[1] * N_CHIPS
    lay = _proj_layout(D)
    cst = _placement_constants()
    assert L == N_LAYERS and S % ATTN_BLOCK == 0 and F % (2 * LANES) == 0 and D % BRANCH_W == 0
    assert w_in.shape[2] * N_CHIPS == _REF["g"] + 3 * D and conv_w.shape[2] == GROUP_W

    send = {n: weights[n].astype(BF16) for n in BIG}
    send["conv_w"] = conv_w
    me_chip = 2 * lax.axis_index("x") + lax.axis_index("y")
    send["w_in"] = w_in.transpose(0, 2, 1).astype(BF16)
    pool = gather_buffers(send)
    gather_now(pool, [("w_in", 0)])
    rest = ("w_out", "w_proj_attn", "w_proj_pool", "w_gate_up", "w_proj_conv", "conv_w")
    late = ("w_out", "w_proj_attn", "w_proj_pool", "w_proj_conv", "conv_w")
    jobs = lambda kind, names, layer: [(kind, n, layer) for n in names]
    carried = {
        ("in_proj", 0): jobs("ici", rest, 0),
        ("attn_prep", 0): jobs("fwd", late, 0),
        ("attn_fwd", 0): jobs("fwd", ("w_gate_up",), 0) + jobs("ici", ("w_in",), 1) + jobs("ici", ("w_down",), 0),
        ("poolconv_fwd", 0): jobs("fwd", ("w_in",), 1) + jobs("fwd", ("w_down",), 0),
        ("mix_fwd", 0): jobs("ici", ("w_down",), 1),
        ("gate_up_proj", 0): jobs("ici", late, 1) + jobs("fwd", ("w_down",), 1),
        ("ffn_down_fwd", 0): jobs("ici", ("w_gate_up",), 1),
        ("in_proj", 1): jobs("fwd", ("w_gate_up",) + late, 1),
    }
    carry = lambda call, layer: Hosted(pool, carried[call, layer]) if (call, layer) in carried else None
    w_down_f = lambda: pool["w_down"].reshape(L, F, D)
    pool_w_b = pool_w.astype(BF16)
    an3, fn3 = attn_norm.reshape(L, 1, D), ffn_norm.reshape(L, 1, D)
    bg3, ps3 = b_gate.reshape(L, 1, 3 * D), pool_scale.reshape(L, 1, BRANCH_W)
    bf3 = jnp.pad(b_forget, ((0, 0), (0, LANES - HEADS))).reshape(L, 1, LANES)

    xs = x.reshape(T, D)
    saved = []
    w_in_p = []
    for l in range(L):
        w_in_p.append(pack_w_in(pool["w_in"], l))
        proj, h = norm_matmul(xs, an3, w_in_p[l], l, "rows", "in_proj", carry("in_proj", l))
        proj3 = proj.reshape(Bl, S, lay["width"])
        qa, ka = attn_prep(proj3, bf3, l, cst, lay, carry("attn_prep", l))
        ao, lse = attn_fwd(qa, ka, proj3, lay, carry("attn_fwd", l))
        po, co = poolconv_fwd(proj3, pool_w_b, ps3, pool["conv_w"], l, lay, carry("poolconv_fwd", l))
        ao2, po2, co2 = (a.reshape(T, BRANCH_W) for a in (ao, po, co))
        x1, ys, mixed = mix_fwd(ao2, po2, co2, proj, bg3, pool["w_proj_attn"], pool["w_proj_pool"],
                                pool["w_proj_conv"], pool["w_out"], l, xs, carry("mix_fwd", l))
        ab, h2 = norm_matmul(x1, fn3, pool["w_gate_up"], l, "by_shard", "gate_up_proj", carry("gate_up_proj", l))
        x2, s_act = ffn_down_fwd(ab, w_down_f(), l, x1, carry("ffn_down_fwd", l))
        saved.append(dict(x=xs, proj=proj, proj3=proj3, h=h, qa=qa, ka=ka, ao=ao, lse=lse, ao2=ao2, po2=po2,
                          co2=co2, ys=ys, mixed=mixed, x1=x1, ab=ab, h2=h2, s=s_act))
        xs = x2
    w_gu, w_o, conv_w_g = pool["w_gate_up"], pool["w_out"], pool["conv_w"]
    wpa, wpp, wpc = pool["w_proj_attn"], pool["w_proj_pool"], pool["w_proj_conv"]
    w_down_f = w_down_f()

    loss_row, dx, dxb, g_final = loss_head(xs, final_norm.reshape(1, D), loss_target.reshape(T, D))
    loss = lax.psum(loss_row[0, 0], AXES)

    reduced_names = tuple(n for n in BIG if n != "conv_w")
    early_names = tuple(n for n in reduced_names if n != "w_in")
    proj_names = ("w_out", "w_proj_attn", "w_proj_pool", "w_proj_conv")
    first_names = ("w_in", "w_gate_up", "w_down")
    where = jnp.stack([lax.axis_index("c"), me_chip]).astype(jnp.int32)
    rs = {}

    def reduce_begin(layer, grads):
        for n, g in grads.items():
            g5 = g.reshape((1, N_CHIPS, 2, -1) + g.shape[-1:])
            rs["g%d:%s" % (layer, n)] = g5
            for role in "ra":
                rs["%s%d:%s" % (role, layer, n)] = lax.empty((N_CHIPS,) + g5.shape[3:], BF16)

    swap_jobs = lambda layer, names: [("swap", "g%d:%s" % (layer, n), "r%d:%s" % (layer, n), 0) for n in names]
    xchg_jobs = lambda layer, names: [("xchg", "s%d:%s" % (layer, n), "a%d:%s" % (layer, n)) for n in names]
    join_jobs = lambda layer, names: [("join", "o:" + n, layer) for n in names]

    def pair_sums(layer, names):
        for n in names:
            rs["s%d:%s" % (layer, n)] = add_pair(rs["g%d:%s" % (layer, n)], 0, where, rs["r%d:%s" % (layer, n)],
                                                 "add_pair_" + n)

    def chip_sums(layer, names, slot, n_slots):
        for n in names:
            rs["o:" + n] = add_chips(rs["a%d:%s" % (layer, n)], rs["s%d:%s" % (layer, n)], slot, where, n_slots,
                                     rs.get("o:" + n), "add_chips_" + n)

    small = {n: [None] * L for n in REPLICATED if n != "final_norm"}
    g_conv = [None] * L
    to3 = lambda a: a.reshape(Bl, S, -1)
    for l in reversed(range(L)):
        sv = saved[l]
        behind = (lambda jobs: Hosted(rs, jobs)) if l == 0 else (lambda jobs: None)
        grads = {}
        da, db = ffn_down_bwd(dxb, w_down_f, l, sv["ab"], behind(swap_jobs(1, reduced_names)))
        if l == 0:
            pair_sums(1, reduced_names)
        grads["w_down"] = matmul_tn(sv["s"], [dxb], "grad_w_down", hosted=behind(xchg_jobs(1, ("w_down",))))
        grads["w_gate_up"] = matmul_tn(sv["h2"], [da, db], "grad_w_gate_up", by_dest=True, tn=2 * F // N_CHIPS,
                                       tk=_tile(T, (1024, 512, 256)), hosted=behind(xchg_jobs(1, ("w_gate_up",))))
        dx1, dx1b, g_fn = matmul_nt_normbwd([da, db], w_gu, l, "by_shard", sv["x1"], fn3, dx, "gate_up_bwd",
                                            behind(xchg_jobs(1, ("w_in",))))
        small["ffn_norm"][l] = g_fn[0]
        if l == 0:
            chip_sums(1, first_names, 1, L)
        dys, dproj, dao, dpo, dco, g_bg = mix_bwd(dx1b, w_o, sv["proj"], bg3, sv["ys"], wpa, wpp, wpc, l,
                                                  lay["width"],
                                                  behind(xchg_jobs(1, proj_names) + join_jobs(1, first_names)))
        if l == 0:
            chip_sums(1, proj_names, 1, L)
        small["b_gate"][l] = g_bg[0]
        grads["w_out"] = matmul_tn(sv["mixed"], [dx1b], "grad_w_out")
        for n, (name, br) in enumerate((("w_proj_attn", sv["ao2"]), ("w_proj_pool", sv["po2"]),
                                        ("w_proj_conv", sv["co2"]))):
            grads[name] = matmul_tn(br, [dys], "grad_" + name, b_col0=n * D, n_cols=D, by_dest=True,
                                    tn=D // N_CHIPS)
        if l == 0:
            reduce_begin(0, grads)
        dqa, dka, dproj3 = attn_bwd(sv["qa"], sv["ka"], sv["proj3"], to3(dao), sv["ao"], sv["lse"], to3(dproj), lay,
                                    behind(swap_jobs(0, early_names) + join_jobs(1, proj_names)))
        if l == 0:
            pair_sums(0, early_names)
        dproj3, g_bf = attn_post(dqa, dka, sv["proj3"], bf3, l, dproj3, cst, lay, behind(xchg_jobs(
            0, ("w_out", "w_proj_attn", "w_proj_pool", "w_proj_conv"))))
        small["b_forget"][l] = g_bf[0, :HEADS]
        dproj3, g_pw, g_ps, g_conv[l] = poolconv_bwd(sv["proj3"], to3(dpo), to3(dco), pool_w_b, ps3, conv_w_g, l,
                                                     dproj3, lay, behind(xchg_jobs(0, ("w_down",))))
        small["pool_w"][l], small["pool_scale"][l] = g_pw, g_ps[0]
        dproj = dproj3.reshape(T, lay["width"])
        g_w_in = unpack_w_in(matmul_tn(dproj, [sv["h"]], "grad_w_in", hosted=behind(xchg_jobs(
            0, ("w_gate_up",)))), w_in.shape[2])
        if l:
            reduce_begin(l, {**grads, "w_in": g_w_in})
        else:
            reduce_begin(0, {"w_in": g_w_in})
            comm_now(rs, [swap_jobs(0, ("w_in",))], "swap_w_in_halves")
            pair_sums(0, ("w_in",))
        dx, dxb, g_an = matmul_nt_normbwd([dproj], w_in_p[l], l, "rows", sv["x"], an3, dx1, "in_proj_bwd",
                                          behind(xchg_jobs(0, ("w_in",))))
        small["attn_norm"][l] = g_an[0]
    grad_x = dx.reshape(Bl, S, D)

    small_shapes = [weights[n].shape for n in REPLICATED] + [(L, N_CHIPS) + conv_w.shape[1:]]
    small_vec = jnp.concatenate([jnp.stack(small[n]).reshape(-1) for n in REPLICATED[:-1]]
                                + [g_final[0], jnp.stack(g_conv).reshape(-1)])
    n_small = small_vec.shape[0]
    small_vec = jnp.pad(small_vec, (0, -n_small % (2 * N_CHIPS * 16 * LANES))).astype(BF16)
    rs["g0:small"] = small_vec.reshape(1, N_CHIPS, 2, -1, LANES)
    for role in "ra":
        rs[role + "0:small"] = lax.empty((N_CHIPS,) + rs["g0:small"].shape[3:], BF16)
    last = ("small",)
    comm_now(rs, [swap_jobs(0, last)], "swap_grad_halves")
    pair_sums(0, last)
    comm_now(rs, [xchg_jobs(0, last)], "exchange_grad_chips")
    chip_sums(0, reduced_names, 0, L)
    chip_sums(0, ("small",), 0, 1)
    comm_now(rs, [join_jobs(0, reduced_names + ("small",))], "join_grad_halves")
    shard_grads = {n: rs["o:" + n].reshape((L, -1) + rs["o:" + n].shape[-1:]) for n in reduced_names}
    small_all = allgather_chips(rs["o:small"].reshape(-1, LANES), "allgather_small_grads").reshape(-1)[:n_small]
    *rep_list, conv_all = _split_flat(small_all, small_shapes)
    rep_grads = dict(zip(REPLICATED, rep_list))
    shard_grads["conv_w"] = lax.dynamic_index_in_dim(conv_all, me_chip, 1, keepdims=False)

    delta, new_m, new_v = {}, {}, {}
    for n in BIG:
        shp = weights[n].shape
        if n == "w_in":
            view, back = (lambda a: a.transpose(2, 0, 1)), (lambda a: a.transpose(1, 2, 0))
            g = shard_grads[n].transpose(1, 0, 2)
        else:
            view, back = (lambda a: a.reshape(-1, shp[-1])), (lambda a: a.reshape(shp))
            g = view(shard_grads[n])
        d, nm, nv = adamw(view(weights[n]), g, view(mom_m[n]), view(mom_v[n]), "adamw_" + n)
        delta[n], new_m[n], new_v[n], shard_grads[n] = back(d), back(nm), back(nv), back(g)

    def rows(d):
        vec = jnp.concatenate([d[n].reshape(-1) for n in REPLICATED])
        return jnp.pad(vec, (0, -vec.shape[0] % (8 * LANES))).reshape(-1, LANES)

    outs = adamw(rows(weights), rows(rep_grads), rows(mom_m), rows(mom_v), "adamw_replicated")
    for res, o in zip((delta, new_m, new_v), outs):
        res.update(zip(REPLICATED, _split_flat(o.reshape(-1), small_shapes[:len(REPLICATED)])))
    all_grads = {**shard_grads, **rep_grads}

    return (loss, grad_x, *[all_grads[n] for n in ORDER], *[delta[n] for n in ORDER],
            *[new_m[n] for n in ORDER], *[new_v[n] for n in ORDER])
```

```python
import numpy as np
import jax
import jax.numpy as jnp
from jax import lax
from jax.experimental import pallas as pl
from jax.experimental.pallas import tpu as pltpu

F32, BF16 = jnp.float32, jnp.bfloat16
SDS = jax.ShapeDtypeStruct
MESH = pl.DeviceIdType.MESH
AXES = ("x", "y", "c")
N_CHIPS = 4
N_LAYERS = 2
LANES = 128
VMEM_LIMIT = 48 * 1024 * 1024

HEADS, HEAD_DIM = 8, 64
HEAD_PAD = 128
BRANCH_W = 512
GROUP_W = 128
N_GROUPS = BRANCH_W // GROUP_W
POOL_WINDOWS = (2, 4, 8, 16)
F_PAD = 512
ATTN_BLOCK = 256
RMS_EPS = 1e-6
NEG_INF = -1e30
ADAM_LR, ADAM_B1, ADAM_B2, ADAM_EPS, ADAM_WD, ADAM_STEP = 0.001, 0.9, 0.999, 1e-08, 0.01, 10

NT = (((1,), (1,)), ((), ()))
TN = (((0,), (0,)), ((), ()))
_ANY = pl.BlockSpec(memory_space=pl.ANY)


def _tile(n, prefs):
    for p in prefs:
        if n % p == 0:
            return p
    raise ValueError(f"no tile of {prefs} divides {n}")


def _params(*sem):
    return pltpu.CompilerParams(dimension_semantics=sem, vmem_limit_bytes=VMEM_LIMIT)


def _sigmoid(z):
    return 0.5 * jnp.tanh(0.5 * z) + 0.5


def _split3(x):
    h1 = x.astype(BF16)
    r1 = x - h1.astype(F32)
    h2 = r1.astype(BF16)
    h3 = (r1 - h2.astype(F32)).astype(BF16)
    return h1, h2, h3


def _position():
    return lax.axis_index("x"), lax.axis_index("y"), lax.axis_index("c")


def _other_chips(x, y):
    return [(1 - x, y), (x, 1 - y), (1 - x, 1 - y)]


def _remote(src, dst, send_sem, recv_sem, device):
    return pltpu.make_async_remote_copy(src_ref=src, dst_ref=dst, send_sem=send_sem, recv_sem=recv_sem,
                                        device_id=device, device_id_type=MESH)


ROW_SHARDED = ("w_out", "w_down")
FETCHER = dict(w_in=0, w_out=0, w_proj_attn=0, w_proj_pool=0, w_gate_up=1, w_down=1, w_proj_conv=1, conv_w=1)


class Hosted:
    def __init__(self, pool, jobs):
        self.pool, self.jobs = pool, list(jobs)
        names = set()
        for job in self.jobs:
            names.update(job[1:3] if job[0] in ("swap", "xchg") else job[1:2])
        self.names = sorted(names)


def _hosted_plan(hosted, refs, send_sems, recv_sems):
    x, y, c = _position()
    me = 2 * x + y
    others = _other_chips(x, y)
    sibling = (x, y, 1 - c)
    plan = []
    for j, job in enumerate(hosted.jobs):
        kind = job[0]
        sems = lambda k, j=j: (send_sems.at[j, k], recv_sems.at[j, k])
        if kind in ("ici", "fwd"):
            _, name, layer = job
            ref = refs[name]
            win = (lambda chip, ref=ref, layer=layer: ref.at[layer, chip]) if name in ROW_SHARDED else (
                lambda chip, ref=ref, layer=layer: ref.at[chip, layer])
            mine = c == FETCHER[name]
            if kind == "ici":
                sends = [_remote(win(me), win(me), *sems(k), (px, py, c)) for k, (px, py) in enumerate(others)]
                arrivals = [_remote(win(2 * px + py), win(2 * px + py), *sems(k), (px, py, c))
                            for k, (px, py) in enumerate(others)]
                plan.append((mine, sends, arrivals, []))
            else:
                sends = [_remote(win(2 * px + py), win(2 * px + py), *sems(k), sibling)
                         for k, (px, py) in enumerate(others)]
                plan.append((mine, sends, [], sends))
        elif kind == "swap":
            _, src, dst, layer = job
            cp = _remote(refs[src].at[layer, :, 1 - c], refs[dst], *sems(0), sibling)
            plan.append((True, [cp], [cp], []))
        elif kind == "xchg":
            _, src, dst = job
            sends = [_remote(refs[src].at[2 * px + py], refs[dst].at[me], *sems(k), (px, py, c))
                     for k, (px, py) in enumerate(others)]
            arrivals = [_remote(refs[src].at[me], refs[dst].at[2 * px + py], *sems(k), (px, py, c))
                        for k, (px, py) in enumerate(others)]
            plan.append((True, sends, arrivals, []))
        else:
            _, name, layer = job
            ref = refs[name]
            cp = _remote(ref.at[layer, c], ref.at[layer, c], *sems(0), sibling)
            arrival = _remote(ref.at[layer, c], ref.at[layer, 1 - c], *sems(0), sibling)
            plan.append((True, [cp], [arrival], []))
    return plan


def _hosted_start(plan, now):
    for mine, sends, _, _ in plan:
        @pl.when(now & mine)
        def _(sends=sends):
            for cp in sends:
                cp.start()


def _hosted_finish(plan, now):
    for mine, sends, arrivals, sibling_arrivals in plan:
        @pl.when(now & mine)
        def _(sends=sends, arrivals=arrivals):
            for cp in arrivals:
                cp.wait_recv()
            for cp in sends:
                cp.wait_send()

        if sibling_arrivals:
            @pl.when(now & jnp.logical_not(mine))
            def _(sibling_arrivals=sibling_arrivals):
                for cp in sibling_arrivals:
                    cp.wait_recv()


def _pcall(body, hosted, *, name, grid, in_specs, out_specs, out_shape, semantics, scratch_shapes=(), aliases=None):
    aliases = dict(aliases or {})
    if hosted is None or not hosted.jobs:
        return pl.pallas_call(body, name=name, grid=grid, in_specs=in_specs, out_specs=out_specs,
                              out_shape=out_shape, scratch_shapes=list(scratch_shapes),
                              input_output_aliases=aliases, compiler_params=_params(*semantics))
    single = not isinstance(out_shape, (list, tuple))
    out_specs_l = [out_specs] if single else list(out_specs)
    out_shape_l = [out_shape] if single else list(out_shape)
    n_in, n_out, n_buf, n_job = len(in_specs), len(out_specs_l), len(hosted.names), len(hosted.jobs)

    def carrying(*refs):
        ins, outs = refs[:n_in], refs[n_in + n_buf:n_in + n_buf + n_out]
        bufs = refs[n_in + n_buf + n_out:n_in + 2 * n_buf + n_out]
        rest = refs[n_in + 2 * n_buf + n_out:]
        scratch, send_sems, recv_sems = rest[:-2], rest[-2], rest[-1]
        first, last = True, True
        for axis, size in enumerate(grid):
            first = first & (pl.program_id(axis) == 0)
            last = last & (pl.program_id(axis) == size - 1)
        plan = _hosted_plan(hosted, dict(zip(hosted.names, bufs)), send_sems, recv_sems)
        _hosted_start(plan, first)
        body(*ins, *outs, *scratch)
        _hosted_finish(plan, last)

    def run(*args):
        bufs = [hosted.pool[n] for n in hosted.names]
        sem = pltpu.SemaphoreType.DMA
        res = pl.pallas_call(
            carrying, name=name, grid=grid, in_specs=list(in_specs) + [_ANY] * n_buf,
            out_specs=out_specs_l + [_ANY] * n_buf,
            out_shape=out_shape_l + [SDS(b.shape, b.dtype) for b in bufs],
            scratch_shapes=list(scratch_shapes) + [sem((n_job, 3)), sem((n_job, 3))],
            input_output_aliases={**aliases, **{n_in + i: n_out + i for i in range(n_buf)}},
            compiler_params=pltpu.CompilerParams(dimension_semantics=semantics, vmem_limit_bytes=VMEM_LIMIT,
                                                 has_side_effects=True),
        )(*args, *bufs)
        hosted.pool.update(zip(hosted.names, res[n_out:]))
        return res[0] if single else res[:n_out]

    return run


def _dot(a, b):
    return jnp.dot(a, b, preferred_element_type=F32)


def _dot_nt(a, b):
    return lax.dot_general(a, b, NT, preferred_element_type=F32)


def _dot_tn(a, b):
    return lax.dot_general(a, b, TN, preferred_element_type=F32)


def norm_matmul(x, gain, w, layer, kind, name, hosted=None):
    T, D = x.shape
    if kind == "by_shard":
        tn = w.shape[3]
        N = N_CHIPS * tn
        w_spec = pl.BlockSpec((None, None, D, tn), lambda i, j: (j, layer, 0, 0))
        mm = _dot
    else:
        N = w.shape[0]
        tn = _tile(N, (1024, 512, 256, 128))
        w_spec = pl.BlockSpec((tn, D), lambda i, j: (j, 0))
        mm = _dot_nt
    tm = _tile(T, (1024, 512, 256, 128))

    def body(x_ref, g_ref, w_ref, y_ref, h_ref):
        @pl.when(pl.program_id(1) == 0)
        def _():
            xf = x_ref[...]
            r = lax.rsqrt(jnp.mean(xf * xf, axis=-1, keepdims=True) + RMS_EPS)
            h_ref[...] = ((xf * r) * g_ref[...]).astype(BF16)

        y_ref[...] = mm(h_ref[...], w_ref[...]).astype(BF16)

    return _pcall(
        body, hosted, name=name, grid=(T // tm, N // tn),
        in_specs=[pl.BlockSpec((tm, D), lambda i, j: (i, 0)),
                  pl.BlockSpec((None, 1, D), lambda i, j: (layer, 0, 0)),
                  w_spec],
        out_specs=[pl.BlockSpec((tm, tn), lambda i, j: (i, j)),
                   pl.BlockSpec((tm, D), lambda i, j: (i, 0))],
        out_shape=[SDS((T, N), BF16), SDS((T, D), BF16)],
        semantics=("arbitrary", "arbitrary"),
    )(x, gain, w)


def matmul_nt_normbwd(dys, w, layer, kind, x, gain, dres, name, hosted=None):
    T, D = x.shape
    width = dys[0].shape[1]
    if kind == "by_shard":
        tk = w.shape[3]
        w_spec = pl.BlockSpec((None, None, D, tk), lambda i, k: (k, layer, 0, 0))
        mm = _dot_nt
    else:
        tk = _tile(width, (3584, 1024, 512, 256, 128))
        w_spec = pl.BlockSpec((tk, D), lambda i, k: (k, 0))
        mm = _dot
    per = width // tk
    nk = per * len(dys)
    tm = _tile(T, (512, 256, 128))
    n_dy = len(dys)

    def dy_spec(p):
        return pl.BlockSpec((tm, tk), lambda i, k: (i, jnp.clip(k - p * per, 0, per - 1)))

    def body(*refs):
        dy_refs = refs[:n_dy]
        w_ref, x_ref, g_ref, dres_ref, dx_ref, dxb_ref, dg_ref, acc_ref = refs[n_dy:]
        i, k = pl.program_id(0), pl.program_id(1)

        @pl.when(k == 0)
        def _():
            acc_ref[...] = jnp.zeros_like(acc_ref)

        for p in range(n_dy):
            @pl.when((k >= p * per) & (k < (p + 1) * per))
            def _(p=p):
                acc_ref[...] += mm(dy_refs[p][...], w_ref[...])

        @pl.when(k == nk - 1)
        def _():
            xf = x_ref[...]
            r = lax.rsqrt(jnp.mean(xf * xf, axis=-1, keepdims=True) + RMS_EPS)
            xhat = xf * r
            dh = acc_ref[...]
            dhg = dh * g_ref[...]
            dx = dres_ref[...] + r * (dhg - xhat * jnp.mean(dhg * xhat, axis=-1, keepdims=True))
            dx_ref[...] = dx
            dxb_ref[...] = dx.astype(BF16)
            part = jnp.sum(dh * xhat, axis=0, keepdims=True)

            @pl.when(i == 0)
            def _():
                dg_ref[...] = part

            @pl.when(i > 0)
            def _():
                dg_ref[...] += part

    row = pl.BlockSpec((tm, D), lambda i, k: (i, 0))
    return _pcall(
        body, hosted, name=name, grid=(T // tm, nk),
        in_specs=[dy_spec(p) for p in range(n_dy)] + [
            w_spec, row, pl.BlockSpec((None, 1, D), lambda i, k: (layer, 0, 0)), row],
        out_specs=[row, row, pl.BlockSpec((1, D), lambda i, k: (0, 0))],
        out_shape=[SDS((T, D), F32), SDS((T, D), BF16), SDS((1, D), F32)],
        scratch_shapes=[pltpu.VMEM((tm, D), F32)],
        semantics=("arbitrary", "arbitrary"),
    )(*dys, w, x, gain, dres)


def matmul_tn(a, bs, name, b_col0=0, n_cols=None, by_dest=False, tn=None, tk=None, hosted=None):
    T, M = a.shape
    width = bs[0].shape[1]
    N = n_cols if n_cols else width * len(bs)
    tm = _tile(M, (1024, 512, 256, 128))
    tn = tn or _tile(N, (512, 256, 128))
    tk = tk or _tile(T, (4096, 2048, 1024, 512, 256))
    assert b_col0 % tn == 0 and width % tn == 0
    j0, per, nk, n_b = b_col0 // tn, width // tn, T // tk, len(bs)

    def b_spec(p):
        return pl.BlockSpec((tk, tn), lambda i, j, k: (k, jnp.clip(j0 + j - p * per, 0, per - 1)))

    def body(*refs):
        a_ref, b_refs = refs[0], refs[1:1 + n_b]
        o_ref, acc_ref = refs[-2], refs[-1]
        j, k = pl.program_id(1), pl.program_id(2)

        @pl.when(k == 0)
        def _():
            acc_ref[...] = jnp.zeros_like(acc_ref)

        for p in range(n_b):
            @pl.when((j0 + j >= p * per) & (j0 + j < (p + 1) * per))
            def _(p=p):
                acc_ref[...] += _dot_tn(a_ref[...], b_refs[p][...])

        @pl.when(k == nk - 1)
        def _():
            o_ref[...] = acc_ref[...].astype(BF16)

    if by_dest:
        cs = N // N_CHIPS
        npd = cs // tn
        out_shape = SDS((N_CHIPS, M, cs), BF16)
        out_spec = pl.BlockSpec((None, tm, tn), lambda i, j, k: (j // npd, i, j % npd))
    else:
        out_shape = SDS((M, N), BF16)
        out_spec = pl.BlockSpec((tm, tn), lambda i, j, k: (i, j))
    return _pcall(
        body, hosted, name=name, grid=(M // tm, N // tn, nk),
        in_specs=[pl.BlockSpec((tk, tm), lambda i, j, k: (k, i))] + [b_spec(p) for p in range(n_b)],
        out_specs=out_spec, out_shape=out_shape,
        scratch_shapes=[pltpu.VMEM((tm, tn), F32)],
        semantics=("arbitrary", "arbitrary", "arbitrary"),
    )(a, *bs)


def ffn_down_fwd(ab, w_down, layer, x1, hosted=None):
    T, D = x1.shape
    F = w_down.shape[1]
    tm = _tile(T, (512, 256, 128))
    tk = F // 2
    nk = F // tk

    def body(a_ref, b_ref, w_ref, x_ref, x2_ref, s_ref, acc_ref):
        k = pl.program_id(1)

        @pl.when(k == 0)
        def _():
            acc_ref[...] = x_ref[...]

        a = a_ref[...].astype(F32)
        s = (a * _sigmoid(a) * b_ref[...].astype(F32)).astype(BF16)
        s_ref[...] = s
        acc_ref[...] += _dot(s, w_ref[...])

        @pl.when(k == nk - 1)
        def _():
            x2_ref[...] = acc_ref[...]

    return _pcall(
        body, hosted, name="ffn_down_fwd", grid=(T // tm, nk),
        in_specs=[pl.BlockSpec((tm, tk), lambda i, k: (i, k)),
                  pl.BlockSpec((tm, tk), lambda i, k: (i, nk + k)),
                  pl.BlockSpec((None, tk, D), lambda i, k: (layer, k, 0)),
                  pl.BlockSpec((tm, D), lambda i, k: (i, 0))],
        out_specs=[pl.BlockSpec((tm, D), lambda i, k: (i, 0)),
                   pl.BlockSpec((tm, tk), lambda i, k: (i, k))],
        out_shape=[SDS((T, D), F32), SDS((T, F), BF16)],
        scratch_shapes=[pltpu.VMEM((tm, D), F32)],
        semantics=("arbitrary", "arbitrary"),
    )(ab, ab, w_down, x1)


def ffn_down_bwd(dx2b, w_down, layer, ab, hosted=None):
    T, D = dx2b.shape
    F = w_down.shape[1]
    tm = _tile(T, (512, 256, 128))
    tn = F // 2
    nj = F // tn

    def body(dx_ref, w_ref, a_ref, b_ref, da_ref, db_ref):
        ds = _dot_nt(dx_ref[...], w_ref[...])
        a = a_ref[...].astype(F32)
        sg = _sigmoid(a)
        da_ref[...] = (ds * b_ref[...].astype(F32) * (sg * (1.0 + a * (1.0 - sg)))).astype(BF16)
        db_ref[...] = (ds * (a * sg)).astype(BF16)

    blk = pl.BlockSpec((tm, tn), lambda i, j: (i, j))
    return _pcall(
        body, hosted, name="ffn_down_bwd", grid=(T // tm, nj),
        in_specs=[pl.BlockSpec((tm, D), lambda i, j: (i, 0)),
                  pl.BlockSpec((None, tn, D), lambda i, j: (layer, j, 0)),
                  blk, pl.BlockSpec((tm, tn), lambda i, j: (i, nj + j))],
        out_specs=[blk, blk],
        out_shape=[SDS((T, F), BF16), SDS((T, F), BF16)],
        semantics=("arbitrary", "arbitrary"),
    )(dx2b, w_down, ab, ab)


def _mix_specs(tm, D, layer):
    cs = D // N_CHIPS
    row = lambda w: pl.BlockSpec((tm, w), lambda i: (i, 0))
    wp = pl.BlockSpec((N_CHIPS, None, BRANCH_W, cs), lambda i: (0, layer, 0, 0))
    wo = pl.BlockSpec((None, N_CHIPS, cs, D), lambda i: (layer, 0, 0, 0))
    bg = pl.BlockSpec((None, 1, 3 * D), lambda i: (layer, 0, 0))
    return row, wp, wo, bg


def mix_fwd(ao, po, co, proj, b_gate, wpa, wpp, wpc, w_out, layer, x, hosted=None):
    T, D = x.shape
    cs = D // N_CHIPS
    tm = _tile(T, (256, 128))
    row, wp, wo, bg = _mix_specs(tm, D, layer)

    def body(ao_ref, po_ref, co_ref, g_ref, bg_ref, wpa_ref, wpp_ref, wpc_ref, wo_ref, x_ref,
             x1_ref, ys_ref, mixed_ref):
        mixed = jnp.zeros((tm, D), F32)
        for n, (br, wp_ref) in enumerate(((ao_ref, wpa_ref), (po_ref, wpp_ref), (co_ref, wpc_ref))):
            y = jnp.concatenate([_dot(br[...], wp_ref[j]) for j in range(N_CHIPS)], axis=1)
            cols = slice(n * D, (n + 1) * D)
            gate = _sigmoid(g_ref[:, cols].astype(F32) + bg_ref[:, cols])
            ys_ref[:, cols] = y.astype(BF16)
            mixed = mixed + gate * y
        mb = mixed.astype(BF16)
        mixed_ref[...] = mb
        acc = x_ref[...]
        for j in range(N_CHIPS):
            acc = acc + _dot(mb[:, j * cs:(j + 1) * cs], wo_ref[j])
        x1_ref[...] = acc

    return _pcall(
        body, hosted, name="mix_fwd", grid=(T // tm,),
        in_specs=[row(BRANCH_W), row(BRANCH_W), row(BRANCH_W), row(3 * D), bg, wp, wp, wp, wo, row(D)],
        out_specs=[row(D), row(3 * D), row(D)],
        out_shape=[SDS((T, D), F32), SDS((T, 3 * D), BF16), SDS((T, D), BF16)],
        semantics=("arbitrary",),
    )(ao, po, co, proj, b_gate, wpa, wpp, wpc, w_out, x)


def mix_bwd(dx1b, w_out, proj, b_gate, ys, wpa, wpp, wpc, layer, width, hosted=None):
    T, D = dx1b.shape
    cs = D // N_CHIPS
    tm = _tile(T, (256, 128))
    row, wp, wo, bg = _mix_specs(tm, D, layer)

    def body(dx_ref, wo_ref, g_ref, bg_ref, ys_ref, wpa_ref, wpp_ref, wpc_ref,
             dys_ref, dg_ref, dao_ref, dpo_ref, dco_ref, dbg_ref):
        i = pl.program_id(0)
        dx = dx_ref[...]
        dmixed = jnp.concatenate([_dot_nt(dx, wo_ref[j]) for j in range(N_CHIPS)], axis=1)
        for n, (wp_ref, dbr) in enumerate(((wpa_ref, dao_ref), (wpp_ref, dpo_ref), (wpc_ref, dco_ref))):
            cols = slice(n * D, (n + 1) * D)
            gate = _sigmoid(g_ref[:, cols].astype(F32) + bg_ref[:, cols])
            dy = (dmixed * gate).astype(BF16)
            dys_ref[:, cols] = dy
            dgp = dmixed * ys_ref[:, cols].astype(F32) * gate * (1.0 - gate)
            dg_ref[:, cols] = dgp.astype(BF16)
            part = jnp.sum(dgp, axis=0, keepdims=True)

            @pl.when(i == 0)
            def _():
                dbg_ref[:, cols] = part

            @pl.when(i > 0)
            def _():
                dbg_ref[:, cols] += part

            acc = jnp.zeros((tm, BRANCH_W), F32)
            for j in range(N_CHIPS):
                acc = acc + _dot_nt(dy[:, j * cs:(j + 1) * cs], wp_ref[j])
            dbr[...] = acc.astype(BF16)

    return _pcall(
        body, hosted, name="mix_bwd", grid=(T // tm,),
        in_specs=[row(D), wo, row(3 * D), bg, row(3 * D), wp, wp, wp],
        out_specs=[row(3 * D), row(3 * D), row(BRANCH_W), row(BRANCH_W), row(BRANCH_W),
                   pl.BlockSpec((1, 3 * D), lambda i: (0, 0))],
        out_shape=[SDS((T, 3 * D), BF16), SDS((T, width), BF16), SDS((T, BRANCH_W), BF16),
                   SDS((T, BRANCH_W), BF16), SDS((T, BRANCH_W), BF16), SDS((1, 3 * D), F32)],
        semantics=("arbitrary",),
    )(dx1b, w_out, proj, b_gate, ys, wpa, wpp, wpc)


def loss_head(x2, gain, target):
    T, D = x2.shape
    tm = _tile(T, (512, 256, 128))

    def body(x_ref, g_ref, t_ref, loss_ref, dx_ref, dxb_ref, dg_ref):
        i = pl.program_id(0)
        xf = x_ref[...]
        g = g_ref[...]
        r = lax.rsqrt(jnp.mean(xf * xf, axis=-1, keepdims=True) + RMS_EPS)
        xhat = xf * r
        diff = xhat * g - t_ref[...]
        part_loss = 0.5 * jnp.sum(jnp.mean(diff * diff, axis=-1, keepdims=True), axis=0, keepdims=True)
        dy = diff * (1.0 / D)
        dhg = dy * g
        dx = r * (dhg - xhat * jnp.mean(dhg * xhat, axis=-1, keepdims=True))
        dx_ref[...] = dx
        dxb_ref[...] = dx.astype(BF16)
        part_g = jnp.sum(dy * xhat, axis=0, keepdims=True)
        part_l = jnp.broadcast_to(part_loss, (1, LANES))

        @pl.when(i == 0)
        def _():
            dg_ref[...] = part_g
            loss_ref[...] = part_l

        @pl.when(i > 0)
        def _():
            dg_ref[...] += part_g
            loss_ref[...] += part_l

    row = pl.BlockSpec((tm, D), lambda i: (i, 0))
    return pl.pallas_call(
        body, name="loss_head", grid=(T // tm,),
        in_specs=[row, pl.BlockSpec((1, D), lambda i: (0, 0)), row],
        out_specs=[pl.BlockSpec((1, LANES), lambda i: (0, 0)), row, row, pl.BlockSpec((1, D), lambda i: (0, 0))],
        out_shape=[SDS((1, LANES), F32), SDS((T, D), F32), SDS((T, D), BF16), SDS((1, D), F32)],
        compiler_params=_params("arbitrary"),
    )(x2, gain, target)


def _placement_constants():
    w = HEADS * HEAD_PAD
    pq = np.zeros((BRANCH_W, w), np.float32)
    pk = np.zeros((BRANCH_W, w), np.float32)
    pfq = np.zeros((3, LANES, w), np.float32)
    pfk = np.zeros((3, LANES, w), np.float32)
    cq = np.zeros((1, w), np.float32)
    ck = np.zeros((1, w), np.float32)
    eq = np.zeros((w, LANES), np.float32)
    ek = np.zeros((w, LANES), np.float32)
    for h in range(HEADS):
        for d in range(HEAD_DIM):
            pq[h * HEAD_DIM + d, h * HEAD_PAD + d] = HEAD_DIM ** -0.5
            pk[h * HEAD_DIM + d, h * HEAD_PAD + d] = 1.0
        for i in range(3):
            pfq[i, h, h * HEAD_PAD + HEAD_DIM + i] = 1.0
            pfk[i, h, h * HEAD_PAD + HEAD_DIM + 3 + i] = -1.0
            cq[0, h * HEAD_PAD + HEAD_DIM + 3 + i] = 1.0
            ck[0, h * HEAD_PAD + HEAD_DIM + i] = 1.0
        eq[h * HEAD_PAD + HEAD_DIM, h] = 1.0
        ek[h * HEAD_PAD + HEAD_DIM + 3, h] = -1.0
    bf = lambda a: jnp.asarray(a, BF16)
    return dict(pq=bf(pq), pk=bf(pk), pfq=bf(pfq), pfk=bf(pfk), cq=jnp.asarray(cq), ck=jnp.asarray(ck),
                pqkt=bf(np.concatenate([pq.T, pk.T], axis=0)), eq=bf(eq), ek=bf(ek))


def attn_prep(proj3, bf_rows, layer, cst, lay, hosted=None):
    Bl, S, _ = proj3.shape
    ts = ATTN_BLOCK
    w = HEADS * HEAD_PAD

    def body(q_ref, k_ref, f_ref, bf_ref, pq_ref, pk_ref, pfq_ref, pfk_ref, cq_ref, ck_ref,
             qa_ref, ka_ref, carry_ref):
        @pl.when(pl.program_id(1) == 0)
        def _():
            carry_ref[...] = jnp.zeros_like(carry_ref)

        z = f_ref[...].astype(F32) + bf_ref[...]
        logf = jnp.minimum(z, 0.0) - jnp.log(1.0 + jnp.exp(-jnp.abs(z)))
        r = lax.broadcasted_iota(jnp.int32, (ts, ts), 0)
        c = lax.broadcasted_iota(jnp.int32, (ts, ts), 1)
        tri = jnp.where(r >= c, 1.0, 0.0).astype(BF16)
        fcum = carry_ref[...]
        for part in _split3(logf):
            fcum = fcum + _dot(tri, part)
        carry_ref[...] = fcum[ts - 1:ts, :]
        qa = _dot(q_ref[...], pq_ref[...]) + cq_ref[...]
        ka = _dot(k_ref[...], pk_ref[...]) + ck_ref[...]
        for i, part in enumerate(_split3(fcum)):
            qa = qa + _dot(part, pfq_ref[i])
            ka = ka + _dot(part, pfk_ref[i])
        qa_ref[...] = qa.astype(BF16)
        ka_ref[...] = ka.astype(BF16)

    cfull = lambda shape: pl.BlockSpec(shape, lambda b, s: (0,) * len(shape))
    return _pcall(
        body, hosted, name="attn_prep", grid=(Bl, S // ts),
        in_specs=[pl.BlockSpec((None, ts, BRANCH_W), lambda b, s: (b, s, lay["q"] // BRANCH_W)),
                  pl.BlockSpec((None, ts, BRANCH_W), lambda b, s: (b, s, lay["k"] // BRANCH_W)),
                  pl.BlockSpec((None, ts, LANES), lambda b, s: (b, s, lay["f"] // LANES)),
                  pl.BlockSpec((None, 1, LANES), lambda b, s: (layer, 0, 0)),
                  cfull((BRANCH_W, w)), cfull((BRANCH_W, w)),
                  cfull((3, LANES, w)), cfull((3, LANES, w)), cfull((1, w)), cfull((1, w))],
        out_specs=[pl.BlockSpec((None, ts, w), lambda b, s: (b, s, 0)),
                   pl.BlockSpec((None, ts, w), lambda b, s: (b, s, 0))],
        out_shape=[SDS((Bl, S, w), BF16), SDS((Bl, S, w), BF16)],
        scratch_shapes=[pltpu.VMEM((1, LANES), F32)],
        semantics=("arbitrary", "arbitrary"),
    )(proj3, proj3, proj3, bf_rows, cst["pq"], cst["pk"], cst["pfq"], cst["pfk"], cst["cq"], cst["ck"])


def attn_fwd(qa, ka, proj3, lay, hosted=None):
    Bl, S, _ = qa.shape
    tq = ATTN_BLOCK
    nq = S // tq
    pairs = HEADS // 2
    pw = 2 * HEAD_PAD
    vw = 2 * HEAD_DIM

    def body(qa_ref, ka_ref, v_ref, o_ref, lse_ref):
        row = lax.broadcasted_iota(jnp.int32, (tq, tq), 0)
        col = lax.broadcasted_iota(jnp.int32, (tq, tq), 1)
        causal = row <= col
        for i in range(nq):
            nk = (i + 1) * tq
            rows = slice(i * tq, nk)
            o_t = []
            for h in range(2):
                hs = slice(h * HEAD_PAD, (h + 1) * HEAD_PAD)
                st = _dot_nt(ka_ref[0:nk, hs], qa_ref[rows, hs])
                diag = jnp.where(causal, st[nk - tq:], NEG_INF)
                m = jnp.max(diag, axis=0, keepdims=True)
                if i:
                    m = jnp.maximum(m, jnp.max(st[:nk - tq], axis=0, keepdims=True))
                p_diag = jnp.exp(diag - m)
                l = jnp.sum(p_diag, axis=0, keepdims=True)
                if i:
                    p_top = jnp.exp(st[:nk - tq] - m)
                    l = l + jnp.sum(p_top, axis=0, keepdims=True)
                    p = jnp.concatenate([p_top.astype(BF16), p_diag.astype(BF16)], axis=0)
                else:
                    p = p_diag.astype(BF16)
                acc = _dot_tn(v_ref[0:nk, :], p)
                o_t.append(acc[h * HEAD_DIM:(h + 1) * HEAD_DIM, :] / l)
                lse_ref[h:h + 1, rows] = m + jnp.log(l)
            o_ref[rows, :] = jnp.concatenate(o_t, axis=0).T.astype(BF16)

    return _pcall(
        body, hosted, name="attn_fwd", grid=(Bl, pairs),
        in_specs=[pl.BlockSpec((None, S, pw), lambda b, p: (b, 0, p)),
                  pl.BlockSpec((None, S, pw), lambda b, p: (b, 0, p)),
                  pl.BlockSpec((None, S, vw), lambda b, p: (b, 0, lay["v"] // vw + p))],
        out_specs=[pl.BlockSpec((None, S, vw), lambda b, p: (b, 0, p)),
                   pl.BlockSpec((None, None, 2, S), lambda b, p: (b, p, 0, 0))],
        out_shape=[SDS((Bl, S, BRANCH_W), BF16), SDS((Bl, pairs, 2, S), F32)],
        semantics=("arbitrary", "arbitrary"),
    )(qa, ka, proj3)


def attn_bwd(qa, ka, proj3, dao, ao, lse, dproj3, lay, hosted=None):
    Bl, S, _ = qa.shape
    tk = ATTN_BLOCK
    nq = S // tk
    pairs = HEADS // 2
    pw = 2 * HEAD_PAD
    vw = 2 * HEAD_DIM

    def body(qa_ref, ka_ref, v_ref, do_ref, o_ref, lse_ref, _, dqa_ref, dka_ref, dv_ref):
        row = lax.broadcasted_iota(jnp.int32, (tk, tk), 0)
        col = lax.broadcasted_iota(jnp.int32, (tk, tk), 1)
        causal = row <= col
        lane8 = lax.broadcasted_iota(jnp.int32, (8, vw), 1)
        lane_s = lax.broadcasted_iota(jnp.int32, (S, vw), 1)
        lane_k = lax.broadcasted_iota(jnp.int32, (tk, vw), 1)
        doo = do_ref[...].astype(F32) * o_ref[...].astype(F32)
        hi = doo.astype(BF16)
        lo = (doo - hi.astype(F32)).astype(BF16)
        delta, v_head = [], []
        for h in range(2):
            sel = jnp.where((lane8 >= h * HEAD_DIM) & (lane8 < (h + 1) * HEAD_DIM), 1.0, 0.0).astype(BF16)
            delta.append((_dot_nt(sel, hi) + _dot_nt(sel, lo))[0:1, :])
            in_head = (lane_s >= h * HEAD_DIM) & (lane_s < (h + 1) * HEAD_DIM)
            v_head.append(jnp.where(in_head, v_ref[...], jnp.zeros_like(v_ref[...])))
        dqa_ref[...] = jnp.zeros_like(dqa_ref)
        for j in range(nq):
            q0 = j * tk
            krows = slice(q0, q0 + tk)
            do = do_ref[q0:, :]
            dvs = []
            for h in range(2):
                hs = slice(h * HEAD_PAD, (h + 1) * HEAD_PAD)
                k = ka_ref[krows, hs]
                q = qa_ref[q0:, hs]
                st = _dot_nt(k, q)
                p = jnp.exp(st - lse_ref[h:h + 1, q0:])
                p_diag = jnp.where(causal, p[:, :tk], 0.0)
                p = jnp.concatenate([p_diag, p[:, tk:]], axis=1) if j < nq - 1 else p_diag
                dvs.append(_dot(p.astype(BF16), do))
                dpt = _dot_nt(v_head[h][krows, :], do)
                ds = (p * (dpt - delta[h][:, q0:])).astype(BF16)
                dka_ref[krows, hs] = _dot(ds, q)
                dqa_ref[q0:, hs] += _dot_tn(ds, k)
            dv_ref[krows, :] = jnp.where(lane_k < HEAD_DIM, dvs[0], dvs[1]).astype(BF16)

    seq = lambda w, c0=0: pl.BlockSpec((None, S, w), lambda b, p: (b, 0, c0 + p))
    return _pcall(
        body, hosted, name="attn_bwd", grid=(Bl, pairs),
        in_specs=[seq(pw), seq(pw), seq(vw, lay["v"] // vw), seq(vw), seq(vw),
                  pl.BlockSpec((None, None, 2, S), lambda b, p: (b, p, 0, 0)), _ANY],
        out_specs=[seq(pw), seq(pw), seq(vw, lay["v"] // vw)],
        out_shape=[SDS((Bl, S, HEADS * HEAD_PAD), F32), SDS((Bl, S, HEADS * HEAD_PAD), F32),
                   SDS(dproj3.shape, BF16)],
        aliases={6: 2}, semantics=("arbitrary", "arbitrary"),
    )(qa, ka, proj3, dao, ao, lse, dproj3)


def attn_post(dqa, dka, proj3, bf_rows, layer, dproj3, cst, lay, hosted=None):
    Bl, S, w = dqa.shape
    ts = ATTN_BLOCK
    ns = S // ts
    qkf = 2 * BRANCH_W + F_PAD

    def body(dqa_ref, dka_ref, f_ref, bf_ref, pqkt_ref, eq_ref, ek_ref, _, dqkf_ref, dbf_ref, carry_ref):
        b, s = pl.program_id(0), pl.program_id(1)

        @pl.when(s == 0)
        def _():
            carry_ref[...] = jnp.zeros_like(carry_ref)

        dqa_v, dka_v = dqa_ref[...], dka_ref[...]
        qh = dqa_v.astype(BF16)
        kh = dka_v.astype(BF16)
        dqkf_ref[:, :BRANCH_W] = _dot(qh, pqkt_ref[:w, :]).astype(BF16)
        dqkf_ref[:, BRANCH_W:2 * BRANCH_W] = _dot(kh, pqkt_ref[w:, :]).astype(BF16)
        ql = (dqa_v - qh.astype(F32)).astype(BF16)
        kl = (dka_v - kh.astype(F32)).astype(BF16)
        d_f = (_dot(qh, eq_ref[...]) + _dot(ql, eq_ref[...])) + (_dot(kh, ek_ref[...]) + _dot(kl, ek_ref[...]))
        r = lax.broadcasted_iota(jnp.int32, (ts, ts), 0)
        c = lax.broadcasted_iota(jnp.int32, (ts, ts), 1)
        triu = jnp.where(c >= r, 1.0, 0.0).astype(BF16)
        rev = carry_ref[...]
        for part in _split3(d_f):
            rev = rev + _dot(triu, part)
        carry_ref[...] = rev[0:1, :]
        z = f_ref[...].astype(F32) + bf_ref[...]
        lane = lax.broadcasted_iota(jnp.int32, (ts, LANES), 1)
        dfl = jnp.where(lane < HEADS, rev / (1.0 + jnp.exp(z)), 0.0)
        dqkf_ref[:, 2 * BRANCH_W:] = jnp.concatenate(
            [dfl.astype(BF16), jnp.zeros((ts, F_PAD - LANES), BF16)], axis=1)
        part = jnp.sum(dfl, axis=0, keepdims=True)

        @pl.when((b == 0) & (s == 0))
        def _():
            dbf_ref[...] = part

        @pl.when((b > 0) | (s > 0))
        def _():
            dbf_ref[...] += part

    assert lay["q"] % qkf == 0
    cfull = lambda shape: pl.BlockSpec(shape, lambda b, s: (0,) * len(shape))
    rev_blk = lambda wd, c0=0: pl.BlockSpec((None, ts, wd), lambda b, s: (b, ns - 1 - s, c0))
    return _pcall(
        body, hosted, name="attn_post", grid=(Bl, ns),
        in_specs=[rev_blk(w), rev_blk(w), rev_blk(LANES, lay["f"] // LANES),
                  pl.BlockSpec((None, 1, LANES), lambda b, s: (layer, 0, 0)),
                  cfull((2 * w, BRANCH_W)), cfull((w, LANES)), cfull((w, LANES)), _ANY],
        out_specs=[rev_blk(qkf, lay["q"] // qkf), cfull((1, LANES))],
        out_shape=[SDS(dproj3.shape, BF16), SDS((1, LANES), F32)],
        scratch_shapes=[pltpu.VMEM((1, LANES), F32)],
        aliases={7: 0}, semantics=("arbitrary", "arbitrary"),
    )(dqa, dka, proj3, bf_rows, cst["pqkt"], cst["eq"], cst["ek"], dproj3)


def _shift_down(x, k, row):
    return jnp.where(row >= k, pltpu.roll(x, k, axis=0), 0.0)


def _shift_up(x, k, row):
    n = x.shape[0]
    return jnp.where(row < n - k, pltpu.roll(x, n - k, axis=0), 0.0)


def _window_sum(x, g, row, shift):
    s2 = x + shift(x, 1, row)
    s4 = s2 + shift(s2, 2, row)
    s8 = s4 + shift(s4, 4, row)
    s16 = s8 + shift(s8, 8, row)
    return jnp.where(g == 0, s2, jnp.where(g == 1, s4, jnp.where(g == 2, s8, s16)))


def _window_count(g, row):
    wnd = jnp.where(g == 0, 2, jnp.where(g == 1, 4, jnp.where(g == 2, 8, 16)))
    return jnp.minimum(row + 1, wnd).astype(F32)


def _group_columns(ref):
    return [ref[:, n * GROUP_W:(n + 1) * GROUP_W].astype(F32) for n in range(4)]


def poolconv_fwd(proj3, pool_w, pool_scale, conv_w, layer, lay, hosted=None):
    Bl, S, _ = proj3.shape

    def body(x_ref, pw_ref, ps_ref, cw_ref, po_ref, co_ref):
        g = pl.program_id(1)
        row = lax.broadcasted_iota(jnp.int32, (S, GROUP_W), 0)
        u, cv, cb, cc = _group_columns(x_ref)
        d = _window_sum(u, g, row, _shift_down) / _window_count(g, row) - u
        po_ref[...] = (_dot(d.astype(BF16), pw_ref[...]) * ps_ref[...]).astype(BF16)
        z = cc * cv
        y = cw_ref[0:1, :] * _shift_down(z, 2, row) + cw_ref[1:2, :] * _shift_down(z, 1, row) + cw_ref[2:3, :] * z
        co_ref[...] = (cb * y).astype(BF16)

    out = pl.BlockSpec((None, S, GROUP_W), lambda b, g: (b, 0, g))
    return _pcall(
        body, hosted, name="poolconv_fwd", grid=(Bl, N_GROUPS),
        in_specs=[pl.BlockSpec((None, S, BRANCH_W), lambda b, g: (b, 0, lay["pc"] // BRANCH_W + g)),
                  pl.BlockSpec((None, None, GROUP_W, GROUP_W), lambda b, g: (layer, g, 0, 0)),
                  pl.BlockSpec((None, 1, GROUP_W), lambda b, g: (layer, 0, g)),
                  pl.BlockSpec((None, None, 3, GROUP_W), lambda b, g: (g, layer, 0, 0))],
        out_specs=[out, out],
        out_shape=[SDS((Bl, S, BRANCH_W), BF16), SDS((Bl, S, BRANCH_W), BF16)],
        semantics=("arbitrary", "arbitrary"),
    )(proj3, pool_w, pool_scale, conv_w)


def poolconv_bwd(proj3, dpo, dco, pool_w, pool_scale, conv_w, layer, dproj3, lay, hosted=None):
    Bl, S, _ = proj3.shape

    def body(x_ref, dpo_ref, dco_ref, pw_ref, ps_ref, cw_ref, _, dx_ref, dpw_ref, dps_ref, dcw_ref):
        g, b = pl.program_id(0), pl.program_id(1)
        row = lax.broadcasted_iota(jnp.int32, (S, GROUP_W), 0)
        cnt = _window_count(g, row)
        u, cv, cb, cc = _group_columns(x_ref)
        d = (_window_sum(u, g, row, _shift_down) / cnt - u).astype(BF16)
        pw = pw_ref[...]
        ypre = _dot(d, pw)
        dpo_v = dpo_ref[...].astype(F32)
        dps = jnp.sum(dpo_v * ypre, axis=0, keepdims=True)
        dyp = (dpo_v * ps_ref[...]).astype(BF16)
        dpw = _dot_tn(d, dyp)
        dd = _dot_nt(dyp, pw)
        dx_ref[:, 0:GROUP_W] = (_window_sum(dd / cnt, g, row, _shift_up) - dd).astype(BF16)

        z = cc * cv
        z1, z2 = _shift_down(z, 1, row), _shift_down(z, 2, row)
        w0, w1, w2 = cw_ref[0:1, :], cw_ref[1:2, :], cw_ref[2:3, :]
        y = w0 * z2 + w1 * z1 + w2 * z
        dco_v = dco_ref[...].astype(F32)
        dy = dco_v * cb
        dz = w0 * _shift_up(dy, 2, row) + w1 * _shift_up(dy, 1, row) + w2 * dy
        dx_ref[:, GROUP_W:2 * GROUP_W] = (dz * cc).astype(BF16)
        dx_ref[:, 2 * GROUP_W:3 * GROUP_W] = (dco_v * y).astype(BF16)
        dx_ref[:, 3 * GROUP_W:] = (dz * cv).astype(BF16)
        dcw = jnp.concatenate([jnp.sum(dy * z2, axis=0, keepdims=True),
                               jnp.sum(dy * z1, axis=0, keepdims=True),
                               jnp.sum(dy * z, axis=0, keepdims=True)], axis=0)

        @pl.when(b == 0)
        def _():
            dpw_ref[...] = dpw
            dps_ref[...] = dps
            dcw_ref[...] = dcw

        @pl.when(b > 0)
        def _():
            dpw_ref[...] += dpw
            dps_ref[...] += dps
            dcw_ref[...] += dcw

    blk = pl.BlockSpec((None, S, GROUP_W), lambda g, b: (b, 0, g))
    pc = pl.BlockSpec((None, S, BRANCH_W), lambda g, b: (b, 0, lay["pc"] // BRANCH_W + g))
    return _pcall(
        body, hosted, name="poolconv_bwd", grid=(N_GROUPS, Bl),
        in_specs=[pc, blk, blk,
                  pl.BlockSpec((None, None, GROUP_W, GROUP_W), lambda g, b: (layer, g, 0, 0)),
                  pl.BlockSpec((None, 1, GROUP_W), lambda g, b: (layer, 0, g)),
                  pl.BlockSpec((None, None, 3, GROUP_W), lambda g, b: (g, layer, 0, 0)), _ANY],
        out_specs=[pc, pl.BlockSpec((None, GROUP_W, GROUP_W), lambda g, b: (g, 0, 0)),
                   pl.BlockSpec((1, GROUP_W), lambda g, b: (0, g)),
                   pl.BlockSpec((None, 3, GROUP_W), lambda g, b: (g, 0, 0))],
        out_shape=[SDS(dproj3.shape, BF16), SDS((N_GROUPS, GROUP_W, GROUP_W), F32), SDS((1, BRANCH_W), F32),
                   SDS((N_GROUPS, 3, GROUP_W), F32)],
        aliases={6: 0}, semantics=("arbitrary", "arbitrary"),
    )(proj3, dpo, dco, pool_w, pool_scale, conv_w, dproj3)


def _tile_2d(rows, cols, n_arrays):
    budget = VMEM_LIMIT // 2
    lanes = -(-cols // LANES) * LANES
    if rows % 8 == 0:
        for t in range(min(rows, 2048), 7, -8):
            if rows % t == 0 and 2 * n_arrays * t * lanes * 4 <= budget:
                return t, cols
    for t in (1024, 512, 256, 128):
        if cols % t == 0 and 2 * n_arrays * (rows + 8) * t * 4 <= budget:
            return rows, t
    return rows, cols


def add_pair(kept, layer, where, received, name):
    _, n, _, R, C = kept.shape
    tr, tc = _tile_2d(R, C, 3)

    def body(where_ref, a_ref, b_ref, o_ref):
        o_ref[...] = (a_ref[...].astype(F32) + b_ref[...].astype(F32)).astype(BF16)

    blk = pl.BlockSpec((None, tr, tc), lambda d, i, j, where_ref: (d, i, j))
    grid_spec = pltpu.PrefetchScalarGridSpec(
        num_scalar_prefetch=1, grid=(n, R // tr, C // tc),
        in_specs=[pl.BlockSpec((None, None, None, tr, tc),
                               lambda d, i, j, where_ref: (layer, d, where_ref[0], i, j)), blk],
        out_specs=blk)
    return pl.pallas_call(body, name=name, grid_spec=grid_spec, out_shape=SDS((n, R, C), BF16),
                          compiler_params=_params("arbitrary", "arbitrary", "arbitrary"))(where, kept, received)


def add_chips(arrived, own, layer, where, n_layers, prev, name):
    _, R, C = arrived.shape
    tr, tc = _tile_2d(R, C, 6)

    def body(where_ref, a0, a1, a2, a3, own_ref, *rest):
        o_ref = rest[-1]
        chip = where_ref[1]
        acc = None
        for j, a_ref in enumerate((a0, a1, a2, a3)):
            term = jnp.where(chip == j, own_ref[...], a_ref[...]).astype(F32)
            acc = term if acc is None else acc + term
        o_ref[...] = acc

    def slot(j):
        return pl.BlockSpec((None, tr, tc), lambda i, k, where_ref, j=j: (
            jnp.where(where_ref[1] == j, (j + 1) % N_CHIPS, j), i, k))

    in_specs = [slot(j) for j in range(N_CHIPS)] + [
        pl.BlockSpec((None, tr, tc), lambda i, k, where_ref: (where_ref[1], i, k))]
    args = [where, arrived, arrived, arrived, arrived, own]
    aliases = {}
    if prev is not None:
        in_specs.append(_ANY)
        args.append(prev)
        aliases = {len(args) - 1: 0}
    grid_spec = pltpu.PrefetchScalarGridSpec(
        num_scalar_prefetch=1, grid=(R // tr, C // tc), in_specs=in_specs,
        out_specs=pl.BlockSpec((None, None, tr, tc), lambda i, k, where_ref: (layer, where_ref[0], i, k)))
    return pl.pallas_call(body, name=name, grid_spec=grid_spec, out_shape=SDS((n_layers, 2, R, C), F32),
                          input_output_aliases=aliases,
                          compiler_params=_params("arbitrary", "arbitrary"))(*args)


def adamw(w, g, m, v, name):
    if w.ndim == 2:
        R, C = w.shape
        tr, _ = _tile_2d(R, C, 7)
        grid, blk = (R // tr,), pl.BlockSpec((tr, C), lambda i: (i, 0))
    else:
        N, r, C = w.shape
        tn = max(t for t in range(1, N + 1) if N % t == 0 and t * r * C * 4 <= 512 * 1024)
        grid, blk = (N // tn,), pl.BlockSpec((tn, r, C), lambda i: (i, 0, 0))

    def body(w_ref, g_ref, m_ref, v_ref, d_ref, nm_ref, nv_ref):
        gv = g_ref[...]
        m_new = ADAM_B1 * m_ref[...] + (1.0 - ADAM_B1) * gv
        v_new = ADAM_B2 * v_ref[...] + (1.0 - ADAM_B2) * (gv * gv)
        m_hat = m_new / (1.0 - ADAM_B1 ** ADAM_STEP)
        v_hat = v_new / (1.0 - ADAM_B2 ** ADAM_STEP)
        d_ref[...] = -ADAM_LR * (m_hat / (jnp.sqrt(v_hat) + ADAM_EPS) + ADAM_WD * w_ref[...])
        nm_ref[...] = m_new
        nv_ref[...] = v_new

    out = SDS(w.shape, F32)
    return pl.pallas_call(body, name=name, grid=grid, in_specs=[blk] * 4, out_specs=[blk] * 3,
                          out_shape=[out, out, out], compiler_params=_params("arbitrary"))(w, g, m, v)


_COMM = pltpu.CompilerParams(has_side_effects=True)


def gather_buffers(shards):
    me_chip = 2 * lax.axis_index("x") + lax.axis_index("y")
    pool = {}
    for name, sh in shards.items():
        L, r, c = sh.shape
        if name in ROW_SHARDED:
            pool[name] = lax.dynamic_update_slice(lax.empty((L, N_CHIPS, r, c), sh.dtype), sh[:, None],
                                                  (0, me_chip, 0, 0))
        else:
            pool[name] = lax.dynamic_update_slice(lax.empty((N_CHIPS, L, r, c), sh.dtype), sh[None],
                                                  (me_chip, 0, 0, 0))
    return pool


def comm_now(pool, stages, name):
    stages = [Hosted(pool, jobs) for jobs in stages]
    names = sorted({m for st in stages for m in st.names})
    n = len(names)

    def body(*refs):
        bufs = dict(zip(names, refs[n:2 * n]))
        sems = refs[2 * n:]
        for i, st in enumerate(stages):
            plan = _hosted_plan(st, bufs, sems[2 * i], sems[2 * i + 1])
            _hosted_start(plan, True)
            _hosted_finish(plan, True)

    sem = pltpu.SemaphoreType.DMA
    scratch = []
    for st in stages:
        scratch += [sem((len(st.jobs), 3)), sem((len(st.jobs), 3))]
    res = pl.pallas_call(
        body, name=name, in_specs=[_ANY] * n, out_specs=[_ANY] * n,
        out_shape=[SDS(pool[m].shape, pool[m].dtype) for m in names],
        scratch_shapes=scratch, input_output_aliases={t: t for t in range(n)},
        compiler_params=_COMM,
    )(*[pool[m] for m in names])
    pool.update(zip(names, res))


def gather_now(pool, units):
    comm_now(pool, [[("ici", name, layer) for name, layer in units],
                    [("fwd", name, layer) for name, layer in units]], "gather_now")


def allgather_chips(buf, name):
    def body(src_ref, out_ref, send_sems, recv_sems, local_sem):
        x, y, c = _position()
        me = 2 * x + y
        mine = pltpu.make_async_copy(src_ref, out_ref.at[me], local_sem)
        mine.start()
        sends = []
        for k, (px, py) in enumerate(_other_chips(x, y)):
            cp = _remote(src_ref, out_ref.at[me], send_sems.at[k], recv_sems.at[k], (px, py, c))
            cp.start()
            sends.append(cp)
        for k, (px, py) in enumerate(_other_chips(x, y)):
            _remote(src_ref, out_ref.at[2 * px + py], send_sems.at[k], recv_sems.at[k], (px, py, c)).wait_recv()
        for cp in sends:
            cp.wait_send()
        mine.wait()

    sem = pltpu.SemaphoreType.DMA
    return pl.pallas_call(
        body, name=name, in_specs=[_ANY], out_specs=_ANY, out_shape=SDS((N_CHIPS,) + buf.shape, buf.dtype),
        scratch_shapes=[sem((3,)), sem((3,)), sem], compiler_params=_COMM,
    )(buf)


BIG = ("w_in", "w_proj_attn", "w_proj_pool", "w_proj_conv", "conv_w", "w_out", "w_gate_up", "w_down")
REPLICATED = ("attn_norm", "b_forget", "b_gate", "pool_w", "pool_scale", "ffn_norm", "final_norm")
ORDER = ("attn_norm", "w_in", "b_forget", "b_gate", "w_proj_attn", "pool_w", "pool_scale", "w_proj_pool",
         "conv_w", "w_proj_conv", "w_out", "ffn_norm", "w_gate_up", "w_down", "final_norm")


def _proj_layout(D):
    lay = {"g": 0, "q": 3 * D}
    lay["k"] = lay["q"] + BRANCH_W
    lay["f"] = lay["k"] + BRANCH_W
    lay["v"] = lay["f"] + F_PAD
    lay["pc"] = lay["v"] + BRANCH_W
    lay["width"] = lay["pc"] + 4 * BRANCH_W
    return lay


_REF = dict(q=0, k=512, v=1024, f=1536, u=1544, cv=2056, cb=2568, cc=3080, g=3592)


def _packed_pieces(D):
    pieces = [(_REF["g"], 3 * D), (_REF["q"], BRANCH_W), (_REF["k"], BRANCH_W), (_REF["f"], HEADS),
              (None, F_PAD - HEADS), (_REF["v"], BRANCH_W)]
    for gi in range(N_GROUPS):
        pieces += [(_REF[name] + gi * GROUP_W, GROUP_W) for name in ("u", "cv", "cb", "cc")]
    return pieces


def _packed_runs(D, cs):
    runs, at = [], 0
    for start, n in _packed_pieces(D):
        if start is None:
            runs.append((at, None, 0, n))
            at += n
        while start is not None and n:
            chip, off = divmod(start, cs)
            take = min(n, cs - off)
            runs.append((at, chip, off, take))
            at, start, n = at + take, start + take, n - take
    return runs


def pack_w_in(shards, layer):
    _, _, cs, D = shards.shape
    runs = _packed_runs(D, cs)
    width = runs[-1][0] + runs[-1][3]
    tc = _tile(D, (256, 128))

    def body(s_ref, o_ref):
        for dst, chip, off, rows in runs:
            if chip is None:
                o_ref[dst:dst + rows, :] = jnp.zeros((rows, tc), s_ref.dtype)
            else:
                o_ref[dst:dst + rows, :] = s_ref[chip, off:off + rows, :]

    return pl.pallas_call(
        body, name="pack_w_in", grid=(D // tc,),
        in_specs=[pl.BlockSpec((N_CHIPS, None, cs, tc), lambda j: (0, layer, 0, j))],
        out_specs=pl.BlockSpec((width, tc), lambda j: (0, j)),
        out_shape=SDS((width, D), shards.dtype), compiler_params=_params("arbitrary"),
    )(shards)


def unpack_w_in(p, cs):
    width, D = p.shape
    half = cs // 2
    runs = []
    for src, chip, off, rows in _packed_runs(D, cs):
        while chip is not None and rows:
            h, at = divmod(off, half)
            take = min(rows, half - at)
            runs.append((src, chip, h, at, take))
            src, off, rows = src + take, off + take, rows - take
    tc = _tile(D, (256, 128))

    def body(p_ref, o_ref):
        for src, chip, h, at, rows in runs:
            o_ref[chip, h, at:at + rows, :] = p_ref[src:src + rows, :]

    return pl.pallas_call(
        body, name="unpack_w_in", grid=(D // tc,),
        in_specs=[pl.BlockSpec((width, tc), lambda j: (0, j))],
        out_specs=pl.BlockSpec((N_CHIPS, 2, half, tc), lambda j: (0, 0, 0, j)),
        out_shape=SDS((N_CHIPS, 2, half, D), p.dtype), compiler_params=_params("arbitrary"),
    )(p)


def _split_flat(vec, shapes):
    out, at = [], 0
    for shp in shapes:
        n = int(np.prod(shp))
        out.append(vec[at:at + n].reshape(shp))
        at += n
    return out


def kernel(x, attn_norm, w_in, b_forget, b_gate, w_proj_attn, pool_w, pool_scale, w_proj_pool, conv_w, w_proj_conv, w_out, ffn_norm, w_gate_up, w_down, final_norm, loss_target, m_attn_norm, m_w_in, m_b_forget, m_b_gate, m_w_proj_attn, m_pool_w, m_pool_scale, m_w_proj_pool, m_conv_w, m_w_proj_conv, m_w_out, m_ffn_norm, m_w_gate_up, m_w_down, m_final_norm, v_attn_norm, v_w_in, v_b_forget, v_b_gate, v_w_proj_attn, v_pool_w, v_pool_scale, v_w_proj_pool, v_conv_w, v_w_proj_conv, v_w_out, v_ffn_norm, v_w_gate_up, v_w_down, v_final_norm):
    weights = dict(attn_norm=attn_norm, w_in=w_in, b_forget=b_forget, b_gate=b_gate, w_proj_attn=w_proj_attn,
                   pool_w=pool_w, pool_scale=pool_scale, w_proj_pool=w_proj_pool, conv_w=conv_w,
                   w_proj_conv=w_proj_conv, w_out=w_out, ffn_norm=ffn_norm, w_gate_up=w_gate_up, w_down=w_down,
                   final_norm=final_norm)
    mom_m = dict(attn_norm=m_attn_norm, w_in=m_w_in, b_forget=m_b_forget, b_gate=m_b_gate, w_proj_attn=m_w_proj_attn,
                 pool_w=m_pool_w, pool_scale=m_pool_scale, w_proj_pool=m_w_proj_pool, conv_w=m_conv_w,
                 w_proj_conv=m_w_proj_conv, w_out=m_w_out, ffn_norm=m_ffn_norm, w_gate_up=m_w_gate_up,
                 w_down=m_w_down, final_norm=m_final_norm)
    mom_v = dict(attn_norm=v_attn_norm, w_in=v_w_in, b_forget=v_b_forget, b_gate=v_b_gate, w_proj_attn=v_w_proj_attn,
                 pool_w=v_pool_w, pool_scale=v_pool_scale, w_proj_pool=v_w_proj_pool, conv_w=v_conv_w,
                 w_proj_conv=v_w_proj_conv, w_out=v_w_out, ffn_norm=v_ffn_norm, w_gate_up=v_w_gate_up,
                 w_down=v_w_down, final_norm=v_final_norm)

    Bl, S, D = x.shape
    T = Bl * S
    L = w_in.shape[0]
    F = w_down.shape[1] * N_CHIPS
    lay = _proj_layout(D)
    cst = _placement_constants()
    assert L == N_LAYERS and S % ATTN_BLOCK == 0 and F % (2 * LANES) == 0 and D % BRANCH_W == 0
    assert w_in.shape[2] * N_CHIPS == _REF["g"] + 3 * D and conv_w.shape[2] == GROUP_W

    send = {n: weights[n].astype(BF16) for n in BIG}
    send["conv_w"] = conv_w
    me_chip = 2 * lax.axis_index("x") + lax.axis_index("y")
    send["w_in"] = w_in.transpose(0, 2, 1).astype(BF16)
    pool = gather_buffers(send)
    gather_now(pool, [("w_in", 0)])
    rest = ("w_out", "w_proj_attn", "w_proj_pool", "w_gate_up", "w_proj_conv", "conv_w")
    late = ("w_out", "w_proj_attn", "w_proj_pool", "w_proj_conv", "conv_w")
    jobs = lambda kind, names, layer: [(kind, n, layer) for n in names]
    carried = {
        ("in_proj", 0): jobs("ici", rest, 0),
        ("attn_prep", 0): jobs("fwd", late, 0),
        ("attn_fwd", 0): jobs("fwd", ("w_gate_up",), 0) + jobs("ici", ("w_in",), 1) + jobs("ici", ("w_down",), 0),
        ("poolconv_fwd", 0): jobs("fwd", ("w_in",), 1) + jobs("fwd", ("w_down",), 0),
        ("mix_fwd", 0): jobs("ici", ("w_down",), 1),
        ("gate_up_proj", 0): jobs("ici", late, 1) + jobs("fwd", ("w_down",), 1),
        ("ffn_down_fwd", 0): jobs("ici", ("w_gate_up",), 1),
        ("in_proj", 1): jobs("fwd", ("w_gate_up",) + late, 1),
    }
    carry = lambda call, layer: Hosted(pool, carried[call, layer]) if (call, layer) in carried else None
    w_down_f = lambda: pool["w_down"].reshape(L, F, D)
    pool_w_b = pool_w.astype(BF16)
    an3, fn3 = attn_norm.reshape(L, 1, D), ffn_norm.reshape(L, 1, D)
    bg3, ps3 = b_gate.reshape(L, 1, 3 * D), pool_scale.reshape(L, 1, BRANCH_W)
    bf3 = jnp.pad(b_forget, ((0, 0), (0, LANES - HEADS))).reshape(L, 1, LANES)

    xs = x.reshape(T, D)
    saved = []
    w_in_p = []
    for l in range(L):
        w_in_p.append(pack_w_in(pool["w_in"], l))
        proj, h = norm_matmul(xs, an3, w_in_p[l], l, "rows", "in_proj", carry("in_proj", l))
        proj3 = proj.reshape(Bl, S, lay["width"])
        qa, ka = attn_prep(proj3, bf3, l, cst, lay, carry("attn_prep", l))
        ao, lse = attn_fwd(qa, ka, proj3, lay, carry("attn_fwd", l))
        po, co = poolconv_fwd(proj3, pool_w_b, ps3, pool["conv_w"], l, lay, carry("poolconv_fwd", l))
        ao2, po2, co2 = (a.reshape(T, BRANCH_W) for a in (ao, po, co))
        x1, ys, mixed = mix_fwd(ao2, po2, co2, proj, bg3, pool["w_proj_attn"], pool["w_proj_pool"],
                                pool["w_proj_conv"], pool["w_out"], l, xs, carry("mix_fwd", l))
        ab, h2 = norm_matmul(x1, fn3, pool["w_gate_up"], l, "by_shard", "gate_up_proj", carry("gate_up_proj", l))
        x2, s_act = ffn_down_fwd(ab, w_down_f(), l, x1, carry("ffn_down_fwd", l))
        saved.append(dict(x=xs, proj=proj, proj3=proj3, h=h, qa=qa, ka=ka, ao=ao, lse=lse, ao2=ao2, po2=po2,
                          co2=co2, ys=ys, mixed=mixed, x1=x1, ab=ab, h2=h2, s=s_act))
        xs = x2
    w_gu, w_o, conv_w_g = pool["w_gate_up"], pool["w_out"], pool["conv_w"]
    wpa, wpp, wpc = pool["w_proj_attn"], pool["w_proj_pool"], pool["w_proj_conv"]
    w_down_f = w_down_f()

    loss_row, dx, dxb, g_final = loss_head(xs, final_norm.reshape(1, D), loss_target.reshape(T, D))
    loss = lax.psum(loss_row[0, 0], AXES)

    reduced_names = tuple(n for n in BIG if n != "conv_w")
    early_names = tuple(n for n in reduced_names if n != "w_in")
    proj_names = ("w_out", "w_proj_attn", "w_proj_pool", "w_proj_conv")
    first_names = ("w_in", "w_gate_up", "w_down")
    where = jnp.stack([lax.axis_index("c"), me_chip]).astype(jnp.int32)
    rs = {}

    def reduce_begin(layer, grads):
        for n, g in grads.items():
            g5 = g.reshape((1, N_CHIPS, 2, -1) + g.shape[-1:])
            rs["g%d:%s" % (layer, n)] = g5
            for role in "ra":
                rs["%s%d:%s" % (role, layer, n)] = lax.empty((N_CHIPS,) + g5.shape[3:], BF16)

    swap_jobs = lambda layer, names: [("swap", "g%d:%s" % (layer, n), "r%d:%s" % (layer, n), 0) for n in names]
    xchg_jobs = lambda layer, names: [("xchg", "s%d:%s" % (layer, n), "a%d:%s" % (layer, n)) for n in names]
    join_jobs = lambda layer, names: [("join", "o:" + n, layer) for n in names]

    def pair_sums(layer, names):
        for n in names:
            rs["s%d:%s" % (layer, n)] = add_pair(rs["g%d:%s" % (layer, n)], 0, where, rs["r%d:%s" % (layer, n)],
                                                 "add_pair_" + n)

    def chip_sums(layer, names, slot, n_slots):
        for n in names:
            rs["o:" + n] = add_chips(rs["a%d:%s" % (layer, n)], rs["s%d:%s" % (layer, n)], slot, where, n_slots,
                                     rs.get("o:" + n), "add_chips_" + n)

    small = {n: [None] * L for n in REPLICATED if n != "final_norm"}
    g_conv = [None] * L
    to3 = lambda a: a.reshape(Bl, S, -1)
    for l in reversed(range(L)):
        sv = saved[l]
        behind = (lambda jobs: Hosted(rs, jobs)) if l == 0 else (lambda jobs: None)
        grads = {}
        da, db = ffn_down_bwd(dxb, w_down_f, l, sv["ab"], behind(swap_jobs(1, reduced_names)))
        if l == 0:
            pair_sums(1, reduced_names)
        grads["w_down"] = matmul_tn(sv["s"], [dxb], "grad_w_down", hosted=behind(xchg_jobs(1, ("w_down",))))
        grads["w_gate_up"] = matmul_tn(sv["h2"], [da, db], "grad_w_gate_up", by_dest=True, tn=2 * F // N_CHIPS,
                                       tk=_tile(T, (1024, 512, 256)), hosted=behind(xchg_jobs(1, ("w_gate_up",))))
        dx1, dx1b, g_fn = matmul_nt_normbwd([da, db], w_gu, l, "by_shard", sv["x1"], fn3, dx, "gate_up_bwd",
                                            behind(xchg_jobs(1, ("w_in",))))
        small["ffn_norm"][l] = g_fn[0]
        if l == 0:
            chip_sums(1, first_names, 1, L)
        dys, dproj, dao, dpo, dco, g_bg = mix_bwd(dx1b, w_o, sv["proj"], bg3, sv["ys"], wpa, wpp, wpc, l,
                                                  lay["width"],
                                                  behind(xchg_jobs(1, proj_names) + join_jobs(1, first_names)))
        if l == 0:
            chip_sums(1, proj_names, 1, L)
        small["b_gate"][l] = g_bg[0]
        grads["w_out"] = matmul_tn(sv["mixed"], [dx1b], "grad_w_out")
        for n, (name, br) in enumerate((("w_proj_attn", sv["ao2"]), ("w_proj_pool", sv["po2"]),
                                        ("w_proj_conv", sv["co2"]))):
            grads[name] = matmul_tn(br, [dys], "grad_" + name, b_col0=n * D, n_cols=D, by_dest=True,
                                    tn=D // N_CHIPS)
        if l == 0:
            reduce_begin(0, grads)
        dqa, dka, dproj3 = attn_bwd(sv["qa"], sv["ka"], sv["proj3"], to3(dao), sv["ao"], sv["lse"], to3(dproj), lay,
                                    behind(swap_jobs(0, early_names) + join_jobs(1, proj_names)))
        if l == 0:
            pair_sums(0, early_names)
        dproj3, g_bf = attn_post(dqa, dka, sv["proj3"], bf3, l, dproj3, cst, lay, behind(xchg_jobs(
            0, ("w_out", "w_proj_attn", "w_proj_pool", "w_proj_conv"))))
        small["b_forget"][l] = g_bf[0, :HEADS]
        dproj3, g_pw, g_ps, g_conv[l] = poolconv_bwd(sv["proj3"], to3(dpo), to3(dco), pool_w_b, ps3, conv_w_g, l,
                                                     dproj3, lay, behind(xchg_jobs(0, ("w_down",))))
        small["pool_w"][l], small["pool_scale"][l] = g_pw, g_ps[0]
        dproj = dproj3.reshape(T, lay["width"])
        g_w_in = unpack_w_in(matmul_tn(dproj, [sv["h"]], "grad_w_in", hosted=behind(xchg_jobs(
            0, ("w_gate_up",)))), w_in.shape[2])
        if l:
            reduce_begin(l, {**grads, "w_in": g_w_in})
        else:
            reduce_begin(0, {"w_in": g_w_in})
            comm_now(rs, [swap_jobs(0, ("w_in",))], "swap_w_in_halves")
            pair_sums(0, ("w_in",))
        dx, dxb, g_an = matmul_nt_normbwd([dproj], w_in_p[l], l, "rows", sv["x"], an3, dx1, "in_proj_bwd",
                                          behind(xchg_jobs(0, ("w_in",))))
        small["attn_norm"][l] = g_an[0]
    grad_x = dx.reshape(Bl, S, D)

    small_shapes = [weights[n].shape for n in REPLICATED] + [(L, N_CHIPS) + conv_w.shape[1:]]
    small_vec = jnp.concatenate([jnp.stack(small[n]).reshape(-1) for n in REPLICATED[:-1]]
                                + [g_final[0], jnp.stack(g_conv).reshape(-1)])
    n_small = small_vec.shape[0]
    small_vec = jnp.pad(small_vec, (0, -n_small % (2 * N_CHIPS * 16 * LANES))).astype(BF16)
    rs["g0:small"] = small_vec.reshape(1, N_CHIPS, 2, -1, LANES)
    for role in "ra":
        rs[role + "0:small"] = lax.empty((N_CHIPS,) + rs["g0:small"].shape[3:], BF16)
    last = ("small",)
    comm_now(rs, [swap_jobs(0, last)], "swap_grad_halves")
    pair_sums(0, last)
    comm_now(rs, [xchg_jobs(0, last)], "exchange_grad_chips")
    chip_sums(0, reduced_names, 0, L)
    chip_sums(0, ("small",), 0, 1)
    comm_now(rs, [join_jobs(0, reduced_names + ("small",))], "join_grad_halves")
    shard_grads = {n: rs["o:" + n].reshape((L, -1) + rs["o:" + n].shape[-1:]) for n in reduced_names}
    small_all = allgather_chips(rs["o:small"].reshape(-1, LANES), "allgather_small_grads").reshape(-1)[:n_small]
    *rep_list, conv_all = _split_flat(small_all, small_shapes)
    rep_grads = dict(zip(REPLICATED, rep_list))
    shard_grads["conv_w"] = lax.dynamic_index_in_dim(conv_all, me_chip, 1, keepdims=False)

    delta, new_m, new_v = {}, {}, {}
    for n in BIG:
        shp = weights[n].shape
        if n == "w_in":
            view, back = (lambda a: a.transpose(2, 0, 1)), (lambda a: a.transpose(1, 2, 0))
            g = shard_grads[n].transpose(1, 0, 2)
        else:
            view, back = (lambda a: a.reshape(-1, shp[-1])), (lambda a: a.reshape(shp))
            g = view(shard_grads[n])
        d, nm, nv = adamw(view(weights[n]), g, view(mom_m[n]), view(mom_v[n]), "adamw_" + n)
        delta[n], new_m[n], new_v[n], shard_grads[n] = back(d), back(nm), back(nv), back(g)

    def rows(d):
        vec = jnp.concatenate([d[n].reshape(-1) for n in REPLICATED])
        return jnp.pad(vec, (0, -vec.shape[0] % (8 * LANES))).reshape(-1, LANES)

    outs = adamw(rows(weights), rows(rep_grads), rows(mom_m), rows(mom_v), "adamw_replicated")
    for res, o in zip((delta, new_m, new_v), outs):
        res.update(zip(REPLICATED, _split_flat(o.reshape(-1), small_shapes[:len(REPLICATED)])))
    all_grads = {**shard_grads, **rep_grads}

    return (loss, grad_x, *[all_grads[n] for n in ORDER], *[delta[n] for n in ORDER],
            *[new_m[n] for n in ORDER], *[new_v[n] for n in ORDER])
```

```python
import numpy as np
import jax
import jax.numpy as jnp
from jax import lax
from jax.experimental import pallas as pl
from jax.experimental.pallas import tpu as pltpu

F32, BF16 = jnp.float32, jnp.bfloat16
SDS = jax.ShapeDtypeStruct
MESH = pl.DeviceIdType.MESH
AXES = ("x", "y", "c")
N_CHIPS = 4
N_LAYERS = 2
LANES = 128
VMEM_LIMIT = 48 * 1024 * 1024

HEADS, HEAD_DIM = 8, 64
HEAD_PAD = 128
BRANCH_W = 512
GROUP_W = 128
N_GROUPS = BRANCH_W // GROUP_W
POOL_WINDOWS = (2, 4, 8, 16)
F_PAD = 512
ATTN_BLOCK = 256
RMS_EPS = 1e-6
NEG_INF = -1e30
ADAM_LR, ADAM_B1, ADAM_B2, ADAM_EPS, ADAM_WD, ADAM_STEP = 0.001, 0.9, 0.999, 1e-08, 0.01, 10

NT = (((1,), (1,)), ((), ()))
TN = (((0,), (0,)), ((), ()))
_ANY = pl.BlockSpec(memory_space=pl.ANY)


def _tile(n, prefs):
    for p in prefs:
        if n % p == 0:
            return p
    raise ValueError(f"no tile of {prefs} divides {n}")


def _params(*sem):
    return pltpu.CompilerParams(dimension_semantics=sem, vmem_limit_bytes=VMEM_LIMIT)


def _sigmoid(z):
    return 0.5 * jnp.tanh(0.5 * z) + 0.5


def _split3(x):
    h1 = x.astype(BF16)
    r1 = x - h1.astype(F32)
    h2 = r1.astype(BF16)
    h3 = (r1 - h2.astype(F32)).astype(BF16)
    return h1, h2, h3


def _position():
    return lax.axis_index("x"), lax.axis_index("y"), lax.axis_index("c")


def _other_chips(x, y):
    return [(1 - x, y), (x, 1 - y), (1 - x, 1 - y)]


def _remote(src, dst, send_sem, recv_sem, device):
    return pltpu.make_async_remote_copy(src_ref=src, dst_ref=dst, send_sem=send_sem, recv_sem=recv_sem,
                                        device_id=device, device_id_type=MESH)


ROW_SHARDED = ("w_out", "w_down")
FETCHER = dict(w_in=0, w_out=0, w_proj_attn=0, w_proj_pool=0, w_gate_up=1, w_down=1, w_proj_conv=1, conv_w=1)


class Hosted:
    def __init__(self, pool, jobs):
        self.pool, self.jobs = pool, list(jobs)
        names = set()
        for job in self.jobs:
            names.update(job[1:3] if job[0] in ("swap", "xchg") else job[1:2])
        self.names = sorted(names)


def _hosted_plan(hosted, refs, send_sems, recv_sems):
    x, y, c = _position()
    me = 2 * x + y
    others = _other_chips(x, y)
    sibling = (x, y, 1 - c)
    plan = []
    for j, job in enumerate(hosted.jobs):
        kind = job[0]
        sems = lambda k, j=j: (send_sems.at[j, k], recv_sems.at[j, k])
        if kind in ("ici", "fwd"):
            _, name, layer = job
            ref = refs[name]
            win = (lambda chip, ref=ref, layer=layer: ref.at[layer, chip]) if name in ROW_SHARDED else (
                lambda chip, ref=ref, layer=layer: ref.at[chip, layer])
            mine = c == FETCHER[name]
            if kind == "ici":
                sends = [_remote(win(me), win(me), *sems(k), (px, py, c)) for k, (px, py) in enumerate(others)]
                arrivals = [_remote(win(2 * px + py), win(2 * px + py), *sems(k), (px, py, c))
                            for k, (px, py) in enumerate(others)]
                plan.append((mine, sends, arrivals, []))
            else:
                sends = [_remote(win(2 * px + py), win(2 * px + py), *sems(k), sibling)
                         for k, (px, py) in enumerate(others)]
                plan.append((mine, sends, [], sends))
        elif kind == "swap":
            _, src, dst, layer = job
            cp = _remote(refs[src].at[layer, :, 1 - c], refs[dst], *sems(0), sibling)
            plan.append((True, [cp], [cp], []))
        elif kind == "xchg":
            _, src, dst = job
            sends = [_remote(refs[src].at[2 * px + py], refs[dst].at[me], *sems(k), (px, py, c))
                     for k, (px, py) in enumerate(others)]
            arrivals = [_remote(refs[src].at[me], refs[dst].at[2 * px + py], *sems(k), (px, py, c))
                        for k, (px, py) in enumerate(others)]
            plan.append((True, sends, arrivals, []))
        else:
            _, name, layer = job
            ref = refs[name]
            cp = _remote(ref.at[layer, c], ref.at[layer, c], *sems(0), sibling)
            arrival = _remote(ref.at[layer, c], ref.at[layer, 1 - c], *sems(0), sibling)
            plan.append((True, [cp], [arrival], []))
    return plan


def _hosted_start(plan, now):
    for mine, sends, _, _ in plan:
        @pl.when(now & mine)
        def _(sends=sends):
            for cp in sends:
                cp.start()


def _hosted_finish(plan, now):
    for mine, sends, arrivals, sibling_arrivals in plan:
        @pl.when(now & mine)
        def _(sends=sends, arrivals=arrivals):
            for cp in arrivals:
                cp.wait_recv()
            for cp in sends:
                cp.wait_send()

        if sibling_arrivals:
            @pl.when(now & jnp.logical_not(mine))
            def _(sibling_arrivals=sibling_arrivals):
                for cp in sibling_arrivals:
                    cp.wait_recv()


def _pcall(body, hosted, *, name, grid, in_specs, out_specs, out_shape, semantics, scratch_shapes=(), aliases=None):
    aliases = dict(aliases or {})
    if hosted is None or not hosted.jobs:
        return pl.pallas_call(body, name=name, grid=grid, in_specs=in_specs, out_specs=out_specs,
                              out_shape=out_shape, scratch_shapes=list(scratch_shapes),
                              input_output_aliases=aliases, compiler_params=_params(*semantics))
    single = not isinstance(out_shape, (list, tuple))
    out_specs_l = [out_specs] if single else list(out_specs)
    out_shape_l = [out_shape] if single else list(out_shape)
    n_in, n_out, n_buf, n_job = len(in_specs), len(out_specs_l), len(hosted.names), len(hosted.jobs)

    def carrying(*refs):
        ins, outs = refs[:n_in], refs[n_in + n_buf:n_in + n_buf + n_out]
        bufs = refs[n_in + n_buf + n_out:n_in + 2 * n_buf + n_out]
        rest = refs[n_in + 2 * n_buf + n_out:]
        scratch, send_sems, recv_sems = rest[:-2], rest[-2], rest[-1]
        first, last = True, True
        for axis, size in enumerate(grid):
            first = first & (pl.program_id(axis) == 0)
            last = last & (pl.program_id(axis) == size - 1)
        plan = _hosted_plan(hosted, dict(zip(hosted.names, bufs)), send_sems, recv_sems)
        _hosted_start(plan, first)
        body(*ins, *outs, *scratch)
        _hosted_finish(plan, last)

    def run(*args):
        bufs = [hosted.pool[n] for n in hosted.names]
        sem = pltpu.SemaphoreType.DMA
        res = pl.pallas_call(
            carrying, name=name, grid=grid, in_specs=list(in_specs) + [_ANY] * n_buf,
            out_specs=out_specs_l + [_ANY] * n_buf,
            out_shape=out_shape_l + [SDS(b.shape, b.dtype) for b in bufs],
            scratch_shapes=list(scratch_shapes) + [sem((n_job, 3)), sem((n_job, 3))],
            input_output_aliases={**aliases, **{n_in + i: n_out + i for i in range(n_buf)}},
            compiler_params=pltpu.CompilerParams(dimension_semantics=semantics, vmem_limit_bytes=VMEM_LIMIT,
                                                 has_side_effects=True),
        )(*args, *bufs)
        hosted.pool.update(zip(hosted.names, res[n_out:]))
        return res[0] if single else res[:n_out]

    return run


def _dot(a, b):
    return jnp.dot(a, b, preferred_element_type=F32)


def _dot_nt(a, b):
    return lax.dot_general(a, b, NT, preferred_element_type=F32)


def _dot_tn(a, b):
    return lax.dot_general(a, b, TN, preferred_element_type=F32)


def norm_matmul(x, gain, w, layer, kind, name, hosted=None):
    T, D = x.shape
    if kind == "by_shard":
        tn = w.shape[3]
        N = N_CHIPS * tn
        w_spec = pl.BlockSpec((None, None, D, tn), lambda i, j: (j, layer, 0, 0))
        mm = _dot
    else:
        N = w.shape[0]
        tn = _tile(N, (1024, 512, 256, 128))
        w_spec = pl.BlockSpec((tn, D), lambda i, j: (j, 0))
        mm = _dot_nt
    tm = _tile(T, (1024, 512, 256, 128))

    def body(x_ref, g_ref, w_ref, y_ref, h_ref):
        @pl.when(pl.program_id(1) == 0)
        def _():
            xf = x_ref[...]
            r = lax.rsqrt(jnp.mean(xf * xf, axis=-1, keepdims=True) + RMS_EPS)
            h_ref[...] = ((xf * r) * g_ref[...]).astype(BF16)

        y_ref[...] = mm(h_ref[...], w_ref[...]).astype(BF16)

    return _pcall(
        body, hosted, name=name, grid=(T // tm, N // tn),
        in_specs=[pl.BlockSpec((tm, D), lambda i, j: (i, 0)),
                  pl.BlockSpec((None, 1, D), lambda i, j: (layer, 0, 0)),
                  w_spec],
        out_specs=[pl.BlockSpec((tm, tn), lambda i, j: (i, j)),
                   pl.BlockSpec((tm, D), lambda i, j: (i, 0))],
        out_shape=[SDS((T, N), BF16), SDS((T, D), BF16)],
        semantics=("arbitrary", "arbitrary"),
    )(x, gain, w)


def matmul_nt_normbwd(dys, w, layer, kind, x, gain, dres, name, hosted=None):
    T, D = x.shape
    width = dys[0].shape[1]
    if kind == "by_shard":
        tk = w.shape[3]
        w_spec = pl.BlockSpec((None, None, D, tk), lambda i, k: (k, layer, 0, 0))
        mm = _dot_nt
    else:
        tk = _tile(width, (3584, 1024, 512, 256, 128))
        w_spec = pl.BlockSpec((tk, D), lambda i, k: (k, 0))
        mm = _dot
    per = width // tk
    nk = per * len(dys)
    tm = _tile(T, (512, 256, 128))
    n_dy = len(dys)

    def dy_spec(p):
        return pl.BlockSpec((tm, tk), lambda i, k: (i, jnp.clip(k - p * per, 0, per - 1)))

    def body(*refs):
        dy_refs = refs[:n_dy]
        w_ref, x_ref, g_ref, dres_ref, dx_ref, dxb_ref, dg_ref, acc_ref = refs[n_dy:]
        i, k = pl.program_id(0), pl.program_id(1)

        @pl.when(k == 0)
        def _():
            acc_ref[...] = jnp.zeros_like(acc_ref)

        for p in range(n_dy):
            @pl.when((k >= p * per) & (k < (p + 1) * per))
            def _(p=p):
                acc_ref[...] += mm(dy_refs[p][...], w_ref[...])

        @pl.when(k == nk - 1)
        def _():
            xf = x_ref[...]
            r = lax.rsqrt(jnp.mean(xf * xf, axis=-1, keepdims=True) + RMS_EPS)
            xhat = xf * r
            dh = acc_ref[...]
            dhg = dh * g_ref[...]
            dx = dres_ref[...] + r * (dhg - xhat * jnp.mean(dhg * xhat, axis=-1, keepdims=True))
            dx_ref[...] = dx
            dxb_ref[...] = dx.astype(BF16)
            part = jnp.sum(dh * xhat, axis=0, keepdims=True)

            @pl.when(i == 0)
            def _():
                dg_ref[...] = part

            @pl.when(i > 0)
            def _():
                dg_ref[...] += part

    row = pl.BlockSpec((tm, D), lambda i, k: (i, 0))
    return _pcall(
        body, hosted, name=name, grid=(T // tm, nk),
        in_specs=[dy_spec(p) for p in range(n_dy)] + [
            w_spec, row, pl.BlockSpec((None, 1, D), lambda i, k: (layer, 0, 0)), row],
        out_specs=[row, row, pl.BlockSpec((1, D), lambda i, k: (0, 0))],
        out_shape=[SDS((T, D), F32), SDS((T, D), BF16), SDS((1, D), F32)],
        scratch_shapes=[pltpu.VMEM((tm, D), F32)],
        semantics=("arbitrary", "arbitrary"),
    )(*dys, w, x, gain, dres)


def matmul_tn(a, bs, name, b_col0=0, n_cols=None, by_dest=False, tn=None, tk=None, hosted=None):
    T, M = a.shape
    width = bs[0].shape[1]
    N = n_cols if n_cols else width * len(bs)
    tm = _tile(M, (1024, 512, 256, 128))
    tn = tn or _tile(N, (512, 256, 128))
    tk = tk or _tile(T, (4096, 2048, 1024, 512, 256))
    assert b_col0 % tn == 0 and width % tn == 0
    j0, per, nk, n_b = b_col0 // tn, width // tn, T // tk, len(bs)

    def b_spec(p):
        return pl.BlockSpec((tk, tn), lambda i, j, k: (k, jnp.clip(j0 + j - p * per, 0, per - 1)))

    def body(*refs):
        a_ref, b_refs = refs[0], refs[1:1 + n_b]
        o_ref, acc_ref = refs[-2], refs[-1]
        j, k = pl.program_id(1), pl.program_id(2)

        @pl.when(k == 0)
        def _():
            acc_ref[...] = jnp.zeros_like(acc_ref)

        for p in range(n_b):
            @pl.when((j0 + j >= p * per) & (j0 + j < (p + 1) * per))
            def _(p=p):
                acc_ref[...] += _dot_tn(a_ref[...], b_refs[p][...])

        @pl.when(k == nk - 1)
        def _():
            o_ref[...] = acc_ref[...].astype(BF16)

    if by_dest:
        cs = N // N_CHIPS
        npd = cs // tn
        out_shape = SDS((N_CHIPS, M, cs), BF16)
        out_spec = pl.BlockSpec((None, tm, tn), lambda i, j, k: (j // npd, i, j % npd))
    else:
        out_shape = SDS((M, N), BF16)
        out_spec = pl.BlockSpec((tm, tn), lambda i, j, k: (i, j))
    return _pcall(
        body, hosted, name=name, grid=(M // tm, N // tn, nk),
        in_specs=[pl.BlockSpec((tk, tm), lambda i, j, k: (k, i))] + [b_spec(p) for p in range(n_b)],
        out_specs=out_spec, out_shape=out_shape,
        scratch_shapes=[pltpu.VMEM((tm, tn), F32)],
        semantics=("arbitrary", "arbitrary", "arbitrary"),
    )(a, *bs)


def ffn_down_fwd(ab, w_down, layer, x1, hosted=None):
    T, D = x1.shape
    F = w_down.shape[1]
    tm = _tile(T, (512, 256, 128))
    tk = F // 2
    nk = F // tk

    def body(a_ref, b_ref, w_ref, x_ref, x2_ref, s_ref, acc_ref):
        k = pl.program_id(1)

        @pl.when(k == 0)
        def _():
            acc_ref[...] = x_ref[...]

        a = a_ref[...].astype(F32)
        s = (a * _sigmoid(a) * b_ref[...].astype(F32)).astype(BF16)
        s_ref[...] = s
        acc_ref[...] += _dot(s, w_ref[...])

        @pl.when(k == nk - 1)
        def _():
            x2_ref[...] = acc_ref[...]

    return _pcall(
        body, hosted, name="ffn_down_fwd", grid=(T // tm, nk),
        in_specs=[pl.BlockSpec((tm, tk), lambda i, k: (i, k)),
                  pl.BlockSpec((tm, tk), lambda i, k: (i, nk + k)),
                  pl.BlockSpec((None, tk, D), lambda i, k: (layer, k, 0)),
                  pl.BlockSpec((tm, D), lambda i, k: (i, 0))],
        out_specs=[pl.BlockSpec((tm, D), lambda i, k: (i, 0)),
                   pl.BlockSpec((tm, tk), lambda i, k: (i, k))],
        out_shape=[SDS((T, D), F32), SDS((T, F), BF16)],
        scratch_shapes=[pltpu.VMEM((tm, D), F32)],
        semantics=("arbitrary", "arbitrary"),
    )(ab, ab, w_down, x1)


def ffn_down_bwd(dx2b, w_down, layer, ab, hosted=None):
    T, D = dx2b.shape
    F = w_down.shape[1]
    tm = _tile(T, (512, 256, 128))
    tn = F // 2
    nj = F // tn

    def body(dx_ref, w_ref, a_ref, b_ref, da_ref, db_ref):
        ds = _dot_nt(dx_ref[...], w_ref[...])
        a = a_ref[...].astype(F32)
        sg = _sigmoid(a)
        da_ref[...] = (ds * b_ref[...].astype(F32) * (sg * (1.0 + a * (1.0 - sg)))).astype(BF16)
        db_ref[...] = (ds * (a * sg)).astype(BF16)

    blk = pl.BlockSpec((tm, tn), lambda i, j: (i, j))
    return _pcall(
        body, hosted, name="ffn_down_bwd", grid=(T // tm, nj),
        in_specs=[pl.BlockSpec((tm, D), lambda i, j: (i, 0)),
                  pl.BlockSpec((None, tn, D), lambda i, j: (layer, j, 0)),
                  blk, pl.BlockSpec((tm, tn), lambda i, j: (i, nj + j))],
        out_specs=[blk, blk],
        out_shape=[SDS((T, F), BF16), SDS((T, F), BF16)],
        semantics=("arbitrary", "arbitrary"),
    )(dx2b, w_down, ab, ab)


def _mix_specs(tm, D, layer):
    cs = D // N_CHIPS
    row = lambda w: pl.BlockSpec((tm, w), lambda i: (i, 0))
    wp = pl.BlockSpec((N_CHIPS, None, BRANCH_W, cs), lambda i: (0, layer, 0, 0))
    wo = pl.BlockSpec((None, N_CHIPS, cs, D), lambda i: (layer, 0, 0, 0))
    bg = pl.BlockSpec((None, 1, 3 * D), lambda i: (layer, 0, 0))
    return row, wp, wo, bg


def mix_fwd(ao, po, co, proj, b_gate, wpa, wpp, wpc, w_out, layer, x, hosted=None):
    T, D = x.shape
    cs = D // N_CHIPS
    tm = _tile(T, (256, 128))
    row, wp, wo, bg = _mix_specs(tm, D, layer)

    def body(ao_ref, po_ref, co_ref, g_ref, bg_ref, wpa_ref, wpp_ref, wpc_ref, wo_ref, x_ref,
             x1_ref, ys_ref, mixed_ref):
        mixed = jnp.zeros((tm, D), F32)
        for n, (br, wp_ref) in enumerate(((ao_ref, wpa_ref), (po_ref, wpp_ref), (co_ref, wpc_ref))):
            y = jnp.concatenate([_dot(br[...], wp_ref[j]) for j in range(N_CHIPS)], axis=1)
            cols = slice(n * D, (n + 1) * D)
            gate = _sigmoid(g_ref[:, cols].astype(F32) + bg_ref[:, cols])
            ys_ref[:, cols] = y.astype(BF16)
            mixed = mixed + gate * y
        mb = mixed.astype(BF16)
        mixed_ref[...] = mb
        acc = x_ref[...]
        for j in range(N_CHIPS):
            acc = acc + _dot(mb[:, j * cs:(j + 1) * cs], wo_ref[j])
        x1_ref[...] = acc

    return _pcall(
        body, hosted, name="mix_fwd", grid=(T // tm,),
        in_specs=[row(BRANCH_W), row(BRANCH_W), row(BRANCH_W), row(3 * D), bg, wp, wp, wp, wo, row(D)],
        out_specs=[row(D), row(3 * D), row(D)],
        out_shape=[SDS((T, D), F32), SDS((T, 3 * D), BF16), SDS((T, D), BF16)],
        semantics=("arbitrary",),
    )(ao, po, co, proj, b_gate, wpa, wpp, wpc, w_out, x)


def mix_bwd(dx1b, w_out, proj, b_gate, ys, wpa, wpp, wpc, layer, width, hosted=None):
    T, D = dx1b.shape
    cs = D // N_CHIPS
    tm = _tile(T, (256, 128))
    row, wp, wo, bg = _mix_specs(tm, D, layer)

    def body(dx_ref, wo_ref, g_ref, bg_ref, ys_ref, wpa_ref, wpp_ref, wpc_ref,
             dys_ref, dg_ref, dao_ref, dpo_ref, dco_ref, dbg_ref):
        i = pl.program_id(0)
        dx = dx_ref[...]
        dmixed = jnp.concatenate([_dot_nt(dx, wo_ref[j]) for j in range(N_CHIPS)], axis=1)
        for n, (wp_ref, dbr) in enumerate(((wpa_ref, dao_ref), (wpp_ref, dpo_ref), (wpc_ref, dco_ref))):
            cols = slice(n * D, (n + 1) * D)
            gate = _sigmoid(g_ref[:, cols].astype(F32) + bg_ref[:, cols])
            dy = (dmixed * gate).astype(BF16)
            dys_ref[:, cols] = dy
            dgp = dmixed * ys_ref[:, cols].astype(F32) * gate * (1.0 - gate)
            dg_ref[:, cols] = dgp.astype(BF16)
            part = jnp.sum(dgp, axis=0, keepdims=True)

            @pl.when(i == 0)
            def _():
                dbg_ref[:, cols] = part

            @pl.when(i > 0)
            def _():
                dbg_ref[:, cols] += part

            acc = jnp.zeros((tm, BRANCH_W), F32)
            for j in range(N_CHIPS):
                acc = acc + _dot_nt(dy[:, j * cs:(j + 1) * cs], wp_ref[j])
            dbr[...] = acc.astype(BF16)

    return _pcall(
        body, hosted, name="mix_bwd", grid=(T // tm,),
        in_specs=[row(D), wo, row(3 * D), bg, row(3 * D), wp, wp, wp],
        out_specs=[row(3 * D), row(3 * D), row(BRANCH_W), row(BRANCH_W), row(BRANCH_W),
                   pl.BlockSpec((1, 3 * D), lambda i: (0, 0))],
        out_shape=[SDS((T, 3 * D), BF16), SDS((T, width), BF16), SDS((T, BRANCH_W), BF16),
                   SDS((T, BRANCH_W), BF16), SDS((T, BRANCH_W), BF16), SDS((1, 3 * D), F32)],
        semantics=("arbitrary",),
    )(dx1b, w_out, proj, b_gate, ys, wpa, wpp, wpc)


def loss_head(x2, gain, target):
    T, D = x2.shape
    tm = _tile(T, (512, 256, 128))

    def body(x_ref, g_ref, t_ref, loss_ref, dx_ref, dxb_ref, dg_ref):
        i = pl.program_id(0)
        xf = x_ref[...]
        g = g_ref[...]
        r = lax.rsqrt(jnp.mean(xf * xf, axis=-1, keepdims=True) + RMS_EPS)
        xhat = xf * r
        diff = xhat * g - t_ref[...]
        part_loss = 0.5 * jnp.sum(jnp.mean(diff * diff, axis=-1, keepdims=True), axis=0, keepdims=True)
        dy = diff * (1.0 / D)
        dhg = dy * g
        dx = r * (dhg - xhat * jnp.mean(dhg * xhat, axis=-1, keepdims=True))
        dx_ref[...] = dx
        dxb_ref[...] = dx.astype(BF16)
        part_g = jnp.sum(dy * xhat, axis=0, keepdims=True)
        part_l = jnp.broadcast_to(part_loss, (1, LANES))

        @pl.when(i == 0)
        def _():
            dg_ref[...] = part_g
            loss_ref[...] = part_l

        @pl.when(i > 0)
        def _():
            dg_ref[...] += part_g
            loss_ref[...] += part_l

    row = pl.BlockSpec((tm, D), lambda i: (i, 0))
    return pl.pallas_call(
        body, name="loss_head", grid=(T // tm,),
        in_specs=[row, pl.BlockSpec((1, D), lambda i: (0, 0)), row],
        out_specs=[pl.BlockSpec((1, LANES), lambda i: (0, 0)), row, row, pl.BlockSpec((1, D), lambda i: (0, 0))],
        out_shape=[SDS((1, LANES), F32), SDS((T, D), F32), SDS((T, D), BF16), SDS((1, D), F32)],
        compiler_params=_params("arbitrary"),
    )(x2, gain, target)


def _placement_constants():
    w = HEADS * HEAD_PAD
    pq = np.zeros((BRANCH_W, w), np.float32)
    pk = np.zeros((BRANCH_W, w), np.float32)
    pfq = np.zeros((3, LANES, w), np.float32)
    pfk = np.zeros((3, LANES, w), np.float32)
    cq = np.zeros((1, w), np.float32)
    ck = np.zeros((1, w), np.float32)
    eq = np.zeros((w, LANES), np.float32)
    ek = np.zeros((w, LANES), np.float32)
    for h in range(HEADS):
        for d in range(HEAD_DIM):
            pq[h * HEAD_DIM + d, h * HEAD_PAD + d] = HEAD_DIM ** -0.5
            pk[h * HEAD_DIM + d, h * HEAD_PAD + d] = 1.0
        for i in range(3):
            pfq[i, h, h * HEAD_PAD + HEAD_DIM + i] = 1.0
            pfk[i, h, h * HEAD_PAD + HEAD_DIM + 3 + i] = -1.0
            cq[0, h * HEAD_PAD + HEAD_DIM + 3 + i] = 1.0
            ck[0, h * HEAD_PAD + HEAD_DIM + i] = 1.0
        eq[h * HEAD_PAD + HEAD_DIM, h] = 1.0
        ek[h * HEAD_PAD + HEAD_DIM + 3, h] = -1.0
    bf = lambda a: jnp.asarray(a, BF16)
    return dict(pq=bf(pq), pk=bf(pk), pfq=bf(pfq), pfk=bf(pfk), cq=jnp.asarray(cq), ck=jnp.asarray(ck),
                pqkt=bf(np.concatenate([pq.T, pk.T], axis=0)), eq=bf(eq), ek=bf(ek))


def attn_prep(proj3, bf_rows, layer, cst, lay, hosted=None):
    Bl, S, _ = proj3.shape
    ts = ATTN_BLOCK
    w = HEADS * HEAD_PAD

    def body(q_ref, k_ref, f_ref, bf_ref, pq_ref, pk_ref, pfq_ref, pfk_ref, cq_ref, ck_ref,
             qa_ref, ka_ref, carry_ref):
        @pl.when(pl.program_id(1) == 0)
        def _():
            carry_ref[...] = jnp.zeros_like(carry_ref)

        z = f_ref[...].astype(F32) + bf_ref[...]
        logf = jnp.minimum(z, 0.0) - jnp.log(1.0 + jnp.exp(-jnp.abs(z)))
        r = lax.broadcasted_iota(jnp.int32, (ts, ts), 0)
        c = lax.broadcasted_iota(jnp.int32, (ts, ts), 1)
        tri = jnp.where(r >= c, 1.0, 0.0).astype(BF16)
        fcum = carry_ref[...]
        for part in _split3(logf):
            fcum = fcum + _dot(tri, part)
        carry_ref[...] = fcum[ts - 1:ts, :]
        qa = _dot(q_ref[...], pq_ref[...]) + cq_ref[...]
        ka = _dot(k_ref[...], pk_ref[...]) + ck_ref[...]
        for i, part in enumerate(_split3(fcum)):
            qa = qa + _dot(part, pfq_ref[i])
            ka = ka + _dot(part, pfk_ref[i])
        qa_ref[...] = qa.astype(BF16)
        ka_ref[...] = ka.astype(BF16)

    cfull = lambda shape: pl.BlockSpec(shape, lambda b, s: (0,) * len(shape))
    return _pcall(
        body, hosted, name="attn_prep", grid=(Bl, S // ts),
        in_specs=[pl.BlockSpec((None, ts, BRANCH_W), lambda b, s: (b, s, lay["q"] // BRANCH_W)),
                  pl.BlockSpec((None, ts, BRANCH_W), lambda b, s: (b, s, lay["k"] // BRANCH_W)),
                  pl.BlockSpec((None, ts, LANES), lambda b, s: (b, s, lay["f"] // LANES)),
                  pl.BlockSpec((None, 1, LANES), lambda b, s: (layer, 0, 0)),
                  cfull((BRANCH_W, w)), cfull((BRANCH_W, w)),
                  cfull((3, LANES, w)), cfull((3, LANES, w)), cfull((1, w)), cfull((1, w))],
        out_specs=[pl.BlockSpec((None, ts, w), lambda b, s: (b, s, 0)),
                   pl.BlockSpec((None, ts, w), lambda b, s: (b, s, 0))],
        out_shape=[SDS((Bl, S, w), BF16), SDS((Bl, S, w), BF16)],
        scratch_shapes=[pltpu.VMEM((1, LANES), F32)],
        semantics=("arbitrary", "arbitrary"),
    )(proj3, proj3, proj3, bf_rows, cst["pq"], cst["pk"], cst["pfq"], cst["pfk"], cst["cq"], cst["ck"])


def attn_fwd(qa, ka, proj3, lay, hosted=None):
    Bl, S, _ = qa.shape
    tq = ATTN_BLOCK
    nq = S // tq
    pairs = HEADS // 2
    pw = 2 * HEAD_PAD
    vw = 2 * HEAD_DIM

    def body(qa_ref, ka_ref, v_ref, o_ref, lse_ref):
        row = lax.broadcasted_iota(jnp.int32, (tq, tq), 0)
        col = lax.broadcasted_iota(jnp.int32, (tq, tq), 1)
        causal = row <= col
        for i in range(nq):
            nk = (i + 1) * tq
            rows = slice(i * tq, nk)
            o_t = []
            for h in range(2):
                hs = slice(h * HEAD_PAD, (h + 1) * HEAD_PAD)
                st = _dot_nt(ka_ref[0:nk, hs], qa_ref[rows, hs])
                diag = jnp.where(causal, st[nk - tq:], NEG_INF)
                m = jnp.max(diag, axis=0, keepdims=True)
                if i:
                    m = jnp.maximum(m, jnp.max(st[:nk - tq], axis=0, keepdims=True))
                p_diag = jnp.exp(diag - m)
                l = jnp.sum(p_diag, axis=0, keepdims=True)
                if i:
                    p_top = jnp.exp(st[:nk - tq] - m)
                    l = l + jnp.sum(p_top, axis=0, keepdims=True)
                    p = jnp.concatenate([p_top.astype(BF16), p_diag.astype(BF16)], axis=0)
                else:
                    p = p_diag.astype(BF16)
                acc = _dot_tn(v_ref[0:nk, :], p)
                o_t.append(acc[h * HEAD_DIM:(h + 1) * HEAD_DIM, :] / l)
                lse_ref[h:h + 1, rows] = m + jnp.log(l)
            o_ref[rows, :] = jnp.concatenate(o_t, axis=0).T.astype(BF16)

    return _pcall(
        body, hosted, name="attn_fwd", grid=(Bl, pairs),
        in_specs=[pl.BlockSpec((None, S, pw), lambda b, p: (b, 0, p)),
                  pl.BlockSpec((None, S, pw), lambda b, p: (b, 0, p)),
                  pl.BlockSpec((None, S, vw), lambda b, p: (b, 0, lay["v"] // vw + p))],
        out_specs=[pl.BlockSpec((None, S, vw), lambda b, p: (b, 0, p)),
                   pl.BlockSpec((None, None, 2, S), lambda b, p: (b, p, 0, 0))],
        out_shape=[SDS((Bl, S, BRANCH_W), BF16), SDS((Bl, pairs, 2, S), F32)],
        semantics=("arbitrary", "arbitrary"),
    )(qa, ka, proj3)


def attn_bwd(qa, ka, proj3, dao, ao, lse, dproj3, lay, hosted=None):
    Bl, S, _ = qa.shape
    tk = ATTN_BLOCK
    nq = S // tk
    pairs = HEADS // 2
    pw = 2 * HEAD_PAD
    vw = 2 * HEAD_DIM

    def body(qa_ref, ka_ref, v_ref, do_ref, o_ref, lse_ref, _, dqa_ref, dka_ref, dv_ref):
        row = lax.broadcasted_iota(jnp.int32, (tk, tk), 0)
        col = lax.broadcasted_iota(jnp.int32, (tk, tk), 1)
        causal = row <= col
        lane8 = lax.broadcasted_iota(jnp.int32, (8, vw), 1)
        lane_s = lax.broadcasted_iota(jnp.int32, (S, vw), 1)
        lane_k = lax.broadcasted_iota(jnp.int32, (tk, vw), 1)
        doo = do_ref[...].astype(F32) * o_ref[...].astype(F32)
        hi = doo.astype(BF16)
        lo = (doo - hi.astype(F32)).astype(BF16)
        delta, v_head = [], []
        for h in range(2):
            sel = jnp.where((lane8 >= h * HEAD_DIM) & (lane8 < (h + 1) * HEAD_DIM), 1.0, 0.0).astype(BF16)
            delta.append((_dot_nt(sel, hi) + _dot_nt(sel, lo))[0:1, :])
            in_head = (lane_s >= h * HEAD_DIM) & (lane_s < (h + 1) * HEAD_DIM)
            v_head.append(jnp.where(in_head, v_ref[...], jnp.zeros_like(v_ref[...])))
        dqa_ref[...] = jnp.zeros_like(dqa_ref)
        for j in range(nq):
            q0 = j * tk
            krows = slice(q0, q0 + tk)
            do = do_ref[q0:, :]
            dvs = []
            for h in range(2):
                hs = slice(h * HEAD_PAD, (h + 1) * HEAD_PAD)
                k = ka_ref[krows, hs]
                q = qa_ref[q0:, hs]
                st = _dot_nt(k, q)
                p = jnp.exp(st - lse_ref[h:h + 1, q0:])
                p_diag = jnp.where(causal, p[:, :tk], 0.0)
                p = jnp.concatenate([p_diag, p[:, tk:]], axis=1) if j < nq - 1 else p_diag
                dvs.append(_dot(p.astype(BF16), do))
                dpt = _dot_nt(v_head[h][krows, :], do)
                ds = (p * (dpt - delta[h][:, q0:])).astype(BF16)
                dka_ref[krows, hs] = _dot(ds, q)
                dqa_ref[q0:, hs] += _dot_tn(ds, k)
            dv_ref[krows, :] = jnp.where(lane_k < HEAD_DIM, dvs[0], dvs[1]).astype(BF16)

    seq = lambda w, c0=0: pl.BlockSpec((None, S, w), lambda b, p: (b, 0, c0 + p))
    return _pcall(
        body, hosted, name="attn_bwd", grid=(Bl, pairs),
        in_specs=[seq(pw), seq(pw), seq(vw, lay["v"] // vw), seq(vw), seq(vw),
                  pl.BlockSpec((None, None, 2, S), lambda b, p: (b, p, 0, 0)), _ANY],
        out_specs=[seq(pw), seq(pw), seq(vw, lay["v"] // vw)],
        out_shape=[SDS((Bl, S, HEADS * HEAD_PAD), F32), SDS((Bl, S, HEADS * HEAD_PAD), F32),
                   SDS(dproj3.shape, BF16)],
        aliases={6: 2}, semantics=("arbitrary", "arbitrary"),
    )(qa, ka, proj3, dao, ao, lse, dproj3)


def attn_post(dqa, dka, proj3, bf_rows, layer, dproj3, cst, lay, hosted=None):
    Bl, S, w = dqa.shape
    ts = ATTN_BLOCK
    ns = S // ts
    qkf = 2 * BRANCH_W + F_PAD

    def body(dqa_ref, dka_ref, f_ref, bf_ref, pqkt_ref, eq_ref, ek_ref, _, dqkf_ref, dbf_ref, carry_ref):
        b, s = pl.program_id(0), pl.program_id(1)

        @pl.when(s == 0)
        def _():
            carry_ref[...] = jnp.zeros_like(carry_ref)

        dqa_v, dka_v = dqa_ref[...], dka_ref[...]
        qh = dqa_v.astype(BF16)
        kh = dka_v.astype(BF16)
        dqkf_ref[:, :BRANCH_W] = _dot(qh, pqkt_ref[:w, :]).astype(BF16)
        dqkf_ref[:, BRANCH_W:2 * BRANCH_W] = _dot(kh, pqkt_ref[w:, :]).astype(BF16)
        ql = (dqa_v - qh.astype(F32)).astype(BF16)
        kl = (dka_v - kh.astype(F32)).astype(BF16)
        d_f = (_dot(qh, eq_ref[...]) + _dot(ql, eq_ref[...])) + (_dot(kh, ek_ref[...]) + _dot(kl, ek_ref[...]))
        r = lax.broadcasted_iota(jnp.int32, (ts, ts), 0)
        c = lax.broadcasted_iota(jnp.int32, (ts, ts), 1)
        triu = jnp.where(c >= r, 1.0, 0.0).astype(BF16)
        rev = carry_ref[...]
        for part in _split3(d_f):
            rev = rev + _dot(triu, part)
        carry_ref[...] = rev[0:1, :]
        z = f_ref[...].astype(F32) + bf_ref[...]
        lane = lax.broadcasted_iota(jnp.int32, (ts, LANES), 1)
        dfl = jnp.where(lane < HEADS, rev / (1.0 + jnp.exp(z)), 0.0)
        dqkf_ref[:, 2 * BRANCH_W:] = jnp.concatenate(
            [dfl.astype(BF16), jnp.zeros((ts, F_PAD - LANES), BF16)], axis=1)
        part = jnp.sum(dfl, axis=0, keepdims=True)

        @pl.when((b == 0) & (s == 0))
        def _():
            dbf_ref[...] = part

        @pl.when((b > 0) | (s > 0))
        def _():
            dbf_ref[...] += part

    assert lay["q"] % qkf == 0
    cfull = lambda shape: pl.BlockSpec(shape, lambda b, s: (0,) * len(shape))
    rev_blk = lambda wd, c0=0: pl.BlockSpec((None, ts, wd), lambda b, s: (b, ns - 1 - s, c0))
    return _pcall(
        body, hosted, name="attn_post", grid=(Bl, ns),
        in_specs=[rev_blk(w), rev_blk(w), rev_blk(LANES, lay["f"] // LANES),
                  pl.BlockSpec((None, 1, LANES), lambda b, s: (layer, 0, 0)),
                  cfull((2 * w, BRANCH_W)), cfull((w, LANES)), cfull((w, LANES)), _ANY],
        out_specs=[rev_blk(qkf, lay["q"] // qkf), cfull((1, LANES))],
        out_shape=[SDS(dproj3.shape, BF16), SDS((1, LANES), F32)],
        scratch_shapes=[pltpu.VMEM((1, LANES), F32)],
        aliases={7: 0}, semantics=("arbitrary", "arbitrary"),
    )(dqa, dka, proj3, bf_rows, cst["pqkt"], cst["eq"], cst["ek"], dproj3)


def _shift_down(x, k, row):
    return jnp.where(row >= k, pltpu.roll(x, k, axis=0), 0.0)


def _shift_up(x, k, row):
    n = x.shape[0]
    return jnp.where(row < n - k, pltpu.roll(x, n - k, axis=0), 0.0)


def _window_sum(x, g, row, shift):
    s2 = x + shift(x, 1, row)
    s4 = s2 + shift(s2, 2, row)
    s8 = s4 + shift(s4, 4, row)
    s16 = s8 + shift(s8, 8, row)
    return jnp.where(g == 0, s2, jnp.where(g == 1, s4, jnp.where(g == 2, s8, s16)))


def _window_count(g, row):
    wnd = jnp.where(g == 0, 2, jnp.where(g == 1, 4, jnp.where(g == 2, 8, 16)))
    return jnp.minimum(row + 1, wnd).astype(F32)


def _group_columns(ref):
    return [ref[:, n * GROUP_W:(n + 1) * GROUP_W].astype(F32) for n in range(4)]


def poolconv_fwd(proj3, pool_w, pool_scale, conv_w, layer, lay, hosted=None):
    Bl, S, _ = proj3.shape

    def body(x_ref, pw_ref, ps_ref, cw_ref, po_ref, co_ref):
        g = pl.program_id(1)
        row = lax.broadcasted_iota(jnp.int32, (S, GROUP_W), 0)
        u, cv, cb, cc = _group_columns(x_ref)
        d = _window_sum(u, g, row, _shift_down) / _window_count(g, row) - u
        po_ref[...] = (_dot(d.astype(BF16), pw_ref[...]) * ps_ref[...]).astype(BF16)
        z = cc * cv
        y = cw_ref[0:1, :] * _shift_down(z, 2, row) + cw_ref[1:2, :] * _shift_down(z, 1, row) + cw_ref[2:3, :] * z
        co_ref[...] = (cb * y).astype(BF16)

    out = pl.BlockSpec((None, S, GROUP_W), lambda b, g: (b, 0, g))
    return _pcall(
        body, hosted, name="poolconv_fwd", grid=(Bl, N_GROUPS),
        in_specs=[pl.BlockSpec((None, S, BRANCH_W), lambda b, g: (b, 0, lay["pc"] // BRANCH_W + g)),
                  pl.BlockSpec((None, None, GROUP_W, GROUP_W), lambda b, g: (layer, g, 0, 0)),
                  pl.BlockSpec((None, 1, GROUP_W), lambda b, g: (layer, 0, g)),
                  pl.BlockSpec((None, None, 3, GROUP_W), lambda b, g: (g, layer, 0, 0))],
        out_specs=[out, out],
        out_shape=[SDS((Bl, S, BRANCH_W), BF16), SDS((Bl, S, BRANCH_W), BF16)],
        semantics=("arbitrary", "arbitrary"),
    )(proj3, pool_w, pool_scale, conv_w)


def poolconv_bwd(proj3, dpo, dco, pool_w, pool_scale, conv_w, layer, dproj3, lay, hosted=None):
    Bl, S, _ = proj3.shape

    def body(x_ref, dpo_ref, dco_ref, pw_ref, ps_ref, cw_ref, _, dx_ref, dpw_ref, dps_ref, dcw_ref):
        g, b = pl.program_id(0), pl.program_id(1)
        row = lax.broadcasted_iota(jnp.int32, (S, GROUP_W), 0)
        cnt = _window_count(g, row)
        u, cv, cb, cc = _group_columns(x_ref)
        d = (_window_sum(u, g, row, _shift_down) / cnt - u).astype(BF16)
        pw = pw_ref[...]
        ypre = _dot(d, pw)
        dpo_v = dpo_ref[...].astype(F32)
        dps = jnp.sum(dpo_v * ypre, axis=0, keepdims=True)
        dyp = (dpo_v * ps_ref[...]).astype(BF16)
        dpw = _dot_tn(d, dyp)
        dd = _dot_nt(dyp, pw)
        dx_ref[:, 0:GROUP_W] = (_window_sum(dd / cnt, g, row, _shift_up) - dd).astype(BF16)

        z = cc * cv
        z1, z2 = _shift_down(z, 1, row), _shift_down(z, 2, row)
        w0, w1, w2 = cw_ref[0:1, :], cw_ref[1:2, :], cw_ref[2:3, :]
        y = w0 * z2 + w1 * z1 + w2 * z
        dco_v = dco_ref[...].astype(F32)
        dy = dco_v * cb
        dz = w0 * _shift_up(dy, 2, row) + w1 * _shift_up(dy, 1, row) + w2 * dy
        dx_ref[:, GROUP_W:2 * GROUP_W] = (dz * cc).astype(BF16)
        dx_ref[:, 2 * GROUP_W:3 * GROUP_W] = (dco_v * y).astype(BF16)
        dx_ref[:, 3 * GROUP_W:] = (dz * cv).astype(BF16)
        dcw = jnp.concatenate([jnp.sum(dy * z2, axis=0, keepdims=True),
                               jnp.sum(dy * z1, axis=0, keepdims=True),
                               jnp.sum(dy * z, axis=0, keepdims=True)], axis=0)

        @pl.when(b == 0)
        def _():
            dpw_ref[...] = dpw
            dps_ref[...] = dps
            dcw_ref[...] = dcw

        @pl.when(b > 0)
        def _():
            dpw_ref[...] += dpw
            dps_ref[...] += dps
            dcw_ref[...] += dcw

    blk = pl.BlockSpec((None, S, GROUP_W), lambda g, b: (b, 0, g))
    pc = pl.BlockSpec((None, S, BRANCH_W), lambda g, b: (b, 0, lay["pc"] // BRANCH_W + g))
    return _pcall(
        body, hosted, name="poolconv_bwd", grid=(N_GROUPS, Bl),
        in_specs=[pc, blk, blk,
                  pl.BlockSpec((None, None, GROUP_W, GROUP_W), lambda g, b: (layer, g, 0, 0)),
                  pl.BlockSpec((None, 1, GROUP_W), lambda g, b: (layer, 0, g)),
                  pl.BlockSpec((None, None, 3, GROUP_W), lambda g, b: (g, layer, 0, 0)), _ANY],
        out_specs=[pc, pl.BlockSpec((None, GROUP_W, GROUP_W), lambda g, b: (g, 0, 0)),
                   pl.BlockSpec((1, GROUP_W), lambda g, b: (0, g)),
                   pl.BlockSpec((None, 3, GROUP_W), lambda g, b: (g, 0, 0))],
        out_shape=[SDS(dproj3.shape, BF16), SDS((N_GROUPS, GROUP_W, GROUP_W), F32), SDS((1, BRANCH_W), F32),
                   SDS((N_GROUPS, 3, GROUP_W), F32)],
        aliases={6: 0}, semantics=("arbitrary", "arbitrary"),
    )(proj3, dpo, dco, pool_w, pool_scale, conv_w, dproj3)


def _tile_2d(rows, cols, n_arrays):
    budget = VMEM_LIMIT // 2
    lanes = -(-cols // LANES) * LANES
    if rows % 8 == 0:
        for t in range(min(rows, 2048), 7, -8):
            if rows % t == 0 and 2 * n_arrays * t * lanes * 4 <= budget:
                return t, cols
    for t in (1024, 512, 256, 128):
        if cols % t == 0 and 2 * n_arrays * (rows + 8) * t * 4 <= budget:
            return rows, t
    return rows, cols


def add_pair(kept, layer, where, received, name):
    _, n, _, R, C = kept.shape
    tr, tc = _tile_2d(R, C, 3)

    def body(where_ref, a_ref, b_ref, o_ref):
        o_ref[...] = (a_ref[...].astype(F32) + b_ref[...].astype(F32)).astype(BF16)

    blk = pl.BlockSpec((None, tr, tc), lambda d, i, j, where_ref: (d, i, j))
    grid_spec = pltpu.PrefetchScalarGridSpec(
        num_scalar_prefetch=1, grid=(n, R // tr, C // tc),
        in_specs=[pl.BlockSpec((None, None, None, tr, tc),
                               lambda d, i, j, where_ref: (layer, d, where_ref[0], i, j)), blk],
        out_specs=blk)
    return pl.pallas_call(body, name=name, grid_spec=grid_spec, out_shape=SDS((n, R, C), BF16),
                          compiler_params=_params("arbitrary", "arbitrary", "arbitrary"))(where, kept, received)


def add_chips(arrived, own, layer, where, n_layers, prev, name):
    _, R, C = arrived.shape
    tr, tc = _tile_2d(R, C, 6)

    def body(where_ref, a0, a1, a2, a3, own_ref, *rest):
        o_ref = rest[-1]
        chip = where_ref[1]
        acc = None
        for j, a_ref in enumerate((a0, a1, a2, a3)):
            term = jnp.where(chip == j, own_ref[...], a_ref[...]).astype(F32)
            acc = term if acc is None else acc + term
        o_ref[...] = acc

    def slot(j):
        return pl.BlockSpec((None, tr, tc), lambda i, k, where_ref, j=j: (
            jnp.where(where_ref[1] == j, (j + 1) % N_CHIPS, j), i, k))

    in_specs = [slot(j) for j in range(N_CHIPS)] + [
        pl.BlockSpec((None, tr, tc), lambda i, k, where_ref: (where_ref[1], i, k))]
    args = [where, arrived, arrived, arrived, arrived, own]
    aliases = {}
    if prev is not None:
        in_specs.append(_ANY)
        args.append(prev)
        aliases = {len(args) - 1: 0}
    grid_spec = pltpu.PrefetchScalarGridSpec(
        num_scalar_prefetch=1, grid=(R // tr, C // tc), in_specs=in_specs,
        out_specs=pl.BlockSpec((None, None, tr, tc), lambda i, k, where_ref: (layer, where_ref[0], i, k)))
    return pl.pallas_call(body, name=name, grid_spec=grid_spec, out_shape=SDS((n_layers, 2, R, C), F32),
                          input_output_aliases=aliases,
                          compiler_params=_params("arbitrary", "arbitrary"))(*args)


def adamw(w, g, m, v, name):
    if w.ndim == 2:
        R, C = w.shape
        tr, _ = _tile_2d(R, C, 7)
        grid, blk = (R // tr,), pl.BlockSpec((tr, C), lambda i: (i, 0))
    else:
        N, r, C = w.shape
        tn = max(t for t in range(1, N + 1) if N % t == 0 and t * r * C * 4 <= 1024 * 1024)
        grid, blk = (N // tn,), pl.BlockSpec((tn, r, C), lambda i: (i, 0, 0))

    def body(w_ref, g_ref, m_ref, v_ref, d_ref, nm_ref, nv_ref):
        gv = g_ref[...]
        m_new = ADAM_B1 * m_ref[...] + (1.0 - ADAM_B1) * gv
        v_new = ADAM_B2 * v_ref[...] + (1.0 - ADAM_B2) * (gv * gv)
        m_hat = m_new / (1.0 - ADAM_B1 ** ADAM_STEP)
        v_hat = v_new / (1.0 - ADAM_B2 ** ADAM_STEP)
        d_ref[...] = -ADAM_LR * (m_hat / (jnp.sqrt(v_hat) + ADAM_EPS) + ADAM_WD * w_ref[...])
        nm_ref[...] = m_new
        nv_ref[...] = v_new

    out = SDS(w.shape, F32)
    return pl.pallas_call(body, name=name, grid=grid, in_specs=[blk] * 4, out_specs=[blk] * 3,
                          out_shape=[out, out, out], compiler_params=_params("arbitrary"))(w, g, m, v)


_COMM = pltpu.CompilerParams(has_side_effects=True)


def gather_buffers(shards):
    me_chip = 2 * lax.axis_index("x") + lax.axis_index("y")
    pool = {}
    for name, sh in shards.items():
        L, r, c = sh.shape
        if name in ROW_SHARDED:
            pool[name] = lax.dynamic_update_slice(lax.empty((L, N_CHIPS, r, c), sh.dtype), sh[:, None],
                                                  (0, me_chip, 0, 0))
        else:
            pool[name] = lax.dynamic_update_slice(lax.empty((N_CHIPS, L, r, c), sh.dtype), sh[None],
                                                  (me_chip, 0, 0, 0))
    return pool


def comm_now(pool, stages, name):
    stages = [Hosted(pool, jobs) for jobs in stages]
    names = sorted({m for st in stages for m in st.names})
    n = len(names)

    def body(*refs):
        bufs = dict(zip(names, refs[n:2 * n]))
        sems = refs[2 * n:]
        for i, st in enumerate(stages):
            plan = _hosted_plan(st, bufs, sems[2 * i], sems[2 * i + 1])
            _hosted_start(plan, True)
            _hosted_finish(plan, True)

    sem = pltpu.SemaphoreType.DMA
    scratch = []
    for st in stages:
        scratch += [sem((len(st.jobs), 3)), sem((len(st.jobs), 3))]
    res = pl.pallas_call(
        body, name=name, in_specs=[_ANY] * n, out_specs=[_ANY] * n,
        out_shape=[SDS(pool[m].shape, pool[m].dtype) for m in names],
        scratch_shapes=scratch, input_output_aliases={t: t for t in range(n)},
        compiler_params=_COMM,
    )(*[pool[m] for m in names])
    pool.update(zip(names, res))


def gather_now(pool, units):
    comm_now(pool, [[("ici", name, layer) for name, layer in units],
                    [("fwd", name, layer) for name, layer in units]], "gather_now")


def allgather_chips(buf, name):
    def body(src_ref, out_ref, send_sems, recv_sems, local_sem):
        x, y, c = _position()
        me = 2 * x + y
        mine = pltpu.make_async_copy(src_ref, out_ref.at[me], local_sem)
        mine.start()
        sends = []
        for k, (px, py) in enumerate(_other_chips(x, y)):
            cp = _remote(src_ref, out_ref.at[me], send_sems.at[k], recv_sems.at[k], (px, py, c))
            cp.start()
            sends.append(cp)
        for k, (px, py) in enumerate(_other_chips(x, y)):
            _remote(src_ref, out_ref.at[2 * px + py], send_sems.at[k], recv_sems.at[k], (px, py, c)).wait_recv()
        for cp in sends:
            cp.wait_send()
        mine.wait()

    sem = pltpu.SemaphoreType.DMA
    return pl.pallas_call(
        body, name=name, in_specs=[_ANY], out_specs=_ANY, out_shape=SDS((N_CHIPS,) + buf.shape, buf.dtype),
        scratch_shapes=[sem((3,)), sem((3,)), sem], compiler_params=_COMM,
    )(buf)


BIG = ("w_in", "w_proj_attn", "w_proj_pool", "w_proj_conv", "conv_w", "w_out", "w_gate_up", "w_down")
REPLICATED = ("attn_norm", "b_forget", "b_gate", "pool_w", "pool_scale", "ffn_norm", "final_norm")
ORDER = ("attn_norm", "w_in", "b_forget", "b_gate", "w_proj_attn", "pool_w", "pool_scale", "w_proj_pool",
         "conv_w", "w_proj_conv", "w_out", "ffn_norm", "w_gate_up", "w_down", "final_norm")


def _proj_layout(D):
    lay = {"g": 0, "q": 3 * D}
    lay["k"] = lay["q"] + BRANCH_W
    lay["f"] = lay["k"] + BRANCH_W
    lay["v"] = lay["f"] + F_PAD
    lay["pc"] = lay["v"] + BRANCH_W
    lay["width"] = lay["pc"] + 4 * BRANCH_W
    return lay


_REF = dict(q=0, k=512, v=1024, f=1536, u=1544, cv=2056, cb=2568, cc=3080, g=3592)


def _packed_pieces(D):
    pieces = [(_REF["g"], 3 * D), (_REF["q"], BRANCH_W), (_REF["k"], BRANCH_W), (_REF["f"], HEADS),
              (None, F_PAD - HEADS), (_REF["v"], BRANCH_W)]
    for gi in range(N_GROUPS):
        pieces += [(_REF[name] + gi * GROUP_W, GROUP_W) for name in ("u", "cv", "cb", "cc")]
    return pieces


def _packed_runs(D, cs):
    runs, at = [], 0
    for start, n in _packed_pieces(D):
        if start is None:
            runs.append((at, None, 0, n))
            at += n
        while start is not None and n:
            chip, off = divmod(start, cs)
            take = min(n, cs - off)
            runs.append((at, chip, off, take))
            at, start, n = at + take, start + take, n - take
    return runs


def pack_w_in(shards, layer):
    _, _, cs, D = shards.shape
    runs = _packed_runs(D, cs)
    width = runs[-1][0] + runs[-1][3]
    tc = _tile(D, (256, 128))

    def body(s_ref, o_ref):
        for dst, chip, off, rows in runs:
            if chip is None:
                o_ref[dst:dst + rows, :] = jnp.zeros((rows, tc), s_ref.dtype)
            else:
                o_ref[dst:dst + rows, :] = s_ref[chip, off:off + rows, :]

    return pl.pallas_call(
        body, name="pack_w_in", grid=(D // tc,),
        in_specs=[pl.BlockSpec((N_CHIPS, None, cs, tc), lambda j: (0, layer, 0, j))],
        out_specs=pl.BlockSpec((width, tc), lambda j: (0, j)),
        out_shape=SDS((width, D), shards.dtype), compiler_params=_params("arbitrary"),
    )(shards)


def unpack_w_in(p, cs):
    width, D = p.shape
    half = cs // 2
    runs = []
    for src, chip, off, rows in _packed_runs(D, cs):
        while chip is not None and rows:
            h, at = divmod(off, half)
            take = min(rows, half - at)
            runs.append((src, chip, h, at, take))
            src, off, rows = src + take, off + take, rows - take
    tc = _tile(D, (256, 128))

    def body(p_ref, o_ref):
        for src, chip, h, at, rows in runs:
            o_ref[chip, h, at:at + rows, :] = p_ref[src:src + rows, :]

    return pl.pallas_call(
        body, name="unpack_w_in", grid=(D // tc,),
        in_specs=[pl.BlockSpec((width, tc), lambda j: (0, j))],
        out_specs=pl.BlockSpec((N_CHIPS, 2, half, tc), lambda j: (0, 0, 0, j)),
        out_shape=SDS((N_CHIPS, 2, half, D), p.dtype), compiler_params=_params("arbitrary"),
    )(p)


def _split_flat(vec, shapes):
    out, at = [], 0
    for shp in shapes:
        n = int(np.prod(shp))
        out.append(vec[at:at + n].reshape(shp))
        at += n
    return out


def kernel(x, attn_norm, w_in, b_forget, b_gate, w_proj_attn, pool_w, pool_scale, w_proj_pool, conv_w, w_proj_conv, w_out, ffn_norm, w_gate_up, w_down, final_norm, loss_target, m_attn_norm, m_w_in, m_b_forget, m_b_gate, m_w_proj_attn, m_pool_w, m_pool_scale, m_w_proj_pool, m_conv_w, m_w_proj_conv, m_w_out, m_ffn_norm, m_w_gate_up, m_w_down, m_final_norm, v_attn_norm, v_w_in, v_b_forget, v_b_gate, v_w_proj_attn, v_pool_w, v_pool_scale, v_w_proj_pool, v_conv_w, v_w_proj_conv, v_w_out, v_ffn_norm, v_w_gate_up, v_w_down, v_final_norm):
    weights = dict(attn_norm=attn_norm, w_in=w_in, b_forget=b_forget, b_gate=b_gate, w_proj_attn=w_proj_attn,
                   pool_w=pool_w, pool_scale=pool_scale, w_proj_pool=w_proj_pool, conv_w=conv_w,
                   w_proj_conv=w_proj_conv, w_out=w_out, ffn_norm=ffn_norm, w_gate_up=w_gate_up, w_down=w_down,
                   final_norm=final_norm)
    mom_m = dict(attn_norm=m_attn_norm, w_in=m_w_in, b_forget=m_b_forget, b_gate=m_b_gate, w_proj_attn=m_w_proj_attn,
                 pool_w=m_pool_w, pool_scale=m_pool_scale, w_proj_pool=m_w_proj_pool, conv_w=m_conv_w,
                 w_proj_conv=m_w_proj_conv, w_out=m_w_out, ffn_norm=m_ffn_norm, w_gate_up=m_w_gate_up,
                 w_down=m_w_down, final_norm=m_final_norm)
    mom_v = dict(attn_norm=v_attn_norm, w_in=v_w_in, b_forget=v_b_forget, b_gate=v_b_gate, w_proj_attn=v_w_proj_attn,
                 pool_w=v_pool_w, pool_scale=v_pool_scale, w_proj_pool=v_w_proj_pool, conv_w=v_conv_w,
                 w_proj_conv=v_w_proj_conv, w_out=v_w_out, ffn_norm=v_ffn_norm, w_gate_up=v_w_gate_up,
                 w_down=v_w_down, final_norm=v_final_norm)

    Bl, S, D = x.shape
    T = Bl * S
    L = w_in.shape[0]
    F = w_down.shape[1] * N_CHIPS
    lay = _proj_layout(D)
    cst = _placement_constants()
    assert L == N_LAYERS and S % ATTN_BLOCK == 0 and F % (2 * LANES) == 0 and D % BRANCH_W == 0
    assert w_in.shape[2] * N_CHIPS == _REF["g"] + 3 * D and conv_w.shape[2] == GROUP_W

    send = {n: weights[n].astype(BF16) for n in BIG}
    send["conv_w"] = conv_w
    me_chip = 2 * lax.axis_index("x") + lax.axis_index("y")
    send["w_in"] = w_in.transpose(0, 2, 1).astype(BF16)
    pool = gather_buffers(send)
    gather_now(pool, [("w_in", 0)])
    rest = ("w_out", "w_proj_attn", "w_proj_pool", "w_gate_up", "w_proj_conv", "conv_w")
    late = ("w_out", "w_proj_attn", "w_proj_pool", "w_proj_conv", "conv_w")
    jobs = lambda kind, names, layer: [(kind, n, layer) for n in names]
    carried = {
        ("in_proj", 0): jobs("ici", rest, 0),
        ("attn_prep", 0): jobs("fwd", late, 0),
        ("attn_fwd", 0): jobs("fwd", ("w_gate_up",), 0) + jobs("ici", ("w_in",), 1) + jobs("ici", ("w_down",), 0),
        ("poolconv_fwd", 0): jobs("fwd", ("w_in",), 1) + jobs("fwd", ("w_down",), 0),
        ("mix_fwd", 0): jobs("ici", ("w_down",), 1),
        ("gate_up_proj", 0): jobs("ici", late, 1) + jobs("fwd", ("w_down",), 1),
        ("ffn_down_fwd", 0): jobs("ici", ("w_gate_up",), 1),
        ("in_proj", 1): jobs("fwd", ("w_gate_up",) + late, 1),
    }
    carry = lambda call, layer: Hosted(pool, carried[call, layer]) if (call, layer) in carried else None
    w_down_f = lambda: pool["w_down"].reshape(L, F, D)
    pool_w_b = pool_w.astype(BF16)
    an3, fn3 = attn_norm.reshape(L, 1, D), ffn_norm.reshape(L, 1, D)
    bg3, ps3 = b_gate.reshape(L, 1, 3 * D), pool_scale.reshape(L, 1, BRANCH_W)
    bf3 = jnp.pad(b_forget, ((0, 0), (0, LANES - HEADS))).reshape(L, 1, LANES)

    xs = x.reshape(T, D)
    saved = []
    w_in_p = []
    for l in range(L):
        w_in_p.append(pack_w_in(pool["w_in"], l))
        proj, h = norm_matmul(xs, an3, w_in_p[l], l, "rows", "in_proj", carry("in_proj", l))
        proj3 = proj.reshape(Bl, S, lay["width"])
        qa, ka = attn_prep(proj3, bf3, l, cst, lay, carry("attn_prep", l))
        ao, lse = attn_fwd(qa, ka, proj3, lay, carry("attn_fwd", l))
        po, co = poolconv_fwd(proj3, pool_w_b, ps3, pool["conv_w"], l, lay, carry("poolconv_fwd", l))
        ao2, po2, co2 = (a.reshape(T, BRANCH_W) for a in (ao, po, co))
        x1, ys, mixed = mix_fwd(ao2, po2, co2, proj, bg3, pool["w_proj_attn"], pool["w_proj_pool"],
                                pool["w_proj_conv"], pool["w_out"], l, xs, carry("mix_fwd", l))
        ab, h2 = norm_matmul(x1, fn3, pool["w_gate_up"], l, "by_shard", "gate_up_proj", carry("gate_up_proj", l))
        x2, s_act = ffn_down_fwd(ab, w_down_f(), l, x1, carry("ffn_down_fwd", l))
        saved.append(dict(x=xs, proj=proj, proj3=proj3, h=h, qa=qa, ka=ka, ao=ao, lse=lse, ao2=ao2, po2=po2,
                          co2=co2, ys=ys, mixed=mixed, x1=x1, ab=ab, h2=h2, s=s_act))
        xs = x2
    w_gu, w_o, conv_w_g = pool["w_gate_up"], pool["w_out"], pool["conv_w"]
    wpa, wpp, wpc = pool["w_proj_attn"], pool["w_proj_pool"], pool["w_proj_conv"]
    w_down_f = w_down_f()

    loss_row, dx, dxb, g_final = loss_head(xs, final_norm.reshape(1, D), loss_target.reshape(T, D))
    loss = lax.psum(loss_row[0, 0], AXES)

    reduced_names = tuple(n for n in BIG if n != "conv_w")
    early_names = tuple(n for n in reduced_names if n != "w_in")
    proj_names = ("w_out", "w_proj_attn", "w_proj_pool", "w_proj_conv")
    first_names = ("w_in", "w_gate_up", "w_down")
    where = jnp.stack([lax.axis_index("c"), me_chip]).astype(jnp.int32)
    rs = {}

    def reduce_begin(layer, grads):
        for n, g in grads.items():
            g5 = g.reshape((1, N_CHIPS, 2, -1) + g.shape[-1:])
            rs["g%d:%s" % (layer, n)] = g5
            for role in "ra":
                rs["%s%d:%s" % (role, layer, n)] = lax.empty((N_CHIPS,) + g5.shape[3:], BF16)

    swap_jobs = lambda layer, names: [("swap", "g%d:%s" % (layer, n), "r%d:%s" % (layer, n), 0) for n in names]
    xchg_jobs = lambda layer, names: [("xchg", "s%d:%s" % (layer, n), "a%d:%s" % (layer, n)) for n in names]
    join_jobs = lambda layer, names: [("join", "o:" + n, layer) for n in names]

    def pair_sums(layer, names):
        for n in names:
            rs["s%d:%s" % (layer, n)] = add_pair(rs["g%d:%s" % (layer, n)], 0, where, rs["r%d:%s" % (layer, n)],
                                                 "add_pair_" + n)

    def chip_sums(layer, names, slot, n_slots):
        for n in names:
            rs["o:" + n] = add_chips(rs["a%d:%s" % (layer, n)], rs["s%d:%s" % (layer, n)], slot, where, n_slots,
                                     rs.get("o:" + n), "add_chips_" + n)

    small = {n: [None] * L for n in REPLICATED if n != "final_norm"}
    g_conv = [None] * L
    to3 = lambda a: a.reshape(Bl, S, -1)
    for l in reversed(range(L)):
        sv = saved[l]
        behind = (lambda jobs: Hosted(rs, jobs)) if l == 0 else (lambda jobs: None)
        grads = {}
        da, db = ffn_down_bwd(dxb, w_down_f, l, sv["ab"], behind(swap_jobs(1, reduced_names)))
        if l == 0:
            pair_sums(1, reduced_names)
        grads["w_down"] = matmul_tn(sv["s"], [dxb], "grad_w_down", hosted=behind(xchg_jobs(1, ("w_down",))))
        grads["w_gate_up"] = matmul_tn(sv["h2"], [da, db], "grad_w_gate_up", by_dest=True, tn=2 * F // N_CHIPS,
                                       tk=_tile(T, (1024, 512, 256)), hosted=behind(xchg_jobs(1, ("w_gate_up",))))
        dx1, dx1b, g_fn = matmul_nt_normbwd([da, db], w_gu, l, "by_shard", sv["x1"], fn3, dx, "gate_up_bwd",
                                            behind(xchg_jobs(1, ("w_in",))))
        small["ffn_norm"][l] = g_fn[0]
        if l == 0:
            chip_sums(1, first_names, 1, L)
        dys, dproj, dao, dpo, dco, g_bg = mix_bwd(dx1b, w_o, sv["proj"], bg3, sv["ys"], wpa, wpp, wpc, l,
                                                  lay["width"],
                                                  behind(xchg_jobs(1, proj_names) + join_jobs(1, first_names)))
        if l == 0:
            chip_sums(1, proj_names, 1, L)
        small["b_gate"][l] = g_bg[0]
        grads["w_out"] = matmul_tn(sv["mixed"], [dx1b], "grad_w_out")
        for n, (name, br) in enumerate((("w_proj_attn", sv["ao2"]), ("w_proj_pool", sv["po2"]),
                                        ("w_proj_conv", sv["co2"]))):
            grads[name] = matmul_tn(br, [dys], "grad_" + name, b_col0=n * D, n_cols=D, by_dest=True,
                                    tn=D // N_CHIPS)
        if l == 0:
            reduce_begin(0, grads)
        dqa, dka, dproj3 = attn_bwd(sv["qa"], sv["ka"], sv["proj3"], to3(dao), sv["ao"], sv["lse"], to3(dproj), lay,
                                    behind(swap_jobs(0, early_names) + join_jobs(1, proj_names)))
        if l == 0:
            pair_sums(0, early_names)
        dproj3, g_bf = attn_post(dqa, dka, sv["proj3"], bf3, l, dproj3, cst, lay, behind(xchg_jobs(
            0, ("w_out", "w_proj_attn", "w_proj_pool", "w_proj_conv"))))
        small["b_forget"][l] = g_bf[0, :HEADS]
        dproj3, g_pw, g_ps, g_conv[l] = poolconv_bwd(sv["proj3"], to3(dpo), to3(dco), pool_w_b, ps3, conv_w_g, l,
                                                     dproj3, lay, behind(xchg_jobs(0, ("w_down",))))
        small["pool_w"][l], small["pool_scale"][l] = g_pw, g_ps[0]
        dproj = dproj3.reshape(T, lay["width"])
        g_w_in = unpack_w_in(matmul_tn(dproj, [sv["h"]], "grad_w_in", hosted=behind(xchg_jobs(
            0, ("w_gate_up",)))), w_in.shape[2])
        if l:
            reduce_begin(l, {**grads, "w_in": g_w_in})
        else:
            reduce_begin(0, {"w_in": g_w_in})
            comm_now(rs, [swap_jobs(0, ("w_in",))], "swap_w_in_halves")
            pair_sums(0, ("w_in",))
        dx, dxb, g_an = matmul_nt_normbwd([dproj], w_in_p[l], l, "rows", sv["x"], an3, dx1, "in_proj_bwd",
                                          behind(xchg_jobs(0, ("w_in",))))
        small["attn_norm"][l] = g_an[0]
    grad_x = dx.reshape(Bl, S, D)

    small_shapes = [weights[n].shape for n in REPLICATED] + [(L, N_CHIPS) + conv_w.shape[1:]]
    small_vec = jnp.concatenate([jnp.stack(small[n]).reshape(-1) for n in REPLICATED[:-1]]
                                + [g_final[0], jnp.stack(g_conv).reshape(-1)])
    n_small = small_vec.shape[0]
    small_vec = jnp.pad(small_vec, (0, -n_small % (2 * N_CHIPS * 16 * LANES))).astype(BF16)
    rs["g0:small"] = small_vec.reshape(1, N_CHIPS, 2, -1, LANES)
    for role in "ra":
        rs[role + "0:small"] = lax.empty((N_CHIPS,) + rs["g0:small"].shape[3:], BF16)
    last = ("small",)
    comm_now(rs, [swap_jobs(0, last)], "swap_grad_halves")
    pair_sums(0, last)
    comm_now(rs, [xchg_jobs(0, last)], "exchange_grad_chips")
    chip_sums(0, reduced_names, 0, L)
    chip_sums(0, ("small",), 0, 1)
    comm_now(rs, [join_jobs(0, reduced_names + ("small",))], "join_grad_halves")
    shard_grads = {n: rs["o:" + n].reshape((L, -1) + rs["o:" + n].shape[-1:]) for n in reduced_names}
    small_all = allgather_chips(rs["o:small"].reshape(-1, LANES), "allgather_small_grads").reshape(-1)[:n_small]
    *rep_list, conv_all = _split_flat(small_all, small_shapes)
    rep_grads = dict(zip(REPLICATED, rep_list))
    shard_grads["conv_w"] = lax.dynamic_index_in_dim(conv_all, me_chip, 1, keepdims=False)

    delta, new_m, new_v = {}, {}, {}
    for n in BIG:
        shp = weights[n].shape
        if n == "w_in":
            view, back = (lambda a: a.transpose(2, 0, 1)), (lambda a: a.transpose(1, 2, 0))
            g = shard_grads[n].transpose(1, 0, 2)
        else:
            view, back = (lambda a: a.reshape(-1, shp[-1])), (lambda a: a.reshape(shp))
            g = view(shard_grads[n])
        d, nm, nv = adamw(view(weights[n]), g, view(mom_m[n]), view(mom_v[n]), "adamw_" + n)
        delta[n], new_m[n], new_v[n], shard_grads[n] = back(d), back(nm), back(nv), back(g)

    def rows(d):
        vec = jnp.concatenate([d[n].reshape(-1) for n in REPLICATED])
        return jnp.pad(vec, (0, -vec.shape[0] % (8 * LANES))).reshape(-1, LANES)

    outs = adamw(rows(weights), rows(rep_grads), rows(mom_m), rows(mom_v), "adamw_replicated")
    for res, o in zip((delta, new_m, new_v), outs):
        res.update(zip(REPLICATED, _split_flat(o.reshape(-1), small_shapes[:len(REPLICATED)])))
    all_grads = {**shard_grads, **rep_grads}

    return (loss, grad_x, *[all_grads[n] for n in ORDER], *[delta[n] for n in ORDER],
            *[new_m[n] for n in ORDER], *[new_v[n] for n in ORDER])
```

```python
import numpy as np
import jax
import jax.numpy as jnp
from jax import lax
from jax.experimental import pallas as pl
from jax.experimental.pallas import tpu as pltpu

F32, BF16 = jnp.float32, jnp.bfloat16
SDS = jax.ShapeDtypeStruct
MESH = pl.DeviceIdType.MESH
AXES = ("x", "y", "c")
N_CHIPS = 4
N_LAYERS = 2
LANES = 128
VMEM_LIMIT = 48 * 1024 * 1024

HEADS, HEAD_DIM = 8, 64
HEAD_PAD = 128
BRANCH_W = 512
GROUP_W = 128
N_GROUPS = BRANCH_W // GROUP_W
POOL_WINDOWS = (2, 4, 8, 16)
F_PAD = 512
ATTN_BLOCK = 256
RMS_EPS = 1e-6
NEG_INF = -1e30
ADAM_LR, ADAM_B1, ADAM_B2, ADAM_EPS, ADAM_WD, ADAM_STEP = 0.001, 0.9, 0.999, 1e-08, 0.01, 10

NT = (((1,), (1,)), ((), ()))
TN = (((0,), (0,)), ((), ()))
_ANY = pl.BlockSpec(memory_space=pl.ANY)


def _tile(n, prefs):
    for p in prefs:
        if n % p == 0:
            return p
    raise ValueError(f"no tile of {prefs} divides {n}")


def _params(*sem):
    return pltpu.CompilerParams(dimension_semantics=sem, vmem_limit_bytes=VMEM_LIMIT)


def _sigmoid(z):
    return 0.5 * jnp.tanh(0.5 * z) + 0.5


def _split3(x):
    h1 = x.astype(BF16)
    r1 = x - h1.astype(F32)
    h2 = r1.astype(BF16)
    h3 = (r1 - h2.astype(F32)).astype(BF16)
    return h1, h2, h3


def _position():
    return lax.axis_index("x"), lax.axis_index("y"), lax.axis_index("c")


def _other_chips(x, y):
    return [(1 - x, y), (x, 1 - y), (1 - x, 1 - y)]


def _remote(src, dst, send_sem, recv_sem, device):
    return pltpu.make_async_remote_copy(src_ref=src, dst_ref=dst, send_sem=send_sem, recv_sem=recv_sem,
                                        device_id=device, device_id_type=MESH)


ROW_SHARDED = ("w_out", "w_down")
FETCHER = dict(w_in=0, w_out=0, w_proj_attn=0, w_proj_pool=0, w_gate_up=1, w_down=1, w_proj_conv=1, conv_w=1)


class Hosted:
    def __init__(self, pool, jobs):
        self.pool, self.jobs = pool, list(jobs)
        names = set()
        for job in self.jobs:
            names.update(job[1:3] if job[0] in ("swap", "xchg") else job[1:2])
        self.names = sorted(names)


def _hosted_plan(hosted, refs, send_sems, recv_sems):
    x, y, c = _position()
    me = 2 * x + y
    others = _other_chips(x, y)
    sibling = (x, y, 1 - c)
    plan = []
    for j, job in enumerate(hosted.jobs):
        kind = job[0]
        sems = lambda k, j=j: (send_sems.at[j, k], recv_sems.at[j, k])
        if kind in ("ici", "fwd"):
            _, name, layer = job
            ref = refs[name]
            win = (lambda chip, ref=ref, layer=layer: ref.at[layer, chip]) if name in ROW_SHARDED else (
                lambda chip, ref=ref, layer=layer: ref.at[chip, layer])
            mine = c == FETCHER[name]
            if kind == "ici":
                sends = [_remote(win(me), win(me), *sems(k), (px, py, c)) for k, (px, py) in enumerate(others)]
                arrivals = [_remote(win(2 * px + py), win(2 * px + py), *sems(k), (px, py, c))
                            for k, (px, py) in enumerate(others)]
                plan.append((mine, sends, arrivals, []))
            else:
                sends = [_remote(win(2 * px + py), win(2 * px + py), *sems(k), sibling)
                         for k, (px, py) in enumerate(others)]
                plan.append((mine, sends, [], sends))
        elif kind == "swap":
            _, src, dst, layer = job
            cp = _remote(refs[src].at[layer, :, 1 - c], refs[dst], *sems(0), sibling)
            plan.append((True, [cp], [cp], []))
        elif kind == "xchg":
            _, src, dst = job
            sends = [_remote(refs[src].at[2 * px + py], refs[dst].at[me], *sems(k), (px, py, c))
                     for k, (px, py) in enumerate(others)]
            arrivals = [_remote(refs[src].at[me], refs[dst].at[2 * px + py], *sems(k), (px, py, c))
                        for k, (px, py) in enumerate(others)]
            plan.append((True, sends, arrivals, []))
        else:
            _, name, layer = job
            ref = refs[name]
            cp = _remote(ref.at[layer, c], ref.at[layer, c], *sems(0), sibling)
            arrival = _remote(ref.at[layer, c], ref.at[layer, 1 - c], *sems(0), sibling)
            plan.append((True, [cp], [arrival], []))
    return plan


def _hosted_start(plan, now):
    for mine, sends, _, _ in plan:
        @pl.when(now & mine)
        def _(sends=sends):
            for cp in sends:
                cp.start()


def _hosted_finish(plan, now):
    for mine, sends, arrivals, sibling_arrivals in plan:
        @pl.when(now & mine)
        def _(sends=sends, arrivals=arrivals):
            for cp in arrivals:
                cp.wait_recv()
            for cp in sends:
                cp.wait_send()

        if sibling_arrivals:
            @pl.when(now & jnp.logical_not(mine))
            def _(sibling_arrivals=sibling_arrivals):
                for cp in sibling_arrivals:
                    cp.wait_recv()


def _pcall(body, hosted, *, name, grid, in_specs, out_specs, out_shape, semantics, scratch_shapes=(), aliases=None):
    aliases = dict(aliases or {})
    if hosted is None or not hosted.jobs:
        return pl.pallas_call(body, name=name, grid=grid, in_specs=in_specs, out_specs=out_specs,
                              out_shape=out_shape, scratch_shapes=list(scratch_shapes),
                              input_output_aliases=aliases, compiler_params=_params(*semantics))
    single = not isinstance(out_shape, (list, tuple))
    out_specs_l = [out_specs] if single else list(out_specs)
    out_shape_l = [out_shape] if single else list(out_shape)
    n_in, n_out, n_buf, n_job = len(in_specs), len(out_specs_l), len(hosted.names), len(hosted.jobs)

    def carrying(*refs):
        ins, outs = refs[:n_in], refs[n_in + n_buf:n_in + n_buf + n_out]
        bufs = refs[n_in + n_buf + n_out:n_in + 2 * n_buf + n_out]
        rest = refs[n_in + 2 * n_buf + n_out:]
        scratch, send_sems, recv_sems = rest[:-2], rest[-2], rest[-1]
        first, last = True, True
        for axis, size in enumerate(grid):
            first = first & (pl.program_id(axis) == 0)
            last = last & (pl.program_id(axis) == size - 1)
        plan = _hosted_plan(hosted, dict(zip(hosted.names, bufs)), send_sems, recv_sems)
        _hosted_start(plan, first)
        body(*ins, *outs, *scratch)
        _hosted_finish(plan, last)

    def run(*args):
        bufs = [hosted.pool[n] for n in hosted.names]
        sem = pltpu.SemaphoreType.DMA
        res = pl.pallas_call(
            carrying, name=name, grid=grid, in_specs=list(in_specs) + [_ANY] * n_buf,
            out_specs=out_specs_l + [_ANY] * n_buf,
            out_shape=out_shape_l + [SDS(b.shape, b.dtype) for b in bufs],
            scratch_shapes=list(scratch_shapes) + [sem((n_job, 3)), sem((n_job, 3))],
            input_output_aliases={**aliases, **{n_in + i: n_out + i for i in range(n_buf)}},
            compiler_params=pltpu.CompilerParams(dimension_semantics=semantics, vmem_limit_bytes=VMEM_LIMIT,
                                                 has_side_effects=True),
        )(*args, *bufs)
        hosted.pool.update(zip(hosted.names, res[n_out:]))
        return res[0] if single else res[:n_out]

    return run


def _dot(a, b):
    return jnp.dot(a, b, preferred_element_type=F32)


def _dot_nt(a, b):
    return lax.dot_general(a, b, NT, preferred_element_type=F32)


def _dot_tn(a, b):
    return lax.dot_general(a, b, TN, preferred_element_type=F32)


def norm_matmul(x, gain, w, layer, kind, name, hosted=None):
    T, D = x.shape
    if kind == "by_shard":
        tn = w.shape[3]
        N = N_CHIPS * tn
        w_spec = pl.BlockSpec((None, None, D, tn), lambda i, j: (j, layer, 0, 0))
        mm = _dot
    else:
        N = w.shape[0]
        tn = _tile(N, (1024, 512, 256, 128))
        w_spec = pl.BlockSpec((tn, D), lambda i, j: (j, 0))
        mm = _dot_nt
    tm = _tile(T, (1024, 512, 256, 128))

    def body(x_ref, g_ref, w_ref, y_ref, h_ref):
        @pl.when(pl.program_id(1) == 0)
        def _():
            xf = x_ref[...]
            r = lax.rsqrt(jnp.mean(xf * xf, axis=-1, keepdims=True) + RMS_EPS)
            h_ref[...] = ((xf * r) * g_ref[...]).astype(BF16)

        y_ref[...] = mm(h_ref[...], w_ref[...]).astype(BF16)

    return _pcall(
        body, hosted, name=name, grid=(T // tm, N // tn),
        in_specs=[pl.BlockSpec((tm, D), lambda i, j: (i, 0)),
                  pl.BlockSpec((None, 1, D), lambda i, j: (layer, 0, 0)),
                  w_spec],
        out_specs=[pl.BlockSpec((tm, tn), lambda i, j: (i, j)),
                   pl.BlockSpec((tm, D), lambda i, j: (i, 0))],
        out_shape=[SDS((T, N), BF16), SDS((T, D), BF16)],
        semantics=("arbitrary", "arbitrary"),
    )(x, gain, w)


def matmul_nt_normbwd(dys, w, layer, kind, x, gain, dres, name, hosted=None):
    T, D = x.shape
    width = dys[0].shape[1]
    if kind == "by_shard":
        tk = w.shape[3]
        w_spec = pl.BlockSpec((None, None, D, tk), lambda i, k: (k, layer, 0, 0))
        mm = _dot_nt
    else:
        tk = _tile(width, (3584, 1024, 512, 256, 128))
        w_spec = pl.BlockSpec((tk, D), lambda i, k: (k, 0))
        mm = _dot
    per = width // tk
    nk = per * len(dys)
    tm = _tile(T, (512, 256, 128))
    n_dy = len(dys)

    def dy_spec(p):
        return pl.BlockSpec((tm, tk), lambda i, k: (i, jnp.clip(k - p * per, 0, per - 1)))

    def body(*refs):
        dy_refs = refs[:n_dy]
        w_ref, x_ref, g_ref, dres_ref, dx_ref, dxb_ref, dg_ref, acc_ref = refs[n_dy:]
        i, k = pl.program_id(0), pl.program_id(1)

        @pl.when(k == 0)
        def _():
            acc_ref[...] = jnp.zeros_like(acc_ref)

        for p in range(n_dy):
            @pl.when((k >= p * per) & (k < (p + 1) * per))
            def _(p=p):
                acc_ref[...] += mm(dy_refs[p][...], w_ref[...])

        @pl.when(k == nk - 1)
        def _():
            xf = x_ref[...]
            r = lax.rsqrt(jnp.mean(xf * xf, axis=-1, keepdims=True) + RMS_EPS)
            xhat = xf * r
            dh = acc_ref[...]
            dhg = dh * g_ref[...]
            dx = dres_ref[...] + r * (dhg - xhat * jnp.mean(dhg * xhat, axis=-1, keepdims=True))
            dx_ref[...] = dx
            dxb_ref[...] = dx.astype(BF16)
            part = jnp.sum(dh * xhat, axis=0, keepdims=True)

            @pl.when(i == 0)
            def _():
                dg_ref[...] = part

            @pl.when(i > 0)
            def _():
                dg_ref[...] += part

    row = pl.BlockSpec((tm, D), lambda i, k: (i, 0))
    return _pcall(
        body, hosted, name=name, grid=(T // tm, nk),
        in_specs=[dy_spec(p) for p in range(n_dy)] + [
            w_spec, row, pl.BlockSpec((None, 1, D), lambda i, k: (layer, 0, 0)), row],
        out_specs=[row, row, pl.BlockSpec((1, D), lambda i, k: (0, 0))],
        out_shape=[SDS((T, D), F32), SDS((T, D), BF16), SDS((1, D), F32)],
        scratch_shapes=[pltpu.VMEM((tm, D), F32)],
        semantics=("arbitrary", "arbitrary"),
    )(*dys, w, x, gain, dres)


def matmul_tn(a, bs, name, b_col0=0, n_cols=None, by_dest=False, tn=None, tk=None, hosted=None):
    T, M = a.shape
    width = bs[0].shape[1]
    N = n_cols if n_cols else width * len(bs)
    tm = _tile(M, (1408, 1024, 512, 256, 128))
    tn = tn or _tile(N, (512, 256, 128))
    tk = tk or _tile(T, (4096, 2048, 1024, 512, 256))
    assert b_col0 % tn == 0 and width % tn == 0
    j0, per, nk, n_b = b_col0 // tn, width // tn, T // tk, len(bs)

    def b_spec(p):
        return pl.BlockSpec((tk, tn), lambda i, j, k: (k, jnp.clip(j0 + j - p * per, 0, per - 1)))

    def body(*refs):
        a_ref, b_refs = refs[0], refs[1:1 + n_b]
        o_ref, acc_ref = refs[-2], refs[-1]
        j, k = pl.program_id(1), pl.program_id(2)

        @pl.when(k == 0)
        def _():
            acc_ref[...] = jnp.zeros_like(acc_ref)

        for p in range(n_b):
            @pl.when((j0 + j >= p * per) & (j0 + j < (p + 1) * per))
            def _(p=p):
                acc_ref[...] += _dot_tn(a_ref[...], b_refs[p][...])

        @pl.when(k == nk - 1)
        def _():
            o_ref[...] = acc_ref[...].astype(BF16)

    if by_dest:
        cs = N // N_CHIPS
        npd = cs // tn
        out_shape = SDS((N_CHIPS, M, cs), BF16)
        out_spec = pl.BlockSpec((None, tm, tn), lambda i, j, k: (j // npd, i, j % npd))
    else:
        out_shape = SDS((M, N), BF16)
        out_spec = pl.BlockSpec((tm, tn), lambda i, j, k: (i, j))
    return _pcall(
        body, hosted, name=name, grid=(M // tm, N // tn, nk),
        in_specs=[pl.BlockSpec((tk, tm), lambda i, j, k: (k, i))] + [b_spec(p) for p in range(n_b)],
        out_specs=out_spec, out_shape=out_shape,
        scratch_shapes=[pltpu.VMEM((tm, tn), F32)],
        semantics=("arbitrary", "arbitrary", "arbitrary"),
    )(a, *bs)


def ffn_down_fwd(ab, w_down, layer, x1, hosted=None):
    T, D = x1.shape
    F = w_down.shape[1]
    tm = _tile(T, (512, 256, 128))
    tk = F // 2
    nk = F // tk

    def body(a_ref, b_ref, w_ref, x_ref, x2_ref, s_ref, acc_ref):
        k = pl.program_id(1)

        @pl.when(k == 0)
        def _():
            acc_ref[...] = x_ref[...]

        a = a_ref[...].astype(F32)
        s = (a * _sigmoid(a) * b_ref[...].astype(F32)).astype(BF16)
        s_ref[...] = s
        acc_ref[...] += _dot(s, w_ref[...])

        @pl.when(k == nk - 1)
        def _():
            x2_ref[...] = acc_ref[...]

    return _pcall(
        body, hosted, name="ffn_down_fwd", grid=(T // tm, nk),
        in_specs=[pl.BlockSpec((tm, tk), lambda i, k: (i, k)),
                  pl.BlockSpec((tm, tk), lambda i, k: (i, nk + k)),
                  pl.BlockSpec((None, tk, D), lambda i, k: (layer, k, 0)),
                  pl.BlockSpec((tm, D), lambda i, k: (i, 0))],
        out_specs=[pl.BlockSpec((tm, D), lambda i, k: (i, 0)),
                   pl.BlockSpec((tm, tk), lambda i, k: (i, k))],
        out_shape=[SDS((T, D), F32), SDS((T, F), BF16)],
        scratch_shapes=[pltpu.VMEM((tm, D), F32)],
        semantics=("arbitrary", "arbitrary"),
    )(ab, ab, w_down, x1)


def ffn_down_bwd(dx2b, w_down, layer, ab, hosted=None):
    T, D = dx2b.shape
    F = w_down.shape[1]
    tm = _tile(T, (512, 256, 128))
    tn = F // 2
    nj = F // tn

    def body(dx_ref, w_ref, a_ref, b_ref, da_ref, db_ref):
        ds = _dot_nt(dx_ref[...], w_ref[...])
        a = a_ref[...].astype(F32)
        sg = _sigmoid(a)
        da_ref[...] = (ds * b_ref[...].astype(F32) * (sg * (1.0 + a * (1.0 - sg)))).astype(BF16)
        db_ref[...] = (ds * (a * sg)).astype(BF16)

    blk = pl.BlockSpec((tm, tn), lambda i, j: (i, j))
    return _pcall(
        body, hosted, name="ffn_down_bwd", grid=(T // tm, nj),
        in_specs=[pl.BlockSpec((tm, D), lambda i, j: (i, 0)),
                  pl.BlockSpec((None, tn, D), lambda i, j: (layer, j, 0)),
                  blk, pl.BlockSpec((tm, tn), lambda i, j: (i, nj + j))],
        out_specs=[blk, blk],
        out_shape=[SDS((T, F), BF16), SDS((T, F), BF16)],
        semantics=("arbitrary", "arbitrary"),
    )(dx2b, w_down, ab, ab)


def _mix_specs(tm, D, layer):
    cs = D // N_CHIPS
    row = lambda w: pl.BlockSpec((tm, w), lambda i: (i, 0))
    wp = pl.BlockSpec((N_CHIPS, None, BRANCH_W, cs), lambda i: (0, layer, 0, 0))
    wo = pl.BlockSpec((None, N_CHIPS, cs, D), lambda i: (layer, 0, 0, 0))
    bg = pl.BlockSpec((None, 1, 3 * D), lambda i: (layer, 0, 0))
    return row, wp, wo, bg


def mix_fwd(ao, po, co, proj, b_gate, wpa, wpp, wpc, w_out, layer, x, hosted=None):
    T, D = x.shape
    cs = D // N_CHIPS
    tm = _tile(T, (256, 128))
    row, wp, wo, bg = _mix_specs(tm, D, layer)

    def body(ao_ref, po_ref, co_ref, g_ref, bg_ref, wpa_ref, wpp_ref, wpc_ref, wo_ref, x_ref,
             x1_ref, ys_ref, mixed_ref):
        mixed = jnp.zeros((tm, D), F32)
        for n, (br, wp_ref) in enumerate(((ao_ref, wpa_ref), (po_ref, wpp_ref), (co_ref, wpc_ref))):
            y = jnp.concatenate([_dot(br[...], wp_ref[j]) for j in range(N_CHIPS)], axis=1)
            cols = slice(n * D, (n + 1) * D)
            gate = _sigmoid(g_ref[:, cols].astype(F32) + bg_ref[:, cols])
            ys_ref[:, cols] = y.astype(BF16)
            mixed = mixed + gate * y
        mb = mixed.astype(BF16)
        mixed_ref[...] = mb
        acc = x_ref[...]
        for j in range(N_CHIPS):
            acc = acc + _dot(mb[:, j * cs:(j + 1) * cs], wo_ref[j])
        x1_ref[...] = acc

    return _pcall(
        body, hosted, name="mix_fwd", grid=(T // tm,),
        in_specs=[row(BRANCH_W), row(BRANCH_W), row(BRANCH_W), row(3 * D), bg, wp, wp, wp, wo, row(D)],
        out_specs=[row(D), row(3 * D), row(D)],
        out_shape=[SDS((T, D), F32), SDS((T, 3 * D), BF16), SDS((T, D), BF16)],
        semantics=("arbitrary",),
    )(ao, po, co, proj, b_gate, wpa, wpp, wpc, w_out, x)


def mix_bwd(dx1b, w_out, proj, b_gate, ys, wpa, wpp, wpc, layer, width, hosted=None):
    T, D = dx1b.shape
    cs = D // N_CHIPS
    tm = _tile(T, (256, 128))
    row, wp, wo, bg = _mix_specs(tm, D, layer)

    def body(dx_ref, wo_ref, g_ref, bg_ref, ys_ref, wpa_ref, wpp_ref, wpc_ref,
             dys_ref, dg_ref, dao_ref, dpo_ref, dco_ref, dbg_ref):
        i = pl.program_id(0)
        dx = dx_ref[...]
        dmixed = jnp.concatenate([_dot_nt(dx, wo_ref[j]) for j in range(N_CHIPS)], axis=1)
        for n, (wp_ref, dbr) in enumerate(((wpa_ref, dao_ref), (wpp_ref, dpo_ref), (wpc_ref, dco_ref))):
            cols = slice(n * D, (n + 1) * D)
            gate = _sigmoid(g_ref[:, cols].astype(F32) + bg_ref[:, cols])
            dy = (dmixed * gate).astype(BF16)
            dys_ref[:, cols] = dy
            dgp = dmixed * ys_ref[:, cols].astype(F32) * gate * (1.0 - gate)
            dg_ref[:, cols] = dgp.astype(BF16)
            part = jnp.sum(dgp, axis=0, keepdims=True)

            @pl.when(i == 0)
            def _():
                dbg_ref[:, cols] = part

            @pl.when(i > 0)
            def _():
                dbg_ref[:, cols] += part

            acc = jnp.zeros((tm, BRANCH_W), F32)
            for j in range(N_CHIPS):
                acc = acc + _dot_nt(dy[:, j * cs:(j + 1) * cs], wp_ref[j])
            dbr[...] = acc.astype(BF16)

    return _pcall(
        body, hosted, name="mix_bwd", grid=(T // tm,),
        in_specs=[row(D), wo, row(3 * D), bg, row(3 * D), wp, wp, wp],
        out_specs=[row(3 * D), row(3 * D), row(BRANCH_W), row(BRANCH_W), row(BRANCH_W),
                   pl.BlockSpec((1, 3 * D), lambda i: (0, 0))],
        out_shape=[SDS((T, 3 * D), BF16), SDS((T, width), BF16), SDS((T, BRANCH_W), BF16),
                   SDS((T, BRANCH_W), BF16), SDS((T, BRANCH_W), BF16), SDS((1, 3 * D), F32)],
        semantics=("arbitrary",),
    )(dx1b, w_out, proj, b_gate, ys, wpa, wpp, wpc)


def loss_head(x2, gain, target):
    T, D = x2.shape
    tm = _tile(T, (512, 256, 128))

    def body(x_ref, g_ref, t_ref, loss_ref, dx_ref, dxb_ref, dg_ref):
        i = pl.program_id(0)
        xf = x_ref[...]
        g = g_ref[...]
        r = lax.rsqrt(jnp.mean(xf * xf, axis=-1, keepdims=True) + RMS_EPS)
        xhat = xf * r
        diff = xhat * g - t_ref[...]
        part_loss = 0.5 * jnp.sum(jnp.mean(diff * diff, axis=-1, keepdims=True), axis=0, keepdims=True)
        dy = diff * (1.0 / D)
        dhg = dy * g
        dx = r * (dhg - xhat * jnp.mean(dhg * xhat, axis=-1, keepdims=True))
        dx_ref[...] = dx
        dxb_ref[...] = dx.astype(BF16)
        part_g = jnp.sum(dy * xhat, axis=0, keepdims=True)
        part_l = jnp.broadcast_to(part_loss, (1, LANES))

        @pl.when(i == 0)
        def _():
            dg_ref[...] = part_g
            loss_ref[...] = part_l

        @pl.when(i > 0)
        def _():
            dg_ref[...] += part_g
            loss_ref[...] += part_l

    row = pl.BlockSpec((tm, D), lambda i: (i, 0))
    return pl.pallas_call(
        body, name="loss_head", grid=(T // tm,),
        in_specs=[row, pl.BlockSpec((1, D), lambda i: (0, 0)), row],
        out_specs=[pl.BlockSpec((1, LANES), lambda i: (0, 0)), row, row, pl.BlockSpec((1, D), lambda i: (0, 0))],
        out_shape=[SDS((1, LANES), F32), SDS((T, D), F32), SDS((T, D), BF16), SDS((1, D), F32)],
        compiler_params=_params("arbitrary"),
    )(x2, gain, target)


def _placement_constants():
    w = HEADS * HEAD_PAD
    pq = np.zeros((BRANCH_W, w), np.float32)
    pk = np.zeros((BRANCH_W, w), np.float32)
    pfq = np.zeros((3, LANES, w), np.float32)
    pfk = np.zeros((3, LANES, w), np.float32)
    cq = np.zeros((1, w), np.float32)
    ck = np.zeros((1, w), np.float32)
    eq = np.zeros((w, LANES), np.float32)
    ek = np.zeros((w, LANES), np.float32)
    for h in range(HEADS):
        for d in range(HEAD_DIM):
            pq[h * HEAD_DIM + d, h * HEAD_PAD + d] = HEAD_DIM ** -0.5
            pk[h * HEAD_DIM + d, h * HEAD_PAD + d] = 1.0
        for i in range(3):
            pfq[i, h, h * HEAD_PAD + HEAD_DIM + i] = 1.0
            pfk[i, h, h * HEAD_PAD + HEAD_DIM + 3 + i] = -1.0
            cq[0, h * HEAD_PAD + HEAD_DIM + 3 + i] = 1.0
            ck[0, h * HEAD_PAD + HEAD_DIM + i] = 1.0
        eq[h * HEAD_PAD + HEAD_DIM, h] = 1.0
        ek[h * HEAD_PAD + HEAD_DIM + 3, h] = -1.0
    bf = lambda a: jnp.asarray(a, BF16)
    return dict(pq=bf(pq), pk=bf(pk), pfq=bf(pfq), pfk=bf(pfk), cq=jnp.asarray(cq), ck=jnp.asarray(ck),
                pqkt=bf(np.concatenate([pq.T, pk.T], axis=0)), eq=bf(eq), ek=bf(ek))


def attn_prep(proj3, bf_rows, layer, cst, lay, hosted=None):
    Bl, S, _ = proj3.shape
    ts = ATTN_BLOCK
    w = HEADS * HEAD_PAD

    def body(q_ref, k_ref, f_ref, bf_ref, pq_ref, pk_ref, pfq_ref, pfk_ref, cq_ref, ck_ref,
             qa_ref, ka_ref, carry_ref):
        @pl.when(pl.program_id(1) == 0)
        def _():
            carry_ref[...] = jnp.zeros_like(carry_ref)

        z = f_ref[...].astype(F32) + bf_ref[...]
        logf = jnp.minimum(z, 0.0) - jnp.log(1.0 + jnp.exp(-jnp.abs(z)))
        r = lax.broadcasted_iota(jnp.int32, (ts, ts), 0)
        c = lax.broadcasted_iota(jnp.int32, (ts, ts), 1)
        tri = jnp.where(r >= c, 1.0, 0.0).astype(BF16)
        fcum = carry_ref[...]
        for part in _split3(logf):
            fcum = fcum + _dot(tri, part)
        carry_ref[...] = fcum[ts - 1:ts, :]
        qa = _dot(q_ref[...], pq_ref[...]) + cq_ref[...]
        ka = _dot(k_ref[...], pk_ref[...]) + ck_ref[...]
        for i, part in enumerate(_split3(fcum)):
            qa = qa + _dot(part, pfq_ref[i])
            ka = ka + _dot(part, pfk_ref[i])
        qa_ref[...] = qa.astype(BF16)
        ka_ref[...] = ka.astype(BF16)

    cfull = lambda shape: pl.BlockSpec(shape, lambda b, s: (0,) * len(shape))
    return _pcall(
        body, hosted, name="attn_prep", grid=(Bl, S // ts),
        in_specs=[pl.BlockSpec((None, ts, BRANCH_W), lambda b, s: (b, s, lay["q"] // BRANCH_W)),
                  pl.BlockSpec((None, ts, BRANCH_W), lambda b, s: (b, s, lay["k"] // BRANCH_W)),
                  pl.BlockSpec((None, ts, LANES), lambda b, s: (b, s, lay["f"] // LANES)),
                  pl.BlockSpec((None, 1, LANES), lambda b, s: (layer, 0, 0)),
                  cfull((BRANCH_W, w)), cfull((BRANCH_W, w)),
                  cfull((3, LANES, w)), cfull((3, LANES, w)), cfull((1, w)), cfull((1, w))],
        out_specs=[pl.BlockSpec((None, ts, w), lambda b, s: (b, s, 0)),
                   pl.BlockSpec((None, ts, w), lambda b, s: (b, s, 0))],
        out_shape=[SDS((Bl, S, w), BF16), SDS((Bl, S, w), BF16)],
        scratch_shapes=[pltpu.VMEM((1, LANES), F32)],
        semantics=("arbitrary", "arbitrary"),
    )(proj3, proj3, proj3, bf_rows, cst["pq"], cst["pk"], cst["pfq"], cst["pfk"], cst["cq"], cst["ck"])


def attn_fwd(qa, ka, proj3, lay, hosted=None):
    Bl, S, _ = qa.shape
    tq = ATTN_BLOCK
    nq = S // tq
    pairs = HEADS // 2
    pw = 2 * HEAD_PAD
    vw = 2 * HEAD_DIM

    def body(qa_ref, ka_ref, v_ref, o_ref, lse_ref):
        row = lax.broadcasted_iota(jnp.int32, (tq, tq), 0)
        col = lax.broadcasted_iota(jnp.int32, (tq, tq), 1)
        causal = row <= col
        for i in range(nq):
            nk = (i + 1) * tq
            rows = slice(i * tq, nk)
            o_t = []
            for h in range(2):
                hs = slice(h * HEAD_PAD, (h + 1) * HEAD_PAD)
                st = _dot_nt(ka_ref[0:nk, hs], qa_ref[rows, hs])
                diag = jnp.where(causal, st[nk - tq:], NEG_INF)
                m = jnp.max(diag, axis=0, keepdims=True)
                if i:
                    m = jnp.maximum(m, jnp.max(st[:nk - tq], axis=0, keepdims=True))
                p_diag = jnp.exp(diag - m)
                l = jnp.sum(p_diag, axis=0, keepdims=True)
                if i:
                    p_top = jnp.exp(st[:nk - tq] - m)
                    l = l + jnp.sum(p_top, axis=0, keepdims=True)
                    p = jnp.concatenate([p_top.astype(BF16), p_diag.astype(BF16)], axis=0)
                else:
                    p = p_diag.astype(BF16)
                acc = _dot_tn(v_ref[0:nk, :], p)
                o_t.append(acc[h * HEAD_DIM:(h + 1) * HEAD_DIM, :] / l)
                lse_ref[h:h + 1, rows] = m + jnp.log(l)
            o_ref[rows, :] = jnp.concatenate(o_t, axis=0).T.astype(BF16)

    return _pcall(
        body, hosted, name="attn_fwd", grid=(Bl, pairs),
        in_specs=[pl.BlockSpec((None, S, pw), lambda b, p: (b, 0, p)),
                  pl.BlockSpec((None, S, pw), lambda b, p: (b, 0, p)),
                  pl.BlockSpec((None, S, vw), lambda b, p: (b, 0, lay["v"] // vw + p))],
        out_specs=[pl.BlockSpec((None, S, vw), lambda b, p: (b, 0, p)),
                   pl.BlockSpec((None, None, 2, S), lambda b, p: (b, p, 0, 0))],
        out_shape=[SDS((Bl, S, BRANCH_W), BF16), SDS((Bl, pairs, 2, S), F32)],
        semantics=("arbitrary", "arbitrary"),
    )(qa, ka, proj3)


def attn_bwd(qa, ka, proj3, dao, ao, lse, dproj3, lay, hosted=None):
    Bl, S, _ = qa.shape
    tk = ATTN_BLOCK
    nq = S // tk
    pairs = HEADS // 2
    pw = 2 * HEAD_PAD
    vw = 2 * HEAD_DIM

    def body(qa_ref, ka_ref, v_ref, do_ref, o_ref, lse_ref, _, dqa_ref, dka_ref, dv_ref):
        row = lax.broadcasted_iota(jnp.int32, (tk, tk), 0)
        col = lax.broadcasted_iota(jnp.int32, (tk, tk), 1)
        causal = row <= col
        lane8 = lax.broadcasted_iota(jnp.int32, (8, vw), 1)
        lane_s = lax.broadcasted_iota(jnp.int32, (S, vw), 1)
        lane_k = lax.broadcasted_iota(jnp.int32, (tk, vw), 1)
        doo = do_ref[...].astype(F32) * o_ref[...].astype(F32)
        hi = doo.astype(BF16)
        lo = (doo - hi.astype(F32)).astype(BF16)
        delta, v_head = [], []
        for h in range(2):
            sel = jnp.where((lane8 >= h * HEAD_DIM) & (lane8 < (h + 1) * HEAD_DIM), 1.0, 0.0).astype(BF16)
            delta.append((_dot_nt(sel, hi) + _dot_nt(sel, lo))[0:1, :])
            in_head = (lane_s >= h * HEAD_DIM) & (lane_s < (h + 1) * HEAD_DIM)
            v_head.append(jnp.where(in_head, v_ref[...], jnp.zeros_like(v_ref[...])))
        dqa_ref[...] = jnp.zeros_like(dqa_ref)
        for j in range(nq):
            q0 = j * tk
            krows = slice(q0, q0 + tk)
            do = do_ref[q0:, :]
            dvs = []
            for h in range(2):
                hs = slice(h * HEAD_PAD, (h + 1) * HEAD_PAD)
                k = ka_ref[krows, hs]
                q = qa_ref[q0:, hs]
                st = _dot_nt(k, q)
                p = jnp.exp(st - lse_ref[h:h + 1, q0:])
                p_diag = jnp.where(causal, p[:, :tk], 0.0)
                p = jnp.concatenate([p_diag, p[:, tk:]], axis=1) if j < nq - 1 else p_diag
                dvs.append(_dot(p.astype(BF16), do))
                dpt = _dot_nt(v_head[h][krows, :], do)
                ds = (p * (dpt - delta[h][:, q0:])).astype(BF16)
                dka_ref[krows, hs] = _dot(ds, q)
                dqa_ref[q0:, hs] += _dot_tn(ds, k)
            dv_ref[krows, :] = jnp.where(lane_k < HEAD_DIM, dvs[0], dvs[1]).astype(BF16)

    seq = lambda w, c0=0: pl.BlockSpec((None, S, w), lambda b, p: (b, 0, c0 + p))
    return _pcall(
        body, hosted, name="attn_bwd", grid=(Bl, pairs),
        in_specs=[seq(pw), seq(pw), seq(vw, lay["v"] // vw), seq(vw), seq(vw),
                  pl.BlockSpec((None, None, 2, S), lambda b, p: (b, p, 0, 0)), _ANY],
        out_specs=[seq(pw), seq(pw), seq(vw, lay["v"] // vw)],
        out_shape=[SDS((Bl, S, HEADS * HEAD_PAD), F32), SDS((Bl, S, HEADS * HEAD_PAD), F32),
                   SDS(dproj3.shape, BF16)],
        aliases={6: 2}, semantics=("arbitrary", "arbitrary"),
    )(qa, ka, proj3, dao, ao, lse, dproj3)


def attn_post(dqa, dka, proj3, bf_rows, layer, dproj3, cst, lay, hosted=None):
    Bl, S, w = dqa.shape
    ts = ATTN_BLOCK
    ns = S // ts
    qkf = 2 * BRANCH_W + F_PAD

    def body(dqa_ref, dka_ref, f_ref, bf_ref, pqkt_ref, eq_ref, ek_ref, _, dqkf_ref, dbf_ref, carry_ref):
        b, s = pl.program_id(0), pl.program_id(1)

        @pl.when(s == 0)
        def _():
            carry_ref[...] = jnp.zeros_like(carry_ref)

        dqa_v, dka_v = dqa_ref[...], dka_ref[...]
        qh = dqa_v.astype(BF16)
        kh = dka_v.astype(BF16)
        dqkf_ref[:, :BRANCH_W] = _dot(qh, pqkt_ref[:w, :]).astype(BF16)
        dqkf_ref[:, BRANCH_W:2 * BRANCH_W] = _dot(kh, pqkt_ref[w:, :]).astype(BF16)
        ql = (dqa_v - qh.astype(F32)).astype(BF16)
        kl = (dka_v - kh.astype(F32)).astype(BF16)
        d_f = (_dot(qh, eq_ref[...]) + _dot(ql, eq_ref[...])) + (_dot(kh, ek_ref[...]) + _dot(kl, ek_ref[...]))
        r = lax.broadcasted_iota(jnp.int32, (ts, ts), 0)
        c = lax.broadcasted_iota(jnp.int32, (ts, ts), 1)
        triu = jnp.where(c >= r, 1.0, 0.0).astype(BF16)
        rev = carry_ref[...]
        for part in _split3(d_f):
            rev = rev + _dot(triu, part)
        carry_ref[...] = rev[0:1, :]
        z = f_ref[...].astype(F32) + bf_ref[...]
        lane = lax.broadcasted_iota(jnp.int32, (ts, LANES), 1)
        dfl = jnp.where(lane < HEADS, rev / (1.0 + jnp.exp(z)), 0.0)
        dqkf_ref[:, 2 * BRANCH_W:] = jnp.concatenate(
            [dfl.astype(BF16), jnp.zeros((ts, F_PAD - LANES), BF16)], axis=1)
        part = jnp.sum(dfl, axis=0, keepdims=True)

        @pl.when((b == 0) & (s == 0))
        def _():
            dbf_ref[...] = part

        @pl.when((b > 0) | (s > 0))
        def _():
            dbf_ref[...] += part

    assert lay["q"] % qkf == 0
    cfull = lambda shape: pl.BlockSpec(shape, lambda b, s: (0,) * len(shape))
    rev_blk = lambda wd, c0=0: pl.BlockSpec((None, ts, wd), lambda b, s: (b, ns - 1 - s, c0))
    return _pcall(
        body, hosted, name="attn_post", grid=(Bl, ns),
        in_specs=[rev_blk(w), rev_blk(w), rev_blk(LANES, lay["f"] // LANES),
                  pl.BlockSpec((None, 1, LANES), lambda b, s: (layer, 0, 0)),
                  cfull((2 * w, BRANCH_W)), cfull((w, LANES)), cfull((w, LANES)), _ANY],
        out_specs=[rev_blk(qkf, lay["q"] // qkf), cfull((1, LANES))],
        out_shape=[SDS(dproj3.shape, BF16), SDS((1, LANES), F32)],
        scratch_shapes=[pltpu.VMEM((1, LANES), F32)],
        aliases={7: 0}, semantics=("arbitrary", "arbitrary"),
    )(dqa, dka, proj3, bf_rows, cst["pqkt"], cst["eq"], cst["ek"], dproj3)


def _shift_down(x, k, row):
    return jnp.where(row >= k, pltpu.roll(x, k, axis=0), 0.0)


def _shift_up(x, k, row):
    n = x.shape[0]
    return jnp.where(row < n - k, pltpu.roll(x, n - k, axis=0), 0.0)


def _window_sum(x, g, row, shift):
    s2 = x + shift(x, 1, row)
    s4 = s2 + shift(s2, 2, row)
    s8 = s4 + shift(s4, 4, row)
    s16 = s8 + shift(s8, 8, row)
    return jnp.where(g == 0, s2, jnp.where(g == 1, s4, jnp.where(g == 2, s8, s16)))


def _window_count(g, row):
    wnd = jnp.where(g == 0, 2, jnp.where(g == 1, 4, jnp.where(g == 2, 8, 16)))
    return jnp.minimum(row + 1, wnd).astype(F32)


def _group_columns(ref):
    return [ref[:, n * GROUP_W:(n + 1) * GROUP_W].astype(F32) for n in range(4)]


def poolconv_fwd(proj3, pool_w, pool_scale, conv_w, layer, lay, hosted=None):
    Bl, S, _ = proj3.shape

    def body(x_ref, pw_ref, ps_ref, cw_ref, po_ref, co_ref):
        g = pl.program_id(1)
        row = lax.broadcasted_iota(jnp.int32, (S, GROUP_W), 0)
        u, cv, cb, cc = _group_columns(x_ref)
        d = _window_sum(u, g, row, _shift_down) / _window_count(g, row) - u
        po_ref[...] = (_dot(d.astype(BF16), pw_ref[...]) * ps_ref[...]).astype(BF16)
        z = cc * cv
        y = cw_ref[0:1, :] * _shift_down(z, 2, row) + cw_ref[1:2, :] * _shift_down(z, 1, row) + cw_ref[2:3, :] * z
        co_ref[...] = (cb * y).astype(BF16)

    out = pl.BlockSpec((None, S, GROUP_W), lambda b, g: (b, 0, g))
    return _pcall(
        body, hosted, name="poolconv_fwd", grid=(Bl, N_GROUPS),
        in_specs=[pl.BlockSpec((None, S, BRANCH_W), lambda b, g: (b, 0, lay["pc"] // BRANCH_W + g)),
                  pl.BlockSpec((None, None, GROUP_W, GROUP_W), lambda b, g: (layer, g, 0, 0)),
                  pl.BlockSpec((None, 1, GROUP_W), lambda b, g: (layer, 0, g)),
                  pl.BlockSpec((None, None, 3, GROUP_W), lambda b, g: (g, layer, 0, 0))],
        out_specs=[out, out],
        out_shape=[SDS((Bl, S, BRANCH_W), BF16), SDS((Bl, S, BRANCH_W), BF16)],
        semantics=("arbitrary", "arbitrary"),
    )(proj3, pool_w, pool_scale, conv_w)


def poolconv_bwd(proj3, dpo, dco, pool_w, pool_scale, conv_w, layer, dproj3, lay, hosted=None):
    Bl, S, _ = proj3.shape

    def body(x_ref, dpo_ref, dco_ref, pw_ref, ps_ref, cw_ref, _, dx_ref, dpw_ref, dps_ref, dcw_ref):
        g, b = pl.program_id(0), pl.program_id(1)
        row = lax.broadcasted_iota(jnp.int32, (S, GROUP_W), 0)
        cnt = _window_count(g, row)
        u, cv, cb, cc = _group_columns(x_ref)
        d = (_window_sum(u, g, row, _shift_down) / cnt - u).astype(BF16)
        pw = pw_ref[...]
        ypre = _dot(d, pw)
        dpo_v = dpo_ref[...].astype(F32)
        dps = jnp.sum(dpo_v * ypre, axis=0, keepdims=True)
        dyp = (dpo_v * ps_ref[...]).astype(BF16)
        dpw = _dot_tn(d, dyp)
        dd = _dot_nt(dyp, pw)
        dx_ref[:, 0:GROUP_W] = (_window_sum(dd / cnt, g, row, _shift_up) - dd).astype(BF16)

        z = cc * cv
        z1, z2 = _shift_down(z, 1, row), _shift_down(z, 2, row)
        w0, w1, w2 = cw_ref[0:1, :], cw_ref[1:2, :], cw_ref[2:3, :]
        y = w0 * z2 + w1 * z1 + w2 * z
        dco_v = dco_ref[...].astype(F32)
        dy = dco_v * cb
        dz = w0 * _shift_up(dy, 2, row) + w1 * _shift_up(dy, 1, row) + w2 * dy
        dx_ref[:, GROUP_W:2 * GROUP_W] = (dz * cc).astype(BF16)
        dx_ref[:, 2 * GROUP_W:3 * GROUP_W] = (dco_v * y).astype(BF16)
        dx_ref[:, 3 * GROUP_W:] = (dz * cv).astype(BF16)
        dcw = jnp.concatenate([jnp.sum(dy * z2, axis=0, keepdims=True),
                               jnp.sum(dy * z1, axis=0, keepdims=True),
                               jnp.sum(dy * z, axis=0, keepdims=True)], axis=0)

        @pl.when(b == 0)
        def _():
            dpw_ref[...] = dpw
            dps_ref[...] = dps
            dcw_ref[...] = dcw

        @pl.when(b > 0)
        def _():
            dpw_ref[...] += dpw
            dps_ref[...] += dps
            dcw_ref[...] += dcw

    blk = pl.BlockSpec((None, S, GROUP_W), lambda g, b: (b, 0, g))
    pc = pl.BlockSpec((None, S, BRANCH_W), lambda g, b: (b, 0, lay["pc"] // BRANCH_W + g))
    return _pcall(
        body, hosted, name="poolconv_bwd", grid=(N_GROUPS, Bl),
        in_specs=[pc, blk, blk,
                  pl.BlockSpec((None, None, GROUP_W, GROUP_W), lambda g, b: (layer, g, 0, 0)),
                  pl.BlockSpec((None, 1, GROUP_W), lambda g, b: (layer, 0, g)),
                  pl.BlockSpec((None, None, 3, GROUP_W), lambda g, b: (g, layer, 0, 0)), _ANY],
        out_specs=[pc, pl.BlockSpec((None, GROUP_W, GROUP_W), lambda g, b: (g, 0, 0)),
                   pl.BlockSpec((1, GROUP_W), lambda g, b: (0, g)),
                   pl.BlockSpec((None, 3, GROUP_W), lambda g, b: (g, 0, 0))],
        out_shape=[SDS(dproj3.shape, BF16), SDS((N_GROUPS, GROUP_W, GROUP_W), F32), SDS((1, BRANCH_W), F32),
                   SDS((N_GROUPS, 3, GROUP_W), F32)],
        aliases={6: 0}, semantics=("arbitrary", "arbitrary"),
    )(proj3, dpo, dco, pool_w, pool_scale, conv_w, dproj3)


def _tile_2d(rows, cols, n_arrays):
    budget = VMEM_LIMIT // 2
    lanes = -(-cols // LANES) * LANES
    if rows % 8 == 0:
        for t in range(min(rows, 2048), 7, -8):
            if rows % t == 0 and 2 * n_arrays * t * lanes * 4 <= budget:
                return t, cols
    for t in (1024, 512, 256, 128):
        if cols % t == 0 and 2 * n_arrays * (rows + 8) * t * 4 <= budget:
            return rows, t
    return rows, cols


def add_pair(kept, layer, where, received, name):
    _, n, _, R, C = kept.shape
    tr, tc = _tile_2d(R, C, 3)

    def body(where_ref, a_ref, b_ref, o_ref):
        o_ref[...] = (a_ref[...].astype(F32) + b_ref[...].astype(F32)).astype(BF16)

    blk = pl.BlockSpec((None, tr, tc), lambda d, i, j, where_ref: (d, i, j))
    grid_spec = pltpu.PrefetchScalarGridSpec(
        num_scalar_prefetch=1, grid=(n, R // tr, C // tc),
        in_specs=[pl.BlockSpec((None, None, None, tr, tc),
                               lambda d, i, j, where_ref: (layer, d, where_ref[0], i, j)), blk],
        out_specs=blk)
    return pl.pallas_call(body, name=name, grid_spec=grid_spec, out_shape=SDS((n, R, C), BF16),
                          compiler_params=_params("arbitrary", "arbitrary", "arbitrary"))(where, kept, received)


def add_chips(arrived, own, layer, where, n_layers, prev, name):
    _, R, C = arrived.shape
    tr, tc = _tile_2d(R, C, 6)

    def body(where_ref, a0, a1, a2, a3, own_ref, *rest):
        o_ref = rest[-1]
        chip = where_ref[1]
        acc = None
        for j, a_ref in enumerate((a0, a1, a2, a3)):
            term = jnp.where(chip == j, own_ref[...], a_ref[...]).astype(F32)
            acc = term if acc is None else acc + term
        o_ref[...] = acc

    def slot(j):
        return pl.BlockSpec((None, tr, tc), lambda i, k, where_ref, j=j: (
            jnp.where(where_ref[1] == j, (j + 1) % N_CHIPS, j), i, k))

    in_specs = [slot(j) for j in range(N_CHIPS)] + [
        pl.BlockSpec((None, tr, tc), lambda i, k, where_ref: (where_ref[1], i, k))]
    args = [where, arrived, arrived, arrived, arrived, own]
    aliases = {}
    if prev is not None:
        in_specs.append(_ANY)
        args.append(prev)
        aliases = {len(args) - 1: 0}
    grid_spec = pltpu.PrefetchScalarGridSpec(
        num_scalar_prefetch=1, grid=(R // tr, C // tc), in_specs=in_specs,
        out_specs=pl.BlockSpec((None, None, tr, tc), lambda i, k, where_ref: (layer, where_ref[0], i, k)))
    return pl.pallas_call(body, name=name, grid_spec=grid_spec, out_shape=SDS((n_layers, 2, R, C), F32),
                          input_output_aliases=aliases,
                          compiler_params=_params("arbitrary", "arbitrary"))(*args)


def adamw(w, g, m, v, name):
    if w.ndim == 2:
        R, C = w.shape
        tr, _ = _tile_2d(R, C, 7)
        grid, blk = (R // tr,), pl.BlockSpec((tr, C), lambda i: (i, 0))
    else:
        N, r, C = w.shape
        tn = max(t for t in range(1, N + 1) if N % t == 0 and t * r * C * 4 <= 1024 * 1024)
        grid, blk = (N // tn,), pl.BlockSpec((tn, r, C), lambda i: (i, 0, 0))

    def body(w_ref, g_ref, m_ref, v_ref, d_ref, nm_ref, nv_ref):
        gv = g_ref[...]
        m_new = ADAM_B1 * m_ref[...] + (1.0 - ADAM_B1) * gv
        v_new = ADAM_B2 * v_ref[...] + (1.0 - ADAM_B2) * (gv * gv)
        m_hat = m_new / (1.0 - ADAM_B1 ** ADAM_STEP)
        v_hat = v_new / (1.0 - ADAM_B2 ** ADAM_STEP)
        d_ref[...] = -ADAM_LR * (m_hat / (jnp.sqrt(v_hat) + ADAM_EPS) + ADAM_WD * w_ref[...])
        nm_ref[...] = m_new
        nv_ref[...] = v_new

    out = SDS(w.shape, F32)
    return pl.pallas_call(body, name=name, grid=grid, in_specs=[blk] * 4, out_specs=[blk] * 3,
                          out_shape=[out, out, out], compiler_params=_params("arbitrary"))(w, g, m, v)


_COMM = pltpu.CompilerParams(has_side_effects=True)


def gather_buffers(shards):
    me_chip = 2 * lax.axis_index("x") + lax.axis_index("y")
    pool = {}
    for name, sh in shards.items():
        L, r, c = sh.shape
        if name in ROW_SHARDED:
            pool[name] = lax.dynamic_update_slice(lax.empty((L, N_CHIPS, r, c), sh.dtype), sh[:, None],
                                                  (0, me_chip, 0, 0))
        else:
            pool[name] = lax.dynamic_update_slice(lax.empty((N_CHIPS, L, r, c), sh.dtype), sh[None],
                                                  (me_chip, 0, 0, 0))
    return pool


def comm_now(pool, stages, name):
    stages = [Hosted(pool, jobs) for jobs in stages]
    names = sorted({m for st in stages for m in st.names})
    n = len(names)

    def body(*refs):
        bufs = dict(zip(names, refs[n:2 * n]))
        sems = refs[2 * n:]
        for i, st in enumerate(stages):
            plan = _hosted_plan(st, bufs, sems[2 * i], sems[2 * i + 1])
            _hosted_start(plan, True)
            _hosted_finish(plan, True)

    sem = pltpu.SemaphoreType.DMA
    scratch = []
    for st in stages:
        scratch += [sem((len(st.jobs), 3)), sem((len(st.jobs), 3))]
    res = pl.pallas_call(
        body, name=name, in_specs=[_ANY] * n, out_specs=[_ANY] * n,
        out_shape=[SDS(pool[m].shape, pool[m].dtype) for m in names],
        scratch_shapes=scratch, input_output_aliases={t: t for t in range(n)},
        compiler_params=_COMM,
    )(*[pool[m] for m in names])
    pool.update(zip(names, res))


def gather_now(pool, units):
    comm_now(pool, [[("ici", name, layer) for name, layer in units],
                    [("fwd", name, layer) for name, layer in units]], "gather_now")


def allgather_chips(buf, name):
    def body(src_ref, out_ref, send_sems, recv_sems, local_sem):
        x, y, c = _position()
        me = 2 * x + y
        mine = pltpu.make_async_copy(src_ref, out_ref.at[me], local_sem)
        mine.start()
        sends = []
        for k, (px, py) in enumerate(_other_chips(x, y)):
            cp = _remote(src_ref, out_ref.at[me], send_sems.at[k], recv_sems.at[k], (px, py, c))
            cp.start()
            sends.append(cp)
        for k, (px, py) in enumerate(_other_chips(x, y)):
            _remote(src_ref, out_ref.at[2 * px + py], send_sems.at[k], recv_sems.at[k], (px, py, c)).wait_recv()
        for cp in sends:
            cp.wait_send()
        mine.wait()

    sem = pltpu.SemaphoreType.DMA
    return pl.pallas_call(
        body, name=name, in_specs=[_ANY], out_specs=_ANY, out_shape=SDS((N_CHIPS,) + buf.shape, buf.dtype),
        scratch_shapes=[sem((3,)), sem((3,)), sem], compiler_params=_COMM,
    )(buf)


BIG = ("w_in", "w_proj_attn", "w_proj_pool", "w_proj_conv", "conv_w", "w_out", "w_gate_up", "w_down")
REPLICATED = ("attn_norm", "b_forget", "b_gate", "pool_w", "pool_scale", "ffn_norm", "final_norm")
ORDER = ("attn_norm", "w_in", "b_forget", "b_gate", "w_proj_attn", "pool_w", "pool_scale", "w_proj_pool",
         "conv_w", "w_proj_conv", "w_out", "ffn_norm", "w_gate_up", "w_down", "final_norm")


def _proj_layout(D):
    lay = {"g": 0, "q": 3 * D}
    lay["k"] = lay["q"] + BRANCH_W
    lay["f"] = lay["k"] + BRANCH_W
    lay["v"] = lay["f"] + F_PAD
    lay["pc"] = lay["v"] + BRANCH_W
    lay["width"] = lay["pc"] + 4 * BRANCH_W
    return lay


_REF = dict(q=0, k=512, v=1024, f=1536, u=1544, cv=2056, cb=2568, cc=3080, g=3592)


def _packed_pieces(D):
    pieces = [(_REF["g"], 3 * D), (_REF["q"], BRANCH_W), (_REF["k"], BRANCH_W), (_REF["f"], HEADS),
              (None, F_PAD - HEADS), (_REF["v"], BRANCH_W)]
    for gi in range(N_GROUPS):
        pieces += [(_REF[name] + gi * GROUP_W, GROUP_W) for name in ("u", "cv", "cb", "cc")]
    return pieces


def _packed_runs(D, cs):
    runs, at = [], 0
    for start, n in _packed_pieces(D):
        if start is None:
            runs.append((at, None, 0, n))
            at += n
        while start is not None and n:
            chip, off = divmod(start, cs)
            take = min(n, cs - off)
            runs.append((at, chip, off, take))
            at, start, n = at + take, start + take, n - take
    return runs


def pack_w_in(shards, layer):
    _, _, cs, D = shards.shape
    runs = _packed_runs(D, cs)
    width = runs[-1][0] + runs[-1][3]
    tc = _tile(D, (256, 128))

    def body(s_ref, o_ref):
        for dst, chip, off, rows in runs:
            if chip is None:
                o_ref[dst:dst + rows, :] = jnp.zeros((rows, tc), s_ref.dtype)
            else:
                o_ref[dst:dst + rows, :] = s_ref[chip, off:off + rows, :]

    return pl.pallas_call(
        body, name="pack_w_in", grid=(D // tc,),
        in_specs=[pl.BlockSpec((N_CHIPS, None, cs, tc), lambda j: (0, layer, 0, j))],
        out_specs=pl.BlockSpec((width, tc), lambda j: (0, j)),
        out_shape=SDS((width, D), shards.dtype), compiler_params=_params("arbitrary"),
    )(shards)


def unpack_w_in(p, cs):
    width, D = p.shape
    half = cs // 2
    runs = []
    for src, chip, off, rows in _packed_runs(D, cs):
        while chip is not None and rows:
            h, at = divmod(off, half)
            take = min(rows, half - at)
            runs.append((src, chip, h, at, take))
            src, off, rows = src + take, off + take, rows - take
    tc = _tile(D, (256, 128))

    def body(p_ref, o_ref):
        for src, chip, h, at, rows in runs:
            o_ref[chip, h, at:at + rows, :] = p_ref[src:src + rows, :]

    return pl.pallas_call(
        body, name="unpack_w_in", grid=(D // tc,),
        in_specs=[pl.BlockSpec((width, tc), lambda j: (0, j))],
        out_specs=pl.BlockSpec((N_CHIPS, 2, half, tc), lambda j: (0, 0, 0, j)),
        out_shape=SDS((N_CHIPS, 2, half, D), p.dtype), compiler_params=_params("arbitrary"),
    )(p)


def _split_flat(vec, shapes):
    out, at = [], 0
    for shp in shapes:
        n = int(np.prod(shp))
        out.append(vec[at:at + n].reshape(shp))
        at += n
    return out


def kernel(x, attn_norm, w_in, b_forget, b_gate, w_proj_attn, pool_w, pool_scale, w_proj_pool, conv_w, w_proj_conv, w_out, ffn_norm, w_gate_up, w_down, final_norm, loss_target, m_attn_norm, m_w_in, m_b_forget, m_b_gate, m_w_proj_attn, m_pool_w, m_pool_scale, m_w_proj_pool, m_conv_w, m_w_proj_conv, m_w_out, m_ffn_norm, m_w_gate_up, m_w_down, m_final_norm, v_attn_norm, v_w_in, v_b_forget, v_b_gate, v_w_proj_attn, v_pool_w, v_pool_scale, v_w_proj_pool, v_conv_w, v_w_proj_conv, v_w_out, v_ffn_norm, v_w_gate_up, v_w_down, v_final_norm):
    weights = dict(attn_norm=attn_norm, w_in=w_in, b_forget=b_forget, b_gate=b_gate, w_proj_attn=w_proj_attn,
                   pool_w=pool_w, pool_scale=pool_scale, w_proj_pool=w_proj_pool, conv_w=conv_w,
                   w_proj_conv=w_proj_conv, w_out=w_out, ffn_norm=ffn_norm, w_gate_up=w_gate_up, w_down=w_down,
                   final_norm=final_norm)
    mom_m = dict(attn_norm=m_attn_norm, w_in=m_w_in, b_forget=m_b_forget, b_gate=m_b_gate, w_proj_attn=m_w_proj_attn,
                 pool_w=m_pool_w, pool_scale=m_pool_scale, w_proj_pool=m_w_proj_pool, conv_w=m_conv_w,
                 w_proj_conv=m_w_proj_conv, w_out=m_w_out, ffn_norm=m_ffn_norm, w_gate_up=m_w_gate_up,
                 w_down=m_w_down, final_norm=m_final_norm)
    mom_v = dict(attn_norm=v_attn_norm, w_in=v_w_in, b_forget=v_b_forget, b_gate=v_b_gate, w_proj_attn=v_w_proj_attn,
                 pool_w=v_pool_w, pool_scale=v_pool_scale, w_proj_pool=v_w_proj_pool, conv_w=v_conv_w,
                 w_proj_conv=v_w_proj_conv, w_out=v_w_out, ffn_norm=v_ffn_norm, w_gate_up=v_w_gate_up,
                 w_down=v_w_down, final_norm=v_final_norm)

    Bl, S, D = x.shape
    T = Bl * S
    L = w_in.shape[0]
    F = w_down.shape[1] * N_CHIPS
    lay = _proj_layout(D)
    cst = _placement_constants()
    assert L == N_LAYERS and S % ATTN_BLOCK == 0 and F % (2 * LANES) == 0 and D % BRANCH_W == 0
    assert w_in.shape[2] * N_CHIPS == _REF["g"] + 3 * D and conv_w.shape[2] == GROUP_W

    send = {n: weights[n].astype(BF16) for n in BIG}
    send["conv_w"] = conv_w
    me_chip = 2 * lax.axis_index("x") + lax.axis_index("y")
    send["w_in"] = w_in.transpose(0, 2, 1).astype(BF16)
    pool = gather_buffers(send)
    gather_now(pool, [("w_in", 0)])
    rest = ("w_out", "w_proj_attn", "w_proj_pool", "w_gate_up", "w_proj_conv", "conv_w")
    late = ("w_out", "w_proj_attn", "w_proj_pool", "w_proj_conv", "conv_w")
    jobs = lambda kind, names, layer: [(kind, n, layer) for n in names]
    carried = {
        ("in_proj", 0): jobs("ici", rest, 0),
        ("attn_prep", 0): jobs("fwd", late, 0),
        ("attn_fwd", 0): jobs("fwd", ("w_gate_up",), 0) + jobs("ici", ("w_in",), 1) + jobs("ici", ("w_down",), 0),
        ("poolconv_fwd", 0): jobs("fwd", ("w_in",), 1) + jobs("fwd", ("w_down",), 0),
        ("mix_fwd", 0): jobs("ici", ("w_down",), 1),
        ("gate_up_proj", 0): jobs("ici", late, 1) + jobs("fwd", ("w_down",), 1),
        ("ffn_down_fwd", 0): jobs("ici", ("w_gate_up",), 1),
        ("in_proj", 1): jobs("fwd", ("w_gate_up",) + late, 1),
    }
    carry = lambda call, layer: Hosted(pool, carried[call, layer]) if (call, layer) in carried else None
    w_down_f = lambda: pool["w_down"].reshape(L, F, D)
    pool_w_b = pool_w.astype(BF16)
    an3, fn3 = attn_norm.reshape(L, 1, D), ffn_norm.reshape(L, 1, D)
    bg3, ps3 = b_gate.reshape(L, 1, 3 * D), pool_scale.reshape(L, 1, BRANCH_W)
    bf3 = jnp.pad(b_forget, ((0, 0), (0, LANES - HEADS))).reshape(L, 1, LANES)

    xs = x.reshape(T, D)
    saved = []
    w_in_p = []
    for l in range(L):
        w_in_p.append(pack_w_in(pool["w_in"], l))
        proj, h = norm_matmul(xs, an3, w_in_p[l], l, "rows", "in_proj", carry("in_proj", l))
        proj3 = proj.reshape(Bl, S, lay["width"])
        qa, ka = attn_prep(proj3, bf3, l, cst, lay, carry("attn_prep", l))
        ao, lse = attn_fwd(qa, ka, proj3, lay, carry("attn_fwd", l))
        po, co = poolconv_fwd(proj3, pool_w_b, ps3, pool["conv_w"], l, lay, carry("poolconv_fwd", l))
        ao2, po2, co2 = (a.reshape(T, BRANCH_W) for a in (ao, po, co))
        x1, ys, mixed = mix_fwd(ao2, po2, co2, proj, bg3, pool["w_proj_attn"], pool["w_proj_pool"],
                                pool["w_proj_conv"], pool["w_out"], l, xs, carry("mix_fwd", l))
        ab, h2 = norm_matmul(x1, fn3, pool["w_gate_up"], l, "by_shard", "gate_up_proj", carry("gate_up_proj", l))
        x2, s_act = ffn_down_fwd(ab, w_down_f(), l, x1, carry("ffn_down_fwd", l))
        saved.append(dict(x=xs, proj=proj, proj3=proj3, h=h, qa=qa, ka=ka, ao=ao, lse=lse, ao2=ao2, po2=po2,
                          co2=co2, ys=ys, mixed=mixed, x1=x1, ab=ab, h2=h2, s=s_act))
        xs = x2
    w_gu, w_o, conv_w_g = pool["w_gate_up"], pool["w_out"], pool["conv_w"]
    wpa, wpp, wpc = pool["w_proj_attn"], pool["w_proj_pool"], pool["w_proj_conv"]
    w_down_f = w_down_f()

    loss_row, dx, dxb, g_final = loss_head(xs, final_norm.reshape(1, D), loss_target.reshape(T, D))
    loss = lax.psum(loss_row[0, 0], AXES)

    reduced_names = tuple(n for n in BIG if n != "conv_w")
    early_names = tuple(n for n in reduced_names if n != "w_in")
    proj_names = ("w_out", "w_proj_attn", "w_proj_pool", "w_proj_conv")
    first_names = ("w_in", "w_gate_up", "w_down")
    where = jnp.stack([lax.axis_index("c"), me_chip]).astype(jnp.int32)
    rs = {}

    def reduce_begin(layer, grads):
        for n, g in grads.items():
            g5 = g.reshape((1, N_CHIPS, 2, -1) + g.shape[-1:])
            rs["g%d:%s" % (layer, n)] = g5
            for role in "ra":
                rs["%s%d:%s" % (role, layer, n)] = lax.empty((N_CHIPS,) + g5.shape[3:], BF16)

    swap_jobs = lambda layer, names: [("swap", "g%d:%s" % (layer, n), "r%d:%s" % (layer, n), 0) for n in names]
    xchg_jobs = lambda layer, names: [("xchg", "s%d:%s" % (layer, n), "a%d:%s" % (layer, n)) for n in names]
    join_jobs = lambda layer, names: [("join", "o:" + n, layer) for n in names]

    def pair_sums(layer, names):
        for n in names:
            rs["s%d:%s" % (layer, n)] = add_pair(rs["g%d:%s" % (layer, n)], 0, where, rs["r%d:%s" % (layer, n)],
                                                 "add_pair_" + n)

    def chip_sums(layer, names, slot, n_slots):
        for n in names:
            rs["o:" + n] = add_chips(rs["a%d:%s" % (layer, n)], rs["s%d:%s" % (layer, n)], slot, where, n_slots,
                                     rs.get("o:" + n), "add_chips_" + n)

    small = {n: [None] * L for n in REPLICATED if n != "final_norm"}
    g_conv = [None] * L
    to3 = lambda a: a.reshape(Bl, S, -1)
    for l in reversed(range(L)):
        sv = saved[l]
        behind = (lambda jobs: Hosted(rs, jobs)) if l == 0 else (lambda jobs: None)
        grads = {}
        da, db = ffn_down_bwd(dxb, w_down_f, l, sv["ab"], behind(swap_jobs(1, reduced_names)))
        if l == 0:
            pair_sums(1, reduced_names)
        grads["w_down"] = matmul_tn(sv["s"], [dxb], "grad_w_down", hosted=behind(xchg_jobs(1, ("w_down",))))
        grads["w_gate_up"] = matmul_tn(sv["h2"], [da, db], "grad_w_gate_up", by_dest=True, tn=2 * F // N_CHIPS,
                                       tk=_tile(T, (1024, 512, 256)), hosted=behind(xchg_jobs(1, ("w_gate_up",))))
        dx1, dx1b, g_fn = matmul_nt_normbwd([da, db], w_gu, l, "by_shard", sv["x1"], fn3, dx, "gate_up_bwd",
                                            behind(xchg_jobs(1, ("w_in",))))
        small["ffn_norm"][l] = g_fn[0]
        if l == 0:
            chip_sums(1, first_names, 1, L)
        dys, dproj, dao, dpo, dco, g_bg = mix_bwd(dx1b, w_o, sv["proj"], bg3, sv["ys"], wpa, wpp, wpc, l,
                                                  lay["width"],
                                                  behind(xchg_jobs(1, proj_names) + join_jobs(1, first_names)))
        if l == 0:
            chip_sums(1, proj_names, 1, L)
        small["b_gate"][l] = g_bg[0]
        grads["w_out"] = matmul_tn(sv["mixed"], [dx1b], "grad_w_out")
        for n, (name, br) in enumerate((("w_proj_attn", sv["ao2"]), ("w_proj_pool", sv["po2"]),
                                        ("w_proj_conv", sv["co2"]))):
            grads[name] = matmul_tn(br, [dys], "grad_" + name, b_col0=n * D, n_cols=D, by_dest=True,
                                    tn=D // N_CHIPS)
        if l == 0:
            reduce_begin(0, grads)
        dqa, dka, dproj3 = attn_bwd(sv["qa"], sv["ka"], sv["proj3"], to3(dao), sv["ao"], sv["lse"], to3(dproj), lay,
                                    behind(swap_jobs(0, early_names) + join_jobs(1, proj_names)))
        if l == 0:
            pair_sums(0, early_names)
        dproj3, g_bf = attn_post(dqa, dka, sv["proj3"], bf3, l, dproj3, cst, lay, behind(xchg_jobs(
            0, ("w_out", "w_proj_attn", "w_proj_pool", "w_proj_conv"))))
        small["b_forget"][l] = g_bf[0, :HEADS]
        dproj3, g_pw, g_ps, g_conv[l] = poolconv_bwd(sv["proj3"], to3(dpo), to3(dco), pool_w_b, ps3, conv_w_g, l,
                                                     dproj3, lay, behind(xchg_jobs(0, ("w_down",))))
        small["pool_w"][l], small["pool_scale"][l] = g_pw, g_ps[0]
        dproj = dproj3.reshape(T, lay["width"])
        g_w_in = unpack_w_in(matmul_tn(dproj, [sv["h"]], "grad_w_in", hosted=behind(xchg_jobs(
            0, ("w_gate_up",)))), w_in.shape[2])
        if l:
            reduce_begin(l, {**grads, "w_in": g_w_in})
        else:
            reduce_begin(0, {"w_in": g_w_in})
            comm_now(rs, [swap_jobs(0, ("w_in",))], "swap_w_in_halves")
            pair_sums(0, ("w_in",))
        dx, dxb, g_an = matmul_nt_normbwd([dproj], w_in_p[l], l, "rows", sv["x"], an3, dx1, "in_proj_bwd",
                                          behind(xchg_jobs(0, ("w_in",))))
        small["attn_norm"][l] = g_an[0]
    grad_x = dx.reshape(Bl, S, D)

    small_shapes = [weights[n].shape for n in REPLICATED] + [(L, N_CHIPS) + conv_w.shape[1:]]
    small_vec = jnp.concatenate([jnp.stack(small[n]).reshape(-1) for n in REPLICATED[:-1]]
                                + [g_final[0], jnp.stack(g_conv).reshape(-1)])
    n_small = small_vec.shape[0]
    small_vec = jnp.pad(small_vec, (0, -n_small % (2 * N_CHIPS * 16 * LANES))).astype(BF16)
    rs["g0:small"] = small_vec.reshape(1, N_CHIPS, 2, -1, LANES)
    for role in "ra":
        rs[role + "0:small"] = lax.empty((N_CHIPS,) + rs["g0:small"].shape[3:], BF16)
    last = ("small",)
    comm_now(rs, [swap_jobs(0, last)], "swap_grad_halves")
    pair_sums(0, last)
    comm_now(rs, [xchg_jobs(0, last)], "exchange_grad_chips")
    chip_sums(0, reduced_names, 0, L)
    chip_sums(0, ("small",), 0, 1)
    comm_now(rs, [join_jobs(0, reduced_names + ("small",))], "join_grad_halves")
    shard_grads = {n: rs["o:" + n].reshape((L, -1) + rs["o:" + n].shape[-1:]) for n in reduced_names}
    small_all = allgather_chips(rs["o:small"].reshape(-1, LANES), "allgather_small_grads").reshape(-1)[:n_small]
    *rep_list, conv_all = _split_flat(small_all, small_shapes)
    rep_grads = dict(zip(REPLICATED, rep_list))
    shard_grads["conv_w"] = lax.dynamic_index_in_dim(conv_all, me_chip, 1, keepdims=False)

    delta, new_m, new_v = {}, {}, {}
    for n in BIG:
        shp = weights[n].shape
        if n == "w_in":
            view, back = (lambda a: a.transpose(2, 0, 1)), (lambda a: a.transpose(1, 2, 0))
            g = shard_grads[n].transpose(1, 0, 2)
        else:
            view, back = (lambda a: a.reshape(-1, shp[-1])), (lambda a: a.reshape(shp))
            g = view(shard_grads[n])
        d, nm, nv = adamw(view(weights[n]), g, view(mom_m[n]), view(mom_v[n]), "adamw_" + n)
        delta[n], new_m[n], new_v[n], shard_grads[n] = back(d), back(nm), back(nv), back(g)

    def rows(d):
        vec = jnp.concatenate([d[n].reshape(-1) for n in REPLICATED])
        return jnp.pad(vec, (0, -vec.shape[0] % (8 * LANES))).reshape(-1, LANES)

    outs = adamw(rows(weights), rows(rep_grads), rows(mom_m), rows(mom_v), "adamw_replicated")
    for res, o in zip((delta, new_m, new_v), outs):
        res.update(zip(REPLICATED, _split_flat(o.reshape(-1), small_shapes[:len(REPLICATED)])))
    all_grads = {**shard_grads, **rep_grads}

    return (loss, grad_x, *[all_grads[n] for n in ORDER], *[delta[n] for n in ORDER],
            *[new_m[n] for n in ORDER], *[new_v[n] for n in ORDER])
```

```python
import numpy as np
import jax
import jax.numpy as jnp
from jax import lax
from jax.experimental import pallas as pl
from jax.experimental.pallas import tpu as pltpu

F32, BF16 = jnp.float32, jnp.bfloat16
SDS = jax.ShapeDtypeStruct
MESH = pl.DeviceIdType.MESH
AXES = ("x", "y", "c")
N_CHIPS = 4
N_LAYERS = 2
LANES = 128
VMEM_LIMIT = 48 * 1024 * 1024

HEADS, HEAD_DIM = 8, 64
HEAD_PAD = 128
BRANCH_W = 512
GROUP_W = 128
N_GROUPS = BRANCH_W // GROUP_W
POOL_WINDOWS = (2, 4, 8, 16)
F_PAD = 512
ATTN_BLOCK = 256
RMS_EPS = 1e-6
NEG_INF = -1e30
ADAM_LR, ADAM_B1, ADAM_B2, ADAM_EPS, ADAM_WD, ADAM_STEP = 0.001, 0.9, 0.999, 1e-08, 0.01, 10

NT = (((1,), (1,)), ((), ()))
TN = (((0,), (0,)), ((), ()))
_ANY = pl.BlockSpec(memory_space=pl.ANY)


def _tile(n, prefs):
    for p in prefs:
        if n % p == 0:
            return p
    raise ValueError(f"no tile of {prefs} divides {n}")


def _params(*sem):
    return pltpu.CompilerParams(dimension_semantics=sem, vmem_limit_bytes=VMEM_LIMIT)


def _sigmoid(z):
    return 0.5 * jnp.tanh(0.5 * z) + 0.5


def _split3(x):
    h1 = x.astype(BF16)
    r1 = x - h1.astype(F32)
    h2 = r1.astype(BF16)
    h3 = (r1 - h2.astype(F32)).astype(BF16)
    return h1, h2, h3


def _position():
    return lax.axis_index("x"), lax.axis_index("y"), lax.axis_index("c")


def _other_chips(x, y):
    return [(1 - x, y), (x, 1 - y), (1 - x, 1 - y)]


def _remote(src, dst, send_sem, recv_sem, device):
    return pltpu.make_async_remote_copy(src_ref=src, dst_ref=dst, send_sem=send_sem, recv_sem=recv_sem,
                                        device_id=device, device_id_type=MESH)


ROW_SHARDED = ("w_out", "w_down")
FETCHER = dict(w_in=0, w_out=0, w_proj_attn=0, w_proj_pool=0, w_gate_up=1, w_down=1, w_proj_conv=1, conv_w=1)


class Hosted:
    def __init__(self, pool, jobs):
        self.pool, self.jobs = pool, list(jobs)
        names = set()
        for job in self.jobs:
            names.update(job[1:3] if job[0] in ("swap", "xchg") else job[1:2])
        self.names = sorted(names)


def _hosted_plan(hosted, refs, send_sems, recv_sems):
    x, y, c = _position()
    me = 2 * x + y
    others = _other_chips(x, y)
    sibling = (x, y, 1 - c)
    plan = []
    for j, job in enumerate(hosted.jobs):
        kind = job[0]
        sems = lambda k, j=j: (send_sems.at[j, k], recv_sems.at[j, k])
        if kind in ("ici", "fwd"):
            _, name, layer = job
            ref = refs[name]
            win = (lambda chip, ref=ref, layer=layer: ref.at[layer, chip]) if name in ROW_SHARDED else (
                lambda chip, ref=ref, layer=layer: ref.at[chip, layer])
            mine = c == FETCHER[name]
            if kind == "ici":
                sends = [_remote(win(me), win(me), *sems(k), (px, py, c)) for k, (px, py) in enumerate(others)]
                arrivals = [_remote(win(2 * px + py), win(2 * px + py), *sems(k), (px, py, c))
                            for k, (px, py) in enumerate(others)]
                plan.append((mine, sends, arrivals, []))
            else:
                sends = [_remote(win(2 * px + py), win(2 * px + py), *sems(k), sibling)
                         for k, (px, py) in enumerate(others)]
                plan.append((mine, sends, [], sends))
        elif kind == "swap":
            _, src, dst, layer = job
            cp = _remote(refs[src].at[layer, :, 1 - c], refs[dst], *sems(0), sibling)
            plan.append((True, [cp], [cp], []))
        elif kind == "xchg":
            _, src, dst = job
            sends = [_remote(refs[src].at[2 * px + py], refs[dst].at[me], *sems(k), (px, py, c))
                     for k, (px, py) in enumerate(others)]
            arrivals = [_remote(refs[src].at[me], refs[dst].at[2 * px + py], *sems(k), (px, py, c))
                        for k, (px, py) in enumerate(others)]
            plan.append((True, sends, arrivals, []))
        else:
            _, name, layer = job
            ref = refs[name]
            cp = _remote(ref.at[layer, c], ref.at[layer, c], *sems(0), sibling)
            arrival = _remote(ref.at[layer, c], ref.at[layer, 1 - c], *sems(0), sibling)
            plan.append((True, [cp], [arrival], []))
    return plan


def _hosted_start(plan, now):
    for mine, sends, _, _ in plan:
        @pl.when(now & mine)
        def _(sends=sends):
            for cp in sends:
                cp.start()


def _hosted_finish(plan, now):
    for mine, sends, arrivals, sibling_arrivals in plan:
        @pl.when(now & mine)
        def _(sends=sends, arrivals=arrivals):
            for cp in arrivals:
                cp.wait_recv()
            for cp in sends:
                cp.wait_send()

        if sibling_arrivals:
            @pl.when(now & jnp.logical_not(mine))
            def _(sibling_arrivals=sibling_arrivals):
                for cp in sibling_arrivals:
                    cp.wait_recv()


def _pcall(body, hosted, *, name, grid, in_specs, out_specs, out_shape, semantics, scratch_shapes=(), aliases=None):
    aliases = dict(aliases or {})
    if hosted is None or not hosted.jobs:
        return pl.pallas_call(body, name=name, grid=grid, in_specs=in_specs, out_specs=out_specs,
                              out_shape=out_shape, scratch_shapes=list(scratch_shapes),
                              input_output_aliases=aliases, compiler_params=_params(*semantics))
    single = not isinstance(out_shape, (list, tuple))
    out_specs_l = [out_specs] if single else list(out_specs)
    out_shape_l = [out_shape] if single else list(out_shape)
    n_in, n_out, n_buf, n_job = len(in_specs), len(out_specs_l), len(hosted.names), len(hosted.jobs)

    def carrying(*refs):
        ins, outs = refs[:n_in], refs[n_in + n_buf:n_in + n_buf + n_out]
        bufs = refs[n_in + n_buf + n_out:n_in + 2 * n_buf + n_out]
        rest = refs[n_in + 2 * n_buf + n_out:]
        scratch, send_sems, recv_sems = rest[:-2], rest[-2], rest[-1]
        first, last = True, True
        for axis, size in enumerate(grid):
            first = first & (pl.program_id(axis) == 0)
            last = last & (pl.program_id(axis) == size - 1)
        plan = _hosted_plan(hosted, dict(zip(hosted.names, bufs)), send_sems, recv_sems)
        _hosted_start(plan, first)
        body(*ins, *outs, *scratch)
        _hosted_finish(plan, last)

    def run(*args):
        bufs = [hosted.pool[n] for n in hosted.names]
        sem = pltpu.SemaphoreType.DMA
        res = pl.pallas_call(
            carrying, name=name, grid=grid, in_specs=list(in_specs) + [_ANY] * n_buf,
            out_specs=out_specs_l + [_ANY] * n_buf,
            out_shape=out_shape_l + [SDS(b.shape, b.dtype) for b in bufs],
            scratch_shapes=list(scratch_shapes) + [sem((n_job, 3)), sem((n_job, 3))],
            input_output_aliases={**aliases, **{n_in + i: n_out + i for i in range(n_buf)}},
            compiler_params=pltpu.CompilerParams(dimension_semantics=semantics, vmem_limit_bytes=VMEM_LIMIT,
                                                 has_side_effects=True),
        )(*args, *bufs)
        hosted.pool.update(zip(hosted.names, res[n_out:]))
        return res[0] if single else res[:n_out]

    return run


def _dot(a, b):
    return jnp.dot(a, b, preferred_element_type=F32)


def _dot_nt(a, b):
    return lax.dot_general(a, b, NT, preferred_element_type=F32)


def _dot_tn(a, b):
    return lax.dot_general(a, b, TN, preferred_element_type=F32)


def norm_matmul(x, gain, w, layer, kind, name, hosted=None):
    T, D = x.shape
    if kind == "by_shard":
        tn = w.shape[3]
        N = N_CHIPS * tn
        w_spec = pl.BlockSpec((None, None, D, tn), lambda i, j: (j, layer, 0, 0))
        mm = _dot
    else:
        N = w.shape[0]
        tn = _tile(N, (1024, 512, 256, 128))
        w_spec = pl.BlockSpec((tn, D), lambda i, j: (j, 0))
        mm = _dot_nt
    tm = _tile(T, (2048, 1024, 512, 256, 128) if kind == "rows" else (1024, 512, 256, 128))

    def body(x_ref, g_ref, w_ref, y_ref, h_ref):
        @pl.when(pl.program_id(1) == 0)
        def _():
            xf = x_ref[...]
            r = lax.rsqrt(jnp.mean(xf * xf, axis=-1, keepdims=True) + RMS_EPS)
            h_ref[...] = ((xf * r) * g_ref[...]).astype(BF16)

        y_ref[...] = mm(h_ref[...], w_ref[...]).astype(BF16)

    return _pcall(
        body, hosted, name=name, grid=(T // tm, N // tn),
        in_specs=[pl.BlockSpec((tm, D), lambda i, j: (i, 0)),
                  pl.BlockSpec((None, 1, D), lambda i, j: (layer, 0, 0)),
                  w_spec],
        out_specs=[pl.BlockSpec((tm, tn), lambda i, j: (i, j)),
                   pl.BlockSpec((tm, D), lambda i, j: (i, 0))],
        out_shape=[SDS((T, N), BF16), SDS((T, D), BF16)],
        semantics=("arbitrary", "arbitrary"),
    )(x, gain, w)


def matmul_nt_normbwd(dys, w, layer, kind, x, gain, dres, name, hosted=None):
    T, D = x.shape
    width = dys[0].shape[1]
    if kind == "by_shard":
        tk = w.shape[3]
        w_spec = pl.BlockSpec((None, None, D, tk), lambda i, k: (k, layer, 0, 0))
        mm = _dot_nt
    else:
        tk = _tile(width, (3584, 1024, 512, 256, 128))
        w_spec = pl.BlockSpec((tk, D), lambda i, k: (k, 0))
        mm = _dot
    per = width // tk
    nk = per * len(dys)
    tm = _tile(T, (512, 256, 128))
    n_dy = len(dys)

    def dy_spec(p):
        return pl.BlockSpec((tm, tk), lambda i, k: (i, jnp.clip(k - p * per, 0, per - 1)))

    def body(*refs):
        dy_refs = refs[:n_dy]
        w_ref, x_ref, g_ref, dres_ref, dx_ref, dxb_ref, dg_ref, acc_ref = refs[n_dy:]
        i, k = pl.program_id(0), pl.program_id(1)

        @pl.when(k == 0)
        def _():
            acc_ref[...] = jnp.zeros_like(acc_ref)

        for p in range(n_dy):
            @pl.when((k >= p * per) & (k < (p + 1) * per))
            def _(p=p):
                acc_ref[...] += mm(dy_refs[p][...], w_ref[...])

        @pl.when(k == nk - 1)
        def _():
            xf = x_ref[...]
            r = lax.rsqrt(jnp.mean(xf * xf, axis=-1, keepdims=True) + RMS_EPS)
            xhat = xf * r
            dh = acc_ref[...]
            dhg = dh * g_ref[...]
            dx = dres_ref[...] + r * (dhg - xhat * jnp.mean(dhg * xhat, axis=-1, keepdims=True))
            dx_ref[...] = dx
            dxb_ref[...] = dx.astype(BF16)
            part = jnp.sum(dh * xhat, axis=0, keepdims=True)

            @pl.when(i == 0)
            def _():
                dg_ref[...] = part

            @pl.when(i > 0)
            def _():
                dg_ref[...] += part

    row = pl.BlockSpec((tm, D), lambda i, k: (i, 0))
    return _pcall(
        body, hosted, name=name, grid=(T // tm, nk),
        in_specs=[dy_spec(p) for p in range(n_dy)] + [
            w_spec, row, pl.BlockSpec((None, 1, D), lambda i, k: (layer, 0, 0)), row],
        out_specs=[row, row, pl.BlockSpec((1, D), lambda i, k: (0, 0))],
        out_shape=[SDS((T, D), F32), SDS((T, D), BF16), SDS((1, D), F32)],
        scratch_shapes=[pltpu.VMEM((tm, D), F32)],
        semantics=("arbitrary", "arbitrary"),
    )(*dys, w, x, gain, dres)


def matmul_tn(a, bs, name, b_col0=0, n_cols=None, by_dest=False, tn=None, tk=None, hosted=None):
    T, M = a.shape
    width = bs[0].shape[1]
    N = n_cols if n_cols else width * len(bs)
    tm = _tile(M, (1408, 1024, 512, 256, 128))
    tn = tn or _tile(N, (512, 256, 128))
    tk = tk or _tile(T, (4096, 2048, 1024, 512, 256))
    assert b_col0 % tn == 0 and width % tn == 0
    j0, per, nk, n_b = b_col0 // tn, width // tn, T // tk, len(bs)

    def b_spec(p):
        return pl.BlockSpec((tk, tn), lambda i, j, k: (k, jnp.clip(j0 + j - p * per, 0, per - 1)))

    def body(*refs):
        a_ref, b_refs = refs[0], refs[1:1 + n_b]
        o_ref, acc_ref = refs[-2], refs[-1]
        j, k = pl.program_id(1), pl.program_id(2)

        @pl.when(k == 0)
        def _():
            acc_ref[...] = jnp.zeros_like(acc_ref)

        for p in range(n_b):
            @pl.when((j0 + j >= p * per) & (j0 + j < (p + 1) * per))
            def _(p=p):
                acc_ref[...] += _dot_tn(a_ref[...], b_refs[p][...])

        @pl.when(k == nk - 1)
        def _():
            o_ref[...] = acc_ref[...].astype(BF16)

    if by_dest:
        cs = N // N_CHIPS
        npd = cs // tn
        out_shape = SDS((N_CHIPS, M, cs), BF16)
        out_spec = pl.BlockSpec((None, tm, tn), lambda i, j, k: (j // npd, i, j % npd))
    else:
        out_shape = SDS((M, N), BF16)
        out_spec = pl.BlockSpec((tm, tn), lambda i, j, k: (i, j))
    return _pcall(
        body, hosted, name=name, grid=(M // tm, N // tn, nk),
        in_specs=[pl.BlockSpec((tk, tm), lambda i, j, k: (k, i))] + [b_spec(p) for p in range(n_b)],
        out_specs=out_spec, out_shape=out_shape,
        scratch_shapes=[pltpu.VMEM((tm, tn), F32)],
        semantics=("arbitrary", "arbitrary", "arbitrary"),
    )(a, *bs)


def ffn_down_fwd(ab, w_down, layer, x1, hosted=None):
    T, D = x1.shape
    F = w_down.shape[1]
    tm = _tile(T, (512, 256, 128))
    tk = F // 2
    nk = F // tk

    def body(a_ref, b_ref, w_ref, x_ref, x2_ref, s_ref, acc_ref):
        k = pl.program_id(1)

        @pl.when(k == 0)
        def _():
            acc_ref[...] = x_ref[...]

        a = a_ref[...].astype(F32)
        s = (a * _sigmoid(a) * b_ref[...].astype(F32)).astype(BF16)
        s_ref[...] = s
        acc_ref[...] += _dot(s, w_ref[...])

        @pl.when(k == nk - 1)
        def _():
            x2_ref[...] = acc_ref[...]

    return _pcall(
        body, hosted, name="ffn_down_fwd", grid=(T // tm, nk),
        in_specs=[pl.BlockSpec((tm, tk), lambda i, k: (i, k)),
                  pl.BlockSpec((tm, tk), lambda i, k: (i, nk + k)),
                  pl.BlockSpec((None, tk, D), lambda i, k: (layer, k, 0)),
                  pl.BlockSpec((tm, D), lambda i, k: (i, 0))],
        out_specs=[pl.BlockSpec((tm, D), lambda i, k: (i, 0)),
                   pl.BlockSpec((tm, tk), lambda i, k: (i, k))],
        out_shape=[SDS((T, D), F32), SDS((T, F), BF16)],
        scratch_shapes=[pltpu.VMEM((tm, D), F32)],
        semantics=("arbitrary", "arbitrary"),
    )(ab, ab, w_down, x1)


def ffn_down_bwd(dx2b, w_down, layer, ab, hosted=None):
    T, D = dx2b.shape
    F = w_down.shape[1]
    tm = _tile(T, (512, 256, 128))
    tn = F // 2
    nj = F // tn

    def body(dx_ref, w_ref, a_ref, b_ref, da_ref, db_ref):
        ds = _dot_nt(dx_ref[...], w_ref[...])
        a = a_ref[...].astype(F32)
        sg = _sigmoid(a)
        da_ref[...] = (ds * b_ref[...].astype(F32) * (sg * (1.0 + a * (1.0 - sg)))).astype(BF16)
        db_ref[...] = (ds * (a * sg)).astype(BF16)

    blk = pl.BlockSpec((tm, tn), lambda i, j: (i, j))
    return _pcall(
        body, hosted, name="ffn_down_bwd", grid=(T // tm, nj),
        in_specs=[pl.BlockSpec((tm, D), lambda i, j: (i, 0)),
                  pl.BlockSpec((None, tn, D), lambda i, j: (layer, j, 0)),
                  blk, pl.BlockSpec((tm, tn), lambda i, j: (i, nj + j))],
        out_specs=[blk, blk],
        out_shape=[SDS((T, F), BF16), SDS((T, F), BF16)],
        semantics=("arbitrary", "arbitrary"),
    )(dx2b, w_down, ab, ab)


def _mix_specs(tm, D, layer):
    cs = D // N_CHIPS
    row = lambda w: pl.BlockSpec((tm, w), lambda i: (i, 0))
    wp = pl.BlockSpec((N_CHIPS, None, BRANCH_W, cs), lambda i: (0, layer, 0, 0))
    wo = pl.BlockSpec((None, N_CHIPS, cs, D), lambda i: (layer, 0, 0, 0))
    bg = pl.BlockSpec((None, 1, 3 * D), lambda i: (layer, 0, 0))
    return row, wp, wo, bg


def mix_fwd(ao, po, co, proj, b_gate, wpa, wpp, wpc, w_out, layer, x, hosted=None):
    T, D = x.shape
    cs = D // N_CHIPS
    tm = _tile(T, (256, 128))
    row, wp, wo, bg = _mix_specs(tm, D, layer)

    def body(ao_ref, po_ref, co_ref, g_ref, bg_ref, wpa_ref, wpp_ref, wpc_ref, wo_ref, x_ref,
             x1_ref, ys_ref, mixed_ref):
        mixed = jnp.zeros((tm, D), F32)
        for n, (br, wp_ref) in enumerate(((ao_ref, wpa_ref), (po_ref, wpp_ref), (co_ref, wpc_ref))):
            y = jnp.concatenate([_dot(br[...], wp_ref[j]) for j in range(N_CHIPS)], axis=1)
            cols = slice(n * D, (n + 1) * D)
            gate = _sigmoid(g_ref[:, cols].astype(F32) + bg_ref[:, cols])
            ys_ref[:, cols] = y.astype(BF16)
            mixed = mixed + gate * y
        mb = mixed.astype(BF16)
        mixed_ref[...] = mb
        acc = x_ref[...]
        for j in range(N_CHIPS):
            acc = acc + _dot(mb[:, j * cs:(j + 1) * cs], wo_ref[j])
        x1_ref[...] = acc

    return _pcall(
        body, hosted, name="mix_fwd", grid=(T // tm,),
        in_specs=[row(BRANCH_W), row(BRANCH_W), row(BRANCH_W), row(3 * D), bg, wp, wp, wp, wo, row(D)],
        out_specs=[row(D), row(3 * D), row(D)],
        out_shape=[SDS((T, D), F32), SDS((T, 3 * D), BF16), SDS((T, D), BF16)],
        semantics=("arbitrary",),
    )(ao, po, co, proj, b_gate, wpa, wpp, wpc, w_out, x)


def mix_bwd(dx1b, w_out, proj, b_gate, ys, wpa, wpp, wpc, layer, width, hosted=None):
    T, D = dx1b.shape
    cs = D // N_CHIPS
    tm = _tile(T, (256, 128))
    row, wp, wo, bg = _mix_specs(tm, D, layer)

    def body(dx_ref, wo_ref, g_ref, bg_ref, ys_ref, wpa_ref, wpp_ref, wpc_ref,
             dys_ref, dg_ref, dao_ref, dpo_ref, dco_ref, dbg_ref):
        i = pl.program_id(0)
        dx = dx_ref[...]
        dmixed = jnp.concatenate([_dot_nt(dx, wo_ref[j]) for j in range(N_CHIPS)], axis=1)
        for n, (wp_ref, dbr) in enumerate(((wpa_ref, dao_ref), (wpp_ref, dpo_ref), (wpc_ref, dco_ref))):
            cols = slice(n * D, (n + 1) * D)
            gate = _sigmoid(g_ref[:, cols].astype(F32) + bg_ref[:, cols])
            dy = (dmixed * gate).astype(BF16)
            dys_ref[:, cols] = dy
            dgp = dmixed * ys_ref[:, cols].astype(F32) * gate * (1.0 - gate)
            dg_ref[:, cols] = dgp.astype(BF16)
            part = jnp.sum(dgp, axis=0, keepdims=True)

            @pl.when(i == 0)
            def _():
                dbg_ref[:, cols] = part

            @pl.when(i > 0)
            def _():
                dbg_ref[:, cols] += part

            acc = jnp.zeros((tm, BRANCH_W), F32)
            for j in range(N_CHIPS):
                acc = acc + _dot_nt(dy[:, j * cs:(j + 1) * cs], wp_ref[j])
            dbr[...] = acc.astype(BF16)

    return _pcall(
        body, hosted, name="mix_bwd", grid=(T // tm,),
        in_specs=[row(D), wo, row(3 * D), bg, row(3 * D), wp, wp, wp],
        out_specs=[row(3 * D), row(3 * D), row(BRANCH_W), row(BRANCH_W), row(BRANCH_W),
                   pl.BlockSpec((1, 3 * D), lambda i: (0, 0))],
        out_shape=[SDS((T, 3 * D), BF16), SDS((T, width), BF16), SDS((T, BRANCH_W), BF16),
                   SDS((T, BRANCH_W), BF16), SDS((T, BRANCH_W), BF16), SDS((1, 3 * D), F32)],
        semantics=("arbitrary",),
    )(dx1b, w_out, proj, b_gate, ys, wpa, wpp, wpc)


def loss_head(x2, gain, target):
    T, D = x2.shape
    tm = _tile(T, (512, 256, 128))

    def body(x_ref, g_ref, t_ref, loss_ref, dx_ref, dxb_ref, dg_ref):
        i = pl.program_id(0)
        xf = x_ref[...]
        g = g_ref[...]
        r = lax.rsqrt(jnp.mean(xf * xf, axis=-1, keepdims=True) + RMS_EPS)
        xhat = xf * r
        diff = xhat * g - t_ref[...]
        part_loss = 0.5 * jnp.sum(jnp.mean(diff * diff, axis=-1, keepdims=True), axis=0, keepdims=True)
        dy = diff * (1.0 / D)
        dhg = dy * g
        dx = r * (dhg - xhat * jnp.mean(dhg * xhat, axis=-1, keepdims=True))
        dx_ref[...] = dx
        dxb_ref[...] = dx.astype(BF16)
        part_g = jnp.sum(dy * xhat, axis=0, keepdims=True)
        part_l = jnp.broadcast_to(part_loss, (1, LANES))

        @pl.when(i == 0)
        def _():
            dg_ref[...] = part_g
            loss_ref[...] = part_l

        @pl.when(i > 0)
        def _():
            dg_ref[...] += part_g
            loss_ref[...] += part_l

    row = pl.BlockSpec((tm, D), lambda i: (i, 0))
    return pl.pallas_call(
        body, name="loss_head", grid=(T // tm,),
        in_specs=[row, pl.BlockSpec((1, D), lambda i: (0, 0)), row],
        out_specs=[pl.BlockSpec((1, LANES), lambda i: (0, 0)), row, row, pl.BlockSpec((1, D), lambda i: (0, 0))],
        out_shape=[SDS((1, LANES), F32), SDS((T, D), F32), SDS((T, D), BF16), SDS((1, D), F32)],
        compiler_params=_params("arbitrary"),
    )(x2, gain, target)


def _placement_constants():
    w = HEADS * HEAD_PAD
    pq = np.zeros((BRANCH_W, w), np.float32)
    pk = np.zeros((BRANCH_W, w), np.float32)
    pfq = np.zeros((3, LANES, w), np.float32)
    pfk = np.zeros((3, LANES, w), np.float32)
    cq = np.zeros((1, w), np.float32)
    ck = np.zeros((1, w), np.float32)
    eq = np.zeros((w, LANES), np.float32)
    ek = np.zeros((w, LANES), np.float32)
    for h in range(HEADS):
        for d in range(HEAD_DIM):
            pq[h * HEAD_DIM + d, h * HEAD_PAD + d] = HEAD_DIM ** -0.5
            pk[h * HEAD_DIM + d, h * HEAD_PAD + d] = 1.0
        for i in range(3):
            pfq[i, h, h * HEAD_PAD + HEAD_DIM + i] = 1.0
            pfk[i, h, h * HEAD_PAD + HEAD_DIM + 3 + i] = -1.0
            cq[0, h * HEAD_PAD + HEAD_DIM + 3 + i] = 1.0
            ck[0, h * HEAD_PAD + HEAD_DIM + i] = 1.0
        eq[h * HEAD_PAD + HEAD_DIM, h] = 1.0
        ek[h * HEAD_PAD + HEAD_DIM + 3, h] = -1.0
    bf = lambda a: jnp.asarray(a, BF16)
    return dict(pq=bf(pq), pk=bf(pk), pfq=bf(pfq), pfk=bf(pfk), cq=jnp.asarray(cq), ck=jnp.asarray(ck),
                pqkt=bf(np.concatenate([pq.T, pk.T], axis=0)), eq=bf(eq), ek=bf(ek))


def attn_prep(proj3, bf_rows, layer, cst, lay, hosted=None):
    Bl, S, _ = proj3.shape
    ts = ATTN_BLOCK
    w = HEADS * HEAD_PAD

    def body(q_ref, k_ref, f_ref, bf_ref, pq_ref, pk_ref, pfq_ref, pfk_ref, cq_ref, ck_ref,
             qa_ref, ka_ref, carry_ref):
        @pl.when(pl.program_id(1) == 0)
        def _():
            carry_ref[...] = jnp.zeros_like(carry_ref)

        z = f_ref[...].astype(F32) + bf_ref[...]
        logf = jnp.minimum(z, 0.0) - jnp.log(1.0 + jnp.exp(-jnp.abs(z)))
        r = lax.broadcasted_iota(jnp.int32, (ts, ts), 0)
        c = lax.broadcasted_iota(jnp.int32, (ts, ts), 1)
        tri = jnp.where(r >= c, 1.0, 0.0).astype(BF16)
        fcum = carry_ref[...]
        for part in _split3(logf):
            fcum = fcum + _dot(tri, part)
        carry_ref[...] = fcum[ts - 1:ts, :]
        qa = _dot(q_ref[...], pq_ref[...]) + cq_ref[...]
        ka = _dot(k_ref[...], pk_ref[...]) + ck_ref[...]
        for i, part in enumerate(_split3(fcum)):
            qa = qa + _dot(part, pfq_ref[i])
            ka = ka + _dot(part, pfk_ref[i])
        qa_ref[...] = qa.astype(BF16)
        ka_ref[...] = ka.astype(BF16)

    cfull = lambda shape: pl.BlockSpec(shape, lambda b, s: (0,) * len(shape))
    return _pcall(
        body, hosted, name="attn_prep", grid=(Bl, S // ts),
        in_specs=[pl.BlockSpec((None, ts, BRANCH_W), lambda b, s: (b, s, lay["q"] // BRANCH_W)),
                  pl.BlockSpec((None, ts, BRANCH_W), lambda b, s: (b, s, lay["k"] // BRANCH_W)),
                  pl.BlockSpec((None, ts, LANES), lambda b, s: (b, s, lay["f"] // LANES)),
                  pl.BlockSpec((None, 1, LANES), lambda b, s: (layer, 0, 0)),
                  cfull((BRANCH_W, w)), cfull((BRANCH_W, w)),
                  cfull((3, LANES, w)), cfull((3, LANES, w)), cfull((1, w)), cfull((1, w))],
        out_specs=[pl.BlockSpec((None, ts, w), lambda b, s: (b, s, 0)),
                   pl.BlockSpec((None, ts, w), lambda b, s: (b, s, 0))],
        out_shape=[SDS((Bl, S, w), BF16), SDS((Bl, S, w), BF16)],
        scratch_shapes=[pltpu.VMEM((1, LANES), F32)],
        semantics=("arbitrary", "arbitrary"),
    )(proj3, proj3, proj3, bf_rows, cst["pq"], cst["pk"], cst["pfq"], cst["pfk"], cst["cq"], cst["ck"])


def attn_fwd(qa, ka, proj3, lay, hosted=None):
    Bl, S, _ = qa.shape
    tq = ATTN_BLOCK
    nq = S // tq
    pairs = HEADS // 2
    pw = 2 * HEAD_PAD
    vw = 2 * HEAD_DIM

    def body(qa_ref, ka_ref, v_ref, o_ref, lse_ref):
        row = lax.broadcasted_iota(jnp.int32, (tq, tq), 0)
        col = lax.broadcasted_iota(jnp.int32, (tq, tq), 1)
        causal = row <= col
        for i in range(nq):
            nk = (i + 1) * tq
            rows = slice(i * tq, nk)
            o_t = []
            for h in range(2):
                hs = slice(h * HEAD_PAD, (h + 1) * HEAD_PAD)
                st = _dot_nt(ka_ref[0:nk, hs], qa_ref[rows, hs])
                diag = jnp.where(causal, st[nk - tq:], NEG_INF)
                m = jnp.max(diag, axis=0, keepdims=True)
                if i:
                    m = jnp.maximum(m, jnp.max(st[:nk - tq], axis=0, keepdims=True))
                p_diag = jnp.exp(diag - m)
                l = jnp.sum(p_diag, axis=0, keepdims=True)
                if i:
                    p_top = jnp.exp(st[:nk - tq] - m)
                    l = l + jnp.sum(p_top, axis=0, keepdims=True)
                    p = jnp.concatenate([p_top.astype(BF16), p_diag.astype(BF16)], axis=0)
                else:
                    p = p_diag.astype(BF16)
                acc = _dot_tn(v_ref[0:nk, :], p)
                o_t.append(acc[h * HEAD_DIM:(h + 1) * HEAD_DIM, :] / l)
                lse_ref[h:h + 1, rows] = m + jnp.log(l)
            o_ref[rows, :] = jnp.concatenate(o_t, axis=0).T.astype(BF16)

    return _pcall(
        body, hosted, name="attn_fwd", grid=(Bl, pairs),
        in_specs=[pl.BlockSpec((None, S, pw), lambda b, p: (b, 0, p)),
                  pl.BlockSpec((None, S, pw), lambda b, p: (b, 0, p)),
                  pl.BlockSpec((None, S, vw), lambda b, p: (b, 0, lay["v"] // vw + p))],
        out_specs=[pl.BlockSpec((None, S, vw), lambda b, p: (b, 0, p)),
                   pl.BlockSpec((None, None, 2, S), lambda b, p: (b, p, 0, 0))],
        out_shape=[SDS((Bl, S, BRANCH_W), BF16), SDS((Bl, pairs, 2, S), F32)],
        semantics=("arbitrary", "arbitrary"),
    )(qa, ka, proj3)


def attn_bwd(qa, ka, proj3, dao, ao, lse, dproj3, lay, hosted=None):
    Bl, S, _ = qa.shape
    tk = ATTN_BLOCK
    nq = S // tk
    pairs = HEADS // 2
    pw = 2 * HEAD_PAD
    vw = 2 * HEAD_DIM

    def body(qa_ref, ka_ref, v_ref, do_ref, o_ref, lse_ref, _, dqa_ref, dka_ref, dv_ref):
        row = lax.broadcasted_iota(jnp.int32, (tk, tk), 0)
        col = lax.broadcasted_iota(jnp.int32, (tk, tk), 1)
        causal = row <= col
        lane8 = lax.broadcasted_iota(jnp.int32, (8, vw), 1)
        lane_s = lax.broadcasted_iota(jnp.int32, (S, vw), 1)
        lane_k = lax.broadcasted_iota(jnp.int32, (tk, vw), 1)
        doo = do_ref[...].astype(F32) * o_ref[...].astype(F32)
        hi = doo.astype(BF16)
        lo = (doo - hi.astype(F32)).astype(BF16)
        delta, v_head = [], []
        for h in range(2):
            sel = jnp.where((lane8 >= h * HEAD_DIM) & (lane8 < (h + 1) * HEAD_DIM), 1.0, 0.0).astype(BF16)
            delta.append((_dot_nt(sel, hi) + _dot_nt(sel, lo))[0:1, :])
            in_head = (lane_s >= h * HEAD_DIM) & (lane_s < (h + 1) * HEAD_DIM)
            v_head.append(jnp.where(in_head, v_ref[...], jnp.zeros_like(v_ref[...])))
        dqa_ref[...] = jnp.zeros_like(dqa_ref)
        for j in range(nq):
            q0 = j * tk
            krows = slice(q0, q0 + tk)
            do = do_ref[q0:, :]
            dvs = []
            for h in range(2):
                hs = slice(h * HEAD_PAD, (h + 1) * HEAD_PAD)
                k = ka_ref[krows, hs]
                q = qa_ref[q0:, hs]
                st = _dot_nt(k, q)
                p = jnp.exp(st - lse_ref[h:h + 1, q0:])
                p_diag = jnp.where(causal, p[:, :tk], 0.0)
                p = jnp.concatenate([p_diag, p[:, tk:]], axis=1) if j < nq - 1 else p_diag
                dvs.append(_dot(p.astype(BF16), do))
                dpt = _dot_nt(v_head[h][krows, :], do)
                ds = (p * (dpt - delta[h][:, q0:])).astype(BF16)
                dka_ref[krows, hs] = _dot(ds, q)
                dqa_ref[q0:, hs] += _dot_tn(ds, k)
            dv_ref[krows, :] = jnp.where(lane_k < HEAD_DIM, dvs[0], dvs[1]).astype(BF16)

    seq = lambda w, c0=0: pl.BlockSpec((None, S, w), lambda b, p: (b, 0, c0 + p))
    return _pcall(
        body, hosted, name="attn_bwd", grid=(Bl, pairs),
        in_specs=[seq(pw), seq(pw), seq(vw, lay["v"] // vw), seq(vw), seq(vw),
                  pl.BlockSpec((None, None, 2, S), lambda b, p: (b, p, 0, 0)), _ANY],
        out_specs=[seq(pw), seq(pw), seq(vw, lay["v"] // vw)],
        out_shape=[SDS((Bl, S, HEADS * HEAD_PAD), F32), SDS((Bl, S, HEADS * HEAD_PAD), F32),
                   SDS(dproj3.shape, BF16)],
        aliases={6: 2}, semantics=("arbitrary", "arbitrary"),
    )(qa, ka, proj3, dao, ao, lse, dproj3)


def attn_post(dqa, dka, proj3, bf_rows, layer, dproj3, cst, lay, hosted=None):
    Bl, S, w = dqa.shape
    ts = ATTN_BLOCK
    ns = S // ts
    qkf = 2 * BRANCH_W + F_PAD

    def body(dqa_ref, dka_ref, f_ref, bf_ref, pqkt_ref, eq_ref, ek_ref, _, dqkf_ref, dbf_ref, carry_ref):
        b, s = pl.program_id(0), pl.program_id(1)

        @pl.when(s == 0)
        def _():
            carry_ref[...] = jnp.zeros_like(carry_ref)

        dqa_v, dka_v = dqa_ref[...], dka_ref[...]
        qh = dqa_v.astype(BF16)
        kh = dka_v.astype(BF16)
        dqkf_ref[:, :BRANCH_W] = _dot(qh, pqkt_ref[:w, :]).astype(BF16)
        dqkf_ref[:, BRANCH_W:2 * BRANCH_W] = _dot(kh, pqkt_ref[w:, :]).astype(BF16)
        ql = (dqa_v - qh.astype(F32)).astype(BF16)
        kl = (dka_v - kh.astype(F32)).astype(BF16)
        d_f = (_dot(qh, eq_ref[...]) + _dot(ql, eq_ref[...])) + (_dot(kh, ek_ref[...]) + _dot(kl, ek_ref[...]))
        r = lax.broadcasted_iota(jnp.int32, (ts, ts), 0)
        c = lax.broadcasted_iota(jnp.int32, (ts, ts), 1)
        triu = jnp.where(c >= r, 1.0, 0.0).astype(BF16)
        rev = carry_ref[...]
        for part in _split3(d_f):
            rev = rev + _dot(triu, part)
        carry_ref[...] = rev[0:1, :]
        z = f_ref[...].astype(F32) + bf_ref[...]
        lane = lax.broadcasted_iota(jnp.int32, (ts, LANES), 1)
        dfl = jnp.where(lane < HEADS, rev / (1.0 + jnp.exp(z)), 0.0)
        dqkf_ref[:, 2 * BRANCH_W:] = jnp.concatenate(
            [dfl.astype(BF16), jnp.zeros((ts, F_PAD - LANES), BF16)], axis=1)
        part = jnp.sum(dfl, axis=0, keepdims=True)

        @pl.when((b == 0) & (s == 0))
        def _():
            dbf_ref[...] = part

        @pl.when((b > 0) | (s > 0))
        def _():
            dbf_ref[...] += part

    assert lay["q"] % qkf == 0
    cfull = lambda shape: pl.BlockSpec(shape, lambda b, s: (0,) * len(shape))
    rev_blk = lambda wd, c0=0: pl.BlockSpec((None, ts, wd), lambda b, s: (b, ns - 1 - s, c0))
    return _pcall(
        body, hosted, name="attn_post", grid=(Bl, ns),
        in_specs=[rev_blk(w), rev_blk(w), rev_blk(LANES, lay["f"] // LANES),
                  pl.BlockSpec((None, 1, LANES), lambda b, s: (layer, 0, 0)),
                  cfull((2 * w, BRANCH_W)), cfull((w, LANES)), cfull((w, LANES)), _ANY],
        out_specs=[rev_blk(qkf, lay["q"] // qkf), cfull((1, LANES))],
        out_shape=[SDS(dproj3.shape, BF16), SDS((1, LANES), F32)],
        scratch_shapes=[pltpu.VMEM((1, LANES), F32)],
        aliases={7: 0}, semantics=("arbitrary", "arbitrary"),
    )(dqa, dka, proj3, bf_rows, cst["pqkt"], cst["eq"], cst["ek"], dproj3)


def _shift_down(x, k, row):
    return jnp.where(row >= k, pltpu.roll(x, k, axis=0), 0.0)


def _shift_up(x, k, row):
    n = x.shape[0]
    return jnp.where(row < n - k, pltpu.roll(x, n - k, axis=0), 0.0)


def _window_sum(x, g, row, shift):
    s2 = x + shift(x, 1, row)
    s4 = s2 + shift(s2, 2, row)
    s8 = s4 + shift(s4, 4, row)
    s16 = s8 + shift(s8, 8, row)
    return jnp.where(g == 0, s2, jnp.where(g == 1, s4, jnp.where(g == 2, s8, s16)))


def _window_count(g, row):
    wnd = jnp.where(g == 0, 2, jnp.where(g == 1, 4, jnp.where(g == 2, 8, 16)))
    return jnp.minimum(row + 1, wnd).astype(F32)


def _group_columns(ref):
    return [ref[:, n * GROUP_W:(n + 1) * GROUP_W].astype(F32) for n in range(4)]


def poolconv_fwd(proj3, pool_w, pool_scale, conv_w, layer, lay, hosted=None):
    Bl, S, _ = proj3.shape

    def body(x_ref, pw_ref, ps_ref, cw_ref, po_ref, co_ref):
        g = pl.program_id(1)
        row = lax.broadcasted_iota(jnp.int32, (S, GROUP_W), 0)
        u, cv, cb, cc = _group_columns(x_ref)
        d = _window_sum(u, g, row, _shift_down) / _window_count(g, row) - u
        po_ref[...] = (_dot(d.astype(BF16), pw_ref[...]) * ps_ref[...]).astype(BF16)
        z = cc * cv
        y = cw_ref[0:1, :] * _shift_down(z, 2, row) + cw_ref[1:2, :] * _shift_down(z, 1, row) + cw_ref[2:3, :] * z
        co_ref[...] = (cb * y).astype(BF16)

    out = pl.BlockSpec((None, S, GROUP_W), lambda b, g: (b, 0, g))
    return _pcall(
        body, hosted, name="poolconv_fwd", grid=(Bl, N_GROUPS),
        in_specs=[pl.BlockSpec((None, S, BRANCH_W), lambda b, g: (b, 0, lay["pc"] // BRANCH_W + g)),
                  pl.BlockSpec((None, None, GROUP_W, GROUP_W), lambda b, g: (layer, g, 0, 0)),
                  pl.BlockSpec((None, 1, GROUP_W), lambda b, g: (layer, 0, g)),
                  pl.BlockSpec((None, None, 3, GROUP_W), lambda b, g: (g, layer, 0, 0))],
        out_specs=[out, out],
        out_shape=[SDS((Bl, S, BRANCH_W), BF16), SDS((Bl, S, BRANCH_W), BF16)],
        semantics=("arbitrary", "arbitrary"),
    )(proj3, pool_w, pool_scale, conv_w)


def poolconv_bwd(proj3, dpo, dco, pool_w, pool_scale, conv_w, layer, dproj3, lay, hosted=None):
    Bl, S, _ = proj3.shape

    def body(x_ref, dpo_ref, dco_ref, pw_ref, ps_ref, cw_ref, _, dx_ref, dpw_ref, dps_ref, dcw_ref):
        g, b = pl.program_id(0), pl.program_id(1)
        row = lax.broadcasted_iota(jnp.int32, (S, GROUP_W), 0)
        cnt = _window_count(g, row)
        u, cv, cb, cc = _group_columns(x_ref)
        d = (_window_sum(u, g, row, _shift_down) / cnt - u).astype(BF16)
        pw = pw_ref[...]
        ypre = _dot(d, pw)
        dpo_v = dpo_ref[...].astype(F32)
        dps = jnp.sum(dpo_v * ypre, axis=0, keepdims=True)
        dyp = (dpo_v * ps_ref[...]).astype(BF16)
        dpw = _dot_tn(d, dyp)
        dd = _dot_nt(dyp, pw)
        dx_ref[:, 0:GROUP_W] = (_window_sum(dd / cnt, g, row, _shift_up) - dd).astype(BF16)

        z = cc * cv
        z1, z2 = _shift_down(z, 1, row), _shift_down(z, 2, row)
        w0, w1, w2 = cw_ref[0:1, :], cw_ref[1:2, :], cw_ref[2:3, :]
        y = w0 * z2 + w1 * z1 + w2 * z
        dco_v = dco_ref[...].astype(F32)
        dy = dco_v * cb
        dz = w0 * _shift_up(dy, 2, row) + w1 * _shift_up(dy, 1, row) + w2 * dy
        dx_ref[:, GROUP_W:2 * GROUP_W] = (dz * cc).astype(BF16)
        dx_ref[:, 2 * GROUP_W:3 * GROUP_W] = (dco_v * y).astype(BF16)
        dx_ref[:, 3 * GROUP_W:] = (dz * cv).astype(BF16)
        dcw = jnp.concatenate([jnp.sum(dy * z2, axis=0, keepdims=True),
                               jnp.sum(dy * z1, axis=0, keepdims=True),
                               jnp.sum(dy * z, axis=0, keepdims=True)], axis=0)

        @pl.when(b == 0)
        def _():
            dpw_ref[...] = dpw
            dps_ref[...] = dps
            dcw_ref[...] = dcw

        @pl.when(b > 0)
        def _():
            dpw_ref[...] += dpw
            dps_ref[...] += dps
            dcw_ref[...] += dcw

    blk = pl.BlockSpec((None, S, GROUP_W), lambda g, b: (b, 0, g))
    pc = pl.BlockSpec((None, S, BRANCH_W), lambda g, b: (b, 0, lay["pc"] // BRANCH_W + g))
    return _pcall(
        body, hosted, name="poolconv_bwd", grid=(N_GROUPS, Bl),
        in_specs=[pc, blk, blk,
                  pl.BlockSpec((None, None, GROUP_W, GROUP_W), lambda g, b: (layer, g, 0, 0)),
                  pl.BlockSpec((None, 1, GROUP_W), lambda g, b: (layer, 0, g)),
                  pl.BlockSpec((None, None, 3, GROUP_W), lambda g, b: (g, layer, 0, 0)), _ANY],
        out_specs=[pc, pl.BlockSpec((None, GROUP_W, GROUP_W), lambda g, b: (g, 0, 0)),
                   pl.BlockSpec((1, GROUP_W), lambda g, b: (0, g)),
                   pl.BlockSpec((None, 3, GROUP_W), lambda g, b: (g, 0, 0))],
        out_shape=[SDS(dproj3.shape, BF16), SDS((N_GROUPS, GROUP_W, GROUP_W), F32), SDS((1, BRANCH_W), F32),
                   SDS((N_GROUPS, 3, GROUP_W), F32)],
        aliases={6: 0}, semantics=("arbitrary", "arbitrary"),
    )(proj3, dpo, dco, pool_w, pool_scale, conv_w, dproj3)


def _tile_2d(rows, cols, n_arrays):
    budget = VMEM_LIMIT // 2
    lanes = -(-cols // LANES) * LANES
    if rows % 8 == 0:
        for t in range(min(rows, 2048), 7, -8):
            if rows % t == 0 and 2 * n_arrays * t * lanes * 4 <= budget:
                return t, cols
    for t in (1024, 512, 256, 128):
        if cols % t == 0 and 2 * n_arrays * (rows + 8) * t * 4 <= budget:
            return rows, t
    return rows, cols


def add_pair(kept, layer, where, received, name):
    _, n, _, R, C = kept.shape
    tr, tc = _tile_2d(R, C, 3)

    def body(where_ref, a_ref, b_ref, o_ref):
        o_ref[...] = (a_ref[...].astype(F32) + b_ref[...].astype(F32)).astype(BF16)

    blk = pl.BlockSpec((None, tr, tc), lambda d, i, j, where_ref: (d, i, j))
    grid_spec = pltpu.PrefetchScalarGridSpec(
        num_scalar_prefetch=1, grid=(n, R // tr, C // tc),
        in_specs=[pl.BlockSpec((None, None, None, tr, tc),
                               lambda d, i, j, where_ref: (layer, d, where_ref[0], i, j)), blk],
        out_specs=blk)
    return pl.pallas_call(body, name=name, grid_spec=grid_spec, out_shape=SDS((n, R, C), BF16),
                          compiler_params=_params("arbitrary", "arbitrary", "arbitrary"))(where, kept, received)


def add_chips(arrived, own, layer, where, n_layers, prev, name):
    _, R, C = arrived.shape
    tr, tc = _tile_2d(R, C, 6)

    def body(where_ref, a0, a1, a2, a3, own_ref, *rest):
        o_ref = rest[-1]
        chip = where_ref[1]
        acc = None
        for j, a_ref in enumerate((a0, a1, a2, a3)):
            term = jnp.where(chip == j, own_ref[...], a_ref[...]).astype(F32)
            acc = term if acc is None else acc + term
        o_ref[...] = acc

    def slot(j):
        return pl.BlockSpec((None, tr, tc), lambda i, k, where_ref, j=j: (
            jnp.where(where_ref[1] == j, (j + 1) % N_CHIPS, j), i, k))

    in_specs = [slot(j) for j in range(N_CHIPS)] + [
        pl.BlockSpec((None, tr, tc), lambda i, k, where_ref: (where_ref[1], i, k))]
    args = [where, arrived, arrived, arrived, arrived, own]
    aliases = {}
    if prev is not None:
        in_specs.append(_ANY)
        args.append(prev)
        aliases = {len(args) - 1: 0}
    grid_spec = pltpu.PrefetchScalarGridSpec(
        num_scalar_prefetch=1, grid=(R // tr, C // tc), in_specs=in_specs,
        out_specs=pl.BlockSpec((None, None, tr, tc), lambda i, k, where_ref: (layer, where_ref[0], i, k)))
    return pl.pallas_call(body, name=name, grid_spec=grid_spec, out_shape=SDS((n_layers, 2, R, C), F32),
                          input_output_aliases=aliases,
                          compiler_params=_params("arbitrary", "arbitrary"))(*args)


def adamw(w, g, m, v, name):
    if w.ndim == 2:
        R, C = w.shape
        tr, _ = _tile_2d(R, C, 7)
        grid, blk = (R // tr,), pl.BlockSpec((tr, C), lambda i: (i, 0))
    else:
        N, r, C = w.shape
        tn = max(t for t in range(1, N + 1) if N % t == 0 and t * r * C * 4 <= 1024 * 1024)
        grid, blk = (N // tn,), pl.BlockSpec((tn, r, C), lambda i: (i, 0, 0))

    def body(w_ref, g_ref, m_ref, v_ref, d_ref, nm_ref, nv_ref):
        gv = g_ref[...]
        m_new = ADAM_B1 * m_ref[...] + (1.0 - ADAM_B1) * gv
        v_new = ADAM_B2 * v_ref[...] + (1.0 - ADAM_B2) * (gv * gv)
        m_hat = m_new / (1.0 - ADAM_B1 ** ADAM_STEP)
        v_hat = v_new / (1.0 - ADAM_B2 ** ADAM_STEP)
        d_ref[...] = -ADAM_LR * (m_hat / (jnp.sqrt(v_hat) + ADAM_EPS) + ADAM_WD * w_ref[...])
        nm_ref[...] = m_new
        nv_ref[...] = v_new

    out = SDS(w.shape, F32)
    return pl.pallas_call(body, name=name, grid=grid, in_specs=[blk] * 4, out_specs=[blk] * 3,
                          out_shape=[out, out, out], compiler_params=_params("arbitrary"))(w, g, m, v)


_COMM = pltpu.CompilerParams(has_side_effects=True)


def gather_buffers(shards):
    me_chip = 2 * lax.axis_index("x") + lax.axis_index("y")
    pool = {}
    for name, sh in shards.items():
        L, r, c = sh.shape
        if name in ROW_SHARDED:
            pool[name] = lax.dynamic_update_slice(lax.empty((L, N_CHIPS, r, c), sh.dtype), sh[:, None],
                                                  (0, me_chip, 0, 0))
        else:
            pool[name] = lax.dynamic_update_slice(lax.empty((N_CHIPS, L, r, c), sh.dtype), sh[None],
                                                  (me_chip, 0, 0, 0))
    return pool


def comm_now(pool, stages, name):
    stages = [Hosted(pool, jobs) for jobs in stages]
    names = sorted({m for st in stages for m in st.names})
    n = len(names)

    def body(*refs):
        bufs = dict(zip(names, refs[n:2 * n]))
        sems = refs[2 * n:]
        for i, st in enumerate(stages):
            plan = _hosted_plan(st, bufs, sems[2 * i], sems[2 * i + 1])
            _hosted_start(plan, True)
            _hosted_finish(plan, True)

    sem = pltpu.SemaphoreType.DMA
    scratch = []
    for st in stages:
        scratch += [sem((len(st.jobs), 3)), sem((len(st.jobs), 3))]
    res = pl.pallas_call(
        body, name=name, in_specs=[_ANY] * n, out_specs=[_ANY] * n,
        out_shape=[SDS(pool[m].shape, pool[m].dtype) for m in names],
        scratch_shapes=scratch, input_output_aliases={t: t for t in range(n)},
        compiler_params=_COMM,
    )(*[pool[m] for m in names])
    pool.update(zip(names, res))


def gather_now(pool, units):
    comm_now(pool, [[("ici", name, layer) for name, layer in units],
                    [("fwd", name, layer) for name, layer in units]], "gather_now")


def allgather_chips(buf, name):
    def body(src_ref, out_ref, send_sems, recv_sems, local_sem):
        x, y, c = _position()
        me = 2 * x + y
        mine = pltpu.make_async_copy(src_ref, out_ref.at[me], local_sem)
        mine.start()
        sends = []
        for k, (px, py) in enumerate(_other_chips(x, y)):
            cp = _remote(src_ref, out_ref.at[me], send_sems.at[k], recv_sems.at[k], (px, py, c))
            cp.start()
            sends.append(cp)
        for k, (px, py) in enumerate(_other_chips(x, y)):
            _remote(src_ref, out_ref.at[2 * px + py], send_sems.at[k], recv_sems.at[k], (px, py, c)).wait_recv()
        for cp in sends:
            cp.wait_send()
        mine.wait()

    sem = pltpu.SemaphoreType.DMA
    return pl.pallas_call(
        body, name=name, in_specs=[_ANY], out_specs=_ANY, out_shape=SDS((N_CHIPS,) + buf.shape, buf.dtype),
        scratch_shapes=[sem((3,)), sem((3,)), sem], compiler_params=_COMM,
    )(buf)


BIG = ("w_in", "w_proj_attn", "w_proj_pool", "w_proj_conv", "conv_w", "w_out", "w_gate_up", "w_down")
REPLICATED = ("attn_norm", "b_forget", "b_gate", "pool_w", "pool_scale", "ffn_norm", "final_norm")
ORDER = ("attn_norm", "w_in", "b_forget", "b_gate", "w_proj_attn", "pool_w", "pool_scale", "w_proj_pool",
         "conv_w", "w_proj_conv", "w_out", "ffn_norm", "w_gate_up", "w_down", "final_norm")


def _proj_layout(D):
    lay = {"g": 0, "q": 3 * D}
    lay["k"] = lay["q"] + BRANCH_W
    lay["f"] = lay["k"] + BRANCH_W
    lay["v"] = lay["f"] + F_PAD
    lay["pc"] = lay["v"] + BRANCH_W
    lay["width"] = lay["pc"] + 4 * BRANCH_W
    return lay


_REF = dict(q=0, k=512, v=1024, f=1536, u=1544, cv=2056, cb=2568, cc=3080, g=3592)


def _packed_pieces(D):
    pieces = [(_REF["g"], 3 * D), (_REF["q"], BRANCH_W), (_REF["k"], BRANCH_W), (_REF["f"], HEADS),
              (None, F_PAD - HEADS), (_REF["v"], BRANCH_W)]
    for gi in range(N_GROUPS):
        pieces += [(_REF[name] + gi * GROUP_W, GROUP_W) for name in ("u", "cv", "cb", "cc")]
    return pieces


def _packed_runs(D, cs):
    runs, at = [], 0
    for start, n in _packed_pieces(D):
        if start is None:
            runs.append((at, None, 0, n))
            at += n
        while start is not None and n:
            chip, off = divmod(start, cs)
            take = min(n, cs - off)
            runs.append((at, chip, off, take))
            at, start, n = at + take, start + take, n - take
    return runs


def pack_w_in(shards, layer):
    _, _, cs, D = shards.shape
    runs = _packed_runs(D, cs)
    width = runs[-1][0] + runs[-1][3]
    tc = _tile(D, (256, 128))

    def body(s_ref, o_ref):
        for dst, chip, off, rows in runs:
            if chip is None:
                o_ref[dst:dst + rows, :] = jnp.zeros((rows, tc), s_ref.dtype)
            else:
                o_ref[dst:dst + rows, :] = s_ref[chip, off:off + rows, :]

    return pl.pallas_call(
        body, name="pack_w_in", grid=(D // tc,),
        in_specs=[pl.BlockSpec((N_CHIPS, None, cs, tc), lambda j: (0, layer, 0, j))],
        out_specs=pl.BlockSpec((width, tc), lambda j: (0, j)),
        out_shape=SDS((width, D), shards.dtype), compiler_params=_params("arbitrary"),
    )(shards)


def unpack_w_in(p, cs):
    width, D = p.shape
    half = cs // 2
    runs = []
    for src, chip, off, rows in _packed_runs(D, cs):
        while chip is not None and rows:
            h, at = divmod(off, half)
            take = min(rows, half - at)
            runs.append((src, chip, h, at, take))
            src, off, rows = src + take, off + take, rows - take
    tc = _tile(D, (256, 128))

    def body(p_ref, o_ref):
        for src, chip, h, at, rows in runs:
            o_ref[chip, h, at:at + rows, :] = p_ref[src:src + rows, :]

    return pl.pallas_call(
        body, name="unpack_w_in", grid=(D // tc,),
        in_specs=[pl.BlockSpec((width, tc), lambda j: (0, j))],
        out_specs=pl.BlockSpec((N_CHIPS, 2, half, tc), lambda j: (0, 0, 0, j)),
        out_shape=SDS((N_CHIPS, 2, half, D), p.dtype), compiler_params=_params("arbitrary"),
    )(p)


def _split_flat(vec, shapes):
    out, at = [], 0
    for shp in shapes:
        n = int(np.prod(shp))
        out.append(vec[at:at + n].reshape(shp))
        at += n
    return out


def kernel(x, attn_norm, w_in, b_forget, b_gate, w_proj_attn, pool_w, pool_scale, w_proj_pool, conv_w, w_proj_conv, w_out, ffn_norm, w_gate_up, w_down, final_norm, loss_target, m_attn_norm, m_w_in, m_b_forget, m_b_gate, m_w_proj_attn, m_pool_w, m_pool_scale, m_w_proj_pool, m_conv_w, m_w_proj_conv, m_w_out, m_ffn_norm, m_w_gate_up, m_w_down, m_final_norm, v_attn_norm, v_w_in, v_b_forget, v_b_gate, v_w_proj_attn, v_pool_w, v_pool_scale, v_w_proj_pool, v_conv_w, v_w_proj_conv, v_w_out, v_ffn_norm, v_w_gate_up, v_w_down, v_final_norm):
    weights = dict(attn_norm=attn_norm, w_in=w_in, b_forget=b_forget, b_gate=b_gate, w_proj_attn=w_proj_attn,
                   pool_w=pool_w, pool_scale=pool_scale, w_proj_pool=w_proj_pool, conv_w=conv_w,
                   w_proj_conv=w_proj_conv, w_out=w_out, ffn_norm=ffn_norm, w_gate_up=w_gate_up, w_down=w_down,
                   final_norm=final_norm)
    mom_m = dict(attn_norm=m_attn_norm, w_in=m_w_in, b_forget=m_b_forget, b_gate=m_b_gate, w_proj_attn=m_w_proj_attn,
                 pool_w=m_pool_w, pool_scale=m_pool_scale, w_proj_pool=m_w_proj_pool, conv_w=m_conv_w,
                 w_proj_conv=m_w_proj_conv, w_out=m_w_out, ffn_norm=m_ffn_norm, w_gate_up=m_w_gate_up,
                 w_down=m_w_down, final_norm=m_final_norm)
    mom_v = dict(attn_norm=v_attn_norm, w_in=v_w_in, b_forget=v_b_forget, b_gate=v_b_gate, w_proj_attn=v_w_proj_attn,
                 pool_w=v_pool_w, pool_scale=v_pool_scale, w_proj_pool=v_w_proj_pool, conv_w=v_conv_w,
                 w_proj_conv=v_w_proj_conv, w_out=v_w_out, ffn_norm=v_ffn_norm, w_gate_up=v_w_gate_up,
                 w_down=v_w_down, final_norm=v_final_norm)

    Bl, S, D = x.shape
    T = Bl * S
    L = w_in.shape[0]
    F = w_down.shape[1] * N_CHIPS
    lay = _proj_layout(D)
    cst = _placement_constants()
    assert L == N_LAYERS and S % ATTN_BLOCK == 0 and F % (2 * LANES) == 0 and D % BRANCH_W == 0
    assert w_in.shape[2] * N_CHIPS == _REF["g"] + 3 * D and conv_w.shape[2] == GROUP_W

    send = {n: weights[n].astype(BF16) for n in BIG}
    send["conv_w"] = conv_w
    me_chip = 2 * lax.axis_index("x") + lax.axis_index("y")
    send["w_in"] = w_in.transpose(0, 2, 1).astype(BF16)
    pool = gather_buffers(send)
    gather_now(pool, [("w_in", 0)])
    rest = ("w_out", "w_proj_attn", "w_proj_pool", "w_gate_up", "w_proj_conv", "conv_w")
    late = ("w_out", "w_proj_attn", "w_proj_pool", "w_proj_conv", "conv_w")
    jobs = lambda kind, names, layer: [(kind, n, layer) for n in names]
    carried = {
        ("in_proj", 0): jobs("ici", rest, 0),
        ("attn_prep", 0): jobs("fwd", late, 0),
        ("attn_fwd", 0): jobs("fwd", ("w_gate_up",), 0) + jobs("ici", ("w_in",), 1) + jobs("ici", ("w_down",), 0),
        ("poolconv_fwd", 0): jobs("fwd", ("w_in",), 1) + jobs("fwd", ("w_down",), 0),
        ("mix_fwd", 0): jobs("ici", ("w_down",), 1),
        ("gate_up_proj", 0): jobs("ici", late, 1) + jobs("fwd", ("w_down",), 1),
        ("ffn_down_fwd", 0): jobs("ici", ("w_gate_up",), 1),
        ("in_proj", 1): jobs("fwd", ("w_gate_up",) + late, 1),
    }
    carry = lambda call, layer: Hosted(pool, carried[call, layer]) if (call, layer) in carried else None
    w_down_f = lambda: pool["w_down"].reshape(L, F, D)
    pool_w_b = pool_w.astype(BF16)
    an3, fn3 = attn_norm.reshape(L, 1, D), ffn_norm.reshape(L, 1, D)
    bg3, ps3 = b_gate.reshape(L, 1, 3 * D), pool_scale.reshape(L, 1, BRANCH_W)
    bf3 = jnp.pad(b_forget, ((0, 0), (0, LANES - HEADS))).reshape(L, 1, LANES)

    xs = x.reshape(T, D)
    saved = []
    w_in_p = []
    for l in range(L):
        w_in_p.append(pack_w_in(pool["w_in"], l))
        proj, h = norm_matmul(xs, an3, w_in_p[l], l, "rows", "in_proj", carry("in_proj", l))
        proj3 = proj.reshape(Bl, S, lay["width"])
        qa, ka = attn_prep(proj3, bf3, l, cst, lay, carry("attn_prep", l))
        ao, lse = attn_fwd(qa, ka, proj3, lay, carry("attn_fwd", l))
        po, co = poolconv_fwd(proj3, pool_w_b, ps3, pool["conv_w"], l, lay, carry("poolconv_fwd", l))
        ao2, po2, co2 = (a.reshape(T, BRANCH_W) for a in (ao, po, co))
        x1, ys, mixed = mix_fwd(ao2, po2, co2, proj, bg3, pool["w_proj_attn"], pool["w_proj_pool"],
                                pool["w_proj_conv"], pool["w_out"], l, xs, carry("mix_fwd", l))
        ab, h2 = norm_matmul(x1, fn3, pool["w_gate_up"], l, "by_shard", "gate_up_proj", carry("gate_up_proj", l))
        x2, s_act = ffn_down_fwd(ab, w_down_f(), l, x1, carry("ffn_down_fwd", l))
        saved.append(dict(x=xs, proj=proj, proj3=proj3, h=h, qa=qa, ka=ka, ao=ao, lse=lse, ao2=ao2, po2=po2,
                          co2=co2, ys=ys, mixed=mixed, x1=x1, ab=ab, h2=h2, s=s_act))
        xs = x2
    w_gu, w_o, conv_w_g = pool["w_gate_up"], pool["w_out"], pool["conv_w"]
    wpa, wpp, wpc = pool["w_proj_attn"], pool["w_proj_pool"], pool["w_proj_conv"]
    w_down_f = w_down_f()

    loss_row, dx, dxb, g_final = loss_head(xs, final_norm.reshape(1, D), loss_target.reshape(T, D))
    loss = lax.psum(loss_row[0, 0], AXES)

    reduced_names = tuple(n for n in BIG if n != "conv_w")
    early_names = tuple(n for n in reduced_names if n != "w_in")
    proj_names = ("w_out", "w_proj_attn", "w_proj_pool", "w_proj_conv")
    first_names = ("w_in", "w_gate_up", "w_down")
    where = jnp.stack([lax.axis_index("c"), me_chip]).astype(jnp.int32)
    rs = {}

    def reduce_begin(layer, grads):
        for n, g in grads.items():
            g5 = g.reshape((1, N_CHIPS, 2, -1) + g.shape[-1:])
            rs["g%d:%s" % (layer, n)] = g5
            for role in "ra":
                rs["%s%d:%s" % (role, layer, n)] = lax.empty((N_CHIPS,) + g5.shape[3:], BF16)

    swap_jobs = lambda layer, names: [("swap", "g%d:%s" % (layer, n), "r%d:%s" % (layer, n), 0) for n in names]
    xchg_jobs = lambda layer, names: [("xchg", "s%d:%s" % (layer, n), "a%d:%s" % (layer, n)) for n in names]
    join_jobs = lambda layer, names: [("join", "o:" + n, layer) for n in names]

    def pair_sums(layer, names):
        for n in names:
            rs["s%d:%s" % (layer, n)] = add_pair(rs["g%d:%s" % (layer, n)], 0, where, rs["r%d:%s" % (layer, n)],
                                                 "add_pair_" + n)

    def chip_sums(layer, names, slot, n_slots):
        for n in names:
            rs["o:" + n] = add_chips(rs["a%d:%s" % (layer, n)], rs["s%d:%s" % (layer, n)], slot, where, n_slots,
                                     rs.get("o:" + n), "add_chips_" + n)

    small = {n: [None] * L for n in REPLICATED if n != "final_norm"}
    g_conv = [None] * L
    to3 = lambda a: a.reshape(Bl, S, -1)
    for l in reversed(range(L)):
        sv = saved[l]
        behind = (lambda jobs: Hosted(rs, jobs)) if l == 0 else (lambda jobs: None)
        grads = {}
        da, db = ffn_down_bwd(dxb, w_down_f, l, sv["ab"], behind(swap_jobs(1, reduced_names)))
        if l == 0:
            pair_sums(1, reduced_names)
        grads["w_down"] = matmul_tn(sv["s"], [dxb], "grad_w_down", hosted=behind(xchg_jobs(1, ("w_down",))))
        grads["w_gate_up"] = matmul_tn(sv["h2"], [da, db], "grad_w_gate_up", by_dest=True, tn=2 * F // N_CHIPS,
                                       tk=_tile(T, (1024, 512, 256)), hosted=behind(xchg_jobs(1, ("w_gate_up",))))
        dx1, dx1b, g_fn = matmul_nt_normbwd([da, db], w_gu, l, "by_shard", sv["x1"], fn3, dx, "gate_up_bwd",
                                            behind(xchg_jobs(1, ("w_in",))))
        small["ffn_norm"][l] = g_fn[0]
        if l == 0:
            chip_sums(1, first_names, 1, L)
        dys, dproj, dao, dpo, dco, g_bg = mix_bwd(dx1b, w_o, sv["proj"], bg3, sv["ys"], wpa, wpp, wpc, l,
                                                  lay["width"],
                                                  behind(xchg_jobs(1, proj_names) + join_jobs(1, first_names)))
        if l == 0:
            chip_sums(1, proj_names, 1, L)
        small["b_gate"][l] = g_bg[0]
        grads["w_out"] = matmul_tn(sv["mixed"], [dx1b], "grad_w_out")
        for n, (name, br) in enumerate((("w_proj_attn", sv["ao2"]), ("w_proj_pool", sv["po2"]),
                                        ("w_proj_conv", sv["co2"]))):
            grads[name] = matmul_tn(br, [dys], "grad_" + name, b_col0=n * D, n_cols=D, by_dest=True,
                                    tn=D // N_CHIPS)
        if l == 0:
            reduce_begin(0, grads)
        dqa, dka, dproj3 = attn_bwd(sv["qa"], sv["ka"], sv["proj3"], to3(dao), sv["ao"], sv["lse"], to3(dproj), lay,
                                    behind(swap_jobs(0, early_names) + join_jobs(1, proj_names)))
        if l == 0:
            pair_sums(0, early_names)
        dproj3, g_bf = attn_post(dqa, dka, sv["proj3"], bf3, l, dproj3, cst, lay, behind(xchg_jobs(
            0, ("w_out", "w_proj_attn", "w_proj_pool", "w_proj_conv"))))
        small["b_forget"][l] = g_bf[0, :HEADS]
        dproj3, g_pw, g_ps, g_conv[l] = poolconv_bwd(sv["proj3"], to3(dpo), to3(dco), pool_w_b, ps3, conv_w_g, l,
                                                     dproj3, lay, behind(xchg_jobs(0, ("w_down",))))
        small["pool_w"][l], small["pool_scale"][l] = g_pw, g_ps[0]
        dproj = dproj3.reshape(T, lay["width"])
        g_w_in = unpack_w_in(matmul_tn(dproj, [sv["h"]], "grad_w_in", hosted=behind(xchg_jobs(
            0, ("w_gate_up",)))), w_in.shape[2])
        if l:
            reduce_begin(l, {**grads, "w_in": g_w_in})
        else:
            reduce_begin(0, {"w_in": g_w_in})
            comm_now(rs, [swap_jobs(0, ("w_in",))], "swap_w_in_halves")
            pair_sums(0, ("w_in",))
        dx, dxb, g_an = matmul_nt_normbwd([dproj], w_in_p[l], l, "rows", sv["x"], an3, dx1, "in_proj_bwd",
                                          behind(xchg_jobs(0, ("w_in",))))
        small["attn_norm"][l] = g_an[0]
    grad_x = dx.reshape(Bl, S, D)

    small_shapes = [weights[n].shape for n in REPLICATED] + [(L, N_CHIPS) + conv_w.shape[1:]]
    small_vec = jnp.concatenate([jnp.stack(small[n]).reshape(-1) for n in REPLICATED[:-1]]
                                + [g_final[0], jnp.stack(g_conv).reshape(-1)])
    n_small = small_vec.shape[0]
    small_vec = jnp.pad(small_vec, (0, -n_small % (2 * N_CHIPS * 16 * LANES))).astype(BF16)
    rs["g0:small"] = small_vec.reshape(1, N_CHIPS, 2, -1, LANES)
    for role in "ra":
        rs[role + "0:small"] = lax.empty((N_CHIPS,) + rs["g0:small"].shape[3:], BF16)
    last = ("small",)
    comm_now(rs, [swap_jobs(0, last)], "swap_grad_halves")
    pair_sums(0, last)
    comm_now(rs, [xchg_jobs(0, last)], "exchange_grad_chips")
    chip_sums(0, reduced_names, 0, L)
    chip_sums(0, ("small",), 0, 1)
    comm_now(rs, [join_jobs(0, reduced_names + ("small",))], "join_grad_halves")
    shard_grads = {n: rs["o:" + n].reshape((L, -1) + rs["o:" + n].shape[-1:]) for n in reduced_names}
    small_all = allgather_chips(rs["o:small"].reshape(-1, LANES), "allgather_small_grads").reshape(-1)[:n_small]
    *rep_list, conv_all = _split_flat(small_all, small_shapes)
    rep_grads = dict(zip(REPLICATED, rep_list))
    shard_grads["conv_w"] = lax.dynamic_index_in_dim(conv_all, me_chip, 1, keepdims=False)

    delta, new_m, new_v = {}, {}, {}
    for n in BIG:
        shp = weights[n].shape
        if n == "w_in":
            view, back = (lambda a: a.transpose(2, 0, 1)), (lambda a: a.transpose(1, 2, 0))
            g = shard_grads[n].transpose(1, 0, 2)
        else:
            view, back = (lambda a: a.reshape(-1, shp[-1])), (lambda a: a.reshape(shp))
            g = view(shard_grads[n])
        d, nm, nv = adamw(view(weights[n]), g, view(mom_m[n]), view(mom_v[n]), "adamw_" + n)
        delta[n], new_m[n], new_v[n], shard_grads[n] = back(d), back(nm), back(nv), back(g)

    def rows(d):
        vec = jnp.concatenate([d[n].reshape(-1) for n in REPLICATED])
        return jnp.pad(vec, (0, -vec.shape[0] % (8 * LANES))).reshape(-1, LANES)

    outs = adamw(rows(weights), rows(rep_grads), rows(mom_m), rows(mom_v), "adamw_replicated")
    for res, o in zip((delta, new_m, new_v), outs):
        res.update(zip(REPLICATED, _split_flat(o.reshape(-1), small_shapes[:len(REPLICATED)])))
    all_grads = {**shard_grads, **rep_grads}

    return (loss, grad_x, *[all_grads[n] for n in ORDER], *[delta[n] for n in ORDER],
            *[new_m[n] for n in ORDER], *[new_v[n] for n in ORDER])
```

```python
import numpy as np
import jax
import jax.numpy as jnp
from jax import lax
from jax.experimental import pallas as pl
from jax.experimental.pallas import tpu as pltpu

F32, BF16 = jnp.float32, jnp.bfloat16
SDS = jax.ShapeDtypeStruct
MESH = pl.DeviceIdType.MESH
AXES = ("x", "y", "c")
N_CHIPS = 4
N_LAYERS = 2
LANES = 128
VMEM_LIMIT = 48 * 1024 * 1024

HEADS, HEAD_DIM = 8, 64
HEAD_PAD = 128
BRANCH_W = 512
GROUP_W = 128
N_GROUPS = BRANCH_W // GROUP_W
POOL_WINDOWS = (2, 4, 8, 16)
F_PAD = 512
ATTN_BLOCK = 256
RMS_EPS = 1e-6
NEG_INF = -1e30
ADAM_LR, ADAM_B1, ADAM_B2, ADAM_EPS, ADAM_WD, ADAM_STEP = 0.001, 0.9, 0.999, 1e-08, 0.01, 10

NT = (((1,), (1,)), ((), ()))
TN = (((0,), (0,)), ((), ()))
_ANY = pl.BlockSpec(memory_space=pl.ANY)


def _tile(n, prefs):
    for p in prefs:
        if n % p == 0:
            return p
    raise ValueError(f"no tile of {prefs} divides {n}")


def _params(*sem):
    return pltpu.CompilerParams(dimension_semantics=sem, vmem_limit_bytes=VMEM_LIMIT)


def _sigmoid(z):
    return 0.5 * jnp.tanh(0.5 * z) + 0.5


def _split3(x):
    h1 = x.astype(BF16)
    r1 = x - h1.astype(F32)
    h2 = r1.astype(BF16)
    h3 = (r1 - h2.astype(F32)).astype(BF16)
    return h1, h2, h3


def _position():
    return lax.axis_index("x"), lax.axis_index("y"), lax.axis_index("c")


def _other_chips(x, y):
    return [(1 - x, y), (x, 1 - y), (1 - x, 1 - y)]


def _remote(src, dst, send_sem, recv_sem, device):
    return pltpu.make_async_remote_copy(src_ref=src, dst_ref=dst, send_sem=send_sem, recv_sem=recv_sem,
                                        device_id=device, device_id_type=MESH)


ROW_SHARDED = ("w_out", "w_down")
FETCHER = dict(w_in=0, w_out=0, w_proj_attn=0, w_proj_pool=0, w_gate_up=1, w_down=1, w_proj_conv=1, conv_w=1)


class Hosted:
    def __init__(self, pool, jobs):
        self.pool, self.jobs = pool, list(jobs)
        names = set()
        for job in self.jobs:
            names.update(job[1:3] if job[0] in ("swap", "xchg") else job[1:2])
        self.names = sorted(names)


def _hosted_plan(hosted, refs, send_sems, recv_sems):
    x, y, c = _position()
    me = 2 * x + y
    others = _other_chips(x, y)
    sibling = (x, y, 1 - c)
    plan = []
    for j, job in enumerate(hosted.jobs):
        kind = job[0]
        sems = lambda k, j=j: (send_sems.at[j, k], recv_sems.at[j, k])
        if kind in ("ici", "fwd"):
            _, name, layer = job
            ref = refs[name]
            win = (lambda chip, ref=ref, layer=layer: ref.at[layer, chip]) if name in ROW_SHARDED else (
                lambda chip, ref=ref, layer=layer: ref.at[chip, layer])
            mine = c == FETCHER[name]
            if kind == "ici":
                sends = [_remote(win(me), win(me), *sems(k), (px, py, c)) for k, (px, py) in enumerate(others)]
                arrivals = [_remote(win(2 * px + py), win(2 * px + py), *sems(k), (px, py, c))
                            for k, (px, py) in enumerate(others)]
                plan.append((mine, sends, arrivals, []))
            else:
                sends = [_remote(win(2 * px + py), win(2 * px + py), *sems(k), sibling)
                         for k, (px, py) in enumerate(others)]
                plan.append((mine, sends, [], sends))
        elif kind == "swap":
            _, src, dst, layer = job
            cp = _remote(refs[src].at[layer, :, 1 - c], refs[dst], *sems(0), sibling)
            plan.append((True, [cp], [cp], []))
        elif kind == "xchg":
            _, src, dst = job
            sends = [_remote(refs[src].at[2 * px + py], refs[dst].at[me], *sems(k), (px, py, c))
                     for k, (px, py) in enumerate(others)]
            arrivals = [_remote(refs[src].at[me], refs[dst].at[2 * px + py], *sems(k), (px, py, c))
                        for k, (px, py) in enumerate(others)]
            plan.append((True, sends, arrivals, []))
        else:
            _, name, layer = job
            ref = refs[name]
            cp = _remote(ref.at[layer, c], ref.at[layer, c], *sems(0), sibling)
            arrival = _remote(ref.at[layer, c], ref.at[layer, 1 - c], *sems(0), sibling)
            plan.append((True, [cp], [arrival], []))
    return plan


def _hosted_start(plan, now):
    for mine, sends, _, _ in plan:
        @pl.when(now & mine)
        def _(sends=sends):
            for cp in sends:
                cp.start()


def _hosted_finish(plan, now):
    for mine, sends, arrivals, sibling_arrivals in plan:
        @pl.when(now & mine)
        def _(sends=sends, arrivals=arrivals):
            for cp in arrivals:
                cp.wait_recv()
            for cp in sends:
                cp.wait_send()

        if sibling_arrivals:
            @pl.when(now & jnp.logical_not(mine))
            def _(sibling_arrivals=sibling_arrivals):
                for cp in sibling_arrivals:
                    cp.wait_recv()


def _pcall(body, hosted, *, name, grid, in_specs, out_specs, out_shape, semantics, scratch_shapes=(), aliases=None):
    aliases = dict(aliases or {})
    if hosted is None or not hosted.jobs:
        return pl.pallas_call(body, name=name, grid=grid, in_specs=in_specs, out_specs=out_specs,
                              out_shape=out_shape, scratch_shapes=list(scratch_shapes),
                              input_output_aliases=aliases, compiler_params=_params(*semantics))
    single = not isinstance(out_shape, (list, tuple))
    out_specs_l = [out_specs] if single else list(out_specs)
    out_shape_l = [out_shape] if single else list(out_shape)
    n_in, n_out, n_buf, n_job = len(in_specs), len(out_specs_l), len(hosted.names), len(hosted.jobs)

    def carrying(*refs):
        ins, outs = refs[:n_in], refs[n_in + n_buf:n_in + n_buf + n_out]
        bufs = refs[n_in + n_buf + n_out:n_in + 2 * n_buf + n_out]
        rest = refs[n_in + 2 * n_buf + n_out:]
        scratch, send_sems, recv_sems = rest[:-2], rest[-2], rest[-1]
        first, last = True, True
        for axis, size in enumerate(grid):
            first = first & (pl.program_id(axis) == 0)
            last = last & (pl.program_id(axis) == size - 1)
        plan = _hosted_plan(hosted, dict(zip(hosted.names, bufs)), send_sems, recv_sems)
        _hosted_start(plan, first)
        body(*ins, *outs, *scratch)
        _hosted_finish(plan, last)

    def run(*args):
        bufs = [hosted.pool[n] for n in hosted.names]
        sem = pltpu.SemaphoreType.DMA
        res = pl.pallas_call(
            carrying, name=name, grid=grid, in_specs=list(in_specs) + [_ANY] * n_buf,
            out_specs=out_specs_l + [_ANY] * n_buf,
            out_shape=out_shape_l + [SDS(b.shape, b.dtype) for b in bufs],
            scratch_shapes=list(scratch_shapes) + [sem((n_job, 3)), sem((n_job, 3))],
            input_output_aliases={**aliases, **{n_in + i: n_out + i for i in range(n_buf)}},
            compiler_params=pltpu.CompilerParams(dimension_semantics=semantics, vmem_limit_bytes=VMEM_LIMIT,
                                                 has_side_effects=True),
        )(*args, *bufs)
        hosted.pool.update(zip(hosted.names, res[n_out:]))
        return res[0] if single else res[:n_out]

    return run


def _dot(a, b):
    return jnp.dot(a, b, preferred_element_type=F32)


def _dot_nt(a, b):
    return lax.dot_general(a, b, NT, preferred_element_type=F32)


def _dot_tn(a, b):
    return lax.dot_general(a, b, TN, preferred_element_type=F32)


def norm_matmul(x, gain, w, layer, kind, name, hosted=None):
    T, D = x.shape
    if kind == "by_shard":
        tn = w.shape[3]
        N = N_CHIPS * tn
        w_spec = pl.BlockSpec((None, None, D, tn), lambda i, j: (j, layer, 0, 0))
        mm = _dot
    else:
        N = w.shape[0]
        tn = _tile(N, (1024, 512, 256, 128))
        w_spec = pl.BlockSpec((tn, D), lambda i, j: (j, 0))
        mm = _dot_nt
    tm = _tile(T, (2048, 1024, 512, 256, 128) if kind == "rows" else (1024, 512, 256, 128))

    def body(x_ref, g_ref, w_ref, y_ref, h_ref):
        @pl.when(pl.program_id(1) == 0)
        def _():
            xf = x_ref[...]
            r = lax.rsqrt(jnp.mean(xf * xf, axis=-1, keepdims=True) + RMS_EPS)
            h_ref[...] = ((xf * r) * g_ref[...]).astype(BF16)

        y_ref[...] = mm(h_ref[...], w_ref[...]).astype(BF16)

    return _pcall(
        body, hosted, name=name, grid=(T // tm, N // tn),
        in_specs=[pl.BlockSpec((tm, D), lambda i, j: (i, 0)),
                  pl.BlockSpec((None, 1, D), lambda i, j: (layer, 0, 0)),
                  w_spec],
        out_specs=[pl.BlockSpec((tm, tn), lambda i, j: (i, j)),
                   pl.BlockSpec((tm, D), lambda i, j: (i, 0))],
        out_shape=[SDS((T, N), BF16), SDS((T, D), BF16)],
        semantics=("arbitrary", "arbitrary"),
    )(x, gain, w)


def matmul_nt_normbwd(dys, w, layer, kind, x, gain, dres, name, hosted=None):
    T, D = x.shape
    width = dys[0].shape[1]
    if kind == "by_shard":
        tk = w.shape[3]
        w_spec = pl.BlockSpec((None, None, D, tk), lambda i, k: (k, layer, 0, 0))
        mm = _dot_nt
    else:
        tk = _tile(width, (3584, 1024, 512, 256, 128))
        w_spec = pl.BlockSpec((tk, D), lambda i, k: (k, 0))
        mm = _dot
    per = width // tk
    nk = per * len(dys)
    tm = _tile(T, (512, 256, 128))
    n_dy = len(dys)

    def dy_spec(p):
        return pl.BlockSpec((tm, tk), lambda i, k: (i, jnp.clip(k - p * per, 0, per - 1)))

    def body(*refs):
        dy_refs = refs[:n_dy]
        w_ref, x_ref, g_ref, dres_ref, dx_ref, dxb_ref, dg_ref, acc_ref = refs[n_dy:]
        i, k = pl.program_id(0), pl.program_id(1)

        @pl.when(k == 0)
        def _():
            acc_ref[...] = jnp.zeros_like(acc_ref)

        for p in range(n_dy):
            @pl.when((k >= p * per) & (k < (p + 1) * per))
            def _(p=p):
                acc_ref[...] += mm(dy_refs[p][...], w_ref[...])

        @pl.when(k == nk - 1)
        def _():
            xf = x_ref[...]
            r = lax.rsqrt(jnp.mean(xf * xf, axis=-1, keepdims=True) + RMS_EPS)
            xhat = xf * r
            dh = acc_ref[...]
            dhg = dh * g_ref[...]
            dx = dres_ref[...] + r * (dhg - xhat * jnp.mean(dhg * xhat, axis=-1, keepdims=True))
            dx_ref[...] = dx
            dxb_ref[...] = dx.astype(BF16)
            part = jnp.sum(dh * xhat, axis=0, keepdims=True)

            @pl.when(i == 0)
            def _():
                dg_ref[...] = part

            @pl.when(i > 0)
            def _():
                dg_ref[...] += part

    row = pl.BlockSpec((tm, D), lambda i, k: (i, 0))
    return _pcall(
        body, hosted, name=name, grid=(T // tm, nk),
        in_specs=[dy_spec(p) for p in range(n_dy)] + [
            w_spec, row, pl.BlockSpec((None, 1, D), lambda i, k: (layer, 0, 0)), row],
        out_specs=[row, row, pl.BlockSpec((1, D), lambda i, k: (0, 0))],
        out_shape=[SDS((T, D), F32), SDS((T, D), BF16), SDS((1, D), F32)],
        scratch_shapes=[pltpu.VMEM((tm, D), F32)],
        semantics=("arbitrary", "arbitrary"),
    )(*dys, w, x, gain, dres)


def matmul_tn(a, bs, name, b_col0=0, n_cols=None, by_dest=False, tn=None, tk=None, hosted=None):
    T, M = a.shape
    width = bs[0].shape[1]
    N = n_cols if n_cols else width * len(bs)
    tm = _tile(M, (1408, 1024, 512, 256, 128))
    tn = tn or _tile(N, (512, 256, 128))
    tk = tk or _tile(T, (4096, 2048, 1024, 512, 256))
    assert b_col0 % tn == 0 and width % tn == 0
    j0, per, nk, n_b = b_col0 // tn, width // tn, T // tk, len(bs)

    def b_spec(p):
        return pl.BlockSpec((tk, tn), lambda i, j, k: (k, jnp.clip(j0 + j - p * per, 0, per - 1)))

    def body(*refs):
        a_ref, b_refs = refs[0], refs[1:1 + n_b]
        o_ref, acc_ref = refs[-2], refs[-1]
        j, k = pl.program_id(1), pl.program_id(2)

        @pl.when(k == 0)
        def _():
            acc_ref[...] = jnp.zeros_like(acc_ref)

        for p in range(n_b):
            @pl.when((j0 + j >= p * per) & (j0 + j < (p + 1) * per))
            def _(p=p):
                acc_ref[...] += _dot_tn(a_ref[...], b_refs[p][...])

        @pl.when(k == nk - 1)
        def _():
            o_ref[...] = acc_ref[...].astype(BF16)

    if by_dest:
        cs = N // N_CHIPS
        npd = cs // tn
        out_shape = SDS((N_CHIPS, M, cs), BF16)
        out_spec = pl.BlockSpec((None, tm, tn), lambda i, j, k: (j // npd, i, j % npd))
    else:
        out_shape = SDS((M, N), BF16)
        out_spec = pl.BlockSpec((tm, tn), lambda i, j, k: (i, j))
    return _pcall(
        body, hosted, name=name, grid=(M // tm, N // tn, nk),
        in_specs=[pl.BlockSpec((tk, tm), lambda i, j, k: (k, i))] + [b_spec(p) for p in range(n_b)],
        out_specs=out_spec, out_shape=out_shape,
        scratch_shapes=[pltpu.VMEM((tm, tn), F32)],
        semantics=("arbitrary", "arbitrary", "arbitrary"),
    )(a, *bs)


def ffn_down_fwd(ab, w_down, layer, x1, hosted=None):
    T, D = x1.shape
    F = w_down.shape[1]
    tm = _tile(T, (512, 256, 128))
    tk = F // 2
    nk = F // tk

    def body(a_ref, b_ref, w_ref, x_ref, x2_ref, s_ref, acc_ref):
        k = pl.program_id(1)

        @pl.when(k == 0)
        def _():
            acc_ref[...] = x_ref[...]

        a = a_ref[...].astype(F32)
        s = (a * _sigmoid(a) * b_ref[...].astype(F32)).astype(BF16)
        s_ref[...] = s
        acc_ref[...] += _dot(s, w_ref[...])

        @pl.when(k == nk - 1)
        def _():
            x2_ref[...] = acc_ref[...]

    return _pcall(
        body, hosted, name="ffn_down_fwd", grid=(T // tm, nk),
        in_specs=[pl.BlockSpec((tm, tk), lambda i, k: (i, k)),
                  pl.BlockSpec((tm, tk), lambda i, k: (i, nk + k)),
                  pl.BlockSpec((None, tk, D), lambda i, k: (layer, k, 0)),
                  pl.BlockSpec((tm, D), lambda i, k: (i, 0))],
        out_specs=[pl.BlockSpec((tm, D), lambda i, k: (i, 0)),
                   pl.BlockSpec((tm, tk), lambda i, k: (i, k))],
        out_shape=[SDS((T, D), F32), SDS((T, F), BF16)],
        scratch_shapes=[pltpu.VMEM((tm, D), F32)],
        semantics=("arbitrary", "arbitrary"),
    )(ab, ab, w_down, x1)


def ffn_down_bwd(dx2b, w_down, layer, ab, hosted=None):
    T, D = dx2b.shape
    F = w_down.shape[1]
    tm = _tile(T, (512, 256, 128))
    tn = F // 2
    nj = F // tn

    def body(dx_ref, w_ref, a_ref, b_ref, da_ref, db_ref):
        ds = _dot_nt(dx_ref[...], w_ref[...])
        a = a_ref[...].astype(F32)
        sg = _sigmoid(a)
        da_ref[...] = (ds * b_ref[...].astype(F32) * (sg * (1.0 + a * (1.0 - sg)))).astype(BF16)
        db_ref[...] = (ds * (a * sg)).astype(BF16)

    blk = pl.BlockSpec((tm, tn), lambda j, i: (i, j))
    return _pcall(
        body, hosted, name="ffn_down_bwd", grid=(nj, T // tm),
        in_specs=[pl.BlockSpec((tm, D), lambda j, i: (i, 0)),
                  pl.BlockSpec((None, tn, D), lambda j, i: (layer, j, 0)),
                  blk, pl.BlockSpec((tm, tn), lambda j, i: (i, nj + j))],
        out_specs=[blk, blk],
        out_shape=[SDS((T, F), BF16), SDS((T, F), BF16)],
        semantics=("arbitrary", "arbitrary"),
    )(dx2b, w_down, ab, ab)


def _mix_specs(tm, D, layer):
    cs = D // N_CHIPS
    row = lambda w: pl.BlockSpec((tm, w), lambda i: (i, 0))
    wp = pl.BlockSpec((N_CHIPS, None, BRANCH_W, cs), lambda i: (0, layer, 0, 0))
    wo = pl.BlockSpec((None, N_CHIPS, cs, D), lambda i: (layer, 0, 0, 0))
    bg = pl.BlockSpec((None, 1, 3 * D), lambda i: (layer, 0, 0))
    return row, wp, wo, bg


def mix_fwd(ao, po, co, proj, b_gate, wpa, wpp, wpc, w_out, layer, x, hosted=None):
    T, D = x.shape
    cs = D // N_CHIPS
    tm = _tile(T, (256, 128))
    row, wp, wo, bg = _mix_specs(tm, D, layer)

    def body(ao_ref, po_ref, co_ref, g_ref, bg_ref, wpa_ref, wpp_ref, wpc_ref, wo_ref, x_ref,
             x1_ref, ys_ref, mixed_ref):
        mixed = jnp.zeros((tm, D), F32)
        for n, (br, wp_ref) in enumerate(((ao_ref, wpa_ref), (po_ref, wpp_ref), (co_ref, wpc_ref))):
            y = jnp.concatenate([_dot(br[...], wp_ref[j]) for j in range(N_CHIPS)], axis=1)
            cols = slice(n * D, (n + 1) * D)
            gate = _sigmoid(g_ref[:, cols].astype(F32) + bg_ref[:, cols])
            ys_ref[:, cols] = y.astype(BF16)
            mixed = mixed + gate * y
        mb = mixed.astype(BF16)
        mixed_ref[...] = mb
        acc = x_ref[...]
        for j in range(N_CHIPS):
            acc = acc + _dot(mb[:, j * cs:(j + 1) * cs], wo_ref[j])
        x1_ref[...] = acc

    return _pcall(
        body, hosted, name="mix_fwd", grid=(T // tm,),
        in_specs=[row(BRANCH_W), row(BRANCH_W), row(BRANCH_W), row(3 * D), bg, wp, wp, wp, wo, row(D)],
        out_specs=[row(D), row(3 * D), row(D)],
        out_shape=[SDS((T, D), F32), SDS((T, 3 * D), BF16), SDS((T, D), BF16)],
        semantics=("arbitrary",),
    )(ao, po, co, proj, b_gate, wpa, wpp, wpc, w_out, x)


def mix_bwd(dx1b, w_out, proj, b_gate, ys, wpa, wpp, wpc, layer, width, hosted=None):
    T, D = dx1b.shape
    cs = D // N_CHIPS
    tm = _tile(T, (256, 128))
    row, wp, wo, bg = _mix_specs(tm, D, layer)

    def body(dx_ref, wo_ref, g_ref, bg_ref, ys_ref, wpa_ref, wpp_ref, wpc_ref,
             dys_ref, dg_ref, dao_ref, dpo_ref, dco_ref, dbg_ref):
        i = pl.program_id(0)
        dx = dx_ref[...]
        dmixed = jnp.concatenate([_dot_nt(dx, wo_ref[j]) for j in range(N_CHIPS)], axis=1)
        for n, (wp_ref, dbr) in enumerate(((wpa_ref, dao_ref), (wpp_ref, dpo_ref), (wpc_ref, dco_ref))):
            cols = slice(n * D, (n + 1) * D)
            gate = _sigmoid(g_ref[:, cols].astype(F32) + bg_ref[:, cols])
            dy = (dmixed * gate).astype(BF16)
            dys_ref[:, cols] = dy
            dgp = dmixed * ys_ref[:, cols].astype(F32) * gate * (1.0 - gate)
            dg_ref[:, cols] = dgp.astype(BF16)
            part = jnp.sum(dgp, axis=0, keepdims=True)

            @pl.when(i == 0)
            def _():
                dbg_ref[:, cols] = part

            @pl.when(i > 0)
            def _():
                dbg_ref[:, cols] += part

            acc = jnp.zeros((tm, BRANCH_W), F32)
            for j in range(N_CHIPS):
                acc = acc + _dot_nt(dy[:, j * cs:(j + 1) * cs], wp_ref[j])
            dbr[...] = acc.astype(BF16)

    return _pcall(
        body, hosted, name="mix_bwd", grid=(T // tm,),
        in_specs=[row(D), wo, row(3 * D), bg, row(3 * D), wp, wp, wp],
        out_specs=[row(3 * D), row(3 * D), row(BRANCH_W), row(BRANCH_W), row(BRANCH_W),
                   pl.BlockSpec((1, 3 * D), lambda i: (0, 0))],
        out_shape=[SDS((T, 3 * D), BF16), SDS((T, width), BF16), SDS((T, BRANCH_W), BF16),
                   SDS((T, BRANCH_W), BF16), SDS((T, BRANCH_W), BF16), SDS((1, 3 * D), F32)],
        semantics=("arbitrary",),
    )(dx1b, w_out, proj, b_gate, ys, wpa, wpp, wpc)


def loss_head(x2, gain, target):
    T, D = x2.shape
    tm = _tile(T, (512, 256, 128))

    def body(x_ref, g_ref, t_ref, loss_ref, dx_ref, dxb_ref, dg_ref):
        i = pl.program_id(0)
        xf = x_ref[...]
        g = g_ref[...]
        r = lax.rsqrt(jnp.mean(xf * xf, axis=-1, keepdims=True) + RMS_EPS)
        xhat = xf * r
        diff = xhat * g - t_ref[...]
        part_loss = 0.5 * jnp.sum(jnp.mean(diff * diff, axis=-1, keepdims=True), axis=0, keepdims=True)
        dy = diff * (1.0 / D)
        dhg = dy * g
        dx = r * (dhg - xhat * jnp.mean(dhg * xhat, axis=-1, keepdims=True))
        dx_ref[...] = dx
        dxb_ref[...] = dx.astype(BF16)
        part_g = jnp.sum(dy * xhat, axis=0, keepdims=True)
        part_l = jnp.broadcast_to(part_loss, (1, LANES))

        @pl.when(i == 0)
        def _():
            dg_ref[...] = part_g
            loss_ref[...] = part_l

        @pl.when(i > 0)
        def _():
            dg_ref[...] += part_g
            loss_ref[...] += part_l

    row = pl.BlockSpec((tm, D), lambda i: (i, 0))
    return pl.pallas_call(
        body, name="loss_head", grid=(T // tm,),
        in_specs=[row, pl.BlockSpec((1, D), lambda i: (0, 0)), row],
        out_specs=[pl.BlockSpec((1, LANES), lambda i: (0, 0)), row, row, pl.BlockSpec((1, D), lambda i: (0, 0))],
        out_shape=[SDS((1, LANES), F32), SDS((T, D), F32), SDS((T, D), BF16), SDS((1, D), F32)],
        compiler_params=_params("arbitrary"),
    )(x2, gain, target)


def _placement_constants():
    w = HEADS * HEAD_PAD
    pq = np.zeros((BRANCH_W, w), np.float32)
    pk = np.zeros((BRANCH_W, w), np.float32)
    pfq = np.zeros((3, LANES, w), np.float32)
    pfk = np.zeros((3, LANES, w), np.float32)
    cq = np.zeros((1, w), np.float32)
    ck = np.zeros((1, w), np.float32)
    eq = np.zeros((w, LANES), np.float32)
    ek = np.zeros((w, LANES), np.float32)
    for h in range(HEADS):
        for d in range(HEAD_DIM):
            pq[h * HEAD_DIM + d, h * HEAD_PAD + d] = HEAD_DIM ** -0.5
            pk[h * HEAD_DIM + d, h * HEAD_PAD + d] = 1.0
        for i in range(3):
            pfq[i, h, h * HEAD_PAD + HEAD_DIM + i] = 1.0
            pfk[i, h, h * HEAD_PAD + HEAD_DIM + 3 + i] = -1.0
            cq[0, h * HEAD_PAD + HEAD_DIM + 3 + i] = 1.0
            ck[0, h * HEAD_PAD + HEAD_DIM + i] = 1.0
        eq[h * HEAD_PAD + HEAD_DIM, h] = 1.0
        ek[h * HEAD_PAD + HEAD_DIM + 3, h] = -1.0
    bf = lambda a: jnp.asarray(a, BF16)
    return dict(pq=bf(pq), pk=bf(pk), pfq=bf(pfq), pfk=bf(pfk), cq=jnp.asarray(cq), ck=jnp.asarray(ck),
                pqkt=bf(np.concatenate([pq.T, pk.T], axis=0)), eq=bf(eq), ek=bf(ek))


def attn_prep(proj3, bf_rows, layer, cst, lay, hosted=None):
    Bl, S, _ = proj3.shape
    ts = ATTN_BLOCK
    w = HEADS * HEAD_PAD

    def body(q_ref, k_ref, f_ref, bf_ref, pq_ref, pk_ref, pfq_ref, pfk_ref, cq_ref, ck_ref,
             qa_ref, ka_ref, carry_ref):
        @pl.when(pl.program_id(1) == 0)
        def _():
            carry_ref[...] = jnp.zeros_like(carry_ref)

        z = f_ref[...].astype(F32) + bf_ref[...]
        logf = jnp.minimum(z, 0.0) - jnp.log(1.0 + jnp.exp(-jnp.abs(z)))
        r = lax.broadcasted_iota(jnp.int32, (ts, ts), 0)
        c = lax.broadcasted_iota(jnp.int32, (ts, ts), 1)
        tri = jnp.where(r >= c, 1.0, 0.0).astype(BF16)
        fcum = carry_ref[...]
        for part in _split3(logf):
            fcum = fcum + _dot(tri, part)
        carry_ref[...] = fcum[ts - 1:ts, :]
        qa = _dot(q_ref[...], pq_ref[...]) + cq_ref[...]
        ka = _dot(k_ref[...], pk_ref[...]) + ck_ref[...]
        for i, part in enumerate(_split3(fcum)):
            qa = qa + _dot(part, pfq_ref[i])
            ka = ka + _dot(part, pfk_ref[i])
        qa_ref[...] = qa.astype(BF16)
        ka_ref[...] = ka.astype(BF16)

    cfull = lambda shape: pl.BlockSpec(shape, lambda b, s: (0,) * len(shape))
    return _pcall(
        body, hosted, name="attn_prep", grid=(Bl, S // ts),
        in_specs=[pl.BlockSpec((None, ts, BRANCH_W), lambda b, s: (b, s, lay["q"] // BRANCH_W)),
                  pl.BlockSpec((None, ts, BRANCH_W), lambda b, s: (b, s, lay["k"] // BRANCH_W)),
                  pl.BlockSpec((None, ts, LANES), lambda b, s: (b, s, lay["f"] // LANES)),
                  pl.BlockSpec((None, 1, LANES), lambda b, s: (layer, 0, 0)),
                  cfull((BRANCH_W, w)), cfull((BRANCH_W, w)),
                  cfull((3, LANES, w)), cfull((3, LANES, w)), cfull((1, w)), cfull((1, w))],
        out_specs=[pl.BlockSpec((None, ts, w), lambda b, s: (b, s, 0)),
                   pl.BlockSpec((None, ts, w), lambda b, s: (b, s, 0))],
        out_shape=[SDS((Bl, S, w), BF16), SDS((Bl, S, w), BF16)],
        scratch_shapes=[pltpu.VMEM((1, LANES), F32)],
        semantics=("arbitrary", "arbitrary"),
    )(proj3, proj3, proj3, bf_rows, cst["pq"], cst["pk"], cst["pfq"], cst["pfk"], cst["cq"], cst["ck"])


def attn_fwd(qa, ka, proj3, lay, hosted=None):
    Bl, S, _ = qa.shape
    tq = ATTN_BLOCK
    nq = S // tq
    pairs = HEADS // 2
    pw = 2 * HEAD_PAD
    vw = 2 * HEAD_DIM

    def body(qa_ref, ka_ref, v_ref, o_ref, lse_ref):
        row = lax.broadcasted_iota(jnp.int32, (tq, tq), 0)
        col = lax.broadcasted_iota(jnp.int32, (tq, tq), 1)
        causal = row <= col
        for i in range(nq):
            nk = (i + 1) * tq
            rows = slice(i * tq, nk)
            o_t = []
            for h in range(2):
                hs = slice(h * HEAD_PAD, (h + 1) * HEAD_PAD)
                st = _dot_nt(ka_ref[0:nk, hs], qa_ref[rows, hs])
                diag = jnp.where(causal, st[nk - tq:], NEG_INF)
                m = jnp.max(diag, axis=0, keepdims=True)
                if i:
                    m = jnp.maximum(m, jnp.max(st[:nk - tq], axis=0, keepdims=True))
                p_diag = jnp.exp(diag - m)
                l = jnp.sum(p_diag, axis=0, keepdims=True)
                if i:
                    p_top = jnp.exp(st[:nk - tq] - m)
                    l = l + jnp.sum(p_top, axis=0, keepdims=True)
                    p = jnp.concatenate([p_top.astype(BF16), p_diag.astype(BF16)], axis=0)
                else:
                    p = p_diag.astype(BF16)
                acc = _dot_tn(v_ref[0:nk, :], p)
                o_t.append(acc[h * HEAD_DIM:(h + 1) * HEAD_DIM, :] / l)
                lse_ref[h:h + 1, rows] = m + jnp.log(l)
            o_ref[rows, :] = jnp.concatenate(o_t, axis=0).T.astype(BF16)

    return _pcall(
        body, hosted, name="attn_fwd", grid=(Bl, pairs),
        in_specs=[pl.BlockSpec((None, S, pw), lambda b, p: (b, 0, p)),
                  pl.BlockSpec((None, S, pw), lambda b, p: (b, 0, p)),
                  pl.BlockSpec((None, S, vw), lambda b, p: (b, 0, lay["v"] // vw + p))],
        out_specs=[pl.BlockSpec((None, S, vw), lambda b, p: (b, 0, p)),
                   pl.BlockSpec((None, None, 2, S), lambda b, p: (b, p, 0, 0))],
        out_shape=[SDS((Bl, S, BRANCH_W), BF16), SDS((Bl, pairs, 2, S), F32)],
        semantics=("arbitrary", "arbitrary"),
    )(qa, ka, proj3)


def attn_bwd(qa, ka, proj3, dao, ao, lse, dproj3, lay, hosted=None):
    Bl, S, _ = qa.shape
    tk = ATTN_BLOCK
    nq = S // tk
    pairs = HEADS // 2
    pw = 2 * HEAD_PAD
    vw = 2 * HEAD_DIM

    def body(qa_ref, ka_ref, v_ref, do_ref, o_ref, lse_ref, _, dqa_ref, dka_ref, dv_ref):
        row = lax.broadcasted_iota(jnp.int32, (tk, tk), 0)
        col = lax.broadcasted_iota(jnp.int32, (tk, tk), 1)
        causal = row <= col
        lane8 = lax.broadcasted_iota(jnp.int32, (8, vw), 1)
        lane_s = lax.broadcasted_iota(jnp.int32, (S, vw), 1)
        lane_k = lax.broadcasted_iota(jnp.int32, (tk, vw), 1)
        doo = do_ref[...].astype(F32) * o_ref[...].astype(F32)
        hi = doo.astype(BF16)
        lo = (doo - hi.astype(F32)).astype(BF16)
        delta, v_head = [], []
        for h in range(2):
            sel = jnp.where((lane8 >= h * HEAD_DIM) & (lane8 < (h + 1) * HEAD_DIM), 1.0, 0.0).astype(BF16)
            delta.append((_dot_nt(sel, hi) + _dot_nt(sel, lo))[0:1, :])
            in_head = (lane_s >= h * HEAD_DIM) & (lane_s < (h + 1) * HEAD_DIM)
            v_head.append(jnp.where(in_head, v_ref[...], jnp.zeros_like(v_ref[...])))
        dqa_ref[...] = jnp.zeros_like(dqa_ref)
        for j in range(nq):
            q0 = j * tk
            krows = slice(q0, q0 + tk)
            do = do_ref[q0:, :]
            dvs = []
            for h in range(2):
                hs = slice(h * HEAD_PAD, (h + 1) * HEAD_PAD)
                k = ka_ref[krows, hs]
                q = qa_ref[q0:, hs]
                st = _dot_nt(k, q)
                p = jnp.exp(st - lse_ref[h:h + 1, q0:])
                p_diag = jnp.where(causal, p[:, :tk], 0.0)
                p = jnp.concatenate([p_diag, p[:, tk:]], axis=1) if j < nq - 1 else p_diag
                dvs.append(_dot(p.astype(BF16), do))
                dpt = _dot_nt(v_head[h][krows, :], do)
                ds = (p * (dpt - delta[h][:, q0:])).astype(BF16)
                dka_ref[krows, hs] = _dot(ds, q)
                dqa_ref[q0:, hs] += _dot_tn(ds, k)
            dv_ref[krows, :] = jnp.where(lane_k < HEAD_DIM, dvs[0], dvs[1]).astype(BF16)

    seq = lambda w, c0=0: pl.BlockSpec((None, S, w), lambda b, p: (b, 0, c0 + p))
    return _pcall(
        body, hosted, name="attn_bwd", grid=(Bl, pairs),
        in_specs=[seq(pw), seq(pw), seq(vw, lay["v"] // vw), seq(vw), seq(vw),
                  pl.BlockSpec((None, None, 2, S), lambda b, p: (b, p, 0, 0)), _ANY],
        out_specs=[seq(pw), seq(pw), seq(vw, lay["v"] // vw)],
        out_shape=[SDS((Bl, S, HEADS * HEAD_PAD), F32), SDS((Bl, S, HEADS * HEAD_PAD), F32),
                   SDS(dproj3.shape, BF16)],
        aliases={6: 2}, semantics=("arbitrary", "arbitrary"),
    )(qa, ka, proj3, dao, ao, lse, dproj3)


def attn_post(dqa, dka, proj3, bf_rows, layer, dproj3, cst, lay, hosted=None):
    Bl, S, w = dqa.shape
    ts = ATTN_BLOCK
    ns = S // ts
    qkf = 2 * BRANCH_W + F_PAD

    def body(dqa_ref, dka_ref, f_ref, bf_ref, pqkt_ref, eq_ref, ek_ref, _, dqkf_ref, dbf_ref, carry_ref):
        b, s = pl.program_id(0), pl.program_id(1)

        @pl.when(s == 0)
        def _():
            carry_ref[...] = jnp.zeros_like(carry_ref)

        dqa_v, dka_v = dqa_ref[...], dka_ref[...]
        qh = dqa_v.astype(BF16)
        kh = dka_v.astype(BF16)
        dqkf_ref[:, :BRANCH_W] = _dot(qh, pqkt_ref[:w, :]).astype(BF16)
        dqkf_ref[:, BRANCH_W:2 * BRANCH_W] = _dot(kh, pqkt_ref[w:, :]).astype(BF16)
        ql = (dqa_v - qh.astype(F32)).astype(BF16)
        kl = (dka_v - kh.astype(F32)).astype(BF16)
        d_f = (_dot(qh, eq_ref[...]) + _dot(ql, eq_ref[...])) + (_dot(kh, ek_ref[...]) + _dot(kl, ek_ref[...]))
        r = lax.broadcasted_iota(jnp.int32, (ts, ts), 0)
        c = lax.broadcasted_iota(jnp.int32, (ts, ts), 1)
        triu = jnp.where(c >= r, 1.0, 0.0).astype(BF16)
        rev = carry_ref[...]
        for part in _split3(d_f):
            rev = rev + _dot(triu, part)
        carry_ref[...] = rev[0:1, :]
        z = f_ref[...].astype(F32) + bf_ref[...]
        lane = lax.broadcasted_iota(jnp.int32, (ts, LANES), 1)
        dfl = jnp.where(lane < HEADS, rev / (1.0 + jnp.exp(z)), 0.0)
        dqkf_ref[:, 2 * BRANCH_W:] = jnp.concatenate(
            [dfl.astype(BF16), jnp.zeros((ts, F_PAD - LANES), BF16)], axis=1)
        part = jnp.sum(dfl, axis=0, keepdims=True)

        @pl.when((b == 0) & (s == 0))
        def _():
            dbf_ref[...] = part

        @pl.when((b > 0) | (s > 0))
        def _():
            dbf_ref[...] += part

    assert lay["q"] % qkf == 0
    cfull = lambda shape: pl.BlockSpec(shape, lambda b, s: (0,) * len(shape))
    rev_blk = lambda wd, c0=0: pl.BlockSpec((None, ts, wd), lambda b, s: (b, ns - 1 - s, c0))
    return _pcall(
        body, hosted, name="attn_post", grid=(Bl, ns),
        in_specs=[rev_blk(w), rev_blk(w), rev_blk(LANES, lay["f"] // LANES),
                  pl.BlockSpec((None, 1, LANES), lambda b, s: (layer, 0, 0)),
                  cfull((2 * w, BRANCH_W)), cfull((w, LANES)), cfull((w, LANES)), _ANY],
        out_specs=[rev_blk(qkf, lay["q"] // qkf), cfull((1, LANES))],
        out_shape=[SDS(dproj3.shape, BF16), SDS((1, LANES), F32)],
        scratch_shapes=[pltpu.VMEM((1, LANES), F32)],
        aliases={7: 0}, semantics=("arbitrary", "arbitrary"),
    )(dqa, dka, proj3, bf_rows, cst["pqkt"], cst["eq"], cst["ek"], dproj3)


def _shift_down(x, k, row):
    return jnp.where(row >= k, pltpu.roll(x, k, axis=0), 0.0)


def _shift_up(x, k, row):
    n = x.shape[0]
    return jnp.where(row < n - k, pltpu.roll(x, n - k, axis=0), 0.0)


def _window_sum(x, g, row, shift):
    s2 = x + shift(x, 1, row)
    s4 = s2 + shift(s2, 2, row)
    s8 = s4 + shift(s4, 4, row)
    s16 = s8 + shift(s8, 8, row)
    return jnp.where(g == 0, s2, jnp.where(g == 1, s4, jnp.where(g == 2, s8, s16)))


def _window_count(g, row):
    wnd = jnp.where(g == 0, 2, jnp.where(g == 1, 4, jnp.where(g == 2, 8, 16)))
    return jnp.minimum(row + 1, wnd).astype(F32)


def _group_columns(ref):
    return [ref[:, n * GROUP_W:(n + 1) * GROUP_W].astype(F32) for n in range(4)]


def poolconv_fwd(proj3, pool_w, pool_scale, conv_w, layer, lay, hosted=None):
    Bl, S, _ = proj3.shape

    def body(x_ref, pw_ref, ps_ref, cw_ref, po_ref, co_ref):
        g = pl.program_id(1)
        row = lax.broadcasted_iota(jnp.int32, (S, GROUP_W), 0)
        u, cv, cb, cc = _group_columns(x_ref)
        d = _window_sum(u, g, row, _shift_down) / _window_count(g, row) - u
        po_ref[...] = (_dot(d.astype(BF16), pw_ref[...]) * ps_ref[...]).astype(BF16)
        z = cc * cv
        y = cw_ref[0:1, :] * _shift_down(z, 2, row) + cw_ref[1:2, :] * _shift_down(z, 1, row) + cw_ref[2:3, :] * z
        co_ref[...] = (cb * y).astype(BF16)

    out = pl.BlockSpec((None, S, GROUP_W), lambda b, g: (b, 0, g))
    return _pcall(
        body, hosted, name="poolconv_fwd", grid=(Bl, N_GROUPS),
        in_specs=[pl.BlockSpec((None, S, BRANCH_W), lambda b, g: (b, 0, lay["pc"] // BRANCH_W + g)),
                  pl.BlockSpec((None, None, GROUP_W, GROUP_W), lambda b, g: (layer, g, 0, 0)),
                  pl.BlockSpec((None, 1, GROUP_W), lambda b, g: (layer, 0, g)),
                  pl.BlockSpec((None, None, 3, GROUP_W), lambda b, g: (g, layer, 0, 0))],
        out_specs=[out, out],
        out_shape=[SDS((Bl, S, BRANCH_W), BF16), SDS((Bl, S, BRANCH_W), BF16)],
        semantics=("arbitrary", "arbitrary"),
    )(proj3, pool_w, pool_scale, conv_w)


def poolconv_bwd(proj3, dpo, dco, pool_w, pool_scale, conv_w, layer, dproj3, lay, hosted=None):
    Bl, S, _ = proj3.shape

    def body(x_ref, dpo_ref, dco_ref, pw_ref, ps_ref, cw_ref, _, dx_ref, dpw_ref, dps_ref, dcw_ref):
        g, b = pl.program_id(0), pl.program_id(1)
        row = lax.broadcasted_iota(jnp.int32, (S, GROUP_W), 0)
        cnt = _window_count(g, row)
        u, cv, cb, cc = _group_columns(x_ref)
        d = (_window_sum(u, g, row, _shift_down) / cnt - u).astype(BF16)
        pw = pw_ref[...]
        ypre = _dot(d, pw)
        dpo_v = dpo_ref[...].astype(F32)
        dps = jnp.sum(dpo_v * ypre, axis=0, keepdims=True)
        dyp = (dpo_v * ps_ref[...]).astype(BF16)
        dpw = _dot_tn(d, dyp)
        dd = _dot_nt(dyp, pw)
        dx_ref[:, 0:GROUP_W] = (_window_sum(dd / cnt, g, row, _shift_up) - dd).astype(BF16)

        z = cc * cv
        z1, z2 = _shift_down(z, 1, row), _shift_down(z, 2, row)
        w0, w1, w2 = cw_ref[0:1, :], cw_ref[1:2, :], cw_ref[2:3, :]
        y = w0 * z2 + w1 * z1 + w2 * z
        dco_v = dco_ref[...].astype(F32)
        dy = dco_v * cb
        dz = w0 * _shift_up(dy, 2, row) + w1 * _shift_up(dy, 1, row) + w2 * dy
        dx_ref[:, GROUP_W:2 * GROUP_W] = (dz * cc).astype(BF16)
        dx_ref[:, 2 * GROUP_W:3 * GROUP_W] = (dco_v * y).astype(BF16)
        dx_ref[:, 3 * GROUP_W:] = (dz * cv).astype(BF16)
        dcw = jnp.concatenate([jnp.sum(dy * z2, axis=0, keepdims=True),
                               jnp.sum(dy * z1, axis=0, keepdims=True),
                               jnp.sum(dy * z, axis=0, keepdims=True)], axis=0)

        @pl.when(b == 0)
        def _():
            dpw_ref[...] = dpw
            dps_ref[...] = dps
            dcw_ref[...] = dcw

        @pl.when(b > 0)
        def _():
            dpw_ref[...] += dpw
            dps_ref[...] += dps
            dcw_ref[...] += dcw

    blk = pl.BlockSpec((None, S, GROUP_W), lambda g, b: (b, 0, g))
    pc = pl.BlockSpec((None, S, BRANCH_W), lambda g, b: (b, 0, lay["pc"] // BRANCH_W + g))
    return _pcall(
        body, hosted, name="poolconv_bwd", grid=(N_GROUPS, Bl),
        in_specs=[pc, blk, blk,
                  pl.BlockSpec((None, None, GROUP_W, GROUP_W), lambda g, b: (layer, g, 0, 0)),
                  pl.BlockSpec((None, 1, GROUP_W), lambda g, b: (layer, 0, g)),
                  pl.BlockSpec((None, None, 3, GROUP_W), lambda g, b: (g, layer, 0, 0)), _ANY],
        out_specs=[pc, pl.BlockSpec((None, GROUP_W, GROUP_W), lambda g, b: (g, 0, 0)),
                   pl.BlockSpec((1, GROUP_W), lambda g, b: (0, g)),
                   pl.BlockSpec((None, 3, GROUP_W), lambda g, b: (g, 0, 0))],
        out_shape=[SDS(dproj3.shape, BF16), SDS((N_GROUPS, GROUP_W, GROUP_W), F32), SDS((1, BRANCH_W), F32),
                   SDS((N_GROUPS, 3, GROUP_W), F32)],
        aliases={6: 0}, semantics=("arbitrary", "arbitrary"),
    )(proj3, dpo, dco, pool_w, pool_scale, conv_w, dproj3)


def _tile_2d(rows, cols, n_arrays):
    budget = VMEM_LIMIT // 2
    lanes = -(-cols // LANES) * LANES
    if rows % 8 == 0:
        for t in range(min(rows, 2048), 7, -8):
            if rows % t == 0 and 2 * n_arrays * t * lanes * 4 <= budget:
                return t, cols
    for t in (1024, 512, 256, 128):
        if cols % t == 0 and 2 * n_arrays * (rows + 8) * t * 4 <= budget:
            return rows, t
    return rows, cols


def add_pair(kept, layer, where, received, name):
    _, n, _, R, C = kept.shape
    tr, tc = _tile_2d(R, C, 3)

    def body(where_ref, a_ref, b_ref, o_ref):
        o_ref[...] = (a_ref[...].astype(F32) + b_ref[...].astype(F32)).astype(BF16)

    blk = pl.BlockSpec((None, tr, tc), lambda d, i, j, where_ref: (d, i, j))
    grid_spec = pltpu.PrefetchScalarGridSpec(
        num_scalar_prefetch=1, grid=(n, R // tr, C // tc),
        in_specs=[pl.BlockSpec((None, None, None, tr, tc),
                               lambda d, i, j, where_ref: (layer, d, where_ref[0], i, j)), blk],
        out_specs=blk)
    return pl.pallas_call(body, name=name, grid_spec=grid_spec, out_shape=SDS((n, R, C), BF16),
                          compiler_params=_params("arbitrary", "arbitrary", "arbitrary"))(where, kept, received)


def add_chips(arrived, own, layer, where, n_layers, prev, name):
    _, R, C = arrived.shape
    tr, tc = _tile_2d(R, C, 6)

    def body(where_ref, a0, a1, a2, a3, own_ref, *rest):
        o_ref = rest[-1]
        chip = where_ref[1]
        acc = None
        for j, a_ref in enumerate((a0, a1, a2, a3)):
            term = jnp.where(chip == j, own_ref[...], a_ref[...]).astype(F32)
            acc = term if acc is None else acc + term
        o_ref[...] = acc

    def slot(j):
        return pl.BlockSpec((None, tr, tc), lambda i, k, where_ref, j=j: (
            jnp.where(where_ref[1] == j, (j + 1) % N_CHIPS, j), i, k))

    in_specs = [slot(j) for j in range(N_CHIPS)] + [
        pl.BlockSpec((None, tr, tc), lambda i, k, where_ref: (where_ref[1], i, k))]
    args = [where, arrived, arrived, arrived, arrived, own]
    aliases = {}
    if prev is not None:
        in_specs.append(_ANY)
        args.append(prev)
        aliases = {len(args) - 1: 0}
    grid_spec = pltpu.PrefetchScalarGridSpec(
        num_scalar_prefetch=1, grid=(R // tr, C // tc), in_specs=in_specs,
        out_specs=pl.BlockSpec((None, None, tr, tc), lambda i, k, where_ref: (layer, where_ref[0], i, k)))
    return pl.pallas_call(body, name=name, grid_spec=grid_spec, out_shape=SDS((n_layers, 2, R, C), F32),
                          input_output_aliases=aliases,
                          compiler_params=_params("arbitrary", "arbitrary"))(*args)


def adamw(w, g, m, v, name):
    if w.ndim == 2:
        R, C = w.shape
        tr, _ = _tile_2d(R, C, 7)
        grid, blk = (R // tr,), pl.BlockSpec((tr, C), lambda i: (i, 0))
    else:
        N, r, C = w.shape
        tn = max(t for t in range(1, N + 1) if N % t == 0 and t * r * C * 4 <= 1024 * 1024)
        grid, blk = (N // tn,), pl.BlockSpec((tn, r, C), lambda i: (i, 0, 0))

    def body(w_ref, g_ref, m_ref, v_ref, d_ref, nm_ref, nv_ref):
        gv = g_ref[...]
        m_new = ADAM_B1 * m_ref[...] + (1.0 - ADAM_B1) * gv
        v_new = ADAM_B2 * v_ref[...] + (1.0 - ADAM_B2) * (gv * gv)
        m_hat = m_new / (1.0 - ADAM_B1 ** ADAM_STEP)
        v_hat = v_new / (1.0 - ADAM_B2 ** ADAM_STEP)
        d_ref[...] = -ADAM_LR * (m_hat / (jnp.sqrt(v_hat) + ADAM_EPS) + ADAM_WD * w_ref[...])
        nm_ref[...] = m_new
        nv_ref[...] = v_new

    out = SDS(w.shape, F32)
    return pl.pallas_call(body, name=name, grid=grid, in_specs=[blk] * 4, out_specs=[blk] * 3,
                          out_shape=[out, out, out], compiler_params=_params("arbitrary"))(w, g, m, v)


_COMM = pltpu.CompilerParams(has_side_effects=True)


def gather_buffers(shards):
    me_chip = 2 * lax.axis_index("x") + lax.axis_index("y")
    pool = {}
    for name, sh in shards.items():
        L, r, c = sh.shape
        if name in ROW_SHARDED:
            pool[name] = lax.dynamic_update_slice(lax.empty((L, N_CHIPS, r, c), sh.dtype), sh[:, None],
                                                  (0, me_chip, 0, 0))
        else:
            pool[name] = lax.dynamic_update_slice(lax.empty((N_CHIPS, L, r, c), sh.dtype), sh[None],
                                                  (me_chip, 0, 0, 0))
    return pool


def comm_now(pool, stages, name):
    stages = [Hosted(pool, jobs) for jobs in stages]
    names = sorted({m for st in stages for m in st.names})
    n = len(names)

    def body(*refs):
        bufs = dict(zip(names, refs[n:2 * n]))
        sems = refs[2 * n:]
        for i, st in enumerate(stages):
            plan = _hosted_plan(st, bufs, sems[2 * i], sems[2 * i + 1])
            _hosted_start(plan, True)
            _hosted_finish(plan, True)

    sem = pltpu.SemaphoreType.DMA
    scratch = []
    for st in stages:
        scratch += [sem((len(st.jobs), 3)), sem((len(st.jobs), 3))]
    res = pl.pallas_call(
        body, name=name, in_specs=[_ANY] * n, out_specs=[_ANY] * n,
        out_shape=[SDS(pool[m].shape, pool[m].dtype) for m in names],
        scratch_shapes=scratch, input_output_aliases={t: t for t in range(n)},
        compiler_params=_COMM,
    )(*[pool[m] for m in names])
    pool.update(zip(names, res))


def gather_now(pool, units):
    comm_now(pool, [[("ici", name, layer) for name, layer in units],
                    [("fwd", name, layer) for name, layer in units]], "gather_now")


def allgather_chips(buf, name):
    def body(src_ref, out_ref, send_sems, recv_sems, local_sem):
        x, y, c = _position()
        me = 2 * x + y
        mine = pltpu.make_async_copy(src_ref, out_ref.at[me], local_sem)
        mine.start()
        sends = []
        for k, (px, py) in enumerate(_other_chips(x, y)):
            cp = _remote(src_ref, out_ref.at[me], send_sems.at[k], recv_sems.at[k], (px, py, c))
            cp.start()
            sends.append(cp)
        for k, (px, py) in enumerate(_other_chips(x, y)):
            _remote(src_ref, out_ref.at[2 * px + py], send_sems.at[k], recv_sems.at[k], (px, py, c)).wait_recv()
        for cp in sends:
            cp.wait_send()
        mine.wait()

    sem = pltpu.SemaphoreType.DMA
    return pl.pallas_call(
        body, name=name, in_specs=[_ANY], out_specs=_ANY, out_shape=SDS((N_CHIPS,) + buf.shape, buf.dtype),
        scratch_shapes=[sem((3,)), sem((3,)), sem], compiler_params=_COMM,
    )(buf)


BIG = ("w_in", "w_proj_attn", "w_proj_pool", "w_proj_conv", "conv_w", "w_out", "w_gate_up", "w_down")
REPLICATED = ("attn_norm", "b_forget", "b_gate", "pool_w", "pool_scale", "ffn_norm", "final_norm")
ORDER = ("attn_norm", "w_in", "b_forget", "b_gate", "w_proj_attn", "pool_w", "pool_scale", "w_proj_pool",
         "conv_w", "w_proj_conv", "w_out", "ffn_norm", "w_gate_up", "w_down", "final_norm")


def _proj_layout(D):
    lay = {"g": 0, "q": 3 * D}
    lay["k"] = lay["q"] + BRANCH_W
    lay["f"] = lay["k"] + BRANCH_W
    lay["v"] = lay["f"] + F_PAD
    lay["pc"] = lay["v"] + BRANCH_W
    lay["width"] = lay["pc"] + 4 * BRANCH_W
    return lay


_REF = dict(q=0, k=512, v=1024, f=1536, u=1544, cv=2056, cb=2568, cc=3080, g=3592)


def _packed_pieces(D):
    pieces = [(_REF["g"], 3 * D), (_REF["q"], BRANCH_W), (_REF["k"], BRANCH_W), (_REF["f"], HEADS),
              (None, F_PAD - HEADS), (_REF["v"], BRANCH_W)]
    for gi in range(N_GROUPS):
        pieces += [(_REF[name] + gi * GROUP_W, GROUP_W) for name in ("u", "cv", "cb", "cc")]
    return pieces


def _packed_runs(D, cs):
    runs, at = [], 0
    for start, n in _packed_pieces(D):
        if start is None:
            runs.append((at, None, 0, n))
            at += n
        while start is not None and n:
            chip, off = divmod(start, cs)
            take = min(n, cs - off)
            runs.append((at, chip, off, take))
            at, start, n = at + take, start + take, n - take
    return runs


def pack_w_in(shards, layer):
    _, _, cs, D = shards.shape
    runs = _packed_runs(D, cs)
    width = runs[-1][0] + runs[-1][3]
    tc = _tile(D, (256, 128))

    def body(s_ref, o_ref):
        for dst, chip, off, rows in runs:
            if chip is None:
                o_ref[dst:dst + rows, :] = jnp.zeros((rows, tc), s_ref.dtype)
            else:
                o_ref[dst:dst + rows, :] = s_ref[chip, off:off + rows, :]

    return pl.pallas_call(
        body, name="pack_w_in", grid=(D // tc,),
        in_specs=[pl.BlockSpec((N_CHIPS, None, cs, tc), lambda j: (0, layer, 0, j))],
        out_specs=pl.BlockSpec((width, tc), lambda j: (0, j)),
        out_shape=SDS((width, D), shards.dtype), compiler_params=_params("arbitrary"),
    )(shards)


def unpack_w_in(p, cs):
    width, D = p.shape
    half = cs // 2
    runs = []
    for src, chip, off, rows in _packed_runs(D, cs):
        while chip is not None and rows:
            h, at = divmod(off, half)
            take = min(rows, half - at)
            runs.append((src, chip, h, at, take))
            src, off, rows = src + take, off + take, rows - take
    tc = _tile(D, (256, 128))

    def body(p_ref, o_ref):
        for src, chip, h, at, rows in runs:
            o_ref[chip, h, at:at + rows, :] = p_ref[src:src + rows, :]

    return pl.pallas_call(
        body, name="unpack_w_in", grid=(D // tc,),
        in_specs=[pl.BlockSpec((width, tc), lambda j: (0, j))],
        out_specs=pl.BlockSpec((N_CHIPS, 2, half, tc), lambda j: (0, 0, 0, j)),
        out_shape=SDS((N_CHIPS, 2, half, D), p.dtype), compiler_params=_params("arbitrary"),
    )(p)


def _split_flat(vec, shapes):
    out, at = [], 0
    for shp in shapes:
        n = int(np.prod(shp))
        out.append(vec[at:at + n].reshape(shp))
        at += n
    return out


def kernel(x, attn_norm, w_in, b_forget, b_gate, w_proj_attn, pool_w, pool_scale, w_proj_pool, conv_w, w_proj_conv, w_out, ffn_norm, w_gate_up, w_down, final_norm, loss_target, m_attn_norm, m_w_in, m_b_forget, m_b_gate, m_w_proj_attn, m_pool_w, m_pool_scale, m_w_proj_pool, m_conv_w, m_w_proj_conv, m_w_out, m_ffn_norm, m_w_gate_up, m_w_down, m_final_norm, v_attn_norm, v_w_in, v_b_forget, v_b_gate, v_w_proj_attn, v_pool_w, v_pool_scale, v_w_proj_pool, v_conv_w, v_w_proj_conv, v_w_out, v_ffn_norm, v_w_gate_up, v_w_down, v_final_norm):
    weights = dict(attn_norm=attn_norm, w_in=w_in, b_forget=b_forget, b_gate=b_gate, w_proj_attn=w_proj_attn,
                   pool_w=pool_w, pool_scale=pool_scale, w_proj_pool=w_proj_pool, conv_w=conv_w,
                   w_proj_conv=w_proj_conv, w_out=w_out, ffn_norm=ffn_norm, w_gate_up=w_gate_up, w_down=w_down,
                   final_norm=final_norm)
    mom_m = dict(attn_norm=m_attn_norm, w_in=m_w_in, b_forget=m_b_forget, b_gate=m_b_gate, w_proj_attn=m_w_proj_attn,
                 pool_w=m_pool_w, pool_scale=m_pool_scale, w_proj_pool=m_w_proj_pool, conv_w=m_conv_w,
                 w_proj_conv=m_w_proj_conv, w_out=m_w_out, ffn_norm=m_ffn_norm, w_gate_up=m_w_gate_up,
                 w_down=m_w_down, final_norm=m_final_norm)
    mom_v = dict(attn_norm=v_attn_norm, w_in=v_w_in, b_forget=v_b_forget, b_gate=v_b_gate, w_proj_attn=v_w_proj_attn,
                 pool_w=v_pool_w, pool_scale=v_pool_scale, w_proj_pool=v_w_proj_pool, conv_w=v_conv_w,
                 w_proj_conv=v_w_proj_conv, w_out=v_w_out, ffn_norm=v_ffn_norm, w_gate_up=v_w_gate_up,
                 w_down=v_w_down, final_norm=v_final_norm)

    Bl, S, D = x.shape
    T = Bl * S
    L = w_in.shape[0]
    F = w_down.shape[1] * N_CHIPS
    lay = _proj_layout(D)
    cst = _placement_constants()
    assert L == N_LAYERS and S % ATTN_BLOCK == 0 and F % (2 * LANES) == 0 and D % BRANCH_W == 0
    assert w_in.shape[2] * N_CHIPS == _REF["g"] + 3 * D and conv_w.shape[2] == GROUP_W

    send = {n: weights[n].astype(BF16) for n in BIG}
    send["conv_w"] = conv_w
    me_chip = 2 * lax.axis_index("x") + lax.axis_index("y")
    send["w_in"] = w_in.transpose(0, 2, 1).astype(BF16)
    pool = gather_buffers(send)
    gather_now(pool, [("w_in", 0)])
    rest = ("w_out", "w_proj_attn", "w_proj_pool", "w_gate_up", "w_proj_conv", "conv_w")
    late = ("w_out", "w_proj_attn", "w_proj_pool", "w_proj_conv", "conv_w")
    jobs = lambda kind, names, layer: [(kind, n, layer) for n in names]
    carried = {
        ("in_proj", 0): jobs("ici", rest, 0),
        ("attn_prep", 0): jobs("fwd", late, 0),
        ("attn_fwd", 0): jobs("fwd", ("w_gate_up",), 0) + jobs("ici", ("w_in",), 1) + jobs("ici", ("w_down",), 0),
        ("poolconv_fwd", 0): jobs("fwd", ("w_in",), 1) + jobs("fwd", ("w_down",), 0),
        ("mix_fwd", 0): jobs("ici", ("w_down",), 1),
        ("gate_up_proj", 0): jobs("ici", late, 1) + jobs("fwd", ("w_down",), 1),
        ("ffn_down_fwd", 0): jobs("ici", ("w_gate_up",), 1),
        ("in_proj", 1): jobs("fwd", ("w_gate_up",) + late, 1),
    }
    carry = lambda call, layer: Hosted(pool, carried[call, layer]) if (call, layer) in carried else None
    w_down_f = lambda: pool["w_down"].reshape(L, F, D)
    pool_w_b = pool_w.astype(BF16)
    an3, fn3 = attn_norm.reshape(L, 1, D), ffn_norm.reshape(L, 1, D)
    bg3, ps3 = b_gate.reshape(L, 1, 3 * D), pool_scale.reshape(L, 1, BRANCH_W)
    bf3 = jnp.pad(b_forget, ((0, 0), (0, LANES - HEADS))).reshape(L, 1, LANES)

    xs = x.reshape(T, D)
    saved = []
    w_in_p = []
    for l in range(L):
        w_in_p.append(pack_w_in(pool["w_in"], l))
        proj, h = norm_matmul(xs, an3, w_in_p[l], l, "rows", "in_proj", carry("in_proj", l))
        proj3 = proj.reshape(Bl, S, lay["width"])
        qa, ka = attn_prep(proj3, bf3, l, cst, lay, carry("attn_prep", l))
        ao, lse = attn_fwd(qa, ka, proj3, lay, carry("attn_fwd", l))
        po, co = poolconv_fwd(proj3, pool_w_b, ps3, pool["conv_w"], l, lay, carry("poolconv_fwd", l))
        ao2, po2, co2 = (a.reshape(T, BRANCH_W) for a in (ao, po, co))
        x1, ys, mixed = mix_fwd(ao2, po2, co2, proj, bg3, pool["w_proj_attn"], pool["w_proj_pool"],
                                pool["w_proj_conv"], pool["w_out"], l, xs, carry("mix_fwd", l))
        ab, h2 = norm_matmul(x1, fn3, pool["w_gate_up"], l, "by_shard", "gate_up_proj", carry("gate_up_proj", l))
        x2, s_act = ffn_down_fwd(ab, w_down_f(), l, x1, carry("ffn_down_fwd", l))
        saved.append(dict(x=xs, proj=proj, proj3=proj3, h=h, qa=qa, ka=ka, ao=ao, lse=lse, ao2=ao2, po2=po2,
                          co2=co2, ys=ys, mixed=mixed, x1=x1, ab=ab, h2=h2, s=s_act))
        xs = x2
    w_gu, w_o, conv_w_g = pool["w_gate_up"], pool["w_out"], pool["conv_w"]
    wpa, wpp, wpc = pool["w_proj_attn"], pool["w_proj_pool"], pool["w_proj_conv"]
    w_down_f = w_down_f()

    loss_row, dx, dxb, g_final = loss_head(xs, final_norm.reshape(1, D), loss_target.reshape(T, D))
    loss = lax.psum(loss_row[0, 0], AXES)

    reduced_names = tuple(n for n in BIG if n != "conv_w")
    early_names = tuple(n for n in reduced_names if n != "w_in")
    proj_names = ("w_out", "w_proj_attn", "w_proj_pool", "w_proj_conv")
    first_names = ("w_in", "w_gate_up", "w_down")
    where = jnp.stack([lax.axis_index("c"), me_chip]).astype(jnp.int32)
    rs = {}

    def reduce_begin(layer, grads):
        for n, g in grads.items():
            g5 = g.reshape((1, N_CHIPS, 2, -1) + g.shape[-1:])
            rs["g%d:%s" % (layer, n)] = g5
            for role in "ra":
                rs["%s%d:%s" % (role, layer, n)] = lax.empty((N_CHIPS,) + g5.shape[3:], BF16)

    swap_jobs = lambda layer, names: [("swap", "g%d:%s" % (layer, n), "r%d:%s" % (layer, n), 0) for n in names]
    xchg_jobs = lambda layer, names: [("xchg", "s%d:%s" % (layer, n), "a%d:%s" % (layer, n)) for n in names]
    join_jobs = lambda layer, names: [("join", "o:" + n, layer) for n in names]

    def pair_sums(layer, names):
        for n in names:
            rs["s%d:%s" % (layer, n)] = add_pair(rs["g%d:%s" % (layer, n)], 0, where, rs["r%d:%s" % (layer, n)],
                                                 "add_pair_" + n)

    def chip_sums(layer, names, slot, n_slots):
        for n in names:
            rs["o:" + n] = add_chips(rs["a%d:%s" % (layer, n)], rs["s%d:%s" % (layer, n)], slot, where, n_slots,
                                     rs.get("o:" + n), "add_chips_" + n)

    small = {n: [None] * L for n in REPLICATED if n != "final_norm"}
    g_conv = [None] * L
    to3 = lambda a: a.reshape(Bl, S, -1)
    for l in reversed(range(L)):
        sv = saved[l]
        behind = (lambda jobs: Hosted(rs, jobs)) if l == 0 else (lambda jobs: None)
        grads = {}
        da, db = ffn_down_bwd(dxb, w_down_f, l, sv["ab"], behind(swap_jobs(1, reduced_names)))
        if l == 0:
            pair_sums(1, reduced_names)
        grads["w_down"] = matmul_tn(sv["s"], [dxb], "grad_w_down", hosted=behind(xchg_jobs(1, ("w_down",))))
        grads["w_gate_up"] = matmul_tn(sv["h2"], [da, db], "grad_w_gate_up", by_dest=True, tn=2 * F // N_CHIPS,
                                       tk=_tile(T, (1024, 512, 256)), hosted=behind(xchg_jobs(1, ("w_gate_up",))))
        dx1, dx1b, g_fn = matmul_nt_normbwd([da, db], w_gu, l, "by_shard", sv["x1"], fn3, dx, "gate_up_bwd",
                                            behind(xchg_jobs(1, ("w_in",))))
        small["ffn_norm"][l] = g_fn[0]
        if l == 0:
            chip_sums(1, first_names, 1, L)
        dys, dproj, dao, dpo, dco, g_bg = mix_bwd(dx1b, w_o, sv["proj"], bg3, sv["ys"], wpa, wpp, wpc, l,
                                                  lay["width"],
                                                  behind(xchg_jobs(1, proj_names) + join_jobs(1, first_names)))
        if l == 0:
            chip_sums(1, proj_names, 1, L)
        small["b_gate"][l] = g_bg[0]
        grads["w_out"] = matmul_tn(sv["mixed"], [dx1b], "grad_w_out")
        for n, (name, br) in enumerate((("w_proj_attn", sv["ao2"]), ("w_proj_pool", sv["po2"]),
                                        ("w_proj_conv", sv["co2"]))):
            grads[name] = matmul_tn(br, [dys], "grad_" + name, b_col0=n * D, n_cols=D, by_dest=True,
                                    tn=D // N_CHIPS)
        if l == 0:
            reduce_begin(0, grads)
        dqa, dka, dproj3 = attn_bwd(sv["qa"], sv["ka"], sv["proj3"], to3(dao), sv["ao"], sv["lse"], to3(dproj), lay,
                                    behind(swap_jobs(0, early_names) + join_jobs(1, proj_names)))
        if l == 0:
            pair_sums(0, early_names)
        dproj3, g_bf = attn_post(dqa, dka, sv["proj3"], bf3, l, dproj3, cst, lay, behind(xchg_jobs(
            0, ("w_out", "w_proj_attn", "w_proj_pool", "w_proj_conv"))))
        small["b_forget"][l] = g_bf[0, :HEADS]
        dproj3, g_pw, g_ps, g_conv[l] = poolconv_bwd(sv["proj3"], to3(dpo), to3(dco), pool_w_b, ps3, conv_w_g, l,
                                                     dproj3, lay, behind(xchg_jobs(0, ("w_down",))))
        small["pool_w"][l], small["pool_scale"][l] = g_pw, g_ps[0]
        dproj = dproj3.reshape(T, lay["width"])
        g_w_in = unpack_w_in(matmul_tn(dproj, [sv["h"]], "grad_w_in", hosted=behind(xchg_jobs(
            0, ("w_gate_up",)))), w_in.shape[2])
        if l:
            reduce_begin(l, {**grads, "w_in": g_w_in})
        else:
            reduce_begin(0, {"w_in": g_w_in})
            comm_now(rs, [swap_jobs(0, ("w_in",))], "swap_w_in_halves")
            pair_sums(0, ("w_in",))
        dx, dxb, g_an = matmul_nt_normbwd([dproj], w_in_p[l], l, "rows", sv["x"], an3, dx1, "in_proj_bwd",
                                          behind(xchg_jobs(0, ("w_in",))))
        small["attn_norm"][l] = g_an[0]
    grad_x = dx.reshape(Bl, S, D)

    small_shapes = [weights[n].shape for n in REPLICATED] + [(L, N_CHIPS) + conv_w.shape[1:]]
    small_vec = jnp.concatenate([jnp.stack(small[n]).reshape(-1) for n in REPLICATED[:-1]]
                                + [g_final[0], jnp.stack(g_conv).reshape(-1)])
    n_small = small_vec.shape[0]
    small_vec = jnp.pad(small_vec, (0, -n_small % (2 * N_CHIPS * 16 * LANES))).astype(BF16)
    rs["g0:small"] = small_vec.reshape(1, N_CHIPS, 2, -1, LANES)
    for role in "ra":
        rs[role + "0:small"] = lax.empty((N_CHIPS,) + rs["g0:small"].shape[3:], BF16)
    last = ("small",)
    comm_now(rs, [swap_jobs(0, last)], "swap_grad_halves")
    pair_sums(0, last)
    comm_now(rs, [xchg_jobs(0, last)], "exchange_grad_chips")
    chip_sums(0, reduced_names, 0, L)
    chip_sums(0, ("small",), 0, 1)
    comm_now(rs, [join_jobs(0, reduced_names + ("small",))], "join_grad_halves")
    shard_grads = {n: rs["o:" + n].reshape((L, -1) + rs["o:" + n].shape[-1:]) for n in reduced_names}
    small_all = allgather_chips(rs["o:small"].reshape(-1, LANES), "allgather_small_grads").reshape(-1)[:n_small]
    *rep_list, conv_all = _split_flat(small_all, small_shapes)
    rep_grads = dict(zip(REPLICATED, rep_list))
    shard_grads["conv_w"] = lax.dynamic_index_in_dim(conv_all, me_chip, 1, keepdims=False)

    delta, new_m, new_v = {}, {}, {}
    for n in BIG:
        shp = weights[n].shape
        if n == "w_in":
            view, back = (lambda a: a.transpose(2, 0, 1)), (lambda a: a.transpose(1, 2, 0))
            g = shard_grads[n].transpose(1, 0, 2)
        else:
            view, back = (lambda a: a.reshape(-1, shp[-1])), (lambda a: a.reshape(shp))
            g = view(shard_grads[n])
        d, nm, nv = adamw(view(weights[n]), g, view(mom_m[n]), view(mom_v[n]), "adamw_" + n)
        delta[n], new_m[n], new_v[n], shard_grads[n] = back(d), back(nm), back(nv), back(g)

    def rows(d):
        vec = jnp.concatenate([d[n].reshape(-1) for n in REPLICATED])
        return jnp.pad(vec, (0, -vec.shape[0] % (8 * LANES))).reshape(-1, LANES)

    outs = adamw(rows(weights), rows(rep_grads), rows(mom_m), rows(mom_v), "adamw_replicated")
    for res, o in zip((delta, new_m, new_v), outs):
        res.update(zip(REPLICATED, _split_flat(o.reshape(-1), small_shapes[:len(REPLICATED)])))
    all_grads = {**shard_grads, **rep_grads}

    return (loss, grad_x, *[all_grads[n] for n in ORDER], *[delta[n] for n in ORDER],
            *[new_m[n] for n in ORDER], *[new_v[n] for n in ORDER])
```

```python
import numpy as np
import jax
import jax.numpy as jnp
from jax import lax
from jax.experimental import pallas as pl
from jax.experimental.pallas import tpu as pltpu

F32, BF16 = jnp.float32, jnp.bfloat16
SDS = jax.ShapeDtypeStruct
MESH = pl.DeviceIdType.MESH
AXES = ("x", "y", "c")
N_CHIPS = 4
N_LAYERS = 2
LANES = 128
VMEM_LIMIT = 48 * 1024 * 1024

HEADS, HEAD_DIM = 8, 64
HEAD_PAD = 128
BRANCH_W = 512
GROUP_W = 128
N_GROUPS = BRANCH_W // GROUP_W
POOL_WINDOWS = (2, 4, 8, 16)
F_PAD = 512
ATTN_BLOCK = 256
RMS_EPS = 1e-6
NEG_INF = -1e30
ADAM_LR, ADAM_B1, ADAM_B2, ADAM_EPS, ADAM_WD, ADAM_STEP = 0.001, 0.9, 0.999, 1e-08, 0.01, 10

NT = (((1,), (1,)), ((), ()))
TN = (((0,), (0,)), ((), ()))
_ANY = pl.BlockSpec(memory_space=pl.ANY)


def _tile(n, prefs):
    for p in prefs:
        if n % p == 0:
            return p
    raise ValueError(f"no tile of {prefs} divides {n}")


def _params(*sem):
    return pltpu.CompilerParams(dimension_semantics=sem, vmem_limit_bytes=VMEM_LIMIT)


def _sigmoid(z):
    return 0.5 * jnp.tanh(0.5 * z) + 0.5


def _split3(x):
    h1 = x.astype(BF16)
    r1 = x - h1.astype(F32)
    h2 = r1.astype(BF16)
    h3 = (r1 - h2.astype(F32)).astype(BF16)
    return h1, h2, h3


def _position():
    return lax.axis_index("x"), lax.axis_index("y"), lax.axis_index("c")


def _other_chips(x, y):
    return [(1 - x, y), (x, 1 - y), (1 - x, 1 - y)]


def _remote(src, dst, send_sem, recv_sem, device):
    return pltpu.make_async_remote_copy(src_ref=src, dst_ref=dst, send_sem=send_sem, recv_sem=recv_sem,
                                        device_id=device, device_id_type=MESH)


ROW_SHARDED = ("w_out", "w_down")
FETCHER = dict(w_in=0, w_out=0, w_proj_attn=0, w_proj_pool=0, w_gate_up=1, w_down=1, w_proj_conv=1, conv_w=1)


class Hosted:
    def __init__(self, pool, jobs):
        self.pool, self.jobs = pool, list(jobs)
        names = set()
        for job in self.jobs:
            names.update(job[1:3] if job[0] in ("swap", "xchg") else job[1:2])
        self.names = sorted(names)


def _hosted_plan(hosted, refs, send_sems, recv_sems):
    x, y, c = _position()
    me = 2 * x + y
    others = _other_chips(x, y)
    sibling = (x, y, 1 - c)
    plan = []
    for j, job in enumerate(hosted.jobs):
        kind = job[0]
        sems = lambda k, j=j: (send_sems.at[j, k], recv_sems.at[j, k])
        if kind in ("ici", "fwd"):
            _, name, layer = job
            ref = refs[name]
            win = (lambda chip, ref=ref, layer=layer: ref.at[layer, chip]) if name in ROW_SHARDED else (
                lambda chip, ref=ref, layer=layer: ref.at[chip, layer])
            mine = c == FETCHER[name]
            if kind == "ici":
                sends = [_remote(win(me), win(me), *sems(k), (px, py, c)) for k, (px, py) in enumerate(others)]
                arrivals = [_remote(win(2 * px + py), win(2 * px + py), *sems(k), (px, py, c))
                            for k, (px, py) in enumerate(others)]
                plan.append((mine, sends, arrivals, []))
            else:
                sends = [_remote(win(2 * px + py), win(2 * px + py), *sems(k), sibling)
                         for k, (px, py) in enumerate(others)]
                plan.append((mine, sends, [], sends))
        elif kind == "swap":
            _, src, dst, layer = job
            cp = _remote(refs[src].at[layer, :, 1 - c], refs[dst], *sems(0), sibling)
            plan.append((True, [cp], [cp], []))
        elif kind == "xchg":
            _, src, dst = job
            sends = [_remote(refs[src].at[2 * px + py], refs[dst].at[me], *sems(k), (px, py, c))
                     for k, (px, py) in enumerate(others)]
            arrivals = [_remote(refs[src].at[me], refs[dst].at[2 * px + py], *sems(k), (px, py, c))
                        for k, (px, py) in enumerate(others)]
            plan.append((True, sends, arrivals, []))
        else:
            _, name, layer = job
            ref = refs[name]
            cp = _remote(ref.at[layer, c], ref.at[layer, c], *sems(0), sibling)
            arrival = _remote(ref.at[layer, c], ref.at[layer, 1 - c], *sems(0), sibling)
            plan.append((True, [cp], [arrival], []))
    return plan


def _hosted_start(plan, now):
    for mine, sends, _, _ in plan:
        @pl.when(now & mine)
        def _(sends=sends):
            for cp in sends:
                cp.start()


def _hosted_finish(plan, now):
    for mine, sends, arrivals, sibling_arrivals in plan:
        @pl.when(now & mine)
        def _(sends=sends, arrivals=arrivals):
            for cp in arrivals:
                cp.wait_recv()
            for cp in sends:
                cp.wait_send()

        if sibling_arrivals:
            @pl.when(now & jnp.logical_not(mine))
            def _(sibling_arrivals=sibling_arrivals):
                for cp in sibling_arrivals:
                    cp.wait_recv()


def _pcall(body, hosted, *, name, grid, in_specs, out_specs, out_shape, semantics, scratch_shapes=(), aliases=None):
    aliases = dict(aliases or {})
    if hosted is None or not hosted.jobs:
        return pl.pallas_call(body, name=name, grid=grid, in_specs=in_specs, out_specs=out_specs,
                              out_shape=out_shape, scratch_shapes=list(scratch_shapes),
                              input_output_aliases=aliases, compiler_params=_params(*semantics))
    single = not isinstance(out_shape, (list, tuple))
    out_specs_l = [out_specs] if single else list(out_specs)
    out_shape_l = [out_shape] if single else list(out_shape)
    n_in, n_out, n_buf, n_job = len(in_specs), len(out_specs_l), len(hosted.names), len(hosted.jobs)

    def carrying(*refs):
        ins, outs = refs[:n_in], refs[n_in + n_buf:n_in + n_buf + n_out]
        bufs = refs[n_in + n_buf + n_out:n_in + 2 * n_buf + n_out]
        rest = refs[n_in + 2 * n_buf + n_out:]
        scratch, send_sems, recv_sems = rest[:-2], rest[-2], rest[-1]
        first, last = True, True
        for axis, size in enumerate(grid):
            first = first & (pl.program_id(axis) == 0)
            last = last & (pl.program_id(axis) == size - 1)
        plan = _hosted_plan(hosted, dict(zip(hosted.names, bufs)), send_sems, recv_sems)
        _hosted_start(plan, first)
        body(*ins, *outs, *scratch)
        _hosted_finish(plan, last)

    def run(*args):
        bufs = [hosted.pool[n] for n in hosted.names]
        sem = pltpu.SemaphoreType.DMA
        res = pl.pallas_call(
            carrying, name=name, grid=grid, in_specs=list(in_specs) + [_ANY] * n_buf,
            out_specs=out_specs_l + [_ANY] * n_buf,
            out_shape=out_shape_l + [SDS(b.shape, b.dtype) for b in bufs],
            scratch_shapes=list(scratch_shapes) + [sem((n_job, 3)), sem((n_job, 3))],
            input_output_aliases={**aliases, **{n_in + i: n_out + i for i in range(n_buf)}},
            compiler_params=pltpu.CompilerParams(dimension_semantics=semantics, vmem_limit_bytes=VMEM_LIMIT,
                                                 has_side_effects=True),
        )(*args, *bufs)
        hosted.pool.update(zip(hosted.names, res[n_out:]))
        return res[0] if single else res[:n_out]

    return run


def _dot(a, b):
    return jnp.dot(a, b, preferred_element_type=F32)


def _dot_nt(a, b):
    return lax.dot_general(a, b, NT, preferred_element_type=F32)


def _dot_tn(a, b):
    return lax.dot_general(a, b, TN, preferred_element_type=F32)


def norm_matmul(x, gain, w, layer, kind, name, hosted=None):
    T, D = x.shape
    if kind == "by_shard":
        tn = w.shape[3]
        N = N_CHIPS * tn
        w_spec = pl.BlockSpec((None, None, D, tn), lambda i, j: (j, layer, 0, 0))
        mm = _dot
    else:
        N = w.shape[0]
        tn = _tile(N, (1024, 512, 256, 128))
        w_spec = pl.BlockSpec((tn, D), lambda i, j: (j, 0))
        mm = _dot_nt
    tm = _tile(T, (2048, 1024, 512, 256, 128) if kind == "rows" else (1024, 512, 256, 128))

    def body(x_ref, g_ref, w_ref, y_ref, h_ref):
        @pl.when(pl.program_id(1) == 0)
        def _():
            xf = x_ref[...]
            r = lax.rsqrt(jnp.mean(xf * xf, axis=-1, keepdims=True) + RMS_EPS)
            h_ref[...] = ((xf * r) * g_ref[...]).astype(BF16)

        y_ref[...] = mm(h_ref[...], w_ref[...]).astype(BF16)

    return _pcall(
        body, hosted, name=name, grid=(T // tm, N // tn),
        in_specs=[pl.BlockSpec((tm, D), lambda i, j: (i, 0)),
                  pl.BlockSpec((None, 1, D), lambda i, j: (layer, 0, 0)),
                  w_spec],
        out_specs=[pl.BlockSpec((tm, tn), lambda i, j: (i, j)),
                   pl.BlockSpec((tm, D), lambda i, j: (i, 0))],
        out_shape=[SDS((T, N), BF16), SDS((T, D), BF16)],
        semantics=("arbitrary", "arbitrary"),
    )(x, gain, w)


def matmul_nt_normbwd(dys, w, layer, kind, x, gain, dres, name, hosted=None):
    T, D = x.shape
    width = dys[0].shape[1]
    if kind == "by_shard":
        tk = w.shape[3]
        w_spec = pl.BlockSpec((None, None, D, tk), lambda i, k: (k, layer, 0, 0))
        mm = _dot_nt
    else:
        tk = _tile(width, (3584, 1024, 512, 256, 128))
        w_spec = pl.BlockSpec((tk, D), lambda i, k: (k, 0))
        mm = _dot
    per = width // tk
    nk = per * len(dys)
    tm = _tile(T, (512, 256, 128))
    n_dy = len(dys)

    def dy_spec(p):
        return pl.BlockSpec((tm, tk), lambda i, k: (i, jnp.clip(k - p * per, 0, per - 1)))

    def body(*refs):
        dy_refs = refs[:n_dy]
        w_ref, x_ref, g_ref, dres_ref, dx_ref, dxb_ref, dg_ref, acc_ref = refs[n_dy:]
        i, k = pl.program_id(0), pl.program_id(1)

        @pl.when(k == 0)
        def _():
            acc_ref[...] = jnp.zeros_like(acc_ref)

        for p in range(n_dy):
            @pl.when((k >= p * per) & (k < (p + 1) * per))
            def _(p=p):
                acc_ref[...] += mm(dy_refs[p][...], w_ref[...])

        @pl.when(k == nk - 1)
        def _():
            xf = x_ref[...]
            r = lax.rsqrt(jnp.mean(xf * xf, axis=-1, keepdims=True) + RMS_EPS)
            xhat = xf * r
            dh = acc_ref[...]
            dhg = dh * g_ref[...]
            dx = dres_ref[...] + r * (dhg - xhat * jnp.mean(dhg * xhat, axis=-1, keepdims=True))
            dx_ref[...] = dx
            dxb_ref[...] = dx.astype(BF16)
            part = jnp.sum(dh * xhat, axis=0, keepdims=True)

            @pl.when(i == 0)
            def _():
                dg_ref[...] = part

            @pl.when(i > 0)
            def _():
                dg_ref[...] += part

    row = pl.BlockSpec((tm, D), lambda i, k: (i, 0))
    return _pcall(
        body, hosted, name=name, grid=(T // tm, nk),
        in_specs=[dy_spec(p) for p in range(n_dy)] + [
            w_spec, row, pl.BlockSpec((None, 1, D), lambda i, k: (layer, 0, 0)), row],
        out_specs=[row, row, pl.BlockSpec((1, D), lambda i, k: (0, 0))],
        out_shape=[SDS((T, D), F32), SDS((T, D), BF16), SDS((1, D), F32)],
        scratch_shapes=[pltpu.VMEM((tm, D), F32)],
        semantics=("arbitrary", "arbitrary"),
    )(*dys, w, x, gain, dres)


def matmul_tn(a, bs, name, b_col0=0, n_cols=None, by_dest=False, tn=None, tk=None, hosted=None):
    T, M = a.shape
    width = bs[0].shape[1]
    N = n_cols if n_cols else width * len(bs)
    tm = _tile(M, (1408, 1024, 512, 256, 128))
    tn = tn or _tile(N, (512, 256, 128))
    tk = tk or _tile(T, (4096, 2048, 1024, 512, 256))
    assert b_col0 % tn == 0 and width % tn == 0
    j0, per, nk, n_b = b_col0 // tn, width // tn, T // tk, len(bs)

    def b_spec(p):
        return pl.BlockSpec((tk, tn), lambda i, j, k: (k, jnp.clip(j0 + j - p * per, 0, per - 1)))

    def body(*refs):
        a_ref, b_refs = refs[0], refs[1:1 + n_b]
        o_ref, acc_ref = refs[-2], refs[-1]
        j, k = pl.program_id(1), pl.program_id(2)

        @pl.when(k == 0)
        def _():
            acc_ref[...] = jnp.zeros_like(acc_ref)

        for p in range(n_b):
            @pl.when((j0 + j >= p * per) & (j0 + j < (p + 1) * per))
            def _(p=p):
                acc_ref[...] += _dot_tn(a_ref[...], b_refs[p][...])

        @pl.when(k == nk - 1)
        def _():
            o_ref[...] = acc_ref[...].astype(BF16)

    if by_dest:
        cs = N // N_CHIPS
        npd = cs // tn
        out_shape = SDS((N_CHIPS, M, cs), BF16)
        out_spec = pl.BlockSpec((None, tm, tn), lambda i, j, k: (j // npd, i, j % npd))
    else:
        out_shape = SDS((M, N), BF16)
        out_spec = pl.BlockSpec((tm, tn), lambda i, j, k: (i, j))
    return _pcall(
        body, hosted, name=name, grid=(M // tm, N // tn, nk),
        in_specs=[pl.BlockSpec((tk, tm), lambda i, j, k: (k, i))] + [b_spec(p) for p in range(n_b)],
        out_specs=out_spec, out_shape=out_shape,
        scratch_shapes=[pltpu.VMEM((tm, tn), F32)],
        semantics=("arbitrary", "arbitrary", "arbitrary"),
    )(a, *bs)


def ffn_down_fwd(ab, w_down, layer, x1, hosted=None):
    T, D = x1.shape
    F = w_down.shape[1]
    tm = _tile(T, (512, 256, 128))
    tk = F // 2
    nk = F // tk

    def body(a_ref, b_ref, w_ref, x_ref, x2_ref, s_ref, acc_ref):
        k = pl.program_id(1)

        @pl.when(k == 0)
        def _():
            acc_ref[...] = x_ref[...]

        a = a_ref[...].astype(F32)
        s = (a * _sigmoid(a) * b_ref[...].astype(F32)).astype(BF16)
        s_ref[...] = s
        acc_ref[...] += _dot(s, w_ref[...])

        @pl.when(k == nk - 1)
        def _():
            x2_ref[...] = acc_ref[...]

    return _pcall(
        body, hosted, name="ffn_down_fwd", grid=(T // tm, nk),
        in_specs=[pl.BlockSpec((tm, tk), lambda i, k: (i, k)),
                  pl.BlockSpec((tm, tk), lambda i, k: (i, nk + k)),
                  pl.BlockSpec((None, tk, D), lambda i, k: (layer, k, 0)),
                  pl.BlockSpec((tm, D), lambda i, k: (i, 0))],
        out_specs=[pl.BlockSpec((tm, D), lambda i, k: (i, 0)),
                   pl.BlockSpec((tm, tk), lambda i, k: (i, k))],
        out_shape=[SDS((T, D), F32), SDS((T, F), BF16)],
        scratch_shapes=[pltpu.VMEM((tm, D), F32)],
        semantics=("arbitrary", "arbitrary"),
    )(ab, ab, w_down, x1)


def ffn_down_bwd(dx2b, w_down, layer, ab, hosted=None):
    T, D = dx2b.shape
    F = w_down.shape[1]
    tm = _tile(T, (512, 256, 128))
    tn = F // 2
    nj = F // tn

    def body(dx_ref, w_ref, a_ref, b_ref, da_ref, db_ref):
        ds = _dot_nt(dx_ref[...], w_ref[...])
        a = a_ref[...].astype(F32)
        sg = _sigmoid(a)
        da_ref[...] = (ds * b_ref[...].astype(F32) * (sg * (1.0 + a * (1.0 - sg)))).astype(BF16)
        db_ref[...] = (ds * (a * sg)).astype(BF16)

    blk = pl.BlockSpec((tm, tn), lambda j, i: (i, j))
    return _pcall(
        body, hosted, name="ffn_down_bwd", grid=(nj, T // tm),
        in_specs=[pl.BlockSpec((tm, D), lambda j, i: (i, 0)),
                  pl.BlockSpec((None, tn, D), lambda j, i: (layer, j, 0)),
                  blk, pl.BlockSpec((tm, tn), lambda j, i: (i, nj + j))],
        out_specs=[blk, blk],
        out_shape=[SDS((T, F), BF16), SDS((T, F), BF16)],
        semantics=("arbitrary", "arbitrary"),
    )(dx2b, w_down, ab, ab)


def _mix_specs(tm, D, layer):
    cs = D // N_CHIPS
    row = lambda w: pl.BlockSpec((tm, w), lambda i: (i, 0))
    wp = pl.BlockSpec((N_CHIPS, None, BRANCH_W, cs), lambda i: (0, layer, 0, 0))
    wo = pl.BlockSpec((None, N_CHIPS, cs, D), lambda i: (layer, 0, 0, 0))
    bg = pl.BlockSpec((None, 1, 3 * D), lambda i: (layer, 0, 0))
    return row, wp, wo, bg


def mix_fwd(ao, po, co, proj, b_gate, wpa, wpp, wpc, w_out, layer, x, hosted=None):
    T, D = x.shape
    cs = D // N_CHIPS
    tm = _tile(T, (256, 128))
    row, wp, wo, bg = _mix_specs(tm, D, layer)

    def body(ao_ref, po_ref, co_ref, g_ref, bg_ref, wpa_ref, wpp_ref, wpc_ref, wo_ref, x_ref,
             x1_ref, ys_ref, mixed_ref):
        mixed = jnp.zeros((tm, D), F32)
        for n, (br, wp_ref) in enumerate(((ao_ref, wpa_ref), (po_ref, wpp_ref), (co_ref, wpc_ref))):
            y = jnp.concatenate([_dot(br[...], wp_ref[j]) for j in range(N_CHIPS)], axis=1)
            cols = slice(n * D, (n + 1) * D)
            gate = _sigmoid(g_ref[:, cols].astype(F32) + bg_ref[:, cols])
            ys_ref[:, cols] = y.astype(BF16)
            mixed = mixed + gate * y
        mb = mixed.astype(BF16)
        mixed_ref[...] = mb
        acc = x_ref[...]
        for j in range(N_CHIPS):
            acc = acc + _dot(mb[:, j * cs:(j + 1) * cs], wo_ref[j])
        x1_ref[...] = acc

    return _pcall(
        body, hosted, name="mix_fwd", grid=(T // tm,),
        in_specs=[row(BRANCH_W), row(BRANCH_W), row(BRANCH_W), row(3 * D), bg, wp, wp, wp, wo, row(D)],
        out_specs=[row(D), row(3 * D), row(D)],
        out_shape=[SDS((T, D), F32), SDS((T, 3 * D), BF16), SDS((T, D), BF16)],
        semantics=("arbitrary",),
    )(ao, po, co, proj, b_gate, wpa, wpp, wpc, w_out, x)


def mix_bwd(dx1b, w_out, proj, b_gate, ys, wpa, wpp, wpc, layer, width, hosted=None):
    T, D = dx1b.shape
    cs = D // N_CHIPS
    tm = _tile(T, (256, 128))
    row, wp, wo, bg = _mix_specs(tm, D, layer)

    def body(dx_ref, wo_ref, g_ref, bg_ref, ys_ref, wpa_ref, wpp_ref, wpc_ref,
             dys_ref, dg_ref, dao_ref, dpo_ref, dco_ref, dbg_ref):
        i = pl.program_id(0)
        dx = dx_ref[...]
        dmixed = jnp.concatenate([_dot_nt(dx, wo_ref[j]) for j in range(N_CHIPS)], axis=1)
        for n, (wp_ref, dbr) in enumerate(((wpa_ref, dao_ref), (wpp_ref, dpo_ref), (wpc_ref, dco_ref))):
            cols = slice(n * D, (n + 1) * D)
            gate = _sigmoid(g_ref[:, cols].astype(F32) + bg_ref[:, cols])
            dy = (dmixed * gate).astype(BF16)
            dys_ref[:, cols] = dy
            dgp = dmixed * ys_ref[:, cols].astype(F32) * gate * (1.0 - gate)
            dg_ref[:, cols] = dgp.astype(BF16)
            part = jnp.sum(dgp, axis=0, keepdims=True)

            @pl.when(i == 0)
            def _():
                dbg_ref[:, cols] = part

            @pl.when(i > 0)
            def _():
                dbg_ref[:, cols] += part

            acc = jnp.zeros((tm, BRANCH_W), F32)
            for j in range(N_CHIPS):
                acc = acc + _dot_nt(dy[:, j * cs:(j + 1) * cs], wp_ref[j])
            dbr[...] = acc.astype(BF16)

    return _pcall(
        body, hosted, name="mix_bwd", grid=(T // tm,),
        in_specs=[row(D), wo, row(3 * D), bg, row(3 * D), wp, wp, wp],
        out_specs=[row(3 * D), row(3 * D), row(BRANCH_W), row(BRANCH_W), row(BRANCH_W),
                   pl.BlockSpec((1, 3 * D), lambda i: (0, 0))],
        out_shape=[SDS((T, 3 * D), BF16), SDS((T, width), BF16), SDS((T, BRANCH_W), BF16),
                   SDS((T, BRANCH_W), BF16), SDS((T, BRANCH_W), BF16), SDS((1, 3 * D), F32)],
        semantics=("arbitrary",),
    )(dx1b, w_out, proj, b_gate, ys, wpa, wpp, wpc)


def loss_head(x2, gain, target):
    T, D = x2.shape
    tm = _tile(T, (512, 256, 128))

    def body(x_ref, g_ref, t_ref, loss_ref, dx_ref, dxb_ref, dg_ref):
        i = pl.program_id(0)
        xf = x_ref[...]
        g = g_ref[...]
        r = lax.rsqrt(jnp.mean(xf * xf, axis=-1, keepdims=True) + RMS_EPS)
        xhat = xf * r
        diff = xhat * g - t_ref[...]
        part_loss = 0.5 * jnp.sum(jnp.mean(diff * diff, axis=-1, keepdims=True), axis=0, keepdims=True)
        dy = diff * (1.0 / D)
        dhg = dy * g
        dx = r * (dhg - xhat * jnp.mean(dhg * xhat, axis=-1, keepdims=True))
        dx_ref[...] = dx
        dxb_ref[...] = dx.astype(BF16)
        part_g = jnp.sum(dy * xhat, axis=0, keepdims=True)
        part_l = jnp.broadcast_to(part_loss, (1, LANES))

        @pl.when(i == 0)
        def _():
            dg_ref[...] = part_g
            loss_ref[...] = part_l

        @pl.when(i > 0)
        def _():
            dg_ref[...] += part_g
            loss_ref[...] += part_l

    row = pl.BlockSpec((tm, D), lambda i: (i, 0))
    return pl.pallas_call(
        body, name="loss_head", grid=(T // tm,),
        in_specs=[row, pl.BlockSpec((1, D), lambda i: (0, 0)), row],
        out_specs=[pl.BlockSpec((1, LANES), lambda i: (0, 0)), row, row, pl.BlockSpec((1, D), lambda i: (0, 0))],
        out_shape=[SDS((1, LANES), F32), SDS((T, D), F32), SDS((T, D), BF16), SDS((1, D), F32)],
        compiler_params=_params("arbitrary"),
    )(x2, gain, target)


def _placement_constants():
    w = HEADS * HEAD_PAD
    pq = np.zeros((BRANCH_W, w), np.float32)
    pk = np.zeros((BRANCH_W, w), np.float32)
    pfq = np.zeros((3, LANES, w), np.float32)
    pfk = np.zeros((3, LANES, w), np.float32)
    cq = np.zeros((1, w), np.float32)
    ck = np.zeros((1, w), np.float32)
    eq = np.zeros((w, LANES), np.float32)
    ek = np.zeros((w, LANES), np.float32)
    for h in range(HEADS):
        for d in range(HEAD_DIM):
            pq[h * HEAD_DIM + d, h * HEAD_PAD + d] = HEAD_DIM ** -0.5
            pk[h * HEAD_DIM + d, h * HEAD_PAD + d] = 1.0
        for i in range(3):
            pfq[i, h, h * HEAD_PAD + HEAD_DIM + i] = 1.0
            pfk[i, h, h * HEAD_PAD + HEAD_DIM + 3 + i] = -1.0
            cq[0, h * HEAD_PAD + HEAD_DIM + 3 + i] = 1.0
            ck[0, h * HEAD_PAD + HEAD_DIM + i] = 1.0
        eq[h * HEAD_PAD + HEAD_DIM, h] = 1.0
        ek[h * HEAD_PAD + HEAD_DIM + 3, h] = -1.0
    bf = lambda a: jnp.asarray(a, BF16)
    return dict(pq=bf(pq), pk=bf(pk), pfq=bf(pfq), pfk=bf(pfk), cq=jnp.asarray(cq), ck=jnp.asarray(ck),
                pqkt=bf(np.concatenate([pq.T, pk.T], axis=0)), eq=bf(eq), ek=bf(ek))


def attn_prep(proj3, bf_rows, layer, cst, lay, hosted=None):
    Bl, S, _ = proj3.shape
    ts = ATTN_BLOCK
    w = HEADS * HEAD_PAD

    def body(q_ref, k_ref, f_ref, bf_ref, pq_ref, pk_ref, pfq_ref, pfk_ref, cq_ref, ck_ref,
             qa_ref, ka_ref, carry_ref):
        @pl.when(pl.program_id(1) == 0)
        def _():
            carry_ref[...] = jnp.zeros_like(carry_ref)

        z = f_ref[...].astype(F32) + bf_ref[...]
        logf = jnp.minimum(z, 0.0) - jnp.log(1.0 + jnp.exp(-jnp.abs(z)))
        r = lax.broadcasted_iota(jnp.int32, (ts, ts), 0)
        c = lax.broadcasted_iota(jnp.int32, (ts, ts), 1)
        tri = jnp.where(r >= c, 1.0, 0.0).astype(BF16)
        fcum = carry_ref[...]
        for part in _split3(logf):
            fcum = fcum + _dot(tri, part)
        carry_ref[...] = fcum[ts - 1:ts, :]
        qa = _dot(q_ref[...], pq_ref[...]) + cq_ref[...]
        ka = _dot(k_ref[...], pk_ref[...]) + ck_ref[...]
        for i, part in enumerate(_split3(fcum)):
            qa = qa + _dot(part, pfq_ref[i])
            ka = ka + _dot(part, pfk_ref[i])
        qa_ref[...] = qa.astype(BF16)
        ka_ref[...] = ka.astype(BF16)

    cfull = lambda shape: pl.BlockSpec(shape, lambda b, s: (0,) * len(shape))
    return _pcall(
        body, hosted, name="attn_prep", grid=(Bl, S // ts),
        in_specs=[pl.BlockSpec((None, ts, BRANCH_W), lambda b, s: (b, s, lay["q"] // BRANCH_W)),
                  pl.BlockSpec((None, ts, BRANCH_W), lambda b, s: (b, s, lay["k"] // BRANCH_W)),
                  pl.BlockSpec((None, ts, LANES), lambda b, s: (b, s, lay["f"] // LANES)),
                  pl.BlockSpec((None, 1, LANES), lambda b, s: (layer, 0, 0)),
                  cfull((BRANCH_W, w)), cfull((BRANCH_W, w)),
                  cfull((3, LANES, w)), cfull((3, LANES, w)), cfull((1, w)), cfull((1, w))],
        out_specs=[pl.BlockSpec((None, ts, w), lambda b, s: (b, s, 0)),
                   pl.BlockSpec((None, ts, w), lambda b, s: (b, s, 0))],
        out_shape=[SDS((Bl, S, w), BF16), SDS((Bl, S, w), BF16)],
        scratch_shapes=[pltpu.VMEM((1, LANES), F32)],
        semantics=("arbitrary", "arbitrary"),
    )(proj3, proj3, proj3, bf_rows, cst["pq"], cst["pk"], cst["pfq"], cst["pfk"], cst["cq"], cst["ck"])


def attn_fwd(qa, ka, proj3, lay, hosted=None):
    Bl, S, _ = qa.shape
    tq = ATTN_BLOCK
    nq = S // tq
    pairs = HEADS // 2
    pw = 2 * HEAD_PAD
    vw = 2 * HEAD_DIM

    def body(qa_ref, ka_ref, v_ref, o_ref, lse_ref):
        row = lax.broadcasted_iota(jnp.int32, (tq, tq), 0)
        col = lax.broadcasted_iota(jnp.int32, (tq, tq), 1)
        causal = row <= col
        for i in range(nq):
            nk = (i + 1) * tq
            rows = slice(i * tq, nk)
            o_t = []
            for h in range(2):
                hs = slice(h * HEAD_PAD, (h + 1) * HEAD_PAD)
                st = _dot_nt(ka_ref[0:nk, hs], qa_ref[rows, hs])
                diag = jnp.where(causal, st[nk - tq:], NEG_INF)
                m = jnp.max(diag, axis=0, keepdims=True)
                if i:
                    m = jnp.maximum(m, jnp.max(st[:nk - tq], axis=0, keepdims=True))
                p_diag = jnp.exp(diag - m)
                l = jnp.sum(p_diag, axis=0, keepdims=True)
                if i:
                    p_top = jnp.exp(st[:nk - tq] - m)
                    l = l + jnp.sum(p_top, axis=0, keepdims=True)
                    p = jnp.concatenate([p_top.astype(BF16), p_diag.astype(BF16)], axis=0)
                else:
                    p = p_diag.astype(BF16)
                acc = _dot_tn(v_ref[0:nk, :], p)
                o_t.append(acc[h * HEAD_DIM:(h + 1) * HEAD_DIM, :] / l)
                lse_ref[h:h + 1, rows] = m + jnp.log(l)
            o_ref[rows, :] = jnp.concatenate(o_t, axis=0).T.astype(BF16)

    return _pcall(
        body, hosted, name="attn_fwd", grid=(Bl, pairs),
        in_specs=[pl.BlockSpec((None, S, pw), lambda b, p: (b, 0, p)),
                  pl.BlockSpec((None, S, pw), lambda b, p: (b, 0, p)),
                  pl.BlockSpec((None, S, vw), lambda b, p: (b, 0, lay["v"] // vw + p))],
        out_specs=[pl.BlockSpec((None, S, vw), lambda b, p: (b, 0, p)),
                   pl.BlockSpec((None, None, 2, S), lambda b, p: (b, p, 0, 0))],
        out_shape=[SDS((Bl, S, BRANCH_W), BF16), SDS((Bl, pairs, 2, S), F32)],
        semantics=("arbitrary", "arbitrary"),
    )(qa, ka, proj3)


def attn_bwd(qa, ka, proj3, dao, ao, lse, dproj3, lay, hosted=None):
    Bl, S, _ = qa.shape
    tk = ATTN_BLOCK
    nq = S // tk
    pairs = HEADS // 2
    pw = 2 * HEAD_PAD
    vw = 2 * HEAD_DIM

    def body(qa_ref, ka_ref, v_ref, do_ref, o_ref, lse_ref, _, dqa_ref, dka_ref, dv_ref):
        row = lax.broadcasted_iota(jnp.int32, (tk, tk), 0)
        col = lax.broadcasted_iota(jnp.int32, (tk, tk), 1)
        causal = row <= col
        lane8 = lax.broadcasted_iota(jnp.int32, (8, vw), 1)
        lane_s = lax.broadcasted_iota(jnp.int32, (S, vw), 1)
        lane_k = lax.broadcasted_iota(jnp.int32, (tk, vw), 1)
        doo = do_ref[...].astype(F32) * o_ref[...].astype(F32)
        hi = doo.astype(BF16)
        lo = (doo - hi.astype(F32)).astype(BF16)
        delta, v_head = [], []
        for h in range(2):
            sel = jnp.where((lane8 >= h * HEAD_DIM) & (lane8 < (h + 1) * HEAD_DIM), 1.0, 0.0).astype(BF16)
            delta.append((_dot_nt(sel, hi) + _dot_nt(sel, lo))[0:1, :])
            in_head = (lane_s >= h * HEAD_DIM) & (lane_s < (h + 1) * HEAD_DIM)
            v_head.append(jnp.where(in_head, v_ref[...], jnp.zeros_like(v_ref[...])))
        dqa_ref[...] = jnp.zeros_like(dqa_ref)
        for j in range(nq):
            q0 = j * tk
            krows = slice(q0, q0 + tk)
            do = do_ref[q0:, :]
            dvs = []
            for h in range(2):
                hs = slice(h * HEAD_PAD, (h + 1) * HEAD_PAD)
                k = ka_ref[krows, hs]
                q = qa_ref[q0:, hs]
                st = _dot_nt(k, q)
                p = jnp.exp(st - lse_ref[h:h + 1, q0:])
                p_diag = jnp.where(causal, p[:, :tk], 0.0)
                p = jnp.concatenate([p_diag, p[:, tk:]], axis=1) if j < nq - 1 else p_diag
                dvs.append(_dot(p.astype(BF16), do))
                dpt = _dot_nt(v_head[h][krows, :], do)
                ds = (p * (dpt - delta[h][:, q0:])).astype(BF16)
                dka_ref[krows, hs] = _dot(ds, q)
                dqa_ref[q0:, hs] += _dot_tn(ds, k)
            dv_ref[krows, :] = jnp.where(lane_k < HEAD_DIM, dvs[0], dvs[1]).astype(BF16)

    seq = lambda w, c0=0: pl.BlockSpec((None, S, w), lambda b, p: (b, 0, c0 + p))
    return _pcall(
        body, hosted, name="attn_bwd", grid=(Bl, pairs),
        in_specs=[seq(pw), seq(pw), seq(vw, lay["v"] // vw), seq(vw), seq(vw),
                  pl.BlockSpec((None, None, 2, S), lambda b, p: (b, p, 0, 0)), _ANY],
        out_specs=[seq(pw), seq(pw), seq(vw, lay["v"] // vw)],
        out_shape=[SDS((Bl, S, HEADS * HEAD_PAD), F32), SDS((Bl, S, HEADS * HEAD_PAD), F32),
                   SDS(dproj3.shape, BF16)],
        aliases={6: 2}, semantics=("arbitrary", "arbitrary"),
    )(qa, ka, proj3, dao, ao, lse, dproj3)


def attn_post(dqa, dka, proj3, bf_rows, layer, dproj3, cst, lay, hosted=None):
    Bl, S, w = dqa.shape
    ts = ATTN_BLOCK
    ns = S // ts
    qkf = 2 * BRANCH_W + F_PAD

    def body(dqa_ref, dka_ref, f_ref, bf_ref, pqkt_ref, eq_ref, ek_ref, _, dqkf_ref, dbf_ref, carry_ref):
        b, s = pl.program_id(0), pl.program_id(1)

        @pl.when(s == 0)
        def _():
            carry_ref[...] = jnp.zeros_like(carry_ref)

        dqa_v, dka_v = dqa_ref[...], dka_ref[...]
        qh = dqa_v.astype(BF16)
        kh = dka_v.astype(BF16)
        dqkf_ref[:, :BRANCH_W] = _dot(qh, pqkt_ref[:w, :]).astype(BF16)
        dqkf_ref[:, BRANCH_W:2 * BRANCH_W] = _dot(kh, pqkt_ref[w:, :]).astype(BF16)
        ql = (dqa_v - qh.astype(F32)).astype(BF16)
        kl = (dka_v - kh.astype(F32)).astype(BF16)
        d_f = (_dot(qh, eq_ref[...]) + _dot(ql, eq_ref[...])) + (_dot(kh, ek_ref[...]) + _dot(kl, ek_ref[...]))
        r = lax.broadcasted_iota(jnp.int32, (ts, ts), 0)
        c = lax.broadcasted_iota(jnp.int32, (ts, ts), 1)
        triu = jnp.where(c >= r, 1.0, 0.0).astype(BF16)
        rev = carry_ref[...]
        for part in _split3(d_f):
            rev = rev + _dot(triu, part)
        carry_ref[...] = rev[0:1, :]
        z = f_ref[...].astype(F32) + bf_ref[...]
        lane = lax.broadcasted_iota(jnp.int32, (ts, LANES), 1)
        dfl = jnp.where(lane < HEADS, rev / (1.0 + jnp.exp(z)), 0.0)
        dqkf_ref[:, 2 * BRANCH_W:] = jnp.concatenate(
            [dfl.astype(BF16), jnp.zeros((ts, F_PAD - LANES), BF16)], axis=1)
        part = jnp.sum(dfl, axis=0, keepdims=True)

        @pl.when((b == 0) & (s == 0))
        def _():
            dbf_ref[...] = part

        @pl.when((b > 0) | (s > 0))
        def _():
            dbf_ref[...] += part

    assert lay["q"] % qkf == 0
    cfull = lambda shape: pl.BlockSpec(shape, lambda b, s: (0,) * len(shape))
    rev_blk = lambda wd, c0=0: pl.BlockSpec((None, ts, wd), lambda b, s: (b, ns - 1 - s, c0))
    return _pcall(
        body, hosted, name="attn_post", grid=(Bl, ns),
        in_specs=[rev_blk(w), rev_blk(w), rev_blk(LANES, lay["f"] // LANES),
                  pl.BlockSpec((None, 1, LANES), lambda b, s: (layer, 0, 0)),
                  cfull((2 * w, BRANCH_W)), cfull((w, LANES)), cfull((w, LANES)), _ANY],
        out_specs=[rev_blk(qkf, lay["q"] // qkf), cfull((1, LANES))],
        out_shape=[SDS(dproj3.shape, BF16), SDS((1, LANES), F32)],
        scratch_shapes=[pltpu.VMEM((1, LANES), F32)],
        aliases={7: 0}, semantics=("arbitrary", "arbitrary"),
    )(dqa, dka, proj3, bf_rows, cst["pqkt"], cst["eq"], cst["ek"], dproj3)


def _shift_down(x, k, row):
    return jnp.where(row >= k, pltpu.roll(x, k, axis=0), 0.0)


def _shift_up(x, k, row):
    n = x.shape[0]
    return jnp.where(row < n - k, pltpu.roll(x, n - k, axis=0), 0.0)


def _window_sum(x, g, row, shift):
    s2 = x + shift(x, 1, row)
    s4 = s2 + shift(s2, 2, row)
    s8 = s4 + shift(s4, 4, row)
    s16 = s8 + shift(s8, 8, row)
    return jnp.where(g == 0, s2, jnp.where(g == 1, s4, jnp.where(g == 2, s8, s16)))


def _window_count(g, row):
    wnd = jnp.where(g == 0, 2, jnp.where(g == 1, 4, jnp.where(g == 2, 8, 16)))
    return jnp.minimum(row + 1, wnd).astype(F32)


def _group_columns(ref):
    return [ref[:, n * GROUP_W:(n + 1) * GROUP_W].astype(F32) for n in range(4)]


def poolconv_fwd(proj3, pool_w, pool_scale, conv_w, layer, lay, hosted=None):
    Bl, S, _ = proj3.shape

    def body(x_ref, pw_ref, ps_ref, cw_ref, po_ref, co_ref):
        g = pl.program_id(1)
        row = lax.broadcasted_iota(jnp.int32, (S, GROUP_W), 0)
        u, cv, cb, cc = _group_columns(x_ref)
        d = _window_sum(u, g, row, _shift_down) / _window_count(g, row) - u
        po_ref[...] = (_dot(d.astype(BF16), pw_ref[...]) * ps_ref[...]).astype(BF16)
        z = cc * cv
        y = cw_ref[0:1, :] * _shift_down(z, 2, row) + cw_ref[1:2, :] * _shift_down(z, 1, row) + cw_ref[2:3, :] * z
        co_ref[...] = (cb * y).astype(BF16)

    out = pl.BlockSpec((None, S, GROUP_W), lambda b, g: (b, 0, g))
    return _pcall(
        body, hosted, name="poolconv_fwd", grid=(Bl, N_GROUPS),
        in_specs=[pl.BlockSpec((None, S, BRANCH_W), lambda b, g: (b, 0, lay["pc"] // BRANCH_W + g)),
                  pl.BlockSpec((None, None, GROUP_W, GROUP_W), lambda b, g: (layer, g, 0, 0)),
                  pl.BlockSpec((None, 1, GROUP_W), lambda b, g: (layer, 0, g)),
                  pl.BlockSpec((None, None, 3, GROUP_W), lambda b, g: (g, layer, 0, 0))],
        out_specs=[out, out],
        out_shape=[SDS((Bl, S, BRANCH_W), BF16), SDS((Bl, S, BRANCH_W), BF16)],
        semantics=("arbitrary", "arbitrary"),
    )(proj3, pool_w, pool_scale, conv_w)


def poolconv_bwd(proj3, dpo, dco, pool_w, pool_scale, conv_w, layer, dproj3, lay, hosted=None):
    Bl, S, _ = proj3.shape

    def body(x_ref, dpo_ref, dco_ref, pw_ref, ps_ref, cw_ref, _, dx_ref, dpw_ref, dps_ref, dcw_ref):
        g, b = pl.program_id(0), pl.program_id(1)
        row = lax.broadcasted_iota(jnp.int32, (S, GROUP_W), 0)
        cnt = _window_count(g, row)
        u, cv, cb, cc = _group_columns(x_ref)
        d = (_window_sum(u, g, row, _shift_down) / cnt - u).astype(BF16)
        pw = pw_ref[...]
        ypre = _dot(d, pw)
        dpo_v = dpo_ref[...].astype(F32)
        dps = jnp.sum(dpo_v * ypre, axis=0, keepdims=True)
        dyp = (dpo_v * ps_ref[...]).astype(BF16)
        dpw = _dot_tn(d, dyp)
        dd = _dot_nt(dyp, pw)
        dx_ref[:, 0:GROUP_W] = (_window_sum(dd / cnt, g, row, _shift_up) - dd).astype(BF16)

        z = cc * cv
        z1, z2 = _shift_down(z, 1, row), _shift_down(z, 2, row)
        w0, w1, w2 = cw_ref[0:1, :], cw_ref[1:2, :], cw_ref[2:3, :]
        y = w0 * z2 + w1 * z1 + w2 * z
        dco_v = dco_ref[...].astype(F32)
        dy = dco_v * cb
        dz = w0 * _shift_up(dy, 2, row) + w1 * _shift_up(dy, 1, row) + w2 * dy
        dx_ref[:, GROUP_W:2 * GROUP_W] = (dz * cc).astype(BF16)
        dx_ref[:, 2 * GROUP_W:3 * GROUP_W] = (dco_v * y).astype(BF16)
        dx_ref[:, 3 * GROUP_W:] = (dz * cv).astype(BF16)
        dcw = jnp.concatenate([jnp.sum(dy * z2, axis=0, keepdims=True),
                               jnp.sum(dy * z1, axis=0, keepdims=True),
                               jnp.sum(dy * z, axis=0, keepdims=True)], axis=0)

        @pl.when(b == 0)
        def _():
            dpw_ref[...] = dpw
            dps_ref[...] = dps
            dcw_ref[...] = dcw

        @pl.when(b > 0)
        def _():
            dpw_ref[...] += dpw
            dps_ref[...] += dps
            dcw_ref[...] += dcw

    blk = pl.BlockSpec((None, S, GROUP_W), lambda g, b: (b, 0, g))
    pc = pl.BlockSpec((None, S, BRANCH_W), lambda g, b: (b, 0, lay["pc"] // BRANCH_W + g))
    return _pcall(
        body, hosted, name="poolconv_bwd", grid=(N_GROUPS, Bl),
        in_specs=[pc, blk, blk,
                  pl.BlockSpec((None, None, GROUP_W, GROUP_W), lambda g, b: (layer, g, 0, 0)),
                  pl.BlockSpec((None, 1, GROUP_W), lambda g, b: (layer, 0, g)),
                  pl.BlockSpec((None, None, 3, GROUP_W), lambda g, b: (g, layer, 0, 0)), _ANY],
        out_specs=[pc, pl.BlockSpec((None, GROUP_W, GROUP_W), lambda g, b: (g, 0, 0)),
                   pl.BlockSpec((1, GROUP_W), lambda g, b: (0, g)),
                   pl.BlockSpec((None, 3, GROUP_W), lambda g, b: (g, 0, 0))],
        out_shape=[SDS(dproj3.shape, BF16), SDS((N_GROUPS, GROUP_W, GROUP_W), F32), SDS((1, BRANCH_W), F32),
                   SDS((N_GROUPS, 3, GROUP_W), F32)],
        aliases={6: 0}, semantics=("arbitrary", "arbitrary"),
    )(proj3, dpo, dco, pool_w, pool_scale, conv_w, dproj3)


def _tile_2d(rows, cols, n_arrays):
    budget = VMEM_LIMIT // 2
    lanes = -(-cols // LANES) * LANES
    if rows % 8 == 0:
        for t in range(min(rows, 2048), 7, -8):
            if rows % t == 0 and 2 * n_arrays * t * lanes * 4 <= budget:
                return t, cols
    for t in (1024, 512, 256, 128):
        if cols % t == 0 and 2 * n_arrays * (rows + 8) * t * 4 <= budget:
            return rows, t
    return rows, cols


def add_pair(kept, layer, where, received, name):
    _, n, _, R, C = kept.shape
    tr, tc = _tile_2d(R, C, 3)

    def body(where_ref, a_ref, b_ref, o_ref):
        o_ref[...] = (a_ref[...].astype(F32) + b_ref[...].astype(F32)).astype(BF16)

    blk = pl.BlockSpec((None, tr, tc), lambda d, i, j, where_ref: (d, i, j))
    grid_spec = pltpu.PrefetchScalarGridSpec(
        num_scalar_prefetch=1, grid=(n, R // tr, C // tc),
        in_specs=[pl.BlockSpec((None, None, None, tr, tc),
                               lambda d, i, j, where_ref: (layer, d, where_ref[0], i, j)), blk],
        out_specs=blk)
    return pl.pallas_call(body, name=name, grid_spec=grid_spec, out_shape=SDS((n, R, C), BF16),
                          compiler_params=_params("arbitrary", "arbitrary", "arbitrary"))(where, kept, received)


def add_chips(arrived, own, layer, where, n_layers, prev, name):
    _, R, C = arrived.shape
    tr, tc = _tile_2d(R, C, 6)

    def body(where_ref, a0, a1, a2, a3, own_ref, *rest):
        o_ref = rest[-1]
        chip = where_ref[1]
        acc = None
        for j, a_ref in enumerate((a0, a1, a2, a3)):
            term = jnp.where(chip == j, own_ref[...], a_ref[...]).astype(F32)
            acc = term if acc is None else acc + term
        o_ref[...] = acc

    def slot(j):
        return pl.BlockSpec((None, tr, tc), lambda i, k, where_ref, j=j: (
            jnp.where(where_ref[1] == j, (j + 1) % N_CHIPS, j), i, k))

    in_specs = [slot(j) for j in range(N_CHIPS)] + [
        pl.BlockSpec((None, tr, tc), lambda i, k, where_ref: (where_ref[1], i, k))]
    args = [where, arrived, arrived, arrived, arrived, own]
    aliases = {}
    if prev is not None:
        in_specs.append(_ANY)
        args.append(prev)
        aliases = {len(args) - 1: 0}
    grid_spec = pltpu.PrefetchScalarGridSpec(
        num_scalar_prefetch=1, grid=(R // tr, C // tc), in_specs=in_specs,
        out_specs=pl.BlockSpec((None, None, tr, tc), lambda i, k, where_ref: (layer, where_ref[0], i, k)))
    return pl.pallas_call(body, name=name, grid_spec=grid_spec, out_shape=SDS((n_layers, 2, R, C), F32),
                          input_output_aliases=aliases,
                          compiler_params=_params("arbitrary", "arbitrary"))(*args)


def adamw(w, g, m, v, name):
    if w.ndim == 2:
        R, C = w.shape
        tr, _ = _tile_2d(R, C, 7)
        grid, blk = (R // tr,), pl.BlockSpec((tr, C), lambda i: (i, 0))
    else:
        N, r, C = w.shape
        tn = max(t for t in range(1, N + 1) if N % t == 0 and t * r * C * 4 <= 1024 * 1024)
        grid, blk = (N // tn,), pl.BlockSpec((tn, r, C), lambda i: (i, 0, 0))

    def body(w_ref, g_ref, m_ref, v_ref, d_ref, nm_ref, nv_ref):
        gv = g_ref[...]
        m_new = ADAM_B1 * m_ref[...] + (1.0 - ADAM_B1) * gv
        v_new = ADAM_B2 * v_ref[...] + (1.0 - ADAM_B2) * (gv * gv)
        m_hat = m_new / (1.0 - ADAM_B1 ** ADAM_STEP)
        v_hat = v_new / (1.0 - ADAM_B2 ** ADAM_STEP)
        d_ref[...] = -ADAM_LR * (m_hat / (jnp.sqrt(v_hat) + ADAM_EPS) + ADAM_WD * w_ref[...])
        nm_ref[...] = m_new
        nv_ref[...] = v_new

    out = SDS(w.shape, F32)
    return pl.pallas_call(body, name=name, grid=grid, in_specs=[blk] * 4, out_specs=[blk] * 3,
                          out_shape=[out, out, out], compiler_params=_params("arbitrary"))(w, g, m, v)


_COMM = pltpu.CompilerParams(has_side_effects=True)


def gather_buffers(shards):
    me_chip = 2 * lax.axis_index("x") + lax.axis_index("y")
    pool = {}
    for name, sh in shards.items():
        L, r, c = sh.shape
        if name in ROW_SHARDED:
            pool[name] = lax.dynamic_update_slice(lax.empty((L, N_CHIPS, r, c), sh.dtype), sh[:, None],
                                                  (0, me_chip, 0, 0))
        else:
            pool[name] = lax.dynamic_update_slice(lax.empty((N_CHIPS, L, r, c), sh.dtype), sh[None],
                                                  (me_chip, 0, 0, 0))
    return pool


def comm_now(pool, stages, name):
    stages = [Hosted(pool, jobs) for jobs in stages]
    names = sorted({m for st in stages for m in st.names})
    n = len(names)

    def body(*refs):
        bufs = dict(zip(names, refs[n:2 * n]))
        sems = refs[2 * n:]
        for i, st in enumerate(stages):
            plan = _hosted_plan(st, bufs, sems[2 * i], sems[2 * i + 1])
            _hosted_start(plan, True)
            _hosted_finish(plan, True)

    sem = pltpu.SemaphoreType.DMA
    scratch = []
    for st in stages:
        scratch += [sem((len(st.jobs), 3)), sem((len(st.jobs), 3))]
    res = pl.pallas_call(
        body, name=name, in_specs=[_ANY] * n, out_specs=[_ANY] * n,
        out_shape=[SDS(pool[m].shape, pool[m].dtype) for m in names],
        scratch_shapes=scratch, input_output_aliases={t: t for t in range(n)},
        compiler_params=_COMM,
    )(*[pool[m] for m in names])
    pool.update(zip(names, res))


def gather_now(pool, units):
    comm_now(pool, [[("ici", name, layer) for name, layer in units],
                    [("fwd", name, layer) for name, layer in units]], "gather_now")


def allgather_chips(buf, name):
    def body(src_ref, out_ref, send_sems, recv_sems, local_sem):
        x, y, c = _position()
        me = 2 * x + y
        mine = pltpu.make_async_copy(src_ref, out_ref.at[me], local_sem)
        mine.start()
        sends = []
        for k, (px, py) in enumerate(_other_chips(x, y)):
            cp = _remote(src_ref, out_ref.at[me], send_sems.at[k], recv_sems.at[k], (px, py, c))
            cp.start()
            sends.append(cp)
        for k, (px, py) in enumerate(_other_chips(x, y)):
            _remote(src_ref, out_ref.at[2 * px + py], send_sems.at[k], recv_sems.at[k], (px, py, c)).wait_recv()
        for cp in sends:
            cp.wait_send()
        mine.wait()

    sem = pltpu.SemaphoreType.DMA
    return pl.pallas_call(
        body, name=name, in_specs=[_ANY], out_specs=_ANY, out_shape=SDS((N_CHIPS,) + buf.shape, buf.dtype),
        scratch_shapes=[sem((3,)), sem((3,)), sem], compiler_params=_COMM,
    )(buf)


BIG = ("w_in", "w_proj_attn", "w_proj_pool", "w_proj_conv", "conv_w", "w_out", "w_gate_up", "w_down")
REPLICATED = ("attn_norm", "b_forget", "b_gate", "pool_w", "pool_scale", "ffn_norm", "final_norm")
ORDER = ("attn_norm", "w_in", "b_forget", "b_gate", "w_proj_attn", "pool_w", "pool_scale", "w_proj_pool",
         "conv_w", "w_proj_conv", "w_out", "ffn_norm", "w_gate_up", "w_down", "final_norm")


def _proj_layout(D):
    lay = {"g": 0, "q": 3 * D}
    lay["k"] = lay["q"] + BRANCH_W
    lay["f"] = lay["k"] + BRANCH_W
    lay["v"] = lay["f"] + F_PAD
    lay["pc"] = lay["v"] + BRANCH_W
    lay["width"] = lay["pc"] + 4 * BRANCH_W
    return lay


_REF = dict(q=0, k=512, v=1024, f=1536, u=1544, cv=2056, cb=2568, cc=3080, g=3592)


def _packed_pieces(D):
    pieces = [(_REF["g"], 3 * D), (_REF["q"], BRANCH_W), (_REF["k"], BRANCH_W), (_REF["f"], HEADS),
              (None, F_PAD - HEADS), (_REF["v"], BRANCH_W)]
    for gi in range(N_GROUPS):
        pieces += [(_REF[name] + gi * GROUP_W, GROUP_W) for name in ("u", "cv", "cb", "cc")]
    return pieces


def _packed_runs(D, cs):
    runs, at = [], 0
    for start, n in _packed_pieces(D):
        if start is None:
            runs.append((at, None, 0, n))
            at += n
        while start is not None and n:
            chip, off = divmod(start, cs)
            take = min(n, cs - off)
            runs.append((at, chip, off, take))
            at, start, n = at + take, start + take, n - take
    return runs


def pack_w_in(shards, layer):
    _, _, cs, D = shards.shape
    runs = _packed_runs(D, cs)
    width = runs[-1][0] + runs[-1][3]
    tc = _tile(D, (256, 128))

    def body(s_ref, o_ref):
        for dst, chip, off, rows in runs:
            if chip is None:
                o_ref[dst:dst + rows, :] = jnp.zeros((rows, tc), s_ref.dtype)
            else:
                o_ref[dst:dst + rows, :] = s_ref[chip, off:off + rows, :]

    return pl.pallas_call(
        body, name="pack_w_in", grid=(D // tc,),
        in_specs=[pl.BlockSpec((N_CHIPS, None, cs, tc), lambda j: (0, layer, 0, j))],
        out_specs=pl.BlockSpec((width, tc), lambda j: (0, j)),
        out_shape=SDS((width, D), shards.dtype), compiler_params=_params("arbitrary"),
    )(shards)


def unpack_w_in(p, cs):
    width, D = p.shape
    half = cs // 2
    runs = []
    for src, chip, off, rows in _packed_runs(D, cs):
        while chip is not None and rows:
            h, at = divmod(off, half)
            take = min(rows, half - at)
            runs.append((src, chip, h, at, take))
            src, off, rows = src + take, off + take, rows - take
    tc = _tile(D, (256, 128))

    def body(p_ref, o_ref):
        for src, chip, h, at, rows in runs:
            o_ref[chip, h, at:at + rows, :] = p_ref[src:src + rows, :]

    return pl.pallas_call(
        body, name="unpack_w_in", grid=(D // tc,),
        in_specs=[pl.BlockSpec((width, tc), lambda j: (0, j))],
        out_specs=pl.BlockSpec((N_CHIPS, 2, half, tc), lambda j: (0, 0, 0, j)),
        out_shape=SDS((N_CHIPS, 2, half, D), p.dtype), compiler_params=_params("arbitrary"),
    )(p)


def _split_flat(vec, shapes):
    out, at = [], 0
    for shp in shapes:
        n = int(np.prod(shp))
        out.append(vec[at:at + n].reshape(shp))
        at += n
    return out


def kernel(x, attn_norm, w_in, b_forget, b_gate, w_proj_attn, pool_w, pool_scale, w_proj_pool, conv_w, w_proj_conv, w_out, ffn_norm, w_gate_up, w_down, final_norm, loss_target, m_attn_norm, m_w_in, m_b_forget, m_b_gate, m_w_proj_attn, m_pool_w, m_pool_scale, m_w_proj_pool, m_conv_w, m_w_proj_conv, m_w_out, m_ffn_norm, m_w_gate_up, m_w_down, m_final_norm, v_attn_norm, v_w_in, v_b_forget, v_b_gate, v_w_proj_attn, v_pool_w, v_pool_scale, v_w_proj_pool, v_conv_w, v_w_proj_conv, v_w_out, v_ffn_norm, v_w_gate_up, v_w_down, v_final_norm):
    weights = dict(attn_norm=attn_norm, w_in=w_in, b_forget=b_forget, b_gate=b_gate, w_proj_attn=w_proj_attn,
                   pool_w=pool_w, pool_scale=pool_scale, w_proj_pool=w_proj_pool, conv_w=conv_w,
                   w_proj_conv=w_proj_conv, w_out=w_out, ffn_norm=ffn_norm, w_gate_up=w_gate_up, w_down=w_down,
                   final_norm=final_norm)
    mom_m = dict(attn_norm=m_attn_norm, w_in=m_w_in, b_forget=m_b_forget, b_gate=m_b_gate, w_proj_attn=m_w_proj_attn,
                 pool_w=m_pool_w, pool_scale=m_pool_scale, w_proj_pool=m_w_proj_pool, conv_w=m_conv_w,
                 w_proj_conv=m_w_proj_conv, w_out=m_w_out, ffn_norm=m_ffn_norm, w_gate_up=m_w_gate_up,
                 w_down=m_w_down, final_norm=m_final_norm)
    mom_v = dict(attn_norm=v_attn_norm, w_in=v_w_in, b_forget=v_b_forget, b_gate=v_b_gate, w_proj_attn=v_w_proj_attn,
                 pool_w=v_pool_w, pool_scale=v_pool_scale, w_proj_pool=v_w_proj_pool, conv_w=v_conv_w,
                 w_proj_conv=v_w_proj_conv, w_out=v_w_out, ffn_norm=v_ffn_norm, w_gate_up=v_w_gate_up,
                 w_down=v_w_down, final_norm=v_final_norm)

    Bl, S, D = x.shape
    T = Bl * S
    L = w_in.shape[0]
    F = w_down.shape[1] * N_CHIPS
    lay = _proj_layout(D)
    cst = _placement_constants()
    assert L == N_LAYERS and S % ATTN_BLOCK == 0 and F % (2 * LANES) == 0 and D % BRANCH_W == 0
    assert w_in.shape[2] * N_CHIPS == _REF["g"] + 3 * D and conv_w.shape[2] == GROUP_W

    send = {n: weights[n].astype(BF16) for n in BIG}
    send["conv_w"] = conv_w
    me_chip = 2 * lax.axis_index("x") + lax.axis_index("y")
    send["w_in"] = w_in.transpose(0, 2, 1).astype(BF16)
    pool = gather_buffers(send)
    gather_now(pool, [("w_in", 0)])
    rest = ("w_out", "w_proj_attn", "w_proj_pool", "w_gate_up", "w_proj_conv", "conv_w")
    late = ("w_out", "w_proj_attn", "w_proj_pool", "w_proj_conv", "conv_w")
    jobs = lambda kind, names, layer: [(kind, n, layer) for n in names]
    carried = {
        ("in_proj", 0): jobs("ici", rest, 0),
        ("attn_prep", 0): jobs("fwd", late, 0),
        ("attn_fwd", 0): jobs("fwd", ("w_gate_up",), 0) + jobs("ici", ("w_in",), 1) + jobs("ici", ("w_down",), 0),
        ("poolconv_fwd", 0): jobs("fwd", ("w_in",), 1) + jobs("fwd", ("w_down",), 0),
        ("mix_fwd", 0): jobs("ici", ("w_down",), 1),
        ("gate_up_proj", 0): jobs("ici", late, 1) + jobs("fwd", ("w_down",), 1),
        ("ffn_down_fwd", 0): jobs("ici", ("w_gate_up",), 1),
        ("in_proj", 1): jobs("fwd", ("w_gate_up",) + late, 1),
    }
    carry = lambda call, layer: Hosted(pool, carried[call, layer]) if (call, layer) in carried else None
    w_down_f = lambda: pool["w_down"].reshape(L, F, D)
    pool_w_b = pool_w.astype(BF16)
    an3, fn3 = attn_norm.reshape(L, 1, D), ffn_norm.reshape(L, 1, D)
    bg3, ps3 = b_gate.reshape(L, 1, 3 * D), pool_scale.reshape(L, 1, BRANCH_W)
    bf3 = jnp.pad(b_forget, ((0, 0), (0, LANES - HEADS))).reshape(L, 1, LANES)

    xs = x.reshape(T, D)
    saved = []
    w_in_p = []
    for l in range(L):
        w_in_p.append(pack_w_in(pool["w_in"], l))
        proj, h = norm_matmul(xs, an3, w_in_p[l], l, "rows", "in_proj", carry("in_proj", l))
        proj3 = proj.reshape(Bl, S, lay["width"])
        qa, ka = attn_prep(proj3, bf3, l, cst, lay, carry("attn_prep", l))
        ao, lse = attn_fwd(qa, ka, proj3, lay, carry("attn_fwd", l))
        po, co = poolconv_fwd(proj3, pool_w_b, ps3, pool["conv_w"], l, lay, carry("poolconv_fwd", l))
        ao2, po2, co2 = (a.reshape(T, BRANCH_W) for a in (ao, po, co))
        x1, ys, mixed = mix_fwd(ao2, po2, co2, proj, bg3, pool["w_proj_attn"], pool["w_proj_pool"],
                                pool["w_proj_conv"], pool["w_out"], l, xs, carry("mix_fwd", l))
        ab, h2 = norm_matmul(x1, fn3, pool["w_gate_up"], l, "by_shard", "gate_up_proj", carry("gate_up_proj", l))
        x2, s_act = ffn_down_fwd(ab, w_down_f(), l, x1, carry("ffn_down_fwd", l))
        saved.append(dict(x=xs, proj=proj, proj3=proj3, h=h, qa=qa, ka=ka, ao=ao, lse=lse, ao2=ao2, po2=po2,
                          co2=co2, ys=ys, mixed=mixed, x1=x1, ab=ab, h2=h2, s=s_act))
        xs = x2
    w_gu, w_o, conv_w_g = pool["w_gate_up"], pool["w_out"], pool["conv_w"]
    wpa, wpp, wpc = pool["w_proj_attn"], pool["w_proj_pool"], pool["w_proj_conv"]
    w_down_f = w_down_f()

    loss_row, dx, dxb, g_final = loss_head(xs, final_norm.reshape(1, D), loss_target.reshape(T, D))
    loss = lax.psum(loss_row[0, 0], AXES)

    reduced_names = tuple(n for n in BIG if n != "conv_w")
    early_names = tuple(n for n in reduced_names if n != "w_in")
    proj_names = ("w_out", "w_proj_attn", "w_proj_pool", "w_proj_conv")
    first_names = ("w_in", "w_gate_up", "w_down")
    where = jnp.stack([lax.axis_index("c"), me_chip]).astype(jnp.int32)
    rs = {}

    def reduce_begin(layer, grads):
        for n, g in grads.items():
            g5 = g.reshape((1, N_CHIPS, 2, -1) + g.shape[-1:])
            rs["g%d:%s" % (layer, n)] = g5
            for role in "ra":
                rs["%s%d:%s" % (role, layer, n)] = lax.empty((N_CHIPS,) + g5.shape[3:], BF16)

    swap_jobs = lambda layer, names: [("swap", "g%d:%s" % (layer, n), "r%d:%s" % (layer, n), 0) for n in names]
    xchg_jobs = lambda layer, names: [("xchg", "s%d:%s" % (layer, n), "a%d:%s" % (layer, n)) for n in names]
    join_jobs = lambda layer, names: [("join", "o:" + n, layer) for n in names]

    def pair_sums(layer, names):
        for n in names:
            rs["s%d:%s" % (layer, n)] = add_pair(rs["g%d:%s" % (layer, n)], 0, where, rs["r%d:%s" % (layer, n)],
                                                 "add_pair_" + n)

    def chip_sums(layer, names, slot, n_slots):
        for n in names:
            rs["o:" + n] = add_chips(rs["a%d:%s" % (layer, n)], rs["s%d:%s" % (layer, n)], slot, where, n_slots,
                                     rs.get("o:" + n), "add_chips_" + n)

    small = {n: [None] * L for n in REPLICATED if n != "final_norm"}
    g_conv = [None] * L
    to3 = lambda a: a.reshape(Bl, S, -1)
    for l in reversed(range(L)):
        sv = saved[l]
        behind = (lambda jobs: Hosted(rs, jobs)) if l == 0 else (lambda jobs: None)
        grads = {}
        da, db = ffn_down_bwd(dxb, w_down_f, l, sv["ab"], behind(swap_jobs(1, reduced_names)))
        if l == 0:
            pair_sums(1, reduced_names)
        grads["w_down"] = matmul_tn(sv["s"], [dxb], "grad_w_down", hosted=behind(xchg_jobs(1, ("w_down",))))
        grads["w_gate_up"] = matmul_tn(sv["h2"], [da, db], "grad_w_gate_up", by_dest=True, tn=2 * F // N_CHIPS,
                                       tk=_tile(T, (1024, 512, 256)), hosted=behind(xchg_jobs(1, ("w_gate_up",))))
        dx1, dx1b, g_fn = matmul_nt_normbwd([da, db], w_gu, l, "by_shard", sv["x1"], fn3, dx, "gate_up_bwd",
                                            behind(xchg_jobs(1, ("w_in",))))
        small["ffn_norm"][l] = g_fn[0]
        if l == 0:
            chip_sums(1, first_names, 1, L)
        dys, dproj, dao, dpo, dco, g_bg = mix_bwd(dx1b, w_o, sv["proj"], bg3, sv["ys"], wpa, wpp, wpc, l,
                                                  lay["width"],
                                                  behind(xchg_jobs(1, proj_names) + join_jobs(1, first_names)))
        if l == 0:
            chip_sums(1, proj_names, 1, L)
        small["b_gate"][l] = g_bg[0]
        grads["w_out"] = matmul_tn(sv["mixed"], [dx1b], "grad_w_out")
        for n, (name, br) in enumerate((("w_proj_attn", sv["ao2"]), ("w_proj_pool", sv["po2"]),
                                        ("w_proj_conv", sv["co2"]))):
            grads[name] = matmul_tn(br, [dys], "grad_" + name, b_col0=n * D, n_cols=D, by_dest=True,
                                    tn=D // N_CHIPS)
        if l == 0:
            reduce_begin(0, grads)
        dqa, dka, dproj3 = attn_bwd(sv["qa"], sv["ka"], sv["proj3"], to3(dao), sv["ao"], sv["lse"], to3(dproj), lay,
                                    behind(swap_jobs(0, early_names) + join_jobs(1, proj_names)))
        if l == 0:
            pair_sums(0, early_names)
        dproj3, g_bf = attn_post(dqa, dka, sv["proj3"], bf3, l, dproj3, cst, lay, behind(xchg_jobs(
            0, ("w_out", "w_proj_attn", "w_proj_pool", "w_proj_conv"))))
        small["b_forget"][l] = g_bf[0, :HEADS]
        dproj3, g_pw, g_ps, g_conv[l] = poolconv_bwd(sv["proj3"], to3(dpo), to3(dco), pool_w_b, ps3, conv_w_g, l,
                                                     dproj3, lay, behind(xchg_jobs(0, ("w_down",))))
        small["pool_w"][l], small["pool_scale"][l] = g_pw, g_ps[0]
        dproj = dproj3.reshape(T, lay["width"])
        g_w_in = unpack_w_in(matmul_tn(dproj, [sv["h"]], "grad_w_in", tn=_tile(D, (1024, 512)), hosted=behind(xchg_jobs(
            0, ("w_gate_up",)))), w_in.shape[2])
        if l:
            reduce_begin(l, {**grads, "w_in": g_w_in})
        else:
            reduce_begin(0, {"w_in": g_w_in})
            comm_now(rs, [swap_jobs(0, ("w_in",))], "swap_w_in_halves")
            pair_sums(0, ("w_in",))
        dx, dxb, g_an = matmul_nt_normbwd([dproj], w_in_p[l], l, "rows", sv["x"], an3, dx1, "in_proj_bwd",
                                          behind(xchg_jobs(0, ("w_in",))))
        small["attn_norm"][l] = g_an[0]
    grad_x = dx.reshape(Bl, S, D)

    small_shapes = [weights[n].shape for n in REPLICATED] + [(L, N_CHIPS) + conv_w.shape[1:]]
    small_vec = jnp.concatenate([jnp.stack(small[n]).reshape(-1) for n in REPLICATED[:-1]]
                                + [g_final[0], jnp.stack(g_conv).reshape(-1)])
    n_small = small_vec.shape[0]
    small_vec = jnp.pad(small_vec, (0, -n_small % (2 * N_CHIPS * 16 * LANES))).astype(BF16)
    rs["g0:small"] = small_vec.reshape(1, N_CHIPS, 2, -1, LANES)
    for role in "ra":
        rs[role + "0:small"] = lax.empty((N_CHIPS,) + rs["g0:small"].shape[3:], BF16)
    last = ("small",)
    comm_now(rs, [swap_jobs(0, last)], "swap_grad_halves")
    pair_sums(0, last)
    comm_now(rs, [xchg_jobs(0, last)], "exchange_grad_chips")
    chip_sums(0, reduced_names, 0, L)
    chip_sums(0, ("small",), 0, 1)
    comm_now(rs, [join_jobs(0, reduced_names + ("small",))], "join_grad_halves")
    shard_grads = {n: rs["o:" + n].reshape((L, -1) + rs["o:" + n].shape[-1:]) for n in reduced_names}
    small_all = allgather_chips(rs["o:small"].reshape(-1, LANES), "allgather_small_grads").reshape(-1)[:n_small]
    *rep_list, conv_all = _split_flat(small_all, small_shapes)
    rep_grads = dict(zip(REPLICATED, rep_list))
    shard_grads["conv_w"] = lax.dynamic_index_in_dim(conv_all, me_chip, 1, keepdims=False)

    delta, new_m, new_v = {}, {}, {}
    for n in BIG:
        shp = weights[n].shape
        if n == "w_in":
            view, back = (lambda a: a.transpose(2, 0, 1)), (lambda a: a.transpose(1, 2, 0))
            g = shard_grads[n].transpose(1, 0, 2)
        else:
            view, back = (lambda a: a.reshape(-1, shp[-1])), (lambda a: a.reshape(shp))
            g = view(shard_grads[n])
        d, nm, nv = adamw(view(weights[n]), g, view(mom_m[n]), view(mom_v[n]), "adamw_" + n)
        delta[n], new_m[n], new_v[n], shard_grads[n] = back(d), back(nm), back(nv), back(g)

    def rows(d):
        vec = jnp.concatenate([d[n].reshape(-1) for n in REPLICATED])
        return jnp.pad(vec, (0, -vec.shape[0] % (8 * LANES))).reshape(-1, LANES)

    outs = adamw(rows(weights), rows(rep_grads), rows(mom_m), rows(mom_v), "adamw_replicated")
    for res, o in zip((delta, new_m, new_v), outs):
        res.update(zip(REPLICATED, _split_flat(o.reshape(-1), small_shapes[:len(REPLICATED)])))
    all_grads = {**shard_grads, **rep_grads}

    return (loss, grad_x, *[all_grads[n] for n in ORDER], *[delta[n] for n in ORDER],
            *[new_m[n] for n in ORDER], *[new_v[n] for n in ORDER])
```

```python
import numpy as np
import jax
import jax.numpy as jnp
from jax import lax
from jax.experimental import pallas as pl
from jax.experimental.pallas import tpu as pltpu

F32, BF16 = jnp.float32, jnp.bfloat16
SDS = jax.ShapeDtypeStruct
MESH = pl.DeviceIdType.MESH
AXES = ("x", "y", "c")
N_CHIPS = 4
N_LAYERS = 2
LANES = 128
VMEM_LIMIT = 48 * 1024 * 1024

HEADS, HEAD_DIM = 8, 64
HEAD_PAD = 128
BRANCH_W = 512
GROUP_W = 128
N_GROUPS = BRANCH_W // GROUP_W
POOL_WINDOWS = (2, 4, 8, 16)
F_PAD = 512
ATTN_BLOCK = 512
RMS_EPS = 1e-6
NEG_INF = -1e30
ADAM_LR, ADAM_B1, ADAM_B2, ADAM_EPS, ADAM_WD, ADAM_STEP = 0.001, 0.9, 0.999, 1e-08, 0.01, 10

NT = (((1,), (1,)), ((), ()))
TN = (((0,), (0,)), ((), ()))
_ANY = pl.BlockSpec(memory_space=pl.ANY)


def _tile(n, prefs):
    for p in prefs:
        if n % p == 0:
            return p
    raise ValueError(f"no tile of {prefs} divides {n}")


def _params(*sem):
    return pltpu.CompilerParams(dimension_semantics=sem, vmem_limit_bytes=VMEM_LIMIT)


def _sigmoid(z):
    return 0.5 * jnp.tanh(0.5 * z) + 0.5


def _split3(x):
    h1 = x.astype(BF16)
    r1 = x - h1.astype(F32)
    h2 = r1.astype(BF16)
    h3 = (r1 - h2.astype(F32)).astype(BF16)
    return h1, h2, h3


def _position():
    return lax.axis_index("x"), lax.axis_index("y"), lax.axis_index("c")


def _other_chips(x, y):
    return [(1 - x, y), (x, 1 - y), (1 - x, 1 - y)]


def _remote(src, dst, send_sem, recv_sem, device):
    return pltpu.make_async_remote_copy(src_ref=src, dst_ref=dst, send_sem=send_sem, recv_sem=recv_sem,
                                        device_id=device, device_id_type=MESH)


ROW_SHARDED = ("w_out", "w_down")
FETCHER = dict(w_in=0, w_out=0, w_proj_attn=0, w_proj_pool=0, w_gate_up=1, w_down=1, w_proj_conv=1, conv_w=1)


class Hosted:
    def __init__(self, pool, jobs):
        self.pool, self.jobs = pool, list(jobs)
        names = set()
        for job in self.jobs:
            names.update(job[1:3] if job[0] in ("swap", "xchg") else job[1:2])
        self.names = sorted(names)


def _hosted_plan(hosted, refs, send_sems, recv_sems):
    x, y, c = _position()
    me = 2 * x + y
    others = _other_chips(x, y)
    sibling = (x, y, 1 - c)
    plan = []
    for j, job in enumerate(hosted.jobs):
        kind = job[0]
        sems = lambda k, j=j: (send_sems.at[j, k], recv_sems.at[j, k])
        if kind in ("ici", "fwd"):
            _, name, layer = job
            ref = refs[name]
            win = (lambda chip, ref=ref, layer=layer: ref.at[layer, chip]) if name in ROW_SHARDED else (
                lambda chip, ref=ref, layer=layer: ref.at[chip, layer])
            mine = c == FETCHER[name]
            if kind == "ici":
                sends = [_remote(win(me), win(me), *sems(k), (px, py, c)) for k, (px, py) in enumerate(others)]
                arrivals = [_remote(win(2 * px + py), win(2 * px + py), *sems(k), (px, py, c))
                            for k, (px, py) in enumerate(others)]
                plan.append((mine, sends, arrivals, []))
            else:
                sends = [_remote(win(2 * px + py), win(2 * px + py), *sems(k), sibling)
                         for k, (px, py) in enumerate(others)]
                plan.append((mine, sends, [], sends))
        elif kind == "swap":
            _, src, dst, layer = job
            cp = _remote(refs[src].at[layer, :, 1 - c], refs[dst], *sems(0), sibling)
            plan.append((True, [cp], [cp], []))
        elif kind == "xchg":
            _, src, dst = job
            sends = [_remote(refs[src].at[2 * px + py], refs[dst].at[me], *sems(k), (px, py, c))
                     for k, (px, py) in enumerate(others)]
            arrivals = [_remote(refs[src].at[me], refs[dst].at[2 * px + py], *sems(k), (px, py, c))
                        for k, (px, py) in enumerate(others)]
            plan.append((True, sends, arrivals, []))
        else:
            _, name, layer = job
            ref = refs[name]
            cp = _remote(ref.at[layer, c], ref.at[layer, c], *sems(0), sibling)
            arrival = _remote(ref.at[layer, c], ref.at[layer, 1 - c], *sems(0), sibling)
            plan.append((True, [cp], [arrival], []))
    return plan


def _hosted_start(plan, now):
    for mine, sends, _, _ in plan:
        @pl.when(now & mine)
        def _(sends=sends):
            for cp in sends:
                cp.start()


def _hosted_finish(plan, now):
    for mine, sends, arrivals, sibling_arrivals in plan:
        @pl.when(now & mine)
        def _(sends=sends, arrivals=arrivals):
            for cp in arrivals:
                cp.wait_recv()
            for cp in sends:
                cp.wait_send()

        if sibling_arrivals:
            @pl.when(now & jnp.logical_not(mine))
            def _(sibling_arrivals=sibling_arrivals):
                for cp in sibling_arrivals:
                    cp.wait_recv()


def _pcall(body, hosted, *, name, grid, in_specs, out_specs, out_shape, semantics, scratch_shapes=(), aliases=None):
    aliases = dict(aliases or {})
    if hosted is None or not hosted.jobs:
        return pl.pallas_call(body, name=name, grid=grid, in_specs=in_specs, out_specs=out_specs,
                              out_shape=out_shape, scratch_shapes=list(scratch_shapes),
                              input_output_aliases=aliases, compiler_params=_params(*semantics))
    single = not isinstance(out_shape, (list, tuple))
    out_specs_l = [out_specs] if single else list(out_specs)
    out_shape_l = [out_shape] if single else list(out_shape)
    n_in, n_out, n_buf, n_job = len(in_specs), len(out_specs_l), len(hosted.names), len(hosted.jobs)

    def carrying(*refs):
        ins, outs = refs[:n_in], refs[n_in + n_buf:n_in + n_buf + n_out]
        bufs = refs[n_in + n_buf + n_out:n_in + 2 * n_buf + n_out]
        rest = refs[n_in + 2 * n_buf + n_out:]
        scratch, send_sems, recv_sems = rest[:-2], rest[-2], rest[-1]
        first, last = True, True
        for axis, size in enumerate(grid):
            first = first & (pl.program_id(axis) == 0)
            last = last & (pl.program_id(axis) == size - 1)
        plan = _hosted_plan(hosted, dict(zip(hosted.names, bufs)), send_sems, recv_sems)
        _hosted_start(plan, first)
        body(*ins, *outs, *scratch)
        _hosted_finish(plan, last)

    def run(*args):
        bufs = [hosted.pool[n] for n in hosted.names]
        sem = pltpu.SemaphoreType.DMA
        res = pl.pallas_call(
            carrying, name=name, grid=grid, in_specs=list(in_specs) + [_ANY] * n_buf,
            out_specs=out_specs_l + [_ANY] * n_buf,
            out_shape=out_shape_l + [SDS(b.shape, b.dtype) for b in bufs],
            scratch_shapes=list(scratch_shapes) + [sem((n_job, 3)), sem((n_job, 3))],
            input_output_aliases={**aliases, **{n_in + i: n_out + i for i in range(n_buf)}},
            compiler_params=pltpu.CompilerParams(dimension_semantics=semantics, vmem_limit_bytes=VMEM_LIMIT,
                                                 has_side_effects=True),
        )(*args, *bufs)
        hosted.pool.update(zip(hosted.names, res[n_out:]))
        return res[0] if single else res[:n_out]

    return run


def _dot(a, b):
    return jnp.dot(a, b, preferred_element_type=F32)


def _dot_nt(a, b):
    return lax.dot_general(a, b, NT, preferred_element_type=F32)


def _dot_tn(a, b):
    return lax.dot_general(a, b, TN, preferred_element_type=F32)


def norm_matmul(x, gain, w, layer, kind, name, hosted=None):
    T, D = x.shape
    if kind == "by_shard":
        tn = w.shape[3]
        N = N_CHIPS * tn
        w_spec = pl.BlockSpec((None, None, D, tn), lambda i, j: (j, layer, 0, 0))
        mm = _dot
    else:
        N = w.shape[0]
        tn = _tile(N, (1024, 512, 256, 128))
        w_spec = pl.BlockSpec((tn, D), lambda i, j: (j, 0))
        mm = _dot_nt
    tm = _tile(T, (2048, 1024, 512, 256, 128) if kind == "rows" else (1024, 512, 256, 128))

    def body(x_ref, g_ref, w_ref, y_ref, h_ref):
        @pl.when(pl.program_id(1) == 0)
        def _():
            xf = x_ref[...]
            r = lax.rsqrt(jnp.mean(xf * xf, axis=-1, keepdims=True) + RMS_EPS)
            h_ref[...] = ((xf * r) * g_ref[...]).astype(BF16)

        y_ref[...] = mm(h_ref[...], w_ref[...]).astype(BF16)

    return _pcall(
        body, hosted, name=name, grid=(T // tm, N // tn),
        in_specs=[pl.BlockSpec((tm, D), lambda i, j: (i, 0)),
                  pl.BlockSpec((None, 1, D), lambda i, j: (layer, 0, 0)),
                  w_spec],
        out_specs=[pl.BlockSpec((tm, tn), lambda i, j: (i, j)),
                   pl.BlockSpec((tm, D), lambda i, j: (i, 0))],
        out_shape=[SDS((T, N), BF16), SDS((T, D), BF16)],
        semantics=("arbitrary", "arbitrary"),
    )(x, gain, w)


def matmul_nt_normbwd(dys, w, layer, kind, x, gain, dres, name, hosted=None):
    T, D = x.shape
    width = dys[0].shape[1]
    if kind == "by_shard":
        tk = w.shape[3]
        w_spec = pl.BlockSpec((None, None, D, tk), lambda i, k: (k, layer, 0, 0))
        mm = _dot_nt
    else:
        tk = _tile(width, (3584, 1024, 512, 256, 128))
        w_spec = pl.BlockSpec((tk, D), lambda i, k: (k, 0))
        mm = _dot
    per = width // tk
    nk = per * len(dys)
    tm = _tile(T, (512, 256, 128))
    n_dy = len(dys)

    def dy_spec(p):
        return pl.BlockSpec((tm, tk), lambda i, k: (i, jnp.clip(k - p * per, 0, per - 1)))

    def body(*refs):
        dy_refs = refs[:n_dy]
        w_ref, x_ref, g_ref, dres_ref, dx_ref, dxb_ref, dg_ref, acc_ref = refs[n_dy:]
        i, k = pl.program_id(0), pl.program_id(1)

        @pl.when(k == 0)
        def _():
            acc_ref[...] = jnp.zeros_like(acc_ref)

        for p in range(n_dy):
            @pl.when((k >= p * per) & (k < (p + 1) * per))
            def _(p=p):
                acc_ref[...] += mm(dy_refs[p][...], w_ref[...])

        @pl.when(k == nk - 1)
        def _():
            xf = x_ref[...]
            r = lax.rsqrt(jnp.mean(xf * xf, axis=-1, keepdims=True) + RMS_EPS)
            xhat = xf * r
            dh = acc_ref[...]
            dhg = dh * g_ref[...]
            dx = dres_ref[...] + r * (dhg - xhat * jnp.mean(dhg * xhat, axis=-1, keepdims=True))
            dx_ref[...] = dx
            dxb_ref[...] = dx.astype(BF16)
            part = jnp.sum(dh * xhat, axis=0, keepdims=True)

            @pl.when(i == 0)
            def _():
                dg_ref[...] = part

            @pl.when(i > 0)
            def _():
                dg_ref[...] += part

    row = pl.BlockSpec((tm, D), lambda i, k: (i, 0))
    return _pcall(
        body, hosted, name=name, grid=(T // tm, nk),
        in_specs=[dy_spec(p) for p in range(n_dy)] + [
            w_spec, row, pl.BlockSpec((None, 1, D), lambda i, k: (layer, 0, 0)), row],
        out_specs=[row, row, pl.BlockSpec((1, D), lambda i, k: (0, 0))],
        out_shape=[SDS((T, D), F32), SDS((T, D), BF16), SDS((1, D), F32)],
        scratch_shapes=[pltpu.VMEM((tm, D), F32)],
        semantics=("arbitrary", "arbitrary"),
    )(*dys, w, x, gain, dres)


def matmul_tn(a, bs, name, b_col0=0, n_cols=None, by_dest=False, tn=None, tk=None, hosted=None):
    T, M = a.shape
    width = bs[0].shape[1]
    N = n_cols if n_cols else width * len(bs)
    tm = _tile(M, (1408, 1024, 512, 256, 128))
    tn = tn or _tile(N, (512, 256, 128))
    tk = tk or _tile(T, (4096, 2048, 1024, 512, 256))
    assert b_col0 % tn == 0 and width % tn == 0
    j0, per, nk, n_b = b_col0 // tn, width // tn, T // tk, len(bs)

    def b_spec(p):
        return pl.BlockSpec((tk, tn), lambda i, j, k: (k, jnp.clip(j0 + j - p * per, 0, per - 1)))

    def body(*refs):
        a_ref, b_refs = refs[0], refs[1:1 + n_b]
        o_ref, acc_ref = refs[-2], refs[-1]
        j, k = pl.program_id(1), pl.program_id(2)

        @pl.when(k == 0)
        def _():
            acc_ref[...] = jnp.zeros_like(acc_ref)

        for p in range(n_b):
            @pl.when((j0 + j >= p * per) & (j0 + j < (p + 1) * per))
            def _(p=p):
                acc_ref[...] += _dot_tn(a_ref[...], b_refs[p][...])

        @pl.when(k == nk - 1)
        def _():
            o_ref[...] = acc_ref[...].astype(BF16)

    if by_dest:
        cs = N // N_CHIPS
        npd = cs // tn
        out_shape = SDS((N_CHIPS, M, cs), BF16)
        out_spec = pl.BlockSpec((None, tm, tn), lambda i, j, k: (j // npd, i, j % npd))
    else:
        out_shape = SDS((M, N), BF16)
        out_spec = pl.BlockSpec((tm, tn), lambda i, j, k: (i, j))
    return _pcall(
        body, hosted, name=name, grid=(M // tm, N // tn, nk),
        in_specs=[pl.BlockSpec((tk, tm), lambda i, j, k: (k, i))] + [b_spec(p) for p in range(n_b)],
        out_specs=out_spec, out_shape=out_shape,
        scratch_shapes=[pltpu.VMEM((tm, tn), F32)],
        semantics=("arbitrary", "arbitrary", "arbitrary"),
    )(a, *bs)


def ffn_down_fwd(ab, w_down, layer, x1, hosted=None):
    T, D = x1.shape
    F = w_down.shape[1]
    tm = _tile(T, (512, 256, 128))
    tk = F // 2
    nk = F // tk

    def body(a_ref, b_ref, w_ref, x_ref, x2_ref, s_ref, acc_ref):
        k = pl.program_id(1)

        @pl.when(k == 0)
        def _():
            acc_ref[...] = x_ref[...]

        a = a_ref[...].astype(F32)
        s = (a * _sigmoid(a) * b_ref[...].astype(F32)).astype(BF16)
        s_ref[...] = s
        acc_ref[...] += _dot(s, w_ref[...])

        @pl.when(k == nk - 1)
        def _():
            x2_ref[...] = acc_ref[...]

    return _pcall(
        body, hosted, name="ffn_down_fwd", grid=(T // tm, nk),
        in_specs=[pl.BlockSpec((tm, tk), lambda i, k: (i, k)),
                  pl.BlockSpec((tm, tk), lambda i, k: (i, nk + k)),
                  pl.BlockSpec((None, tk, D), lambda i, k: (layer, k, 0)),
                  pl.BlockSpec((tm, D), lambda i, k: (i, 0))],
        out_specs=[pl.BlockSpec((tm, D), lambda i, k: (i, 0)),
                   pl.BlockSpec((tm, tk), lambda i, k: (i, k))],
        out_shape=[SDS((T, D), F32), SDS((T, F), BF16)],
        scratch_shapes=[pltpu.VMEM((tm, D), F32)],
        semantics=("arbitrary", "arbitrary"),
    )(ab, ab, w_down, x1)


def ffn_down_bwd(dx2b, w_down, layer, ab, hosted=None):
    T, D = dx2b.shape
    F = w_down.shape[1]
    tm = _tile(T, (512, 256, 128))
    tn = F // 2
    nj = F // tn

    def body(dx_ref, w_ref, a_ref, b_ref, da_ref, db_ref):
        ds = _dot_nt(dx_ref[...], w_ref[...])
        a = a_ref[...].astype(F32)
        sg = _sigmoid(a)
        da_ref[...] = (ds * b_ref[...].astype(F32) * (sg * (1.0 + a * (1.0 - sg)))).astype(BF16)
        db_ref[...] = (ds * (a * sg)).astype(BF16)

    blk = pl.BlockSpec((tm, tn), lambda j, i: (i, j))
    return _pcall(
        body, hosted, name="ffn_down_bwd", grid=(nj, T // tm),
        in_specs=[pl.BlockSpec((tm, D), lambda j, i: (i, 0)),
                  pl.BlockSpec((None, tn, D), lambda j, i: (layer, j, 0)),
                  blk, pl.BlockSpec((tm, tn), lambda j, i: (i, nj + j))],
        out_specs=[blk, blk],
        out_shape=[SDS((T, F), BF16), SDS((T, F), BF16)],
        semantics=("arbitrary", "arbitrary"),
    )(dx2b, w_down, ab, ab)


def _mix_specs(tm, D, layer):
    cs = D // N_CHIPS
    row = lambda w: pl.BlockSpec((tm, w), lambda i: (i, 0))
    wp = pl.BlockSpec((N_CHIPS, None, BRANCH_W, cs), lambda i: (0, layer, 0, 0))
    wo = pl.BlockSpec((None, N_CHIPS, cs, D), lambda i: (layer, 0, 0, 0))
    bg = pl.BlockSpec((None, 1, 3 * D), lambda i: (layer, 0, 0))
    return row, wp, wo, bg


def mix_fwd(ao, po, co, proj, b_gate, wpa, wpp, wpc, w_out, layer, x, hosted=None):
    T, D = x.shape
    cs = D // N_CHIPS
    tm = _tile(T, (256, 128))
    row, wp, wo, bg = _mix_specs(tm, D, layer)

    def body(ao_ref, po_ref, co_ref, g_ref, bg_ref, wpa_ref, wpp_ref, wpc_ref, wo_ref, x_ref,
             x1_ref, ys_ref, mixed_ref):
        mixed = jnp.zeros((tm, D), F32)
        for n, (br, wp_ref) in enumerate(((ao_ref, wpa_ref), (po_ref, wpp_ref), (co_ref, wpc_ref))):
            y = jnp.concatenate([_dot(br[...], wp_ref[j]) for j in range(N_CHIPS)], axis=1)
            cols = slice(n * D, (n + 1) * D)
            gate = _sigmoid(g_ref[:, cols].astype(F32) + bg_ref[:, cols])
            ys_ref[:, cols] = y.astype(BF16)
            mixed = mixed + gate * y
        mb = mixed.astype(BF16)
        mixed_ref[...] = mb
        acc = x_ref[...]
        for j in range(N_CHIPS):
            acc = acc + _dot(mb[:, j * cs:(j + 1) * cs], wo_ref[j])
        x1_ref[...] = acc

    return _pcall(
        body, hosted, name="mix_fwd", grid=(T // tm,),
        in_specs=[row(BRANCH_W), row(BRANCH_W), row(BRANCH_W), row(3 * D), bg, wp, wp, wp, wo, row(D)],
        out_specs=[row(D), row(3 * D), row(D)],
        out_shape=[SDS((T, D), F32), SDS((T, 3 * D), BF16), SDS((T, D), BF16)],
        semantics=("arbitrary",),
    )(ao, po, co, proj, b_gate, wpa, wpp, wpc, w_out, x)


def mix_bwd(dx1b, w_out, proj, b_gate, ys, wpa, wpp, wpc, layer, width, hosted=None):
    T, D = dx1b.shape
    cs = D // N_CHIPS
    tm = _tile(T, (256, 128))
    row, wp, wo, bg = _mix_specs(tm, D, layer)

    def body(dx_ref, wo_ref, g_ref, bg_ref, ys_ref, wpa_ref, wpp_ref, wpc_ref,
             dys_ref, dg_ref, dao_ref, dpo_ref, dco_ref, dbg_ref):
        i = pl.program_id(0)
        dx = dx_ref[...]
        dmixed = jnp.concatenate([_dot_nt(dx, wo_ref[j]) for j in range(N_CHIPS)], axis=1)
        for n, (wp_ref, dbr) in enumerate(((wpa_ref, dao_ref), (wpp_ref, dpo_ref), (wpc_ref, dco_ref))):
            cols = slice(n * D, (n + 1) * D)
            gate = _sigmoid(g_ref[:, cols].astype(F32) + bg_ref[:, cols])
            dy = (dmixed * gate).astype(BF16)
            dys_ref[:, cols] = dy
            dgp = dmixed * ys_ref[:, cols].astype(F32) * gate * (1.0 - gate)
            dg_ref[:, cols] = dgp.astype(BF16)
            part = jnp.sum(dgp, axis=0, keepdims=True)

            @pl.when(i == 0)
            def _():
                dbg_ref[:, cols] = part

            @pl.when(i > 0)
            def _():
                dbg_ref[:, cols] += part

            acc = jnp.zeros((tm, BRANCH_W), F32)
            for j in range(N_CHIPS):
                acc = acc + _dot_nt(dy[:, j * cs:(j + 1) * cs], wp_ref[j])
            dbr[...] = acc.astype(BF16)

    return _pcall(
        body, hosted, name="mix_bwd", grid=(T // tm,),
        in_specs=[row(D), wo, row(3 * D), bg, row(3 * D), wp, wp, wp],
        out_specs=[row(3 * D), row(3 * D), row(BRANCH_W), row(BRANCH_W), row(BRANCH_W),
                   pl.BlockSpec((1, 3 * D), lambda i: (0, 0))],
        out_shape=[SDS((T, 3 * D), BF16), SDS((T, width), BF16), SDS((T, BRANCH_W), BF16),
                   SDS((T, BRANCH_W), BF16), SDS((T, BRANCH_W), BF16), SDS((1, 3 * D), F32)],
        semantics=("arbitrary",),
    )(dx1b, w_out, proj, b_gate, ys, wpa, wpp, wpc)


def loss_head(x2, gain, target):
    T, D = x2.shape
    tm = _tile(T, (512, 256, 128))

    def body(x_ref, g_ref, t_ref, loss_ref, dx_ref, dxb_ref, dg_ref):
        i = pl.program_id(0)
        xf = x_ref[...]
        g = g_ref[...]
        r = lax.rsqrt(jnp.mean(xf * xf, axis=-1, keepdims=True) + RMS_EPS)
        xhat = xf * r
        diff = xhat * g - t_ref[...]
        part_loss = 0.5 * jnp.sum(jnp.mean(diff * diff, axis=-1, keepdims=True), axis=0, keepdims=True)
        dy = diff * (1.0 / D)
        dhg = dy * g
        dx = r * (dhg - xhat * jnp.mean(dhg * xhat, axis=-1, keepdims=True))
        dx_ref[...] = dx
        dxb_ref[...] = dx.astype(BF16)
        part_g = jnp.sum(dy * xhat, axis=0, keepdims=True)
        part_l = jnp.broadcast_to(part_loss, (1, LANES))

        @pl.when(i == 0)
        def _():
            dg_ref[...] = part_g
            loss_ref[...] = part_l

        @pl.when(i > 0)
        def _():
            dg_ref[...] += part_g
            loss_ref[...] += part_l

    row = pl.BlockSpec((tm, D), lambda i: (i, 0))
    return pl.pallas_call(
        body, name="loss_head", grid=(T // tm,),
        in_specs=[row, pl.BlockSpec((1, D), lambda i: (0, 0)), row],
        out_specs=[pl.BlockSpec((1, LANES), lambda i: (0, 0)), row, row, pl.BlockSpec((1, D), lambda i: (0, 0))],
        out_shape=[SDS((1, LANES), F32), SDS((T, D), F32), SDS((T, D), BF16), SDS((1, D), F32)],
        compiler_params=_params("arbitrary"),
    )(x2, gain, target)


def _placement_constants():
    w = HEADS * HEAD_PAD
    pq = np.zeros((BRANCH_W, w), np.float32)
    pk = np.zeros((BRANCH_W, w), np.float32)
    pfq = np.zeros((3, LANES, w), np.float32)
    pfk = np.zeros((3, LANES, w), np.float32)
    cq = np.zeros((1, w), np.float32)
    ck = np.zeros((1, w), np.float32)
    eq = np.zeros((w, LANES), np.float32)
    ek = np.zeros((w, LANES), np.float32)
    for h in range(HEADS):
        for d in range(HEAD_DIM):
            pq[h * HEAD_DIM + d, h * HEAD_PAD + d] = HEAD_DIM ** -0.5
            pk[h * HEAD_DIM + d, h * HEAD_PAD + d] = 1.0
        for i in range(3):
            pfq[i, h, h * HEAD_PAD + HEAD_DIM + i] = 1.0
            pfk[i, h, h * HEAD_PAD + HEAD_DIM + 3 + i] = -1.0
            cq[0, h * HEAD_PAD + HEAD_DIM + 3 + i] = 1.0
            ck[0, h * HEAD_PAD + HEAD_DIM + i] = 1.0
        eq[h * HEAD_PAD + HEAD_DIM, h] = 1.0
        ek[h * HEAD_PAD + HEAD_DIM + 3, h] = -1.0
    bf = lambda a: jnp.asarray(a, BF16)
    return dict(pq=bf(pq), pk=bf(pk), pfq=bf(pfq), pfk=bf(pfk), cq=jnp.asarray(cq), ck=jnp.asarray(ck),
                pqkt=bf(np.concatenate([pq.T, pk.T], axis=0)), eq=bf(eq), ek=bf(ek))


def attn_prep(proj3, bf_rows, layer, cst, lay, hosted=None):
    Bl, S, _ = proj3.shape
    ts = ATTN_BLOCK
    w = HEADS * HEAD_PAD

    def body(q_ref, k_ref, f_ref, bf_ref, pq_ref, pk_ref, pfq_ref, pfk_ref, cq_ref, ck_ref,
             qa_ref, ka_ref, carry_ref):
        @pl.when(pl.program_id(1) == 0)
        def _():
            carry_ref[...] = jnp.zeros_like(carry_ref)

        z = f_ref[...].astype(F32) + bf_ref[...]
        logf = jnp.minimum(z, 0.0) - jnp.log(1.0 + jnp.exp(-jnp.abs(z)))
        r = lax.broadcasted_iota(jnp.int32, (ts, ts), 0)
        c = lax.broadcasted_iota(jnp.int32, (ts, ts), 1)
        tri = jnp.where(r >= c, 1.0, 0.0).astype(BF16)
        fcum = carry_ref[...]
        for part in _split3(logf):
            fcum = fcum + _dot(tri, part)
        carry_ref[...] = fcum[ts - 1:ts, :]
        qa = _dot(q_ref[...], pq_ref[...]) + cq_ref[...]
        ka = _dot(k_ref[...], pk_ref[...]) + ck_ref[...]
        for i, part in enumerate(_split3(fcum)):
            qa = qa + _dot(part, pfq_ref[i])
            ka = ka + _dot(part, pfk_ref[i])
        qa_ref[...] = qa.astype(BF16)
        ka_ref[...] = ka.astype(BF16)

    cfull = lambda shape: pl.BlockSpec(shape, lambda b, s: (0,) * len(shape))
    return _pcall(
        body, hosted, name="attn_prep", grid=(Bl, S // ts),
        in_specs=[pl.BlockSpec((None, ts, BRANCH_W), lambda b, s: (b, s, lay["q"] // BRANCH_W)),
                  pl.BlockSpec((None, ts, BRANCH_W), lambda b, s: (b, s, lay["k"] // BRANCH_W)),
                  pl.BlockSpec((None, ts, LANES), lambda b, s: (b, s, lay["f"] // LANES)),
                  pl.BlockSpec((None, 1, LANES), lambda b, s: (layer, 0, 0)),
                  cfull((BRANCH_W, w)), cfull((BRANCH_W, w)),
                  cfull((3, LANES, w)), cfull((3, LANES, w)), cfull((1, w)), cfull((1, w))],
        out_specs=[pl.BlockSpec((None, ts, w), lambda b, s: (b, s, 0)),
                   pl.BlockSpec((None, ts, w), lambda b, s: (b, s, 0))],
        out_shape=[SDS((Bl, S, w), BF16), SDS((Bl, S, w), BF16)],
        scratch_shapes=[pltpu.VMEM((1, LANES), F32)],
        semantics=("arbitrary", "arbitrary"),
    )(proj3, proj3, proj3, bf_rows, cst["pq"], cst["pk"], cst["pfq"], cst["pfk"], cst["cq"], cst["ck"])


def attn_fwd(qa, ka, proj3, lay, hosted=None):
    Bl, S, _ = qa.shape
    tq = ATTN_BLOCK
    nq = S // tq
    pairs = HEADS // 2
    pw = 2 * HEAD_PAD
    vw = 2 * HEAD_DIM

    def body(qa_ref, ka_ref, v_ref, o_ref, lse_ref):
        row = lax.broadcasted_iota(jnp.int32, (tq, tq), 0)
        col = lax.broadcasted_iota(jnp.int32, (tq, tq), 1)
        causal = row <= col
        for i in range(nq):
            nk = (i + 1) * tq
            rows = slice(i * tq, nk)
            o_t = []
            for h in range(2):
                hs = slice(h * HEAD_PAD, (h + 1) * HEAD_PAD)
                st = _dot_nt(ka_ref[0:nk, hs], qa_ref[rows, hs])
                diag = jnp.where(causal, st[nk - tq:], NEG_INF)
                m = jnp.max(diag, axis=0, keepdims=True)
                if i:
                    m = jnp.maximum(m, jnp.max(st[:nk - tq], axis=0, keepdims=True))
                p_diag = jnp.exp(diag - m)
                l = jnp.sum(p_diag, axis=0, keepdims=True)
                if i:
                    p_top = jnp.exp(st[:nk - tq] - m)
                    l = l + jnp.sum(p_top, axis=0, keepdims=True)
                    p = jnp.concatenate([p_top.astype(BF16), p_diag.astype(BF16)], axis=0)
                else:
                    p = p_diag.astype(BF16)
                acc = _dot_tn(v_ref[0:nk, :], p)
                o_t.append(acc[h * HEAD_DIM:(h + 1) * HEAD_DIM, :] / l)
                lse_ref[h:h + 1, rows] = m + jnp.log(l)
            o_ref[rows, :] = jnp.concatenate(o_t, axis=0).T.astype(BF16)

    return _pcall(
        body, hosted, name="attn_fwd", grid=(Bl, pairs),
        in_specs=[pl.BlockSpec((None, S, pw), lambda b, p: (b, 0, p)),
                  pl.BlockSpec((None, S, pw), lambda b, p: (b, 0, p)),
                  pl.BlockSpec((None, S, vw), lambda b, p: (b, 0, lay["v"] // vw + p))],
        out_specs=[pl.BlockSpec((None, S, vw), lambda b, p: (b, 0, p)),
                   pl.BlockSpec((None, None, 2, S), lambda b, p: (b, p, 0, 0))],
        out_shape=[SDS((Bl, S, BRANCH_W), BF16), SDS((Bl, pairs, 2, S), F32)],
        semantics=("arbitrary", "arbitrary"),
    )(qa, ka, proj3)


def attn_bwd(qa, ka, proj3, dao, ao, lse, dproj3, lay, hosted=None):
    Bl, S, _ = qa.shape
    tk = ATTN_BLOCK
    nq = S // tk
    pairs = HEADS // 2
    pw = 2 * HEAD_PAD
    vw = 2 * HEAD_DIM

    def body(qa_ref, ka_ref, v_ref, do_ref, o_ref, lse_ref, _, dqa_ref, dka_ref, dv_ref):
        row = lax.broadcasted_iota(jnp.int32, (tk, tk), 0)
        col = lax.broadcasted_iota(jnp.int32, (tk, tk), 1)
        causal = row <= col
        lane8 = lax.broadcasted_iota(jnp.int32, (8, vw), 1)
        lane_s = lax.broadcasted_iota(jnp.int32, (S, vw), 1)
        lane_k = lax.broadcasted_iota(jnp.int32, (tk, vw), 1)
        doo = do_ref[...].astype(F32) * o_ref[...].astype(F32)
        hi = doo.astype(BF16)
        lo = (doo - hi.astype(F32)).astype(BF16)
        delta, v_head = [], []
        for h in range(2):
            sel = jnp.where((lane8 >= h * HEAD_DIM) & (lane8 < (h + 1) * HEAD_DIM), 1.0, 0.0).astype(BF16)
            delta.append((_dot_nt(sel, hi) + _dot_nt(sel, lo))[0:1, :])
            in_head = (lane_s >= h * HEAD_DIM) & (lane_s < (h + 1) * HEAD_DIM)
            v_head.append(jnp.where(in_head, v_ref[...], jnp.zeros_like(v_ref[...])))
        dqa_ref[...] = jnp.zeros_like(dqa_ref)
        for j in range(nq):
            q0 = j * tk
            krows = slice(q0, q0 + tk)
            do = do_ref[q0:, :]
            dvs = []
            for h in range(2):
                hs = slice(h * HEAD_PAD, (h + 1) * HEAD_PAD)
                k = ka_ref[krows, hs]
                q = qa_ref[q0:, hs]
                st = _dot_nt(k, q)
                p = jnp.exp(st - lse_ref[h:h + 1, q0:])
                p_diag = jnp.where(causal, p[:, :tk], 0.0)
                p = jnp.concatenate([p_diag, p[:, tk:]], axis=1) if j < nq - 1 else p_diag
                dvs.append(_dot(p.astype(BF16), do))
                dpt = _dot_nt(v_head[h][krows, :], do)
                ds = (p * (dpt - delta[h][:, q0:])).astype(BF16)
                dka_ref[krows, hs] = _dot(ds, q)
                dqa_ref[q0:, hs] += _dot_tn(ds, k)
            dv_ref[krows, :] = jnp.where(lane_k < HEAD_DIM, dvs[0], dvs[1]).astype(BF16)

    seq = lambda w, c0=0: pl.BlockSpec((None, S, w), lambda b, p: (b, 0, c0 + p))
    return _pcall(
        body, hosted, name="attn_bwd", grid=(Bl, pairs),
        in_specs=[seq(pw), seq(pw), seq(vw, lay["v"] // vw), seq(vw), seq(vw),
                  pl.BlockSpec((None, None, 2, S), lambda b, p: (b, p, 0, 0)), _ANY],
        out_specs=[seq(pw), seq(pw), seq(vw, lay["v"] // vw)],
        out_shape=[SDS((Bl, S, HEADS * HEAD_PAD), F32), SDS((Bl, S, HEADS * HEAD_PAD), F32),
                   SDS(dproj3.shape, BF16)],
        aliases={6: 2}, semantics=("arbitrary", "arbitrary"),
    )(qa, ka, proj3, dao, ao, lse, dproj3)


def attn_post(dqa, dka, proj3, bf_rows, layer, dproj3, cst, lay, hosted=None):
    Bl, S, w = dqa.shape
    ts = ATTN_BLOCK
    ns = S // ts
    qkf = 2 * BRANCH_W + F_PAD

    def body(dqa_ref, dka_ref, f_ref, bf_ref, pqkt_ref, eq_ref, ek_ref, _, dqkf_ref, dbf_ref, carry_ref):
        b, s = pl.program_id(0), pl.program_id(1)

        @pl.when(s == 0)
        def _():
            carry_ref[...] = jnp.zeros_like(carry_ref)

        dqa_v, dka_v = dqa_ref[...], dka_ref[...]
        qh = dqa_v.astype(BF16)
        kh = dka_v.astype(BF16)
        dqkf_ref[:, :BRANCH_W] = _dot(qh, pqkt_ref[:w, :]).astype(BF16)
        dqkf_ref[:, BRANCH_W:2 * BRANCH_W] = _dot(kh, pqkt_ref[w:, :]).astype(BF16)
        ql = (dqa_v - qh.astype(F32)).astype(BF16)
        kl = (dka_v - kh.astype(F32)).astype(BF16)
        d_f = (_dot(qh, eq_ref[...]) + _dot(ql, eq_ref[...])) + (_dot(kh, ek_ref[...]) + _dot(kl, ek_ref[...]))
        r = lax.broadcasted_iota(jnp.int32, (ts, ts), 0)
        c = lax.broadcasted_iota(jnp.int32, (ts, ts), 1)
        triu = jnp.where(c >= r, 1.0, 0.0).astype(BF16)
        rev = carry_ref[...]
        for part in _split3(d_f):
            rev = rev + _dot(triu, part)
        carry_ref[...] = rev[0:1, :]
        z = f_ref[...].astype(F32) + bf_ref[...]
        lane = lax.broadcasted_iota(jnp.int32, (ts, LANES), 1)
        dfl = jnp.where(lane < HEADS, rev / (1.0 + jnp.exp(z)), 0.0)
        dqkf_ref[:, 2 * BRANCH_W:] = jnp.concatenate(
            [dfl.astype(BF16), jnp.zeros((ts, F_PAD - LANES), BF16)], axis=1)
        part = jnp.sum(dfl, axis=0, keepdims=True)

        @pl.when((b == 0) & (s == 0))
        def _():
            dbf_ref[...] = part

        @pl.when((b > 0) | (s > 0))
        def _():
            dbf_ref[...] += part

    assert lay["q"] % qkf == 0
    cfull = lambda shape: pl.BlockSpec(shape, lambda b, s: (0,) * len(shape))
    rev_blk = lambda wd, c0=0: pl.BlockSpec((None, ts, wd), lambda b, s: (b, ns - 1 - s, c0))
    return _pcall(
        body, hosted, name="attn_post", grid=(Bl, ns),
        in_specs=[rev_blk(w), rev_blk(w), rev_blk(LANES, lay["f"] // LANES),
                  pl.BlockSpec((None, 1, LANES), lambda b, s: (layer, 0, 0)),
                  cfull((2 * w, BRANCH_W)), cfull((w, LANES)), cfull((w, LANES)), _ANY],
        out_specs=[rev_blk(qkf, lay["q"] // qkf), cfull((1, LANES))],
        out_shape=[SDS(dproj3.shape, BF16), SDS((1, LANES), F32)],
        scratch_shapes=[pltpu.VMEM((1, LANES), F32)],
        aliases={7: 0}, semantics=("arbitrary", "arbitrary"),
    )(dqa, dka, proj3, bf_rows, cst["pqkt"], cst["eq"], cst["ek"], dproj3)


def _shift_down(x, k, row):
    return jnp.where(row >= k, pltpu.roll(x, k, axis=0), 0.0)


def _shift_up(x, k, row):
    n = x.shape[0]
    return jnp.where(row < n - k, pltpu.roll(x, n - k, axis=0), 0.0)


def _window_sum(x, g, row, shift):
    s2 = x + shift(x, 1, row)
    s4 = s2 + shift(s2, 2, row)
    s8 = s4 + shift(s4, 4, row)
    s16 = s8 + shift(s8, 8, row)
    return jnp.where(g == 0, s2, jnp.where(g == 1, s4, jnp.where(g == 2, s8, s16)))


def _window_count(g, row):
    wnd = jnp.where(g == 0, 2, jnp.where(g == 1, 4, jnp.where(g == 2, 8, 16)))
    return jnp.minimum(row + 1, wnd).astype(F32)


def _group_columns(ref):
    return [ref[:, n * GROUP_W:(n + 1) * GROUP_W].astype(F32) for n in range(4)]


def poolconv_fwd(proj3, pool_w, pool_scale, conv_w, layer, lay, hosted=None):
    Bl, S, _ = proj3.shape

    def body(x_ref, pw_ref, ps_ref, cw_ref, po_ref, co_ref):
        g = pl.program_id(1)
        row = lax.broadcasted_iota(jnp.int32, (S, GROUP_W), 0)
        u, cv, cb, cc = _group_columns(x_ref)
        d = _window_sum(u, g, row, _shift_down) / _window_count(g, row) - u
        po_ref[...] = (_dot(d.astype(BF16), pw_ref[...]) * ps_ref[...]).astype(BF16)
        z = cc * cv
        y = cw_ref[0:1, :] * _shift_down(z, 2, row) + cw_ref[1:2, :] * _shift_down(z, 1, row) + cw_ref[2:3, :] * z
        co_ref[...] = (cb * y).astype(BF16)

    out = pl.BlockSpec((None, S, GROUP_W), lambda b, g: (b, 0, g))
    return _pcall(
        body, hosted, name="poolconv_fwd", grid=(Bl, N_GROUPS),
        in_specs=[pl.BlockSpec((None, S, BRANCH_W), lambda b, g: (b, 0, lay["pc"] // BRANCH_W + g)),
                  pl.BlockSpec((None, None, GROUP_W, GROUP_W), lambda b, g: (layer, g, 0, 0)),
                  pl.BlockSpec((None, 1, GROUP_W), lambda b, g: (layer, 0, g)),
                  pl.BlockSpec((None, None, 3, GROUP_W), lambda b, g: (g, layer, 0, 0))],
        out_specs=[out, out],
        out_shape=[SDS((Bl, S, BRANCH_W), BF16), SDS((Bl, S, BRANCH_W), BF16)],
        semantics=("arbitrary", "arbitrary"),
    )(proj3, pool_w, pool_scale, conv_w)


def poolconv_bwd(proj3, dpo, dco, pool_w, pool_scale, conv_w, layer, dproj3, lay, hosted=None):
    Bl, S, _ = proj3.shape

    def body(x_ref, dpo_ref, dco_ref, pw_ref, ps_ref, cw_ref, _, dx_ref, dpw_ref, dps_ref, dcw_ref):
        g, b = pl.program_id(0), pl.program_id(1)
        row = lax.broadcasted_iota(jnp.int32, (S, GROUP_W), 0)
        cnt = _window_count(g, row)
        u, cv, cb, cc = _group_columns(x_ref)
        d = (_window_sum(u, g, row, _shift_down) / cnt - u).astype(BF16)
        pw = pw_ref[...]
        ypre = _dot(d, pw)
        dpo_v = dpo_ref[...].astype(F32)
        dps = jnp.sum(dpo_v * ypre, axis=0, keepdims=True)
        dyp = (dpo_v * ps_ref[...]).astype(BF16)
        dpw = _dot_tn(d, dyp)
        dd = _dot_nt(dyp, pw)
        dx_ref[:, 0:GROUP_W] = (_window_sum(dd / cnt, g, row, _shift_up) - dd).astype(BF16)

        z = cc * cv
        z1, z2 = _shift_down(z, 1, row), _shift_down(z, 2, row)
        w0, w1, w2 = cw_ref[0:1, :], cw_ref[1:2, :], cw_ref[2:3, :]
        y = w0 * z2 + w1 * z1 + w2 * z
        dco_v = dco_ref[...].astype(F32)
        dy = dco_v * cb
        dz = w0 * _shift_up(dy, 2, row) + w1 * _shift_up(dy, 1, row) + w2 * dy
        dx_ref[:, GROUP_W:2 * GROUP_W] = (dz * cc).astype(BF16)
        dx_ref[:, 2 * GROUP_W:3 * GROUP_W] = (dco_v * y).astype(BF16)
        dx_ref[:, 3 * GROUP_W:] = (dz * cv).astype(BF16)
        dcw = jnp.concatenate([jnp.sum(dy * z2, axis=0, keepdims=True),
                               jnp.sum(dy * z1, axis=0, keepdims=True),
                               jnp.sum(dy * z, axis=0, keepdims=True)], axis=0)

        @pl.when(b == 0)
        def _():
            dpw_ref[...] = dpw
            dps_ref[...] = dps
            dcw_ref[...] = dcw

        @pl.when(b > 0)
        def _():
            dpw_ref[...] += dpw
            dps_ref[...] += dps
            dcw_ref[...] += dcw

    blk = pl.BlockSpec((None, S, GROUP_W), lambda g, b: (b, 0, g))
    pc = pl.BlockSpec((None, S, BRANCH_W), lambda g, b: (b, 0, lay["pc"] // BRANCH_W + g))
    return _pcall(
        body, hosted, name="poolconv_bwd", grid=(N_GROUPS, Bl),
        in_specs=[pc, blk, blk,
                  pl.BlockSpec((None, None, GROUP_W, GROUP_W), lambda g, b: (layer, g, 0, 0)),
                  pl.BlockSpec((None, 1, GROUP_W), lambda g, b: (layer, 0, g)),
                  pl.BlockSpec((None, None, 3, GROUP_W), lambda g, b: (g, layer, 0, 0)), _ANY],
        out_specs=[pc, pl.BlockSpec((None, GROUP_W, GROUP_W), lambda g, b: (g, 0, 0)),
                   pl.BlockSpec((1, GROUP_W), lambda g, b: (0, g)),
                   pl.BlockSpec((None, 3, GROUP_W), lambda g, b: (g, 0, 0))],
        out_shape=[SDS(dproj3.shape, BF16), SDS((N_GROUPS, GROUP_W, GROUP_W), F32), SDS((1, BRANCH_W), F32),
                   SDS((N_GROUPS, 3, GROUP_W), F32)],
        aliases={6: 0}, semantics=("arbitrary", "arbitrary"),
    )(proj3, dpo, dco, pool_w, pool_scale, conv_w, dproj3)


def _tile_2d(rows, cols, n_arrays):
    budget = VMEM_LIMIT // 2
    lanes = -(-cols // LANES) * LANES
    if rows % 8 == 0:
        for t in range(min(rows, 2048), 7, -8):
            if rows % t == 0 and 2 * n_arrays * t * lanes * 4 <= budget:
                return t, cols
    for t in (1024, 512, 256, 128):
        if cols % t == 0 and 2 * n_arrays * (rows + 8) * t * 4 <= budget:
            return rows, t
    return rows, cols


def add_pair(kept, layer, where, received, name):
    _, n, _, R, C = kept.shape
    tr, tc = _tile_2d(R, C, 3)

    def body(where_ref, a_ref, b_ref, o_ref):
        o_ref[...] = (a_ref[...].astype(F32) + b_ref[...].astype(F32)).astype(BF16)

    blk = pl.BlockSpec((None, tr, tc), lambda d, i, j, where_ref: (d, i, j))
    grid_spec = pltpu.PrefetchScalarGridSpec(
        num_scalar_prefetch=1, grid=(n, R // tr, C // tc),
        in_specs=[pl.BlockSpec((None, None, None, tr, tc),
                               lambda d, i, j, where_ref: (layer, d, where_ref[0], i, j)), blk],
        out_specs=blk)
    return pl.pallas_call(body, name=name, grid_spec=grid_spec, out_shape=SDS((n, R, C), BF16),
                          compiler_params=_params("arbitrary", "arbitrary", "arbitrary"))(where, kept, received)


def add_chips(arrived, own, layer, where, n_layers, prev, name):
    _, R, C = arrived.shape
    tr, tc = _tile_2d(R, C, 6)

    def body(where_ref, a0, a1, a2, a3, own_ref, *rest):
        o_ref = rest[-1]
        chip = where_ref[1]
        acc = None
        for j, a_ref in enumerate((a0, a1, a2, a3)):
            term = jnp.where(chip == j, own_ref[...], a_ref[...]).astype(F32)
            acc = term if acc is None else acc + term
        o_ref[...] = acc

    def slot(j):
        return pl.BlockSpec((None, tr, tc), lambda i, k, where_ref, j=j: (
            jnp.where(where_ref[1] == j, (j + 1) % N_CHIPS, j), i, k))

    in_specs = [slot(j) for j in range(N_CHIPS)] + [
        pl.BlockSpec((None, tr, tc), lambda i, k, where_ref: (where_ref[1], i, k))]
    args = [where, arrived, arrived, arrived, arrived, own]
    aliases = {}
    if prev is not None:
        in_specs.append(_ANY)
        args.append(prev)
        aliases = {len(args) - 1: 0}
    grid_spec = pltpu.PrefetchScalarGridSpec(
        num_scalar_prefetch=1, grid=(R // tr, C // tc), in_specs=in_specs,
        out_specs=pl.BlockSpec((None, None, tr, tc), lambda i, k, where_ref: (layer, where_ref[0], i, k)))
    return pl.pallas_call(body, name=name, grid_spec=grid_spec, out_shape=SDS((n_layers, 2, R, C), F32),
                          input_output_aliases=aliases,
                          compiler_params=_params("arbitrary", "arbitrary"))(*args)


def adamw(w, g, m, v, name):
    if w.ndim == 2:
        R, C = w.shape
        tr, _ = _tile_2d(R, C, 7)
        grid, blk = (R // tr,), pl.BlockSpec((tr, C), lambda i: (i, 0))
    else:
        N, r, C = w.shape
        tn = max(t for t in range(1, N + 1) if N % t == 0 and t * r * C * 4 <= 1024 * 1024)
        grid, blk = (N // tn,), pl.BlockSpec((tn, r, C), lambda i: (i, 0, 0))

    def body(w_ref, g_ref, m_ref, v_ref, d_ref, nm_ref, nv_ref):
        gv = g_ref[...]
        m_new = ADAM_B1 * m_ref[...] + (1.0 - ADAM_B1) * gv
        v_new = ADAM_B2 * v_ref[...] + (1.0 - ADAM_B2) * (gv * gv)
        m_hat = m_new / (1.0 - ADAM_B1 ** ADAM_STEP)
        v_hat = v_new / (1.0 - ADAM_B2 ** ADAM_STEP)
        d_ref[...] = -ADAM_LR * (m_hat / (jnp.sqrt(v_hat) + ADAM_EPS) + ADAM_WD * w_ref[...])
        nm_ref[...] = m_new
        nv_ref[...] = v_new

    out = SDS(w.shape, F32)
    return pl.pallas_call(body, name=name, grid=grid, in_specs=[blk] * 4, out_specs=[blk] * 3,
                          out_shape=[out, out, out], compiler_params=_params("arbitrary"))(w, g, m, v)


_COMM = pltpu.CompilerParams(has_side_effects=True)


def gather_buffers(shards):
    me_chip = 2 * lax.axis_index("x") + lax.axis_index("y")
    pool = {}
    for name, sh in shards.items():
        L, r, c = sh.shape
        if name in ROW_SHARDED:
            pool[name] = lax.dynamic_update_slice(lax.empty((L, N_CHIPS, r, c), sh.dtype), sh[:, None],
                                                  (0, me_chip, 0, 0))
        else:
            pool[name] = lax.dynamic_update_slice(lax.empty((N_CHIPS, L, r, c), sh.dtype), sh[None],
                                                  (me_chip, 0, 0, 0))
    return pool


def comm_now(pool, stages, name):
    stages = [Hosted(pool, jobs) for jobs in stages]
    names = sorted({m for st in stages for m in st.names})
    n = len(names)

    def body(*refs):
        bufs = dict(zip(names, refs[n:2 * n]))
        sems = refs[2 * n:]
        for i, st in enumerate(stages):
            plan = _hosted_plan(st, bufs, sems[2 * i], sems[2 * i + 1])
            _hosted_start(plan, True)
            _hosted_finish(plan, True)

    sem = pltpu.SemaphoreType.DMA
    scratch = []
    for st in stages:
        scratch += [sem((len(st.jobs), 3)), sem((len(st.jobs), 3))]
    res = pl.pallas_call(
        body, name=name, in_specs=[_ANY] * n, out_specs=[_ANY] * n,
        out_shape=[SDS(pool[m].shape, pool[m].dtype) for m in names],
        scratch_shapes=scratch, input_output_aliases={t: t for t in range(n)},
        compiler_params=_COMM,
    )(*[pool[m] for m in names])
    pool.update(zip(names, res))


def gather_now(pool, units):
    comm_now(pool, [[("ici", name, layer) for name, layer in units],
                    [("fwd", name, layer) for name, layer in units]], "gather_now")


def allgather_chips(buf, name):
    def body(src_ref, out_ref, send_sems, recv_sems, local_sem):
        x, y, c = _position()
        me = 2 * x + y
        mine = pltpu.make_async_copy(src_ref, out_ref.at[me], local_sem)
        mine.start()
        sends = []
        for k, (px, py) in enumerate(_other_chips(x, y)):
            cp = _remote(src_ref, out_ref.at[me], send_sems.at[k], recv_sems.at[k], (px, py, c))
            cp.start()
            sends.append(cp)
        for k, (px, py) in enumerate(_other_chips(x, y)):
            _remote(src_ref, out_ref.at[2 * px + py], send_sems.at[k], recv_sems.at[k], (px, py, c)).wait_recv()
        for cp in sends:
            cp.wait_send()
        mine.wait()

    sem = pltpu.SemaphoreType.DMA
    return pl.pallas_call(
        body, name=name, in_specs=[_ANY], out_specs=_ANY, out_shape=SDS((N_CHIPS,) + buf.shape, buf.dtype),
        scratch_shapes=[sem((3,)), sem((3,)), sem], compiler_params=_COMM,
    )(buf)


BIG = ("w_in", "w_proj_attn", "w_proj_pool", "w_proj_conv", "conv_w", "w_out", "w_gate_up", "w_down")
REPLICATED = ("attn_norm", "b_forget", "b_gate", "pool_w", "pool_scale", "ffn_norm", "final_norm")
ORDER = ("attn_norm", "w_in", "b_forget", "b_gate", "w_proj_attn", "pool_w", "pool_scale", "w_proj_pool",
         "conv_w", "w_proj_conv", "w_out", "ffn_norm", "w_gate_up", "w_down", "final_norm")


def _proj_layout(D):
    lay = {"g": 0, "q": 3 * D}
    lay["k"] = lay["q"] + BRANCH_W
    lay["f"] = lay["k"] + BRANCH_W
    lay["v"] = lay["f"] + F_PAD
    lay["pc"] = lay["v"] + BRANCH_W
    lay["width"] = lay["pc"] + 4 * BRANCH_W
    return lay


_REF = dict(q=0, k=512, v=1024, f=1536, u=1544, cv=2056, cb=2568, cc=3080, g=3592)


def _packed_pieces(D):
    pieces = [(_REF["g"], 3 * D), (_REF["q"], BRANCH_W), (_REF["k"], BRANCH_W), (_REF["f"], HEADS),
              (None, F_PAD - HEADS), (_REF["v"], BRANCH_W)]
    for gi in range(N_GROUPS):
        pieces += [(_REF[name] + gi * GROUP_W, GROUP_W) for name in ("u", "cv", "cb", "cc")]
    return pieces


def _packed_runs(D, cs):
    runs, at = [], 0
    for start, n in _packed_pieces(D):
        if start is None:
            runs.append((at, None, 0, n))
            at += n
        while start is not None and n:
            chip, off = divmod(start, cs)
            take = min(n, cs - off)
            runs.append((at, chip, off, take))
            at, start, n = at + take, start + take, n - take
    return runs


def pack_w_in(shards, layer):
    _, _, cs, D = shards.shape
    runs = _packed_runs(D, cs)
    width = runs[-1][0] + runs[-1][3]
    tc = _tile(D, (256, 128))

    def body(s_ref, o_ref):
        for dst, chip, off, rows in runs:
            if chip is None:
                o_ref[dst:dst + rows, :] = jnp.zeros((rows, tc), s_ref.dtype)
            else:
                o_ref[dst:dst + rows, :] = s_ref[chip, off:off + rows, :]

    return pl.pallas_call(
        body, name="pack_w_in", grid=(D // tc,),
        in_specs=[pl.BlockSpec((N_CHIPS, None, cs, tc), lambda j: (0, layer, 0, j))],
        out_specs=pl.BlockSpec((width, tc), lambda j: (0, j)),
        out_shape=SDS((width, D), shards.dtype), compiler_params=_params("arbitrary"),
    )(shards)


def unpack_w_in(p, cs):
    width, D = p.shape
    half = cs // 2
    runs = []
    for src, chip, off, rows in _packed_runs(D, cs):
        while chip is not None and rows:
            h, at = divmod(off, half)
            take = min(rows, half - at)
            runs.append((src, chip, h, at, take))
            src, off, rows = src + take, off + take, rows - take
    tc = _tile(D, (256, 128))

    def body(p_ref, o_ref):
        for src, chip, h, at, rows in runs:
            o_ref[chip, h, at:at + rows, :] = p_ref[src:src + rows, :]

    return pl.pallas_call(
        body, name="unpack_w_in", grid=(D // tc,),
        in_specs=[pl.BlockSpec((width, tc), lambda j: (0, j))],
        out_specs=pl.BlockSpec((N_CHIPS, 2, half, tc), lambda j: (0, 0, 0, j)),
        out_shape=SDS((N_CHIPS, 2, half, D), p.dtype), compiler_params=_params("arbitrary"),
    )(p)


def _split_flat(vec, shapes):
    out, at = [], 0
    for shp in shapes:
        n = int(np.prod(shp))
        out.append(vec[at:at + n].reshape(shp))
        at += n
    return out


def kernel(x, attn_norm, w_in, b_forget, b_gate, w_proj_attn, pool_w, pool_scale, w_proj_pool, conv_w, w_proj_conv, w_out, ffn_norm, w_gate_up, w_down, final_norm, loss_target, m_attn_norm, m_w_in, m_b_forget, m_b_gate, m_w_proj_attn, m_pool_w, m_pool_scale, m_w_proj_pool, m_conv_w, m_w_proj_conv, m_w_out, m_ffn_norm, m_w_gate_up, m_w_down, m_final_norm, v_attn_norm, v_w_in, v_b_forget, v_b_gate, v_w_proj_attn, v_pool_w, v_pool_scale, v_w_proj_pool, v_conv_w, v_w_proj_conv, v_w_out, v_ffn_norm, v_w_gate_up, v_w_down, v_final_norm):
    weights = dict(attn_norm=attn_norm, w_in=w_in, b_forget=b_forget, b_gate=b_gate, w_proj_attn=w_proj_attn,
                   pool_w=pool_w, pool_scale=pool_scale, w_proj_pool=w_proj_pool, conv_w=conv_w,
                   w_proj_conv=w_proj_conv, w_out=w_out, ffn_norm=ffn_norm, w_gate_up=w_gate_up, w_down=w_down,
                   final_norm=final_norm)
    mom_m = dict(attn_norm=m_attn_norm, w_in=m_w_in, b_forget=m_b_forget, b_gate=m_b_gate, w_proj_attn=m_w_proj_attn,
                 pool_w=m_pool_w, pool_scale=m_pool_scale, w_proj_pool=m_w_proj_pool, conv_w=m_conv_w,
                 w_proj_conv=m_w_proj_conv, w_out=m_w_out, ffn_norm=m_ffn_norm, w_gate_up=m_w_gate_up,
                 w_down=m_w_down, final_norm=m_final_norm)
    mom_v = dict(attn_norm=v_attn_norm, w_in=v_w_in, b_forget=v_b_forget, b_gate=v_b_gate, w_proj_attn=v_w_proj_attn,
                 pool_w=v_pool_w, pool_scale=v_pool_scale, w_proj_pool=v_w_proj_pool, conv_w=v_conv_w,
                 w_proj_conv=v_w_proj_conv, w_out=v_w_out, ffn_norm=v_ffn_norm, w_gate_up=v_w_gate_up,
                 w_down=v_w_down, final_norm=v_final_norm)

    Bl, S, D = x.shape
    T = Bl * S
    L = w_in.shape[0]
    F = w_down.shape[1] * N_CHIPS
    lay = _proj_layout(D)
    cst = _placement_constants()
    assert L == N_LAYERS and S % ATTN_BLOCK == 0 and F % (2 * LANES) == 0 and D % BRANCH_W == 0
    assert w_in.shape[2] * N_CHIPS == _REF["g"] + 3 * D and conv_w.shape[2] == GROUP_W

    send = {n: weights[n].astype(BF16) for n in BIG}
    send["conv_w"] = conv_w
    me_chip = 2 * lax.axis_index("x") + lax.axis_index("y")
    send["w_in"] = w_in.transpose(0, 2, 1).astype(BF16)
    pool = gather_buffers(send)
    gather_now(pool, [("w_in", 0)])
    rest = ("w_out", "w_proj_attn", "w_proj_pool", "w_gate_up", "w_proj_conv", "conv_w")
    late = ("w_out", "w_proj_attn", "w_proj_pool", "w_proj_conv", "conv_w")
    jobs = lambda kind, names, layer: [(kind, n, layer) for n in names]
    carried = {
        ("in_proj", 0): jobs("ici", rest, 0),
        ("attn_prep", 0): jobs("fwd", late, 0),
        ("attn_fwd", 0): jobs("fwd", ("w_gate_up",), 0) + jobs("ici", ("w_in",), 1) + jobs("ici", ("w_down",), 0),
        ("poolconv_fwd", 0): jobs("fwd", ("w_in",), 1) + jobs("fwd", ("w_down",), 0),
        ("mix_fwd", 0): jobs("ici", ("w_down",), 1),
        ("gate_up_proj", 0): jobs("ici", late, 1) + jobs("fwd", ("w_down",), 1),
        ("ffn_down_fwd", 0): jobs("ici", ("w_gate_up",), 1),
        ("in_proj", 1): jobs("fwd", ("w_gate_up",) + late, 1),
    }
    carry = lambda call, layer: Hosted(pool, carried[call, layer]) if (call, layer) in carried else None
    w_down_f = lambda: pool["w_down"].reshape(L, F, D)
    pool_w_b = pool_w.astype(BF16)
    an3, fn3 = attn_norm.reshape(L, 1, D), ffn_norm.reshape(L, 1, D)
    bg3, ps3 = b_gate.reshape(L, 1, 3 * D), pool_scale.reshape(L, 1, BRANCH_W)
    bf3 = jnp.pad(b_forget, ((0, 0), (0, LANES - HEADS))).reshape(L, 1, LANES)

    xs = x.reshape(T, D)
    saved = []
    w_in_p = []
    for l in range(L):
        w_in_p.append(pack_w_in(pool["w_in"], l))
        proj, h = norm_matmul(xs, an3, w_in_p[l], l, "rows", "in_proj", carry("in_proj", l))
        proj3 = proj.reshape(Bl, S, lay["width"])
        qa, ka = attn_prep(proj3, bf3, l, cst, lay, carry("attn_prep", l))
        ao, lse = attn_fwd(qa, ka, proj3, lay, carry("attn_fwd", l))
        po, co = poolconv_fwd(proj3, pool_w_b, ps3, pool["conv_w"], l, lay, carry("poolconv_fwd", l))
        ao2, po2, co2 = (a.reshape(T, BRANCH_W) for a in (ao, po, co))
        x1, ys, mixed = mix_fwd(ao2, po2, co2, proj, bg3, pool["w_proj_attn"], pool["w_proj_pool"],
                                pool["w_proj_conv"], pool["w_out"], l, xs, carry("mix_fwd", l))
        ab, h2 = norm_matmul(x1, fn3, pool["w_gate_up"], l, "by_shard", "gate_up_proj", carry("gate_up_proj", l))
        x2, s_act = ffn_down_fwd(ab, w_down_f(), l, x1, carry("ffn_down_fwd", l))
        saved.append(dict(x=xs, proj=proj, proj3=proj3, h=h, qa=qa, ka=ka, ao=ao, lse=lse, ao2=ao2, po2=po2,
                          co2=co2, ys=ys, mixed=mixed, x1=x1, ab=ab, h2=h2, s=s_act))
        xs = x2
    w_gu, w_o, conv_w_g = pool["w_gate_up"], pool["w_out"], pool["conv_w"]
    wpa, wpp, wpc = pool["w_proj_attn"], pool["w_proj_pool"], pool["w_proj_conv"]
    w_down_f = w_down_f()

    loss_row, dx, dxb, g_final = loss_head(xs, final_norm.reshape(1, D), loss_target.reshape(T, D))
    loss = lax.psum(loss_row[0, 0], AXES)

    reduced_names = tuple(n for n in BIG if n != "conv_w")
    early_names = tuple(n for n in reduced_names if n != "w_in")
    proj_names = ("w_out", "w_proj_attn", "w_proj_pool", "w_proj_conv")
    first_names = ("w_in", "w_gate_up", "w_down")
    where = jnp.stack([lax.axis_index("c"), me_chip]).astype(jnp.int32)
    rs = {}

    def reduce_begin(layer, grads):
        for n, g in grads.items():
            g5 = g.reshape((1, N_CHIPS, 2, -1) + g.shape[-1:])
            rs["g%d:%s" % (layer, n)] = g5
            for role in "ra":
                rs["%s%d:%s" % (role, layer, n)] = lax.empty((N_CHIPS,) + g5.shape[3:], BF16)

    swap_jobs = lambda layer, names: [("swap", "g%d:%s" % (layer, n), "r%d:%s" % (layer, n), 0) for n in names]
    xchg_jobs = lambda layer, names: [("xchg", "s%d:%s" % (layer, n), "a%d:%s" % (layer, n)) for n in names]
    join_jobs = lambda layer, names: [("join", "o:" + n, layer) for n in names]

    def pair_sums(layer, names):
        for n in names:
            rs["s%d:%s" % (layer, n)] = add_pair(rs["g%d:%s" % (layer, n)], 0, where, rs["r%d:%s" % (layer, n)],
                                                 "add_pair_" + n)

    def chip_sums(layer, names, slot, n_slots):
        for n in names:
            rs["o:" + n] = add_chips(rs["a%d:%s" % (layer, n)], rs["s%d:%s" % (layer, n)], slot, where, n_slots,
                                     rs.get("o:" + n), "add_chips_" + n)

    small = {n: [None] * L for n in REPLICATED if n != "final_norm"}
    g_conv = [None] * L
    to3 = lambda a: a.reshape(Bl, S, -1)
    for l in reversed(range(L)):
        sv = saved[l]
        behind = (lambda jobs: Hosted(rs, jobs)) if l == 0 else (lambda jobs: None)
        grads = {}
        da, db = ffn_down_bwd(dxb, w_down_f, l, sv["ab"], behind(swap_jobs(1, reduced_names)))
        if l == 0:
            pair_sums(1, reduced_names)
        grads["w_down"] = matmul_tn(sv["s"], [dxb], "grad_w_down", hosted=behind(xchg_jobs(1, ("w_down",))))
        grads["w_gate_up"] = matmul_tn(sv["h2"], [da, db], "grad_w_gate_up", by_dest=True, tn=2 * F // N_CHIPS,
                                       tk=_tile(T, (1024, 512, 256)), hosted=behind(xchg_jobs(1, ("w_gate_up",))))
        dx1, dx1b, g_fn = matmul_nt_normbwd([da, db], w_gu, l, "by_shard", sv["x1"], fn3, dx, "gate_up_bwd",
                                            behind(xchg_jobs(1, ("w_in",))))
        small["ffn_norm"][l] = g_fn[0]
        if l == 0:
            chip_sums(1, first_names, 1, L)
        dys, dproj, dao, dpo, dco, g_bg = mix_bwd(dx1b, w_o, sv["proj"], bg3, sv["ys"], wpa, wpp, wpc, l,
                                                  lay["width"],
                                                  behind(xchg_jobs(1, proj_names) + join_jobs(1, first_names)))
        if l == 0:
            chip_sums(1, proj_names, 1, L)
        small["b_gate"][l] = g_bg[0]
        grads["w_out"] = matmul_tn(sv["mixed"], [dx1b], "grad_w_out")
        for n, (name, br) in enumerate((("w_proj_attn", sv["ao2"]), ("w_proj_pool", sv["po2"]),
                                        ("w_proj_conv", sv["co2"]))):
            grads[name] = matmul_tn(br, [dys], "grad_" + name, b_col0=n * D, n_cols=D, by_dest=True,
                                    tn=D // N_CHIPS)
        if l == 0:
            reduce_begin(0, grads)
        dqa, dka, dproj3 = attn_bwd(sv["qa"], sv["ka"], sv["proj3"], to3(dao), sv["ao"], sv["lse"], to3(dproj), lay,
                                    behind(swap_jobs(0, early_names) + join_jobs(1, proj_names)))
        if l == 0:
            pair_sums(0, early_names)
        dproj3, g_bf = attn_post(dqa, dka, sv["proj3"], bf3, l, dproj3, cst, lay, behind(xchg_jobs(
            0, ("w_out", "w_proj_attn", "w_proj_pool", "w_proj_conv"))))
        small["b_forget"][l] = g_bf[0, :HEADS]
        dproj3, g_pw, g_ps, g_conv[l] = poolconv_bwd(sv["proj3"], to3(dpo), to3(dco), pool_w_b, ps3, conv_w_g, l,
                                                     dproj3, lay, behind(xchg_jobs(0, ("w_down",))))
        small["pool_w"][l], small["pool_scale"][l] = g_pw, g_ps[0]
        dproj = dproj3.reshape(T, lay["width"])
        g_w_in = unpack_w_in(matmul_tn(dproj, [sv["h"]], "grad_w_in", tn=_tile(D, (1024, 512)), hosted=behind(xchg_jobs(
            0, ("w_gate_up",)))), w_in.shape[2])
        if l:
            reduce_begin(l, {**grads, "w_in": g_w_in})
        else:
            reduce_begin(0, {"w_in": g_w_in})
            comm_now(rs, [swap_jobs(0, ("w_in",))], "swap_w_in_halves")
            pair_sums(0, ("w_in",))
        dx, dxb, g_an = matmul_nt_normbwd([dproj], w_in_p[l], l, "rows", sv["x"], an3, dx1, "in_proj_bwd",
                                          behind(xchg_jobs(0, ("w_in",))))
        small["attn_norm"][l] = g_an[0]
    grad_x = dx.reshape(Bl, S, D)

    small_shapes = [weights[n].shape for n in REPLICATED] + [(L, N_CHIPS) + conv_w.shape[1:]]
    small_vec = jnp.concatenate([jnp.stack(small[n]).reshape(-1) for n in REPLICATED[:-1]]
                                + [g_final[0], jnp.stack(g_conv).reshape(-1)])
    n_small = small_vec.shape[0]
    small_vec = jnp.pad(small_vec, (0, -n_small % (2 * N_CHIPS * 16 * LANES))).astype(BF16)
    rs["g0:small"] = small_vec.reshape(1, N_CHIPS, 2, -1, LANES)
    for role in "ra":
        rs[role + "0:small"] = lax.empty((N_CHIPS,) + rs["g0:small"].shape[3:], BF16)
    last = ("small",)
    comm_now(rs, [swap_jobs(0, last)], "swap_grad_halves")
    pair_sums(0, last)
    comm_now(rs, [xchg_jobs(0, last)], "exchange_grad_chips")
    chip_sums(0, reduced_names, 0, L)
    chip_sums(0, ("small",), 0, 1)
    comm_now(rs, [join_jobs(0, reduced_names + ("small",))], "join_grad_halves")
    shard_grads = {n: rs["o:" + n].reshape((L, -1) + rs["o:" + n].shape[-1:]) for n in reduced_names}
    small_all = allgather_chips(rs["o:small"].reshape(-1, LANES), "allgather_small_grads").reshape(-1)[:n_small]
    *rep_list, conv_all = _split_flat(small_all, small_shapes)
    rep_grads = dict(zip(REPLICATED, rep_list))
    shard_grads["conv_w"] = lax.dynamic_index_in_dim(conv_all, me_chip, 1, keepdims=False)

    delta, new_m, new_v = {}, {}, {}
    for n in BIG:
        shp = weights[n].shape
        if n == "w_in":
            view, back = (lambda a: a.transpose(2, 0, 1)), (lambda a: a.transpose(1, 2, 0))
            g = shard_grads[n].transpose(1, 0, 2)
        else:
            view, back = (lambda a: a.reshape(-1, shp[-1])), (lambda a: a.reshape(shp))
            g = view(shard_grads[n])
        d, nm, nv = adamw(view(weights[n]), g, view(mom_m[n]), view(mom_v[n]), "adamw_" + n)
        delta[n], new_m[n], new_v[n], shard_grads[n] = back(d), back(nm), back(nv), back(g)

    def rows(d):
        vec = jnp.concatenate([d[n].reshape(-1) for n in REPLICATED])
        return jnp.pad(vec, (0, -vec.shape[0] % (8 * LANES))).reshape(-1, LANES)

    outs = adamw(rows(weights), rows(rep_grads), rows(mom_m), rows(mom_v), "adamw_replicated")
    for res, o in zip((delta, new_m, new_v), outs):
        res.update(zip(REPLICATED, _split_flat(o.reshape(-1), small_shapes[:len(REPLICATED)])))
    all_grads = {**shard_grads, **rep_grads}

    return (loss, grad_x, *[all_grads[n] for n in ORDER], *[delta[n] for n in ORDER],
            *[new_m[n] for n in ORDER], *[new_v[n] for n in ORDER])
```

```python
import numpy as np
import jax
import jax.numpy as jnp
from jax import lax
from jax.experimental import pallas as pl
from jax.experimental.pallas import tpu as pltpu

F32, BF16 = jnp.float32, jnp.bfloat16
SDS = jax.ShapeDtypeStruct
MESH = pl.DeviceIdType.MESH
AXES = ("x", "y", "c")
N_CHIPS = 4
N_LAYERS = 2
LANES = 128
VMEM_LIMIT = 48 * 1024 * 1024

HEADS, HEAD_DIM = 8, 64
HEAD_PAD = 128
BRANCH_W = 512
GROUP_W = 128
N_GROUPS = BRANCH_W // GROUP_W
POOL_WINDOWS = (2, 4, 8, 16)
F_PAD = 512
ATTN_BLOCK = 256
ATTN_FWD_BLOCK = 512
RMS_EPS = 1e-6
NEG_INF = -1e30
ADAM_LR, ADAM_B1, ADAM_B2, ADAM_EPS, ADAM_WD, ADAM_STEP = 0.001, 0.9, 0.999, 1e-08, 0.01, 10

NT = (((1,), (1,)), ((), ()))
TN = (((0,), (0,)), ((), ()))
_ANY = pl.BlockSpec(memory_space=pl.ANY)


def _tile(n, prefs):
    for p in prefs:
        if n % p == 0:
            return p
    raise ValueError(f"no tile of {prefs} divides {n}")


def _params(*sem):
    return pltpu.CompilerParams(dimension_semantics=sem, vmem_limit_bytes=VMEM_LIMIT)


def _sigmoid(z):
    return 0.5 * jnp.tanh(0.5 * z) + 0.5


def _split3(x):
    h1 = x.astype(BF16)
    r1 = x - h1.astype(F32)
    h2 = r1.astype(BF16)
    h3 = (r1 - h2.astype(F32)).astype(BF16)
    return h1, h2, h3


def _position():
    return lax.axis_index("x"), lax.axis_index("y"), lax.axis_index("c")


def _other_chips(x, y):
    return [(1 - x, y), (x, 1 - y), (1 - x, 1 - y)]


def _remote(src, dst, send_sem, recv_sem, device):
    return pltpu.make_async_remote_copy(src_ref=src, dst_ref=dst, send_sem=send_sem, recv_sem=recv_sem,
                                        device_id=device, device_id_type=MESH)


ROW_SHARDED = ("w_out", "w_down")
FETCHER = dict(w_in=0, w_out=0, w_proj_attn=0, w_proj_pool=0, w_gate_up=1, w_down=1, w_proj_conv=1, conv_w=1)


class Hosted:
    def __init__(self, pool, jobs):
        self.pool, self.jobs = pool, list(jobs)
        names = set()
        for job in self.jobs:
            names.update(job[1:3] if job[0] in ("swap", "xchg") else job[1:2])
        self.names = sorted(names)


def _hosted_plan(hosted, refs, send_sems, recv_sems):
    x, y, c = _position()
    me = 2 * x + y
    others = _other_chips(x, y)
    sibling = (x, y, 1 - c)
    plan = []
    for j, job in enumerate(hosted.jobs):
        kind = job[0]
        sems = lambda k, j=j: (send_sems.at[j, k], recv_sems.at[j, k])
        if kind in ("ici", "fwd"):
            _, name, layer = job
            ref = refs[name]
            win = (lambda chip, ref=ref, layer=layer: ref.at[layer, chip]) if name in ROW_SHARDED else (
                lambda chip, ref=ref, layer=layer: ref.at[chip, layer])
            mine = c == FETCHER[name]
            if kind == "ici":
                sends = [_remote(win(me), win(me), *sems(k), (px, py, c)) for k, (px, py) in enumerate(others)]
                arrivals = [_remote(win(2 * px + py), win(2 * px + py), *sems(k), (px, py, c))
                            for k, (px, py) in enumerate(others)]
                plan.append((mine, sends, arrivals, []))
            else:
                sends = [_remote(win(2 * px + py), win(2 * px + py), *sems(k), sibling)
                         for k, (px, py) in enumerate(others)]
                plan.append((mine, sends, [], sends))
        elif kind == "swap":
            _, src, dst, layer = job
            cp = _remote(refs[src].at[layer, :, 1 - c], refs[dst], *sems(0), sibling)
            plan.append((True, [cp], [cp], []))
        elif kind == "xchg":
            _, src, dst = job
            sends = [_remote(refs[src].at[2 * px + py], refs[dst].at[me], *sems(k), (px, py, c))
                     for k, (px, py) in enumerate(others)]
            arrivals = [_remote(refs[src].at[me], refs[dst].at[2 * px + py], *sems(k), (px, py, c))
                        for k, (px, py) in enumerate(others)]
            plan.append((True, sends, arrivals, []))
        else:
            _, name, layer = job
            ref = refs[name]
            cp = _remote(ref.at[layer, c], ref.at[layer, c], *sems(0), sibling)
            arrival = _remote(ref.at[layer, c], ref.at[layer, 1 - c], *sems(0), sibling)
            plan.append((True, [cp], [arrival], []))
    return plan


def _hosted_start(plan, now):
    for mine, sends, _, _ in plan:
        @pl.when(now & mine)
        def _(sends=sends):
            for cp in sends:
                cp.start()


def _hosted_finish(plan, now):
    for mine, sends, arrivals, sibling_arrivals in plan:
        @pl.when(now & mine)
        def _(sends=sends, arrivals=arrivals):
            for cp in arrivals:
                cp.wait_recv()
            for cp in sends:
                cp.wait_send()

        if sibling_arrivals:
            @pl.when(now & jnp.logical_not(mine))
            def _(sibling_arrivals=sibling_arrivals):
                for cp in sibling_arrivals:
                    cp.wait_recv()


def _pcall(body, hosted, *, name, grid, in_specs, out_specs, out_shape, semantics, scratch_shapes=(), aliases=None):
    aliases = dict(aliases or {})
    if hosted is None or not hosted.jobs:
        return pl.pallas_call(body, name=name, grid=grid, in_specs=in_specs, out_specs=out_specs,
                              out_shape=out_shape, scratch_shapes=list(scratch_shapes),
                              input_output_aliases=aliases, compiler_params=_params(*semantics))
    single = not isinstance(out_shape, (list, tuple))
    out_specs_l = [out_specs] if single else list(out_specs)
    out_shape_l = [out_shape] if single else list(out_shape)
    n_in, n_out, n_buf, n_job = len(in_specs), len(out_specs_l), len(hosted.names), len(hosted.jobs)

    def carrying(*refs):
        ins, outs = refs[:n_in], refs[n_in + n_buf:n_in + n_buf + n_out]
        bufs = refs[n_in + n_buf + n_out:n_in + 2 * n_buf + n_out]
        rest = refs[n_in + 2 * n_buf + n_out:]
        scratch, send_sems, recv_sems = rest[:-2], rest[-2], rest[-1]
        first, last = True, True
        for axis, size in enumerate(grid):
            first = first & (pl.program_id(axis) == 0)
            last = last & (pl.program_id(axis) == size - 1)
        plan = _hosted_plan(hosted, dict(zip(hosted.names, bufs)), send_sems, recv_sems)
        _hosted_start(plan, first)
        body(*ins, *outs, *scratch)
        _hosted_finish(plan, last)

    def run(*args):
        bufs = [hosted.pool[n] for n in hosted.names]
        sem = pltpu.SemaphoreType.DMA
        res = pl.pallas_call(
            carrying, name=name, grid=grid, in_specs=list(in_specs) + [_ANY] * n_buf,
            out_specs=out_specs_l + [_ANY] * n_buf,
            out_shape=out_shape_l + [SDS(b.shape, b.dtype) for b in bufs],
            scratch_shapes=list(scratch_shapes) + [sem((n_job, 3)), sem((n_job, 3))],
            input_output_aliases={**aliases, **{n_in + i: n_out + i for i in range(n_buf)}},
            compiler_params=pltpu.CompilerParams(dimension_semantics=semantics, vmem_limit_bytes=VMEM_LIMIT,
                                                 has_side_effects=True),
        )(*args, *bufs)
        hosted.pool.update(zip(hosted.names, res[n_out:]))
        return res[0] if single else res[:n_out]

    return run


def _dot(a, b):
    return jnp.dot(a, b, preferred_element_type=F32)


def _dot_nt(a, b):
    return lax.dot_general(a, b, NT, preferred_element_type=F32)


def _dot_tn(a, b):
    return lax.dot_general(a, b, TN, preferred_element_type=F32)


def norm_matmul(x, gain, w, layer, kind, name, hosted=None):
    T, D = x.shape
    if kind == "by_shard":
        tn = w.shape[3]
        N = N_CHIPS * tn
        w_spec = pl.BlockSpec((None, None, D, tn), lambda i, j: (j, layer, 0, 0))
        mm = _dot
    else:
        N = w.shape[0]
        tn = _tile(N, (1024, 512, 256, 128))
        w_spec = pl.BlockSpec((tn, D), lambda i, j: (j, 0))
        mm = _dot_nt
    tm = _tile(T, (2048, 1024, 512, 256, 128) if kind == "rows" else (1024, 512, 256, 128))

    def body(x_ref, g_ref, w_ref, y_ref, h_ref):
        @pl.when(pl.program_id(1) == 0)
        def _():
            xf = x_ref[...]
            r = lax.rsqrt(jnp.mean(xf * xf, axis=-1, keepdims=True) + RMS_EPS)
            h_ref[...] = ((xf * r) * g_ref[...]).astype(BF16)

        y_ref[...] = mm(h_ref[...], w_ref[...]).astype(BF16)

    return _pcall(
        body, hosted, name=name, grid=(T // tm, N // tn),
        in_specs=[pl.BlockSpec((tm, D), lambda i, j: (i, 0)),
                  pl.BlockSpec((None, 1, D), lambda i, j: (layer, 0, 0)),
                  w_spec],
        out_specs=[pl.BlockSpec((tm, tn), lambda i, j: (i, j)),
                   pl.BlockSpec((tm, D), lambda i, j: (i, 0))],
        out_shape=[SDS((T, N), BF16), SDS((T, D), BF16)],
        semantics=("arbitrary", "arbitrary"),
    )(x, gain, w)


def matmul_nt_normbwd(dys, w, layer, kind, x, gain, dres, name, hosted=None):
    T, D = x.shape
    width = dys[0].shape[1]
    if kind == "by_shard":
        tk = w.shape[3]
        w_spec = pl.BlockSpec((None, None, D, tk), lambda i, k: (k, layer, 0, 0))
        mm = _dot_nt
    else:
        tk = _tile(width, (3584, 1024, 512, 256, 128))
        w_spec = pl.BlockSpec((tk, D), lambda i, k: (k, 0))
        mm = _dot
    per = width // tk
    nk = per * len(dys)
    tm = _tile(T, (512, 256, 128))
    n_dy = len(dys)

    def dy_spec(p):
        return pl.BlockSpec((tm, tk), lambda i, k: (i, jnp.clip(k - p * per, 0, per - 1)))

    def body(*refs):
        dy_refs = refs[:n_dy]
        w_ref, x_ref, g_ref, dres_ref, dx_ref, dxb_ref, dg_ref, acc_ref = refs[n_dy:]
        i, k = pl.program_id(0), pl.program_id(1)

        @pl.when(k == 0)
        def _():
            acc_ref[...] = jnp.zeros_like(acc_ref)

        for p in range(n_dy):
            @pl.when((k >= p * per) & (k < (p + 1) * per))
            def _(p=p):
                acc_ref[...] += mm(dy_refs[p][...], w_ref[...])

        @pl.when(k == nk - 1)
        def _():
            xf = x_ref[...]
            r = lax.rsqrt(jnp.mean(xf * xf, axis=-1, keepdims=True) + RMS_EPS)
            xhat = xf * r
            dh = acc_ref[...]
            dhg = dh * g_ref[...]
            dx = dres_ref[...] + r * (dhg - xhat * jnp.mean(dhg * xhat, axis=-1, keepdims=True))
            dx_ref[...] = dx
            dxb_ref[...] = dx.astype(BF16)
            part = jnp.sum(dh * xhat, axis=0, keepdims=True)

            @pl.when(i == 0)
            def _():
                dg_ref[...] = part

            @pl.when(i > 0)
            def _():
                dg_ref[...] += part

    row = pl.BlockSpec((tm, D), lambda i, k: (i, 0))
    return _pcall(
        body, hosted, name=name, grid=(T // tm, nk),
        in_specs=[dy_spec(p) for p in range(n_dy)] + [
            w_spec, row, pl.BlockSpec((None, 1, D), lambda i, k: (layer, 0, 0)), row],
        out_specs=[row, row, pl.BlockSpec((1, D), lambda i, k: (0, 0))],
        out_shape=[SDS((T, D), F32), SDS((T, D), BF16), SDS((1, D), F32)],
        scratch_shapes=[pltpu.VMEM((tm, D), F32)],
        semantics=("arbitrary", "arbitrary"),
    )(*dys, w, x, gain, dres)


def matmul_tn(a, bs, name, b_col0=0, n_cols=None, by_dest=False, tn=None, tk=None, hosted=None):
    T, M = a.shape
    width = bs[0].shape[1]
    N = n_cols if n_cols else width * len(bs)
    tm = _tile(M, (1408, 1024, 512, 256, 128))
    tn = tn or _tile(N, (512, 256, 128))
    tk = tk or _tile(T, (4096, 2048, 1024, 512, 256))
    assert b_col0 % tn == 0 and width % tn == 0
    j0, per, nk, n_b = b_col0 // tn, width // tn, T // tk, len(bs)

    def b_spec(p):
        return pl.BlockSpec((tk, tn), lambda i, j, k: (k, jnp.clip(j0 + j - p * per, 0, per - 1)))

    def body(*refs):
        a_ref, b_refs = refs[0], refs[1:1 + n_b]
        o_ref, acc_ref = refs[-2], refs[-1]
        j, k = pl.program_id(1), pl.program_id(2)

        @pl.when(k == 0)
        def _():
            acc_ref[...] = jnp.zeros_like(acc_ref)

        for p in range(n_b):
            @pl.when((j0 + j >= p * per) & (j0 + j < (p + 1) * per))
            def _(p=p):
                acc_ref[...] += _dot_tn(a_ref[...], b_refs[p][...])

        @pl.when(k == nk - 1)
        def _():
            o_ref[...] = acc_ref[...].astype(BF16)

    if by_dest:
        cs = N // N_CHIPS
        npd = cs // tn
        out_shape = SDS((N_CHIPS, M, cs), BF16)
        out_spec = pl.BlockSpec((None, tm, tn), lambda i, j, k: (j // npd, i, j % npd))
    else:
        out_shape = SDS((M, N), BF16)
        out_spec = pl.BlockSpec((tm, tn), lambda i, j, k: (i, j))
    return _pcall(
        body, hosted, name=name, grid=(M // tm, N // tn, nk),
        in_specs=[pl.BlockSpec((tk, tm), lambda i, j, k: (k, i))] + [b_spec(p) for p in range(n_b)],
        out_specs=out_spec, out_shape=out_shape,
        scratch_shapes=[pltpu.VMEM((tm, tn), F32)],
        semantics=("arbitrary", "arbitrary", "arbitrary"),
    )(a, *bs)


def ffn_down_fwd(ab, w_down, layer, x1, hosted=None):
    T, D = x1.shape
    F = w_down.shape[1]
    tm = _tile(T, (512, 256, 128))
    tk = F // 2
    nk = F // tk

    def body(a_ref, b_ref, w_ref, x_ref, x2_ref, s_ref, acc_ref):
        k = pl.program_id(1)

        @pl.when(k == 0)
        def _():
            acc_ref[...] = x_ref[...]

        a = a_ref[...].astype(F32)
        s = (a * _sigmoid(a) * b_ref[...].astype(F32)).astype(BF16)
        s_ref[...] = s
        acc_ref[...] += _dot(s, w_ref[...])

        @pl.when(k == nk - 1)
        def _():
            x2_ref[...] = acc_ref[...]

    return _pcall(
        body, hosted, name="ffn_down_fwd", grid=(T // tm, nk),
        in_specs=[pl.BlockSpec((tm, tk), lambda i, k: (i, k)),
                  pl.BlockSpec((tm, tk), lambda i, k: (i, nk + k)),
                  pl.BlockSpec((None, tk, D), lambda i, k: (layer, k, 0)),
                  pl.BlockSpec((tm, D), lambda i, k: (i, 0))],
        out_specs=[pl.BlockSpec((tm, D), lambda i, k: (i, 0)),
                   pl.BlockSpec((tm, tk), lambda i, k: (i, k))],
        out_shape=[SDS((T, D), F32), SDS((T, F), BF16)],
        scratch_shapes=[pltpu.VMEM((tm, D), F32)],
        semantics=("arbitrary", "arbitrary"),
    )(ab, ab, w_down, x1)


def ffn_down_bwd(dx2b, w_down, layer, ab, hosted=None):
    T, D = dx2b.shape
    F = w_down.shape[1]
    tm = _tile(T, (512, 256, 128))
    tn = F // 2
    nj = F // tn

    def body(dx_ref, w_ref, a_ref, b_ref, da_ref, db_ref):
        ds = _dot_nt(dx_ref[...], w_ref[...])
        a = a_ref[...].astype(F32)
        sg = _sigmoid(a)
        da_ref[...] = (ds * b_ref[...].astype(F32) * (sg * (1.0 + a * (1.0 - sg)))).astype(BF16)
        db_ref[...] = (ds * (a * sg)).astype(BF16)

    blk = pl.BlockSpec((tm, tn), lambda j, i: (i, j))
    return _pcall(
        body, hosted, name="ffn_down_bwd", grid=(nj, T // tm),
        in_specs=[pl.BlockSpec((tm, D), lambda j, i: (i, 0)),
                  pl.BlockSpec((None, tn, D), lambda j, i: (layer, j, 0)),
                  blk, pl.BlockSpec((tm, tn), lambda j, i: (i, nj + j))],
        out_specs=[blk, blk],
        out_shape=[SDS((T, F), BF16), SDS((T, F), BF16)],
        semantics=("arbitrary", "arbitrary"),
    )(dx2b, w_down, ab, ab)


def _mix_specs(tm, D, layer):
    cs = D // N_CHIPS
    row = lambda w: pl.BlockSpec((tm, w), lambda i: (i, 0))
    wp = pl.BlockSpec((N_CHIPS, None, BRANCH_W, cs), lambda i: (0, layer, 0, 0))
    wo = pl.BlockSpec((None, N_CHIPS, cs, D), lambda i: (layer, 0, 0, 0))
    bg = pl.BlockSpec((None, 1, 3 * D), lambda i: (layer, 0, 0))
    return row, wp, wo, bg


def mix_fwd(ao, po, co, proj, b_gate, wpa, wpp, wpc, w_out, layer, x, hosted=None):
    T, D = x.shape
    cs = D // N_CHIPS
    tm = _tile(T, (256, 128))
    row, wp, wo, bg = _mix_specs(tm, D, layer)

    def body(ao_ref, po_ref, co_ref, g_ref, bg_ref, wpa_ref, wpp_ref, wpc_ref, wo_ref, x_ref,
             x1_ref, ys_ref, mixed_ref):
        mixed = jnp.zeros((tm, D), F32)
        for n, (br, wp_ref) in enumerate(((ao_ref, wpa_ref), (po_ref, wpp_ref), (co_ref, wpc_ref))):
            y = jnp.concatenate([_dot(br[...], wp_ref[j]) for j in range(N_CHIPS)], axis=1)
            cols = slice(n * D, (n + 1) * D)
            gate = _sigmoid(g_ref[:, cols].astype(F32) + bg_ref[:, cols])
            ys_ref[:, cols] = y.astype(BF16)
            mixed = mixed + gate * y
        mb = mixed.astype(BF16)
        mixed_ref[...] = mb
        acc = x_ref[...]
        for j in range(N_CHIPS):
            acc = acc + _dot(mb[:, j * cs:(j + 1) * cs], wo_ref[j])
        x1_ref[...] = acc

    return _pcall(
        body, hosted, name="mix_fwd", grid=(T // tm,),
        in_specs=[row(BRANCH_W), row(BRANCH_W), row(BRANCH_W), row(3 * D), bg, wp, wp, wp, wo, row(D)],
        out_specs=[row(D), row(3 * D), row(D)],
        out_shape=[SDS((T, D), F32), SDS((T, 3 * D), BF16), SDS((T, D), BF16)],
        semantics=("arbitrary",),
    )(ao, po, co, proj, b_gate, wpa, wpp, wpc, w_out, x)


def mix_bwd(dx1b, w_out, proj, b_gate, ys, wpa, wpp, wpc, layer, width, hosted=None):
    T, D = dx1b.shape
    cs = D // N_CHIPS
    tm = _tile(T, (256, 128))
    row, wp, wo, bg = _mix_specs(tm, D, layer)

    def body(dx_ref, wo_ref, g_ref, bg_ref, ys_ref, wpa_ref, wpp_ref, wpc_ref,
             dys_ref, dg_ref, dao_ref, dpo_ref, dco_ref, dbg_ref):
        i = pl.program_id(0)
        dx = dx_ref[...]
        dmixed = jnp.concatenate([_dot_nt(dx, wo_ref[j]) for j in range(N_CHIPS)], axis=1)
        for n, (wp_ref, dbr) in enumerate(((wpa_ref, dao_ref), (wpp_ref, dpo_ref), (wpc_ref, dco_ref))):
            cols = slice(n * D, (n + 1) * D)
            gate = _sigmoid(g_ref[:, cols].astype(F32) + bg_ref[:, cols])
            dy = (dmixed * gate).astype(BF16)
            dys_ref[:, cols] = dy
            dgp = dmixed * ys_ref[:, cols].astype(F32) * gate * (1.0 - gate)
            dg_ref[:, cols] = dgp.astype(BF16)
            part = jnp.sum(dgp, axis=0, keepdims=True)

            @pl.when(i == 0)
            def _():
                dbg_ref[:, cols] = part

            @pl.when(i > 0)
            def _():
                dbg_ref[:, cols] += part

            acc = jnp.zeros((tm, BRANCH_W), F32)
            for j in range(N_CHIPS):
                acc = acc + _dot_nt(dy[:, j * cs:(j + 1) * cs], wp_ref[j])
            dbr[...] = acc.astype(BF16)

    return _pcall(
        body, hosted, name="mix_bwd", grid=(T // tm,),
        in_specs=[row(D), wo, row(3 * D), bg, row(3 * D), wp, wp, wp],
        out_specs=[row(3 * D), row(3 * D), row(BRANCH_W), row(BRANCH_W), row(BRANCH_W),
                   pl.BlockSpec((1, 3 * D), lambda i: (0, 0))],
        out_shape=[SDS((T, 3 * D), BF16), SDS((T, width), BF16), SDS((T, BRANCH_W), BF16),
                   SDS((T, BRANCH_W), BF16), SDS((T, BRANCH_W), BF16), SDS((1, 3 * D), F32)],
        semantics=("arbitrary",),
    )(dx1b, w_out, proj, b_gate, ys, wpa, wpp, wpc)


def loss_head(x2, gain, target):
    T, D = x2.shape
    tm = _tile(T, (512, 256, 128))

    def body(x_ref, g_ref, t_ref, loss_ref, dx_ref, dxb_ref, dg_ref):
        i = pl.program_id(0)
        xf = x_ref[...]
        g = g_ref[...]
        r = lax.rsqrt(jnp.mean(xf * xf, axis=-1, keepdims=True) + RMS_EPS)
        xhat = xf * r
        diff = xhat * g - t_ref[...]
        part_loss = 0.5 * jnp.sum(jnp.mean(diff * diff, axis=-1, keepdims=True), axis=0, keepdims=True)
        dy = diff * (1.0 / D)
        dhg = dy * g
        dx = r * (dhg - xhat * jnp.mean(dhg * xhat, axis=-1, keepdims=True))
        dx_ref[...] = dx
        dxb_ref[...] = dx.astype(BF16)
        part_g = jnp.sum(dy * xhat, axis=0, keepdims=True)
        part_l = jnp.broadcast_to(part_loss, (1, LANES))

        @pl.when(i == 0)
        def _():
            dg_ref[...] = part_g
            loss_ref[...] = part_l

        @pl.when(i > 0)
        def _():
            dg_ref[...] += part_g
            loss_ref[...] += part_l

    row = pl.BlockSpec((tm, D), lambda i: (i, 0))
    return pl.pallas_call(
        body, name="loss_head", grid=(T // tm,),
        in_specs=[row, pl.BlockSpec((1, D), lambda i: (0, 0)), row],
        out_specs=[pl.BlockSpec((1, LANES), lambda i: (0, 0)), row, row, pl.BlockSpec((1, D), lambda i: (0, 0))],
        out_shape=[SDS((1, LANES), F32), SDS((T, D), F32), SDS((T, D), BF16), SDS((1, D), F32)],
        compiler_params=_params("arbitrary"),
    )(x2, gain, target)


def _placement_constants():
    w = HEADS * HEAD_PAD
    pq = np.zeros((BRANCH_W, w), np.float32)
    pk = np.zeros((BRANCH_W, w), np.float32)
    pfq = np.zeros((3, LANES, w), np.float32)
    pfk = np.zeros((3, LANES, w), np.float32)
    cq = np.zeros((1, w), np.float32)
    ck = np.zeros((1, w), np.float32)
    eq = np.zeros((w, LANES), np.float32)
    ek = np.zeros((w, LANES), np.float32)
    for h in range(HEADS):
        for d in range(HEAD_DIM):
            pq[h * HEAD_DIM + d, h * HEAD_PAD + d] = HEAD_DIM ** -0.5
            pk[h * HEAD_DIM + d, h * HEAD_PAD + d] = 1.0
        for i in range(3):
            pfq[i, h, h * HEAD_PAD + HEAD_DIM + i] = 1.0
            pfk[i, h, h * HEAD_PAD + HEAD_DIM + 3 + i] = -1.0
            cq[0, h * HEAD_PAD + HEAD_DIM + 3 + i] = 1.0
            ck[0, h * HEAD_PAD + HEAD_DIM + i] = 1.0
        eq[h * HEAD_PAD + HEAD_DIM, h] = 1.0
        ek[h * HEAD_PAD + HEAD_DIM + 3, h] = -1.0
    bf = lambda a: jnp.asarray(a, BF16)
    return dict(pq=bf(pq), pk=bf(pk), pfq=bf(pfq), pfk=bf(pfk), cq=jnp.asarray(cq), ck=jnp.asarray(ck),
                pqkt=bf(np.concatenate([pq.T, pk.T], axis=0)), eq=bf(eq), ek=bf(ek))


def attn_prep(proj3, bf_rows, layer, cst, lay, hosted=None):
    Bl, S, _ = proj3.shape
    ts = ATTN_BLOCK
    w = HEADS * HEAD_PAD

    def body(q_ref, k_ref, f_ref, bf_ref, pq_ref, pk_ref, pfq_ref, pfk_ref, cq_ref, ck_ref,
             qa_ref, ka_ref, carry_ref):
        @pl.when(pl.program_id(1) == 0)
        def _():
            carry_ref[...] = jnp.zeros_like(carry_ref)

        z = f_ref[...].astype(F32) + bf_ref[...]
        logf = jnp.minimum(z, 0.0) - jnp.log(1.0 + jnp.exp(-jnp.abs(z)))
        r = lax.broadcasted_iota(jnp.int32, (ts, ts), 0)
        c = lax.broadcasted_iota(jnp.int32, (ts, ts), 1)
        tri = jnp.where(r >= c, 1.0, 0.0).astype(BF16)
        fcum = carry_ref[...]
        for part in _split3(logf):
            fcum = fcum + _dot(tri, part)
        carry_ref[...] = fcum[ts - 1:ts, :]
        qa = _dot(q_ref[...], pq_ref[...]) + cq_ref[...]
        ka = _dot(k_ref[...], pk_ref[...]) + ck_ref[...]
        for i, part in enumerate(_split3(fcum)):
            qa = qa + _dot(part, pfq_ref[i])
            ka = ka + _dot(part, pfk_ref[i])
        qa_ref[...] = qa.astype(BF16)
        ka_ref[...] = ka.astype(BF16)

    cfull = lambda shape: pl.BlockSpec(shape, lambda b, s: (0,) * len(shape))
    return _pcall(
        body, hosted, name="attn_prep", grid=(Bl, S // ts),
        in_specs=[pl.BlockSpec((None, ts, BRANCH_W), lambda b, s: (b, s, lay["q"] // BRANCH_W)),
                  pl.BlockSpec((None, ts, BRANCH_W), lambda b, s: (b, s, lay["k"] // BRANCH_W)),
                  pl.BlockSpec((None, ts, LANES), lambda b, s: (b, s, lay["f"] // LANES)),
                  pl.BlockSpec((None, 1, LANES), lambda b, s: (layer, 0, 0)),
                  cfull((BRANCH_W, w)), cfull((BRANCH_W, w)),
                  cfull((3, LANES, w)), cfull((3, LANES, w)), cfull((1, w)), cfull((1, w))],
        out_specs=[pl.BlockSpec((None, ts, w), lambda b, s: (b, s, 0)),
                   pl.BlockSpec((None, ts, w), lambda b, s: (b, s, 0))],
        out_shape=[SDS((Bl, S, w), BF16), SDS((Bl, S, w), BF16)],
        scratch_shapes=[pltpu.VMEM((1, LANES), F32)],
        semantics=("arbitrary", "arbitrary"),
    )(proj3, proj3, proj3, bf_rows, cst["pq"], cst["pk"], cst["pfq"], cst["pfk"], cst["cq"], cst["ck"])


def attn_fwd(qa, ka, proj3, lay, hosted=None):
    Bl, S, _ = qa.shape
    tq = ATTN_FWD_BLOCK
    nq = S // tq
    pairs = HEADS // 2
    pw = 2 * HEAD_PAD
    vw = 2 * HEAD_DIM

    def body(qa_ref, ka_ref, v_ref, o_ref, lse_ref):
        row = lax.broadcasted_iota(jnp.int32, (tq, tq), 0)
        col = lax.broadcasted_iota(jnp.int32, (tq, tq), 1)
        causal = row <= col
        for i in range(nq):
            nk = (i + 1) * tq
            rows = slice(i * tq, nk)
            o_t = []
            for h in range(2):
                hs = slice(h * HEAD_PAD, (h + 1) * HEAD_PAD)
                st = _dot_nt(ka_ref[0:nk, hs], qa_ref[rows, hs])
                diag = jnp.where(causal, st[nk - tq:], NEG_INF)
                m = jnp.max(diag, axis=0, keepdims=True)
                if i:
                    m = jnp.maximum(m, jnp.max(st[:nk - tq], axis=0, keepdims=True))
                p_diag = jnp.exp(diag - m)
                l = jnp.sum(p_diag, axis=0, keepdims=True)
                if i:
                    p_top = jnp.exp(st[:nk - tq] - m)
                    l = l + jnp.sum(p_top, axis=0, keepdims=True)
                    p = jnp.concatenate([p_top.astype(BF16), p_diag.astype(BF16)], axis=0)
                else:
                    p = p_diag.astype(BF16)
                acc = _dot_tn(v_ref[0:nk, :], p)
                o_t.append(acc[h * HEAD_DIM:(h + 1) * HEAD_DIM, :] / l)
                lse_ref[h:h + 1, rows] = m + jnp.log(l)
            o_ref[rows, :] = jnp.concatenate(o_t, axis=0).T.astype(BF16)

    return _pcall(
        body, hosted, name="attn_fwd", grid=(Bl, pairs),
        in_specs=[pl.BlockSpec((None, S, pw), lambda b, p: (b, 0, p)),
                  pl.BlockSpec((None, S, pw), lambda b, p: (b, 0, p)),
                  pl.BlockSpec((None, S, vw), lambda b, p: (b, 0, lay["v"] // vw + p))],
        out_specs=[pl.BlockSpec((None, S, vw), lambda b, p: (b, 0, p)),
                   pl.BlockSpec((None, None, 2, S), lambda b, p: (b, p, 0, 0))],
        out_shape=[SDS((Bl, S, BRANCH_W), BF16), SDS((Bl, pairs, 2, S), F32)],
        semantics=("arbitrary", "arbitrary"),
    )(qa, ka, proj3)


def attn_bwd(qa, ka, proj3, dao, ao, lse, dproj3, lay, hosted=None):
    Bl, S, _ = qa.shape
    tk = ATTN_BLOCK
    nq = S // tk
    pairs = HEADS // 2
    pw = 2 * HEAD_PAD
    vw = 2 * HEAD_DIM

    def body(qa_ref, ka_ref, v_ref, do_ref, o_ref, lse_ref, _, dqa_ref, dka_ref, dv_ref):
        row = lax.broadcasted_iota(jnp.int32, (tk, tk), 0)
        col = lax.broadcasted_iota(jnp.int32, (tk, tk), 1)
        causal = row <= col
        lane8 = lax.broadcasted_iota(jnp.int32, (8, vw), 1)
        lane_s = lax.broadcasted_iota(jnp.int32, (S, vw), 1)
        lane_k = lax.broadcasted_iota(jnp.int32, (tk, vw), 1)
        doo = do_ref[...].astype(F32) * o_ref[...].astype(F32)
        hi = doo.astype(BF16)
        lo = (doo - hi.astype(F32)).astype(BF16)
        delta, v_head = [], []
        for h in range(2):
            sel = jnp.where((lane8 >= h * HEAD_DIM) & (lane8 < (h + 1) * HEAD_DIM), 1.0, 0.0).astype(BF16)
            delta.append((_dot_nt(sel, hi) + _dot_nt(sel, lo))[0:1, :])
            in_head = (lane_s >= h * HEAD_DIM) & (lane_s < (h + 1) * HEAD_DIM)
            v_head.append(jnp.where(in_head, v_ref[...], jnp.zeros_like(v_ref[...])))
        dqa_ref[...] = jnp.zeros_like(dqa_ref)
        for j in range(nq):
            q0 = j * tk
            krows = slice(q0, q0 + tk)
            do = do_ref[q0:, :]
            dvs = []
            for h in range(2):
                hs = slice(h * HEAD_PAD, (h + 1) * HEAD_PAD)
                k = ka_ref[krows, hs]
                q = qa_ref[q0:, hs]
                st = _dot_nt(k, q)
                p = jnp.exp(st - lse_ref[h:h + 1, q0:])
                p_diag = jnp.where(causal, p[:, :tk], 0.0)
                p = jnp.concatenate([p_diag, p[:, tk:]], axis=1) if j < nq - 1 else p_diag
                dvs.append(_dot(p.astype(BF16), do))
                dpt = _dot_nt(v_head[h][krows, :], do)
                ds = (p * (dpt - delta[h][:, q0:])).astype(BF16)
                dka_ref[krows, hs] = _dot(ds, q)
                dqa_ref[q0:, hs] += _dot_tn(ds, k)
            dv_ref[krows, :] = jnp.where(lane_k < HEAD_DIM, dvs[0], dvs[1]).astype(BF16)

    seq = lambda w, c0=0: pl.BlockSpec((None, S, w), lambda b, p: (b, 0, c0 + p))
    return _pcall(
        body, hosted, name="attn_bwd", grid=(Bl, pairs),
        in_specs=[seq(pw), seq(pw), seq(vw, lay["v"] // vw), seq(vw), seq(vw),
                  pl.BlockSpec((None, None, 2, S), lambda b, p: (b, p, 0, 0)), _ANY],
        out_specs=[seq(pw), seq(pw), seq(vw, lay["v"] // vw)],
        out_shape=[SDS((Bl, S, HEADS * HEAD_PAD), F32), SDS((Bl, S, HEADS * HEAD_PAD), F32),
                   SDS(dproj3.shape, BF16)],
        aliases={6: 2}, semantics=("arbitrary", "arbitrary"),
    )(qa, ka, proj3, dao, ao, lse, dproj3)


def attn_post(dqa, dka, proj3, bf_rows, layer, dproj3, cst, lay, hosted=None):
    Bl, S, w = dqa.shape
    ts = ATTN_BLOCK
    ns = S // ts
    qkf = 2 * BRANCH_W + F_PAD

    def body(dqa_ref, dka_ref, f_ref, bf_ref, pqkt_ref, eq_ref, ek_ref, _, dqkf_ref, dbf_ref, carry_ref):
        b, s = pl.program_id(0), pl.program_id(1)

        @pl.when(s == 0)
        def _():
            carry_ref[...] = jnp.zeros_like(carry_ref)

        dqa_v, dka_v = dqa_ref[...], dka_ref[...]
        qh = dqa_v.astype(BF16)
        kh = dka_v.astype(BF16)
        dqkf_ref[:, :BRANCH_W] = _dot(qh, pqkt_ref[:w, :]).astype(BF16)
        dqkf_ref[:, BRANCH_W:2 * BRANCH_W] = _dot(kh, pqkt_ref[w:, :]).astype(BF16)
        ql = (dqa_v - qh.astype(F32)).astype(BF16)
        kl = (dka_v - kh.astype(F32)).astype(BF16)
        d_f = (_dot(qh, eq_ref[...]) + _dot(ql, eq_ref[...])) + (_dot(kh, ek_ref[...]) + _dot(kl, ek_ref[...]))
        r = lax.broadcasted_iota(jnp.int32, (ts, ts), 0)
        c = lax.broadcasted_iota(jnp.int32, (ts, ts), 1)
        triu = jnp.where(c >= r, 1.0, 0.0).astype(BF16)
        rev = carry_ref[...]
        for part in _split3(d_f):
            rev = rev + _dot(triu, part)
        carry_ref[...] = rev[0:1, :]
        z = f_ref[...].astype(F32) + bf_ref[...]
        lane = lax.broadcasted_iota(jnp.int32, (ts, LANES), 1)
        dfl = jnp.where(lane < HEADS, rev / (1.0 + jnp.exp(z)), 0.0)
        dqkf_ref[:, 2 * BRANCH_W:] = jnp.concatenate(
            [dfl.astype(BF16), jnp.zeros((ts, F_PAD - LANES), BF16)], axis=1)
        part = jnp.sum(dfl, axis=0, keepdims=True)

        @pl.when((b == 0) & (s == 0))
        def _():
            dbf_ref[...] = part

        @pl.when((b > 0) | (s > 0))
        def _():
            dbf_ref[...] += part

    assert lay["q"] % qkf == 0
    cfull = lambda shape: pl.BlockSpec(shape, lambda b, s: (0,) * len(shape))
    rev_blk = lambda wd, c0=0: pl.BlockSpec((None, ts, wd), lambda b, s: (b, ns - 1 - s, c0))
    return _pcall(
        body, hosted, name="attn_post", grid=(Bl, ns),
        in_specs=[rev_blk(w), rev_blk(w), rev_blk(LANES, lay["f"] // LANES),
                  pl.BlockSpec((None, 1, LANES), lambda b, s: (layer, 0, 0)),
                  cfull((2 * w, BRANCH_W)), cfull((w, LANES)), cfull((w, LANES)), _ANY],
        out_specs=[rev_blk(qkf, lay["q"] // qkf), cfull((1, LANES))],
        out_shape=[SDS(dproj3.shape, BF16), SDS((1, LANES), F32)],
        scratch_shapes=[pltpu.VMEM((1, LANES), F32)],
        aliases={7: 0}, semantics=("arbitrary", "arbitrary"),
    )(dqa, dka, proj3, bf_rows, cst["pqkt"], cst["eq"], cst["ek"], dproj3)


def _shift_down(x, k, row):
    return jnp.where(row >= k, pltpu.roll(x, k, axis=0), 0.0)


def _shift_up(x, k, row):
    n = x.shape[0]
    return jnp.where(row < n - k, pltpu.roll(x, n - k, axis=0), 0.0)


def _window_sum(x, g, row, shift):
    s2 = x + shift(x, 1, row)
    s4 = s2 + shift(s2, 2, row)
    s8 = s4 + shift(s4, 4, row)
    s16 = s8 + shift(s8, 8, row)
    return jnp.where(g == 0, s2, jnp.where(g == 1, s4, jnp.where(g == 2, s8, s16)))


def _window_count(g, row):
    wnd = jnp.where(g == 0, 2, jnp.where(g == 1, 4, jnp.where(g == 2, 8, 16)))
    return jnp.minimum(row + 1, wnd).astype(F32)


def _group_columns(ref):
    return [ref[:, n * GROUP_W:(n + 1) * GROUP_W].astype(F32) for n in range(4)]


def poolconv_fwd(proj3, pool_w, pool_scale, conv_w, layer, lay, hosted=None):
    Bl, S, _ = proj3.shape

    def body(x_ref, pw_ref, ps_ref, cw_ref, po_ref, co_ref):
        g = pl.program_id(1)
        row = lax.broadcasted_iota(jnp.int32, (S, GROUP_W), 0)
        u, cv, cb, cc = _group_columns(x_ref)
        d = _window_sum(u, g, row, _shift_down) / _window_count(g, row) - u
        po_ref[...] = (_dot(d.astype(BF16), pw_ref[...]) * ps_ref[...]).astype(BF16)
        z = cc * cv
        y = cw_ref[0:1, :] * _shift_down(z, 2, row) + cw_ref[1:2, :] * _shift_down(z, 1, row) + cw_ref[2:3, :] * z
        co_ref[...] = (cb * y).astype(BF16)

    out = pl.BlockSpec((None, S, GROUP_W), lambda b, g: (b, 0, g))
    return _pcall(
        body, hosted, name="poolconv_fwd", grid=(Bl, N_GROUPS),
        in_specs=[pl.BlockSpec((None, S, BRANCH_W), lambda b, g: (b, 0, lay["pc"] // BRANCH_W + g)),
                  pl.BlockSpec((None, None, GROUP_W, GROUP_W), lambda b, g: (layer, g, 0, 0)),
                  pl.BlockSpec((None, 1, GROUP_W), lambda b, g: (layer, 0, g)),
                  pl.BlockSpec((None, None, 3, GROUP_W), lambda b, g: (g, layer, 0, 0))],
        out_specs=[out, out],
        out_shape=[SDS((Bl, S, BRANCH_W), BF16), SDS((Bl, S, BRANCH_W), BF16)],
        semantics=("arbitrary", "arbitrary"),
    )(proj3, pool_w, pool_scale, conv_w)


def poolconv_bwd(proj3, dpo, dco, pool_w, pool_scale, conv_w, layer, dproj3, lay, hosted=None):
    Bl, S, _ = proj3.shape

    def body(x_ref, dpo_ref, dco_ref, pw_ref, ps_ref, cw_ref, _, dx_ref, dpw_ref, dps_ref, dcw_ref):
        g, b = pl.program_id(0), pl.program_id(1)
        row = lax.broadcasted_iota(jnp.int32, (S, GROUP_W), 0)
        cnt = _window_count(g, row)
        u, cv, cb, cc = _group_columns(x_ref)
        d = (_window_sum(u, g, row, _shift_down) / cnt - u).astype(BF16)
        pw = pw_ref[...]
        ypre = _dot(d, pw)
        dpo_v = dpo_ref[...].astype(F32)
        dps = jnp.sum(dpo_v * ypre, axis=0, keepdims=True)
        dyp = (dpo_v * ps_ref[...]).astype(BF16)
        dpw = _dot_tn(d, dyp)
        dd = _dot_nt(dyp, pw)
        dx_ref[:, 0:GROUP_W] = (_window_sum(dd / cnt, g, row, _shift_up) - dd).astype(BF16)

        z = cc * cv
        z1, z2 = _shift_down(z, 1, row), _shift_down(z, 2, row)
        w0, w1, w2 = cw_ref[0:1, :], cw_ref[1:2, :], cw_ref[2:3, :]
        y = w0 * z2 + w1 * z1 + w2 * z
        dco_v = dco_ref[...].astype(F32)
        dy = dco_v * cb
        dz = w0 * _shift_up(dy, 2, row) + w1 * _shift_up(dy, 1, row) + w2 * dy
        dx_ref[:, GROUP_W:2 * GROUP_W] = (dz * cc).astype(BF16)
        dx_ref[:, 2 * GROUP_W:3 * GROUP_W] = (dco_v * y).astype(BF16)
        dx_ref[:, 3 * GROUP_W:] = (dz * cv).astype(BF16)
        dcw = jnp.concatenate([jnp.sum(dy * z2, axis=0, keepdims=True),
                               jnp.sum(dy * z1, axis=0, keepdims=True),
                               jnp.sum(dy * z, axis=0, keepdims=True)], axis=0)

        @pl.when(b == 0)
        def _():
            dpw_ref[...] = dpw
            dps_ref[...] = dps
            dcw_ref[...] = dcw

        @pl.when(b > 0)
        def _():
            dpw_ref[...] += dpw
            dps_ref[...] += dps
            dcw_ref[...] += dcw

    blk = pl.BlockSpec((None, S, GROUP_W), lambda g, b: (b, 0, g))
    pc = pl.BlockSpec((None, S, BRANCH_W), lambda g, b: (b, 0, lay["pc"] // BRANCH_W + g))
    return _pcall(
        body, hosted, name="poolconv_bwd", grid=(N_GROUPS, Bl),
        in_specs=[pc, blk, blk,
                  pl.BlockSpec((None, None, GROUP_W, GROUP_W), lambda g, b: (layer, g, 0, 0)),
                  pl.BlockSpec((None, 1, GROUP_W), lambda g, b: (layer, 0, g)),
                  pl.BlockSpec((None, None, 3, GROUP_W), lambda g, b: (g, layer, 0, 0)), _ANY],
        out_specs=[pc, pl.BlockSpec((None, GROUP_W, GROUP_W), lambda g, b: (g, 0, 0)),
                   pl.BlockSpec((1, GROUP_W), lambda g, b: (0, g)),
                   pl.BlockSpec((None, 3, GROUP_W), lambda g, b: (g, 0, 0))],
        out_shape=[SDS(dproj3.shape, BF16), SDS((N_GROUPS, GROUP_W, GROUP_W), F32), SDS((1, BRANCH_W), F32),
                   SDS((N_GROUPS, 3, GROUP_W), F32)],
        aliases={6: 0}, semantics=("arbitrary", "arbitrary"),
    )(proj3, dpo, dco, pool_w, pool_scale, conv_w, dproj3)


def _tile_2d(rows, cols, n_arrays):
    budget = VMEM_LIMIT // 2
    lanes = -(-cols // LANES) * LANES
    if rows % 8 == 0:
        for t in range(min(rows, 2048), 7, -8):
            if rows % t == 0 and 2 * n_arrays * t * lanes * 4 <= budget:
                return t, cols
    for t in (1024, 512, 256, 128):
        if cols % t == 0 and 2 * n_arrays * (rows + 8) * t * 4 <= budget:
            return rows, t
    return rows, cols


def add_pair(kept, layer, where, received, name):
    _, n, _, R, C = kept.shape
    tr, tc = _tile_2d(R, C, 3)

    def body(where_ref, a_ref, b_ref, o_ref):
        o_ref[...] = (a_ref[...].astype(F32) + b_ref[...].astype(F32)).astype(BF16)

    blk = pl.BlockSpec((None, tr, tc), lambda d, i, j, where_ref: (d, i, j))
    grid_spec = pltpu.PrefetchScalarGridSpec(
        num_scalar_prefetch=1, grid=(n, R // tr, C // tc),
        in_specs=[pl.BlockSpec((None, None, None, tr, tc),
                               lambda d, i, j, where_ref: (layer, d, where_ref[0], i, j)), blk],
        out_specs=blk)
    return pl.pallas_call(body, name=name, grid_spec=grid_spec, out_shape=SDS((n, R, C), BF16),
                          compiler_params=_params("arbitrary", "arbitrary", "arbitrary"))(where, kept, received)


def add_chips(arrived, own, layer, where, n_layers, prev, name):
    _, R, C = arrived.shape
    tr, tc = _tile_2d(R, C, 6)

    def body(where_ref, a0, a1, a2, a3, own_ref, *rest):
        o_ref = rest[-1]
        chip = where_ref[1]
        acc = None
        for j, a_ref in enumerate((a0, a1, a2, a3)):
            term = jnp.where(chip == j, own_ref[...], a_ref[...]).astype(F32)
            acc = term if acc is None else acc + term
        o_ref[...] = acc

    def slot(j):
        return pl.BlockSpec((None, tr, tc), lambda i, k, where_ref, j=j: (
            jnp.where(where_ref[1] == j, (j + 1) % N_CHIPS, j), i, k))

    in_specs = [slot(j) for j in range(N_CHIPS)] + [
        pl.BlockSpec((None, tr, tc), lambda i, k, where_ref: (where_ref[1], i, k))]
    args = [where, arrived, arrived, arrived, arrived, own]
    aliases = {}
    if prev is not None:
        in_specs.append(_ANY)
        args.append(prev)
        aliases = {len(args) - 1: 0}
    grid_spec = pltpu.PrefetchScalarGridSpec(
        num_scalar_prefetch=1, grid=(R // tr, C // tc), in_specs=in_specs,
        out_specs=pl.BlockSpec((None, None, tr, tc), lambda i, k, where_ref: (layer, where_ref[0], i, k)))
    return pl.pallas_call(body, name=name, grid_spec=grid_spec, out_shape=SDS((n_layers, 2, R, C), F32),
                          input_output_aliases=aliases,
                          compiler_params=_params("arbitrary", "arbitrary"))(*args)


def adamw(w, g, m, v, name):
    if w.ndim == 2:
        R, C = w.shape
        tr, _ = _tile_2d(R, C, 7)
        grid, blk = (R // tr,), pl.BlockSpec((tr, C), lambda i: (i, 0))
    else:
        N, r, C = w.shape
        tn = max(t for t in range(1, N + 1) if N % t == 0 and t * r * C * 4 <= 1024 * 1024)
        grid, blk = (N // tn,), pl.BlockSpec((tn, r, C), lambda i: (i, 0, 0))

    def body(w_ref, g_ref, m_ref, v_ref, d_ref, nm_ref, nv_ref):
        gv = g_ref[...]
        m_new = ADAM_B1 * m_ref[...] + (1.0 - ADAM_B1) * gv
        v_new = ADAM_B2 * v_ref[...] + (1.0 - ADAM_B2) * (gv * gv)
        m_hat = m_new / (1.0 - ADAM_B1 ** ADAM_STEP)
        v_hat = v_new / (1.0 - ADAM_B2 ** ADAM_STEP)
        d_ref[...] = -ADAM_LR * (m_hat / (jnp.sqrt(v_hat) + ADAM_EPS) + ADAM_WD * w_ref[...])
        nm_ref[...] = m_new
        nv_ref[...] = v_new

    out = SDS(w.shape, F32)
    return pl.pallas_call(body, name=name, grid=grid, in_specs=[blk] * 4, out_specs=[blk] * 3,
                          out_shape=[out, out, out], compiler_params=_params("arbitrary"))(w, g, m, v)


_COMM = pltpu.CompilerParams(has_side_effects=True)


def gather_buffers(shards):
    me_chip = 2 * lax.axis_index("x") + lax.axis_index("y")
    pool = {}
    for name, sh in shards.items():
        L, r, c = sh.shape
        if name in ROW_SHARDED:
            pool[name] = lax.dynamic_update_slice(lax.empty((L, N_CHIPS, r, c), sh.dtype), sh[:, None],
                                                  (0, me_chip, 0, 0))
        else:
            pool[name] = lax.dynamic_update_slice(lax.empty((N_CHIPS, L, r, c), sh.dtype), sh[None],
                                                  (me_chip, 0, 0, 0))
    return pool


def comm_now(pool, stages, name):
    stages = [Hosted(pool, jobs) for jobs in stages]
    names = sorted({m for st in stages for m in st.names})
    n = len(names)

    def body(*refs):
        bufs = dict(zip(names, refs[n:2 * n]))
        sems = refs[2 * n:]
        for i, st in enumerate(stages):
            plan = _hosted_plan(st, bufs, sems[2 * i], sems[2 * i + 1])
            _hosted_start(plan, True)
            _hosted_finish(plan, True)

    sem = pltpu.SemaphoreType.DMA
    scratch = []
    for st in stages:
        scratch += [sem((len(st.jobs), 3)), sem((len(st.jobs), 3))]
    res = pl.pallas_call(
        body, name=name, in_specs=[_ANY] * n, out_specs=[_ANY] * n,
        out_shape=[SDS(pool[m].shape, pool[m].dtype) for m in names],
        scratch_shapes=scratch, input_output_aliases={t: t for t in range(n)},
        compiler_params=_COMM,
    )(*[pool[m] for m in names])
    pool.update(zip(names, res))


def gather_now(pool, units):
    comm_now(pool, [[("ici", name, layer) for name, layer in units],
                    [("fwd", name, layer) for name, layer in units]], "gather_now")


def allgather_chips(buf, name):
    def body(src_ref, out_ref, send_sems, recv_sems, local_sem):
        x, y, c = _position()
        me = 2 * x + y
        mine = pltpu.make_async_copy(src_ref, out_ref.at[me], local_sem)
        mine.start()
        sends = []
        for k, (px, py) in enumerate(_other_chips(x, y)):
            cp = _remote(src_ref, out_ref.at[me], send_sems.at[k], recv_sems.at[k], (px, py, c))
            cp.start()
            sends.append(cp)
        for k, (px, py) in enumerate(_other_chips(x, y)):
            _remote(src_ref, out_ref.at[2 * px + py], send_sems.at[k], recv_sems.at[k], (px, py, c)).wait_recv()
        for cp in sends:
            cp.wait_send()
        mine.wait()

    sem = pltpu.SemaphoreType.DMA
    return pl.pallas_call(
        body, name=name, in_specs=[_ANY], out_specs=_ANY, out_shape=SDS((N_CHIPS,) + buf.shape, buf.dtype),
        scratch_shapes=[sem((3,)), sem((3,)), sem], compiler_params=_COMM,
    )(buf)


BIG = ("w_in", "w_proj_attn", "w_proj_pool", "w_proj_conv", "conv_w", "w_out", "w_gate_up", "w_down")
REPLICATED = ("attn_norm", "b_forget", "b_gate", "pool_w", "pool_scale", "ffn_norm", "final_norm")
ORDER = ("attn_norm", "w_in", "b_forget", "b_gate", "w_proj_attn", "pool_w", "pool_scale", "w_proj_pool",
         "conv_w", "w_proj_conv", "w_out", "ffn_norm", "w_gate_up", "w_down", "final_norm")


def _proj_layout(D):
    lay = {"g": 0, "q": 3 * D}
    lay["k"] = lay["q"] + BRANCH_W
    lay["f"] = lay["k"] + BRANCH_W
    lay["v"] = lay["f"] + F_PAD
    lay["pc"] = lay["v"] + BRANCH_W
    lay["width"] = lay["pc"] + 4 * BRANCH_W
    return lay


_REF = dict(q=0, k=512, v=1024, f=1536, u=1544, cv=2056, cb=2568, cc=3080, g=3592)


def _packed_pieces(D):
    pieces = [(_REF["g"], 3 * D), (_REF["q"], BRANCH_W), (_REF["k"], BRANCH_W), (_REF["f"], HEADS),
              (None, F_PAD - HEADS), (_REF["v"], BRANCH_W)]
    for gi in range(N_GROUPS):
        pieces += [(_REF[name] + gi * GROUP_W, GROUP_W) for name in ("u", "cv", "cb", "cc")]
    return pieces


def _packed_runs(D, cs):
    runs, at = [], 0
    for start, n in _packed_pieces(D):
        if start is None:
            runs.append((at, None, 0, n))
            at += n
        while start is not None and n:
            chip, off = divmod(start, cs)
            take = min(n, cs - off)
            runs.append((at, chip, off, take))
            at, start, n = at + take, start + take, n - take
    return runs


def pack_w_in(shards, layer):
    _, _, cs, D = shards.shape
    runs = _packed_runs(D, cs)
    width = runs[-1][0] + runs[-1][3]
    tc = _tile(D, (256, 128))

    def body(s_ref, o_ref):
        for dst, chip, off, rows in runs:
            if chip is None:
                o_ref[dst:dst + rows, :] = jnp.zeros((rows, tc), s_ref.dtype)
            else:
                o_ref[dst:dst + rows, :] = s_ref[chip, off:off + rows, :]

    return pl.pallas_call(
        body, name="pack_w_in", grid=(D // tc,),
        in_specs=[pl.BlockSpec((N_CHIPS, None, cs, tc), lambda j: (0, layer, 0, j))],
        out_specs=pl.BlockSpec((width, tc), lambda j: (0, j)),
        out_shape=SDS((width, D), shards.dtype), compiler_params=_params("arbitrary"),
    )(shards)


def unpack_w_in(p, cs):
    width, D = p.shape
    half = cs // 2
    runs = []
    for src, chip, off, rows in _packed_runs(D, cs):
        while chip is not None and rows:
            h, at = divmod(off, half)
            take = min(rows, half - at)
            runs.append((src, chip, h, at, take))
            src, off, rows = src + take, off + take, rows - take
    tc = _tile(D, (256, 128))

    def body(p_ref, o_ref):
        for src, chip, h, at, rows in runs:
            o_ref[chip, h, at:at + rows, :] = p_ref[src:src + rows, :]

    return pl.pallas_call(
        body, name="unpack_w_in", grid=(D // tc,),
        in_specs=[pl.BlockSpec((width, tc), lambda j: (0, j))],
        out_specs=pl.BlockSpec((N_CHIPS, 2, half, tc), lambda j: (0, 0, 0, j)),
        out_shape=SDS((N_CHIPS, 2, half, D), p.dtype), compiler_params=_params("arbitrary"),
    )(p)


def _split_flat(vec, shapes):
    out, at = [], 0
    for shp in shapes:
        n = int(np.prod(shp))
        out.append(vec[at:at + n].reshape(shp))
        at += n
    return out


def kernel(x, attn_norm, w_in, b_forget, b_gate, w_proj_attn, pool_w, pool_scale, w_proj_pool, conv_w, w_proj_conv, w_out, ffn_norm, w_gate_up, w_down, final_norm, loss_target, m_attn_norm, m_w_in, m_b_forget, m_b_gate, m_w_proj_attn, m_pool_w, m_pool_scale, m_w_proj_pool, m_conv_w, m_w_proj_conv, m_w_out, m_ffn_norm, m_w_gate_up, m_w_down, m_final_norm, v_attn_norm, v_w_in, v_b_forget, v_b_gate, v_w_proj_attn, v_pool_w, v_pool_scale, v_w_proj_pool, v_conv_w, v_w_proj_conv, v_w_out, v_ffn_norm, v_w_gate_up, v_w_down, v_final_norm):
    weights = dict(attn_norm=attn_norm, w_in=w_in, b_forget=b_forget, b_gate=b_gate, w_proj_attn=w_proj_attn,
                   pool_w=pool_w, pool_scale=pool_scale, w_proj_pool=w_proj_pool, conv_w=conv_w,
                   w_proj_conv=w_proj_conv, w_out=w_out, ffn_norm=ffn_norm, w_gate_up=w_gate_up, w_down=w_down,
                   final_norm=final_norm)
    mom_m = dict(attn_norm=m_attn_norm, w_in=m_w_in, b_forget=m_b_forget, b_gate=m_b_gate, w_proj_attn=m_w_proj_attn,
                 pool_w=m_pool_w, pool_scale=m_pool_scale, w_proj_pool=m_w_proj_pool, conv_w=m_conv_w,
                 w_proj_conv=m_w_proj_conv, w_out=m_w_out, ffn_norm=m_ffn_norm, w_gate_up=m_w_gate_up,
                 w_down=m_w_down, final_norm=m_final_norm)
    mom_v = dict(attn_norm=v_attn_norm, w_in=v_w_in, b_forget=v_b_forget, b_gate=v_b_gate, w_proj_attn=v_w_proj_attn,
                 pool_w=v_pool_w, pool_scale=v_pool_scale, w_proj_pool=v_w_proj_pool, conv_w=v_conv_w,
                 w_proj_conv=v_w_proj_conv, w_out=v_w_out, ffn_norm=v_ffn_norm, w_gate_up=v_w_gate_up,
                 w_down=v_w_down, final_norm=v_final_norm)

    Bl, S, D = x.shape
    T = Bl * S
    L = w_in.shape[0]
    F = w_down.shape[1] * N_CHIPS
    lay = _proj_layout(D)
    cst = _placement_constants()
    assert L == N_LAYERS and S % ATTN_FWD_BLOCK == 0 and S % ATTN_BLOCK == 0
    assert F % (2 * LANES) == 0 and D % BRANCH_W == 0
    assert w_in.shape[2] * N_CHIPS == _REF["g"] + 3 * D and conv_w.shape[2] == GROUP_W

    send = {n: weights[n].astype(BF16) for n in BIG}
    send["conv_w"] = conv_w
    me_chip = 2 * lax.axis_index("x") + lax.axis_index("y")
    send["w_in"] = w_in.transpose(0, 2, 1).astype(BF16)
    pool = gather_buffers(send)
    gather_now(pool, [("w_in", 0)])
    rest = ("w_out", "w_proj_attn", "w_proj_pool", "w_gate_up", "w_proj_conv", "conv_w")
    late = ("w_out", "w_proj_attn", "w_proj_pool", "w_proj_conv", "conv_w")
    jobs = lambda kind, names, layer: [(kind, n, layer) for n in names]
    carried = {
        ("in_proj", 0): jobs("ici", rest, 0),
        ("attn_prep", 0): jobs("fwd", late, 0),
        ("attn_fwd", 0): jobs("fwd", ("w_gate_up",), 0) + jobs("ici", ("w_in",), 1) + jobs("ici", ("w_down",), 0),
        ("poolconv_fwd", 0): jobs("fwd", ("w_in",), 1) + jobs("fwd", ("w_down",), 0),
        ("mix_fwd", 0): jobs("ici", ("w_down",), 1),
        ("gate_up_proj", 0): jobs("ici", late, 1) + jobs("fwd", ("w_down",), 1),
        ("ffn_down_fwd", 0): jobs("ici", ("w_gate_up",), 1),
        ("in_proj", 1): jobs("fwd", ("w_gate_up",) + late, 1),
    }
    carry = lambda call, layer: Hosted(pool, carried[call, layer]) if (call, layer) in carried else None
    w_down_f = lambda: pool["w_down"].reshape(L, F, D)
    pool_w_b = pool_w.astype(BF16)
    an3, fn3 = attn_norm.reshape(L, 1, D), ffn_norm.reshape(L, 1, D)
    bg3, ps3 = b_gate.reshape(L, 1, 3 * D), pool_scale.reshape(L, 1, BRANCH_W)
    bf3 = jnp.pad(b_forget, ((0, 0), (0, LANES - HEADS))).reshape(L, 1, LANES)

    xs = x.reshape(T, D)
    saved = []
    w_in_p = []
    for l in range(L):
        w_in_p.append(pack_w_in(pool["w_in"], l))
        proj, h = norm_matmul(xs, an3, w_in_p[l], l, "rows", "in_proj", carry("in_proj", l))
        proj3 = proj.reshape(Bl, S, lay["width"])
        qa, ka = attn_prep(proj3, bf3, l, cst, lay, carry("attn_prep", l))
        ao, lse = attn_fwd(qa, ka, proj3, lay, carry("attn_fwd", l))
        po, co = poolconv_fwd(proj3, pool_w_b, ps3, pool["conv_w"], l, lay, carry("poolconv_fwd", l))
        ao2, po2, co2 = (a.reshape(T, BRANCH_W) for a in (ao, po, co))
        x1, ys, mixed = mix_fwd(ao2, po2, co2, proj, bg3, pool["w_proj_attn"], pool["w_proj_pool"],
                                pool["w_proj_conv"], pool["w_out"], l, xs, carry("mix_fwd", l))
        ab, h2 = norm_matmul(x1, fn3, pool["w_gate_up"], l, "by_shard", "gate_up_proj", carry("gate_up_proj", l))
        x2, s_act = ffn_down_fwd(ab, w_down_f(), l, x1, carry("ffn_down_fwd", l))
        saved.append(dict(x=xs, proj=proj, proj3=proj3, h=h, qa=qa, ka=ka, ao=ao, lse=lse, ao2=ao2, po2=po2,
                          co2=co2, ys=ys, mixed=mixed, x1=x1, ab=ab, h2=h2, s=s_act))
        xs = x2
    w_gu, w_o, conv_w_g = pool["w_gate_up"], pool["w_out"], pool["conv_w"]
    wpa, wpp, wpc = pool["w_proj_attn"], pool["w_proj_pool"], pool["w_proj_conv"]
    w_down_f = w_down_f()

    loss_row, dx, dxb, g_final = loss_head(xs, final_norm.reshape(1, D), loss_target.reshape(T, D))
    loss = lax.psum(loss_row[0, 0], AXES)

    reduced_names = tuple(n for n in BIG if n != "conv_w")
    early_names = tuple(n for n in reduced_names if n != "w_in")
    proj_names = ("w_out", "w_proj_attn", "w_proj_pool", "w_proj_conv")
    first_names = ("w_in", "w_gate_up", "w_down")
    where = jnp.stack([lax.axis_index("c"), me_chip]).astype(jnp.int32)
    rs = {}

    def reduce_begin(layer, grads):
        for n, g in grads.items():
            g5 = g.reshape((1, N_CHIPS, 2, -1) + g.shape[-1:])
            rs["g%d:%s" % (layer, n)] = g5
            for role in "ra":
                rs["%s%d:%s" % (role, layer, n)] = lax.empty((N_CHIPS,) + g5.shape[3:], BF16)

    swap_jobs = lambda layer, names: [("swap", "g%d:%s" % (layer, n), "r%d:%s" % (layer, n), 0) for n in names]
    xchg_jobs = lambda layer, names: [("xchg", "s%d:%s" % (layer, n), "a%d:%s" % (layer, n)) for n in names]
    join_jobs = lambda layer, names: [("join", "o:" + n, layer) for n in names]

    def pair_sums(layer, names):
        for n in names:
            rs["s%d:%s" % (layer, n)] = add_pair(rs["g%d:%s" % (layer, n)], 0, where, rs["r%d:%s" % (layer, n)],
                                                 "add_pair_" + n)

    def chip_sums(layer, names, slot, n_slots):
        for n in names:
            rs["o:" + n] = add_chips(rs["a%d:%s" % (layer, n)], rs["s%d:%s" % (layer, n)], slot, where, n_slots,
                                     rs.get("o:" + n), "add_chips_" + n)

    small = {n: [None] * L for n in REPLICATED if n != "final_norm"}
    g_conv = [None] * L
    to3 = lambda a: a.reshape(Bl, S, -1)
    for l in reversed(range(L)):
        sv = saved[l]
        behind = (lambda jobs: Hosted(rs, jobs)) if l == 0 else (lambda jobs: None)
        grads = {}
        da, db = ffn_down_bwd(dxb, w_down_f, l, sv["ab"], behind(swap_jobs(1, reduced_names)))
        if l == 0:
            pair_sums(1, reduced_names)
        grads["w_down"] = matmul_tn(sv["s"], [dxb], "grad_w_down", hosted=behind(xchg_jobs(1, ("w_down",))))
        grads["w_gate_up"] = matmul_tn(sv["h2"], [da, db], "grad_w_gate_up", by_dest=True, tn=2 * F // N_CHIPS,
                                       tk=_tile(T, (1024, 512, 256)), hosted=behind(xchg_jobs(1, ("w_gate_up",))))
        dx1, dx1b, g_fn = matmul_nt_normbwd([da, db], w_gu, l, "by_shard", sv["x1"], fn3, dx, "gate_up_bwd",
                                            behind(xchg_jobs(1, ("w_in",))))
        small["ffn_norm"][l] = g_fn[0]
        if l == 0:
            chip_sums(1, first_names, 1, L)
        dys, dproj, dao, dpo, dco, g_bg = mix_bwd(dx1b, w_o, sv["proj"], bg3, sv["ys"], wpa, wpp, wpc, l,
                                                  lay["width"],
                                                  behind(xchg_jobs(1, proj_names) + join_jobs(1, first_names)))
        if l == 0:
            chip_sums(1, proj_names, 1, L)
        small["b_gate"][l] = g_bg[0]
        grads["w_out"] = matmul_tn(sv["mixed"], [dx1b], "grad_w_out")
        for n, (name, br) in enumerate((("w_proj_attn", sv["ao2"]), ("w_proj_pool", sv["po2"]),
                                        ("w_proj_conv", sv["co2"]))):
            grads[name] = matmul_tn(br, [dys], "grad_" + name, b_col0=n * D, n_cols=D, by_dest=True,
                                    tn=D // N_CHIPS)
        if l == 0:
            reduce_begin(0, grads)
        dqa, dka, dproj3 = attn_bwd(sv["qa"], sv["ka"], sv["proj3"], to3(dao), sv["ao"], sv["lse"], to3(dproj), lay,
                                    behind(swap_jobs(0, early_names) + join_jobs(1, proj_names)))
        if l == 0:
            pair_sums(0, early_names)
        dproj3, g_bf = attn_post(dqa, dka, sv["proj3"], bf3, l, dproj3, cst, lay, behind(xchg_jobs(
            0, ("w_out", "w_proj_attn", "w_proj_pool", "w_proj_conv"))))
        small["b_forget"][l] = g_bf[0, :HEADS]
        dproj3, g_pw, g_ps, g_conv[l] = poolconv_bwd(sv["proj3"], to3(dpo), to3(dco), pool_w_b, ps3, conv_w_g, l,
                                                     dproj3, lay, behind(xchg_jobs(0, ("w_down",))))
        small["pool_w"][l], small["pool_scale"][l] = g_pw, g_ps[0]
        dproj = dproj3.reshape(T, lay["width"])
        g_w_in = unpack_w_in(matmul_tn(dproj, [sv["h"]], "grad_w_in", tn=_tile(D, (1024, 512)), hosted=behind(xchg_jobs(
            0, ("w_gate_up",)))), w_in.shape[2])
        if l:
            reduce_begin(l, {**grads, "w_in": g_w_in})
        else:
            reduce_begin(0, {"w_in": g_w_in})
            comm_now(rs, [swap_jobs(0, ("w_in",))], "swap_w_in_halves")
            pair_sums(0, ("w_in",))
        dx, dxb, g_an = matmul_nt_normbwd([dproj], w_in_p[l], l, "rows", sv["x"], an3, dx1, "in_proj_bwd",
                                          behind(xchg_jobs(0, ("w_in",))))
        small["attn_norm"][l] = g_an[0]
    grad_x = dx.reshape(Bl, S, D)

    small_shapes = [weights[n].shape for n in REPLICATED] + [(L, N_CHIPS) + conv_w.shape[1:]]
    small_vec = jnp.concatenate([jnp.stack(small[n]).reshape(-1) for n in REPLICATED[:-1]]
                                + [g_final[0], jnp.stack(g_conv).reshape(-1)])
    n_small = small_vec.shape[0]
    small_vec = jnp.pad(small_vec, (0, -n_small % (2 * N_CHIPS * 16 * LANES))).astype(BF16)
    rs["g0:small"] = small_vec.reshape(1, N_CHIPS, 2, -1, LANES)
    for role in "ra":
        rs[role + "0:small"] = lax.empty((N_CHIPS,) + rs["g0:small"].shape[3:], BF16)
    last = ("small",)
    comm_now(rs, [swap_jobs(0, last)], "swap_grad_halves")
    pair_sums(0, last)
    comm_now(rs, [xchg_jobs(0, last)], "exchange_grad_chips")
    chip_sums(0, reduced_names, 0, L)
    chip_sums(0, ("small",), 0, 1)
    comm_now(rs, [join_jobs(0, reduced_names + ("small",))], "join_grad_halves")
    shard_grads = {n: rs["o:" + n].reshape((L, -1) + rs["o:" + n].shape[-1:]) for n in reduced_names}
    small_all = allgather_chips(rs["o:small"].reshape(-1, LANES), "allgather_small_grads").reshape(-1)[:n_small]
    *rep_list, conv_all = _split_flat(small_all, small_shapes)
    rep_grads = dict(zip(REPLICATED, rep_list))
    shard_grads["conv_w"] = lax.dynamic_index_in_dim(conv_all, me_chip, 1, keepdims=False)

    delta, new_m, new_v = {}, {}, {}
    for n in BIG:
        shp = weights[n].shape
        if n == "w_in":
            view, back = (lambda a: a.transpose(2, 0, 1)), (lambda a: a.transpose(1, 2, 0))
            g = shard_grads[n].transpose(1, 0, 2)
        else:
            view, back = (lambda a: a.reshape(-1, shp[-1])), (lambda a: a.reshape(shp))
            g = view(shard_grads[n])
        d, nm, nv = adamw(view(weights[n]), g, view(mom_m[n]), view(mom_v[n]), "adamw_" + n)
        delta[n], new_m[n], new_v[n], shard_grads[n] = back(d), back(nm), back(nv), back(g)

    def rows(d):
        vec = jnp.concatenate([d[n].reshape(-1) for n in REPLICATED])
        return jnp.pad(vec, (0, -vec.shape[0] % (8 * LANES))).reshape(-1, LANES)

    outs = adamw(rows(weights), rows(rep_grads), rows(mom_m), rows(mom_v), "adamw_replicated")
    for res, o in zip((delta, new_m, new_v), outs):
        res.update(zip(REPLICATED, _split_flat(o.reshape(-1), small_shapes[:len(REPLICATED)])))
    all_grads = {**shard_grads, **rep_grads}

    return (loss, grad_x, *[all_grads[n] for n in ORDER], *[delta[n] for n in ORDER],
            *[new_m[n] for n in ORDER], *[new_v[n] for n in ORDER])
```
